```python
import math
import jax
import jax.numpy as jnp
from jax import lax
import numpy as np

D_MODEL = 1024
BATCH = 16
SEQ = 256
DEPTH = 4
DEC_BATCH = 2
DEC_SEQ = 1024
PAST_LEN = 256

F32 = jnp.float32
GRID_W = 64
N_ADA = 9
D_FF = 2816
GROUP_W = D_MODEL // 4
MIX_W = 4 * GROUP_W
CHUNK = 64
Q_BLOCK = 128
RMS_EPS = 1e-6

HG_HEADS = 4
HG_DK = GROUP_W // HG_HEADS
HG_DV = GROUP_W // HG_HEADS
HY_W = GROUP_W
HY_EMB = 33
HY_BANDS = (HY_EMB - 1) // 2
HY_FH = 64
HY_TARGET = 1e-2
HY_FAST = 0.3
HY_SLOW = 1.5
MLA_HEADS = 4
MLA_NOPE = 64
MLA_ROPE = 32
MLA_V = GROUP_W // MLA_HEADS
MLA_Q_LORA = 256
MLA_KV_LORA = 128
ROPE_BASE = 10000.0
GD_HEADS = 4
GD_DK = 64
GD_DV = GROUP_W // GD_HEADS

HG_COLS = 5 * GROUP_W
HY_COLS = 3 * HY_W
MLA_COLS = MLA_Q_LORA + MLA_KV_LORA + MLA_ROPE
GD_COLS = 4 * GROUP_W + 4 * GD_HEADS
IN_COLS = HG_COLS + HY_COLS + MLA_COLS + GD_COLS

kernel_name = 'hybrid_diffusion_hgrn2_hyena_mla_gdn_step'


def rms_norm(x, g):
    xf = x.astype(F32)
    y = xf * lax.rsqrt(jnp.mean(xf * xf, axis=-1, keepdims=True) + RMS_EPS)
    return (y * g.astype(F32)).astype(x.dtype)


def l2_norm(x):
    return x * lax.rsqrt(jnp.sum(x * x, axis=-1, keepdims=True) + 1e-6)


def heads(x, n):
    B, T, _ = x.shape
    return x.reshape(B, T, n, -1).transpose(0, 2, 1, 3)


def merge_heads(x):
    B, n, T, d = x.shape
    return x.transpose(0, 2, 1, 3).reshape(B, T, n * d)


def head_norm_gate(o, g_norm, gate):
    o = rms_norm(o.transpose(0, 2, 1, 3), g_norm)
    B, T, H, d = o.shape
    return o.reshape(B, T, H * d) * jax.nn.silu(gate)


def conv3_centred(x, w):
    xp = jnp.pad(x, ((0, 0), (1, 1), (0, 0)))
    return xp[:, :-2] * w[0] + xp[:, 1:-1] * w[1] + xp[:, 2:] * w[2]


def swiglu(h, w_gu, w_down):
    gate, up = jnp.split(h @ w_gu, 2, axis=-1)
    return (jax.nn.silu(gate) * up) @ w_down


def adaln(cond, w, b):
    return (jax.nn.silu(cond) @ w + b).reshape(cond.shape[0], N_ADA, D_MODEL)


def rope_tables(rows):
    T = rows * GRID_W
    row = jnp.repeat(jnp.arange(rows, dtype=F32), GRID_W)
    col = (jnp.arange(T) % GRID_W).astype(F32)
    pairs = MLA_ROPE // 4
    inv = ROPE_BASE ** (-jnp.arange(pairs, dtype=F32) / pairs)
    ang = jnp.concatenate([row[:, None] * inv, col[:, None] * inv], axis=-1)
    return jnp.cos(ang), jnp.sin(ang)


def apply_rope(x, cos, sin):
    xn, xr = x[..., :MLA_NOPE], x[..., MLA_NOPE:]
    x1, x2 = jnp.split(xr, 2, axis=-1)
    return jnp.concatenate([xn, x1 * cos - x2 * sin, x2 * cos + x1 * sin], axis=-1)


def attention(q, k, v, scale):
    B, H, T, dq = q.shape
    nb = T // Q_BLOCK
    qb = q.reshape(B, H, nb, Q_BLOCK, dq).transpose(2, 0, 1, 3, 4)

    def block(qi):
        s = jnp.einsum('bhqd,bhkd->bhqk', qi, k).astype(F32) * scale
        p = jax.nn.softmax(s, axis=-1)
        return jnp.einsum('bhqk,bhkd->bhqd', p.astype(v.dtype), v)

    o = lax.map(block, qb)
    return o.transpose(1, 2, 0, 3, 4).reshape(B, H, T, -1)


def hgrn2_chunk_scan(q, k, v, log_f, s0):
    B, H, T, dk = q.shape
    n = T // CHUNK
    q, k, v, log_f = (a.reshape(B, H, n, CHUNK, a.shape[-1]) for a in (q, k, v, log_f))
    b = jnp.cumsum(log_f, axis=3)
    causal = jnp.tril(jnp.ones((CHUNK, CHUNK), bool))
    rel = jnp.where(causal[:, :, None], b[:, :, :, :, None, :] - b[:, :, :, None, :, :], -jnp.inf)
    scores = jnp.einsum('bhnik,bhnjk,bhnijk->bhnij', q, k, jnp.exp(rel))
    o_intra = jnp.einsum('bhnij,bhnjv->bhniv', scores, v)
    q_in = q * jnp.exp(b)
    k_out = k * jnp.exp(b[:, :, :, -1:] - b)
    decay = jnp.exp(b[:, :, :, -1])

    def step(s, xs):
        qi, ki, vi, di = xs
        o = jnp.einsum('bhck,bhkv->bhcv', qi, s)
        s = s * di[..., None] + jnp.einsum('bhck,bhcv->bhkv', ki, vi)
        return s, o

    xs = tuple(jnp.moveaxis(a, 2, 0) for a in (q_in, k_out, v, decay))
    s_fin, o_inter = lax.scan(step, s0, xs)
    o = o_intra + jnp.moveaxis(o_inter, 0, 2)
    return o.reshape(B, H, T, -1), s_fin


def hgrn2_mixer(u, lb, g_norm, s0):
    q, i, g, z_f, z_b = jnp.split(u, 5, axis=-1)
    q = heads(q, HG_HEADS) * HG_DK ** -0.5
    v = heads(i, HG_HEADS)
    out = 0.0
    finals = []
    for d, z in enumerate((z_f, z_b)):
        log_f = heads(jnp.logaddexp(jnp.log(lb[d]), jnp.log1p(-lb[d]) + jax.nn.log_sigmoid(z)), HG_HEADS)
        k = -jnp.expm1(log_f)
        if d == 0:
            o, s = hgrn2_chunk_scan(q, k, v, log_f, s0[:, d])
        else:
            o, s = hgrn2_chunk_scan(jnp.flip(q, 2), jnp.flip(k, 2), jnp.flip(v, 2), jnp.flip(log_f, 2), s0[:, d])
            o = jnp.flip(o, 2)
        out = out + o
        finals.append(s)
    return head_norm_gate(out, g_norm.reshape(HG_HEADS, HG_DV), g), jnp.stack(finals, axis=1)


def hyena_filter(L, w1, b1, freq, w2, b2, w3):
    w1, b1, freq, w2, b2, w3 = (a.astype(F32) for a in (w1, b1, freq, w2, b2, w3))
    pos = jnp.arange(L, dtype=F32)
    t = pos / (L - 1)
    bands = jnp.linspace(1e-4, HY_BANDS - 1, HY_BANDS, dtype=F32)
    ang = (2.0 * math.pi / L) * pos[:, None] * bands[None, :]
    z = jnp.concatenate([t[:, None], jnp.cos(ang), -jnp.sin(ang)], axis=-1)
    h = jnp.sin(freq * (z @ w1 + b1))
    h = jnp.sin(freq * (h @ w2 + b2))
    h = h @ w3
    max_decay = math.log(HY_TARGET) / HY_FAST
    min_decay = math.log(HY_TARGET) / HY_SLOW
    deltas = jnp.linspace(min_decay, max_decay, HY_W, dtype=F32)
    window = jnp.exp(-t[:, None] * jnp.abs(deltas)[None, :])
    h_f, h_b = h[:, :HY_W] * window, h[:, HY_W:] * window
    return jnp.concatenate([h_f, jnp.zeros((1, HY_W), F32), jnp.flip(h_b[1:], axis=0)], axis=0)


def hyena_mixer(u, conv_w, conv_b, w1, b1, freq, w2, b2, w3, skip):
    B, T, _ = u.shape
    uc = conv3_centred(u, conv_w) + conv_b
    x0, x1, v = jnp.split(uc, 3, axis=-1)
    z = x1 * v
    filt = hyena_filter(T, w1, b1, freq, w2, b2, w3)
    y = jnp.fft.irfft(jnp.fft.rfft(z, n=2 * T, axis=1) * jnp.fft.rfft(filt, n=2 * T, axis=0)[None], n=2 * T, axis=1)[:, :T]
    return x0 * (y + z * skip)


def mla_keys_values(ckv, krope, w_kv_up):
    B, S, _ = ckv.shape
    kv = heads(ckv @ w_kv_up, MLA_HEADS)
    k_nope, v = kv[..., :MLA_NOPE], kv[..., MLA_NOPE:]
    k = jnp.concatenate([k_nope, jnp.broadcast_to(krope[:, None], (B, MLA_HEADS, S, MLA_ROPE))], axis=-1)
    return k, v


def mla_mixer(u, q_norm_a, w_q_up, kv_norm_a, w_kv_up, qk_norm, rope, ctx):
    cq, ckv, krope = jnp.split(u, [MLA_Q_LORA, MLA_Q_LORA + MLA_KV_LORA], axis=-1)
    q = heads(rms_norm(cq, q_norm_a) @ w_q_up, MLA_HEADS)
    ckv = rms_norm(ckv, kv_norm_a)
    k, v = mla_keys_values(ckv, krope, w_kv_up)
    q = rms_norm(q, qk_norm[0])
    k = rms_norm(k, qk_norm[1])
    if ctx is not None:
        q = apply_rope(q, rope[0], rope[1])
        k = apply_rope(k, rope[0], rope[1])
        kc, vc = mla_keys_values(ctx[0], ctx[1], w_kv_up)
        k = jnp.concatenate([k, rms_norm(kc, qk_norm[1])], axis=2)
        v = jnp.concatenate([v, vc], axis=2)
    o = attention(q, k, v, (MLA_NOPE + MLA_ROPE) ** -0.5)
    return merge_heads(o), ckv, krope


def gdn_chunk_scan(q, k, v, beta, log_a, s0):
    B, H, T, dk = q.shape
    n = T // CHUNK
    q, k, v = (a.reshape(B, H, n, CHUNK, a.shape[-1]) for a in (q, k, v))
    beta, log_a = (a.reshape(B, H, n, CHUNK) for a in (beta, log_a))
    G = jnp.cumsum(log_a, axis=-1)
    incl = jnp.tril(jnp.ones((CHUNK, CHUNK), bool))
    strict = jnp.tril(jnp.ones((CHUNK, CHUNK), F32), -1)
    decay = jnp.exp(jnp.where(incl, G[..., :, None] - G[..., None, :], -jnp.inf))
    kb = k * beta[..., None]
    lmat = jnp.einsum('bhnik,bhnjk->bhnij', kb, k) * decay * strict
    eye = jnp.eye(CHUNK, dtype=F32)
    tmat = lax.linalg.triangular_solve(eye + lmat, jnp.broadcast_to(eye, lmat.shape), left_side=True, lower=True, unit_diagonal=True)
    u_w = tmat @ (v * beta[..., None])
    w_w = tmat @ (kb * jnp.exp(G)[..., None])
    attn = jnp.einsum('bhnik,bhnjk->bhnij', q, k) * decay
    q_in = q * jnp.exp(G)[..., None]
    k_out = k * jnp.exp(G[..., -1:] - G)[..., None]
    a_last = jnp.exp(G[..., -1])

    def step(s, xs):
        qi, ki, wi, ui, ai, di = xs
        v_new = ui - wi @ s
        o = qi @ s + ai @ v_new
        s = s * di[..., None, None] + jnp.swapaxes(ki, -1, -2) @ v_new
        return s, o

    xs = tuple(jnp.moveaxis(a, 2, 0) for a in (q_in, k_out, w_w, u_w, attn, a_last))
    s_fin, o = lax.scan(step, s0, xs)
    return jnp.moveaxis(o, 0, 2).reshape(B, H, T, -1), s_fin


def gdn_mixer(u, conv_w, a_log, dt_bias, g_norm, s0):
    B, T, _ = u.shape
    qkv, g, ab = jnp.split(u, [3 * GROUP_W, 4 * GROUP_W], axis=-1)
    qkv = jax.nn.silu(conv3_centred(qkv, conv_w))
    q, k, v = jnp.split(qkv, 3, axis=-1)
    q = l2_norm(heads(q, GD_HEADS)) * GD_DK ** -0.5
    k = l2_norm(heads(k, GD_HEADS))
    v = heads(v, GD_HEADS)
    a = ab[..., :2 * GD_HEADS].reshape(B, T, 2, GD_HEADS)
    bb = ab[..., 2 * GD_HEADS:].reshape(B, T, 2, GD_HEADS)
    a_log, dt_bias = a_log.astype(F32), dt_bias.astype(F32)
    out = 0.0
    finals = []
    for d in range(2):
        log_a = (-jnp.exp(a_log[d]) * jax.nn.softplus(a[:, :, d] + dt_bias[d])).transpose(0, 2, 1)
        beta = jax.nn.sigmoid(bb[:, :, d]).transpose(0, 2, 1)
        if d == 0:
            o, s = gdn_chunk_scan(q, k, v, beta, log_a, s0[:, d])
        else:
            o, s = gdn_chunk_scan(jnp.flip(q, 2), jnp.flip(k, 2), jnp.flip(v, 2), jnp.flip(beta, 2), jnp.flip(log_a, 2), s0[:, d])
            o = jnp.flip(o, 2)
        out = out + o
        finals.append(s)
    return head_norm_gate(out, g_norm, g), jnp.stack(finals, axis=1)


def trunk_layer(x, ada, lb, rope, ctx, l, P):
    m = [ada[:, i, None, :] for i in range(N_ADA)]
    h = rms_norm(x, P['norm_ffn'][l, 0]) * (1.0 + m[1]) + m[0]
    x = x + 0.5 * m[2] * swiglu(h, P['w_ffn_gu'][l, 0], P['w_ffn_down'][l, 0])
    h = rms_norm(x, P['norm_mix'][l]) * (1.0 + m[4]) + m[3]
    u = (h @ P['w_in'][l]).astype(F32)
    u_hg, u_hy, u_mla, u_gd = jnp.split(u, [HG_COLS, HG_COLS + HY_COLS, HG_COLS + HY_COLS + MLA_COLS], axis=-1)
    B = x.shape[0]
    if ctx is None:
        mla_ctx = None
        s_hg0 = jnp.zeros((B, 2, HG_HEADS, HG_DK, HG_DV), F32)
        s_gd0 = jnp.zeros((B, 2, GD_HEADS, GD_DK, GD_DV), F32)
    else:
        mla_ctx = (ctx[0].astype(F32), ctx[1].astype(F32))
        s_hg0, s_gd0 = ctx[2].astype(F32), ctx[3].astype(F32)
    o_hg, s_hg = hgrn2_mixer(u_hg, lb, P['hgrn_norm'][l], s_hg0)
    o_hy = hyena_mixer(u_hy, P['hy_conv_w'][l], P['hy_conv_b'][l], P['hy_w1'][l], P['hy_b1'][l], P['hy_freq'][l],
                       P['hy_w2'][l], P['hy_b2'][l], P['hy_w3'][l], P['hy_skip'][l])
    o_mla, ckv, krope = mla_mixer(u_mla, P['mla_q_norm_a'][l], P['mla_w_q_up'][l], P['mla_kv_norm_a'][l],
                                  P['mla_w_kv_up'][l], P['mla_qk_norm'][l], rope, mla_ctx)
    o_gd, s_gd = gdn_mixer(u_gd, P['gdn_conv_w'][l], P['gdn_a_log'][l], P['gdn_dt_bias'][l], P['gdn_norm'][l], s_gd0)
    o = jnp.concatenate([o_hg, o_hy, o_mla, o_gd], axis=-1).astype(x.dtype) @ P['w_out'][l]
    x = x + m[5] * o
    h = rms_norm(x, P['norm_ffn'][l, 1]) * (1.0 + m[7]) + m[6]
    x = x + 0.5 * m[8] * swiglu(h, P['w_ffn_gu'][l, 1], P['w_ffn_down'][l, 1])
    return x, (ckv, krope, s_hg, s_gd)


def setup_inputs(seed: int = 0) -> dict:
    key = jax.random.key(seed)
    ks = iter(jax.random.split(key, 48))

    def nrm(shape, s):
        return s * jax.random.normal(next(ks), shape, F32)

    def gain(shape):
        return 1.0 + nrm(shape, 0.02)

    dt = jnp.exp(jax.random.uniform(next(ks), (DEPTH, 2, GD_HEADS), F32, math.log(1e-3), math.log(1e-1)))
    a_init = jax.random.uniform(next(ks), (DEPTH, 2, GD_HEADS), F32, 1.0, 16.0)
    return {
        'x_prompt': nrm((BATCH, SEQ, D_MODEL), 1.0),
        'x_sample': nrm((DEC_BATCH, DEC_SEQ, D_MODEL), 1.0),
        'cache_mla_ckv': nrm((DEC_BATCH, DEPTH, PAST_LEN, MLA_KV_LORA), 1.0),
        'cache_mla_krope': nrm((DEC_BATCH, DEPTH, PAST_LEN, MLA_ROPE), 1.0),
        'state_hgrn': nrm((DEC_BATCH, DEPTH, 2, HG_HEADS, HG_DK, HG_DV), 0.3),
        'state_gdn': nrm((DEC_BATCH, DEPTH, 2, GD_HEADS, GD_DK, GD_DV), 0.1),
        'c': nrm((DEC_BATCH, D_MODEL), 1.0),
        'c_ctx': nrm((D_MODEL,), 1.0),
        'w_ada': nrm((DEPTH, D_MODEL, N_ADA * D_MODEL), 0.5 * D_MODEL ** -0.5),
        'b_ada': nrm((DEPTH, N_ADA * D_MODEL), 0.02),
        'norm_ffn': gain((DEPTH, 2, D_MODEL)),
        'w_ffn_gu': nrm((DEPTH, 2, D_MODEL, 2 * D_FF), D_MODEL ** -0.5),
        'w_ffn_down': nrm((DEPTH, 2, D_FF, D_MODEL), D_FF ** -0.5),
        'norm_mix': gain((DEPTH, D_MODEL)),
        'w_in': nrm((DEPTH, D_MODEL, IN_COLS), D_MODEL ** -0.5),
        'w_out': nrm((DEPTH, MIX_W, D_MODEL), MIX_W ** -0.5),
        'hgrn_lb': nrm((DEPTH, 2, GROUP_W), 0.1),
        'hgrn_norm': gain((DEPTH, GROUP_W)),
        'hy_conv_w': nrm((DEPTH, 3, HY_COLS), 0.5),
        'hy_conv_b': nrm((DEPTH, HY_COLS), 0.02),
        'hy_w1': nrm((DEPTH, HY_EMB, HY_FH), HY_EMB ** -0.5),
        'hy_b1': nrm((DEPTH, HY_FH), 0.1),
        'hy_freq': gain((DEPTH, HY_FH)),
        'hy_w2': nrm((DEPTH, HY_FH, HY_FH), HY_FH ** -0.5),
        'hy_b2': nrm((DEPTH, HY_FH), 0.1),
        'hy_w3': nrm((DEPTH, HY_FH, 2 * HY_W), 0.01),
        'hy_skip': nrm((DEPTH, HY_W), 0.5),
        'mla_q_norm_a': gain((DEPTH, MLA_Q_LORA)),
        'mla_w_q_up': nrm((DEPTH, MLA_Q_LORA, MLA_HEADS * (MLA_NOPE + MLA_ROPE)), MLA_Q_LORA ** -0.5),
        'mla_kv_norm_a': gain((DEPTH, MLA_KV_LORA)),
        'mla_w_kv_up': nrm((DEPTH, MLA_KV_LORA, MLA_HEADS * (MLA_NOPE + MLA_V)), MLA_KV_LORA ** -0.5),
        'mla_qk_norm': gain((DEPTH, 2, MLA_NOPE + MLA_ROPE)),
        'gdn_conv_w': nrm((DEPTH, 3, 3 * GROUP_W), 0.5),
        'gdn_a_log': jnp.log(a_init),
        'gdn_dt_bias': dt + jnp.log(-jnp.expm1(-dt)),
        'gdn_norm': gain((DEPTH, GD_DV)),
    }


def reference(x_prompt, x_sample, cache_mla_ckv, cache_mla_krope, state_hgrn, state_gdn, c, c_ctx,
              w_ada, b_ada, norm_ffn, w_ffn_gu, w_ffn_down, norm_mix, w_in, w_out, hgrn_lb, hgrn_norm,
              hy_conv_w, hy_conv_b, hy_w1, hy_b1, hy_freq, hy_w2, hy_b2, hy_w3, hy_skip,
              mla_q_norm_a, mla_w_q_up, mla_kv_norm_a, mla_w_kv_up, mla_qk_norm,
              gdn_conv_w, gdn_a_log, gdn_dt_bias, gdn_norm):
    P = dict(norm_ffn=norm_ffn, w_ffn_gu=w_ffn_gu, w_ffn_down=w_ffn_down, norm_mix=norm_mix, w_in=w_in,
             w_out=w_out, hgrn_norm=hgrn_norm, hy_conv_w=hy_conv_w, hy_conv_b=hy_conv_b, hy_w1=hy_w1,
             hy_b1=hy_b1, hy_freq=hy_freq, hy_w2=hy_w2, hy_b2=hy_b2, hy_w3=hy_w3, hy_skip=hy_skip,
             mla_q_norm_a=mla_q_norm_a, mla_w_q_up=mla_w_q_up, mla_kv_norm_a=mla_kv_norm_a,
             mla_w_kv_up=mla_w_kv_up, mla_qk_norm=mla_qk_norm, gdn_conv_w=gdn_conv_w, gdn_a_log=gdn_a_log,
             gdn_dt_bias=gdn_dt_bias, gdn_norm=gdn_norm)
    lb_all = jnp.cumsum(jax.nn.softmax(hgrn_lb.astype(F32), axis=0), axis=0)
    lb_all = lb_all - lb_all[0]
    rows = x_sample.shape[1] // GRID_W
    rope = rope_tables(rows)
    ctx_cond = c_ctx[None]
    y_prompt, y_sample = x_prompt, x_sample
    ctx_states = []
    for l in range(DEPTH):
        y_prompt, st = trunk_layer(y_prompt, adaln(ctx_cond, w_ada[l], b_ada[l]), lb_all[l], None, None, l, P)
        ctx_states.append(st)
        cached = (cache_mla_ckv[:, l], cache_mla_krope[:, l], state_hgrn[:, l], state_gdn[:, l])
        y_sample, _ = trunk_layer(y_sample, adaln(c, w_ada[l], b_ada[l]), lb_all[l], rope, cached, l, P)
    new_ckv = jnp.stack([s[0] for s in ctx_states], axis=1)
    new_krope = jnp.stack([s[1] for s in ctx_states], axis=1)
    new_hg = jnp.stack([s[2] for s in ctx_states], axis=1)
    new_gd = jnp.stack([s[3] for s in ctx_states], axis=1)
    return (y_prompt, y_sample, new_ckv, new_krope, new_hg, new_gd)
```

```python
import functools
import math

import numpy as np
import jax
import jax.numpy as jnp
from jax import lax
from jax.experimental import pallas as pl
from jax.experimental.pallas import tpu as pltpu

F32 = jnp.float32
BF16 = jnp.bfloat16

D_MODEL = 1024
BATCH = 16
SEQ = 256
DEPTH = 4
DEC_BATCH = 2
DEC_SEQ = 1024
PAST_LEN = 256
GRID_W = 64
N_ADA = 9
D_FF = 2816
GROUP_W = 256
CHUNK = 64
RMS_EPS = 1e-6
N_HEADS = 4
HEAD_W = 64
HY_EMB = 33
HY_FH = 64
HY_TARGET = 1e-2
HY_FAST = 0.3
HY_SLOW = 1.5
MLA_NOPE = 64
MLA_ROPE = 32
MLA_QK = MLA_NOPE + MLA_ROPE
MLA_Q_LORA = 256
MLA_KV_LORA = 128
ROPE_BASE = 10000.0

HG_COLS = 5 * GROUP_W
HY_COLS = 3 * GROUP_W
MLA_COLS = MLA_Q_LORA + MLA_KV_LORA + MLA_ROPE
GD_COLS = 4 * GROUP_W + 16
MLA_PAD = 512
GD_PAD = 1152

N_PROMPT = BATCH * SEQ
N_SAMPLE = DEC_BATCH * DEC_SEQ
N_TOK = N_PROMPT + N_SAMPLE
LANE = 128
VMEM_LIMIT = 56 * 1024 * 1024
ROW_TILE = 1024
FF_TILE = 256
ADA_TILE = 1536
ATT_QBLOCK = 256


def _bdot(a, b):
    return jnp.dot(a.astype(BF16), b.astype(BF16), preferred_element_type=F32)


def _bdot_nt(a, b):
    return lax.dot_general(a.astype(BF16), b.astype(BF16), (((1,), (1,)), ((), ())),
                           preferred_element_type=F32)


def _bdot_tn(a, b):
    return lax.dot_general(a.astype(BF16), b.astype(BF16), (((0,), (0,)), ((), ())),
                           preferred_element_type=F32)


def _split2(x):
    hi = x.astype(BF16)
    lo = (x - hi.astype(F32)).astype(BF16)
    return hi, lo


def _split3(x):
    hi = x.astype(BF16)
    r = x - hi.astype(F32)
    mid = r.astype(BF16)
    lo = (r - mid.astype(F32)).astype(BF16)
    return hi, mid, lo


def _dot3(a, b):
    ah, al = _split2(a)
    bh, bl = _split2(b)
    return (jnp.dot(ah, bh, preferred_element_type=F32) + jnp.dot(ah, bl, preferred_element_type=F32)
            + jnp.dot(al, bh, preferred_element_type=F32))


def _sel_dot(c, x):
    h, m, l = _split3(x)
    return (jnp.dot(c, h, preferred_element_type=F32) + jnp.dot(c, m, preferred_element_type=F32)
            + jnp.dot(c, l, preferred_element_type=F32))


def _dot_sel(x, c):
    h, m, l = _split3(x)
    return (jnp.dot(h, c, preferred_element_type=F32) + jnp.dot(m, c, preferred_element_type=F32)
            + jnp.dot(l, c, preferred_element_type=F32))


def _sigmoid(x):
    return 1.0 / (1.0 + jnp.exp(-x))


def _silu(x):
    return x * _sigmoid(x)


def _rms(x, g):
    return x * lax.rsqrt(jnp.mean(x * x, axis=-1, keepdims=True) + RMS_EPS) * g


def _cparams(sem):
    return pltpu.CompilerParams(dimension_semantics=sem, vmem_limit_bytes=VMEM_LIMIT)


def _cond_of_tile(i):
    return jnp.maximum(i - (N_PROMPT // ROW_TILE - 1), 0)


def _ada_kernel(c_ref, w_ref, b_ref, o_ref):
    o_ref[...] = _dot3(_silu(c_ref[...]), w_ref[...]) + b_ref[...]


def _ada_call(cond8, w_ada, b_ada):
    n = N_ADA * D_MODEL
    out = pl.pallas_call(
        _ada_kernel,
        grid=(DEPTH, n // ADA_TILE),
        in_specs=[
            pl.BlockSpec((8, D_MODEL), lambda l, j: (0, 0)),
            pl.BlockSpec((None, D_MODEL, ADA_TILE), lambda l, j: (l, 0, j)),
            pl.BlockSpec((None, 1, ADA_TILE), lambda l, j: (l, 0, j)),
        ],
        out_specs=pl.BlockSpec((None, 8, ADA_TILE), lambda l, j: (l, 0, j)),
        out_shape=jax.ShapeDtypeStruct((DEPTH, 8, n), F32),
        compiler_params=_cparams(("parallel", "parallel")),
        name="ada",
    )(cond8, w_ada, b_ada.reshape(DEPTH, 1, n))
    return out.reshape(DEPTH, 8, N_ADA, D_MODEL)


def _ffn_kernel(x_ref, ada_ref, g_ref, wg_ref, wu_ref, wd_ref, o_ref, h_scr, acc_scr, *, sub):
    f = pl.program_id(1)

    @pl.when(f == 0)
    def _():
        y = _rms(x_ref[...], g_ref[...])
        h = y * (1.0 + ada_ref[3 * sub + 1:3 * sub + 2, :]) + ada_ref[3 * sub:3 * sub + 1, :]
        h_scr[...] = h.astype(BF16)
        acc_scr[...] = jnp.zeros_like(acc_scr)

    h = h_scr[...]
    gate = jnp.dot(h, wg_ref[...].astype(BF16), preferred_element_type=F32)
    up = jnp.dot(h, wu_ref[...].astype(BF16), preferred_element_type=F32)
    a = (_silu(gate) * up).astype(BF16)
    acc_scr[...] += jnp.dot(a, wd_ref[...].astype(BF16), preferred_element_type=F32)

    @pl.when(f == pl.num_programs(1) - 1)
    def _():
        o_ref[...] = x_ref[...] + 0.5 * ada_ref[3 * sub + 2:3 * sub + 3, :] * acc_scr[...]


def _ffn_call(x, ada, norm_ffn, w_gu, w_down, l, j):
    sub = 2 * j
    nf = D_FF // FF_TILE
    return pl.pallas_call(
        functools.partial(_ffn_kernel, sub=sub),
        grid=(N_TOK // ROW_TILE, nf),
        in_specs=[
            pl.BlockSpec((ROW_TILE, D_MODEL), lambda i, f: (i, 0)),
            pl.BlockSpec((None, None, N_ADA, D_MODEL), lambda i, f: (l, _cond_of_tile(i), 0, 0)),
            pl.BlockSpec((None, None, 1, D_MODEL), lambda i, f: (l, j, 0, 0)),
            pl.BlockSpec((None, None, D_MODEL, FF_TILE), lambda i, f: (l, j, 0, f)),
            pl.BlockSpec((None, None, D_MODEL, FF_TILE), lambda i, f: (l, j, 0, nf + f)),
            pl.BlockSpec((None, None, FF_TILE, D_MODEL), lambda i, f: (l, j, f, 0)),
        ],
        out_specs=pl.BlockSpec((ROW_TILE, D_MODEL), lambda i, f: (i, 0)),
        out_shape=jax.ShapeDtypeStruct((N_TOK, D_MODEL), F32),
        scratch_shapes=[pltpu.VMEM((ROW_TILE, D_MODEL), BF16), pltpu.VMEM((ROW_TILE, D_MODEL), F32)],
        compiler_params=_cparams(("parallel", "arbitrary")),
        name="ffn",
    )(x, ada, norm_ffn.reshape(DEPTH, 2, 1, D_MODEL), w_gu, w_gu, w_down)


IN_TILE = 512


def _inproj_kernel(x_ref, ada_ref, g_ref, w1, w2, w3, w4, o1, o2, o3, o4):
    y = _rms(x_ref[...], g_ref[...])
    h = (y * (1.0 + ada_ref[4:5, :]) + ada_ref[3:4, :]).astype(BF16)
    for w, o in ((w1, o1), (w2, o2), (w3, o3), (w4, o4)):
        o[...] = jnp.dot(h, w[...], preferred_element_type=F32)


def _inproj_call(x, ada, norm_mix, ws, l):
    widths = (HG_COLS, HY_COLS, MLA_PAD, GD_PAD)
    per = ROW_TILE // IN_TILE
    return pl.pallas_call(
        _inproj_kernel,
        grid=(N_TOK // IN_TILE,),
        in_specs=[
            pl.BlockSpec((IN_TILE, D_MODEL), lambda i: (i, 0)),
            pl.BlockSpec((None, None, N_ADA, D_MODEL), lambda i: (l, _cond_of_tile(i // per), 0, 0)),
            pl.BlockSpec((None, 1, D_MODEL), lambda i: (l, 0, 0)),
        ] + [pl.BlockSpec((None, D_MODEL, w), lambda i: (l, 0, 0)) for w in widths],
        out_specs=[pl.BlockSpec((IN_TILE, w), lambda i: (i, 0)) for w in widths],
        out_shape=[jax.ShapeDtypeStruct((N_TOK, w), F32) for w in widths],
        compiler_params=_cparams(("parallel",)),
        name="inproj",
    )(x, ada, norm_mix.reshape(DEPTH, 1, D_MODEL), *ws)


OUT_TILE = 512


def _outproj_kernel(x_ref, ada_ref, w_ref, *refs):
    o_ref = refs[-1]
    i = pl.program_id(0)
    n_p = N_PROMPT // OUT_TILE

    def run(srcs):
        acc = jnp.zeros((OUT_TILE, D_MODEL), F32)
        for g, s in enumerate(srcs):
            acc += jnp.dot(s[...].astype(BF16), w_ref[g * GROUP_W:(g + 1) * GROUP_W, :],
                           preferred_element_type=F32)
        o_ref[...] = x_ref[...] + ada_ref[5:6, :] * acc

    @pl.when(i < n_p)
    def _():
        run(refs[0:4])

    @pl.when(i >= n_p)
    def _():
        run(refs[4:8])


def _outproj_call(x, ada, w_out_bf, o_p, o_s, l):
    per = ROW_TILE // OUT_TILE
    n_p = N_PROMPT // OUT_TILE
    n_s = N_SAMPLE // OUT_TILE
    return pl.pallas_call(
        _outproj_kernel,
        grid=(N_TOK // OUT_TILE,),
        in_specs=[
            pl.BlockSpec((OUT_TILE, D_MODEL), lambda i: (i, 0)),
            pl.BlockSpec((None, None, N_ADA, D_MODEL), lambda i: (l, _cond_of_tile(i // per), 0, 0)),
            pl.BlockSpec((None, D_MODEL, D_MODEL), lambda i: (l, 0, 0)),
        ] + [pl.BlockSpec((OUT_TILE, GROUP_W), lambda i: (jnp.minimum(i, n_p - 1), 0))] * 4
          + [pl.BlockSpec((OUT_TILE, GROUP_W), lambda i: (jnp.clip(i - n_p, 0, n_s - 1), 0))] * 4,
        out_specs=pl.BlockSpec((OUT_TILE, D_MODEL), lambda i: (i, 0)),
        out_shape=jax.ShapeDtypeStruct((N_TOK, D_MODEL), F32),
        compiler_params=_cparams(("parallel",)),
        name="outproj",
    )(x, ada, w_out_bf, *o_p, *o_s)


def _block_diag_ones():
    idx = np.arange(GROUP_W) // HEAD_W
    return (idx[:, None] == idx[None, :]).astype(np.float32)


def _hgrn_consts():
    C = CHUNK
    i = np.arange(C)[:, None]
    t = np.arange(C)[None, :]
    mats = [(t <= i), (t > i)]
    masks = []
    s = C // 2
    while s >= 1:
        blk = i // (2 * s)
        up = (i // s) % 2 == 1
        mu = blk * 2 * s + s - 1
        a = np.where(up, (t > mu) & (t <= i), (t > i) & (t <= mu))
        mats.append(a)
        jj = np.arange(C)[None, :]
        up_j = (jj // s) % 2 == 1
        masks.append(up & (~up_j) & (blk == jj // (2 * s)))
        s //= 2
    fwd_a = np.concatenate([m.astype(np.float32) for m in mats], axis=0)
    fwd_m = np.stack([m.astype(np.float32) for m in masks])
    bwd_a = np.concatenate([m.astype(np.float32)[::-1, ::-1] for m in mats], axis=0)
    bwd_m = np.stack([m.astype(np.float32)[::-1, ::-1] for m in masks])
    return np.stack([fwd_a, bwd_a]), np.stack([fwd_m, bwd_m])


def _gdn_consts():
    C = CHUNK
    i = np.arange(C)[:, None]
    t = np.arange(C)[None, :]
    tril = np.stack([(t <= i), (t >= i)]).astype(np.float32)
    masks = np.stack([np.stack([(t <= i), (t < i)]), np.stack([(t >= i), (t > i)])]).astype(np.float32)
    expand = np.zeros((2, LANE, 2 * GROUP_W), np.float32)
    for d in range(2):
        for h in range(N_HEADS):
            expand[d, d * N_HEADS + h, h * HEAD_W:(h + 1) * HEAD_W] = 1.0
            expand[d, 8 + d * N_HEADS + h, GROUP_W + h * HEAD_W:GROUP_W + (h + 1) * HEAD_W] = 1.0
    return tril, masks, expand


def _dft_consts(T):
    n2 = 4 * T
    k = np.arange(T, dtype=np.int64)[:, None]
    s = np.arange(T, dtype=np.int64)[None, :]
    ang = np.pi * (((2 * k + 1) * s) % n2).astype(np.float64) / (2 * T)
    fwd = np.concatenate([np.cos(ang), -np.sin(ang)], axis=0)
    inv = fwd.T / T
    return fwd.astype(np.float32), inv.astype(np.float32)


def _np_split2(x):
    hi = jnp.asarray(x, F32).astype(BF16)
    lo = (jnp.asarray(x, F32) - hi.astype(F32)).astype(BF16)
    return hi, lo


def _hyena_pos_consts(T):
    pos = np.arange(T, dtype=np.float32)
    t = pos / np.float32(T - 1)
    bands = np.linspace(1e-4, (HY_EMB - 1) // 2 - 1, (HY_EMB - 1) // 2, dtype=np.float32)
    ang = (np.float32(2.0 * math.pi / T) * pos[:, None]) * bands[None, :]
    z = np.concatenate([t[:, None], np.cos(ang), -np.sin(ang)], axis=-1).astype(np.float32)
    zp = np.zeros((T, LANE), np.float32)
    zp[:, :HY_EMB] = z
    max_decay = math.log(HY_TARGET) / HY_FAST
    min_decay = math.log(HY_TARGET) / HY_SLOW
    deltas = np.linspace(min_decay, max_decay, GROUP_W, dtype=np.float32)
    window = np.exp(-t[:, None] * np.abs(deltas)[None, :]).astype(np.float32)
    return zp, window


def _rope_consts(T):
    rows = T // GRID_W
    row = np.repeat(np.arange(rows, dtype=np.float32), GRID_W)
    col = (np.arange(T) % GRID_W).astype(np.float32)
    pairs = MLA_ROPE // 4
    inv = (np.float32(ROPE_BASE) ** (-np.arange(pairs, dtype=np.float32) / np.float32(pairs))).astype(np.float32)
    ang = np.concatenate([row[:, None] * inv, col[:, None] * inv], axis=-1).astype(np.float32)
    cos, sin = np.cos(ang), np.sin(ang)
    cosf = np.ones((T, LANE), np.float32)
    sinf = np.zeros((T, LANE), np.float32)
    half = MLA_ROPE // 2
    cosf[:, MLA_NOPE:MLA_NOPE + half] = cos
    cosf[:, MLA_NOPE + half:MLA_QK] = cos
    sinf[:, MLA_NOPE:MLA_NOPE + half] = -sin
    sinf[:, MLA_NOPE + half:MLA_QK] = sin
    return cosf, sinf


def _head_norm_gate(tot, bd, gn, gate):
    ms = _sel_dot_right(tot * tot, bd) * (1.0 / HEAD_W)
    return tot * lax.rsqrt(ms + RMS_EPS) * gn * _silu(gate)


def _sel_dot_right(x, c):
    h, l = _split2(x)
    return jnp.dot(h, c, preferred_element_type=F32) + jnp.dot(l, c, preferred_element_type=F32)


def _hgrn_kernel(*refs, T, has_s0):
    if has_s0:
        (uq_ref, z_ref, lb_ref, gn_ref, amat_ref, lmask_ref, bd_ref, s0_ref,
         o_ref, sfin_ref, lf_s, kk_s, st_s) = refs
    else:
        (uq_ref, z_ref, lb_ref, gn_ref, amat_ref, lmask_ref, bd_ref,
         o_ref, sfin_ref, lf_s, kk_s, st_s) = refs
    d = pl.program_id(1)
    n = T // CHUNK
    C = CHUNK

    z = z_ref[...]
    lb = lb_ref[...]
    log_sig = jnp.minimum(z, 0.0) - jnp.log(1.0 + jnp.exp(-jnp.abs(z)))
    a = jnp.log(lb)
    c = jnp.log(1.0 - lb) + log_sig
    m = jnp.maximum(a, c)
    lf_s[...] = m + jnp.log(1.0 + jnp.exp(jnp.minimum(a, c) - m))
    kk_s[...] = (1.0 - lb) * _sigmoid(-z)

    if has_s0:
        st_s[...] = jnp.concatenate([s0_ref[h].T for h in range(N_HEADS)], axis=-1)
    else:
        st_s[...] = jnp.zeros_like(st_s)

    amat = amat_ref[...]
    bd = bd_ref[...]

    def chunk(ci, carry):
        cidx = jnp.where(d == 0, ci, n - 1 - ci)
        r0 = pl.multiple_of(cidx * C, C)
        rows = pl.ds(r0, C)
        q = uq_ref[rows, 0:GROUP_W] * (HEAD_W ** -0.5)
        v = uq_ref[rows, GROUP_W:2 * GROUP_W]
        lf = lf_s[rows, :]
        k = kk_s[rows, :]
        r = _sel_dot(amat, lf)
        er = jnp.exp(r)
        qin = q * er[0:C]
        kout = k * er[C:2 * C]
        decay = jnp.exp(r[0:1] + r[C:C + 1])
        qs = [q * er[(2 + lv) * C:(3 + lv) * C] for lv in range(6)]
        ks = [k * er[(2 + lv) * C:(3 + lv) * C] for lv in range(6)]
        st = st_s[...]
        outs = []
        new_st = []
        for h in range(N_HEADS):
            sl = slice(h * HEAD_W, (h + 1) * HEAD_W)
            sc = jnp.zeros((C, C), F32)
            for lv in range(6):
                sc += lmask_ref[lv] * _bdot_nt(qs[lv][:, sl], ks[lv][:, sl])
            st_h = st[:, sl]
            outs.append(_bdot(sc, v[:, sl]) + _bdot_nt(qin[:, sl], st_h))
            new_st.append(st_h * decay[:, sl] + _bdot_tn(v[:, sl], kout[:, sl]))
        diag = _bdot(q * k, bd) * v
        o = jnp.concatenate(outs, axis=-1) + diag
        st_s[...] = jnp.concatenate(new_st, axis=-1)

        @pl.when(d == 0)
        def _():
            o_ref[rows, :] = o

        @pl.when(d == 1)
        def _():
            tot = o_ref[rows, :] + o
            o_ref[rows, :] = _head_norm_gate(tot, bd, gn_ref[...], uq_ref[rows, 2 * GROUP_W:3 * GROUP_W])

        return carry

    lax.fori_loop(0, n, chunk, 0)
    for h in range(N_HEADS):
        sfin_ref[h] = st_s[:, h * HEAD_W:(h + 1) * HEAD_W].T


def _hgrn_call(u_hg, lb_l, gn, consts, s0, l, T, nb, row0):
    amat, lmask, bd = consts
    tb = row0 // T
    has_s0 = s0 is not None
    in_specs = [
        pl.BlockSpec((T, 3 * GROUP_W), lambda b, d: (tb + b, 0)),
        pl.BlockSpec((T, GROUP_W), lambda b, d: (tb + b, 3 + d)),
        pl.BlockSpec((None, 1, GROUP_W), lambda b, d: (d, 0, 0)),
        pl.BlockSpec((1, GROUP_W), lambda b, d: (0, 0)),
        pl.BlockSpec((None, 8 * CHUNK, CHUNK), lambda b, d: (d, 0, 0)),
        pl.BlockSpec((None, 6, CHUNK, CHUNK), lambda b, d: (d, 0, 0, 0)),
        pl.BlockSpec((GROUP_W, GROUP_W), lambda b, d: (0, 0)),
    ]
    args = [u_hg, u_hg, lb_l.reshape(2, 1, GROUP_W), gn.reshape(1, GROUP_W), amat, lmask, bd]
    if has_s0:
        in_specs.append(pl.BlockSpec((None, None, None, N_HEADS, HEAD_W, HEAD_W), lambda b, d: (b, l, d, 0, 0, 0)))
        args.append(s0)
    return pl.pallas_call(
        functools.partial(_hgrn_kernel, T=T, has_s0=has_s0),
        grid=(nb, 2),
        in_specs=in_specs,
        out_specs=[
            pl.BlockSpec((T, GROUP_W), lambda b, d: (b, 0)),
            pl.BlockSpec((None, None, N_HEADS, HEAD_W, HEAD_W), lambda b, d: (b, d, 0, 0, 0)),
        ],
        out_shape=[jax.ShapeDtypeStruct((nb * T, GROUP_W), F32),
                   jax.ShapeDtypeStruct((nb, 2, N_HEADS, HEAD_W, HEAD_W), F32)],
        scratch_shapes=[pltpu.VMEM((T, GROUP_W), F32), pltpu.VMEM((T, GROUP_W), F32),
                        pltpu.VMEM((HEAD_W, GROUP_W), F32)],
        compiler_params=_cparams(("parallel", "arbitrary")),
        name="hgrn",
    )(*args)


def _shift_rows(x, T):
    row = lax.broadcasted_iota(jnp.int32, x.shape, 0)
    prev = jnp.where(row == 0, 0.0, pltpu.roll(x, 1, 0))
    nxt = jnp.where(row == T - 1, 0.0, pltpu.roll(x, T - 1, 0))
    return prev, nxt


def _conv3(x, w_ref, T):
    prev, nxt = _shift_rows(x, T)
    return prev * w_ref[0:1, :] + x * w_ref[1:2, :] + nxt * w_ref[2:3, :]


def _gdn_kernel(*refs, T, has_s0):
    if has_s0:
        (u_ref, cw_ref, alog_ref, dtb_ref, exp_ref, tril_ref, mask_ref, bd_ref, gn_ref, s0_ref,
         o_ref, sfin_ref, q_s, k_s, v_s, la_s, be_s, st_s) = refs
    else:
        (u_ref, cw_ref, alog_ref, dtb_ref, exp_ref, tril_ref, mask_ref, bd_ref, gn_ref,
         o_ref, sfin_ref, q_s, k_s, v_s, la_s, be_s, st_s) = refs
    d = pl.program_id(1)
    n = T // CHUNK
    C = CHUNK
    bd = bd_ref[...]

    @pl.when(d == 0)
    def _():
        qkv = _silu(_conv3(u_ref[:, 0:3 * GROUP_W], cw_ref, T))
        q = qkv[:, 0:GROUP_W]
        k = qkv[:, GROUP_W:2 * GROUP_W]
        q_s[...] = q * lax.rsqrt(_sel_dot_right(q * q, bd) + 1e-6) * (HEAD_W ** -0.5)
        k_s[...] = k * lax.rsqrt(_sel_dot_right(k * k, bd) + 1e-6)
        v_s[...] = qkv[:, 2 * GROUP_W:3 * GROUP_W]

    ab = u_ref[:, 4 * GROUP_W:4 * GROUP_W + LANE]
    xa = ab + dtb_ref[...]
    softplus = jnp.maximum(xa, 0.0) + jnp.log(1.0 + jnp.exp(-jnp.abs(xa)))
    log_a = -jnp.exp(alog_ref[...]) * softplus
    lane = lax.broadcasted_iota(jnp.int32, ab.shape, 1)
    narrow = jnp.where(lane < 8, log_a, _sigmoid(ab))
    wide = _dot_sel(narrow, exp_ref[...])
    la_s[...] = wide[:, 0:GROUP_W]
    be_s[...] = wide[:, GROUP_W:2 * GROUP_W]

    if has_s0:
        st_s[...] = jnp.concatenate([s0_ref[h] for h in range(N_HEADS)], axis=-1)
    else:
        st_s[...] = jnp.zeros_like(st_s)

    tril = tril_ref[...]
    incl = mask_ref[0] > 0.5
    strict = mask_ref[1]

    def chunk(ci, carry):
        cidx = jnp.where(d == 0, ci, n - 1 - ci)
        r0 = pl.multiple_of(cidx * C, C)
        rows = pl.ds(r0, C)
        q = q_s[rows, :]
        k = k_s[rows, :]
        v = v_s[rows, :]
        la = la_s[rows, :]
        be = be_s[rows, :]
        gx = _sel_dot(tril, la)
        gtot = jnp.sum(la, axis=0, keepdims=True)
        eg = jnp.exp(gx)
        kout = k * jnp.exp(gtot - gx)
        qin = q * eg
        alast = jnp.exp(gtot)
        kb = k * be
        vb = v * be
        kbg = kb * eg
        st = st_s[...]
        outs = []
        new_st = []
        for h in range(N_HEADS):
            sl = slice(h * HEAD_W, (h + 1) * HEAD_W)
            gh = gx[:, sl]
            dmat = gh - gh.T
            dec = jnp.where(incl, jnp.exp(jnp.where(incl, dmat, 0.0)), 0.0)
            nmat = -(_bdot_nt(kb[:, sl], k[:, sl]) * dec * strict)
            x = jnp.concatenate([vb[:, sl], kbg[:, sl]], axis=-1)
            p = nmat
            for step in range(6):
                x = x + _dot3(p, x)
                if step < 5:
                    p = _dot3(p, p)
            uw = x[:, 0:HEAD_W]
            ww = x[:, HEAD_W:2 * HEAD_W]
            attn = _bdot_nt(q[:, sl], k[:, sl]) * dec
            s_h = st[:, sl]
            vnew = uw - _bdot(ww, s_h)
            outs.append(_bdot(qin[:, sl], s_h) + _bdot(attn, vnew))
            new_st.append(s_h * alast[:, sl] + _bdot_tn(kout[:, sl], vnew))
        o = jnp.concatenate(outs, axis=-1)
        st_s[...] = jnp.concatenate(new_st, axis=-1)

        @pl.when(d == 0)
        def _():
            o_ref[rows, :] = o

        @pl.when(d == 1)
        def _():
            tot = o_ref[rows, :] + o
            o_ref[rows, :] = _head_norm_gate(tot, bd, gn_ref[...], u_ref[rows, 3 * GROUP_W:4 * GROUP_W])

        return carry

    lax.fori_loop(0, n, chunk, 0)
    for h in range(N_HEADS):
        sfin_ref[h] = st_s[:, h * HEAD_W:(h + 1) * HEAD_W]


def _gdn_call(u_gd, cw, alog, dtb, gn, consts, s0, l, T, nb, row0):
    tril, masks, expand, bd = consts
    tb = row0 // T
    has_s0 = s0 is not None
    in_specs = [
        pl.BlockSpec((T, GD_PAD), lambda b, d: (tb + b, 0)),
        pl.BlockSpec((3, 3 * GROUP_W), lambda b, d: (0, 0)),
        pl.BlockSpec((1, LANE), lambda b, d: (0, 0)),
        pl.BlockSpec((1, LANE), lambda b, d: (0, 0)),
        pl.BlockSpec((None, LANE, 2 * GROUP_W), lambda b, d: (d, 0, 0)),
        pl.BlockSpec((None, CHUNK, CHUNK), lambda b, d: (d, 0, 0)),
        pl.BlockSpec((None, 2, CHUNK, CHUNK), lambda b, d: (d, 0, 0, 0)),
        pl.BlockSpec((GROUP_W, GROUP_W), lambda b, d: (0, 0)),
        pl.BlockSpec((1, GROUP_W), lambda b, d: (0, 0)),
    ]
    args = [u_gd, cw, alog, dtb, expand, tril, masks, bd, gn]
    if has_s0:
        in_specs.append(pl.BlockSpec((None, None, None, N_HEADS, HEAD_W, HEAD_W), lambda b, d: (b, l, d, 0, 0, 0)))
        args.append(s0)
    return pl.pallas_call(
        functools.partial(_gdn_kernel, T=T, has_s0=has_s0),
        grid=(nb, 2),
        in_specs=in_specs,
        out_specs=[
            pl.BlockSpec((T, GROUP_W), lambda b, d: (b, 0)),
            pl.BlockSpec((None, None, N_HEADS, HEAD_W, HEAD_W), lambda b, d: (b, d, 0, 0, 0)),
        ],
        out_shape=[jax.ShapeDtypeStruct((nb * T, GROUP_W), F32),
                   jax.ShapeDtypeStruct((nb, 2, N_HEADS, HEAD_W, HEAD_W), F32)],
        scratch_shapes=[pltpu.VMEM((T, GROUP_W), F32)] * 5 + [pltpu.VMEM((HEAD_W, GROUP_W), F32)],
        compiler_params=_cparams(("parallel", "arbitrary")),
        name="gdn",
    )(*args)


def _hyfilt_kernel(z_ref, win_ref, fh_ref, fl_ref, w1_ref, b1_ref, fr_ref, w2_ref, b2_ref, w3_ref, o_ref, *, T):
    fr = fr_ref[...]
    h = jnp.sin(fr * (_dot3(z_ref[...], w1_ref[...]) + b1_ref[...]))
    h = jnp.sin(fr * (_dot3(h, w2_ref[...]) + b2_ref[...]))
    h = _dot3(h, w3_ref[...])
    win = win_ref[...]
    hf = h[:, 0:GROUP_W] * win
    hb = h[:, GROUP_W:2 * GROUP_W] * win
    row = lax.broadcasted_iota(jnp.int32, hb.shape, 0)
    hb = jnp.where(row == 0, 0.0, hb)
    sh, sl = _split2(jnp.concatenate([hf + hb, hf - hb], axis=-1))
    fh = fh_ref[...]
    spec = (jnp.dot(fh, sh, preferred_element_type=F32) + jnp.dot(fh, sl, preferred_element_type=F32)
            + jnp.dot(fl_ref[...], sh, preferred_element_type=F32))
    o_ref[0:T, :] = spec[0:T, 0:GROUP_W]
    o_ref[T:2 * T, :] = spec[T:2 * T, GROUP_W:2 * GROUP_W]


def _hyfilt_call(T, zp, win, fh, fl, w1p, b1, freq, w2, b2, w3):
    c2 = lambda l: (0, 0)
    return pl.pallas_call(
        functools.partial(_hyfilt_kernel, T=T),
        grid=(DEPTH,),
        in_specs=[
            pl.BlockSpec((T, LANE), c2),
            pl.BlockSpec((T, GROUP_W), c2),
            pl.BlockSpec((2 * T, T), c2),
            pl.BlockSpec((2 * T, T), c2),
            pl.BlockSpec((None, LANE, HY_FH), lambda l: (l, 0, 0)),
            pl.BlockSpec((None, 1, HY_FH), lambda l: (l, 0, 0)),
            pl.BlockSpec((None, 1, HY_FH), lambda l: (l, 0, 0)),
            pl.BlockSpec((None, HY_FH, HY_FH), lambda l: (l, 0, 0)),
            pl.BlockSpec((None, 1, HY_FH), lambda l: (l, 0, 0)),
            pl.BlockSpec((None, HY_FH, 2 * GROUP_W), lambda l: (l, 0, 0)),
        ],
        out_specs=pl.BlockSpec((None, 2 * T, GROUP_W), lambda l: (l, 0, 0)),
        out_shape=jax.ShapeDtypeStruct((DEPTH, 2 * T, GROUP_W), F32),
        compiler_params=_cparams(("parallel",)),
        name="hyfilt",
    )(zp, win, fh, fl, w1p, b1, freq, w2, b2, w3)


def _hyena_kernel(u_ref, cw_ref, cb_ref, spec_ref, skip_ref, fh_ref, fl_ref, ih_ref, il_ref, o_ref, *, T):
    uc = _conv3(u_ref[...], cw_ref, T) + cb_ref[...]
    x0 = uc[:, 0:GROUP_W]
    z = uc[:, GROUP_W:2 * GROUP_W] * uc[:, 2 * GROUP_W:3 * GROUP_W]
    zh, zl = _split2(z)
    fh = fh_ref[...]
    zs = (jnp.dot(fh, zh, preferred_element_type=F32) + jnp.dot(fh, zl, preferred_element_type=F32)
          + jnp.dot(fl_ref[...], zh, preferred_element_type=F32))
    ar, ai = zs[0:T], zs[T:2 * T]
    br, bi = spec_ref[0:T, :], spec_ref[T:2 * T, :]
    ph, plo = _split2(jnp.concatenate([ar * br - ai * bi, ar * bi + ai * br], axis=0))
    ih = ih_ref[...]
    y = (jnp.dot(ih, ph, preferred_element_type=F32) + jnp.dot(ih, plo, preferred_element_type=F32)
         + jnp.dot(il_ref[...], ph, preferred_element_type=F32))
    o_ref[...] = x0 * (y + z * skip_ref[...])


def _hyena_call(u_hy, cw, cb, spec, skip, dft, l, T, nb, row0):
    fh, fl, ih, il = dft
    tb = row0 // T
    c2 = lambda b: (0, 0)
    return pl.pallas_call(
        functools.partial(_hyena_kernel, T=T),
        grid=(nb,),
        in_specs=[
            pl.BlockSpec((T, HY_COLS), lambda b: (tb + b, 0)),
            pl.BlockSpec((3, HY_COLS), c2),
            pl.BlockSpec((1, HY_COLS), c2),
            pl.BlockSpec((None, 2 * T, GROUP_W), lambda b: (l, 0, 0)),
            pl.BlockSpec((1, GROUP_W), c2),
            pl.BlockSpec((2 * T, T), c2),
            pl.BlockSpec((2 * T, T), c2),
            pl.BlockSpec((T, 2 * T), c2),
            pl.BlockSpec((T, 2 * T), c2),
        ],
        out_specs=pl.BlockSpec((T, GROUP_W), lambda b: (b, 0)),
        out_shape=jax.ShapeDtypeStruct((nb * T, GROUP_W), F32),
        compiler_params=_cparams(("parallel",)),
        name="hyena",
    )(u_hy, cw, cb, spec, skip, fh, fl, ih, il)


def _rope(x, cosf, sinf):
    lane = lax.broadcasted_iota(jnp.int32, x.shape, 1)
    half = MLA_ROPE // 2
    partner = jnp.where(lane < MLA_NOPE + half, pltpu.roll(x, LANE - half, 1), pltpu.roll(x, half, 1))
    return x * cosf + partner * sinf


def _qk_norm(x, g):
    ms = jnp.sum(x * x, axis=-1, keepdims=True) * (1.0 / MLA_QK)
    return x * lax.rsqrt(ms + RMS_EPS) * g


def _mla_kernel(*refs, T, ctx):
    if ctx:
        (u_ref, qn_ref, wq_ref, kvn_ref, wkv_ref, qkn_ref, cos_ref, sin_ref, cckv_ref, ckr_ref, o_ref) = refs
    else:
        (u_ref, qn_ref, wq_ref, kvn_ref, wkv_ref, qkn_ref, o_ref, ckv_ref, kr_ref) = refs
    u = u_ref[...]
    cq = _rms(u[:, 0:MLA_Q_LORA], qn_ref[...])
    ckv = _rms(u[:, MLA_Q_LORA:MLA_Q_LORA + MLA_KV_LORA], kvn_ref[...])
    kr = u[:, MLA_Q_LORA + MLA_KV_LORA:MLA_Q_LORA + MLA_KV_LORA + MLA_ROPE]
    if not ctx:
        ckv_ref[...] = ckv
        kr_ref[...] = kr
    q_all = _bdot(cq, wq_ref[...])
    kv = _bdot(ckv, wkv_ref[...])
    gq = qkn_ref[0:1, :]
    gk = qkn_ref[1:2, :]
    if ctx:
        kvc = _bdot(cckv_ref[...], wkv_ref[...])
        krc = ckr_ref[...]
        cosf, sinf = cos_ref[...], sin_ref[...]
    scale = MLA_QK ** -0.5
    outs = []
    for h in range(N_HEADS):
        qh = _qk_norm(q_all[:, h * LANE:(h + 1) * LANE], gq)
        zpad = jnp.zeros((T, LANE - MLA_QK), F32)
        kh = _qk_norm(jnp.concatenate([kv[:, h * HEAD_W:(h + 1) * HEAD_W], kr, zpad], axis=-1), gk)
        vh = kv[:, GROUP_W + h * HEAD_W:GROUP_W + (h + 1) * HEAD_W]
        if ctx:
            qh = _rope(qh, cosf, sinf)
            kh = _rope(kh, cosf, sinf)
            s_len = krc.shape[0]
            zc = jnp.zeros((s_len, LANE - MLA_QK), F32)
            kc = _qk_norm(jnp.concatenate([kvc[:, h * HEAD_W:(h + 1) * HEAD_W], krc, zc], axis=-1), gk)
            kh = jnp.concatenate([kh, kc], axis=0)
            vh = jnp.concatenate([vh, kvc[:, GROUP_W + h * HEAD_W:GROUP_W + (h + 1) * HEAD_W]], axis=0)
        khb = kh.astype(BF16)
        vhb = vh.astype(BF16)
        blocks = []
        for qb in range(T // ATT_QBLOCK):
            s = _bdot_nt(qh[qb * ATT_QBLOCK:(qb + 1) * ATT_QBLOCK], khb) * scale
            e = jnp.exp(s - jnp.max(s, axis=-1, keepdims=True))
            blocks.append(_bdot(e, vhb) / jnp.sum(e, axis=-1, keepdims=True))
        outs.append(blocks[0] if len(blocks) == 1 else jnp.concatenate(blocks, axis=0))
    o_ref[...] = jnp.concatenate(outs, axis=-1)


def _mla_call(u_mla, qn, wq, kvn, wkv, qkn, rope, cache, l, T, nb, row0):
    tb = row0 // T
    ctx = cache is not None
    c2 = lambda b: (0, 0)
    in_specs = [
        pl.BlockSpec((T, MLA_PAD), lambda b: (tb + b, 0)),
        pl.BlockSpec((1, MLA_Q_LORA), c2),
        pl.BlockSpec((MLA_Q_LORA, N_HEADS * LANE), c2),
        pl.BlockSpec((1, MLA_KV_LORA), c2),
        pl.BlockSpec((MLA_KV_LORA, 2 * GROUP_W), c2),
        pl.BlockSpec((2, LANE), c2),
    ]
    args = [u_mla, qn, wq, kvn, wkv, qkn]
    out_specs = [pl.BlockSpec((T, GROUP_W), lambda b: (b, 0))]
    out_shape = [jax.ShapeDtypeStruct((nb * T, GROUP_W), F32)]
    if ctx:
        in_specs += [
            pl.BlockSpec((T, LANE), c2),
            pl.BlockSpec((T, LANE), c2),
            pl.BlockSpec((None, None, PAST_LEN, MLA_KV_LORA), lambda b: (b, l, 0, 0)),
            pl.BlockSpec((None, None, PAST_LEN, MLA_ROPE), lambda b: (b, l, 0, 0)),
        ]
        args += [rope[0], rope[1], cache[0], cache[1]]
    else:
        out_specs += [pl.BlockSpec((T, MLA_KV_LORA), lambda b: (b, 0)),
                      pl.BlockSpec((T, MLA_ROPE), lambda b: (b, 0))]
        out_shape += [jax.ShapeDtypeStruct((nb * T, MLA_KV_LORA), F32),
                      jax.ShapeDtypeStruct((nb * T, MLA_ROPE), F32)]
    return pl.pallas_call(
        functools.partial(_mla_kernel, T=T, ctx=ctx),
        grid=(nb,),
        in_specs=in_specs,
        out_specs=out_specs,
        out_shape=out_shape,
        compiler_params=_cparams(("parallel",)),
        name="mla",
    )(*args)


def _pad_cols(w, width):
    return jnp.pad(w, [(0, 0)] * (w.ndim - 1) + [(0, width - w.shape[-1])])


def _prep_w_in(w_in):
    o1 = HG_COLS
    o2 = o1 + HY_COLS
    o3 = o2 + MLA_COLS
    return (w_in[..., :o1].astype(BF16), w_in[..., o1:o2].astype(BF16),
            _pad_cols(w_in[..., o2:o3], MLA_PAD).astype(BF16), _pad_cols(w_in[..., o3:], GD_PAD).astype(BF16))


def _prep_wq(w_q_up):
    w = w_q_up.reshape(DEPTH, MLA_Q_LORA, N_HEADS, MLA_QK)
    return _pad_cols(w, LANE).reshape(DEPTH, MLA_Q_LORA, N_HEADS * LANE).astype(BF16)


def _prep_wkv(w_kv_up):
    w = w_kv_up.reshape(DEPTH, MLA_KV_LORA, N_HEADS, 2, HEAD_W)
    return w.transpose(0, 1, 3, 2, 4).reshape(DEPTH, MLA_KV_LORA, 2 * GROUP_W).astype(BF16)


def _lower_bounds(hgrn_lb):
    lb = jnp.cumsum(jax.nn.softmax(hgrn_lb.astype(F32), axis=0), axis=0)
    return lb - lb[0]


def kernel(x_prompt, x_sample, cache_mla_ckv, cache_mla_krope, state_hgrn, state_gdn, c, c_ctx, w_ada, b_ada, norm_ffn, w_ffn_gu, w_ffn_down, norm_mix, w_in, w_out, hgrn_lb, hgrn_norm, hy_conv_w, hy_conv_b, hy_w1, hy_b1, hy_freq, hy_w2, hy_b2, hy_w3, hy_skip, mla_q_norm_a, mla_w_q_up, mla_kv_norm_a, mla_w_kv_up, mla_qk_norm, gdn_conv_w, gdn_a_log, gdn_dt_bias, gdn_norm):
    x = jnp.concatenate([x_prompt.reshape(N_PROMPT, D_MODEL), x_sample.reshape(N_SAMPLE, D_MODEL)], axis=0)

    cond8 = jnp.zeros((8, D_MODEL), F32).at[0].set(c_ctx).at[1:1 + DEC_BATCH].set(c)
    ada = _ada_call(cond8, w_ada, b_ada)

    w_in_parts = _prep_w_in(w_in)
    w_out_bf = w_out.astype(BF16)
    wq = _prep_wq(mla_w_q_up)
    wkv = _prep_wkv(mla_w_kv_up)
    qkn = _pad_cols(mla_qk_norm, LANE)
    lb_all = _lower_bounds(hgrn_lb)
    alog = _pad_cols(gdn_a_log.reshape(DEPTH, 1, 8), LANE)
    dtb = _pad_cols(gdn_dt_bias.reshape(DEPTH, 1, 8), LANE)
    gdn_gn = jnp.tile(gdn_norm, (1, N_HEADS)).reshape(DEPTH, 1, GROUP_W)
    w1p = jnp.pad(hy_w1, ((0, 0), (0, LANE - HY_EMB), (0, 0)))

    bd = jnp.asarray(_block_diag_ones(), BF16)
    hg_a, hg_m = _hgrn_consts()
    hg_consts = (jnp.asarray(hg_a, BF16), jnp.asarray(hg_m, F32), bd)
    gd_tril, gd_masks, gd_expand = _gdn_consts()
    gd_consts = (jnp.asarray(gd_tril, BF16), jnp.asarray(gd_masks, F32), jnp.asarray(gd_expand, BF16), bd)
    rope = tuple(jnp.asarray(a) for a in _rope_consts(DEC_SEQ))
    groups = ((SEQ, BATCH, 0), (DEC_SEQ, DEC_BATCH, N_PROMPT))
    dft = {}
    spec = {}
    for T, _, _ in groups:
        fwd, inv = _dft_consts(T)
        fh, fl = _np_split2(fwd)
        ih, il = _np_split2(inv)
        dft[T] = (fh, fl, ih, il)
        zp, win = _hyena_pos_consts(T)
        spec[T] = _hyfilt_call(T, jnp.asarray(zp), jnp.asarray(win), fh, fl, w1p,
                               hy_b1.reshape(DEPTH, 1, HY_FH), hy_freq.reshape(DEPTH, 1, HY_FH), hy_w2,
                               hy_b2.reshape(DEPTH, 1, HY_FH), hy_w3)

    new_ckv, new_kr, new_hg, new_gd = [], [], [], []
    for l in range(DEPTH):
        x = _ffn_call(x, ada, norm_ffn, w_ffn_gu, w_ffn_down, l, 0)
        u_hg, u_hy, u_mla, u_gd = _inproj_call(x, ada, norm_mix, w_in_parts, l)
        outs = []
        for gi, (T, nb, row0) in enumerate(groups):
            latent = gi == 1
            o_hg, s_hg = _hgrn_call(u_hg, lb_all[l], hgrn_norm[l], hg_consts,
                                    state_hgrn if latent else None, l, T, nb, row0)
            o_hy = _hyena_call(u_hy, hy_conv_w[l], hy_conv_b[l].reshape(1, HY_COLS), spec[T],
                               hy_skip[l].reshape(1, GROUP_W), dft[T], l, T, nb, row0)
            mla = _mla_call(u_mla, mla_q_norm_a[l].reshape(1, MLA_Q_LORA), wq[l],
                            mla_kv_norm_a[l].reshape(1, MLA_KV_LORA), wkv[l], qkn[l],
                            rope if latent else None,
                            (cache_mla_ckv, cache_mla_krope) if latent else None, l, T, nb, row0)
            o_gd, s_gd = _gdn_call(u_gd, gdn_conv_w[l], alog[l], dtb[l], gdn_gn[l], gd_consts,
                                   state_gdn if latent else None, l, T, nb, row0)
            outs.append((o_hg, o_hy, mla[0], o_gd))
            if not latent:
                new_ckv.append(mla[1].reshape(BATCH, SEQ, MLA_KV_LORA))
                new_kr.append(mla[2].reshape(BATCH, SEQ, MLA_ROPE))
                new_hg.append(s_hg)
                new_gd.append(s_gd)
        x = _outproj_call(x, ada, w_out_bf, outs[0], outs[1], l)
        x = _ffn_call(x, ada, norm_ffn, w_ffn_gu, w_ffn_down, l, 1)

    y_prompt = x[:N_PROMPT].reshape(BATCH, SEQ, D_MODEL)
    y_sample = x[N_PROMPT:].reshape(DEC_BATCH, DEC_SEQ, D_MODEL)
    return (y_prompt, y_sample, jnp.stack(new_ckv, axis=1), jnp.stack(new_kr, axis=1),
            jnp.stack(new_hg, axis=1), jnp.stack(new_gd, axis=1))
```

```python
import functools
import math

import numpy as np
import jax
import jax.numpy as jnp
from jax import lax
from jax.experimental import pallas as pl
from jax.experimental.pallas import tpu as pltpu

F32 = jnp.float32
BF16 = jnp.bfloat16

D_MODEL = 1024
BATCH = 16
SEQ = 256
DEPTH = 4
DEC_BATCH = 2
DEC_SEQ = 1024
PAST_LEN = 256
GRID_W = 64
N_ADA = 9
D_FF = 2816
GROUP_W = 256
CHUNK = 64
RMS_EPS = 1e-6
N_HEADS = 4
HEAD_W = 64
HY_EMB = 33
HY_FH = 64
HY_TARGET = 1e-2
HY_FAST = 0.3
HY_SLOW = 1.5
MLA_NOPE = 64
MLA_ROPE = 32
MLA_QK = MLA_NOPE + MLA_ROPE
MLA_Q_LORA = 256
MLA_KV_LORA = 128
ROPE_BASE = 10000.0

HG_COLS = 5 * GROUP_W
HY_COLS = 3 * GROUP_W
MLA_COLS = MLA_Q_LORA + MLA_KV_LORA + MLA_ROPE
GD_COLS = 4 * GROUP_W + 16
MLA_PAD = 512
GD_PAD = 1152

N_PROMPT = BATCH * SEQ
N_SAMPLE = DEC_BATCH * DEC_SEQ
N_TOK = N_PROMPT + N_SAMPLE
LANE = 128
VMEM_LIMIT = 56 * 1024 * 1024
ROW_TILE = 1024
FF_TILE = 256
ADA_TILE = 1536
ATT_QBLOCK = 256


def _bdot(a, b):
    return jnp.dot(a.astype(BF16), b.astype(BF16), preferred_element_type=F32)


def _bdot_nt(a, b):
    return lax.dot_general(a.astype(BF16), b.astype(BF16), (((1,), (1,)), ((), ())),
                           preferred_element_type=F32)


def _bdot_tn(a, b):
    return lax.dot_general(a.astype(BF16), b.astype(BF16), (((0,), (0,)), ((), ())),
                           preferred_element_type=F32)


def _split2(x):
    hi = x.astype(BF16)
    lo = (x - hi.astype(F32)).astype(BF16)
    return hi, lo


def _split3(x):
    hi = x.astype(BF16)
    r = x - hi.astype(F32)
    mid = r.astype(BF16)
    lo = (r - mid.astype(F32)).astype(BF16)
    return hi, mid, lo


def _dot3(a, b):
    ah, al = _split2(a)
    bh, bl = _split2(b)
    return (jnp.dot(ah, bh, preferred_element_type=F32) + jnp.dot(ah, bl, preferred_element_type=F32)
            + jnp.dot(al, bh, preferred_element_type=F32))


def _sel_dot(c, x):
    h, m, l = _split3(x)
    return (jnp.dot(c, h, preferred_element_type=F32) + jnp.dot(c, m, preferred_element_type=F32)
            + jnp.dot(c, l, preferred_element_type=F32))


def _dot_sel(x, c):
    h, m, l = _split3(x)
    return (jnp.dot(h, c, preferred_element_type=F32) + jnp.dot(m, c, preferred_element_type=F32)
            + jnp.dot(l, c, preferred_element_type=F32))


def _sigmoid(x):
    return 1.0 / (1.0 + jnp.exp(-x))


def _silu(x):
    return x * _sigmoid(x)


def _rms(x, g):
    return x * lax.rsqrt(jnp.mean(x * x, axis=-1, keepdims=True) + RMS_EPS) * g


def _cparams(sem):
    return pltpu.CompilerParams(dimension_semantics=sem, vmem_limit_bytes=VMEM_LIMIT)


def _cond_of_tile(i):
    return jnp.maximum(i - (N_PROMPT // ROW_TILE - 1), 0)


def _ada_kernel(c_ref, w_ref, b_ref, o_ref):
    o_ref[...] = _dot3(_silu(c_ref[...]), w_ref[...]) + b_ref[...]


def _ada_call(cond8, w_ada, b_ada):
    n = N_ADA * D_MODEL
    out = pl.pallas_call(
        _ada_kernel,
        grid=(DEPTH, n // ADA_TILE),
        in_specs=[
            pl.BlockSpec((8, D_MODEL), lambda l, j: (0, 0)),
            pl.BlockSpec((None, D_MODEL, ADA_TILE), lambda l, j: (l, 0, j)),
            pl.BlockSpec((None, 1, ADA_TILE), lambda l, j: (l, 0, j)),
        ],
        out_specs=pl.BlockSpec((None, 8, ADA_TILE), lambda l, j: (l, 0, j)),
        out_shape=jax.ShapeDtypeStruct((DEPTH, 8, n), F32),
        compiler_params=_cparams(("parallel", "parallel")),
        name="ada",
    )(cond8, w_ada, b_ada.reshape(DEPTH, 1, n))
    return out.reshape(DEPTH, 8, N_ADA, D_MODEL)


def _ffn_kernel(x_ref, ada_ref, g_ref, wg_ref, wu_ref, wd_ref, o_ref, h_scr, acc_scr, *, sub):
    f = pl.program_id(1)

    @pl.when(f == 0)
    def _():
        y = _rms(x_ref[...], g_ref[...])
        h = y * (1.0 + ada_ref[3 * sub + 1:3 * sub + 2, :]) + ada_ref[3 * sub:3 * sub + 1, :]
        h_scr[...] = h.astype(BF16)
        acc_scr[...] = jnp.zeros_like(acc_scr)

    h = h_scr[...]
    gate = jnp.dot(h, wg_ref[...].astype(BF16), preferred_element_type=F32)
    up = jnp.dot(h, wu_ref[...].astype(BF16), preferred_element_type=F32)
    a = (_silu(gate) * up).astype(BF16)
    acc_scr[...] += jnp.dot(a, wd_ref[...].astype(BF16), preferred_element_type=F32)

    @pl.when(f == pl.num_programs(1) - 1)
    def _():
        o_ref[...] = x_ref[...] + 0.5 * ada_ref[3 * sub + 2:3 * sub + 3, :] * acc_scr[...]


def _ffn_call(x, ada, norm_ffn, w_gu, w_down, l, j):
    sub = 2 * j
    nf = D_FF // FF_TILE
    return pl.pallas_call(
        functools.partial(_ffn_kernel, sub=sub),
        grid=(N_TOK // ROW_TILE, nf),
        in_specs=[
            pl.BlockSpec((ROW_TILE, D_MODEL), lambda i, f: (i, 0)),
            pl.BlockSpec((None, None, N_ADA, D_MODEL), lambda i, f: (l, _cond_of_tile(i), 0, 0)),
            pl.BlockSpec((None, None, 1, D_MODEL), lambda i, f: (l, j, 0, 0)),
            pl.BlockSpec((None, None, D_MODEL, FF_TILE), lambda i, f: (l, j, 0, f)),
            pl.BlockSpec((None, None, D_MODEL, FF_TILE), lambda i, f: (l, j, 0, nf + f)),
            pl.BlockSpec((None, None, FF_TILE, D_MODEL), lambda i, f: (l, j, f, 0)),
        ],
        out_specs=pl.BlockSpec((ROW_TILE, D_MODEL), lambda i, f: (i, 0)),
        out_shape=jax.ShapeDtypeStruct((N_TOK, D_MODEL), F32),
        scratch_shapes=[pltpu.VMEM((ROW_TILE, D_MODEL), BF16), pltpu.VMEM((ROW_TILE, D_MODEL), F32)],
        compiler_params=_cparams(("parallel", "arbitrary")),
        name="ffn",
    )(x, ada, norm_ffn.reshape(DEPTH, 2, 1, D_MODEL), w_gu, w_gu, w_down)


IN_TILE = 512


def _inproj_kernel(x_ref, ada_ref, g_ref, w1, w2, w3, w4, o1, o2, o3, o4):
    y = _rms(x_ref[...], g_ref[...])
    h = (y * (1.0 + ada_ref[4:5, :]) + ada_ref[3:4, :]).astype(BF16)
    for w, o in ((w1, o1), (w2, o2), (w3, o3), (w4, o4)):
        o[...] = jnp.dot(h, w[...], preferred_element_type=F32)


def _inproj_call(x, ada, norm_mix, ws, l):
    widths = (HG_COLS, HY_COLS, MLA_PAD, GD_PAD)
    per = ROW_TILE // IN_TILE
    return pl.pallas_call(
        _inproj_kernel,
        grid=(N_TOK // IN_TILE,),
        in_specs=[
            pl.BlockSpec((IN_TILE, D_MODEL), lambda i: (i, 0)),
            pl.BlockSpec((None, None, N_ADA, D_MODEL), lambda i: (l, _cond_of_tile(i // per), 0, 0)),
            pl.BlockSpec((None, 1, D_MODEL), lambda i: (l, 0, 0)),
        ] + [pl.BlockSpec((None, D_MODEL, w), lambda i: (l, 0, 0)) for w in widths],
        out_specs=[pl.BlockSpec((IN_TILE, w), lambda i: (i, 0)) for w in widths],
        out_shape=[jax.ShapeDtypeStruct((N_TOK, w), F32) for w in widths],
        compiler_params=_cparams(("parallel",)),
        name="inproj",
    )(x, ada, norm_mix.reshape(DEPTH, 1, D_MODEL), *ws)


OUT_TILE = 512


def _outproj_kernel(x_ref, ada_ref, w_ref, *refs):
    o_ref = refs[-1]
    i = pl.program_id(0)
    n_p = N_PROMPT // OUT_TILE

    def run(srcs):
        acc = jnp.zeros((OUT_TILE, D_MODEL), F32)
        for g, s in enumerate(srcs):
            acc += jnp.dot(s[...].astype(BF16), w_ref[g * GROUP_W:(g + 1) * GROUP_W, :],
                           preferred_element_type=F32)
        o_ref[...] = x_ref[...] + ada_ref[5:6, :] * acc

    @pl.when(i < n_p)
    def _():
        run(refs[0:4])

    @pl.when(i >= n_p)
    def _():
        run(refs[4:8])


def _outproj_call(x, ada, w_out_bf, o_p, o_s, l):
    per = ROW_TILE // OUT_TILE
    n_p = N_PROMPT // OUT_TILE
    n_s = N_SAMPLE // OUT_TILE
    return pl.pallas_call(
        _outproj_kernel,
        grid=(N_TOK // OUT_TILE,),
        in_specs=[
            pl.BlockSpec((OUT_TILE, D_MODEL), lambda i: (i, 0)),
            pl.BlockSpec((None, None, N_ADA, D_MODEL), lambda i: (l, _cond_of_tile(i // per), 0, 0)),
            pl.BlockSpec((None, D_MODEL, D_MODEL), lambda i: (l, 0, 0)),
        ] + [pl.BlockSpec((OUT_TILE, GROUP_W), lambda i: (jnp.minimum(i, n_p - 1), 0))] * 4
          + [pl.BlockSpec((OUT_TILE, GROUP_W), lambda i: (jnp.clip(i - n_p, 0, n_s - 1), 0))] * 4,
        out_specs=pl.BlockSpec((OUT_TILE, D_MODEL), lambda i: (i, 0)),
        out_shape=jax.ShapeDtypeStruct((N_TOK, D_MODEL), F32),
        compiler_params=_cparams(("parallel",)),
        name="outproj",
    )(x, ada, w_out_bf, *o_p, *o_s)


def _block_diag_ones():
    idx = np.arange(GROUP_W) // HEAD_W
    return (idx[:, None] == idx[None, :]).astype(np.float32)


def _hgrn_consts():
    C = CHUNK
    i = np.arange(C)[:, None]
    t = np.arange(C)[None, :]
    mats = [(t <= i), (t > i)]
    masks = []
    s = C // 2
    while s >= 1:
        blk = i // (2 * s)
        up = (i // s) % 2 == 1
        mu = blk * 2 * s + s - 1
        a = np.where(up, (t > mu) & (t <= i), (t > i) & (t <= mu))
        mats.append(a)
        jj = np.arange(C)[None, :]
        up_j = (jj // s) % 2 == 1
        masks.append(up & (~up_j) & (blk == jj // (2 * s)))
        s //= 2
    fwd_a = np.concatenate([m.astype(np.float32) for m in mats], axis=0)
    fwd_m = np.stack([m.astype(np.float32) for m in masks])
    bwd_a = np.concatenate([m.astype(np.float32)[::-1, ::-1] for m in mats], axis=0)
    bwd_m = np.stack([m.astype(np.float32)[::-1, ::-1] for m in masks])
    return np.stack([fwd_a, bwd_a]), np.stack([fwd_m, bwd_m])


def _gdn_consts():
    C = CHUNK
    i = np.arange(C)[:, None]
    t = np.arange(C)[None, :]
    tril = np.stack([(t <= i), (t >= i)]).astype(np.float32)
    masks = np.stack([np.stack([(t <= i), (t < i)]), np.stack([(t >= i), (t > i)])]).astype(np.float32)
    expand = np.zeros((2, LANE, 2 * GROUP_W), np.float32)
    for d in range(2):
        for h in range(N_HEADS):
            expand[d, d * N_HEADS + h, h * HEAD_W:(h + 1) * HEAD_W] = 1.0
            expand[d, 8 + d * N_HEADS + h, GROUP_W + h * HEAD_W:GROUP_W + (h + 1) * HEAD_W] = 1.0
    return tril, masks, expand


def _dft_consts(T):
    n2 = 4 * T
    k = np.arange(T, dtype=np.int64)[:, None]
    s = np.arange(T, dtype=np.int64)[None, :]
    ang = np.pi * (((2 * k + 1) * s) % n2).astype(np.float64) / (2 * T)
    fwd = np.concatenate([np.cos(ang), -np.sin(ang)], axis=0)
    inv = fwd.T / T
    return fwd.astype(np.float32), inv.astype(np.float32)


def _np_split2(x):
    hi = jnp.asarray(x, F32).astype(BF16)
    lo = (jnp.asarray(x, F32) - hi.astype(F32)).astype(BF16)
    return hi, lo


def _hyena_pos_consts(T):
    pos = np.arange(T, dtype=np.float32)
    t = pos / np.float32(T - 1)
    bands = np.linspace(1e-4, (HY_EMB - 1) // 2 - 1, (HY_EMB - 1) // 2, dtype=np.float32)
    ang = (np.float32(2.0 * math.pi / T) * pos[:, None]) * bands[None, :]
    z = np.concatenate([t[:, None], np.cos(ang), -np.sin(ang)], axis=-1).astype(np.float32)
    zp = np.zeros((T, LANE), np.float32)
    zp[:, :HY_EMB] = z
    max_decay = math.log(HY_TARGET) / HY_FAST
    min_decay = math.log(HY_TARGET) / HY_SLOW
    deltas = np.linspace(min_decay, max_decay, GROUP_W, dtype=np.float32)
    window = np.exp(-t[:, None] * np.abs(deltas)[None, :]).astype(np.float32)
    return zp, window


def _rope_consts(T):
    rows = T // GRID_W
    row = np.repeat(np.arange(rows, dtype=np.float32), GRID_W)
    col = (np.arange(T) % GRID_W).astype(np.float32)
    pairs = MLA_ROPE // 4
    inv = (np.float32(ROPE_BASE) ** (-np.arange(pairs, dtype=np.float32) / np.float32(pairs))).astype(np.float32)
    ang = np.concatenate([row[:, None] * inv, col[:, None] * inv], axis=-1).astype(np.float32)
    cos, sin = np.cos(ang), np.sin(ang)
    cosf = np.ones((T, LANE), np.float32)
    sinf = np.zeros((T, LANE), np.float32)
    half = MLA_ROPE // 2
    cosf[:, MLA_NOPE:MLA_NOPE + half] = cos
    cosf[:, MLA_NOPE + half:MLA_QK] = cos
    sinf[:, MLA_NOPE:MLA_NOPE + half] = -sin
    sinf[:, MLA_NOPE + half:MLA_QK] = sin
    return cosf, sinf


def _head_norm_gate(tot, bd, gn, gate):
    ms = _sel_dot_right(tot * tot, bd) * (1.0 / HEAD_W)
    return tot * lax.rsqrt(ms + RMS_EPS) * gn * _silu(gate)


def _sel_dot_right(x, c):
    h, l = _split2(x)
    return jnp.dot(h, c, preferred_element_type=F32) + jnp.dot(l, c, preferred_element_type=F32)


def _hgrn_kernel(*refs, T, has_s0):
    if has_s0:
        (uq_ref, z_ref, lb_ref, gn_ref, amat_ref, lmask_ref, bd_ref, s0_ref,
         o_ref, sfin_ref, lf_s, kk_s, st_s) = refs
    else:
        (uq_ref, z_ref, lb_ref, gn_ref, amat_ref, lmask_ref, bd_ref,
         o_ref, sfin_ref, lf_s, kk_s, st_s) = refs
    d = pl.program_id(1)
    n = T // CHUNK
    C = CHUNK

    z = z_ref[...]
    lb = lb_ref[...]
    log_sig = jnp.minimum(z, 0.0) - jnp.log(1.0 + jnp.exp(-jnp.abs(z)))
    a = jnp.log(lb)
    c = jnp.log(1.0 - lb) + log_sig
    m = jnp.maximum(a, c)
    lf_s[...] = m + jnp.log(1.0 + jnp.exp(jnp.minimum(a, c) - m))
    kk_s[...] = (1.0 - lb) * _sigmoid(-z)

    if has_s0:
        st_s[...] = jnp.concatenate([s0_ref[h].T for h in range(N_HEADS)], axis=-1)
    else:
        st_s[...] = jnp.zeros_like(st_s)

    amat = amat_ref[...]
    bd = bd_ref[...]

    def chunk(ci, carry):
        cidx = jnp.where(d == 0, ci, n - 1 - ci)
        r0 = pl.multiple_of(cidx * C, C)
        rows = pl.ds(r0, C)
        q = uq_ref[rows, 0:GROUP_W] * (HEAD_W ** -0.5)
        v = uq_ref[rows, GROUP_W:2 * GROUP_W]
        lf = lf_s[rows, :]
        k = kk_s[rows, :]
        r = _sel_dot(amat, lf)
        er = jnp.exp(r)
        qin = q * er[0:C]
        kout = k * er[C:2 * C]
        decay = jnp.exp(r[0:1] + r[C:C + 1])
        qs = [q * er[(2 + lv) * C:(3 + lv) * C] for lv in range(6)]
        ks = [k * er[(2 + lv) * C:(3 + lv) * C] for lv in range(6)]
        st = st_s[...]
        outs = []
        new_st = []
        for h in range(N_HEADS):
            sl = slice(h * HEAD_W, (h + 1) * HEAD_W)
            sc = jnp.zeros((C, C), F32)
            for lv in range(6):
                sc += lmask_ref[lv] * _bdot_nt(qs[lv][:, sl], ks[lv][:, sl])
            st_h = st[:, sl]
            outs.append(_bdot(sc, v[:, sl]) + _bdot_nt(qin[:, sl], st_h))
            new_st.append(st_h * decay[:, sl] + _bdot_tn(v[:, sl], kout[:, sl]))
        diag = _bdot(q * k, bd) * v
        o = jnp.concatenate(outs, axis=-1) + diag
        st_s[...] = jnp.concatenate(new_st, axis=-1)

        @pl.when(d == 0)
        def _():
            o_ref[rows, :] = o

        @pl.when(d == 1)
        def _():
            tot = o_ref[rows, :] + o
            o_ref[rows, :] = _head_norm_gate(tot, bd, gn_ref[...], uq_ref[rows, 2 * GROUP_W:3 * GROUP_W])

        return carry

    lax.fori_loop(0, n, chunk, 0)
    for h in range(N_HEADS):
        sfin_ref[h] = st_s[:, h * HEAD_W:(h + 1) * HEAD_W].T


def _hgrn_call(u_hg, lb_l, gn, consts, s0, l, T, nb, row0):
    amat, lmask, bd = consts
    tb = row0 // T
    has_s0 = s0 is not None
    in_specs = [
        pl.BlockSpec((T, 3 * GROUP_W), lambda b, d: (tb + b, 0)),
        pl.BlockSpec((T, GROUP_W), lambda b, d: (tb + b, 3 + d)),
        pl.BlockSpec((None, 1, GROUP_W), lambda b, d: (d, 0, 0)),
        pl.BlockSpec((1, GROUP_W), lambda b, d: (0, 0)),
        pl.BlockSpec((None, 8 * CHUNK, CHUNK), lambda b, d: (d, 0, 0)),
        pl.BlockSpec((None, 6, CHUNK, CHUNK), lambda b, d: (d, 0, 0, 0)),
        pl.BlockSpec((GROUP_W, GROUP_W), lambda b, d: (0, 0)),
    ]
    args = [u_hg, u_hg, lb_l.reshape(2, 1, GROUP_W), gn.reshape(1, GROUP_W), amat, lmask, bd]
    if has_s0:
        in_specs.append(pl.BlockSpec((None, None, None, N_HEADS, HEAD_W, HEAD_W), lambda b, d: (b, l, d, 0, 0, 0)))
        args.append(s0)
    return pl.pallas_call(
        functools.partial(_hgrn_kernel, T=T, has_s0=has_s0),
        grid=(nb, 2),
        in_specs=in_specs,
        out_specs=[
            pl.BlockSpec((T, GROUP_W), lambda b, d: (b, 0)),
            pl.BlockSpec((None, None, N_HEADS, HEAD_W, HEAD_W), lambda b, d: (b, d, 0, 0, 0)),
        ],
        out_shape=[jax.ShapeDtypeStruct((nb * T, GROUP_W), F32),
                   jax.ShapeDtypeStruct((nb, 2, N_HEADS, HEAD_W, HEAD_W), F32)],
        scratch_shapes=[pltpu.VMEM((T, GROUP_W), F32), pltpu.VMEM((T, GROUP_W), F32),
                        pltpu.VMEM((HEAD_W, GROUP_W), F32)],
        compiler_params=_cparams(("parallel", "arbitrary")),
        name="hgrn",
    )(*args)


def _shift_rows(x, T):
    row = lax.broadcasted_iota(jnp.int32, x.shape, 0)
    prev = jnp.where(row == 0, 0.0, pltpu.roll(x, 1, 0))
    nxt = jnp.where(row == T - 1, 0.0, pltpu.roll(x, T - 1, 0))
    return prev, nxt


def _conv3(x, w_ref, T):
    prev, nxt = _shift_rows(x, T)
    return prev * w_ref[0:1, :] + x * w_ref[1:2, :] + nxt * w_ref[2:3, :]


GDN_UNROLL = 1


def _solve_unit_lower(systems):
    w = systems[0][0].shape[-1]
    slabs = [jnp.concatenate([rhs, nmat, nmat, nmat], axis=-1) for rhs, nmat in systems]
    steps = int(math.log2(CHUNK))
    for step in range(steps):
        last = step == steps - 1
        nxt = []
        for slab in slabs:
            hi = slab.astype(BF16)
            lo = (slab - hi.astype(F32)).astype(BF16)
            lhs = jnp.concatenate([hi[:, w:w + 2 * CHUNK], lo[:, w + 2 * CHUNK:w + 3 * CHUNK]], axis=-1)
            ncols = w if last else slab.shape[-1]
            rhs3 = jnp.concatenate([hi[:, :ncols], lo[:, :ncols], hi[:, :ncols]], axis=0)
            prod = jnp.dot(lhs, rhs3, preferred_element_type=F32)
            if last:
                nxt.append(slab[:, :w] + prod)
            else:
                nxt.append(jnp.concatenate([slab[:, :w] + prod[:, :w], prod[:, w:]], axis=-1))
        slabs = nxt
    return slabs


def _gdn_kernel(*refs, T, has_s0):
    if has_s0:
        (u_ref, cw_ref, alog_ref, dtb_ref, exp_ref, tril_ref, mask_ref, bd_ref, gn_ref, s0_ref,
         o_ref, sfin_ref, q_s, k_s, v_s, la_s, be_s, xw_s, at_s, qin_s, kt_s, al_s, of_s, st_s) = refs
    else:
        (u_ref, cw_ref, alog_ref, dtb_ref, exp_ref, tril_ref, mask_ref, bd_ref, gn_ref,
         o_ref, sfin_ref, q_s, k_s, v_s, la_s, be_s, xw_s, at_s, qin_s, kt_s, al_s, of_s, st_s) = refs
    n = T // CHUNK
    C = CHUNK
    bd = bd_ref[...]

    qkv = _silu(_conv3(u_ref[:, 0:3 * GROUP_W], cw_ref, T))
    q = qkv[:, 0:GROUP_W]
    k = qkv[:, GROUP_W:2 * GROUP_W]
    q_s[...] = q * lax.rsqrt(_sel_dot_right(q * q, bd) + 1e-6) * (HEAD_W ** -0.5)
    k_s[...] = k * lax.rsqrt(_sel_dot_right(k * k, bd) + 1e-6)
    v_s[...] = qkv[:, 2 * GROUP_W:3 * GROUP_W]

    ab = u_ref[:, 4 * GROUP_W:4 * GROUP_W + LANE]
    xa = ab + dtb_ref[...]
    softplus = jnp.maximum(xa, 0.0) + jnp.log(1.0 + jnp.exp(-jnp.abs(xa)))
    log_a = -jnp.exp(alog_ref[...]) * softplus
    lane = lax.broadcasted_iota(jnp.int32, ab.shape, 1)
    narrow = jnp.where(lane < 8, log_a, _sigmoid(ab))
    for d in range(2):
        wide = _dot_sel(narrow, exp_ref[d])
        la_s[d] = wide[:, 0:GROUP_W]
        be_s[d] = wide[:, GROUP_W:2 * GROUP_W]
        if has_s0:
            st_s[d] = jnp.concatenate([s0_ref[d, h] for h in range(N_HEADS)], axis=-1)
        else:
            st_s[d] = jnp.zeros((HEAD_W, GROUP_W), F32)

    def prepare(cidx):
        r0 = pl.multiple_of(cidx * C, C)
        rows = pl.ds(r0, C)
        arow = pl.ds(pl.multiple_of(cidx * 8, 8), 8)
        q = q_s[rows, :]
        k = k_s[rows, :]
        v = v_s[rows, :]
        systems = []
        attns = []
        kts = []
        for d in range(2):
            incl = mask_ref[d, 0] > 0.5
            strict = mask_ref[d, 1]
            la = la_s[d, rows, :]
            be = be_s[d, rows, :]
            gx = _sel_dot(tril_ref[d], la)
            gtot = jnp.sum(la, axis=0, keepdims=True)
            eg = jnp.exp(gx)
            kout = k * jnp.exp(gtot - gx)
            qin_s[d, rows, :] = q * eg
            al_s[d, arow, :] = jnp.broadcast_to(jnp.exp(gtot), (8, GROUP_W))
            kb = k * be
            vb = v * be
            kbg = kb * eg
            for h in range(N_HEADS):
                sl = slice(h * HEAD_W, (h + 1) * HEAD_W)
                gh = gx[:, sl]
                dmat = gh - gh.T
                dec = jnp.where(incl, jnp.exp(jnp.where(incl, dmat, 0.0)), 0.0)
                nmat = -(_bdot_nt(kb[:, sl], k[:, sl]) * dec * strict)
                systems.append((jnp.concatenate([vb[:, sl], kbg[:, sl]], axis=-1), nmat))
                attns.append(_bdot_nt(q[:, sl], k[:, sl]) * dec)
                kts.append(kout[:, sl].T)
        sols = _solve_unit_lower(systems)
        for d in range(2):
            for h in range(N_HEADS):
                xw_s[d, rows, h * LANE:(h + 1) * LANE] = sols[d * N_HEADS + h]
            at_s[d, rows, :] = jnp.concatenate(attns[d * N_HEADS:(d + 1) * N_HEADS], axis=-1)
            kt_s[d, rows, :] = jnp.concatenate(kts[d * N_HEADS:(d + 1) * N_HEADS], axis=-1)

    def prep_body(i, carry):
        for j in range(GDN_UNROLL):
            prepare(i * GDN_UNROLL + j)
        return carry

    lax.fori_loop(0, n // GDN_UNROLL, prep_body, 0)

    def chunk(ci, carry):
        heads = [(d, h) for d in range(2) for h in range(N_HEADS)]
        rows = []
        alast = []
        for d, cidx in ((0, ci), (1, n - 1 - ci)):
            rows.append(pl.ds(pl.multiple_of(cidx * C, C), C))
            alast.append(al_s[d, pl.ds(pl.multiple_of(cidx * 8, 8), 1), :])
        sts = [st_s[d][:, h * HEAD_W:(h + 1) * HEAD_W] for d, h in heads]
        xws = [xw_s[d, rows[d], h * LANE:(h + 1) * LANE] for d, h in heads]
        both = [_bdot(jnp.concatenate([xws[i][:, HEAD_W:], qin_s[d, rows[d], h * HEAD_W:(h + 1) * HEAD_W]], axis=0),
                      sts[i]) for i, (d, h) in enumerate(heads)]
        vnew = [xws[i][:, :HEAD_W] - both[i][:C] for i in range(len(heads))]
        upd = [_bdot(jnp.concatenate([at_s[d, rows[d], h * HEAD_W:(h + 1) * HEAD_W],
                                      kt_s[d, rows[d], h * HEAD_W:(h + 1) * HEAD_W]], axis=0), vnew[i])
               for i, (d, h) in enumerate(heads)]
        for d in range(2):
            idx = range(d * N_HEADS, (d + 1) * N_HEADS)
            of_s[d, rows[d], :] = jnp.concatenate([both[i][C:] + upd[i][:C] for i in idx], axis=-1)
            st_s[d] = jnp.concatenate(
                [sts[i] * alast[d][:, (i - d * N_HEADS) * HEAD_W:(i - d * N_HEADS + 1) * HEAD_W] + upd[i][C:]
                 for i in idx], axis=-1)
        return carry

    lax.fori_loop(0, n, chunk, 0)
    o_ref[...] = _head_norm_gate(of_s[0] + of_s[1], bd, gn_ref[...], u_ref[:, 3 * GROUP_W:4 * GROUP_W])
    for d in range(2):
        for h in range(N_HEADS):
            sfin_ref[d, h] = st_s[d][:, h * HEAD_W:(h + 1) * HEAD_W]


def _gdn_call(u_gd, cw, alog, dtb, gn, consts, s0, l, T, nb, row0):
    tril, masks, expand, bd = consts
    tb = row0 // T
    has_s0 = s0 is not None
    in_specs = [
        pl.BlockSpec((T, GD_PAD), lambda b: (tb + b, 0)),
        pl.BlockSpec((3, 3 * GROUP_W), lambda b: (0, 0)),
        pl.BlockSpec((1, LANE), lambda b: (0, 0)),
        pl.BlockSpec((1, LANE), lambda b: (0, 0)),
        pl.BlockSpec((2, LANE, 2 * GROUP_W), lambda b: (0, 0, 0)),
        pl.BlockSpec((2, CHUNK, CHUNK), lambda b: (0, 0, 0)),
        pl.BlockSpec((2, 2, CHUNK, CHUNK), lambda b: (0, 0, 0, 0)),
        pl.BlockSpec((GROUP_W, GROUP_W), lambda b: (0, 0)),
        pl.BlockSpec((1, GROUP_W), lambda b: (0, 0)),
    ]
    args = [u_gd, cw, alog, dtb, expand, tril, masks, bd, gn]
    if has_s0:
        in_specs.append(pl.BlockSpec((None, None, 2, N_HEADS, HEAD_W, HEAD_W), lambda b: (b, l, 0, 0, 0, 0)))
        args.append(s0)
    seq = pltpu.VMEM((2, T, GROUP_W), F32)
    return pl.pallas_call(
        functools.partial(_gdn_kernel, T=T, has_s0=has_s0),
        grid=(nb,),
        in_specs=in_specs,
        out_specs=[
            pl.BlockSpec((T, GROUP_W), lambda b: (b, 0)),
            pl.BlockSpec((None, 2, N_HEADS, HEAD_W, HEAD_W), lambda b: (b, 0, 0, 0, 0)),
        ],
        out_shape=[jax.ShapeDtypeStruct((nb * T, GROUP_W), F32),
                   jax.ShapeDtypeStruct((nb, 2, N_HEADS, HEAD_W, HEAD_W), F32)],
        scratch_shapes=[pltpu.VMEM((T, GROUP_W), F32)] * 3 + [seq, seq,
            pltpu.VMEM((2, T, N_HEADS * LANE), F32), seq, seq, seq,
            pltpu.VMEM((2, T // CHUNK * 8, GROUP_W), F32), seq, pltpu.VMEM((2, HEAD_W, GROUP_W), F32)],
        compiler_params=_cparams(("parallel",)),
        name="gdn",
    )(*args)


def _hyfilt_kernel(z_ref, win_ref, fh_ref, fl_ref, w1_ref, b1_ref, fr_ref, w2_ref, b2_ref, w3_ref, o_ref, *, T):
    fr = fr_ref[...]
    h = jnp.sin(fr * (_dot3(z_ref[...], w1_ref[...]) + b1_ref[...]))
    h = jnp.sin(fr * (_dot3(h, w2_ref[...]) + b2_ref[...]))
    h = _dot3(h, w3_ref[...])
    win = win_ref[...]
    hf = h[:, 0:GROUP_W] * win
    hb = h[:, GROUP_W:2 * GROUP_W] * win
    row = lax.broadcasted_iota(jnp.int32, hb.shape, 0)
    hb = jnp.where(row == 0, 0.0, hb)
    sh, sl = _split2(jnp.concatenate([hf + hb, hf - hb], axis=-1))
    fh = fh_ref[...]
    spec = (jnp.dot(fh, sh, preferred_element_type=F32) + jnp.dot(fh, sl, preferred_element_type=F32)
            + jnp.dot(fl_ref[...], sh, preferred_element_type=F32))
    o_ref[0:T, :] = spec[0:T, 0:GROUP_W]
    o_ref[T:2 * T, :] = spec[T:2 * T, GROUP_W:2 * GROUP_W]


def _hyfilt_call(T, zp, win, fh, fl, w1p, b1, freq, w2, b2, w3):
    c2 = lambda l: (0, 0)
    return pl.pallas_call(
        functools.partial(_hyfilt_kernel, T=T),
        grid=(DEPTH,),
        in_specs=[
            pl.BlockSpec((T, LANE), c2),
            pl.BlockSpec((T, GROUP_W), c2),
            pl.BlockSpec((2 * T, T), c2),
            pl.BlockSpec((2 * T, T), c2),
            pl.BlockSpec((None, LANE, HY_FH), lambda l: (l, 0, 0)),
            pl.BlockSpec((None, 1, HY_FH), lambda l: (l, 0, 0)),
            pl.BlockSpec((None, 1, HY_FH), lambda l: (l, 0, 0)),
            pl.BlockSpec((None, HY_FH, HY_FH), lambda l: (l, 0, 0)),
            pl.BlockSpec((None, 1, HY_FH), lambda l: (l, 0, 0)),
            pl.BlockSpec((None, HY_FH, 2 * GROUP_W), lambda l: (l, 0, 0)),
        ],
        out_specs=pl.BlockSpec((None, 2 * T, GROUP_W), lambda l: (l, 0, 0)),
        out_shape=jax.ShapeDtypeStruct((DEPTH, 2 * T, GROUP_W), F32),
        compiler_params=_cparams(("parallel",)),
        name="hyfilt",
    )(zp, win, fh, fl, w1p, b1, freq, w2, b2, w3)


def _hyena_kernel(u_ref, cw_ref, cb_ref, spec_ref, skip_ref, fh_ref, fl_ref, ih_ref, il_ref, o_ref, *, T):
    uc = _conv3(u_ref[...], cw_ref, T) + cb_ref[...]
    x0 = uc[:, 0:GROUP_W]
    z = uc[:, GROUP_W:2 * GROUP_W] * uc[:, 2 * GROUP_W:3 * GROUP_W]
    zh, zl = _split2(z)
    fh = fh_ref[...]
    zs = (jnp.dot(fh, zh, preferred_element_type=F32) + jnp.dot(fh, zl, preferred_element_type=F32)
          + jnp.dot(fl_ref[...], zh, preferred_element_type=F32))
    ar, ai = zs[0:T], zs[T:2 * T]
    br, bi = spec_ref[0:T, :], spec_ref[T:2 * T, :]
    ph, plo = _split2(jnp.concatenate([ar * br - ai * bi, ar * bi + ai * br], axis=0))
    ih = ih_ref[...]
    y = (jnp.dot(ih, ph, preferred_element_type=F32) + jnp.dot(ih, plo, preferred_element_type=F32)
         + jnp.dot(il_ref[...], ph, preferred_element_type=F32))
    o_ref[...] = x0 * (y + z * skip_ref[...])


def _hyena_call(u_hy, cw, cb, spec, skip, dft, l, T, nb, row0):
    fh, fl, ih, il = dft
    tb = row0 // T
    c2 = lambda b: (0, 0)
    return pl.pallas_call(
        functools.partial(_hyena_kernel, T=T),
        grid=(nb,),
        in_specs=[
            pl.BlockSpec((T, HY_COLS), lambda b: (tb + b, 0)),
            pl.BlockSpec((3, HY_COLS), c2),
            pl.BlockSpec((1, HY_COLS), c2),
            pl.BlockSpec((None, 2 * T, GROUP_W), lambda b: (l, 0, 0)),
            pl.BlockSpec((1, GROUP_W), c2),
            pl.BlockSpec((2 * T, T), c2),
            pl.BlockSpec((2 * T, T), c2),
            pl.BlockSpec((T, 2 * T), c2),
            pl.BlockSpec((T, 2 * T), c2),
        ],
        out_specs=pl.BlockSpec((T, GROUP_W), lambda b: (b, 0)),
        out_shape=jax.ShapeDtypeStruct((nb * T, GROUP_W), F32),
        compiler_params=_cparams(("parallel",)),
        name="hyena",
    )(u_hy, cw, cb, spec, skip, fh, fl, ih, il)


def _rope(x, cosf, sinf):
    lane = lax.broadcasted_iota(jnp.int32, x.shape, 1)
    half = MLA_ROPE // 2
    partner = jnp.where(lane < MLA_NOPE + half, pltpu.roll(x, LANE - half, 1), pltpu.roll(x, half, 1))
    return x * cosf + partner * sinf


def _qk_norm(x, g):
    ms = jnp.sum(x * x, axis=-1, keepdims=True) * (1.0 / MLA_QK)
    return x * lax.rsqrt(ms + RMS_EPS) * g


def _mla_kernel(*refs, T, ctx):
    if ctx:
        (u_ref, qn_ref, wq_ref, kvn_ref, wkv_ref, qkn_ref, cos_ref, sin_ref, cckv_ref, ckr_ref, o_ref) = refs
    else:
        (u_ref, qn_ref, wq_ref, kvn_ref, wkv_ref, qkn_ref, o_ref, ckv_ref, kr_ref) = refs
    u = u_ref[...]
    cq = _rms(u[:, 0:MLA_Q_LORA], qn_ref[...])
    ckv = _rms(u[:, MLA_Q_LORA:MLA_Q_LORA + MLA_KV_LORA], kvn_ref[...])
    kr = u[:, MLA_Q_LORA + MLA_KV_LORA:MLA_Q_LORA + MLA_KV_LORA + MLA_ROPE]
    if not ctx:
        ckv_ref[...] = ckv
        kr_ref[...] = kr
    q_all = _bdot(cq, wq_ref[...])
    kv = _bdot(ckv, wkv_ref[...])
    gq = qkn_ref[0:1, :]
    gk = qkn_ref[1:2, :]
    if ctx:
        kvc = _bdot(cckv_ref[...], wkv_ref[...])
        krc = ckr_ref[...]
        cosf, sinf = cos_ref[...], sin_ref[...]
    scale = MLA_QK ** -0.5
    outs = []
    for h in range(N_HEADS):
        qh = _qk_norm(q_all[:, h * LANE:(h + 1) * LANE], gq)
        zpad = jnp.zeros((T, LANE - MLA_QK), F32)
        kh = _qk_norm(jnp.concatenate([kv[:, h * HEAD_W:(h + 1) * HEAD_W], kr, zpad], axis=-1), gk)
        vh = kv[:, GROUP_W + h * HEAD_W:GROUP_W + (h + 1) * HEAD_W]
        if ctx:
            qh = _rope(qh, cosf, sinf)
            kh = _rope(kh, cosf, sinf)
            s_len = krc.shape[0]
            zc = jnp.zeros((s_len, LANE - MLA_QK), F32)
            kc = _qk_norm(jnp.concatenate([kvc[:, h * HEAD_W:(h + 1) * HEAD_W], krc, zc], axis=-1), gk)
            kh = jnp.concatenate([kh, kc], axis=0)
            vh = jnp.concatenate([vh, kvc[:, GROUP_W + h * HEAD_W:GROUP_W + (h + 1) * HEAD_W]], axis=0)
        khb = kh.astype(BF16)
        vhb = vh.astype(BF16)
        blocks = []
        for qb in range(T // ATT_QBLOCK):
            s = _bdot_nt(qh[qb * ATT_QBLOCK:(qb + 1) * ATT_QBLOCK], khb) * scale
            e = jnp.exp(s - jnp.max(s, axis=-1, keepdims=True))
            blocks.append(_bdot(e, vhb) / jnp.sum(e, axis=-1, keepdims=True))
        outs.append(blocks[0] if len(blocks) == 1 else jnp.concatenate(blocks, axis=0))
    o_ref[...] = jnp.concatenate(outs, axis=-1)


def _mla_call(u_mla, qn, wq, kvn, wkv, qkn, rope, cache, l, T, nb, row0):
    tb = row0 // T
    ctx = cache is not None
    c2 = lambda b: (0, 0)
    in_specs = [
        pl.BlockSpec((T, MLA_PAD), lambda b: (tb + b, 0)),
        pl.BlockSpec((1, MLA_Q_LORA), c2),
        pl.BlockSpec((MLA_Q_LORA, N_HEADS * LANE), c2),
        pl.BlockSpec((1, MLA_KV_LORA), c2),
        pl.BlockSpec((MLA_KV_LORA, 2 * GROUP_W), c2),
        pl.BlockSpec((2, LANE), c2),
    ]
    args = [u_mla, qn, wq, kvn, wkv, qkn]
    out_specs = [pl.BlockSpec((T, GROUP_W), lambda b: (b, 0))]
    out_shape = [jax.ShapeDtypeStruct((nb * T, GROUP_W), F32)]
    if ctx:
        in_specs += [
            pl.BlockSpec((T, LANE), c2),
            pl.BlockSpec((T, LANE), c2),
            pl.BlockSpec((None, None, PAST_LEN, MLA_KV_LORA), lambda b: (b, l, 0, 0)),
            pl.BlockSpec((None, None, PAST_LEN, MLA_ROPE), lambda b: (b, l, 0, 0)),
        ]
        args += [rope[0], rope[1], cache[0], cache[1]]
    else:
        out_specs += [pl.BlockSpec((T, MLA_KV_LORA), lambda b: (b, 0)),
                      pl.BlockSpec((T, MLA_ROPE), lambda b: (b, 0))]
        out_shape += [jax.ShapeDtypeStruct((nb * T, MLA_KV_LORA), F32),
                      jax.ShapeDtypeStruct((nb * T, MLA_ROPE), F32)]
    return pl.pallas_call(
        functools.partial(_mla_kernel, T=T, ctx=ctx),
        grid=(nb,),
        in_specs=in_specs,
        out_specs=out_specs,
        out_shape=out_shape,
        compiler_params=_cparams(("parallel",)),
        name="mla",
    )(*args)


def _pad_cols(w, width):
    return jnp.pad(w, [(0, 0)] * (w.ndim - 1) + [(0, width - w.shape[-1])])


def _prep_w_in(w_in):
    o1 = HG_COLS
    o2 = o1 + HY_COLS
    o3 = o2 + MLA_COLS
    return (w_in[..., :o1].astype(BF16), w_in[..., o1:o2].astype(BF16),
            _pad_cols(w_in[..., o2:o3], MLA_PAD).astype(BF16), _pad_cols(w_in[..., o3:], GD_PAD).astype(BF16))


def _prep_wq(w_q_up):
    w = w_q_up.reshape(DEPTH, MLA_Q_LORA, N_HEADS, MLA_QK)
    return _pad_cols(w, LANE).reshape(DEPTH, MLA_Q_LORA, N_HEADS * LANE).astype(BF16)


def _prep_wkv(w_kv_up):
    w = w_kv_up.reshape(DEPTH, MLA_KV_LORA, N_HEADS, 2, HEAD_W)
    return w.transpose(0, 1, 3, 2, 4).reshape(DEPTH, MLA_KV_LORA, 2 * GROUP_W).astype(BF16)


def _lower_bounds(hgrn_lb):
    lb = jnp.cumsum(jax.nn.softmax(hgrn_lb.astype(F32), axis=0), axis=0)
    return lb - lb[0]


def kernel(x_prompt, x_sample, cache_mla_ckv, cache_mla_krope, state_hgrn, state_gdn, c, c_ctx, w_ada, b_ada, norm_ffn, w_ffn_gu, w_ffn_down, norm_mix, w_in, w_out, hgrn_lb, hgrn_norm, hy_conv_w, hy_conv_b, hy_w1, hy_b1, hy_freq, hy_w2, hy_b2, hy_w3, hy_skip, mla_q_norm_a, mla_w_q_up, mla_kv_norm_a, mla_w_kv_up, mla_qk_norm, gdn_conv_w, gdn_a_log, gdn_dt_bias, gdn_norm):
    x = jnp.concatenate([x_prompt.reshape(N_PROMPT, D_MODEL), x_sample.reshape(N_SAMPLE, D_MODEL)], axis=0)

    cond8 = jnp.zeros((8, D_MODEL), F32).at[0].set(c_ctx).at[1:1 + DEC_BATCH].set(c)
    ada = _ada_call(cond8, w_ada, b_ada)

    w_in_parts = _prep_w_in(w_in)
    w_out_bf = w_out.astype(BF16)
    wq = _prep_wq(mla_w_q_up)
    wkv = _prep_wkv(mla_w_kv_up)
    qkn = _pad_cols(mla_qk_norm, LANE)
    lb_all = _lower_bounds(hgrn_lb)
    alog = _pad_cols(gdn_a_log.reshape(DEPTH, 1, 8), LANE)
    dtb = _pad_cols(gdn_dt_bias.reshape(DEPTH, 1, 8), LANE)
    gdn_gn = jnp.tile(gdn_norm, (1, N_HEADS)).reshape(DEPTH, 1, GROUP_W)
    w1p = jnp.pad(hy_w1, ((0, 0), (0, LANE - HY_EMB), (0, 0)))

    bd = jnp.asarray(_block_diag_ones(), BF16)
    hg_a, hg_m = _hgrn_consts()
    hg_consts = (jnp.asarray(hg_a, BF16), jnp.asarray(hg_m, F32), bd)
    gd_tril, gd_masks, gd_expand = _gdn_consts()
    gd_consts = (jnp.asarray(gd_tril, BF16), jnp.asarray(gd_masks, F32), jnp.asarray(gd_expand, BF16), bd)
    rope = tuple(jnp.asarray(a) for a in _rope_consts(DEC_SEQ))
    groups = ((SEQ, BATCH, 0), (DEC_SEQ, DEC_BATCH, N_PROMPT))
    dft = {}
    spec = {}
    for T, _, _ in groups:
        fwd, inv = _dft_consts(T)
        fh, fl = _np_split2(fwd)
        ih, il = _np_split2(inv)
        dft[T] = (fh, fl, ih, il)
        zp, win = _hyena_pos_consts(T)
        spec[T] = _hyfilt_call(T, jnp.asarray(zp), jnp.asarray(win), fh, fl, w1p,
                               hy_b1.reshape(DEPTH, 1, HY_FH), hy_freq.reshape(DEPTH, 1, HY_FH), hy_w2,
                               hy_b2.reshape(DEPTH, 1, HY_FH), hy_w3)

    new_ckv, new_kr, new_hg, new_gd = [], [], [], []
    for l in range(DEPTH):
        x = _ffn_call(x, ada, norm_ffn, w_ffn_gu, w_ffn_down, l, 0)
        u_hg, u_hy, u_mla, u_gd = _inproj_call(x, ada, norm_mix, w_in_parts, l)
        outs = []
        for gi, (T, nb, row0) in enumerate(groups):
            latent = gi == 1
            o_hg, s_hg = _hgrn_call(u_hg, lb_all[l], hgrn_norm[l], hg_consts,
                                    state_hgrn if latent else None, l, T, nb, row0)
            o_hy = _hyena_call(u_hy, hy_conv_w[l], hy_conv_b[l].reshape(1, HY_COLS), spec[T],
                               hy_skip[l].reshape(1, GROUP_W), dft[T], l, T, nb, row0)
            mla = _mla_call(u_mla, mla_q_norm_a[l].reshape(1, MLA_Q_LORA), wq[l],
                            mla_kv_norm_a[l].reshape(1, MLA_KV_LORA), wkv[l], qkn[l],
                            rope if latent else None,
                            (cache_mla_ckv, cache_mla_krope) if latent else None, l, T, nb, row0)
            o_gd, s_gd = _gdn_call(u_gd, gdn_conv_w[l], alog[l], dtb[l], gdn_gn[l], gd_consts,
                                   state_gdn if latent else None, l, T, nb, row0)
            outs.append((o_hg, o_hy, mla[0], o_gd))
            if not latent:
                new_ckv.append(mla[1].reshape(BATCH, SEQ, MLA_KV_LORA))
                new_kr.append(mla[2].reshape(BATCH, SEQ, MLA_ROPE))
                new_hg.append(s_hg)
                new_gd.append(s_gd)
        x = _outproj_call(x, ada, w_out_bf, outs[0], outs[1], l)
        x = _ffn_call(x, ada, norm_ffn, w_ffn_gu, w_ffn_down, l, 1)

    y_prompt = x[:N_PROMPT].reshape(BATCH, SEQ, D_MODEL)
    y_sample = x[N_PROMPT:].reshape(DEC_BATCH, DEC_SEQ, D_MODEL)
    return (y_prompt, y_sample, jnp.stack(new_ckv, axis=1), jnp.stack(new_kr, axis=1),
            jnp.stack(new_hg, axis=1), jnp.stack(new_gd, axis=1))
```

```python
import functools
import math

import numpy as np
import jax
import jax.numpy as jnp
from jax import lax
from jax.experimental import pallas as pl
from jax.experimental.pallas import tpu as pltpu

F32 = jnp.float32
BF16 = jnp.bfloat16

D_MODEL = 1024
BATCH = 16
SEQ = 256
DEPTH = 4
DEC_BATCH = 2
DEC_SEQ = 1024
PAST_LEN = 256
GRID_W = 64
N_ADA = 9
D_FF = 2816
GROUP_W = 256
CHUNK = 64
RMS_EPS = 1e-6
N_HEADS = 4
HEAD_W = 64
HY_EMB = 33
HY_FH = 64
HY_TARGET = 1e-2
HY_FAST = 0.3
HY_SLOW = 1.5
MLA_NOPE = 64
MLA_ROPE = 32
MLA_QK = MLA_NOPE + MLA_ROPE
MLA_Q_LORA = 256
MLA_KV_LORA = 128
ROPE_BASE = 10000.0

HG_COLS = 5 * GROUP_W
HY_COLS = 3 * GROUP_W
MLA_COLS = MLA_Q_LORA + MLA_KV_LORA + MLA_ROPE
GD_COLS = 4 * GROUP_W + 16
MLA_PAD = 512
GD_PAD = 1152

N_PROMPT = BATCH * SEQ
N_SAMPLE = DEC_BATCH * DEC_SEQ
N_TOK = N_PROMPT + N_SAMPLE
LANE = 128
VMEM_LIMIT = 56 * 1024 * 1024
ROW_TILE = 1024
FF_TILE = 256
ADA_TILE = 1536
ATT_QBLOCK = 256


def _bdot(a, b):
    return jnp.dot(a.astype(BF16), b.astype(BF16), preferred_element_type=F32)


def _bdot_nt(a, b):
    return lax.dot_general(a.astype(BF16), b.astype(BF16), (((1,), (1,)), ((), ())),
                           preferred_element_type=F32)


def _bdot_tn(a, b):
    return lax.dot_general(a.astype(BF16), b.astype(BF16), (((0,), (0,)), ((), ())),
                           preferred_element_type=F32)


def _split2(x):
    hi = x.astype(BF16)
    lo = (x - hi.astype(F32)).astype(BF16)
    return hi, lo


def _split3(x):
    hi = x.astype(BF16)
    r = x - hi.astype(F32)
    mid = r.astype(BF16)
    lo = (r - mid.astype(F32)).astype(BF16)
    return hi, mid, lo


def _dot3(a, b):
    ah, al = _split2(a)
    bh, bl = _split2(b)
    return (jnp.dot(ah, bh, preferred_element_type=F32) + jnp.dot(ah, bl, preferred_element_type=F32)
            + jnp.dot(al, bh, preferred_element_type=F32))


def _sel_dot(c, x):
    h, m, l = _split3(x)
    return (jnp.dot(c, h, preferred_element_type=F32) + jnp.dot(c, m, preferred_element_type=F32)
            + jnp.dot(c, l, preferred_element_type=F32))


def _dot_sel(x, c):
    h, m, l = _split3(x)
    return (jnp.dot(h, c, preferred_element_type=F32) + jnp.dot(m, c, preferred_element_type=F32)
            + jnp.dot(l, c, preferred_element_type=F32))


def _sigmoid(x):
    return 1.0 / (1.0 + jnp.exp(-x))


def _silu(x):
    return x * _sigmoid(x)


def _rms(x, g):
    return x * lax.rsqrt(jnp.mean(x * x, axis=-1, keepdims=True) + RMS_EPS) * g


def _cparams(sem):
    return pltpu.CompilerParams(dimension_semantics=sem, vmem_limit_bytes=VMEM_LIMIT)


def _cond_of_tile(i):
    return jnp.maximum(i - (N_PROMPT // ROW_TILE - 1), 0)


def _ada_kernel(c_ref, w_ref, b_ref, o_ref):
    o_ref[...] = _dot3(_silu(c_ref[...]), w_ref[...]) + b_ref[...]


def _ada_call(cond8, w_ada, b_ada):
    n = N_ADA * D_MODEL
    out = pl.pallas_call(
        _ada_kernel,
        grid=(DEPTH, n // ADA_TILE),
        in_specs=[
            pl.BlockSpec((8, D_MODEL), lambda l, j: (0, 0)),
            pl.BlockSpec((None, D_MODEL, ADA_TILE), lambda l, j: (l, 0, j)),
            pl.BlockSpec((None, 1, ADA_TILE), lambda l, j: (l, 0, j)),
        ],
        out_specs=pl.BlockSpec((None, 8, ADA_TILE), lambda l, j: (l, 0, j)),
        out_shape=jax.ShapeDtypeStruct((DEPTH, 8, n), F32),
        compiler_params=_cparams(("parallel", "parallel")),
        name="ada",
    )(cond8, w_ada, b_ada.reshape(DEPTH, 1, n))
    return out.reshape(DEPTH, 8, N_ADA, D_MODEL)


def _ffn_kernel(x_ref, ada_ref, g_ref, wg_ref, wu_ref, wd_ref, o_ref, h_scr, acc_scr, *, sub):
    f = pl.program_id(1)

    @pl.when(f == 0)
    def _():
        y = _rms(x_ref[...], g_ref[...])
        h = y * (1.0 + ada_ref[3 * sub + 1:3 * sub + 2, :]) + ada_ref[3 * sub:3 * sub + 1, :]
        h_scr[...] = h.astype(BF16)
        acc_scr[...] = jnp.zeros_like(acc_scr)

    h = h_scr[...]
    gate = jnp.dot(h, wg_ref[...].astype(BF16), preferred_element_type=F32)
    up = jnp.dot(h, wu_ref[...].astype(BF16), preferred_element_type=F32)
    a = (_silu(gate) * up).astype(BF16)
    acc_scr[...] += jnp.dot(a, wd_ref[...].astype(BF16), preferred_element_type=F32)

    @pl.when(f == pl.num_programs(1) - 1)
    def _():
        o_ref[...] = x_ref[...] + 0.5 * ada_ref[3 * sub + 2:3 * sub + 3, :] * acc_scr[...]


def _ffn_call(x, ada, norm_ffn, w_gu, w_down, l, j):
    sub = 2 * j
    nf = D_FF // FF_TILE
    return pl.pallas_call(
        functools.partial(_ffn_kernel, sub=sub),
        grid=(N_TOK // ROW_TILE, nf),
        in_specs=[
            pl.BlockSpec((ROW_TILE, D_MODEL), lambda i, f: (i, 0)),
            pl.BlockSpec((None, None, N_ADA, D_MODEL), lambda i, f: (l, _cond_of_tile(i), 0, 0)),
            pl.BlockSpec((None, None, 1, D_MODEL), lambda i, f: (l, j, 0, 0)),
            pl.BlockSpec((None, None, D_MODEL, FF_TILE), lambda i, f: (l, j, 0, f)),
            pl.BlockSpec((None, None, D_MODEL, FF_TILE), lambda i, f: (l, j, 0, nf + f)),
            pl.BlockSpec((None, None, FF_TILE, D_MODEL), lambda i, f: (l, j, f, 0)),
        ],
        out_specs=pl.BlockSpec((ROW_TILE, D_MODEL), lambda i, f: (i, 0)),
        out_shape=jax.ShapeDtypeStruct((N_TOK, D_MODEL), F32),
        scratch_shapes=[pltpu.VMEM((ROW_TILE, D_MODEL), BF16), pltpu.VMEM((ROW_TILE, D_MODEL), F32)],
        compiler_params=_cparams(("parallel", "arbitrary")),
        name="ffn",
    )(x, ada, norm_ffn.reshape(DEPTH, 2, 1, D_MODEL), w_gu, w_gu, w_down)


IN_TILE = 512


def _inproj_kernel(x_ref, ada_ref, g_ref, w1, w2, w3, w4, o1, o2, o3, o4):
    y = _rms(x_ref[...], g_ref[...])
    h = (y * (1.0 + ada_ref[4:5, :]) + ada_ref[3:4, :]).astype(BF16)
    for w, o in ((w1, o1), (w2, o2), (w3, o3), (w4, o4)):
        o[...] = jnp.dot(h, w[...], preferred_element_type=F32)


def _inproj_call(x, ada, norm_mix, ws, l):
    widths = (HG_COLS, HY_COLS, MLA_PAD, GD_PAD)
    per = ROW_TILE // IN_TILE
    return pl.pallas_call(
        _inproj_kernel,
        grid=(N_TOK // IN_TILE,),
        in_specs=[
            pl.BlockSpec((IN_TILE, D_MODEL), lambda i: (i, 0)),
            pl.BlockSpec((None, None, N_ADA, D_MODEL), lambda i: (l, _cond_of_tile(i // per), 0, 0)),
            pl.BlockSpec((None, 1, D_MODEL), lambda i: (l, 0, 0)),
        ] + [pl.BlockSpec((None, D_MODEL, w), lambda i: (l, 0, 0)) for w in widths],
        out_specs=[pl.BlockSpec((IN_TILE, w), lambda i: (i, 0)) for w in widths],
        out_shape=[jax.ShapeDtypeStruct((N_TOK, w), F32) for w in widths],
        compiler_params=_cparams(("parallel",)),
        name="inproj",
    )(x, ada, norm_mix.reshape(DEPTH, 1, D_MODEL), *ws)


OUT_TILE = 512


def _outproj_kernel(x_ref, ada_ref, w_ref, *refs):
    o_ref = refs[-1]
    i = pl.program_id(0)
    n_p = N_PROMPT // OUT_TILE

    def run(srcs):
        acc = jnp.zeros((OUT_TILE, D_MODEL), F32)
        for g, s in enumerate(srcs):
            acc += jnp.dot(s[...].astype(BF16), w_ref[g * GROUP_W:(g + 1) * GROUP_W, :],
                           preferred_element_type=F32)
        o_ref[...] = x_ref[...] + ada_ref[5:6, :] * acc

    @pl.when(i < n_p)
    def _():
        run(refs[0:4])

    @pl.when(i >= n_p)
    def _():
        run(refs[4:8])


def _outproj_call(x, ada, w_out_bf, o_p, o_s, l):
    per = ROW_TILE // OUT_TILE
    n_p = N_PROMPT // OUT_TILE
    n_s = N_SAMPLE // OUT_TILE
    return pl.pallas_call(
        _outproj_kernel,
        grid=(N_TOK // OUT_TILE,),
        in_specs=[
            pl.BlockSpec((OUT_TILE, D_MODEL), lambda i: (i, 0)),
            pl.BlockSpec((None, None, N_ADA, D_MODEL), lambda i: (l, _cond_of_tile(i // per), 0, 0)),
            pl.BlockSpec((None, D_MODEL, D_MODEL), lambda i: (l, 0, 0)),
        ] + [pl.BlockSpec((OUT_TILE, GROUP_W), lambda i: (jnp.minimum(i, n_p - 1), 0))] * 4
          + [pl.BlockSpec((OUT_TILE, GROUP_W), lambda i: (jnp.clip(i - n_p, 0, n_s - 1), 0))] * 4,
        out_specs=pl.BlockSpec((OUT_TILE, D_MODEL), lambda i: (i, 0)),
        out_shape=jax.ShapeDtypeStruct((N_TOK, D_MODEL), F32),
        compiler_params=_cparams(("parallel",)),
        name="outproj",
    )(x, ada, w_out_bf, *o_p, *o_s)


def _block_diag_ones():
    idx = np.arange(GROUP_W) // HEAD_W
    return (idx[:, None] == idx[None, :]).astype(np.float32)


def _hgrn_consts():
    C = CHUNK
    i = np.arange(C)[:, None]
    j = np.arange(C)[None, :]
    masks = []
    s = C // 2
    while s >= 1:
        up_i = (i // s) % 2 == 1
        up_j = (j // s) % 2 == 1
        masks.append(up_i & (~up_j) & (i // (2 * s) == j // (2 * s)))
        s //= 2
    fwd_m = np.stack([m.astype(np.float32) for m in masks])
    bwd_m = np.stack([m.astype(np.float32)[::-1, ::-1] for m in masks])
    tril = np.stack([(j <= i), (j >= i)]).astype(np.float32)
    return tril, np.stack([fwd_m, bwd_m])


def _gdn_consts():
    C = CHUNK
    i = np.arange(C)[:, None]
    t = np.arange(C)[None, :]
    tril = np.stack([(t <= i), (t >= i)]).astype(np.float32)
    masks = np.stack([np.stack([(t <= i), (t < i)]), np.stack([(t >= i), (t > i)])]).astype(np.float32)
    expand = np.zeros((2, LANE, 2 * GROUP_W), np.float32)
    for d in range(2):
        for h in range(N_HEADS):
            expand[d, d * N_HEADS + h, h * HEAD_W:(h + 1) * HEAD_W] = 1.0
            expand[d, 8 + d * N_HEADS + h, GROUP_W + h * HEAD_W:GROUP_W + (h + 1) * HEAD_W] = 1.0
    return tril, masks, expand


def _dft_consts(T):
    n2 = 4 * T
    k = np.arange(T, dtype=np.int64)[:, None]
    s = np.arange(T, dtype=np.int64)[None, :]
    ang = np.pi * (((2 * k + 1) * s) % n2).astype(np.float64) / (2 * T)
    fwd = np.concatenate([np.cos(ang), -np.sin(ang)], axis=0)
    inv = fwd.T / T
    return fwd.astype(np.float32), inv.astype(np.float32)


def _np_split2(x):
    hi = jnp.asarray(x, F32).astype(BF16)
    lo = (jnp.asarray(x, F32) - hi.astype(F32)).astype(BF16)
    return hi, lo


def _hyena_pos_consts(T):
    pos = np.arange(T, dtype=np.float32)
    t = pos / np.float32(T - 1)
    bands = np.linspace(1e-4, (HY_EMB - 1) // 2 - 1, (HY_EMB - 1) // 2, dtype=np.float32)
    ang = (np.float32(2.0 * math.pi / T) * pos[:, None]) * bands[None, :]
    z = np.concatenate([t[:, None], np.cos(ang), -np.sin(ang)], axis=-1).astype(np.float32)
    zp = np.zeros((T, LANE), np.float32)
    zp[:, :HY_EMB] = z
    max_decay = math.log(HY_TARGET) / HY_FAST
    min_decay = math.log(HY_TARGET) / HY_SLOW
    deltas = np.linspace(min_decay, max_decay, GROUP_W, dtype=np.float32)
    window = np.exp(-t[:, None] * np.abs(deltas)[None, :]).astype(np.float32)
    return zp, window


def _rope_consts(T):
    rows = T // GRID_W
    row = np.repeat(np.arange(rows, dtype=np.float32), GRID_W)
    col = (np.arange(T) % GRID_W).astype(np.float32)
    pairs = MLA_ROPE // 4
    inv = (np.float32(ROPE_BASE) ** (-np.arange(pairs, dtype=np.float32) / np.float32(pairs))).astype(np.float32)
    ang = np.concatenate([row[:, None] * inv, col[:, None] * inv], axis=-1).astype(np.float32)
    cos, sin = np.cos(ang), np.sin(ang)
    cosf = np.ones((T, LANE), np.float32)
    sinf = np.zeros((T, LANE), np.float32)
    half = MLA_ROPE // 2
    cosf[:, MLA_NOPE:MLA_NOPE + half] = cos
    cosf[:, MLA_NOPE + half:MLA_QK] = cos
    sinf[:, MLA_NOPE:MLA_NOPE + half] = -sin
    sinf[:, MLA_NOPE + half:MLA_QK] = sin
    return cosf, sinf


def _head_norm_gate(tot, bd, gn, gate):
    ms = _sel_dot_right(tot * tot, bd) * (1.0 / HEAD_W)
    return tot * lax.rsqrt(ms + RMS_EPS) * gn * _silu(gate)


def _sel_dot_right(x, c):
    h, l = _split2(x)
    return jnp.dot(h, c, preferred_element_type=F32) + jnp.dot(l, c, preferred_element_type=F32)


def _block_ref(b, two_s, r):
    C, W = b.shape
    if two_s % 8 == 0:
        b3 = b.reshape(C // two_s, two_s, W)
        return jnp.broadcast_to(b3[:, r:r + 1, :], b3.shape).reshape(C, W)
    pos = lax.broadcasted_iota(jnp.int32, b.shape, 0) % two_s
    out = b
    for p in range(two_s):
        if p != r:
            out = jnp.where(pos == p, pltpu.roll(b, (p - r) % C, 0), out)
    return out


def _hgrn_kernel(*refs, T, has_s0):
    if has_s0:
        (u_ref, lb_ref, gn_ref, tril_ref, lmask_ref, bd_ref, s0_ref,
         o_ref, sfin_ref, lf_s, kk_s, oi_s, qin_s, up_s, dc_s, st_s) = refs
    else:
        (u_ref, lb_ref, gn_ref, tril_ref, lmask_ref, bd_ref,
         o_ref, sfin_ref, lf_s, kk_s, oi_s, qin_s, up_s, dc_s, st_s) = refs
    n = T // CHUNK
    C = CHUNK
    bd = bd_ref[...]
    n_lv = int(math.log2(C))
    heads = [(d, h) for d in range(2) for h in range(N_HEADS)]

    for d in range(2):
        z = u_ref[:, (3 + d) * GROUP_W:(4 + d) * GROUP_W]
        lb = lb_ref[d]
        log_sig = jnp.minimum(z, 0.0) - jnp.log(1.0 + jnp.exp(-jnp.abs(z)))
        a = jnp.log(lb)
        c = jnp.log(1.0 - lb) + log_sig
        m = jnp.maximum(a, c)
        lf_s[d] = m + jnp.log(1.0 + jnp.exp(jnp.minimum(a, c) - m))
        kk_s[d] = (1.0 - lb) * _sigmoid(-z)
        if has_s0:
            st_s[d] = jnp.concatenate([s0_ref[d, h].T for h in range(N_HEADS)], axis=-1)
        else:
            st_s[d] = jnp.zeros((HEAD_W, GROUP_W), F32)

    def prepare(cidx, carry):
        rows = pl.ds(pl.multiple_of(cidx * C, C), C)
        arow = pl.ds(pl.multiple_of(cidx * 8, 8), 8)
        q = u_ref[rows, 0:GROUP_W] * (HEAD_W ** -0.5)
        v = u_ref[rows, GROUP_W:2 * GROUP_W]
        vt = [v[:, h * HEAD_W:(h + 1) * HEAD_W].T for h in range(N_HEADS)]
        ql, kl, diag = [], [], []
        for d in range(2):
            lf = lf_s[d, rows, :]
            k = kk_s[d, rows, :]
            hi, lo = _split2(lf)
            tril = tril_ref[d]
            b = jnp.dot(tril, hi, preferred_element_type=F32) + jnp.dot(tril, lo, preferred_element_type=F32)
            tot = jnp.sum(lf, axis=0, keepdims=True)
            qin_s[d, rows, :] = q * jnp.exp(b)
            ko = k * jnp.exp(tot - b)
            up_s[d, rows, :] = jnp.concatenate(
                [_bdot(vt[h], ko[:, h * HEAD_W:(h + 1) * HEAD_W]) for h in range(N_HEADS)], axis=-1)
            dc_s[d, arow, :] = jnp.broadcast_to(jnp.exp(tot), (8, GROUP_W))
            s = C // 2
            while s >= 1:
                e = jnp.exp(-jnp.abs(b - _block_ref(b, 2 * s, s - 1 if d == 0 else s)))
                ql.append(q * e)
                kl.append(k * e)
                s //= 2
            diag.append(_bdot(q * k, bd) * v)
        sc = [jnp.zeros((C, C), F32) for _ in heads]
        for lv in range(n_lv):
            for i, (d, h) in enumerate(heads):
                sl = slice(h * HEAD_W, (h + 1) * HEAD_W)
                sc[i] = sc[i] + lmask_ref[d, lv] * _bdot_nt(ql[d * n_lv + lv][:, sl], kl[d * n_lv + lv][:, sl])
        oi = [_bdot(sc[i], v[:, h * HEAD_W:(h + 1) * HEAD_W]) for i, (d, h) in enumerate(heads)]
        for d in range(2):
            oi_s[d, rows, :] = jnp.concatenate(oi[d * N_HEADS:(d + 1) * N_HEADS], axis=-1) + diag[d]
        return carry

    lax.fori_loop(0, n, prepare, 0)

    def chunk(ci, carry):
        rows = [pl.ds(pl.multiple_of(cidx * C, C), C) for cidx in (ci, n - 1 - ci)]
        decay = [dc_s[d, pl.ds(pl.multiple_of(cidx * 8, 8), 1), :] for d, cidx in ((0, ci), (1, n - 1 - ci))]
        sts = [st_s[d][:, h * HEAD_W:(h + 1) * HEAD_W] for d, h in heads]
        o_inter = [_bdot_nt(qin_s[d, rows[d], h * HEAD_W:(h + 1) * HEAD_W], sts[i]) for i, (d, h) in enumerate(heads)]
        for d in range(2):
            idx = range(d * N_HEADS, (d + 1) * N_HEADS)
            oi_s[d, rows[d], :] = oi_s[d, rows[d], :] + jnp.concatenate([o_inter[i] for i in idx], axis=-1)
            st_s[d] = st_s[d] * decay[d] + up_s[d, rows[d], :]
        return carry

    lax.fori_loop(0, n, chunk, 0)
    o_ref[...] = _head_norm_gate(oi_s[0] + oi_s[1], bd, gn_ref[...], u_ref[:, 2 * GROUP_W:3 * GROUP_W])
    for d in range(2):
        for h in range(N_HEADS):
            sfin_ref[d, h] = st_s[d][:, h * HEAD_W:(h + 1) * HEAD_W].T


def _hgrn_call(u_hg, lb_l, gn, consts, s0, l, T, nb, row0):
    tril, lmask, bd = consts
    tb = row0 // T
    has_s0 = s0 is not None
    in_specs = [
        pl.BlockSpec((T, HG_COLS), lambda b: (tb + b, 0)),
        pl.BlockSpec((2, 1, GROUP_W), lambda b: (0, 0, 0)),
        pl.BlockSpec((1, GROUP_W), lambda b: (0, 0)),
        pl.BlockSpec((2, CHUNK, CHUNK), lambda b: (0, 0, 0)),
        pl.BlockSpec((2, 6, CHUNK, CHUNK), lambda b: (0, 0, 0, 0)),
        pl.BlockSpec((GROUP_W, GROUP_W), lambda b: (0, 0)),
    ]
    args = [u_hg, lb_l.reshape(2, 1, GROUP_W), gn.reshape(1, GROUP_W), tril, lmask, bd]
    if has_s0:
        in_specs.append(pl.BlockSpec((None, None, 2, N_HEADS, HEAD_W, HEAD_W), lambda b: (b, l, 0, 0, 0, 0)))
        args.append(s0)
    seq = pltpu.VMEM((2, T, GROUP_W), F32)
    return pl.pallas_call(
        functools.partial(_hgrn_kernel, T=T, has_s0=has_s0),
        grid=(nb,),
        in_specs=in_specs,
        out_specs=[
            pl.BlockSpec((T, GROUP_W), lambda b: (b, 0)),
            pl.BlockSpec((None, 2, N_HEADS, HEAD_W, HEAD_W), lambda b: (b, 0, 0, 0, 0)),
        ],
        out_shape=[jax.ShapeDtypeStruct((nb * T, GROUP_W), F32),
                   jax.ShapeDtypeStruct((nb, 2, N_HEADS, HEAD_W, HEAD_W), F32)],
        scratch_shapes=[seq, seq, seq, seq, seq,
                        pltpu.VMEM((2, T // CHUNK * 8, GROUP_W), F32), pltpu.VMEM((2, HEAD_W, GROUP_W), F32)],
        compiler_params=_cparams(("parallel",)),
        name="hgrn",
    )(*args)


def _shift_rows(x, T):
    row = lax.broadcasted_iota(jnp.int32, x.shape, 0)
    prev = jnp.where(row == 0, 0.0, pltpu.roll(x, 1, 0))
    nxt = jnp.where(row == T - 1, 0.0, pltpu.roll(x, T - 1, 0))
    return prev, nxt


def _conv3(x, w_ref, T):
    prev, nxt = _shift_rows(x, T)
    return prev * w_ref[0:1, :] + x * w_ref[1:2, :] + nxt * w_ref[2:3, :]


GDN_UNROLL = 1


def _solve_unit_lower(systems):
    w = systems[0][0].shape[-1]
    slabs = [jnp.concatenate([rhs, nmat, nmat, nmat], axis=-1) for rhs, nmat in systems]
    steps = int(math.log2(CHUNK))
    for step in range(steps):
        last = step == steps - 1
        nxt = []
        for slab in slabs:
            hi = slab.astype(BF16)
            lo = (slab - hi.astype(F32)).astype(BF16)
            lhs = jnp.concatenate([hi[:, w:w + 2 * CHUNK], lo[:, w + 2 * CHUNK:w + 3 * CHUNK]], axis=-1)
            ncols = w if last else slab.shape[-1]
            rhs3 = jnp.concatenate([hi[:, :ncols], lo[:, :ncols], hi[:, :ncols]], axis=0)
            prod = jnp.dot(lhs, rhs3, preferred_element_type=F32)
            if last:
                nxt.append(slab[:, :w] + prod)
            else:
                nxt.append(jnp.concatenate([slab[:, :w] + prod[:, :w], prod[:, w:]], axis=-1))
        slabs = nxt
    return slabs


def _gdn_kernel(*refs, T, has_s0):
    if has_s0:
        (u_ref, cw_ref, alog_ref, dtb_ref, exp_ref, tril_ref, mask_ref, bd_ref, gn_ref, s0_ref,
         o_ref, sfin_ref, q_s, k_s, v_s, la_s, be_s, xw_s, at_s, qin_s, kt_s, al_s, of_s, st_s) = refs
    else:
        (u_ref, cw_ref, alog_ref, dtb_ref, exp_ref, tril_ref, mask_ref, bd_ref, gn_ref,
         o_ref, sfin_ref, q_s, k_s, v_s, la_s, be_s, xw_s, at_s, qin_s, kt_s, al_s, of_s, st_s) = refs
    n = T // CHUNK
    C = CHUNK
    bd = bd_ref[...]

    qkv = _silu(_conv3(u_ref[:, 0:3 * GROUP_W], cw_ref, T))
    q = qkv[:, 0:GROUP_W]
    k = qkv[:, GROUP_W:2 * GROUP_W]
    q_s[...] = q * lax.rsqrt(_sel_dot_right(q * q, bd) + 1e-6) * (HEAD_W ** -0.5)
    k_s[...] = k * lax.rsqrt(_sel_dot_right(k * k, bd) + 1e-6)
    v_s[...] = qkv[:, 2 * GROUP_W:3 * GROUP_W]

    ab = u_ref[:, 4 * GROUP_W:4 * GROUP_W + LANE]
    xa = ab + dtb_ref[...]
    softplus = jnp.maximum(xa, 0.0) + jnp.log(1.0 + jnp.exp(-jnp.abs(xa)))
    log_a = -jnp.exp(alog_ref[...]) * softplus
    lane = lax.broadcasted_iota(jnp.int32, ab.shape, 1)
    narrow = jnp.where(lane < 8, log_a, _sigmoid(ab))
    for d in range(2):
        wide = _dot_sel(narrow, exp_ref[d])
        la_s[d] = wide[:, 0:GROUP_W]
        be_s[d] = wide[:, GROUP_W:2 * GROUP_W]
        if has_s0:
            st_s[d] = jnp.concatenate([s0_ref[d, h] for h in range(N_HEADS)], axis=-1)
        else:
            st_s[d] = jnp.zeros((HEAD_W, GROUP_W), F32)

    def prepare(cidx):
        r0 = pl.multiple_of(cidx * C, C)
        rows = pl.ds(r0, C)
        arow = pl.ds(pl.multiple_of(cidx * 8, 8), 8)
        q = q_s[rows, :]
        k = k_s[rows, :]
        v = v_s[rows, :]
        systems = []
        attns = []
        kts = []
        for d in range(2):
            incl = mask_ref[d, 0] > 0.5
            strict = mask_ref[d, 1]
            la = la_s[d, rows, :]
            be = be_s[d, rows, :]
            gx = _sel_dot(tril_ref[d], la)
            gtot = jnp.sum(la, axis=0, keepdims=True)
            eg = jnp.exp(gx)
            kout = k * jnp.exp(gtot - gx)
            qin_s[d, rows, :] = q * eg
            al_s[d, arow, :] = jnp.broadcast_to(jnp.exp(gtot), (8, GROUP_W))
            kb = k * be
            vb = v * be
            kbg = kb * eg
            for h in range(N_HEADS):
                sl = slice(h * HEAD_W, (h + 1) * HEAD_W)
                gh = gx[:, sl]
                dmat = gh - gh.T
                dec = jnp.where(incl, jnp.exp(jnp.where(incl, dmat, 0.0)), 0.0)
                nmat = -(_bdot_nt(kb[:, sl], k[:, sl]) * dec * strict)
                systems.append((jnp.concatenate([vb[:, sl], kbg[:, sl]], axis=-1), nmat))
                attns.append(_bdot_nt(q[:, sl], k[:, sl]) * dec)
                kts.append(kout[:, sl].T)
        sols = _solve_unit_lower(systems)
        for d in range(2):
            for h in range(N_HEADS):
                xw_s[d, rows, h * LANE:(h + 1) * LANE] = sols[d * N_HEADS + h]
            at_s[d, rows, :] = jnp.concatenate(attns[d * N_HEADS:(d + 1) * N_HEADS], axis=-1)
            kt_s[d, rows, :] = jnp.concatenate(kts[d * N_HEADS:(d + 1) * N_HEADS], axis=-1)

    def prep_body(i, carry):
        for j in range(GDN_UNROLL):
            prepare(i * GDN_UNROLL + j)
        return carry

    lax.fori_loop(0, n // GDN_UNROLL, prep_body, 0)

    def chunk(ci, carry):
        heads = [(d, h) for d in range(2) for h in range(N_HEADS)]
        rows = []
        alast = []
        for d, cidx in ((0, ci), (1, n - 1 - ci)):
            rows.append(pl.ds(pl.multiple_of(cidx * C, C), C))
            alast.append(al_s[d, pl.ds(pl.multiple_of(cidx * 8, 8), 1), :])
        sts = [st_s[d][:, h * HEAD_W:(h + 1) * HEAD_W] for d, h in heads]
        xws = [xw_s[d, rows[d], h * LANE:(h + 1) * LANE] for d, h in heads]
        both = [_bdot(jnp.concatenate([xws[i][:, HEAD_W:], qin_s[d, rows[d], h * HEAD_W:(h + 1) * HEAD_W]], axis=0),
                      sts[i]) for i, (d, h) in enumerate(heads)]
        vnew = [xws[i][:, :HEAD_W] - both[i][:C] for i in range(len(heads))]
        upd = [_bdot(jnp.concatenate([at_s[d, rows[d], h * HEAD_W:(h + 1) * HEAD_W],
                                      kt_s[d, rows[d], h * HEAD_W:(h + 1) * HEAD_W]], axis=0), vnew[i])
               for i, (d, h) in enumerate(heads)]
        for d in range(2):
            idx = range(d * N_HEADS, (d + 1) * N_HEADS)
            of_s[d, rows[d], :] = jnp.concatenate([both[i][C:] + upd[i][:C] for i in idx], axis=-1)
            st_s[d] = jnp.concatenate(
                [sts[i] * alast[d][:, (i - d * N_HEADS) * HEAD_W:(i - d * N_HEADS + 1) * HEAD_W] + upd[i][C:]
                 for i in idx], axis=-1)
        return carry

    lax.fori_loop(0, n, chunk, 0)
    o_ref[...] = _head_norm_gate(of_s[0] + of_s[1], bd, gn_ref[...], u_ref[:, 3 * GROUP_W:4 * GROUP_W])
    for d in range(2):
        for h in range(N_HEADS):
            sfin_ref[d, h] = st_s[d][:, h * HEAD_W:(h + 1) * HEAD_W]


def _gdn_call(u_gd, cw, alog, dtb, gn, consts, s0, l, T, nb, row0):
    tril, masks, expand, bd = consts
    tb = row0 // T
    has_s0 = s0 is not None
    in_specs = [
        pl.BlockSpec((T, GD_PAD), lambda b: (tb + b, 0)),
        pl.BlockSpec((3, 3 * GROUP_W), lambda b: (0, 0)),
        pl.BlockSpec((1, LANE), lambda b: (0, 0)),
        pl.BlockSpec((1, LANE), lambda b: (0, 0)),
        pl.BlockSpec((2, LANE, 2 * GROUP_W), lambda b: (0, 0, 0)),
        pl.BlockSpec((2, CHUNK, CHUNK), lambda b: (0, 0, 0)),
        pl.BlockSpec((2, 2, CHUNK, CHUNK), lambda b: (0, 0, 0, 0)),
        pl.BlockSpec((GROUP_W, GROUP_W), lambda b: (0, 0)),
        pl.BlockSpec((1, GROUP_W), lambda b: (0, 0)),
    ]
    args = [u_gd, cw, alog, dtb, expand, tril, masks, bd, gn]
    if has_s0:
        in_specs.append(pl.BlockSpec((None, None, 2, N_HEADS, HEAD_W, HEAD_W), lambda b: (b, l, 0, 0, 0, 0)))
        args.append(s0)
    seq = pltpu.VMEM((2, T, GROUP_W), F32)
    return pl.pallas_call(
        functools.partial(_gdn_kernel, T=T, has_s0=has_s0),
        grid=(nb,),
        in_specs=in_specs,
        out_specs=[
            pl.BlockSpec((T, GROUP_W), lambda b: (b, 0)),
            pl.BlockSpec((None, 2, N_HEADS, HEAD_W, HEAD_W), lambda b: (b, 0, 0, 0, 0)),
        ],
        out_shape=[jax.ShapeDtypeStruct((nb * T, GROUP_W), F32),
                   jax.ShapeDtypeStruct((nb, 2, N_HEADS, HEAD_W, HEAD_W), F32)],
        scratch_shapes=[pltpu.VMEM((T, GROUP_W), F32)] * 3 + [seq, seq,
            pltpu.VMEM((2, T, N_HEADS * LANE), F32), seq, seq, seq,
            pltpu.VMEM((2, T // CHUNK * 8, GROUP_W), F32), seq, pltpu.VMEM((2, HEAD_W, GROUP_W), F32)],
        compiler_params=_cparams(("parallel",)),
        name="gdn",
    )(*args)


def _hyfilt_kernel(z_ref, win_ref, fh_ref, fl_ref, w1_ref, b1_ref, fr_ref, w2_ref, b2_ref, w3_ref, o_ref, *, T):
    fr = fr_ref[...]
    h = jnp.sin(fr * (_dot3(z_ref[...], w1_ref[...]) + b1_ref[...]))
    h = jnp.sin(fr * (_dot3(h, w2_ref[...]) + b2_ref[...]))
    h = _dot3(h, w3_ref[...])
    win = win_ref[...]
    hf = h[:, 0:GROUP_W] * win
    hb = h[:, GROUP_W:2 * GROUP_W] * win
    row = lax.broadcasted_iota(jnp.int32, hb.shape, 0)
    hb = jnp.where(row == 0, 0.0, hb)
    sh, sl = _split2(jnp.concatenate([hf + hb, hf - hb], axis=-1))
    fh = fh_ref[...]
    spec = (jnp.dot(fh, sh, preferred_element_type=F32) + jnp.dot(fh, sl, preferred_element_type=F32)
            + jnp.dot(fl_ref[...], sh, preferred_element_type=F32))
    o_ref[0:T, :] = spec[0:T, 0:GROUP_W]
    o_ref[T:2 * T, :] = spec[T:2 * T, GROUP_W:2 * GROUP_W]


def _hyfilt_call(T, zp, win, fh, fl, w1p, b1, freq, w2, b2, w3):
    c2 = lambda l: (0, 0)
    return pl.pallas_call(
        functools.partial(_hyfilt_kernel, T=T),
        grid=(DEPTH,),
        in_specs=[
            pl.BlockSpec((T, LANE), c2),
            pl.BlockSpec((T, GROUP_W), c2),
            pl.BlockSpec((2 * T, T), c2),
            pl.BlockSpec((2 * T, T), c2),
            pl.BlockSpec((None, LANE, HY_FH), lambda l: (l, 0, 0)),
            pl.BlockSpec((None, 1, HY_FH), lambda l: (l, 0, 0)),
            pl.BlockSpec((None, 1, HY_FH), lambda l: (l, 0, 0)),
            pl.BlockSpec((None, HY_FH, HY_FH), lambda l: (l, 0, 0)),
            pl.BlockSpec((None, 1, HY_FH), lambda l: (l, 0, 0)),
            pl.BlockSpec((None, HY_FH, 2 * GROUP_W), lambda l: (l, 0, 0)),
        ],
        out_specs=pl.BlockSpec((None, 2 * T, GROUP_W), lambda l: (l, 0, 0)),
        out_shape=jax.ShapeDtypeStruct((DEPTH, 2 * T, GROUP_W), F32),
        compiler_params=_cparams(("parallel",)),
        name="hyfilt",
    )(zp, win, fh, fl, w1p, b1, freq, w2, b2, w3)


def _hyena_kernel(u_ref, cw_ref, cb_ref, spec_ref, skip_ref, fh_ref, fl_ref, ih_ref, il_ref, o_ref, *, T):
    uc = _conv3(u_ref[...], cw_ref, T) + cb_ref[...]
    x0 = uc[:, 0:GROUP_W]
    z = uc[:, GROUP_W:2 * GROUP_W] * uc[:, 2 * GROUP_W:3 * GROUP_W]
    zh, zl = _split2(z)
    fh = fh_ref[...]
    zs = (jnp.dot(fh, zh, preferred_element_type=F32) + jnp.dot(fh, zl, preferred_element_type=F32)
          + jnp.dot(fl_ref[...], zh, preferred_element_type=F32))
    ar, ai = zs[0:T], zs[T:2 * T]
    br, bi = spec_ref[0:T, :], spec_ref[T:2 * T, :]
    ph, plo = _split2(jnp.concatenate([ar * br - ai * bi, ar * bi + ai * br], axis=0))
    ih = ih_ref[...]
    y = (jnp.dot(ih, ph, preferred_element_type=F32) + jnp.dot(ih, plo, preferred_element_type=F32)
         + jnp.dot(il_ref[...], ph, preferred_element_type=F32))
    o_ref[...] = x0 * (y + z * skip_ref[...])


def _hyena_call(u_hy, cw, cb, spec, skip, dft, l, T, nb, row0):
    fh, fl, ih, il = dft
    tb = row0 // T
    c2 = lambda b: (0, 0)
    return pl.pallas_call(
        functools.partial(_hyena_kernel, T=T),
        grid=(nb,),
        in_specs=[
            pl.BlockSpec((T, HY_COLS), lambda b: (tb + b, 0)),
            pl.BlockSpec((3, HY_COLS), c2),
            pl.BlockSpec((1, HY_COLS), c2),
            pl.BlockSpec((None, 2 * T, GROUP_W), lambda b: (l, 0, 0)),
            pl.BlockSpec((1, GROUP_W), c2),
            pl.BlockSpec((2 * T, T), c2),
            pl.BlockSpec((2 * T, T), c2),
            pl.BlockSpec((T, 2 * T), c2),
            pl.BlockSpec((T, 2 * T), c2),
        ],
        out_specs=pl.BlockSpec((T, GROUP_W), lambda b: (b, 0)),
        out_shape=jax.ShapeDtypeStruct((nb * T, GROUP_W), F32),
        compiler_params=_cparams(("parallel",)),
        name="hyena",
    )(u_hy, cw, cb, spec, skip, fh, fl, ih, il)


def _rope(x, cosf, sinf):
    lane = lax.broadcasted_iota(jnp.int32, x.shape, 1)
    half = MLA_ROPE // 2
    partner = jnp.where(lane < MLA_NOPE + half, pltpu.roll(x, LANE - half, 1), pltpu.roll(x, half, 1))
    return x * cosf + partner * sinf


def _qk_norm(x, g):
    ms = jnp.sum(x * x, axis=-1, keepdims=True) * (1.0 / MLA_QK)
    return x * lax.rsqrt(ms + RMS_EPS) * g


def _mla_kernel(*refs, T, ctx):
    if ctx:
        (u_ref, qn_ref, wq_ref, kvn_ref, wkv_ref, qkn_ref, cos_ref, sin_ref, cckv_ref, ckr_ref, o_ref) = refs
    else:
        (u_ref, qn_ref, wq_ref, kvn_ref, wkv_ref, qkn_ref, o_ref, ckv_ref, kr_ref) = refs
    u = u_ref[...]
    cq = _rms(u[:, 0:MLA_Q_LORA], qn_ref[...])
    ckv = _rms(u[:, MLA_Q_LORA:MLA_Q_LORA + MLA_KV_LORA], kvn_ref[...])
    kr = u[:, MLA_Q_LORA + MLA_KV_LORA:MLA_Q_LORA + MLA_KV_LORA + MLA_ROPE]
    if not ctx:
        ckv_ref[...] = ckv
        kr_ref[...] = kr
    q_all = _bdot(cq, wq_ref[...])
    kv = _bdot(ckv, wkv_ref[...])
    gq = qkn_ref[0:1, :]
    gk = qkn_ref[1:2, :]
    if ctx:
        kvc = _bdot(cckv_ref[...], wkv_ref[...])
        krc = ckr_ref[...]
        cosf, sinf = cos_ref[...], sin_ref[...]
    scale = MLA_QK ** -0.5
    outs = []
    for h in range(N_HEADS):
        qh = _qk_norm(q_all[:, h * LANE:(h + 1) * LANE], gq)
        zpad = jnp.zeros((T, LANE - MLA_QK), F32)
        kh = _qk_norm(jnp.concatenate([kv[:, h * HEAD_W:(h + 1) * HEAD_W], kr, zpad], axis=-1), gk)
        vh = kv[:, GROUP_W + h * HEAD_W:GROUP_W + (h + 1) * HEAD_W]
        if ctx:
            qh = _rope(qh, cosf, sinf)
            kh = _rope(kh, cosf, sinf)
            s_len = krc.shape[0]
            zc = jnp.zeros((s_len, LANE - MLA_QK), F32)
            kc = _qk_norm(jnp.concatenate([kvc[:, h * HEAD_W:(h + 1) * HEAD_W], krc, zc], axis=-1), gk)
            kh = jnp.concatenate([kh, kc], axis=0)
            vh = jnp.concatenate([vh, kvc[:, GROUP_W + h * HEAD_W:GROUP_W + (h + 1) * HEAD_W]], axis=0)
        khb = kh.astype(BF16)
        vhb = vh.astype(BF16)
        blocks = []
        for qb in range(T // ATT_QBLOCK):
            s = _bdot_nt(qh[qb * ATT_QBLOCK:(qb + 1) * ATT_QBLOCK], khb) * scale
            e = jnp.exp(s - jnp.max(s, axis=-1, keepdims=True))
            blocks.append(_bdot(e, vhb) / jnp.sum(e, axis=-1, keepdims=True))
        outs.append(blocks[0] if len(blocks) == 1 else jnp.concatenate(blocks, axis=0))
    o_ref[...] = jnp.concatenate(outs, axis=-1)


def _mla_call(u_mla, qn, wq, kvn, wkv, qkn, rope, cache, l, T, nb, row0):
    tb = row0 // T
    ctx = cache is not None
    c2 = lambda b: (0, 0)
    in_specs = [
        pl.BlockSpec((T, MLA_PAD), lambda b: (tb + b, 0)),
        pl.BlockSpec((1, MLA_Q_LORA), c2),
        pl.BlockSpec((MLA_Q_LORA, N_HEADS * LANE), c2),
        pl.BlockSpec((1, MLA_KV_LORA), c2),
        pl.BlockSpec((MLA_KV_LORA, 2 * GROUP_W), c2),
        pl.BlockSpec((2, LANE), c2),
    ]
    args = [u_mla, qn, wq, kvn, wkv, qkn]
    out_specs = [pl.BlockSpec((T, GROUP_W), lambda b: (b, 0))]
    out_shape = [jax.ShapeDtypeStruct((nb * T, GROUP_W), F32)]
    if ctx:
        in_specs += [
            pl.BlockSpec((T, LANE), c2),
            pl.BlockSpec((T, LANE), c2),
            pl.BlockSpec((None, None, PAST_LEN, MLA_KV_LORA), lambda b: (b, l, 0, 0)),
            pl.BlockSpec((None, None, PAST_LEN, MLA_ROPE), lambda b: (b, l, 0, 0)),
        ]
        args += [rope[0], rope[1], cache[0], cache[1]]
    else:
        out_specs += [pl.BlockSpec((T, MLA_KV_LORA), lambda b: (b, 0)),
                      pl.BlockSpec((T, MLA_ROPE), lambda b: (b, 0))]
        out_shape += [jax.ShapeDtypeStruct((nb * T, MLA_KV_LORA), F32),
                      jax.ShapeDtypeStruct((nb * T, MLA_ROPE), F32)]
    return pl.pallas_call(
        functools.partial(_mla_kernel, T=T, ctx=ctx),
        grid=(nb,),
        in_specs=in_specs,
        out_specs=out_specs,
        out_shape=out_shape,
        compiler_params=_cparams(("parallel",)),
        name="mla",
    )(*args)


def _pad_cols(w, width):
    return jnp.pad(w, [(0, 0)] * (w.ndim - 1) + [(0, width - w.shape[-1])])


def _prep_w_in(w_in):
    o1 = HG_COLS
    o2 = o1 + HY_COLS
    o3 = o2 + MLA_COLS
    return (w_in[..., :o1].astype(BF16), w_in[..., o1:o2].astype(BF16),
            _pad_cols(w_in[..., o2:o3], MLA_PAD).astype(BF16), _pad_cols(w_in[..., o3:], GD_PAD).astype(BF16))


def _prep_wq(w_q_up):
    w = w_q_up.reshape(DEPTH, MLA_Q_LORA, N_HEADS, MLA_QK)
    return _pad_cols(w, LANE).reshape(DEPTH, MLA_Q_LORA, N_HEADS * LANE).astype(BF16)


def _prep_wkv(w_kv_up):
    w = w_kv_up.reshape(DEPTH, MLA_KV_LORA, N_HEADS, 2, HEAD_W)
    return w.transpose(0, 1, 3, 2, 4).reshape(DEPTH, MLA_KV_LORA, 2 * GROUP_W).astype(BF16)


def _lower_bounds(hgrn_lb):
    lb = jnp.cumsum(jax.nn.softmax(hgrn_lb.astype(F32), axis=0), axis=0)
    return lb - lb[0]


def kernel(x_prompt, x_sample, cache_mla_ckv, cache_mla_krope, state_hgrn, state_gdn, c, c_ctx, w_ada, b_ada, norm_ffn, w_ffn_gu, w_ffn_down, norm_mix, w_in, w_out, hgrn_lb, hgrn_norm, hy_conv_w, hy_conv_b, hy_w1, hy_b1, hy_freq, hy_w2, hy_b2, hy_w3, hy_skip, mla_q_norm_a, mla_w_q_up, mla_kv_norm_a, mla_w_kv_up, mla_qk_norm, gdn_conv_w, gdn_a_log, gdn_dt_bias, gdn_norm):
    x = jnp.concatenate([x_prompt.reshape(N_PROMPT, D_MODEL), x_sample.reshape(N_SAMPLE, D_MODEL)], axis=0)

    cond8 = jnp.zeros((8, D_MODEL), F32).at[0].set(c_ctx).at[1:1 + DEC_BATCH].set(c)
    ada = _ada_call(cond8, w_ada, b_ada)

    w_in_parts = _prep_w_in(w_in)
    w_out_bf = w_out.astype(BF16)
    wq = _prep_wq(mla_w_q_up)
    wkv = _prep_wkv(mla_w_kv_up)
    qkn = _pad_cols(mla_qk_norm, LANE)
    lb_all = _lower_bounds(hgrn_lb)
    alog = _pad_cols(gdn_a_log.reshape(DEPTH, 1, 8), LANE)
    dtb = _pad_cols(gdn_dt_bias.reshape(DEPTH, 1, 8), LANE)
    gdn_gn = jnp.tile(gdn_norm, (1, N_HEADS)).reshape(DEPTH, 1, GROUP_W)
    w1p = jnp.pad(hy_w1, ((0, 0), (0, LANE - HY_EMB), (0, 0)))

    bd = jnp.asarray(_block_diag_ones(), BF16)
    hg_tril, hg_m = _hgrn_consts()
    hg_consts = (jnp.asarray(hg_tril, BF16), jnp.asarray(hg_m, F32), bd)
    gd_tril, gd_masks, gd_expand = _gdn_consts()
    gd_consts = (jnp.asarray(gd_tril, BF16), jnp.asarray(gd_masks, F32), jnp.asarray(gd_expand, BF16), bd)
    rope = tuple(jnp.asarray(a) for a in _rope_consts(DEC_SEQ))
    groups = ((SEQ, BATCH, 0), (DEC_SEQ, DEC_BATCH, N_PROMPT))
    dft = {}
    spec = {}
    for T, _, _ in groups:
        fwd, inv = _dft_consts(T)
        fh, fl = _np_split2(fwd)
        ih, il = _np_split2(inv)
        dft[T] = (fh, fl, ih, il)
        zp, win = _hyena_pos_consts(T)
        spec[T] = _hyfilt_call(T, jnp.asarray(zp), jnp.asarray(win), fh, fl, w1p,
                               hy_b1.reshape(DEPTH, 1, HY_FH), hy_freq.reshape(DEPTH, 1, HY_FH), hy_w2,
                               hy_b2.reshape(DEPTH, 1, HY_FH), hy_w3)

    new_ckv, new_kr, new_hg, new_gd = [], [], [], []
    for l in range(DEPTH):
        x = _ffn_call(x, ada, norm_ffn, w_ffn_gu, w_ffn_down, l, 0)
        u_hg, u_hy, u_mla, u_gd = _inproj_call(x, ada, norm_mix, w_in_parts, l)
        outs = []
        for gi, (T, nb, row0) in enumerate(groups):
            latent = gi == 1
            o_hg, s_hg = _hgrn_call(u_hg, lb_all[l], hgrn_norm[l], hg_consts,
                                    state_hgrn if latent else None, l, T, nb, row0)
            o_hy = _hyena_call(u_hy, hy_conv_w[l], hy_conv_b[l].reshape(1, HY_COLS), spec[T],
                               hy_skip[l].reshape(1, GROUP_W), dft[T], l, T, nb, row0)
            mla = _mla_call(u_mla, mla_q_norm_a[l].reshape(1, MLA_Q_LORA), wq[l],
                            mla_kv_norm_a[l].reshape(1, MLA_KV_LORA), wkv[l], qkn[l],
                            rope if latent else None,
                            (cache_mla_ckv, cache_mla_krope) if latent else None, l, T, nb, row0)
            o_gd, s_gd = _gdn_call(u_gd, gdn_conv_w[l], alog[l], dtb[l], gdn_gn[l], gd_consts,
                                   state_gdn if latent else None, l, T, nb, row0)
            outs.append((o_hg, o_hy, mla[0], o_gd))
            if not latent:
                new_ckv.append(mla[1].reshape(BATCH, SEQ, MLA_KV_LORA))
                new_kr.append(mla[2].reshape(BATCH, SEQ, MLA_ROPE))
                new_hg.append(s_hg)
                new_gd.append(s_gd)
        x = _outproj_call(x, ada, w_out_bf, outs[0], outs[1], l)
        x = _ffn_call(x, ada, norm_ffn, w_ffn_gu, w_ffn_down, l, 1)

    y_prompt = x[:N_PROMPT].reshape(BATCH, SEQ, D_MODEL)
    y_sample = x[N_PROMPT:].reshape(DEC_BATCH, DEC_SEQ, D_MODEL)
    return (y_prompt, y_sample, jnp.stack(new_ckv, axis=1), jnp.stack(new_kr, axis=1),
            jnp.stack(new_hg, axis=1), jnp.stack(new_gd, axis=1))
```

```python
import functools
import math

import numpy as np
import jax
import jax.numpy as jnp
from jax import lax
from jax.experimental import pallas as pl
from jax.experimental.pallas import tpu as pltpu

F32 = jnp.float32
BF16 = jnp.bfloat16

D_MODEL = 1024
BATCH = 16
SEQ = 256
DEPTH = 4
DEC_BATCH = 2
DEC_SEQ = 1024
PAST_LEN = 256
GRID_W = 64
N_ADA = 9
D_FF = 2816
GROUP_W = 256
CHUNK = 64
RMS_EPS = 1e-6
N_HEADS = 4
HEAD_W = 64
HY_EMB = 33
HY_FH = 64
HY_TARGET = 1e-2
HY_FAST = 0.3
HY_SLOW = 1.5
MLA_NOPE = 64
MLA_ROPE = 32
MLA_QK = MLA_NOPE + MLA_ROPE
MLA_Q_LORA = 256
MLA_KV_LORA = 128
ROPE_BASE = 10000.0

HG_COLS = 5 * GROUP_W
HY_COLS = 3 * GROUP_W
MLA_COLS = MLA_Q_LORA + MLA_KV_LORA + MLA_ROPE
GD_COLS = 4 * GROUP_W + 16
MLA_PAD = 512
GD_PAD = 1152

N_PROMPT = BATCH * SEQ
N_SAMPLE = DEC_BATCH * DEC_SEQ
N_TOK = N_PROMPT + N_SAMPLE
LANE = 128
VMEM_LIMIT = 56 * 1024 * 1024
ROW_TILE = 1024
FF_TILE = 256
ADA_TILE = 1536
ATT_QBLOCK = 256


def _bdot(a, b):
    return jnp.dot(a.astype(BF16), b.astype(BF16), preferred_element_type=F32)


def _bdot_nt(a, b):
    return lax.dot_general(a.astype(BF16), b.astype(BF16), (((1,), (1,)), ((), ())),
                           preferred_element_type=F32)


def _bdot_tn(a, b):
    return lax.dot_general(a.astype(BF16), b.astype(BF16), (((0,), (0,)), ((), ())),
                           preferred_element_type=F32)


def _split2(x):
    hi = x.astype(BF16)
    lo = (x - hi.astype(F32)).astype(BF16)
    return hi, lo


def _split3(x):
    hi = x.astype(BF16)
    r = x - hi.astype(F32)
    mid = r.astype(BF16)
    lo = (r - mid.astype(F32)).astype(BF16)
    return hi, mid, lo


def _dot3(a, b):
    ah, al = _split2(a)
    bh, bl = _split2(b)
    return (jnp.dot(ah, bh, preferred_element_type=F32) + jnp.dot(ah, bl, preferred_element_type=F32)
            + jnp.dot(al, bh, preferred_element_type=F32))


def _sel_dot(c, x):
    h, m, l = _split3(x)
    return (jnp.dot(c, h, preferred_element_type=F32) + jnp.dot(c, m, preferred_element_type=F32)
            + jnp.dot(c, l, preferred_element_type=F32))


def _dot_sel(x, c):
    h, m, l = _split3(x)
    return (jnp.dot(h, c, preferred_element_type=F32) + jnp.dot(m, c, preferred_element_type=F32)
            + jnp.dot(l, c, preferred_element_type=F32))


def _sigmoid(x):
    return 1.0 / (1.0 + jnp.exp(-x))


def _silu(x):
    return x * _sigmoid(x)


def _rms(x, g):
    return x * lax.rsqrt(jnp.mean(x * x, axis=-1, keepdims=True) + RMS_EPS) * g


def _cparams(sem):
    return pltpu.CompilerParams(dimension_semantics=sem, vmem_limit_bytes=VMEM_LIMIT)


def _cond_of_tile(i):
    return jnp.maximum(i - (N_PROMPT // ROW_TILE - 1), 0)


def _ada_kernel(c_ref, w_ref, b_ref, o_ref):
    o_ref[...] = _dot3(_silu(c_ref[...]), w_ref[...]) + b_ref[...]


def _ada_call(cond8, w_ada, b_ada):
    n = N_ADA * D_MODEL
    out = pl.pallas_call(
        _ada_kernel,
        grid=(DEPTH, n // ADA_TILE),
        in_specs=[
            pl.BlockSpec((8, D_MODEL), lambda l, j: (0, 0)),
            pl.BlockSpec((None, D_MODEL, ADA_TILE), lambda l, j: (l, 0, j)),
            pl.BlockSpec((None, 1, ADA_TILE), lambda l, j: (l, 0, j)),
        ],
        out_specs=pl.BlockSpec((None, 8, ADA_TILE), lambda l, j: (l, 0, j)),
        out_shape=jax.ShapeDtypeStruct((DEPTH, 8, n), F32),
        compiler_params=_cparams(("parallel", "parallel")),
        name="ada",
    )(cond8, w_ada, b_ada.reshape(DEPTH, 1, n))
    return out.reshape(DEPTH, 8, N_ADA, D_MODEL)


def _ffn_kernel(x_ref, ada_ref, g_ref, wg_ref, wu_ref, wd_ref, o_ref, h_scr, acc_scr, *, sub):
    f = pl.program_id(1)

    @pl.when(f == 0)
    def _():
        y = _rms(x_ref[...], g_ref[...])
        h = y * (1.0 + ada_ref[3 * sub + 1:3 * sub + 2, :]) + ada_ref[3 * sub:3 * sub + 1, :]
        h_scr[...] = h.astype(BF16)
        acc_scr[...] = jnp.zeros_like(acc_scr)

    h = h_scr[...]
    gate = jnp.dot(h, wg_ref[...].astype(BF16), preferred_element_type=F32)
    up = jnp.dot(h, wu_ref[...].astype(BF16), preferred_element_type=F32)
    a = (_silu(gate) * up).astype(BF16)
    acc_scr[...] += jnp.dot(a, wd_ref[...].astype(BF16), preferred_element_type=F32)

    @pl.when(f == pl.num_programs(1) - 1)
    def _():
        o_ref[...] = x_ref[...] + 0.5 * ada_ref[3 * sub + 2:3 * sub + 3, :] * acc_scr[...]


def _ffn_call(x, ada, norm_ffn, w_gu, w_down, l, j):
    sub = 2 * j
    nf = D_FF // FF_TILE
    return pl.pallas_call(
        functools.partial(_ffn_kernel, sub=sub),
        grid=(N_TOK // ROW_TILE, nf),
        in_specs=[
            pl.BlockSpec((ROW_TILE, D_MODEL), lambda i, f: (i, 0)),
            pl.BlockSpec((None, None, N_ADA, D_MODEL), lambda i, f: (l, _cond_of_tile(i), 0, 0)),
            pl.BlockSpec((None, None, 1, D_MODEL), lambda i, f: (l, j, 0, 0)),
            pl.BlockSpec((None, None, D_MODEL, FF_TILE), lambda i, f: (l, j, 0, f)),
            pl.BlockSpec((None, None, D_MODEL, FF_TILE), lambda i, f: (l, j, 0, nf + f)),
            pl.BlockSpec((None, None, FF_TILE, D_MODEL), lambda i, f: (l, j, f, 0)),
        ],
        out_specs=pl.BlockSpec((ROW_TILE, D_MODEL), lambda i, f: (i, 0)),
        out_shape=jax.ShapeDtypeStruct((N_TOK, D_MODEL), F32),
        scratch_shapes=[pltpu.VMEM((ROW_TILE, D_MODEL), BF16), pltpu.VMEM((ROW_TILE, D_MODEL), F32)],
        compiler_params=_cparams(("parallel", "arbitrary")),
        name="ffn",
    )(x, ada, norm_ffn.reshape(DEPTH, 2, 1, D_MODEL), w_gu, w_gu, w_down)


IN_TILE = 512


def _inproj_kernel(x_ref, ada_ref, g_ref, w1, w2, w3, w4, o1, o2, o3, o4):
    y = _rms(x_ref[...], g_ref[...])
    h = (y * (1.0 + ada_ref[4:5, :]) + ada_ref[3:4, :]).astype(BF16)
    for w, o in ((w1, o1), (w2, o2), (w3, o3), (w4, o4)):
        o[...] = jnp.dot(h, w[...], preferred_element_type=F32)


def _inproj_call(x, ada, norm_mix, ws, l):
    widths = (HG_COLS, HY_COLS, MLA_PAD, GD_PAD)
    per = ROW_TILE // IN_TILE
    return pl.pallas_call(
        _inproj_kernel,
        grid=(N_TOK // IN_TILE,),
        in_specs=[
            pl.BlockSpec((IN_TILE, D_MODEL), lambda i: (i, 0)),
            pl.BlockSpec((None, None, N_ADA, D_MODEL), lambda i: (l, _cond_of_tile(i // per), 0, 0)),
            pl.BlockSpec((None, 1, D_MODEL), lambda i: (l, 0, 0)),
        ] + [pl.BlockSpec((None, D_MODEL, w), lambda i: (l, 0, 0)) for w in widths],
        out_specs=[pl.BlockSpec((IN_TILE, w), lambda i: (i, 0)) for w in widths],
        out_shape=[jax.ShapeDtypeStruct((N_TOK, w), F32) for w in widths],
        compiler_params=_cparams(("parallel",)),
        name="inproj",
    )(x, ada, norm_mix.reshape(DEPTH, 1, D_MODEL), *ws)


OUT_TILE = 512


def _outproj_kernel(x_ref, ada_ref, w_ref, *refs):
    o_ref = refs[-1]
    i = pl.program_id(0)
    n_p = N_PROMPT // OUT_TILE

    def run(srcs):
        acc = jnp.zeros((OUT_TILE, D_MODEL), F32)
        for g, s in enumerate(srcs):
            acc += jnp.dot(s[...].astype(BF16), w_ref[g * GROUP_W:(g + 1) * GROUP_W, :],
                           preferred_element_type=F32)
        o_ref[...] = x_ref[...] + ada_ref[5:6, :] * acc

    @pl.when(i < n_p)
    def _():
        run(refs[0:4])

    @pl.when(i >= n_p)
    def _():
        run(refs[4:8])


def _outproj_call(x, ada, w_out_bf, o_p, o_s, l):
    per = ROW_TILE // OUT_TILE
    n_p = N_PROMPT // OUT_TILE
    n_s = N_SAMPLE // OUT_TILE
    return pl.pallas_call(
        _outproj_kernel,
        grid=(N_TOK // OUT_TILE,),
        in_specs=[
            pl.BlockSpec((OUT_TILE, D_MODEL), lambda i: (i, 0)),
            pl.BlockSpec((None, None, N_ADA, D_MODEL), lambda i: (l, _cond_of_tile(i // per), 0, 0)),
            pl.BlockSpec((None, D_MODEL, D_MODEL), lambda i: (l, 0, 0)),
        ] + [pl.BlockSpec((OUT_TILE, GROUP_W), lambda i: (jnp.minimum(i, n_p - 1), 0))] * 4
          + [pl.BlockSpec((OUT_TILE, GROUP_W), lambda i: (jnp.clip(i - n_p, 0, n_s - 1), 0))] * 4,
        out_specs=pl.BlockSpec((OUT_TILE, D_MODEL), lambda i: (i, 0)),
        out_shape=jax.ShapeDtypeStruct((N_TOK, D_MODEL), F32),
        compiler_params=_cparams(("parallel",)),
        name="outproj",
    )(x, ada, w_out_bf, *o_p, *o_s)


def _block_diag_ones():
    idx = np.arange(GROUP_W) // HEAD_W
    return (idx[:, None] == idx[None, :]).astype(np.float32)


def _hgrn_consts():
    C = CHUNK
    i = np.arange(C)[:, None]
    j = np.arange(C)[None, :]
    masks = []
    s = C // 2
    while s >= 1:
        up_i = (i // s) % 2 == 1
        up_j = (j // s) % 2 == 1
        masks.append(up_i & (~up_j) & (i // (2 * s) == j // (2 * s)))
        s //= 2
    fwd_m = np.stack([m.astype(np.float32) for m in masks])
    bwd_m = np.stack([m.astype(np.float32)[::-1, ::-1] for m in masks])
    tril = np.stack([(j <= i), (j >= i)]).astype(np.float32)
    return tril, np.tile(np.stack([fwd_m, bwd_m]), (1, 1, 1, 2))


def _gdn_consts():
    C = CHUNK
    i = np.arange(C)[:, None]
    t = np.arange(C)[None, :]
    tril = np.stack([(t <= i), (t >= i)]).astype(np.float32)
    masks = np.stack([np.stack([(t <= i), (t < i)]), np.stack([(t >= i), (t > i)])]).astype(np.float32)
    expand = np.zeros((2, LANE, 2 * GROUP_W), np.float32)
    for d in range(2):
        for h in range(N_HEADS):
            expand[d, d * N_HEADS + h, h * HEAD_W:(h + 1) * HEAD_W] = 1.0
            expand[d, 8 + d * N_HEADS + h, GROUP_W + h * HEAD_W:GROUP_W + (h + 1) * HEAD_W] = 1.0
    return tril, masks, expand


def _dft_consts(T):
    n2 = 4 * T
    k = np.arange(T, dtype=np.int64)[:, None]
    s = np.arange(T, dtype=np.int64)[None, :]
    ang = np.pi * (((2 * k + 1) * s) % n2).astype(np.float64) / (2 * T)
    fwd = np.concatenate([np.cos(ang), -np.sin(ang)], axis=0)
    inv = fwd.T / T
    return fwd.astype(np.float32), inv.astype(np.float32)


def _np_split2(x):
    hi = jnp.asarray(x, F32).astype(BF16)
    lo = (jnp.asarray(x, F32) - hi.astype(F32)).astype(BF16)
    return hi, lo


def _hyena_pos_consts(T):
    pos = np.arange(T, dtype=np.float32)
    t = pos / np.float32(T - 1)
    bands = np.linspace(1e-4, (HY_EMB - 1) // 2 - 1, (HY_EMB - 1) // 2, dtype=np.float32)
    ang = (np.float32(2.0 * math.pi / T) * pos[:, None]) * bands[None, :]
    z = np.concatenate([t[:, None], np.cos(ang), -np.sin(ang)], axis=-1).astype(np.float32)
    zp = np.zeros((T, LANE), np.float32)
    zp[:, :HY_EMB] = z
    max_decay = math.log(HY_TARGET) / HY_FAST
    min_decay = math.log(HY_TARGET) / HY_SLOW
    deltas = np.linspace(min_decay, max_decay, GROUP_W, dtype=np.float32)
    window = np.exp(-t[:, None] * np.abs(deltas)[None, :]).astype(np.float32)
    return zp, window


def _rope_consts(T):
    rows = T // GRID_W
    row = np.repeat(np.arange(rows, dtype=np.float32), GRID_W)
    col = (np.arange(T) % GRID_W).astype(np.float32)
    pairs = MLA_ROPE // 4
    inv = (np.float32(ROPE_BASE) ** (-np.arange(pairs, dtype=np.float32) / np.float32(pairs))).astype(np.float32)
    ang = np.concatenate([row[:, None] * inv, col[:, None] * inv], axis=-1).astype(np.float32)
    cos, sin = np.cos(ang), np.sin(ang)
    cosf = np.ones((T, LANE), np.float32)
    sinf = np.zeros((T, LANE), np.float32)
    half = MLA_ROPE // 2
    cosf[:, MLA_NOPE:MLA_NOPE + half] = cos
    cosf[:, MLA_NOPE + half:MLA_QK] = cos
    sinf[:, MLA_NOPE:MLA_NOPE + half] = -sin
    sinf[:, MLA_NOPE + half:MLA_QK] = sin
    return cosf, sinf


def _head_norm_gate(tot, bd, gn, gate):
    ms = _sel_dot_right(tot * tot, bd) * (1.0 / HEAD_W)
    return tot * lax.rsqrt(ms + RMS_EPS) * gn * _silu(gate)


def _sel_dot_right(x, c):
    h, l = _split2(x)
    return jnp.dot(h, c, preferred_element_type=F32) + jnp.dot(l, c, preferred_element_type=F32)


def _block_ref(b, two_s, r):
    C, W = b.shape
    if two_s % 8 == 0:
        b3 = b.reshape(C // two_s, two_s, W)
        return jnp.broadcast_to(b3[:, r:r + 1, :], b3.shape).reshape(C, W)
    pos = lax.broadcasted_iota(jnp.int32, b.shape, 0) % two_s
    out = b
    for p in range(two_s):
        if p != r:
            out = jnp.where(pos == p, pltpu.roll(b, (p - r) % C, 0), out)
    return out


N_PAIRS = N_HEADS // 2


def _pair_blockdiag(x):
    lane = lax.broadcasted_iota(jnp.int32, x.shape, 1)
    zero = jnp.zeros_like(x)
    return jnp.concatenate([jnp.where(lane < HEAD_W, x, zero), jnp.where(lane >= HEAD_W, x, zero)], axis=0)


def _hgrn_kernel(*refs, T, has_s0):
    if has_s0:
        (u_ref, lb_ref, gn_ref, tril_ref, lmask_ref, bd_ref, s0_ref,
         o_ref, sfin_ref, lf_s, kk_s, oi_s, qin_s, up_s, dc_s, st_s) = refs
    else:
        (u_ref, lb_ref, gn_ref, tril_ref, lmask_ref, bd_ref,
         o_ref, sfin_ref, lf_s, kk_s, oi_s, qin_s, up_s, dc_s, st_s) = refs
    n = T // CHUNK
    C = CHUNK
    bd = bd_ref[...]
    n_lv = int(math.log2(C))

    for d in range(2):
        z = u_ref[:, (3 + d) * GROUP_W:(4 + d) * GROUP_W]
        lb = lb_ref[d]
        log_sig = jnp.minimum(z, 0.0) - jnp.log(1.0 + jnp.exp(-jnp.abs(z)))
        a = jnp.log(lb)
        c = jnp.log(1.0 - lb) + log_sig
        m = jnp.maximum(a, c)
        lf_s[d] = m + jnp.log(1.0 + jnp.exp(jnp.minimum(a, c) - m))
        kk_s[d] = (1.0 - lb) * _sigmoid(-z)
        if has_s0:
            st_s[d] = jnp.concatenate([s0_ref[d, h].T for h in range(N_HEADS)], axis=-1)
        else:
            st_s[d] = jnp.zeros((HEAD_W, GROUP_W), F32)

    def prepare(cidx, carry):
        rows = pl.ds(pl.multiple_of(cidx * C, C), C)
        arow = pl.ds(pl.multiple_of(cidx * 8, 8), 8)
        q = u_ref[rows, 0:GROUP_W] * (HEAD_W ** -0.5)
        v = u_ref[rows, GROUP_W:2 * GROUP_W]
        vb = v.astype(BF16)
        vt = [jnp.concatenate([v[:, h * HEAD_W:(h + 1) * HEAD_W].T for h in (2 * p, 2 * p + 1)], axis=-1)
              for p in range(N_PAIRS)]
        v_bd = [_pair_blockdiag(vb[:, p * LANE:(p + 1) * LANE]) for p in range(N_PAIRS)]
        ql, kl, diag = [], [], []
        for d in range(2):
            lf = lf_s[d, rows, :]
            k = kk_s[d, rows, :]
            hi, lo = _split2(lf)
            tril = tril_ref[d]
            b = jnp.dot(tril, hi, preferred_element_type=F32) + jnp.dot(tril, lo, preferred_element_type=F32)
            tot = jnp.sum(lf, axis=0, keepdims=True)
            qin_s[d, rows, :] = q * jnp.exp(b)
            ko = (k * jnp.exp(tot - b)).astype(BF16)
            up_s[d, rows, :] = jnp.concatenate(
                [jnp.dot(vt[p].astype(BF16), _pair_blockdiag(ko[:, p * LANE:(p + 1) * LANE]),
                         preferred_element_type=F32) for p in range(N_PAIRS)], axis=-1)
            dc_s[d, arow, :] = jnp.broadcast_to(jnp.exp(tot), (8, GROUP_W))
            s = C // 2
            while s >= 1:
                e = jnp.exp(-jnp.abs(b - _block_ref(b, 2 * s, s - 1 if d == 0 else s)))
                ql.append((q * e).astype(BF16))
                kl.append((k * e).astype(BF16))
                s //= 2
            diag.append(_bdot(q * k, bd) * v)
        pairs = [(d, p) for d in range(2) for p in range(N_PAIRS)]
        sc = [jnp.zeros((C, LANE), F32) for _ in pairs]
        for lv in range(n_lv):
            for i, (d, p) in enumerate(pairs):
                sl = slice(p * LANE, (p + 1) * LANE)
                prod = lax.dot_general(ql[d * n_lv + lv][:, sl], _pair_blockdiag(kl[d * n_lv + lv][:, sl]),
                                       (((1,), (1,)), ((), ())), preferred_element_type=F32)
                sc[i] = sc[i] + lmask_ref[d, lv] * prod
        oi = [jnp.dot(sc[i].astype(BF16), v_bd[p], preferred_element_type=F32) for i, (d, p) in enumerate(pairs)]
        for d in range(2):
            oi_s[d, rows, :] = jnp.concatenate(oi[d * N_PAIRS:(d + 1) * N_PAIRS], axis=-1) + diag[d]
        return carry

    lax.fori_loop(0, n, prepare, 0, unroll=2)

    def chunk(ci, carry):
        rows = [pl.ds(pl.multiple_of(cidx * C, C), C) for cidx in (ci, n - 1 - ci)]
        decay = [dc_s[d, pl.ds(pl.multiple_of(cidx * 8, 8), 1), :] for d, cidx in ((0, ci), (1, n - 1 - ci))]
        for d in range(2):
            st = st_s[d]
            stb = st.astype(BF16)
            o_inter = [lax.dot_general(qin_s[d, rows[d], p * LANE:(p + 1) * LANE].astype(BF16),
                                       _pair_blockdiag(stb[:, p * LANE:(p + 1) * LANE]),
                                       (((1,), (1,)), ((), ())), preferred_element_type=F32)
                       for p in range(N_PAIRS)]
            oi_s[d, rows[d], :] = oi_s[d, rows[d], :] + jnp.concatenate(o_inter, axis=-1)
            st_s[d] = st * decay[d] + up_s[d, rows[d], :]
        return carry

    lax.fori_loop(0, n, chunk, 0)
    o_ref[...] = _head_norm_gate(oi_s[0] + oi_s[1], bd, gn_ref[...], u_ref[:, 2 * GROUP_W:3 * GROUP_W])
    for d in range(2):
        for h in range(N_HEADS):
            sfin_ref[d, h] = st_s[d][:, h * HEAD_W:(h + 1) * HEAD_W].T


def _hgrn_call(u_hg, lb_l, gn, consts, s0, l, T, nb, row0):
    tril, lmask, bd = consts
    tb = row0 // T
    has_s0 = s0 is not None
    in_specs = [
        pl.BlockSpec((T, HG_COLS), lambda b: (tb + b, 0)),
        pl.BlockSpec((2, 1, GROUP_W), lambda b: (0, 0, 0)),
        pl.BlockSpec((1, GROUP_W), lambda b: (0, 0)),
        pl.BlockSpec((2, CHUNK, CHUNK), lambda b: (0, 0, 0)),
        pl.BlockSpec((2, 6, CHUNK, LANE), lambda b: (0, 0, 0, 0)),
        pl.BlockSpec((GROUP_W, GROUP_W), lambda b: (0, 0)),
    ]
    args = [u_hg, lb_l.reshape(2, 1, GROUP_W), gn.reshape(1, GROUP_W), tril, lmask, bd]
    if has_s0:
        in_specs.append(pl.BlockSpec((None, None, 2, N_HEADS, HEAD_W, HEAD_W), lambda b: (b, l, 0, 0, 0, 0)))
        args.append(s0)
    seq = pltpu.VMEM((2, T, GROUP_W), F32)
    return pl.pallas_call(
        functools.partial(_hgrn_kernel, T=T, has_s0=has_s0),
        grid=(nb,),
        in_specs=in_specs,
        out_specs=[
            pl.BlockSpec((T, GROUP_W), lambda b: (b, 0)),
            pl.BlockSpec((None, 2, N_HEADS, HEAD_W, HEAD_W), lambda b: (b, 0, 0, 0, 0)),
        ],
        out_shape=[jax.ShapeDtypeStruct((nb * T, GROUP_W), F32),
                   jax.ShapeDtypeStruct((nb, 2, N_HEADS, HEAD_W, HEAD_W), F32)],
        scratch_shapes=[seq, seq, seq, seq, seq,
                        pltpu.VMEM((2, T // CHUNK * 8, GROUP_W), F32), pltpu.VMEM((2, HEAD_W, GROUP_W), F32)],
        compiler_params=_cparams(("parallel",)),
        name="hgrn",
    )(*args)


def _shift_rows(x, T):
    row = lax.broadcasted_iota(jnp.int32, x.shape, 0)
    prev = jnp.where(row == 0, 0.0, pltpu.roll(x, 1, 0))
    nxt = jnp.where(row == T - 1, 0.0, pltpu.roll(x, T - 1, 0))
    return prev, nxt


def _conv3(x, w_ref, T):
    prev, nxt = _shift_rows(x, T)
    return prev * w_ref[0:1, :] + x * w_ref[1:2, :] + nxt * w_ref[2:3, :]


GDN_UNROLL = 2


def _solve_unit_lower(systems):
    c2 = 2 * CHUNK
    slabs = [jnp.concatenate([nmat, nmat, rhs], axis=-1) for rhs, nmat in systems]
    steps = int(math.log2(CHUNK))
    for step in range(steps):
        last = step == steps - 1
        nxt = []
        for slab in slabs:
            hi = slab.astype(BF16)
            lo = (slab - hi.astype(F32)).astype(BF16)
            lhs = jnp.concatenate([hi[:, :c2], lo[:, :CHUNK]], axis=-1)
            first = c2 if last else 0
            rhs3 = jnp.concatenate([hi[:, first:], lo[:, first:], hi[:, first:]], axis=0)
            prod = jnp.dot(lhs, rhs3, preferred_element_type=F32)
            if last:
                nxt.append(slab[:, c2:] + prod)
            else:
                nxt.append(jnp.concatenate([prod[:, :c2], slab[:, c2:] + prod[:, c2:]], axis=-1))
        slabs = nxt
    return slabs


def _gdn_kernel(*refs, T, has_s0):
    if has_s0:
        (u_ref, cw_ref, alog_ref, dtb_ref, exp_ref, tril_ref, mask_ref, bd_ref, gn_ref, s0_ref,
         o_ref, sfin_ref, q_s, k_s, v_s, la_s, be_s, xw_s, at_s, qin_s, kt_s, al_s, of_s, st_s) = refs
    else:
        (u_ref, cw_ref, alog_ref, dtb_ref, exp_ref, tril_ref, mask_ref, bd_ref, gn_ref,
         o_ref, sfin_ref, q_s, k_s, v_s, la_s, be_s, xw_s, at_s, qin_s, kt_s, al_s, of_s, st_s) = refs
    n = T // CHUNK
    C = CHUNK
    bd = bd_ref[...]

    qkv = _silu(_conv3(u_ref[:, 0:3 * GROUP_W], cw_ref, T))
    q = qkv[:, 0:GROUP_W]
    k = qkv[:, GROUP_W:2 * GROUP_W]
    q_s[...] = q * lax.rsqrt(_sel_dot_right(q * q, bd) + 1e-6) * (HEAD_W ** -0.5)
    k_s[...] = k * lax.rsqrt(_sel_dot_right(k * k, bd) + 1e-6)
    v_s[...] = qkv[:, 2 * GROUP_W:3 * GROUP_W]

    ab = u_ref[:, 4 * GROUP_W:4 * GROUP_W + LANE]
    xa = ab + dtb_ref[...]
    softplus = jnp.maximum(xa, 0.0) + jnp.log(1.0 + jnp.exp(-jnp.abs(xa)))
    log_a = -jnp.exp(alog_ref[...]) * softplus
    lane = lax.broadcasted_iota(jnp.int32, ab.shape, 1)
    narrow = jnp.where(lane < 8, log_a, _sigmoid(ab))
    for d in range(2):
        wide = _dot_sel(narrow, exp_ref[d])
        la_s[d] = wide[:, 0:GROUP_W]
        be_s[d] = wide[:, GROUP_W:2 * GROUP_W]
        if has_s0:
            st_s[d] = jnp.concatenate([s0_ref[d, h] for h in range(N_HEADS)], axis=-1)
        else:
            st_s[d] = jnp.zeros((HEAD_W, GROUP_W), F32)

    def prepare(cidx):
        r0 = pl.multiple_of(cidx * C, C)
        rows = pl.ds(r0, C)
        arow = pl.ds(pl.multiple_of(cidx * 8, 8), 8)
        q = q_s[rows, :]
        k = k_s[rows, :]
        v = v_s[rows, :]
        systems = []
        attns = []
        kts = []
        for d in range(2):
            incl = mask_ref[d, 0] > 0.5
            strict = mask_ref[d, 1]
            la = la_s[d, rows, :]
            be = be_s[d, rows, :]
            gx = _sel_dot(tril_ref[d], la)
            gtot = jnp.sum(la, axis=0, keepdims=True)
            eg = jnp.exp(gx)
            kout = k * jnp.exp(gtot - gx)
            qin_s[d, rows, :] = q * eg
            al_s[d, arow, :] = jnp.broadcast_to(jnp.exp(gtot), (8, GROUP_W))
            kb = k * be
            vb = v * be
            kbg = kb * eg
            for h in range(N_HEADS):
                sl = slice(h * HEAD_W, (h + 1) * HEAD_W)
                gh = gx[:, sl]
                dmat = gh - gh.T
                dec = jnp.where(incl, jnp.exp(jnp.where(incl, dmat, 0.0)), 0.0)
                qk = _bdot_nt(jnp.concatenate([kb[:, sl], q[:, sl]], axis=0), k[:, sl])
                nmat = -(qk[:C] * dec * strict)
                systems.append((jnp.concatenate([vb[:, sl], kbg[:, sl]], axis=-1), nmat))
                attns.append(qk[C:] * dec)
                kts.append(kout[:, sl].T)
        sols = _solve_unit_lower(systems)
        for d in range(2):
            for h in range(N_HEADS):
                xw_s[d, rows, h * LANE:(h + 1) * LANE] = sols[d * N_HEADS + h]
            at_s[d, rows, :] = jnp.concatenate(attns[d * N_HEADS:(d + 1) * N_HEADS], axis=-1)
            kt_s[d, rows, :] = jnp.concatenate(kts[d * N_HEADS:(d + 1) * N_HEADS], axis=-1)

    def prep_body(i, carry):
        for j in range(GDN_UNROLL):
            prepare(i * GDN_UNROLL + j)
        return carry

    lax.fori_loop(0, n // GDN_UNROLL, prep_body, 0)

    def chunk(ci, carry):
        heads = [(d, h) for d in range(2) for h in range(N_HEADS)]
        rows = []
        alast = []
        for d, cidx in ((0, ci), (1, n - 1 - ci)):
            rows.append(pl.ds(pl.multiple_of(cidx * C, C), C))
            alast.append(al_s[d, pl.ds(pl.multiple_of(cidx * 8, 8), 1), :])
        sts = [st_s[d][:, h * HEAD_W:(h + 1) * HEAD_W] for d, h in heads]
        xws = [xw_s[d, rows[d], h * LANE:(h + 1) * LANE] for d, h in heads]
        both = [_bdot(jnp.concatenate([xws[i][:, HEAD_W:], qin_s[d, rows[d], h * HEAD_W:(h + 1) * HEAD_W]], axis=0),
                      sts[i]) for i, (d, h) in enumerate(heads)]
        vnew = [xws[i][:, :HEAD_W] - both[i][:C] for i in range(len(heads))]
        upd = [_bdot(jnp.concatenate([at_s[d, rows[d], h * HEAD_W:(h + 1) * HEAD_W],
                                      kt_s[d, rows[d], h * HEAD_W:(h + 1) * HEAD_W]], axis=0), vnew[i])
               for i, (d, h) in enumerate(heads)]
        for d in range(2):
            idx = range(d * N_HEADS, (d + 1) * N_HEADS)
            of_s[d, rows[d], :] = jnp.concatenate([both[i][C:] + upd[i][:C] for i in idx], axis=-1)
            st_s[d] = jnp.concatenate(
                [sts[i] * alast[d][:, (i - d * N_HEADS) * HEAD_W:(i - d * N_HEADS + 1) * HEAD_W] + upd[i][C:]
                 for i in idx], axis=-1)
        return carry

    lax.fori_loop(0, n, chunk, 0)
    o_ref[...] = _head_norm_gate(of_s[0] + of_s[1], bd, gn_ref[...], u_ref[:, 3 * GROUP_W:4 * GROUP_W])
    for d in range(2):
        for h in range(N_HEADS):
            sfin_ref[d, h] = st_s[d][:, h * HEAD_W:(h + 1) * HEAD_W]


def _gdn_call(u_gd, cw, alog, dtb, gn, consts, s0, l, T, nb, row0):
    tril, masks, expand, bd = consts
    tb = row0 // T
    has_s0 = s0 is not None
    in_specs = [
        pl.BlockSpec((T, GD_PAD), lambda b: (tb + b, 0)),
        pl.BlockSpec((3, 3 * GROUP_W), lambda b: (0, 0)),
        pl.BlockSpec((1, LANE), lambda b: (0, 0)),
        pl.BlockSpec((1, LANE), lambda b: (0, 0)),
        pl.BlockSpec((2, LANE, 2 * GROUP_W), lambda b: (0, 0, 0)),
        pl.BlockSpec((2, CHUNK, CHUNK), lambda b: (0, 0, 0)),
        pl.BlockSpec((2, 2, CHUNK, CHUNK), lambda b: (0, 0, 0, 0)),
        pl.BlockSpec((GROUP_W, GROUP_W), lambda b: (0, 0)),
        pl.BlockSpec((1, GROUP_W), lambda b: (0, 0)),
    ]
    args = [u_gd, cw, alog, dtb, expand, tril, masks, bd, gn]
    if has_s0:
        in_specs.append(pl.BlockSpec((None, None, 2, N_HEADS, HEAD_W, HEAD_W), lambda b: (b, l, 0, 0, 0, 0)))
        args.append(s0)
    seq = pltpu.VMEM((2, T, GROUP_W), F32)
    return pl.pallas_call(
        functools.partial(_gdn_kernel, T=T, has_s0=has_s0),
        grid=(nb,),
        in_specs=in_specs,
        out_specs=[
            pl.BlockSpec((T, GROUP_W), lambda b: (b, 0)),
            pl.BlockSpec((None, 2, N_HEADS, HEAD_W, HEAD_W), lambda b: (b, 0, 0, 0, 0)),
        ],
        out_shape=[jax.ShapeDtypeStruct((nb * T, GROUP_W), F32),
                   jax.ShapeDtypeStruct((nb, 2, N_HEADS, HEAD_W, HEAD_W), F32)],
        scratch_shapes=[pltpu.VMEM((T, GROUP_W), F32)] * 3 + [seq, seq,
            pltpu.VMEM((2, T, N_HEADS * LANE), F32), seq, seq, seq,
            pltpu.VMEM((2, T // CHUNK * 8, GROUP_W), F32), seq, pltpu.VMEM((2, HEAD_W, GROUP_W), F32)],
        compiler_params=_cparams(("parallel",)),
        name="gdn",
    )(*args)


def _hyfilt_kernel(z_ref, win_ref, fh_ref, fl_ref, w1_ref, b1_ref, fr_ref, w2_ref, b2_ref, w3_ref, o_ref, *, T):
    fr = fr_ref[...]
    h = jnp.sin(fr * (_dot3(z_ref[...], w1_ref[...]) + b1_ref[...]))
    h = jnp.sin(fr * (_dot3(h, w2_ref[...]) + b2_ref[...]))
    h = _dot3(h, w3_ref[...])
    win = win_ref[...]
    hf = h[:, 0:GROUP_W] * win
    hb = h[:, GROUP_W:2 * GROUP_W] * win
    row = lax.broadcasted_iota(jnp.int32, hb.shape, 0)
    hb = jnp.where(row == 0, 0.0, hb)
    sh, sl = _split2(jnp.concatenate([hf + hb, hf - hb], axis=-1))
    fh = fh_ref[...]
    spec = (jnp.dot(fh, sh, preferred_element_type=F32) + jnp.dot(fh, sl, preferred_element_type=F32)
            + jnp.dot(fl_ref[...], sh, preferred_element_type=F32))
    o_ref[0:T, :] = spec[0:T, 0:GROUP_W]
    o_ref[T:2 * T, :] = spec[T:2 * T, GROUP_W:2 * GROUP_W]


def _hyfilt_call(T, zp, win, fh, fl, w1p, b1, freq, w2, b2, w3):
    c2 = lambda l: (0, 0)
    return pl.pallas_call(
        functools.partial(_hyfilt_kernel, T=T),
        grid=(DEPTH,),
        in_specs=[
            pl.BlockSpec((T, LANE), c2),
            pl.BlockSpec((T, GROUP_W), c2),
            pl.BlockSpec((2 * T, T), c2),
            pl.BlockSpec((2 * T, T), c2),
            pl.BlockSpec((None, LANE, HY_FH), lambda l: (l, 0, 0)),
            pl.BlockSpec((None, 1, HY_FH), lambda l: (l, 0, 0)),
            pl.BlockSpec((None, 1, HY_FH), lambda l: (l, 0, 0)),
            pl.BlockSpec((None, HY_FH, HY_FH), lambda l: (l, 0, 0)),
            pl.BlockSpec((None, 1, HY_FH), lambda l: (l, 0, 0)),
            pl.BlockSpec((None, HY_FH, 2 * GROUP_W), lambda l: (l, 0, 0)),
        ],
        out_specs=pl.BlockSpec((None, 2 * T, GROUP_W), lambda l: (l, 0, 0)),
        out_shape=jax.ShapeDtypeStruct((DEPTH, 2 * T, GROUP_W), F32),
        compiler_params=_cparams(("parallel",)),
        name="hyfilt",
    )(zp, win, fh, fl, w1p, b1, freq, w2, b2, w3)


def _hyena_kernel(u_ref, cw_ref, cb_ref, spec_ref, skip_ref, fh_ref, fl_ref, ih_ref, il_ref, o_ref, *, T):
    uc = _conv3(u_ref[...], cw_ref, T) + cb_ref[...]
    x0 = uc[:, 0:GROUP_W]
    z = uc[:, GROUP_W:2 * GROUP_W] * uc[:, 2 * GROUP_W:3 * GROUP_W]
    zh, zl = _split2(z)
    fh = fh_ref[...]
    zs = (jnp.dot(fh, zh, preferred_element_type=F32) + jnp.dot(fh, zl, preferred_element_type=F32)
          + jnp.dot(fl_ref[...], zh, preferred_element_type=F32))
    ar, ai = zs[0:T], zs[T:2 * T]
    br, bi = spec_ref[0:T, :], spec_ref[T:2 * T, :]
    ph, plo = _split2(jnp.concatenate([ar * br - ai * bi, ar * bi + ai * br], axis=0))
    ih = ih_ref[...]
    y = (jnp.dot(ih, ph, preferred_element_type=F32) + jnp.dot(ih, plo, preferred_element_type=F32)
         + jnp.dot(il_ref[...], ph, preferred_element_type=F32))
    o_ref[...] = x0 * (y + z * skip_ref[...])


def _hyena_call(u_hy, cw, cb, spec, skip, dft, l, T, nb, row0):
    fh, fl, ih, il = dft
    tb = row0 // T
    c2 = lambda b: (0, 0)
    return pl.pallas_call(
        functools.partial(_hyena_kernel, T=T),
        grid=(nb,),
        in_specs=[
            pl.BlockSpec((T, HY_COLS), lambda b: (tb + b, 0)),
            pl.BlockSpec((3, HY_COLS), c2),
            pl.BlockSpec((1, HY_COLS), c2),
            pl.BlockSpec((None, 2 * T, GROUP_W), lambda b: (l, 0, 0)),
            pl.BlockSpec((1, GROUP_W), c2),
            pl.BlockSpec((2 * T, T), c2),
            pl.BlockSpec((2 * T, T), c2),
            pl.BlockSpec((T, 2 * T), c2),
            pl.BlockSpec((T, 2 * T), c2),
        ],
        out_specs=pl.BlockSpec((T, GROUP_W), lambda b: (b, 0)),
        out_shape=jax.ShapeDtypeStruct((nb * T, GROUP_W), F32),
        compiler_params=_cparams(("parallel",)),
        name="hyena",
    )(u_hy, cw, cb, spec, skip, fh, fl, ih, il)


def _rope(x, cosf, sinf):
    lane = lax.broadcasted_iota(jnp.int32, x.shape, 1)
    half = MLA_ROPE // 2
    partner = jnp.where(lane < MLA_NOPE + half, pltpu.roll(x, LANE - half, 1), pltpu.roll(x, half, 1))
    return x * cosf + partner * sinf


def _qk_norm(x, g):
    ms = jnp.sum(x * x, axis=-1, keepdims=True) * (1.0 / MLA_QK)
    return x * lax.rsqrt(ms + RMS_EPS) * g


def _mla_kernel(*refs, T, ctx):
    if ctx:
        (u_ref, qn_ref, wq_ref, kvn_ref, wkv_ref, qkn_ref, cos_ref, sin_ref, cckv_ref, ckr_ref, o_ref) = refs
    else:
        (u_ref, qn_ref, wq_ref, kvn_ref, wkv_ref, qkn_ref, o_ref, ckv_ref, kr_ref) = refs
    u = u_ref[...]
    cq = _rms(u[:, 0:MLA_Q_LORA], qn_ref[...])
    ckv = _rms(u[:, MLA_Q_LORA:MLA_Q_LORA + MLA_KV_LORA], kvn_ref[...])
    kr = u[:, MLA_Q_LORA + MLA_KV_LORA:MLA_Q_LORA + MLA_KV_LORA + MLA_ROPE]
    if not ctx:
        ckv_ref[...] = ckv
        kr_ref[...] = kr
    q_all = _bdot(cq, wq_ref[...])
    kv = _bdot(ckv, wkv_ref[...])
    gq = qkn_ref[0:1, :]
    gk = qkn_ref[1:2, :]
    if ctx:
        kvc = _bdot(cckv_ref[...], wkv_ref[...])
        krc = ckr_ref[...]
        cosf, sinf = cos_ref[...], sin_ref[...]
    scale = MLA_QK ** -0.5
    outs = []
    for h in range(N_HEADS):
        qh = _qk_norm(q_all[:, h * LANE:(h + 1) * LANE], gq)
        zpad = jnp.zeros((T, LANE - MLA_QK), F32)
        kh = _qk_norm(jnp.concatenate([kv[:, h * HEAD_W:(h + 1) * HEAD_W], kr, zpad], axis=-1), gk)
        vh = kv[:, GROUP_W + h * HEAD_W:GROUP_W + (h + 1) * HEAD_W]
        if ctx:
            qh = _rope(qh, cosf, sinf)
            kh = _rope(kh, cosf, sinf)
            s_len = krc.shape[0]
            zc = jnp.zeros((s_len, LANE - MLA_QK), F32)
            kc = _qk_norm(jnp.concatenate([kvc[:, h * HEAD_W:(h + 1) * HEAD_W], krc, zc], axis=-1), gk)
            kh = jnp.concatenate([kh, kc], axis=0)
            vh = jnp.concatenate([vh, kvc[:, GROUP_W + h * HEAD_W:GROUP_W + (h + 1) * HEAD_W]], axis=0)
        khb = kh.astype(BF16)
        vhb = vh.astype(BF16)
        blocks = []
        for qb in range(T // ATT_QBLOCK):
            s = _bdot_nt(qh[qb * ATT_QBLOCK:(qb + 1) * ATT_QBLOCK], khb) * scale
            e = jnp.exp(s - jnp.max(s, axis=-1, keepdims=True))
            blocks.append(_bdot(e, vhb) / jnp.sum(e, axis=-1, keepdims=True))
        outs.append(blocks[0] if len(blocks) == 1 else jnp.concatenate(blocks, axis=0))
    o_ref[...] = jnp.concatenate(outs, axis=-1)


def _mla_call(u_mla, qn, wq, kvn, wkv, qkn, rope, cache, l, T, nb, row0):
    tb = row0 // T
    ctx = cache is not None
    c2 = lambda b: (0, 0)
    in_specs = [
        pl.BlockSpec((T, MLA_PAD), lambda b: (tb + b, 0)),
        pl.BlockSpec((1, MLA_Q_LORA), c2),
        pl.BlockSpec((MLA_Q_LORA, N_HEADS * LANE), c2),
        pl.BlockSpec((1, MLA_KV_LORA), c2),
        pl.BlockSpec((MLA_KV_LORA, 2 * GROUP_W), c2),
        pl.BlockSpec((2, LANE), c2),
    ]
    args = [u_mla, qn, wq, kvn, wkv, qkn]
    out_specs = [pl.BlockSpec((T, GROUP_W), lambda b: (b, 0))]
    out_shape = [jax.ShapeDtypeStruct((nb * T, GROUP_W), F32)]
    if ctx:
        in_specs += [
            pl.BlockSpec((T, LANE), c2),
            pl.BlockSpec((T, LANE), c2),
            pl.BlockSpec((None, None, PAST_LEN, MLA_KV_LORA), lambda b: (b, l, 0, 0)),
            pl.BlockSpec((None, None, PAST_LEN, MLA_ROPE), lambda b: (b, l, 0, 0)),
        ]
        args += [rope[0], rope[1], cache[0], cache[1]]
    else:
        out_specs += [pl.BlockSpec((T, MLA_KV_LORA), lambda b: (b, 0)),
                      pl.BlockSpec((T, MLA_ROPE), lambda b: (b, 0))]
        out_shape += [jax.ShapeDtypeStruct((nb * T, MLA_KV_LORA), F32),
                      jax.ShapeDtypeStruct((nb * T, MLA_ROPE), F32)]
    return pl.pallas_call(
        functools.partial(_mla_kernel, T=T, ctx=ctx),
        grid=(nb,),
        in_specs=in_specs,
        out_specs=out_specs,
        out_shape=out_shape,
        compiler_params=_cparams(("parallel",)),
        name="mla",
    )(*args)


def _pad_cols(w, width):
    return jnp.pad(w, [(0, 0)] * (w.ndim - 1) + [(0, width - w.shape[-1])])


def _prep_w_in(w_in):
    o1 = HG_COLS
    o2 = o1 + HY_COLS
    o3 = o2 + MLA_COLS
    return (w_in[..., :o1].astype(BF16), w_in[..., o1:o2].astype(BF16),
            _pad_cols(w_in[..., o2:o3], MLA_PAD).astype(BF16), _pad_cols(w_in[..., o3:], GD_PAD).astype(BF16))


def _prep_wq(w_q_up):
    w = w_q_up.reshape(DEPTH, MLA_Q_LORA, N_HEADS, MLA_QK)
    return _pad_cols(w, LANE).reshape(DEPTH, MLA_Q_LORA, N_HEADS * LANE).astype(BF16)


def _prep_wkv(w_kv_up):
    w = w_kv_up.reshape(DEPTH, MLA_KV_LORA, N_HEADS, 2, HEAD_W)
    return w.transpose(0, 1, 3, 2, 4).reshape(DEPTH, MLA_KV_LORA, 2 * GROUP_W).astype(BF16)


def _lower_bounds(hgrn_lb):
    lb = jnp.cumsum(jax.nn.softmax(hgrn_lb.astype(F32), axis=0), axis=0)
    return lb - lb[0]


def kernel(x_prompt, x_sample, cache_mla_ckv, cache_mla_krope, state_hgrn, state_gdn, c, c_ctx, w_ada, b_ada, norm_ffn, w_ffn_gu, w_ffn_down, norm_mix, w_in, w_out, hgrn_lb, hgrn_norm, hy_conv_w, hy_conv_b, hy_w1, hy_b1, hy_freq, hy_w2, hy_b2, hy_w3, hy_skip, mla_q_norm_a, mla_w_q_up, mla_kv_norm_a, mla_w_kv_up, mla_qk_norm, gdn_conv_w, gdn_a_log, gdn_dt_bias, gdn_norm):
    x = jnp.concatenate([x_prompt.reshape(N_PROMPT, D_MODEL), x_sample.reshape(N_SAMPLE, D_MODEL)], axis=0)

    cond8 = jnp.zeros((8, D_MODEL), F32).at[0].set(c_ctx).at[1:1 + DEC_BATCH].set(c)
    ada = _ada_call(cond8, w_ada, b_ada)

    w_in_parts = _prep_w_in(w_in)
    w_out_bf = w_out.astype(BF16)
    wq = _prep_wq(mla_w_q_up)
    wkv = _prep_wkv(mla_w_kv_up)
    qkn = _pad_cols(mla_qk_norm, LANE)
    lb_all = _lower_bounds(hgrn_lb)
    alog = _pad_cols(gdn_a_log.reshape(DEPTH, 1, 8), LANE)
    dtb = _pad_cols(gdn_dt_bias.reshape(DEPTH, 1, 8), LANE)
    gdn_gn = jnp.tile(gdn_norm, (1, N_HEADS)).reshape(DEPTH, 1, GROUP_W)
    w1p = jnp.pad(hy_w1, ((0, 0), (0, LANE - HY_EMB), (0, 0)))

    bd = jnp.asarray(_block_diag_ones(), BF16)
    hg_tril, hg_m = _hgrn_consts()
    hg_consts = (jnp.asarray(hg_tril, BF16), jnp.asarray(hg_m, F32), bd)
    gd_tril, gd_masks, gd_expand = _gdn_consts()
    gd_consts = (jnp.asarray(gd_tril, BF16), jnp.asarray(gd_masks, F32), jnp.asarray(gd_expand, BF16), bd)
    rope = tuple(jnp.asarray(a) for a in _rope_consts(DEC_SEQ))
    groups = ((SEQ, BATCH, 0), (DEC_SEQ, DEC_BATCH, N_PROMPT))
    dft = {}
    spec = {}
    for T, _, _ in groups:
        fwd, inv = _dft_consts(T)
        fh, fl = _np_split2(fwd)
        ih, il = _np_split2(inv)
        dft[T] = (fh, fl, ih, il)
        zp, win = _hyena_pos_consts(T)
        spec[T] = _hyfilt_call(T, jnp.asarray(zp), jnp.asarray(win), fh, fl, w1p,
                               hy_b1.reshape(DEPTH, 1, HY_FH), hy_freq.reshape(DEPTH, 1, HY_FH), hy_w2,
                               hy_b2.reshape(DEPTH, 1, HY_FH), hy_w3)

    new_ckv, new_kr, new_hg, new_gd = [], [], [], []
    for l in range(DEPTH):
        x = _ffn_call(x, ada, norm_ffn, w_ffn_gu, w_ffn_down, l, 0)
        u_hg, u_hy, u_mla, u_gd = _inproj_call(x, ada, norm_mix, w_in_parts, l)
        outs = []
        for gi, (T, nb, row0) in enumerate(groups):
            latent = gi == 1
            o_hg, s_hg = _hgrn_call(u_hg, lb_all[l], hgrn_norm[l], hg_consts,
                                    state_hgrn if latent else None, l, T, nb, row0)
            o_hy = _hyena_call(u_hy, hy_conv_w[l], hy_conv_b[l].reshape(1, HY_COLS), spec[T],
                               hy_skip[l].reshape(1, GROUP_W), dft[T], l, T, nb, row0)
            mla = _mla_call(u_mla, mla_q_norm_a[l].reshape(1, MLA_Q_LORA), wq[l],
                            mla_kv_norm_a[l].reshape(1, MLA_KV_LORA), wkv[l], qkn[l],
                            rope if latent else None,
                            (cache_mla_ckv, cache_mla_krope) if latent else None, l, T, nb, row0)
            o_gd, s_gd = _gdn_call(u_gd, gdn_conv_w[l], alog[l], dtb[l], gdn_gn[l], gd_consts,
                                   state_gdn if latent else None, l, T, nb, row0)
            outs.append((o_hg, o_hy, mla[0], o_gd))
            if not latent:
                new_ckv.append(mla[1].reshape(BATCH, SEQ, MLA_KV_LORA))
                new_kr.append(mla[2].reshape(BATCH, SEQ, MLA_ROPE))
                new_hg.append(s_hg)
                new_gd.append(s_gd)
        x = _outproj_call(x, ada, w_out_bf, outs[0], outs[1], l)
        x = _ffn_call(x, ada, norm_ffn, w_ffn_gu, w_ffn_down, l, 1)

    y_prompt = x[:N_PROMPT].reshape(BATCH, SEQ, D_MODEL)
    y_sample = x[N_PROMPT:].reshape(DEC_BATCH, DEC_SEQ, D_MODEL)
    return (y_prompt, y_sample, jnp.stack(new_ckv, axis=1), jnp.stack(new_kr, axis=1),
            jnp.stack(new_hg, axis=1), jnp.stack(new_gd, axis=1))
```

```python
import functools
import math

import numpy as np
import jax
import jax.numpy as jnp
from jax import lax
from jax.experimental import pallas as pl
from jax.experimental.pallas import tpu as pltpu

F32 = jnp.float32
BF16 = jnp.bfloat16

D_MODEL = 1024
BATCH = 16
SEQ = 256
DEPTH = 4
DEC_BATCH = 2
DEC_SEQ = 1024
PAST_LEN = 256
GRID_W = 64
N_ADA = 9
D_FF = 2816
GROUP_W = 256
CHUNK = 64
RMS_EPS = 1e-6
N_HEADS = 4
HEAD_W = 64
HY_EMB = 33
HY_FH = 64
HY_TARGET = 1e-2
HY_FAST = 0.3
HY_SLOW = 1.5
MLA_NOPE = 64
MLA_ROPE = 32
MLA_QK = MLA_NOPE + MLA_ROPE
MLA_Q_LORA = 256
MLA_KV_LORA = 128
ROPE_BASE = 10000.0

HG_COLS = 5 * GROUP_W
HY_COLS = 3 * GROUP_W
MLA_COLS = MLA_Q_LORA + MLA_KV_LORA + MLA_ROPE
GD_COLS = 4 * GROUP_W + 16
MLA_PAD = 512
GD_PAD = 1152

N_PROMPT = BATCH * SEQ
N_SAMPLE = DEC_BATCH * DEC_SEQ
N_TOK = N_PROMPT + N_SAMPLE
LANE = 128
VMEM_LIMIT = 56 * 1024 * 1024
ROW_TILE = 1024
FF_TILE = 256
ADA_TILE = 1536
ATT_QBLOCK = 256


def _bdot(a, b):
    return jnp.dot(a.astype(BF16), b.astype(BF16), preferred_element_type=F32)


def _bdot_nt(a, b):
    return lax.dot_general(a.astype(BF16), b.astype(BF16), (((1,), (1,)), ((), ())),
                           preferred_element_type=F32)


def _bdot_tn(a, b):
    return lax.dot_general(a.astype(BF16), b.astype(BF16), (((0,), (0,)), ((), ())),
                           preferred_element_type=F32)


def _split2(x):
    hi = x.astype(BF16)
    lo = (x - hi.astype(F32)).astype(BF16)
    return hi, lo


def _split3(x):
    hi = x.astype(BF16)
    r = x - hi.astype(F32)
    mid = r.astype(BF16)
    lo = (r - mid.astype(F32)).astype(BF16)
    return hi, mid, lo


def _dot3(a, b):
    ah, al = _split2(a)
    bh, bl = _split2(b)
    return (jnp.dot(ah, bh, preferred_element_type=F32) + jnp.dot(ah, bl, preferred_element_type=F32)
            + jnp.dot(al, bh, preferred_element_type=F32))


def _sel_dot(c, x):
    h, m, l = _split3(x)
    return (jnp.dot(c, h, preferred_element_type=F32) + jnp.dot(c, m, preferred_element_type=F32)
            + jnp.dot(c, l, preferred_element_type=F32))


def _dot_sel(x, c):
    h, m, l = _split3(x)
    return (jnp.dot(h, c, preferred_element_type=F32) + jnp.dot(m, c, preferred_element_type=F32)
            + jnp.dot(l, c, preferred_element_type=F32))


def _sigmoid(x):
    return 1.0 / (1.0 + jnp.exp(-x))


def _silu(x):
    return x * _sigmoid(x)


def _rms(x, g):
    return x * lax.rsqrt(jnp.mean(x * x, axis=-1, keepdims=True) + RMS_EPS) * g


def _cparams(sem):
    return pltpu.CompilerParams(dimension_semantics=sem, vmem_limit_bytes=VMEM_LIMIT)


def _cond_of_tile(i):
    return jnp.maximum(i - (N_PROMPT // ROW_TILE - 1), 0)


def _ada_kernel(c_ref, w_ref, b_ref, o_ref):
    o_ref[...] = _dot3(_silu(c_ref[...]), w_ref[...]) + b_ref[...]


def _ada_call(cond8, w_ada, b_ada):
    n = N_ADA * D_MODEL
    out = pl.pallas_call(
        _ada_kernel,
        grid=(DEPTH, n // ADA_TILE),
        in_specs=[
            pl.BlockSpec((8, D_MODEL), lambda l, j: (0, 0)),
            pl.BlockSpec((None, D_MODEL, ADA_TILE), lambda l, j: (l, 0, j)),
            pl.BlockSpec((None, 1, ADA_TILE), lambda l, j: (l, 0, j)),
        ],
        out_specs=pl.BlockSpec((None, 8, ADA_TILE), lambda l, j: (l, 0, j)),
        out_shape=jax.ShapeDtypeStruct((DEPTH, 8, n), F32),
        compiler_params=_cparams(("parallel", "parallel")),
        name="ada",
    )(cond8, w_ada, b_ada.reshape(DEPTH, 1, n))
    return out.reshape(DEPTH, 8, N_ADA, D_MODEL)


FFN_SUBTILES = 2


def _ffn_kernel(*refs, sub):
    x_ref = refs[0]
    ada_refs = refs[1:1 + FFN_SUBTILES]
    g_ref, wg_ref, wu_ref, wd_ref, o_ref, h_scr = refs[1 + FFN_SUBTILES:]
    f = pl.program_id(1)

    @pl.when(f == 0)
    def _():
        for r, ada_ref in enumerate(ada_refs):
            rows = slice(r * ROW_TILE, (r + 1) * ROW_TILE)
            y = _rms(x_ref[rows, :], g_ref[...])
            h = y * (1.0 + ada_ref[3 * sub + 1:3 * sub + 2, :]) + ada_ref[3 * sub:3 * sub + 1, :]
            h_scr[rows, :] = h.astype(BF16)
        o_ref[...] = jnp.zeros_like(o_ref)

    wg = wg_ref[...].astype(BF16)
    wu = wu_ref[...].astype(BF16)
    wd = wd_ref[...].astype(BF16)
    for r in range(FFN_SUBTILES):
        rows = slice(r * ROW_TILE, (r + 1) * ROW_TILE)
        h = h_scr[rows, :]
        gate = jnp.dot(h, wg, preferred_element_type=F32)
        up = jnp.dot(h, wu, preferred_element_type=F32)
        a = (_silu(gate) * up).astype(BF16)
        o_ref[rows, :] += jnp.dot(a, wd, preferred_element_type=F32)

    @pl.when(f == pl.num_programs(1) - 1)
    def _():
        for r, ada_ref in enumerate(ada_refs):
            rows = slice(r * ROW_TILE, (r + 1) * ROW_TILE)
            o_ref[rows, :] = x_ref[rows, :] + 0.5 * ada_ref[3 * sub + 2:3 * sub + 3, :] * o_ref[rows, :]


def _ffn_call(x, ada, norm_ffn, w_gu, w_down, l, j):
    sub = 2 * j
    nf = D_FF // FF_TILE
    rows = FFN_SUBTILES * ROW_TILE

    def ada_spec(r):
        return pl.BlockSpec((None, None, N_ADA, D_MODEL),
                            lambda i, f: (l, _cond_of_tile(i * FFN_SUBTILES + r), 0, 0))

    return pl.pallas_call(
        functools.partial(_ffn_kernel, sub=sub),
        grid=(N_TOK // rows, nf),
        in_specs=[pl.BlockSpec((rows, D_MODEL), lambda i, f: (i, 0), pipeline_mode=pl.Buffered(1))]
        + [ada_spec(r) for r in range(FFN_SUBTILES)] + [
            pl.BlockSpec((None, None, 1, D_MODEL), lambda i, f: (l, j, 0, 0)),
            pl.BlockSpec((None, None, D_MODEL, FF_TILE), lambda i, f: (l, j, 0, f)),
            pl.BlockSpec((None, None, D_MODEL, FF_TILE), lambda i, f: (l, j, 0, nf + f)),
            pl.BlockSpec((None, None, FF_TILE, D_MODEL), lambda i, f: (l, j, f, 0)),
        ],
        out_specs=pl.BlockSpec((rows, D_MODEL), lambda i, f: (i, 0)),
        out_shape=jax.ShapeDtypeStruct((N_TOK, D_MODEL), F32),
        scratch_shapes=[pltpu.VMEM((rows, D_MODEL), BF16)],
        compiler_params=_cparams(("parallel", "arbitrary")),
        name="ffn",
    )(x, *([ada] * FFN_SUBTILES), norm_ffn.reshape(DEPTH, 2, 1, D_MODEL), w_gu, w_gu, w_down)


IN_TILE = 512


def _inproj_kernel(x_ref, ada_ref, g_ref, w1, w2, w3, w4, o1, o2, o3, o4):
    y = _rms(x_ref[...], g_ref[...])
    h = (y * (1.0 + ada_ref[4:5, :]) + ada_ref[3:4, :]).astype(BF16)
    for w, o in ((w1, o1), (w2, o2), (w3, o3), (w4, o4)):
        o[...] = jnp.dot(h, w[...], preferred_element_type=F32)


def _inproj_call(x, ada, norm_mix, ws, l):
    widths = (HG_COLS, HY_COLS, MLA_PAD, GD_PAD)
    per = ROW_TILE // IN_TILE
    return pl.pallas_call(
        _inproj_kernel,
        grid=(N_TOK // IN_TILE,),
        in_specs=[
            pl.BlockSpec((IN_TILE, D_MODEL), lambda i: (i, 0)),
            pl.BlockSpec((None, None, N_ADA, D_MODEL), lambda i: (l, _cond_of_tile(i // per), 0, 0)),
            pl.BlockSpec((None, 1, D_MODEL), lambda i: (l, 0, 0)),
        ] + [pl.BlockSpec((None, D_MODEL, w), lambda i: (l, 0, 0)) for w in widths],
        out_specs=[pl.BlockSpec((IN_TILE, w), lambda i: (i, 0)) for w in widths],
        out_shape=[jax.ShapeDtypeStruct((N_TOK, w), F32) for w in widths],
        compiler_params=_cparams(("parallel",)),
        name="inproj",
    )(x, ada, norm_mix.reshape(DEPTH, 1, D_MODEL), *ws)


OUT_TILE = 512


def _outproj_kernel(x_ref, ada_ref, w_ref, *refs):
    o_ref = refs[-1]
    i = pl.program_id(0)
    n_p = N_PROMPT // OUT_TILE

    def run(srcs):
        acc = jnp.zeros((OUT_TILE, D_MODEL), F32)
        for g, s in enumerate(srcs):
            acc += jnp.dot(s[...].astype(BF16), w_ref[g * GROUP_W:(g + 1) * GROUP_W, :],
                           preferred_element_type=F32)
        o_ref[...] = x_ref[...] + ada_ref[5:6, :] * acc

    @pl.when(i < n_p)
    def _():
        run(refs[0:4])

    @pl.when(i >= n_p)
    def _():
        run(refs[4:8])


def _outproj_call(x, ada, w_out_bf, o_p, o_s, l):
    per = ROW_TILE // OUT_TILE
    n_p = N_PROMPT // OUT_TILE
    n_s = N_SAMPLE // OUT_TILE
    return pl.pallas_call(
        _outproj_kernel,
        grid=(N_TOK // OUT_TILE,),
        in_specs=[
            pl.BlockSpec((OUT_TILE, D_MODEL), lambda i: (i, 0)),
            pl.BlockSpec((None, None, N_ADA, D_MODEL), lambda i: (l, _cond_of_tile(i // per), 0, 0)),
            pl.BlockSpec((None, D_MODEL, D_MODEL), lambda i: (l, 0, 0)),
        ] + [pl.BlockSpec((OUT_TILE, GROUP_W), lambda i: (jnp.minimum(i, n_p - 1), 0))] * 4
          + [pl.BlockSpec((OUT_TILE, GROUP_W), lambda i: (jnp.clip(i - n_p, 0, n_s - 1), 0))] * 4,
        out_specs=pl.BlockSpec((OUT_TILE, D_MODEL), lambda i: (i, 0)),
        out_shape=jax.ShapeDtypeStruct((N_TOK, D_MODEL), F32),
        compiler_params=_cparams(("parallel",)),
        name="outproj",
    )(x, ada, w_out_bf, *o_p, *o_s)


def _block_diag_ones():
    idx = np.arange(GROUP_W) // HEAD_W
    return (idx[:, None] == idx[None, :]).astype(np.float32)


def _hgrn_consts():
    C = CHUNK
    i = np.arange(C)[:, None]
    j = np.arange(C)[None, :]
    masks = []
    s = C // 2
    while s >= 1:
        up_i = (i // s) % 2 == 1
        up_j = (j // s) % 2 == 1
        masks.append(up_i & (~up_j) & (i // (2 * s) == j // (2 * s)))
        s //= 2
    fwd_m = np.stack([m.astype(np.float32) for m in masks])
    bwd_m = np.stack([m.astype(np.float32)[::-1, ::-1] for m in masks])
    tril = np.stack([(j <= i), (j >= i)]).astype(np.float32)
    return tril, np.tile(np.stack([fwd_m, bwd_m]), (1, 1, 1, 2))


def _gdn_consts():
    C = CHUNK
    i = np.arange(C)[:, None]
    t = np.arange(C)[None, :]
    tril = np.stack([(t <= i), (t >= i)]).astype(np.float32)
    masks = np.stack([np.stack([(t <= i), (t < i)]), np.stack([(t >= i), (t > i)])]).astype(np.float32)
    expand = np.zeros((2, LANE, 2 * GROUP_W), np.float32)
    for d in range(2):
        for h in range(N_HEADS):
            expand[d, d * N_HEADS + h, h * HEAD_W:(h + 1) * HEAD_W] = 1.0
            expand[d, 8 + d * N_HEADS + h, GROUP_W + h * HEAD_W:GROUP_W + (h + 1) * HEAD_W] = 1.0
    return tril, masks, expand


def _dft_consts(T):
    n2 = 4 * T
    k = np.arange(T, dtype=np.int64)[:, None]
    s = np.arange(T, dtype=np.int64)[None, :]
    ang = np.pi * (((2 * k + 1) * s) % n2).astype(np.float64) / (2 * T)
    fwd = np.concatenate([np.cos(ang), -np.sin(ang)], axis=0)
    inv = fwd.T / T
    return fwd.astype(np.float32), inv.astype(np.float32)


def _np_split2(x):
    hi = jnp.asarray(x, F32).astype(BF16)
    lo = (jnp.asarray(x, F32) - hi.astype(F32)).astype(BF16)
    return hi, lo


def _hyena_pos_consts(T):
    pos = np.arange(T, dtype=np.float32)
    t = pos / np.float32(T - 1)
    bands = np.linspace(1e-4, (HY_EMB - 1) // 2 - 1, (HY_EMB - 1) // 2, dtype=np.float32)
    ang = (np.float32(2.0 * math.pi / T) * pos[:, None]) * bands[None, :]
    z = np.concatenate([t[:, None], np.cos(ang), -np.sin(ang)], axis=-1).astype(np.float32)
    zp = np.zeros((T, LANE), np.float32)
    zp[:, :HY_EMB] = z
    max_decay = math.log(HY_TARGET) / HY_FAST
    min_decay = math.log(HY_TARGET) / HY_SLOW
    deltas = np.linspace(min_decay, max_decay, GROUP_W, dtype=np.float32)
    window = np.exp(-t[:, None] * np.abs(deltas)[None, :]).astype(np.float32)
    return zp, window


def _rope_consts(T):
    rows = T // GRID_W
    row = np.repeat(np.arange(rows, dtype=np.float32), GRID_W)
    col = (np.arange(T) % GRID_W).astype(np.float32)
    pairs = MLA_ROPE // 4
    inv = (np.float32(ROPE_BASE) ** (-np.arange(pairs, dtype=np.float32) / np.float32(pairs))).astype(np.float32)
    ang = np.concatenate([row[:, None] * inv, col[:, None] * inv], axis=-1).astype(np.float32)
    cos, sin = np.cos(ang), np.sin(ang)
    cosf = np.ones((T, LANE), np.float32)
    sinf = np.zeros((T, LANE), np.float32)
    half = MLA_ROPE // 2
    cosf[:, MLA_NOPE:MLA_NOPE + half] = cos
    cosf[:, MLA_NOPE + half:MLA_QK] = cos
    sinf[:, MLA_NOPE:MLA_NOPE + half] = -sin
    sinf[:, MLA_NOPE + half:MLA_QK] = sin
    return cosf, sinf


def _head_norm_gate(tot, bd, gn, gate):
    ms = _sel_dot_right(tot * tot, bd) * (1.0 / HEAD_W)
    return tot * lax.rsqrt(ms + RMS_EPS) * gn * _silu(gate)


def _sel_dot_right(x, c):
    h, l = _split2(x)
    return jnp.dot(h, c, preferred_element_type=F32) + jnp.dot(l, c, preferred_element_type=F32)


def _block_ref(b, two_s, r):
    C, W = b.shape
    if two_s % 8 == 0:
        b3 = b.reshape(C // two_s, two_s, W)
        return jnp.broadcast_to(b3[:, r:r + 1, :], b3.shape).reshape(C, W)
    pos = lax.broadcasted_iota(jnp.int32, b.shape, 0) % two_s
    out = b
    for p in range(two_s):
        if p != r:
            out = jnp.where(pos == p, pltpu.roll(b, (p - r) % C, 0), out)
    return out


N_PAIRS = N_HEADS // 2


def _pair_blockdiag(x):
    lane = lax.broadcasted_iota(jnp.int32, x.shape, 1)
    zero = jnp.zeros_like(x)
    return jnp.concatenate([jnp.where(lane < HEAD_W, x, zero), jnp.where(lane >= HEAD_W, x, zero)], axis=0)


def _hgrn_kernel(*refs, T, has_s0):
    if has_s0:
        (u_ref, lb_ref, gn_ref, tril_ref, lmask_ref, bd_ref, s0_ref,
         o_ref, sfin_ref, lf_s, kk_s, oi_s, qin_s, up_s, dc_s, st_s) = refs
    else:
        (u_ref, lb_ref, gn_ref, tril_ref, lmask_ref, bd_ref,
         o_ref, sfin_ref, lf_s, kk_s, oi_s, qin_s, up_s, dc_s, st_s) = refs
    n = T // CHUNK
    C = CHUNK
    bd = bd_ref[...]
    n_lv = int(math.log2(C))

    for d in range(2):
        z = u_ref[:, (3 + d) * GROUP_W:(4 + d) * GROUP_W]
        lb = lb_ref[d]
        log_sig = jnp.minimum(z, 0.0) - jnp.log(1.0 + jnp.exp(-jnp.abs(z)))
        a = jnp.log(lb)
        c = jnp.log(1.0 - lb) + log_sig
        m = jnp.maximum(a, c)
        lf_s[d] = m + jnp.log(1.0 + jnp.exp(jnp.minimum(a, c) - m))
        kk_s[d] = (1.0 - lb) * _sigmoid(-z)
        if has_s0:
            st_s[d] = jnp.concatenate([s0_ref[d, h].T for h in range(N_HEADS)], axis=-1)
        else:
            st_s[d] = jnp.zeros((HEAD_W, GROUP_W), F32)

    def prepare(cidx, carry):
        rows = pl.ds(pl.multiple_of(cidx * C, C), C)
        arow = pl.ds(pl.multiple_of(cidx * 8, 8), 8)
        q = u_ref[rows, 0:GROUP_W] * (HEAD_W ** -0.5)
        v = u_ref[rows, GROUP_W:2 * GROUP_W]
        vb = v.astype(BF16)
        vt = [jnp.concatenate([v[:, h * HEAD_W:(h + 1) * HEAD_W].T for h in (2 * p, 2 * p + 1)], axis=-1)
              for p in range(N_PAIRS)]
        v_bd = [_pair_blockdiag(vb[:, p * LANE:(p + 1) * LANE]) for p in range(N_PAIRS)]
        ql, kl, diag = [], [], []
        for d in range(2):
            lf = lf_s[d, rows, :]
            k = kk_s[d, rows, :]
            hi, lo = _split2(lf)
            tril = tril_ref[d]
            b = jnp.dot(tril, hi, preferred_element_type=F32) + jnp.dot(tril, lo, preferred_element_type=F32)
            tot = jnp.sum(lf, axis=0, keepdims=True)
            qin_s[d, rows, :] = q * jnp.exp(b)
            ko = (k * jnp.exp(tot - b)).astype(BF16)
            up_s[d, rows, :] = jnp.concatenate(
                [jnp.dot(vt[p].astype(BF16), _pair_blockdiag(ko[:, p * LANE:(p + 1) * LANE]),
                         preferred_element_type=F32) for p in range(N_PAIRS)], axis=-1)
            dc_s[d, arow, :] = jnp.broadcast_to(jnp.exp(tot), (8, GROUP_W))
            s = C // 2
            while s >= 1:
                e = jnp.exp(-jnp.abs(b - _block_ref(b, 2 * s, s - 1 if d == 0 else s)))
                ql.append((q * e).astype(BF16))
                kl.append((k * e).astype(BF16))
                s //= 2
            diag.append(_bdot(q * k, bd) * v)
        pairs = [(d, p) for d in range(2) for p in range(N_PAIRS)]
        sc = [jnp.zeros((C, LANE), F32) for _ in pairs]
        for lv in range(n_lv):
            for i, (d, p) in enumerate(pairs):
                sl = slice(p * LANE, (p + 1) * LANE)
                prod = lax.dot_general(ql[d * n_lv + lv][:, sl], _pair_blockdiag(kl[d * n_lv + lv][:, sl]),
                                       (((1,), (1,)), ((), ())), preferred_element_type=F32)
                sc[i] = sc[i] + lmask_ref[d, lv] * prod
        oi = [jnp.dot(sc[i].astype(BF16), v_bd[p], preferred_element_type=F32) for i, (d, p) in enumerate(pairs)]
        for d in range(2):
            oi_s[d, rows, :] = jnp.concatenate(oi[d * N_PAIRS:(d + 1) * N_PAIRS], axis=-1) + diag[d]
        return carry

    lax.fori_loop(0, n, prepare, 0, unroll=2)

    def chunk(ci, carry):
        rows = [pl.ds(pl.multiple_of(cidx * C, C), C) for cidx in (ci, n - 1 - ci)]
        decay = [dc_s[d, pl.ds(pl.multiple_of(cidx * 8, 8), 1), :] for d, cidx in ((0, ci), (1, n - 1 - ci))]
        for d in range(2):
            st = st_s[d]
            stb = st.astype(BF16)
            o_inter = [lax.dot_general(qin_s[d, rows[d], p * LANE:(p + 1) * LANE].astype(BF16),
                                       _pair_blockdiag(stb[:, p * LANE:(p + 1) * LANE]),
                                       (((1,), (1,)), ((), ())), preferred_element_type=F32)
                       for p in range(N_PAIRS)]
            oi_s[d, rows[d], :] = oi_s[d, rows[d], :] + jnp.concatenate(o_inter, axis=-1)
            st_s[d] = st * decay[d] + up_s[d, rows[d], :]
        return carry

    lax.fori_loop(0, n, chunk, 0)
    o_ref[...] = _head_norm_gate(oi_s[0] + oi_s[1], bd, gn_ref[...], u_ref[:, 2 * GROUP_W:3 * GROUP_W])
    for d in range(2):
        for h in range(N_HEADS):
            sfin_ref[d, h] = st_s[d][:, h * HEAD_W:(h + 1) * HEAD_W].T


def _hgrn_call(u_hg, lb_l, gn, consts, s0, l, T, nb, row0):
    tril, lmask, bd = consts
    tb = row0 // T
    has_s0 = s0 is not None
    in_specs = [
        pl.BlockSpec((T, HG_COLS), lambda b: (tb + b, 0)),
        pl.BlockSpec((2, 1, GROUP_W), lambda b: (0, 0, 0)),
        pl.BlockSpec((1, GROUP_W), lambda b: (0, 0)),
        pl.BlockSpec((2, CHUNK, CHUNK), lambda b: (0, 0, 0)),
        pl.BlockSpec((2, 6, CHUNK, LANE), lambda b: (0, 0, 0, 0)),
        pl.BlockSpec((GROUP_W, GROUP_W), lambda b: (0, 0)),
    ]
    args = [u_hg, lb_l.reshape(2, 1, GROUP_W), gn.reshape(1, GROUP_W), tril, lmask, bd]
    if has_s0:
        in_specs.append(pl.BlockSpec((None, None, 2, N_HEADS, HEAD_W, HEAD_W), lambda b: (b, l, 0, 0, 0, 0)))
        args.append(s0)
    seq = pltpu.VMEM((2, T, GROUP_W), F32)
    return pl.pallas_call(
        functools.partial(_hgrn_kernel, T=T, has_s0=has_s0),
        grid=(nb,),
        in_specs=in_specs,
        out_specs=[
            pl.BlockSpec((T, GROUP_W), lambda b: (b, 0)),
            pl.BlockSpec((None, 2, N_HEADS, HEAD_W, HEAD_W), lambda b: (b, 0, 0, 0, 0)),
        ],
        out_shape=[jax.ShapeDtypeStruct((nb * T, GROUP_W), F32),
                   jax.ShapeDtypeStruct((nb, 2, N_HEADS, HEAD_W, HEAD_W), F32)],
        scratch_shapes=[seq, seq, seq, seq, seq,
                        pltpu.VMEM((2, T // CHUNK * 8, GROUP_W), F32), pltpu.VMEM((2, HEAD_W, GROUP_W), F32)],
        compiler_params=_cparams(("parallel",)),
        name="hgrn",
    )(*args)


def _shift_rows(x, T):
    row = lax.broadcasted_iota(jnp.int32, x.shape, 0)
    prev = jnp.where(row == 0, 0.0, pltpu.roll(x, 1, 0))
    nxt = jnp.where(row == T - 1, 0.0, pltpu.roll(x, T - 1, 0))
    return prev, nxt


def _conv3(x, w_ref, T):
    prev, nxt = _shift_rows(x, T)
    return prev * w_ref[0:1, :] + x * w_ref[1:2, :] + nxt * w_ref[2:3, :]


GDN_UNROLL = 2


def _solve_unit_lower(systems):
    c2 = 2 * CHUNK
    slabs = [jnp.concatenate([nmat, nmat, rhs], axis=-1) for rhs, nmat in systems]
    steps = int(math.log2(CHUNK))
    for step in range(steps):
        last = step == steps - 1
        nxt = []
        for slab in slabs:
            hi = slab.astype(BF16)
            lo = (slab - hi.astype(F32)).astype(BF16)
            lhs = jnp.concatenate([hi[:, :c2], lo[:, :CHUNK]], axis=-1)
            first = c2 if last else 0
            rhs3 = jnp.concatenate([hi[:, first:], lo[:, first:], hi[:, first:]], axis=0)
            prod = jnp.dot(lhs, rhs3, preferred_element_type=F32)
            if last:
                nxt.append(slab[:, c2:] + prod)
            else:
                nxt.append(jnp.concatenate([prod[:, :c2], slab[:, c2:] + prod[:, c2:]], axis=-1))
        slabs = nxt
    return slabs


def _gdn_kernel(*refs, T, has_s0):
    if has_s0:
        (u_ref, cw_ref, alog_ref, dtb_ref, exp_ref, tril_ref, mask_ref, bd_ref, gn_ref, s0_ref,
         o_ref, sfin_ref, q_s, k_s, v_s, la_s, be_s, xw_s, at_s, qin_s, kt_s, al_s, of_s, st_s) = refs
    else:
        (u_ref, cw_ref, alog_ref, dtb_ref, exp_ref, tril_ref, mask_ref, bd_ref, gn_ref,
         o_ref, sfin_ref, q_s, k_s, v_s, la_s, be_s, xw_s, at_s, qin_s, kt_s, al_s, of_s, st_s) = refs
    n = T // CHUNK
    C = CHUNK
    bd = bd_ref[...]

    qkv = _silu(_conv3(u_ref[:, 0:3 * GROUP_W], cw_ref, T))
    q = qkv[:, 0:GROUP_W]
    k = qkv[:, GROUP_W:2 * GROUP_W]
    q_s[...] = q * lax.rsqrt(_sel_dot_right(q * q, bd) + 1e-6) * (HEAD_W ** -0.5)
    k_s[...] = k * lax.rsqrt(_sel_dot_right(k * k, bd) + 1e-6)
    v_s[...] = qkv[:, 2 * GROUP_W:3 * GROUP_W]

    ab = u_ref[:, 4 * GROUP_W:4 * GROUP_W + LANE]
    xa = ab + dtb_ref[...]
    softplus = jnp.maximum(xa, 0.0) + jnp.log(1.0 + jnp.exp(-jnp.abs(xa)))
    log_a = -jnp.exp(alog_ref[...]) * softplus
    lane = lax.broadcasted_iota(jnp.int32, ab.shape, 1)
    narrow = jnp.where(lane < 8, log_a, _sigmoid(ab))
    for d in range(2):
        wide = _dot_sel(narrow, exp_ref[d])
        la_s[d] = wide[:, 0:GROUP_W]
        be_s[d] = wide[:, GROUP_W:2 * GROUP_W]
        if has_s0:
            st_s[d] = jnp.concatenate([s0_ref[d, h] for h in range(N_HEADS)], axis=-1)
        else:
            st_s[d] = jnp.zeros((HEAD_W, GROUP_W), F32)

    def prepare(cidx):
        r0 = pl.multiple_of(cidx * C, C)
        rows = pl.ds(r0, C)
        arow = pl.ds(pl.multiple_of(cidx * 8, 8), 8)
        q = q_s[rows, :]
        k = k_s[rows, :]
        v = v_s[rows, :]
        systems = []
        attns = []
        kts = []
        for d in range(2):
            incl = mask_ref[d, 0] > 0.5
            strict = mask_ref[d, 1]
            la = la_s[d, rows, :]
            be = be_s[d, rows, :]
            gx = _sel_dot(tril_ref[d], la)
            gtot = jnp.sum(la, axis=0, keepdims=True)
            eg = jnp.exp(gx)
            kout = k * jnp.exp(gtot - gx)
            qin_s[d, rows, :] = q * eg
            al_s[d, arow, :] = jnp.broadcast_to(jnp.exp(gtot), (8, GROUP_W))
            kb = k * be
            vb = v * be
            kbg = kb * eg
            for h in range(N_HEADS):
                sl = slice(h * HEAD_W, (h + 1) * HEAD_W)
                gh = gx[:, sl]
                dmat = gh - gh.T
                dec = jnp.where(incl, jnp.exp(jnp.where(incl, dmat, 0.0)), 0.0)
                qk = _bdot_nt(jnp.concatenate([kb[:, sl], q[:, sl]], axis=0), k[:, sl])
                nmat = -(qk[:C] * dec * strict)
                systems.append((jnp.concatenate([vb[:, sl], kbg[:, sl]], axis=-1), nmat))
                attns.append(qk[C:] * dec)
                kts.append(kout[:, sl].T)
        sols = _solve_unit_lower(systems)
        for d in range(2):
            for h in range(N_HEADS):
                xw_s[d, rows, h * LANE:(h + 1) * LANE] = sols[d * N_HEADS + h]
            at_s[d, rows, :] = jnp.concatenate(attns[d * N_HEADS:(d + 1) * N_HEADS], axis=-1)
            kt_s[d, rows, :] = jnp.concatenate(kts[d * N_HEADS:(d + 1) * N_HEADS], axis=-1)

    def prep_body(i, carry):
        for j in range(GDN_UNROLL):
            prepare(i * GDN_UNROLL + j)
        return carry

    lax.fori_loop(0, n // GDN_UNROLL, prep_body, 0)

    def chunk(ci, carry):
        heads = [(d, h) for d in range(2) for h in range(N_HEADS)]
        rows = []
        alast = []
        for d, cidx in ((0, ci), (1, n - 1 - ci)):
            rows.append(pl.ds(pl.multiple_of(cidx * C, C), C))
            alast.append(al_s[d, pl.ds(pl.multiple_of(cidx * 8, 8), 1), :])
        sts = [st_s[d][:, h * HEAD_W:(h + 1) * HEAD_W] for d, h in heads]
        xws = [xw_s[d, rows[d], h * LANE:(h + 1) * LANE] for d, h in heads]
        both = [_bdot(jnp.concatenate([xws[i][:, HEAD_W:], qin_s[d, rows[d], h * HEAD_W:(h + 1) * HEAD_W]], axis=0),
                      sts[i]) for i, (d, h) in enumerate(heads)]
        vnew = [xws[i][:, :HEAD_W] - both[i][:C] for i in range(len(heads))]
        upd = [_bdot(jnp.concatenate([at_s[d, rows[d], h * HEAD_W:(h + 1) * HEAD_W],
                                      kt_s[d, rows[d], h * HEAD_W:(h + 1) * HEAD_W]], axis=0), vnew[i])
               for i, (d, h) in enumerate(heads)]
        for d in range(2):
            idx = range(d * N_HEADS, (d + 1) * N_HEADS)
            of_s[d, rows[d], :] = jnp.concatenate([both[i][C:] + upd[i][:C] for i in idx], axis=-1)
            st_s[d] = jnp.concatenate(
                [sts[i] * alast[d][:, (i - d * N_HEADS) * HEAD_W:(i - d * N_HEADS + 1) * HEAD_W] + upd[i][C:]
                 for i in idx], axis=-1)
        return carry

    lax.fori_loop(0, n, chunk, 0)
    o_ref[...] = _head_norm_gate(of_s[0] + of_s[1], bd, gn_ref[...], u_ref[:, 3 * GROUP_W:4 * GROUP_W])
    for d in range(2):
        for h in range(N_HEADS):
            sfin_ref[d, h] = st_s[d][:, h * HEAD_W:(h + 1) * HEAD_W]


def _gdn_call(u_gd, cw, alog, dtb, gn, consts, s0, l, T, nb, row0):
    tril, masks, expand, bd = consts
    tb = row0 // T
    has_s0 = s0 is not None
    in_specs = [
        pl.BlockSpec((T, GD_PAD), lambda b: (tb + b, 0)),
        pl.BlockSpec((3, 3 * GROUP_W), lambda b: (0, 0)),
        pl.BlockSpec((1, LANE), lambda b: (0, 0)),
        pl.BlockSpec((1, LANE), lambda b: (0, 0)),
        pl.BlockSpec((2, LANE, 2 * GROUP_W), lambda b: (0, 0, 0)),
        pl.BlockSpec((2, CHUNK, CHUNK), lambda b: (0, 0, 0)),
        pl.BlockSpec((2, 2, CHUNK, CHUNK), lambda b: (0, 0, 0, 0)),
        pl.BlockSpec((GROUP_W, GROUP_W), lambda b: (0, 0)),
        pl.BlockSpec((1, GROUP_W), lambda b: (0, 0)),
    ]
    args = [u_gd, cw, alog, dtb, expand, tril, masks, bd, gn]
    if has_s0:
        in_specs.append(pl.BlockSpec((None, None, 2, N_HEADS, HEAD_W, HEAD_W), lambda b: (b, l, 0, 0, 0, 0)))
        args.append(s0)
    seq = pltpu.VMEM((2, T, GROUP_W), F32)
    return pl.pallas_call(
        functools.partial(_gdn_kernel, T=T, has_s0=has_s0),
        grid=(nb,),
        in_specs=in_specs,
        out_specs=[
            pl.BlockSpec((T, GROUP_W), lambda b: (b, 0)),
            pl.BlockSpec((None, 2, N_HEADS, HEAD_W, HEAD_W), lambda b: (b, 0, 0, 0, 0)),
        ],
        out_shape=[jax.ShapeDtypeStruct((nb * T, GROUP_W), F32),
                   jax.ShapeDtypeStruct((nb, 2, N_HEADS, HEAD_W, HEAD_W), F32)],
        scratch_shapes=[pltpu.VMEM((T, GROUP_W), F32)] * 3 + [seq, seq,
            pltpu.VMEM((2, T, N_HEADS * LANE), F32), seq, seq, seq,
            pltpu.VMEM((2, T // CHUNK * 8, GROUP_W), F32), seq, pltpu.VMEM((2, HEAD_W, GROUP_W), F32)],
        compiler_params=_cparams(("parallel",)),
        name="gdn",
    )(*args)


def _hyfilt_kernel(z_ref, win_ref, fh_ref, fl_ref, w1_ref, b1_ref, fr_ref, w2_ref, b2_ref, w3_ref, o_ref, *, T):
    fr = fr_ref[...]
    h = jnp.sin(fr * (_dot3(z_ref[...], w1_ref[...]) + b1_ref[...]))
    h = jnp.sin(fr * (_dot3(h, w2_ref[...]) + b2_ref[...]))
    h = _dot3(h, w3_ref[...])
    win = win_ref[...]
    hf = h[:, 0:GROUP_W] * win
    hb = h[:, GROUP_W:2 * GROUP_W] * win
    row = lax.broadcasted_iota(jnp.int32, hb.shape, 0)
    hb = jnp.where(row == 0, 0.0, hb)
    sh, sl = _split2(jnp.concatenate([hf + hb, hf - hb], axis=-1))
    fh = fh_ref[...]
    spec = (jnp.dot(fh, sh, preferred_element_type=F32) + jnp.dot(fh, sl, preferred_element_type=F32)
            + jnp.dot(fl_ref[...], sh, preferred_element_type=F32))
    o_ref[0:T, :] = spec[0:T, 0:GROUP_W]
    o_ref[T:2 * T, :] = spec[T:2 * T, GROUP_W:2 * GROUP_W]


def _hyfilt_call(T, zp, win, fh, fl, w1p, b1, freq, w2, b2, w3):
    c2 = lambda l: (0, 0)
    return pl.pallas_call(
        functools.partial(_hyfilt_kernel, T=T),
        grid=(DEPTH,),
        in_specs=[
            pl.BlockSpec((T, LANE), c2),
            pl.BlockSpec((T, GROUP_W), c2),
            pl.BlockSpec((2 * T, T), c2),
            pl.BlockSpec((2 * T, T), c2),
            pl.BlockSpec((None, LANE, HY_FH), lambda l: (l, 0, 0)),
            pl.BlockSpec((None, 1, HY_FH), lambda l: (l, 0, 0)),
            pl.BlockSpec((None, 1, HY_FH), lambda l: (l, 0, 0)),
            pl.BlockSpec((None, HY_FH, HY_FH), lambda l: (l, 0, 0)),
            pl.BlockSpec((None, 1, HY_FH), lambda l: (l, 0, 0)),
            pl.BlockSpec((None, HY_FH, 2 * GROUP_W), lambda l: (l, 0, 0)),
        ],
        out_specs=pl.BlockSpec((None, 2 * T, GROUP_W), lambda l: (l, 0, 0)),
        out_shape=jax.ShapeDtypeStruct((DEPTH, 2 * T, GROUP_W), F32),
        compiler_params=_cparams(("parallel",)),
        name="hyfilt",
    )(zp, win, fh, fl, w1p, b1, freq, w2, b2, w3)


def _hyena_kernel(u_ref, cw_ref, cb_ref, spec_ref, skip_ref, fh_ref, fl_ref, ih_ref, il_ref, o_ref, *, T):
    uc = _conv3(u_ref[...], cw_ref, T) + cb_ref[...]
    x0 = uc[:, 0:GROUP_W]
    z = uc[:, GROUP_W:2 * GROUP_W] * uc[:, 2 * GROUP_W:3 * GROUP_W]
    zb = z.astype(BF16)
    zs = (jnp.dot(fh_ref[...], zb, preferred_element_type=F32)
          + jnp.dot(fl_ref[...], zb, preferred_element_type=F32))
    ar, ai = zs[0:T], zs[T:2 * T]
    br, bi = spec_ref[0:T, :], spec_ref[T:2 * T, :]
    pb = jnp.concatenate([ar * br - ai * bi, ar * bi + ai * br], axis=0).astype(BF16)
    y = (jnp.dot(ih_ref[...], pb, preferred_element_type=F32)
         + jnp.dot(il_ref[...], pb, preferred_element_type=F32))
    o_ref[...] = x0 * (y + z * skip_ref[...])


def _hyena_call(u_hy, cw, cb, spec, skip, dft, l, T, nb, row0):
    fh, fl, ih, il = dft
    tb = row0 // T
    c2 = lambda b: (0, 0)
    return pl.pallas_call(
        functools.partial(_hyena_kernel, T=T),
        grid=(nb,),
        in_specs=[
            pl.BlockSpec((T, HY_COLS), lambda b: (tb + b, 0)),
            pl.BlockSpec((3, HY_COLS), c2),
            pl.BlockSpec((1, HY_COLS), c2),
            pl.BlockSpec((None, 2 * T, GROUP_W), lambda b: (l, 0, 0)),
            pl.BlockSpec((1, GROUP_W), c2),
            pl.BlockSpec((2 * T, T), c2),
            pl.BlockSpec((2 * T, T), c2),
            pl.BlockSpec((T, 2 * T), c2),
            pl.BlockSpec((T, 2 * T), c2),
        ],
        out_specs=pl.BlockSpec((T, GROUP_W), lambda b: (b, 0)),
        out_shape=jax.ShapeDtypeStruct((nb * T, GROUP_W), F32),
        compiler_params=_cparams(("parallel",)),
        name="hyena",
    )(u_hy, cw, cb, spec, skip, fh, fl, ih, il)


def _rope(x, cosf, sinf):
    lane = lax.broadcasted_iota(jnp.int32, x.shape, 1)
    half = MLA_ROPE // 2
    partner = jnp.where(lane < MLA_NOPE + half, pltpu.roll(x, LANE - half, 1), pltpu.roll(x, half, 1))
    return x * cosf + partner * sinf


def _qk_norm(x, g):
    ms = jnp.sum(x * x, axis=-1, keepdims=True) * (1.0 / MLA_QK)
    return x * lax.rsqrt(ms + RMS_EPS) * g


def _mla_kernel(*refs, T, ctx):
    if ctx:
        (u_ref, qn_ref, wq_ref, kvn_ref, wkv_ref, qkn_ref, cos_ref, sin_ref, cckv_ref, ckr_ref, o_ref) = refs
    else:
        (u_ref, qn_ref, wq_ref, kvn_ref, wkv_ref, qkn_ref, o_ref, ckv_ref, kr_ref) = refs
    u = u_ref[...]
    cq = _rms(u[:, 0:MLA_Q_LORA], qn_ref[...])
    ckv = _rms(u[:, MLA_Q_LORA:MLA_Q_LORA + MLA_KV_LORA], kvn_ref[...])
    kr = u[:, MLA_Q_LORA + MLA_KV_LORA:MLA_Q_LORA + MLA_KV_LORA + MLA_ROPE]
    if not ctx:
        ckv_ref[...] = ckv
        kr_ref[...] = kr
    q_all = _bdot(cq, wq_ref[...])
    kv = _bdot(ckv, wkv_ref[...])
    gq = qkn_ref[0:1, :]
    gk = qkn_ref[1:2, :]
    if ctx:
        kvc = _bdot(cckv_ref[...], wkv_ref[...])
        krc = ckr_ref[...]
        cosf, sinf = cos_ref[...], sin_ref[...]
    scale = MLA_QK ** -0.5
    outs = []
    for h in range(N_HEADS):
        qh = _qk_norm(q_all[:, h * LANE:(h + 1) * LANE], gq)
        zpad = jnp.zeros((T, LANE - MLA_QK), F32)
        kh = _qk_norm(jnp.concatenate([kv[:, h * HEAD_W:(h + 1) * HEAD_W], kr, zpad], axis=-1), gk)
        vh = kv[:, GROUP_W + h * HEAD_W:GROUP_W + (h + 1) * HEAD_W]
        if ctx:
            qh = _rope(qh, cosf, sinf)
            kh = _rope(kh, cosf, sinf)
            s_len = krc.shape[0]
            zc = jnp.zeros((s_len, LANE - MLA_QK), F32)
            kc = _qk_norm(jnp.concatenate([kvc[:, h * HEAD_W:(h + 1) * HEAD_W], krc, zc], axis=-1), gk)
            kh = jnp.concatenate([kh, kc], axis=0)
            vh = jnp.concatenate([vh, kvc[:, GROUP_W + h * HEAD_W:GROUP_W + (h + 1) * HEAD_W]], axis=0)
        khb = kh.astype(BF16)
        vhb = vh.astype(BF16)
        blocks = []
        for qb in range(T // ATT_QBLOCK):
            s = _bdot_nt(qh[qb * ATT_QBLOCK:(qb + 1) * ATT_QBLOCK], khb) * scale
            e = jnp.exp(s - jnp.max(s, axis=-1, keepdims=True))
            blocks.append(_bdot(e, vhb) / jnp.sum(e, axis=-1, keepdims=True))
        outs.append(blocks[0] if len(blocks) == 1 else jnp.concatenate(blocks, axis=0))
    o_ref[...] = jnp.concatenate(outs, axis=-1)


def _mla_call(u_mla, qn, wq, kvn, wkv, qkn, rope, cache, l, T, nb, row0):
    tb = row0 // T
    ctx = cache is not None
    c2 = lambda b: (0, 0)
    in_specs = [
        pl.BlockSpec((T, MLA_PAD), lambda b: (tb + b, 0)),
        pl.BlockSpec((1, MLA_Q_LORA), c2),
        pl.BlockSpec((MLA_Q_LORA, N_HEADS * LANE), c2),
        pl.BlockSpec((1, MLA_KV_LORA), c2),
        pl.BlockSpec((MLA_KV_LORA, 2 * GROUP_W), c2),
        pl.BlockSpec((2, LANE), c2),
    ]
    args = [u_mla, qn, wq, kvn, wkv, qkn]
    out_specs = [pl.BlockSpec((T, GROUP_W), lambda b: (b, 0))]
    out_shape = [jax.ShapeDtypeStruct((nb * T, GROUP_W), F32)]
    if ctx:
        in_specs += [
            pl.BlockSpec((T, LANE), c2),
            pl.BlockSpec((T, LANE), c2),
            pl.BlockSpec((None, None, PAST_LEN, MLA_KV_LORA), lambda b: (b, l, 0, 0)),
            pl.BlockSpec((None, None, PAST_LEN, MLA_ROPE), lambda b: (b, l, 0, 0)),
        ]
        args += [rope[0], rope[1], cache[0], cache[1]]
    else:
        out_specs += [pl.BlockSpec((T, MLA_KV_LORA), lambda b: (b, 0)),
                      pl.BlockSpec((T, MLA_ROPE), lambda b: (b, 0))]
        out_shape += [jax.ShapeDtypeStruct((nb * T, MLA_KV_LORA), F32),
                      jax.ShapeDtypeStruct((nb * T, MLA_ROPE), F32)]
    return pl.pallas_call(
        functools.partial(_mla_kernel, T=T, ctx=ctx),
        grid=(nb,),
        in_specs=in_specs,
        out_specs=out_specs,
        out_shape=out_shape,
        compiler_params=_cparams(("parallel",)),
        name="mla",
    )(*args)


def _pad_cols(w, width):
    return jnp.pad(w, [(0, 0)] * (w.ndim - 1) + [(0, width - w.shape[-1])])


def _prep_w_in(w_in):
    o1 = HG_COLS
    o2 = o1 + HY_COLS
    o3 = o2 + MLA_COLS
    return (w_in[..., :o1].astype(BF16), w_in[..., o1:o2].astype(BF16),
            _pad_cols(w_in[..., o2:o3], MLA_PAD).astype(BF16), _pad_cols(w_in[..., o3:], GD_PAD).astype(BF16))


def _prep_wq(w_q_up):
    w = w_q_up.reshape(DEPTH, MLA_Q_LORA, N_HEADS, MLA_QK)
    return _pad_cols(w, LANE).reshape(DEPTH, MLA_Q_LORA, N_HEADS * LANE).astype(BF16)


def _prep_wkv(w_kv_up):
    w = w_kv_up.reshape(DEPTH, MLA_KV_LORA, N_HEADS, 2, HEAD_W)
    return w.transpose(0, 1, 3, 2, 4).reshape(DEPTH, MLA_KV_LORA, 2 * GROUP_W).astype(BF16)


def _lower_bounds(hgrn_lb):
    lb = jnp.cumsum(jax.nn.softmax(hgrn_lb.astype(F32), axis=0), axis=0)
    return lb - lb[0]


def kernel(x_prompt, x_sample, cache_mla_ckv, cache_mla_krope, state_hgrn, state_gdn, c, c_ctx, w_ada, b_ada, norm_ffn, w_ffn_gu, w_ffn_down, norm_mix, w_in, w_out, hgrn_lb, hgrn_norm, hy_conv_w, hy_conv_b, hy_w1, hy_b1, hy_freq, hy_w2, hy_b2, hy_w3, hy_skip, mla_q_norm_a, mla_w_q_up, mla_kv_norm_a, mla_w_kv_up, mla_qk_norm, gdn_conv_w, gdn_a_log, gdn_dt_bias, gdn_norm):
    x = jnp.concatenate([x_prompt.reshape(N_PROMPT, D_MODEL), x_sample.reshape(N_SAMPLE, D_MODEL)], axis=0)

    cond8 = jnp.zeros((8, D_MODEL), F32).at[0].set(c_ctx).at[1:1 + DEC_BATCH].set(c)
    ada = _ada_call(cond8, w_ada, b_ada)

    w_in_parts = _prep_w_in(w_in)
    w_out_bf = w_out.astype(BF16)
    wq = _prep_wq(mla_w_q_up)
    wkv = _prep_wkv(mla_w_kv_up)
    qkn = _pad_cols(mla_qk_norm, LANE)
    lb_all = _lower_bounds(hgrn_lb)
    alog = _pad_cols(gdn_a_log.reshape(DEPTH, 1, 8), LANE)
    dtb = _pad_cols(gdn_dt_bias.reshape(DEPTH, 1, 8), LANE)
    gdn_gn = jnp.tile(gdn_norm, (1, N_HEADS)).reshape(DEPTH, 1, GROUP_W)
    w1p = jnp.pad(hy_w1, ((0, 0), (0, LANE - HY_EMB), (0, 0)))

    bd = jnp.asarray(_block_diag_ones(), BF16)
    hg_tril, hg_m = _hgrn_consts()
    hg_consts = (jnp.asarray(hg_tril, BF16), jnp.asarray(hg_m, F32), bd)
    gd_tril, gd_masks, gd_expand = _gdn_consts()
    gd_consts = (jnp.asarray(gd_tril, BF16), jnp.asarray(gd_masks, F32), jnp.asarray(gd_expand, BF16), bd)
    rope = tuple(jnp.asarray(a) for a in _rope_consts(DEC_SEQ))
    groups = ((SEQ, BATCH, 0), (DEC_SEQ, DEC_BATCH, N_PROMPT))
    dft = {}
    spec = {}
    for T, _, _ in groups:
        fwd, inv = _dft_consts(T)
        fh, fl = _np_split2(fwd)
        ih, il = _np_split2(inv)
        dft[T] = (fh, fl, ih, il)
        zp, win = _hyena_pos_consts(T)
        spec[T] = _hyfilt_call(T, jnp.asarray(zp), jnp.asarray(win), fh, fl, w1p,
                               hy_b1.reshape(DEPTH, 1, HY_FH), hy_freq.reshape(DEPTH, 1, HY_FH), hy_w2,
                               hy_b2.reshape(DEPTH, 1, HY_FH), hy_w3)

    new_ckv, new_kr, new_hg, new_gd = [], [], [], []
    for l in range(DEPTH):
        x = _ffn_call(x, ada, norm_ffn, w_ffn_gu, w_ffn_down, l, 0)
        u_hg, u_hy, u_mla, u_gd = _inproj_call(x, ada, norm_mix, w_in_parts, l)
        outs = []
        for gi, (T, nb, row0) in enumerate(groups):
            latent = gi == 1
            o_hg, s_hg = _hgrn_call(u_hg, lb_all[l], hgrn_norm[l], hg_consts,
                                    state_hgrn if latent else None, l, T, nb, row0)
            o_hy = _hyena_call(u_hy, hy_conv_w[l], hy_conv_b[l].reshape(1, HY_COLS), spec[T],
                               hy_skip[l].reshape(1, GROUP_W), dft[T], l, T, nb, row0)
            mla = _mla_call(u_mla, mla_q_norm_a[l].reshape(1, MLA_Q_LORA), wq[l],
                            mla_kv_norm_a[l].reshape(1, MLA_KV_LORA), wkv[l], qkn[l],
                            rope if latent else None,
                            (cache_mla_ckv, cache_mla_krope) if latent else None, l, T, nb, row0)
            o_gd, s_gd = _gdn_call(u_gd, gdn_conv_w[l], alog[l], dtb[l], gdn_gn[l], gd_consts,
                                   state_gdn if latent else None, l, T, nb, row0)
            outs.append((o_hg, o_hy, mla[0], o_gd))
            if not latent:
                new_ckv.append(mla[1].reshape(BATCH, SEQ, MLA_KV_LORA))
                new_kr.append(mla[2].reshape(BATCH, SEQ, MLA_ROPE))
                new_hg.append(s_hg)
                new_gd.append(s_gd)
        x = _outproj_call(x, ada, w_out_bf, outs[0], outs[1], l)
        x = _ffn_call(x, ada, norm_ffn, w_ffn_gu, w_ffn_down, l, 1)

    y_prompt = x[:N_PROMPT].reshape(BATCH, SEQ, D_MODEL)
    y_sample = x[N_PROMPT:].reshape(DEC_BATCH, DEC_SEQ, D_MODEL)
    return (y_prompt, y_sample, jnp.stack(new_ckv, axis=1), jnp.stack(new_kr, axis=1),
            jnp.stack(new_hg, axis=1), jnp.stack(new_gd, axis=1))
```

```python
import functools
import math

import numpy as np
import jax
import jax.numpy as jnp
from jax import lax
from jax.experimental import pallas as pl
from jax.experimental.pallas import tpu as pltpu

F32 = jnp.float32
BF16 = jnp.bfloat16

D_MODEL = 1024
BATCH = 16
SEQ = 256
DEPTH = 4
DEC_BATCH = 2
DEC_SEQ = 1024
PAST_LEN = 256
GRID_W = 64
N_ADA = 9
D_FF = 2816
GROUP_W = 256
CHUNK = 64
RMS_EPS = 1e-6
N_HEADS = 4
HEAD_W = 64
HY_EMB = 33
HY_FH = 64
HY_TARGET = 1e-2
HY_FAST = 0.3
HY_SLOW = 1.5
MLA_NOPE = 64
MLA_ROPE = 32
MLA_QK = MLA_NOPE + MLA_ROPE
MLA_Q_LORA = 256
MLA_KV_LORA = 128
ROPE_BASE = 10000.0

HG_COLS = 5 * GROUP_W
HY_COLS = 3 * GROUP_W
MLA_COLS = MLA_Q_LORA + MLA_KV_LORA + MLA_ROPE
GD_COLS = 4 * GROUP_W + 16
MLA_PAD = 512
GD_PAD = 1152

N_PROMPT = BATCH * SEQ
N_SAMPLE = DEC_BATCH * DEC_SEQ
N_TOK = N_PROMPT + N_SAMPLE
LANE = 128
VMEM_LIMIT = 56 * 1024 * 1024
ROW_TILE = 1024
FF_TILE = 256
ADA_TILE = 1536
ATT_QBLOCK = 256


def _bdot(a, b):
    return jnp.dot(a.astype(BF16), b.astype(BF16), preferred_element_type=F32)


def _bdot_nt(a, b):
    return lax.dot_general(a.astype(BF16), b.astype(BF16), (((1,), (1,)), ((), ())),
                           preferred_element_type=F32)


def _bdot_tn(a, b):
    return lax.dot_general(a.astype(BF16), b.astype(BF16), (((0,), (0,)), ((), ())),
                           preferred_element_type=F32)


def _split2(x):
    hi = x.astype(BF16)
    lo = (x - hi.astype(F32)).astype(BF16)
    return hi, lo


def _split3(x):
    hi = x.astype(BF16)
    r = x - hi.astype(F32)
    mid = r.astype(BF16)
    lo = (r - mid.astype(F32)).astype(BF16)
    return hi, mid, lo


def _dot3(a, b):
    ah, al = _split2(a)
    bh, bl = _split2(b)
    return (jnp.dot(ah, bh, preferred_element_type=F32) + jnp.dot(ah, bl, preferred_element_type=F32)
            + jnp.dot(al, bh, preferred_element_type=F32))


def _sel_dot(c, x):
    h, m, l = _split3(x)
    return (jnp.dot(c, h, preferred_element_type=F32) + jnp.dot(c, m, preferred_element_type=F32)
            + jnp.dot(c, l, preferred_element_type=F32))


def _dot_sel(x, c):
    h, m, l = _split3(x)
    return (jnp.dot(h, c, preferred_element_type=F32) + jnp.dot(m, c, preferred_element_type=F32)
            + jnp.dot(l, c, preferred_element_type=F32))


def _sigmoid(x):
    return 1.0 / (1.0 + jnp.exp(-x))


def _silu(x):
    return x * _sigmoid(x)


def _rms(x, g):
    return x * lax.rsqrt(jnp.mean(x * x, axis=-1, keepdims=True) + RMS_EPS) * g


def _cparams(sem):
    return pltpu.CompilerParams(dimension_semantics=sem, vmem_limit_bytes=VMEM_LIMIT)


def _cond_of_tile(i):
    return jnp.maximum(i - (N_PROMPT // ROW_TILE - 1), 0)


def _ada_kernel(c_ref, w_ref, b_ref, o_ref):
    o_ref[...] = _dot3(_silu(c_ref[...]), w_ref[...]) + b_ref[...]


def _ada_call(cond8, w_ada, b_ada):
    n = N_ADA * D_MODEL
    out = pl.pallas_call(
        _ada_kernel,
        grid=(DEPTH, n // ADA_TILE),
        in_specs=[
            pl.BlockSpec((8, D_MODEL), lambda l, j: (0, 0)),
            pl.BlockSpec((None, D_MODEL, ADA_TILE), lambda l, j: (l, 0, j)),
            pl.BlockSpec((None, 1, ADA_TILE), lambda l, j: (l, 0, j)),
        ],
        out_specs=pl.BlockSpec((None, 8, ADA_TILE), lambda l, j: (l, 0, j)),
        out_shape=jax.ShapeDtypeStruct((DEPTH, 8, n), F32),
        compiler_params=_cparams(("parallel", "parallel")),
        name="ada",
    )(cond8, w_ada, b_ada.reshape(DEPTH, 1, n))
    return out.reshape(DEPTH, 8, N_ADA, D_MODEL)


FFN_SUBTILES = 2


def _ffn_kernel(*refs, sub):
    x_ref = refs[0]
    ada_refs = refs[1:1 + FFN_SUBTILES]
    g_ref, wg_ref, wu_ref, wd_ref, o_ref, h_scr = refs[1 + FFN_SUBTILES:]
    f = pl.program_id(1)

    @pl.when(f == 0)
    def _():
        for r, ada_ref in enumerate(ada_refs):
            rows = slice(r * ROW_TILE, (r + 1) * ROW_TILE)
            y = _rms(x_ref[rows, :], g_ref[...])
            h = y * (1.0 + ada_ref[3 * sub + 1:3 * sub + 2, :]) + ada_ref[3 * sub:3 * sub + 1, :]
            h_scr[rows, :] = h.astype(BF16)
        o_ref[...] = jnp.zeros_like(o_ref)

    wg = wg_ref[...].astype(BF16)
    wu = wu_ref[...].astype(BF16)
    wd = wd_ref[...].astype(BF16)
    for r in range(FFN_SUBTILES):
        rows = slice(r * ROW_TILE, (r + 1) * ROW_TILE)
        h = h_scr[rows, :]
        gate = jnp.dot(h, wg, preferred_element_type=F32)
        up = jnp.dot(h, wu, preferred_element_type=F32)
        a = (_silu(gate) * up).astype(BF16)
        o_ref[rows, :] += jnp.dot(a, wd, preferred_element_type=F32)

    @pl.when(f == pl.num_programs(1) - 1)
    def _():
        for r, ada_ref in enumerate(ada_refs):
            rows = slice(r * ROW_TILE, (r + 1) * ROW_TILE)
            o_ref[rows, :] = x_ref[rows, :] + 0.5 * ada_ref[3 * sub + 2:3 * sub + 3, :] * o_ref[rows, :]


def _ffn_call(x, ada, norm_ffn, w_gu, w_down, l, j):
    sub = 2 * j
    nf = D_FF // FF_TILE
    rows = FFN_SUBTILES * ROW_TILE

    def ada_spec(r):
        return pl.BlockSpec((None, None, N_ADA, D_MODEL),
                            lambda i, f: (l, _cond_of_tile(i * FFN_SUBTILES + r), 0, 0))

    return pl.pallas_call(
        functools.partial(_ffn_kernel, sub=sub),
        grid=(N_TOK // rows, nf),
        in_specs=[pl.BlockSpec((rows, D_MODEL), lambda i, f: (i, 0), pipeline_mode=pl.Buffered(1))]
        + [ada_spec(r) for r in range(FFN_SUBTILES)] + [
            pl.BlockSpec((None, None, 1, D_MODEL), lambda i, f: (l, j, 0, 0)),
            pl.BlockSpec((None, None, D_MODEL, FF_TILE), lambda i, f: (l, j, 0, f)),
            pl.BlockSpec((None, None, D_MODEL, FF_TILE), lambda i, f: (l, j, 0, nf + f)),
            pl.BlockSpec((None, None, FF_TILE, D_MODEL), lambda i, f: (l, j, f, 0)),
        ],
        out_specs=pl.BlockSpec((rows, D_MODEL), lambda i, f: (i, 0)),
        out_shape=jax.ShapeDtypeStruct((N_TOK, D_MODEL), F32),
        scratch_shapes=[pltpu.VMEM((rows, D_MODEL), BF16)],
        compiler_params=_cparams(("parallel", "arbitrary")),
        name="ffn",
    )(x, *([ada] * FFN_SUBTILES), norm_ffn.reshape(DEPTH, 2, 1, D_MODEL), w_gu, w_gu, w_down)


IN_TILE = 512


def _inproj_kernel(x_ref, ada_ref, g_ref, w1, w2, w3, w4, o1, o2, o3, o4):
    y = _rms(x_ref[...], g_ref[...])
    h = (y * (1.0 + ada_ref[4:5, :]) + ada_ref[3:4, :]).astype(BF16)
    for w, o in ((w1, o1), (w2, o2), (w3, o3), (w4, o4)):
        o[...] = jnp.dot(h, w[...], preferred_element_type=F32)


def _inproj_call(x, ada, norm_mix, ws, l):
    widths = (HG_COLS, HY_COLS, MLA_PAD, GD_PAD)
    per = ROW_TILE // IN_TILE
    return pl.pallas_call(
        _inproj_kernel,
        grid=(N_TOK // IN_TILE,),
        in_specs=[
            pl.BlockSpec((IN_TILE, D_MODEL), lambda i: (i, 0)),
            pl.BlockSpec((None, None, N_ADA, D_MODEL), lambda i: (l, _cond_of_tile(i // per), 0, 0)),
            pl.BlockSpec((None, 1, D_MODEL), lambda i: (l, 0, 0)),
        ] + [pl.BlockSpec((None, D_MODEL, w), lambda i: (l, 0, 0)) for w in widths],
        out_specs=[pl.BlockSpec((IN_TILE, w), lambda i: (i, 0)) for w in widths],
        out_shape=[jax.ShapeDtypeStruct((N_TOK, w), F32) for w in widths],
        compiler_params=_cparams(("parallel",)),
        name="inproj",
    )(x, ada, norm_mix.reshape(DEPTH, 1, D_MODEL), *ws)


OUT_TILE = 512


def _outproj_kernel(x_ref, ada_ref, w_ref, *refs):
    o_ref = refs[-1]
    i = pl.program_id(0)
    n_p = N_PROMPT // OUT_TILE

    def run(srcs):
        acc = jnp.zeros((OUT_TILE, D_MODEL), F32)
        for g, s in enumerate(srcs):
            acc += jnp.dot(s[...].astype(BF16), w_ref[g * GROUP_W:(g + 1) * GROUP_W, :],
                           preferred_element_type=F32)
        o_ref[...] = x_ref[...] + ada_ref[5:6, :] * acc

    @pl.when(i < n_p)
    def _():
        run(refs[0:4])

    @pl.when(i >= n_p)
    def _():
        run(refs[4:8])


def _outproj_call(x, ada, w_out_bf, o_p, o_s, l):
    per = ROW_TILE // OUT_TILE
    n_p = N_PROMPT // OUT_TILE
    n_s = N_SAMPLE // OUT_TILE
    return pl.pallas_call(
        _outproj_kernel,
        grid=(N_TOK // OUT_TILE,),
        in_specs=[
            pl.BlockSpec((OUT_TILE, D_MODEL), lambda i: (i, 0)),
            pl.BlockSpec((None, None, N_ADA, D_MODEL), lambda i: (l, _cond_of_tile(i // per), 0, 0)),
            pl.BlockSpec((None, D_MODEL, D_MODEL), lambda i: (l, 0, 0)),
        ] + [pl.BlockSpec((OUT_TILE, GROUP_W), lambda i: (jnp.minimum(i, n_p - 1), 0))] * 4
          + [pl.BlockSpec((OUT_TILE, GROUP_W), lambda i: (jnp.clip(i - n_p, 0, n_s - 1), 0))] * 4,
        out_specs=pl.BlockSpec((OUT_TILE, D_MODEL), lambda i: (i, 0)),
        out_shape=jax.ShapeDtypeStruct((N_TOK, D_MODEL), F32),
        compiler_params=_cparams(("parallel",)),
        name="outproj",
    )(x, ada, w_out_bf, *o_p, *o_s)


def _block_diag_ones():
    idx = np.arange(GROUP_W) // HEAD_W
    return (idx[:, None] == idx[None, :]).astype(np.float32)


def _hgrn_consts():
    C = CHUNK
    i = np.arange(C)[:, None]
    j = np.arange(C)[None, :]
    masks = []
    s = C // 2
    while s >= 1:
        up_i = (i // s) % 2 == 1
        up_j = (j // s) % 2 == 1
        masks.append(up_i & (~up_j) & (i // (2 * s) == j // (2 * s)))
        s //= 2
    masks.append(j <= i)
    fwd_m = np.stack([m.astype(np.float32) for m in masks])
    bwd_m = np.stack([m.astype(np.float32)[::-1, ::-1] for m in masks])
    tril = np.stack([(j <= i), (j >= i)]).astype(np.float32)
    return tril, np.tile(np.stack([fwd_m, bwd_m]), (1, 1, 1, 2))


def _gdn_consts():
    C = CHUNK
    i = np.arange(C)[:, None]
    t = np.arange(C)[None, :]
    tril = np.stack([(t <= i), (t >= i)]).astype(np.float32)
    masks = np.stack([np.stack([(t <= i), (t < i)]), np.stack([(t >= i), (t > i)])]).astype(np.float32)
    expand = np.zeros((2, LANE, 2 * GROUP_W), np.float32)
    for d in range(2):
        for h in range(N_HEADS):
            expand[d, d * N_HEADS + h, h * HEAD_W:(h + 1) * HEAD_W] = 1.0
            expand[d, 8 + d * N_HEADS + h, GROUP_W + h * HEAD_W:GROUP_W + (h + 1) * HEAD_W] = 1.0
    return tril, masks, expand


def _dft_consts(T):
    n2 = 4 * T
    k = np.arange(T, dtype=np.int64)[:, None]
    s = np.arange(T, dtype=np.int64)[None, :]
    ang = np.pi * (((2 * k + 1) * s) % n2).astype(np.float64) / (2 * T)
    fwd = np.concatenate([np.cos(ang), -np.sin(ang)], axis=0)
    inv = fwd.T / T
    return fwd.astype(np.float32), inv.astype(np.float32)


def _np_split2(x):
    hi = jnp.asarray(x, F32).astype(BF16)
    lo = (jnp.asarray(x, F32) - hi.astype(F32)).astype(BF16)
    return hi, lo


def _hyena_pos_consts(T):
    pos = np.arange(T, dtype=np.float32)
    t = pos / np.float32(T - 1)
    bands = np.linspace(1e-4, (HY_EMB - 1) // 2 - 1, (HY_EMB - 1) // 2, dtype=np.float32)
    ang = (np.float32(2.0 * math.pi / T) * pos[:, None]) * bands[None, :]
    z = np.concatenate([t[:, None], np.cos(ang), -np.sin(ang)], axis=-1).astype(np.float32)
    zp = np.zeros((T, LANE), np.float32)
    zp[:, :HY_EMB] = z
    max_decay = math.log(HY_TARGET) / HY_FAST
    min_decay = math.log(HY_TARGET) / HY_SLOW
    deltas = np.linspace(min_decay, max_decay, GROUP_W, dtype=np.float32)
    window = np.exp(-t[:, None] * np.abs(deltas)[None, :]).astype(np.float32)
    return zp, window


def _rope_consts(T):
    rows = T // GRID_W
    row = np.repeat(np.arange(rows, dtype=np.float32), GRID_W)
    col = (np.arange(T) % GRID_W).astype(np.float32)
    pairs = MLA_ROPE // 4
    inv = (np.float32(ROPE_BASE) ** (-np.arange(pairs, dtype=np.float32) / np.float32(pairs))).astype(np.float32)
    ang = np.concatenate([row[:, None] * inv, col[:, None] * inv], axis=-1).astype(np.float32)
    cos, sin = np.cos(ang), np.sin(ang)
    cosf = np.ones((T, LANE), np.float32)
    sinf = np.zeros((T, LANE), np.float32)
    half = MLA_ROPE // 2
    cosf[:, MLA_NOPE:MLA_NOPE + half] = cos
    cosf[:, MLA_NOPE + half:MLA_QK] = cos
    sinf[:, MLA_NOPE:MLA_NOPE + half] = -sin
    sinf[:, MLA_NOPE + half:MLA_QK] = sin
    return cosf, sinf


def _head_norm_gate(tot, bd, gn, gate):
    ms = _sel_dot_right(tot * tot, bd) * (1.0 / HEAD_W)
    return tot * lax.rsqrt(ms + RMS_EPS) * gn * _silu(gate)


def _sel_dot_right(x, c):
    h, l = _split2(x)
    return jnp.dot(h, c, preferred_element_type=F32) + jnp.dot(l, c, preferred_element_type=F32)


def _block_ref(b, two_s, r):
    C, W = b.shape
    if two_s % 8 == 0:
        b3 = b.reshape(C // two_s, two_s, W)
        return jnp.broadcast_to(b3[:, r:r + 1, :], b3.shape).reshape(C, W)
    pos = lax.broadcasted_iota(jnp.int32, b.shape, 0) % two_s
    out = b
    for p in range(two_s):
        if p != r:
            out = jnp.where(pos == p, pltpu.roll(b, (p - r) % C, 0), out)
    return out


N_PAIRS = N_HEADS // 2
HG_GROUP = 2
HG_DIRECT_MAX = 80.0


def _pair_blockdiag(x):
    lane = lax.broadcasted_iota(jnp.int32, x.shape, 1)
    zero = jnp.zeros_like(x)
    return jnp.concatenate([jnp.where(lane < HEAD_W, x, zero), jnp.where(lane >= HEAD_W, x, zero)], axis=0)


def _hgrn_kernel(*refs, T, has_s0):
    if has_s0:
        (u_ref, lb_ref, gn_ref, tril_ref, lmask_ref, bd_ref, s0_ref,
         o_ref, sfin_ref, oi_s, qin_s, up_s, dc_s, st_s) = refs
    else:
        (u_ref, lb_ref, gn_ref, tril_ref, lmask_ref, bd_ref,
         o_ref, sfin_ref, oi_s, qin_s, up_s, dc_s, st_s) = refs
    n = T // CHUNK
    C = CHUNK
    bd = bd_ref[...]
    n_lv = int(math.log2(C))

    log_lb = [jnp.log(lb_ref[d]) for d in range(2)]
    log_1mlb = [jnp.log(1.0 - lb_ref[d]) for d in range(2)]

    def gates(rows, d):
        z = u_ref[rows, (3 + d) * GROUP_W:(4 + d) * GROUP_W]
        t = jnp.exp(-jnp.abs(z))
        log_sig = jnp.minimum(z, 0.0) - jnp.log(1.0 + t)
        c = log_1mlb[d] + log_sig
        m = jnp.maximum(log_lb[d], c)
        lf = m + jnp.log(1.0 + jnp.exp(jnp.minimum(log_lb[d], c) - m))
        sig_neg = jnp.where(z > 0.0, t, 1.0) / (1.0 + t)
        return lf, (1.0 - lb_ref[d]) * sig_neg

    for d in range(2):
        if has_s0:
            st_s[d] = jnp.concatenate([s0_ref[d, h].T for h in range(N_HEADS)], axis=-1)
        else:
            st_s[d] = jnp.zeros((HEAD_W, GROUP_W), F32)

    def prepare(it, carry):
        units = [(c, d) for c in range(HG_GROUP) for d in range(2)]
        rows = [pl.ds(pl.multiple_of((it * HG_GROUP + c) * C, C), C) for c in range(HG_GROUP)]
        arow = [pl.ds(pl.multiple_of((it * HG_GROUP + c) * 8, 8), 8) for c in range(HG_GROUP)]
        q = [u_ref[rows[c], 0:GROUP_W] * (HEAD_W ** -0.5) for c in range(HG_GROUP)]
        v = [u_ref[rows[c], GROUP_W:2 * GROUP_W] for c in range(HG_GROUP)]
        vt = [[jnp.concatenate([v[c][:, h * HEAD_W:(h + 1) * HEAD_W].T for h in (2 * p, 2 * p + 1)],
                               axis=-1).astype(BF16) for p in range(N_PAIRS)]
              for c in range(HG_GROUP)]
        v_bd = [[_pair_blockdiag(v[c][:, p * LANE:(p + 1) * LANE].astype(BF16)) for p in range(N_PAIRS)]
                for c in range(HG_GROUP)]
        lf, ks = zip(*[gates(rows[c], d) for c, d in units])
        bs = []
        for i, (c, d) in enumerate(units):
            hi, lo = _split2(lf[i])
            tril = tril_ref[d]
            bs.append(jnp.dot(tril, hi, preferred_element_type=F32) + jnp.dot(tril, lo, preferred_element_type=F32))
        tot = [jnp.sum(x, axis=0, keepdims=True) for x in lf]
        lowest = functools.reduce(jnp.minimum, tot)
        ko = [(ks[i] * jnp.exp(tot[i] - bs[i])).astype(BF16) for i in range(len(units))]
        up = [[jnp.dot(vt[c][p], _pair_blockdiag(ko[i][:, p * LANE:(p + 1) * LANE]), preferred_element_type=F32)
               for p in range(N_PAIRS)] for i, (c, d) in enumerate(units)]
        for i, (c, d) in enumerate(units):
            qin_s[d, rows[c], :] = q[c] * jnp.exp(bs[i])
            up_s[d, rows[c], :] = jnp.concatenate(up[i], axis=-1)
            dc_s[d, arow[c], :] = jnp.broadcast_to(jnp.exp(tot[i]), (8, GROUP_W))

        def masked_scores(qe, ke, lv):
            out = []
            for i, (c, d) in enumerate(units):
                per_pair = []
                for p in range(N_PAIRS):
                    sl = slice(p * LANE, (p + 1) * LANE)
                    prod = lax.dot_general(qe[i][:, sl], _pair_blockdiag(ke[i][:, sl]), (((1,), (1,)), ((), ())),
                                           preferred_element_type=F32)
                    per_pair.append(lmask_ref[d, lv] * prod)
                out.append(per_pair)
            return out

        def finish(sc):
            return [jnp.concatenate([jnp.dot(sc[i][p].astype(BF16), v_bd[c][p], preferred_element_type=F32)
                                     for p in range(N_PAIRS)], axis=-1) for i, (c, d) in enumerate(units)]

        def intra_direct():
            mid = [_block_ref(bs[i], C, C // 2 - 1 if d == 0 else C // 2) for i, (c, d) in enumerate(units)]
            qe = [(q[c] * jnp.exp(bs[i] - mid[i])).astype(BF16) for i, (c, d) in enumerate(units)]
            ke = [(ks[i] * jnp.exp(mid[i] - bs[i])).astype(BF16) for i in range(len(units))]
            return tuple(finish(masked_scores(qe, ke, n_lv)))

        def intra_split():
            sc = [[jnp.zeros((C, LANE), F32) for _ in range(N_PAIRS)] for _ in units]
            s = C // 2
            lv = 0
            while s >= 1:
                e = [jnp.exp(-jnp.abs(bs[i] - _block_ref(bs[i], 2 * s, s - 1 if d == 0 else s)))
                     for i, (c, d) in enumerate(units)]
                part = masked_scores([(q[c] * e[i]).astype(BF16) for i, (c, d) in enumerate(units)],
                                     [(ks[i] * e[i]).astype(BF16) for i in range(len(units))], lv)
                sc = [[sc[i][p] + part[i][p] for p in range(N_PAIRS)] for i in range(len(units))]
                s //= 2
                lv += 1
            fin = finish(sc)
            return tuple(fin[i] + _bdot(q[c] * ks[i], bd) * v[c] for i, (c, d) in enumerate(units))

        oi = lax.cond(jnp.min(lowest) > -HG_DIRECT_MAX, intra_direct, intra_split)
        for i, (c, d) in enumerate(units):
            oi_s[d, rows[c], :] = oi[i]
        return carry

    lax.fori_loop(0, n // HG_GROUP, prepare, 0)

    def chunk(ci, carry):
        rows = [pl.ds(pl.multiple_of(cidx * C, C), C) for cidx in (ci, n - 1 - ci)]
        decay = [dc_s[d, pl.ds(pl.multiple_of(cidx * 8, 8), 1), :] for d, cidx in ((0, ci), (1, n - 1 - ci))]
        st = [st_s[d] for d in range(2)]
        o_inter = [[lax.dot_general(qin_s[d, rows[d], p * LANE:(p + 1) * LANE].astype(BF16),
                                    _pair_blockdiag(st[d][:, p * LANE:(p + 1) * LANE].astype(BF16)),
                                    (((1,), (1,)), ((), ())), preferred_element_type=F32)
                    for p in range(N_PAIRS)] for d in range(2)]
        for d in range(2):
            st_s[d] = st[d] * decay[d] + up_s[d, rows[d], :]
            oi_s[d, rows[d], :] = oi_s[d, rows[d], :] + jnp.concatenate(o_inter[d], axis=-1)
        return carry

    lax.fori_loop(0, n, chunk, 0, unroll=2)
    o_ref[...] = _head_norm_gate(oi_s[0] + oi_s[1], bd, gn_ref[...], u_ref[:, 2 * GROUP_W:3 * GROUP_W])
    for d in range(2):
        for h in range(N_HEADS):
            sfin_ref[d, h] = st_s[d][:, h * HEAD_W:(h + 1) * HEAD_W].T


def _hgrn_call(u_hg, lb_l, gn, consts, s0, l, T, nb, row0):
    tril, lmask, bd = consts
    tb = row0 // T
    has_s0 = s0 is not None
    in_specs = [
        pl.BlockSpec((T, HG_COLS), lambda b: (tb + b, 0)),
        pl.BlockSpec((2, 1, GROUP_W), lambda b: (0, 0, 0)),
        pl.BlockSpec((1, GROUP_W), lambda b: (0, 0)),
        pl.BlockSpec((2, CHUNK, CHUNK), lambda b: (0, 0, 0)),
        pl.BlockSpec((2, 7, CHUNK, LANE), lambda b: (0, 0, 0, 0)),
        pl.BlockSpec((GROUP_W, GROUP_W), lambda b: (0, 0)),
    ]
    args = [u_hg, lb_l.reshape(2, 1, GROUP_W), gn.reshape(1, GROUP_W), tril, lmask, bd]
    if has_s0:
        in_specs.append(pl.BlockSpec((None, None, 2, N_HEADS, HEAD_W, HEAD_W), lambda b: (b, l, 0, 0, 0, 0)))
        args.append(s0)
    seq = pltpu.VMEM((2, T, GROUP_W), F32)
    return pl.pallas_call(
        functools.partial(_hgrn_kernel, T=T, has_s0=has_s0),
        grid=(nb,),
        in_specs=in_specs,
        out_specs=[
            pl.BlockSpec((T, GROUP_W), lambda b: (b, 0)),
            pl.BlockSpec((None, 2, N_HEADS, HEAD_W, HEAD_W), lambda b: (b, 0, 0, 0, 0)),
        ],
        out_shape=[jax.ShapeDtypeStruct((nb * T, GROUP_W), F32),
                   jax.ShapeDtypeStruct((nb, 2, N_HEADS, HEAD_W, HEAD_W), F32)],
        scratch_shapes=[seq, seq, seq,
                        pltpu.VMEM((2, T // CHUNK * 8, GROUP_W), F32), pltpu.VMEM((2, HEAD_W, GROUP_W), F32)],
        compiler_params=_cparams(("parallel",)),
        name="hgrn",
    )(*args)


def _shift_rows(x, T):
    row = lax.broadcasted_iota(jnp.int32, x.shape, 0)
    prev = jnp.where(row == 0, 0.0, pltpu.roll(x, 1, 0))
    nxt = jnp.where(row == T - 1, 0.0, pltpu.roll(x, T - 1, 0))
    return prev, nxt


def _conv3(x, w_ref, T):
    prev, nxt = _shift_rows(x, T)
    return prev * w_ref[0:1, :] + x * w_ref[1:2, :] + nxt * w_ref[2:3, :]


GDN_UNROLL = 2


def _solve_unit_lower(systems):
    c2 = 2 * CHUNK
    slabs = [jnp.concatenate([nmat, nmat, rhs], axis=-1) for rhs, nmat in systems]
    steps = int(math.log2(CHUNK))
    for step in range(steps):
        last = step == steps - 1
        nxt = []
        for slab in slabs:
            hi = slab.astype(BF16)
            lo = (slab - hi.astype(F32)).astype(BF16)
            lhs = jnp.concatenate([hi[:, :c2], lo[:, :CHUNK]], axis=-1)
            first = c2 if last else 0
            rhs3 = jnp.concatenate([hi[:, first:], lo[:, first:], hi[:, first:]], axis=0)
            prod = jnp.dot(lhs, rhs3, preferred_element_type=F32)
            if last:
                nxt.append(slab[:, c2:] + prod)
            else:
                nxt.append(jnp.concatenate([prod[:, :c2], slab[:, c2:] + prod[:, c2:]], axis=-1))
        slabs = nxt
    return slabs


def _gdn_kernel(*refs, T, has_s0):
    if has_s0:
        (u_ref, cw_ref, alog_ref, dtb_ref, exp_ref, tril_ref, mask_ref, bd_ref, gn_ref, s0_ref,
         o_ref, sfin_ref, q_s, k_s, v_s, la_s, be_s, uw_s, ww_s, at_s, qin_s, kt_s, al_s, of_s, st_s) = refs
    else:
        (u_ref, cw_ref, alog_ref, dtb_ref, exp_ref, tril_ref, mask_ref, bd_ref, gn_ref,
         o_ref, sfin_ref, q_s, k_s, v_s, la_s, be_s, uw_s, ww_s, at_s, qin_s, kt_s, al_s, of_s, st_s) = refs
    n = T // CHUNK
    C = CHUNK
    bd = bd_ref[...]

    qkv = _silu(_conv3(u_ref[:, 0:3 * GROUP_W], cw_ref, T))
    q = qkv[:, 0:GROUP_W]
    k = qkv[:, GROUP_W:2 * GROUP_W]
    q_s[...] = q * lax.rsqrt(_sel_dot_right(q * q, bd) + 1e-6) * (HEAD_W ** -0.5)
    k_s[...] = k * lax.rsqrt(_sel_dot_right(k * k, bd) + 1e-6)
    v_s[...] = qkv[:, 2 * GROUP_W:3 * GROUP_W]

    ab = u_ref[:, 4 * GROUP_W:4 * GROUP_W + LANE]
    xa = ab + dtb_ref[...]
    softplus = jnp.maximum(xa, 0.0) + jnp.log(1.0 + jnp.exp(-jnp.abs(xa)))
    log_a = -jnp.exp(alog_ref[...]) * softplus
    lane = lax.broadcasted_iota(jnp.int32, ab.shape, 1)
    narrow = jnp.where(lane < 8, log_a, _sigmoid(ab))
    for d in range(2):
        wide = _dot_sel(narrow, exp_ref[d])
        la_s[d] = wide[:, 0:GROUP_W]
        be_s[d] = wide[:, GROUP_W:2 * GROUP_W]
        if has_s0:
            st_s[d] = jnp.concatenate([s0_ref[d, h] for h in range(N_HEADS)], axis=-1)
        else:
            st_s[d] = jnp.zeros((HEAD_W, GROUP_W), F32)

    def prepare(cidx):
        r0 = pl.multiple_of(cidx * C, C)
        rows = pl.ds(r0, C)
        arow = pl.ds(pl.multiple_of(cidx * 8, 8), 8)
        q = q_s[rows, :]
        k = k_s[rows, :]
        v = v_s[rows, :]
        systems = []
        attns = []
        kts = []
        for d in range(2):
            incl = mask_ref[d, 0] > 0.5
            strict = mask_ref[d, 1]
            la = la_s[d, rows, :]
            be = be_s[d, rows, :]
            gx = _sel_dot(tril_ref[d], la)
            gtot = jnp.sum(la, axis=0, keepdims=True)
            eg = jnp.exp(gx)
            kout = k * jnp.exp(gtot - gx)
            qin_s[d, rows, :] = q * eg
            al_s[d, arow, :] = jnp.broadcast_to(jnp.exp(gtot), (8, GROUP_W))
            kb = k * be
            vb = v * be
            kbg = kb * eg
            for h in range(N_HEADS):
                sl = slice(h * HEAD_W, (h + 1) * HEAD_W)
                gh = gx[:, sl]
                dmat = gh - gh.T
                dec = jnp.where(incl, jnp.exp(jnp.where(incl, dmat, 0.0)), 0.0)
                qk = _bdot_nt(jnp.concatenate([kb[:, sl], q[:, sl]], axis=0), k[:, sl])
                nmat = -(qk[:C] * dec * strict)
                systems.append((jnp.concatenate([vb[:, sl], kbg[:, sl]], axis=-1), nmat))
                attns.append(qk[C:] * dec)
                kts.append(kout[:, sl].T)
        sols = _solve_unit_lower(systems)
        for d in range(2):
            mine = sols[d * N_HEADS:(d + 1) * N_HEADS]
            uw_s[d, rows, :] = jnp.concatenate([x[:, :HEAD_W] for x in mine], axis=-1)
            ww_s[d, rows, :] = jnp.concatenate([x[:, HEAD_W:] for x in mine], axis=-1)
            at_s[d, rows, :] = jnp.concatenate(attns[d * N_HEADS:(d + 1) * N_HEADS], axis=-1)
            kt_s[d, rows, :] = jnp.concatenate(kts[d * N_HEADS:(d + 1) * N_HEADS], axis=-1)

    def prep_body(i, carry):
        for j in range(GDN_UNROLL):
            prepare(i * GDN_UNROLL + j)
        return carry

    lax.fori_loop(0, n // GDN_UNROLL, prep_body, 0)

    def chunk(ci, carry):
        units = [(d, p) for d in range(2) for p in range(N_PAIRS)]
        rows = []
        alast = []
        for d, cidx in ((0, ci), (1, n - 1 - ci)):
            rows.append(pl.ds(pl.multiple_of(cidx * C, C), C))
            alast.append(al_s[d, pl.ds(pl.multiple_of(cidx * 8, 8), 1), :])
        st = [st_s[d] for d in range(2)]
        lanes = [slice(p * LANE, (p + 1) * LANE) for p in range(N_PAIRS)]
        both = [jnp.dot(jnp.concatenate([ww_s[d, rows[d], lanes[p]], qin_s[d, rows[d], lanes[p]]], axis=0).astype(BF16),
                        _pair_blockdiag(st[d][:, lanes[p]].astype(BF16)), preferred_element_type=F32)
                for d, p in units]
        vnew = [uw_s[d, rows[d], lanes[p]] - both[i][:C] for i, (d, p) in enumerate(units)]
        upd = [jnp.dot(jnp.concatenate([at_s[d, rows[d], lanes[p]], kt_s[d, rows[d], lanes[p]]], axis=0).astype(BF16),
                       _pair_blockdiag(vnew[i].astype(BF16)), preferred_element_type=F32)
               for i, (d, p) in enumerate(units)]
        for d in range(2):
            idx = range(d * N_PAIRS, (d + 1) * N_PAIRS)
            of_s[d, rows[d], :] = jnp.concatenate([both[i][C:] + upd[i][:C] for i in idx], axis=-1)
            st_s[d] = st[d] * alast[d] + jnp.concatenate([upd[i][C:] for i in idx], axis=-1)
        return carry

    lax.fori_loop(0, n, chunk, 0)
    o_ref[...] = _head_norm_gate(of_s[0] + of_s[1], bd, gn_ref[...], u_ref[:, 3 * GROUP_W:4 * GROUP_W])
    for d in range(2):
        for h in range(N_HEADS):
            sfin_ref[d, h] = st_s[d][:, h * HEAD_W:(h + 1) * HEAD_W]


def _gdn_call(u_gd, cw, alog, dtb, gn, consts, s0, l, T, nb, row0):
    tril, masks, expand, bd = consts
    tb = row0 // T
    has_s0 = s0 is not None
    in_specs = [
        pl.BlockSpec((T, GD_PAD), lambda b: (tb + b, 0)),
        pl.BlockSpec((3, 3 * GROUP_W), lambda b: (0, 0)),
        pl.BlockSpec((1, LANE), lambda b: (0, 0)),
        pl.BlockSpec((1, LANE), lambda b: (0, 0)),
        pl.BlockSpec((2, LANE, 2 * GROUP_W), lambda b: (0, 0, 0)),
        pl.BlockSpec((2, CHUNK, CHUNK), lambda b: (0, 0, 0)),
        pl.BlockSpec((2, 2, CHUNK, CHUNK), lambda b: (0, 0, 0, 0)),
        pl.BlockSpec((GROUP_W, GROUP_W), lambda b: (0, 0)),
        pl.BlockSpec((1, GROUP_W), lambda b: (0, 0)),
    ]
    args = [u_gd, cw, alog, dtb, expand, tril, masks, bd, gn]
    if has_s0:
        in_specs.append(pl.BlockSpec((None, None, 2, N_HEADS, HEAD_W, HEAD_W), lambda b: (b, l, 0, 0, 0, 0)))
        args.append(s0)
    seq = pltpu.VMEM((2, T, GROUP_W), F32)
    return pl.pallas_call(
        functools.partial(_gdn_kernel, T=T, has_s0=has_s0),
        grid=(nb,),
        in_specs=in_specs,
        out_specs=[
            pl.BlockSpec((T, GROUP_W), lambda b: (b, 0)),
            pl.BlockSpec((None, 2, N_HEADS, HEAD_W, HEAD_W), lambda b: (b, 0, 0, 0, 0)),
        ],
        out_shape=[jax.ShapeDtypeStruct((nb * T, GROUP_W), F32),
                   jax.ShapeDtypeStruct((nb, 2, N_HEADS, HEAD_W, HEAD_W), F32)],
        scratch_shapes=[pltpu.VMEM((T, GROUP_W), F32)] * 3 + [seq, seq, seq, seq, seq, seq, seq,
            pltpu.VMEM((2, T // CHUNK * 8, GROUP_W), F32), seq, pltpu.VMEM((2, HEAD_W, GROUP_W), F32)],
        compiler_params=_cparams(("parallel",)),
        name="gdn",
    )(*args)


def _hyfilt_kernel(z_ref, win_ref, fh_ref, fl_ref, w1_ref, b1_ref, fr_ref, w2_ref, b2_ref, w3_ref, o_ref, *, T):
    fr = fr_ref[...]
    h = jnp.sin(fr * (_dot3(z_ref[...], w1_ref[...]) + b1_ref[...]))
    h = jnp.sin(fr * (_dot3(h, w2_ref[...]) + b2_ref[...]))
    h = _dot3(h, w3_ref[...])
    win = win_ref[...]
    hf = h[:, 0:GROUP_W] * win
    hb = h[:, GROUP_W:2 * GROUP_W] * win
    row = lax.broadcasted_iota(jnp.int32, hb.shape, 0)
    hb = jnp.where(row == 0, 0.0, hb)
    sh, sl = _split2(jnp.concatenate([hf + hb, hf - hb], axis=-1))
    fh = fh_ref[...]
    spec = (jnp.dot(fh, sh, preferred_element_type=F32) + jnp.dot(fh, sl, preferred_element_type=F32)
            + jnp.dot(fl_ref[...], sh, preferred_element_type=F32))
    o_ref[0:T, :] = spec[0:T, 0:GROUP_W]
    o_ref[T:2 * T, :] = spec[T:2 * T, GROUP_W:2 * GROUP_W]


def _hyfilt_call(T, zp, win, fh, fl, w1p, b1, freq, w2, b2, w3):
    c2 = lambda l: (0, 0)
    return pl.pallas_call(
        functools.partial(_hyfilt_kernel, T=T),
        grid=(DEPTH,),
        in_specs=[
            pl.BlockSpec((T, LANE), c2),
            pl.BlockSpec((T, GROUP_W), c2),
            pl.BlockSpec((2 * T, T), c2),
            pl.BlockSpec((2 * T, T), c2),
            pl.BlockSpec((None, LANE, HY_FH), lambda l: (l, 0, 0)),
            pl.BlockSpec((None, 1, HY_FH), lambda l: (l, 0, 0)),
            pl.BlockSpec((None, 1, HY_FH), lambda l: (l, 0, 0)),
            pl.BlockSpec((None, HY_FH, HY_FH), lambda l: (l, 0, 0)),
            pl.BlockSpec((None, 1, HY_FH), lambda l: (l, 0, 0)),
            pl.BlockSpec((None, HY_FH, 2 * GROUP_W), lambda l: (l, 0, 0)),
        ],
        out_specs=pl.BlockSpec((None, 2 * T, GROUP_W), lambda l: (l, 0, 0)),
        out_shape=jax.ShapeDtypeStruct((DEPTH, 2 * T, GROUP_W), F32),
        compiler_params=_cparams(("parallel",)),
        name="hyfilt",
    )(zp, win, fh, fl, w1p, b1, freq, w2, b2, w3)


def _hyena_kernel(u_ref, cw_ref, cb_ref, spec_ref, skip_ref, fh_ref, fl_ref, ih_ref, il_ref, o_ref, *, T):
    uc = _conv3(u_ref[...], cw_ref, T) + cb_ref[...]
    x0 = uc[:, 0:GROUP_W]
    z = uc[:, GROUP_W:2 * GROUP_W] * uc[:, 2 * GROUP_W:3 * GROUP_W]
    zb = z.astype(BF16)
    zs = (jnp.dot(fh_ref[...], zb, preferred_element_type=F32)
          + jnp.dot(fl_ref[...], zb, preferred_element_type=F32))
    ar, ai = zs[0:T], zs[T:2 * T]
    br, bi = spec_ref[0:T, :], spec_ref[T:2 * T, :]
    pb = jnp.concatenate([ar * br - ai * bi, ar * bi + ai * br], axis=0).astype(BF16)
    y = (jnp.dot(ih_ref[...], pb, preferred_element_type=F32)
         + jnp.dot(il_ref[...], pb, preferred_element_type=F32))
    o_ref[...] = x0 * (y + z * skip_ref[...])


def _hyena_call(u_hy, cw, cb, spec, skip, dft, l, T, nb, row0):
    fh, fl, ih, il = dft
    tb = row0 // T
    c2 = lambda b: (0, 0)
    return pl.pallas_call(
        functools.partial(_hyena_kernel, T=T),
        grid=(nb,),
        in_specs=[
            pl.BlockSpec((T, HY_COLS), lambda b: (tb + b, 0)),
            pl.BlockSpec((3, HY_COLS), c2),
            pl.BlockSpec((1, HY_COLS), c2),
            pl.BlockSpec((None, 2 * T, GROUP_W), lambda b: (l, 0, 0)),
            pl.BlockSpec((1, GROUP_W), c2),
            pl.BlockSpec((2 * T, T), c2),
            pl.BlockSpec((2 * T, T), c2),
            pl.BlockSpec((T, 2 * T), c2),
            pl.BlockSpec((T, 2 * T), c2),
        ],
        out_specs=pl.BlockSpec((T, GROUP_W), lambda b: (b, 0)),
        out_shape=jax.ShapeDtypeStruct((nb * T, GROUP_W), F32),
        compiler_params=_cparams(("parallel",)),
        name="hyena",
    )(u_hy, cw, cb, spec, skip, fh, fl, ih, il)


def _rope(x, cosf, sinf):
    lane = lax.broadcasted_iota(jnp.int32, x.shape, 1)
    half = MLA_ROPE // 2
    partner = jnp.where(lane < MLA_NOPE + half, pltpu.roll(x, LANE - half, 1), pltpu.roll(x, half, 1))
    return x * cosf + partner * sinf


def _qk_norm(x, g):
    ms = jnp.sum(x * x, axis=-1, keepdims=True) * (1.0 / MLA_QK)
    return x * lax.rsqrt(ms + RMS_EPS) * g


def _mla_kernel(*refs, T, ctx):
    if ctx:
        (u_ref, qn_ref, wq_ref, kvn_ref, wkv_ref, qkn_ref, cos_ref, sin_ref, cckv_ref, ckr_ref, o_ref) = refs
    else:
        (u_ref, qn_ref, wq_ref, kvn_ref, wkv_ref, qkn_ref, o_ref, ckv_ref, kr_ref) = refs
    u = u_ref[...]
    cq = _rms(u[:, 0:MLA_Q_LORA], qn_ref[...])
    ckv = _rms(u[:, MLA_Q_LORA:MLA_Q_LORA + MLA_KV_LORA], kvn_ref[...])
    kr = u[:, MLA_Q_LORA + MLA_KV_LORA:MLA_Q_LORA + MLA_KV_LORA + MLA_ROPE]
    if not ctx:
        ckv_ref[...] = ckv
        kr_ref[...] = kr
    q_all = _bdot(cq, wq_ref[...])
    kv = _bdot(ckv, wkv_ref[...])
    gq = qkn_ref[0:1, :]
    gk = qkn_ref[1:2, :]
    if ctx:
        kvc = _bdot(cckv_ref[...], wkv_ref[...])
        krc = ckr_ref[...]
        cosf, sinf = cos_ref[...], sin_ref[...]
    scale = MLA_QK ** -0.5
    outs = []
    for h in range(N_HEADS):
        qh = _qk_norm(q_all[:, h * LANE:(h + 1) * LANE], gq)
        zpad = jnp.zeros((T, LANE - MLA_QK), F32)
        kh = _qk_norm(jnp.concatenate([kv[:, h * HEAD_W:(h + 1) * HEAD_W], kr, zpad], axis=-1), gk)
        vh = kv[:, GROUP_W + h * HEAD_W:GROUP_W + (h + 1) * HEAD_W]
        if ctx:
            qh = _rope(qh, cosf, sinf)
            kh = _rope(kh, cosf, sinf)
            s_len = krc.shape[0]
            zc = jnp.zeros((s_len, LANE - MLA_QK), F32)
            kc = _qk_norm(jnp.concatenate([kvc[:, h * HEAD_W:(h + 1) * HEAD_W], krc, zc], axis=-1), gk)
            kh = jnp.concatenate([kh, kc], axis=0)
            vh = jnp.concatenate([vh, kvc[:, GROUP_W + h * HEAD_W:GROUP_W + (h + 1) * HEAD_W]], axis=0)
        khb = kh.astype(BF16)
        vhb = vh.astype(BF16)
        blocks = []
        for qb in range(T // ATT_QBLOCK):
            s = _bdot_nt(qh[qb * ATT_QBLOCK:(qb + 1) * ATT_QBLOCK], khb) * scale
            e = jnp.exp(s - jnp.max(s, axis=-1, keepdims=True))
            blocks.append(_bdot(e, vhb) / jnp.sum(e, axis=-1, keepdims=True))
        outs.append(blocks[0] if len(blocks) == 1 else jnp.concatenate(blocks, axis=0))
    o_ref[...] = jnp.concatenate(outs, axis=-1)


def _mla_call(u_mla, qn, wq, kvn, wkv, qkn, rope, cache, l, T, nb, row0):
    tb = row0 // T
    ctx = cache is not None
    c2 = lambda b: (0, 0)
    in_specs = [
        pl.BlockSpec((T, MLA_PAD), lambda b: (tb + b, 0)),
        pl.BlockSpec((1, MLA_Q_LORA), c2),
        pl.BlockSpec((MLA_Q_LORA, N_HEADS * LANE), c2),
        pl.BlockSpec((1, MLA_KV_LORA), c2),
        pl.BlockSpec((MLA_KV_LORA, 2 * GROUP_W), c2),
        pl.BlockSpec((2, LANE), c2),
    ]
    args = [u_mla, qn, wq, kvn, wkv, qkn]
    out_specs = [pl.BlockSpec((T, GROUP_W), lambda b: (b, 0))]
    out_shape = [jax.ShapeDtypeStruct((nb * T, GROUP_W), F32)]
    if ctx:
        in_specs += [
            pl.BlockSpec((T, LANE), c2),
            pl.BlockSpec((T, LANE), c2),
            pl.BlockSpec((None, None, PAST_LEN, MLA_KV_LORA), lambda b: (b, l, 0, 0)),
            pl.BlockSpec((None, None, PAST_LEN, MLA_ROPE), lambda b: (b, l, 0, 0)),
        ]
        args += [rope[0], rope[1], cache[0], cache[1]]
    else:
        out_specs += [pl.BlockSpec((T, MLA_KV_LORA), lambda b: (b, 0)),
                      pl.BlockSpec((T, MLA_ROPE), lambda b: (b, 0))]
        out_shape += [jax.ShapeDtypeStruct((nb * T, MLA_KV_LORA), F32),
                      jax.ShapeDtypeStruct((nb * T, MLA_ROPE), F32)]
    return pl.pallas_call(
        functools.partial(_mla_kernel, T=T, ctx=ctx),
        grid=(nb,),
        in_specs=in_specs,
        out_specs=out_specs,
        out_shape=out_shape,
        compiler_params=_cparams(("parallel",)),
        name="mla",
    )(*args)


def _pad_cols(w, width):
    return jnp.pad(w, [(0, 0)] * (w.ndim - 1) + [(0, width - w.shape[-1])])


def _prep_w_in(w_in):
    o1 = HG_COLS
    o2 = o1 + HY_COLS
    o3 = o2 + MLA_COLS
    return (w_in[..., :o1].astype(BF16), w_in[..., o1:o2].astype(BF16),
            _pad_cols(w_in[..., o2:o3], MLA_PAD).astype(BF16), _pad_cols(w_in[..., o3:], GD_PAD).astype(BF16))


def _prep_wq(w_q_up):
    w = w_q_up.reshape(DEPTH, MLA_Q_LORA, N_HEADS, MLA_QK)
    return _pad_cols(w, LANE).reshape(DEPTH, MLA_Q_LORA, N_HEADS * LANE).astype(BF16)


def _prep_wkv(w_kv_up):
    w = w_kv_up.reshape(DEPTH, MLA_KV_LORA, N_HEADS, 2, HEAD_W)
    return w.transpose(0, 1, 3, 2, 4).reshape(DEPTH, MLA_KV_LORA, 2 * GROUP_W).astype(BF16)


def _lower_bounds(hgrn_lb):
    lb = jnp.cumsum(jax.nn.softmax(hgrn_lb.astype(F32), axis=0), axis=0)
    return lb - lb[0]


def kernel(x_prompt, x_sample, cache_mla_ckv, cache_mla_krope, state_hgrn, state_gdn, c, c_ctx, w_ada, b_ada, norm_ffn, w_ffn_gu, w_ffn_down, norm_mix, w_in, w_out, hgrn_lb, hgrn_norm, hy_conv_w, hy_conv_b, hy_w1, hy_b1, hy_freq, hy_w2, hy_b2, hy_w3, hy_skip, mla_q_norm_a, mla_w_q_up, mla_kv_norm_a, mla_w_kv_up, mla_qk_norm, gdn_conv_w, gdn_a_log, gdn_dt_bias, gdn_norm):
    x = jnp.concatenate([x_prompt.reshape(N_PROMPT, D_MODEL), x_sample.reshape(N_SAMPLE, D_MODEL)], axis=0)

    cond8 = jnp.zeros((8, D_MODEL), F32).at[0].set(c_ctx).at[1:1 + DEC_BATCH].set(c)
    ada = _ada_call(cond8, w_ada, b_ada)

    w_in_parts = _prep_w_in(w_in)
    w_out_bf = w_out.astype(BF16)
    wq = _prep_wq(mla_w_q_up)
    wkv = _prep_wkv(mla_w_kv_up)
    qkn = _pad_cols(mla_qk_norm, LANE)
    lb_all = _lower_bounds(hgrn_lb)
    alog = _pad_cols(gdn_a_log.reshape(DEPTH, 1, 8), LANE)
    dtb = _pad_cols(gdn_dt_bias.reshape(DEPTH, 1, 8), LANE)
    gdn_gn = jnp.tile(gdn_norm, (1, N_HEADS)).reshape(DEPTH, 1, GROUP_W)
    w1p = jnp.pad(hy_w1, ((0, 0), (0, LANE - HY_EMB), (0, 0)))

    bd = jnp.asarray(_block_diag_ones(), BF16)
    hg_tril, hg_m = _hgrn_consts()
    hg_consts = (jnp.asarray(hg_tril, BF16), jnp.asarray(hg_m, F32), bd)
    gd_tril, gd_masks, gd_expand = _gdn_consts()
    gd_consts = (jnp.asarray(gd_tril, BF16), jnp.asarray(gd_masks, F32), jnp.asarray(gd_expand, BF16), bd)
    rope = tuple(jnp.asarray(a) for a in _rope_consts(DEC_SEQ))
    groups = ((SEQ, BATCH, 0), (DEC_SEQ, DEC_BATCH, N_PROMPT))
    dft = {}
    spec = {}
    for T, _, _ in groups:
        fwd, inv = _dft_consts(T)
        fh, fl = _np_split2(fwd)
        ih, il = _np_split2(inv)
        dft[T] = (fh, fl, ih, il)
        zp, win = _hyena_pos_consts(T)
        spec[T] = _hyfilt_call(T, jnp.asarray(zp), jnp.asarray(win), fh, fl, w1p,
                               hy_b1.reshape(DEPTH, 1, HY_FH), hy_freq.reshape(DEPTH, 1, HY_FH), hy_w2,
                               hy_b2.reshape(DEPTH, 1, HY_FH), hy_w3)

    new_ckv, new_kr, new_hg, new_gd = [], [], [], []
    for l in range(DEPTH):
        x = _ffn_call(x, ada, norm_ffn, w_ffn_gu, w_ffn_down, l, 0)
        u_hg, u_hy, u_mla, u_gd = _inproj_call(x, ada, norm_mix, w_in_parts, l)
        outs = []
        for gi, (T, nb, row0) in enumerate(groups):
            latent = gi == 1
            o_hg, s_hg = _hgrn_call(u_hg, lb_all[l], hgrn_norm[l], hg_consts,
                                    state_hgrn if latent else None, l, T, nb, row0)
            o_hy = _hyena_call(u_hy, hy_conv_w[l], hy_conv_b[l].reshape(1, HY_COLS), spec[T],
                               hy_skip[l].reshape(1, GROUP_W), dft[T], l, T, nb, row0)
            mla = _mla_call(u_mla, mla_q_norm_a[l].reshape(1, MLA_Q_LORA), wq[l],
                            mla_kv_norm_a[l].reshape(1, MLA_KV_LORA), wkv[l], qkn[l],
                            rope if latent else None,
                            (cache_mla_ckv, cache_mla_krope) if latent else None, l, T, nb, row0)
            o_gd, s_gd = _gdn_call(u_gd, gdn_conv_w[l], alog[l], dtb[l], gdn_gn[l], gd_consts,
                                   state_gdn if latent else None, l, T, nb, row0)
            outs.append((o_hg, o_hy, mla[0], o_gd))
            if not latent:
                new_ckv.append(mla[1].reshape(BATCH, SEQ, MLA_KV_LORA))
                new_kr.append(mla[2].reshape(BATCH, SEQ, MLA_ROPE))
                new_hg.append(s_hg)
                new_gd.append(s_gd)
        x = _outproj_call(x, ada, w_out_bf, outs[0], outs[1], l)
        x = _ffn_call(x, ada, norm_ffn, w_ffn_gu, w_ffn_down, l, 1)

    y_prompt = x[:N_PROMPT].reshape(BATCH, SEQ, D_MODEL)
    y_sample = x[N_PROMPT:].reshape(DEC_BATCH, DEC_SEQ, D_MODEL)
    return (y_prompt, y_sample, jnp.stack(new_ckv, axis=1), jnp.stack(new_kr, axis=1),
            jnp.stack(new_hg, axis=1), jnp.stack(new_gd, axis=1))
```

```python
import functools
import math

import numpy as np
import jax
import jax.numpy as jnp
from jax import lax
from jax.experimental import pallas as pl
from jax.experimental.pallas import tpu as pltpu

F32 = jnp.float32
BF16 = jnp.bfloat16

D_MODEL = 1024
BATCH = 16
SEQ = 256
DEPTH = 4
DEC_BATCH = 2
DEC_SEQ = 1024
PAST_LEN = 256
GRID_W = 64
N_ADA = 9
D_FF = 2816
GROUP_W = 256
CHUNK = 64
RMS_EPS = 1e-6
N_HEADS = 4
HEAD_W = 64
HY_EMB = 33
HY_FH = 64
HY_TARGET = 1e-2
HY_FAST = 0.3
HY_SLOW = 1.5
MLA_NOPE = 64
MLA_ROPE = 32
MLA_QK = MLA_NOPE + MLA_ROPE
MLA_Q_LORA = 256
MLA_KV_LORA = 128
ROPE_BASE = 10000.0

HG_COLS = 5 * GROUP_W
HY_COLS = 3 * GROUP_W
MLA_COLS = MLA_Q_LORA + MLA_KV_LORA + MLA_ROPE
GD_COLS = 4 * GROUP_W + 16
MLA_PAD = 512
GD_PAD = 1152

N_PROMPT = BATCH * SEQ
N_SAMPLE = DEC_BATCH * DEC_SEQ
N_TOK = N_PROMPT + N_SAMPLE
LANE = 128
VMEM_LIMIT = 56 * 1024 * 1024
ROW_TILE = 1024
FF_TILE = 256
ADA_TILE = 1536
ATT_QBLOCK = 256


def _bdot(a, b):
    return jnp.dot(a.astype(BF16), b.astype(BF16), preferred_element_type=F32)


def _bdot_nt(a, b):
    return lax.dot_general(a.astype(BF16), b.astype(BF16), (((1,), (1,)), ((), ())),
                           preferred_element_type=F32)


def _bdot_tn(a, b):
    return lax.dot_general(a.astype(BF16), b.astype(BF16), (((0,), (0,)), ((), ())),
                           preferred_element_type=F32)


def _split2(x):
    hi = x.astype(BF16)
    lo = (x - hi.astype(F32)).astype(BF16)
    return hi, lo


def _split3(x):
    hi = x.astype(BF16)
    r = x - hi.astype(F32)
    mid = r.astype(BF16)
    lo = (r - mid.astype(F32)).astype(BF16)
    return hi, mid, lo


def _dot3(a, b):
    ah, al = _split2(a)
    bh, bl = _split2(b)
    return (jnp.dot(ah, bh, preferred_element_type=F32) + jnp.dot(ah, bl, preferred_element_type=F32)
            + jnp.dot(al, bh, preferred_element_type=F32))


def _sel_dot(c, x):
    h, m, l = _split3(x)
    return (jnp.dot(c, h, preferred_element_type=F32) + jnp.dot(c, m, preferred_element_type=F32)
            + jnp.dot(c, l, preferred_element_type=F32))


def _dot_sel(x, c):
    h, m, l = _split3(x)
    return (jnp.dot(h, c, preferred_element_type=F32) + jnp.dot(m, c, preferred_element_type=F32)
            + jnp.dot(l, c, preferred_element_type=F32))


def _sigmoid(x):
    return 1.0 / (1.0 + jnp.exp(-x))


def _silu(x):
    return x * _sigmoid(x)


def _rms(x, g):
    return x * lax.rsqrt(jnp.mean(x * x, axis=-1, keepdims=True) + RMS_EPS) * g


def _cparams(sem):
    return pltpu.CompilerParams(dimension_semantics=sem, vmem_limit_bytes=VMEM_LIMIT)


def _cond_of_tile(i):
    return jnp.maximum(i - (N_PROMPT // ROW_TILE - 1), 0)


def _ada_kernel(c_ref, w_ref, b_ref, o_ref):
    o_ref[...] = _dot3(_silu(c_ref[...]), w_ref[...]) + b_ref[...]


def _ada_call(cond8, w_ada, b_ada):
    n = N_ADA * D_MODEL
    out = pl.pallas_call(
        _ada_kernel,
        grid=(DEPTH, n // ADA_TILE),
        in_specs=[
            pl.BlockSpec((8, D_MODEL), lambda l, j: (0, 0)),
            pl.BlockSpec((None, D_MODEL, ADA_TILE), lambda l, j: (l, 0, j)),
            pl.BlockSpec((None, 1, ADA_TILE), lambda l, j: (l, 0, j)),
        ],
        out_specs=pl.BlockSpec((None, 8, ADA_TILE), lambda l, j: (l, 0, j)),
        out_shape=jax.ShapeDtypeStruct((DEPTH, 8, n), F32),
        compiler_params=_cparams(("parallel", "parallel")),
        name="ada",
    )(cond8, w_ada, b_ada.reshape(DEPTH, 1, n))
    return out.reshape(DEPTH, 8, N_ADA, D_MODEL)


FFN_SUBTILES = 2


def _ffn_kernel(*refs, sub, split):
    nx = 1 if split is None else 2
    x_refs = refs[:nx]
    ada_refs = refs[nx:nx + FFN_SUBTILES]
    g_ref, wg_ref, wu_ref, wd_ref, o_ref, h_scr = refs[nx + FFN_SUBTILES:]
    i = pl.program_id(0)
    f = pl.program_id(1)

    def prologue(x_ref):
        for r, ada_ref in enumerate(ada_refs):
            rows = slice(r * ROW_TILE, (r + 1) * ROW_TILE)
            y = _rms(x_ref[rows, :], g_ref[...])
            h = y * (1.0 + ada_ref[3 * sub + 1:3 * sub + 2, :]) + ada_ref[3 * sub:3 * sub + 1, :]
            h_scr[rows, :] = h.astype(BF16)
        o_ref[...] = jnp.zeros_like(o_ref)

    def epilogue(x_ref):
        for r, ada_ref in enumerate(ada_refs):
            rows = slice(r * ROW_TILE, (r + 1) * ROW_TILE)
            o_ref[rows, :] = x_ref[rows, :] + 0.5 * ada_ref[3 * sub + 2:3 * sub + 3, :] * o_ref[rows, :]

    def on(step, fn):
        if split is None:
            pl.when(f == step)(functools.partial(fn, x_refs[0]))
        else:
            pl.when((f == step) & (i < split))(functools.partial(fn, x_refs[0]))
            pl.when((f == step) & (i >= split))(functools.partial(fn, x_refs[1]))

    on(0, prologue)
    wg = wg_ref[...].astype(BF16)
    wu = wu_ref[...].astype(BF16)
    wd = wd_ref[...].astype(BF16)
    for r in range(FFN_SUBTILES):
        rows = slice(r * ROW_TILE, (r + 1) * ROW_TILE)
        h = h_scr[rows, :]
        gate = jnp.dot(h, wg, preferred_element_type=F32)
        up = jnp.dot(h, wu, preferred_element_type=F32)
        a = (_silu(gate) * up).astype(BF16)
        o_ref[rows, :] += jnp.dot(a, wd, preferred_element_type=F32)
    on(pl.num_programs(1) - 1, epilogue)


def _ffn_call(xs, ada, norm_ffn, w_gu, w_down, l, j):
    sub = 2 * j
    nf = D_FF // FF_TILE
    rows = FFN_SUBTILES * ROW_TILE

    def ada_spec(r):
        return pl.BlockSpec((None, None, N_ADA, D_MODEL),
                            lambda i, f: (l, _cond_of_tile(i * FFN_SUBTILES + r), 0, 0))

    if isinstance(xs, tuple):
        split = xs[0].shape[0] // rows
        n_tail = xs[1].shape[0] // rows
        x_specs = [pl.BlockSpec((rows, D_MODEL), lambda i, f: (jnp.minimum(i, split - 1), 0),
                                pipeline_mode=pl.Buffered(1)),
                   pl.BlockSpec((rows, D_MODEL), lambda i, f: (jnp.clip(i - split, 0, n_tail - 1), 0),
                                pipeline_mode=pl.Buffered(1))]
    else:
        split = None
        xs = (xs,)
        x_specs = [pl.BlockSpec((rows, D_MODEL), lambda i, f: (i, 0), pipeline_mode=pl.Buffered(1))]

    return pl.pallas_call(
        functools.partial(_ffn_kernel, sub=sub, split=split),
        grid=(N_TOK // rows, nf),
        in_specs=x_specs
        + [ada_spec(r) for r in range(FFN_SUBTILES)] + [
            pl.BlockSpec((None, None, 1, D_MODEL), lambda i, f: (l, j, 0, 0)),
            pl.BlockSpec((None, None, D_MODEL, FF_TILE), lambda i, f: (l, j, 0, f)),
            pl.BlockSpec((None, None, D_MODEL, FF_TILE), lambda i, f: (l, j, 0, nf + f)),
            pl.BlockSpec((None, None, FF_TILE, D_MODEL), lambda i, f: (l, j, f, 0)),
        ],
        out_specs=pl.BlockSpec((rows, D_MODEL), lambda i, f: (i, 0)),
        out_shape=jax.ShapeDtypeStruct((N_TOK, D_MODEL), F32),
        scratch_shapes=[pltpu.VMEM((rows, D_MODEL), BF16)],
        compiler_params=_cparams(("parallel", "arbitrary")),
        name="ffn",
    )(*xs, *([ada] * FFN_SUBTILES), norm_ffn.reshape(DEPTH, 2, 1, D_MODEL), w_gu, w_gu, w_down)


IN_TILE = 512


def _inproj_kernel(x_ref, ada_ref, g_ref, w1, w2, w3, w4, o1, o2, o3, o4):
    y = _rms(x_ref[...], g_ref[...])
    h = (y * (1.0 + ada_ref[4:5, :]) + ada_ref[3:4, :]).astype(BF16)
    for w, o in ((w1, o1), (w2, o2), (w3, o3), (w4, o4)):
        o[...] = jnp.dot(h, w[...], preferred_element_type=F32)


def _inproj_call(x, ada, norm_mix, ws, l):
    widths = (HG_COLS, HY_COLS, MLA_PAD, GD_PAD)
    per = ROW_TILE // IN_TILE
    return pl.pallas_call(
        _inproj_kernel,
        grid=(N_TOK // IN_TILE,),
        in_specs=[
            pl.BlockSpec((IN_TILE, D_MODEL), lambda i: (i, 0)),
            pl.BlockSpec((None, None, N_ADA, D_MODEL), lambda i: (l, _cond_of_tile(i // per), 0, 0)),
            pl.BlockSpec((None, 1, D_MODEL), lambda i: (l, 0, 0)),
        ] + [pl.BlockSpec((None, D_MODEL, w), lambda i: (l, 0, 0)) for w in widths],
        out_specs=[pl.BlockSpec((IN_TILE, w), lambda i: (i, 0)) for w in widths],
        out_shape=[jax.ShapeDtypeStruct((N_TOK, w), F32) for w in widths],
        compiler_params=_cparams(("parallel",)),
        name="inproj",
    )(x, ada, norm_mix.reshape(DEPTH, 1, D_MODEL), *ws)


OUT_TILE = 512


def _outproj_kernel(x_ref, ada_ref, w_ref, *refs):
    o_ref = refs[-1]
    i = pl.program_id(0)
    n_p = N_PROMPT // OUT_TILE

    def run(srcs):
        acc = jnp.zeros((OUT_TILE, D_MODEL), F32)
        for g, s in enumerate(srcs):
            acc += jnp.dot(s[...].astype(BF16), w_ref[g * GROUP_W:(g + 1) * GROUP_W, :],
                           preferred_element_type=F32)
        o_ref[...] = x_ref[...] + ada_ref[5:6, :] * acc

    @pl.when(i < n_p)
    def _():
        run(refs[0:4])

    @pl.when(i >= n_p)
    def _():
        run(refs[4:8])


def _outproj_call(x, ada, w_out_bf, o_p, o_s, l):
    per = ROW_TILE // OUT_TILE
    n_p = N_PROMPT // OUT_TILE
    n_s = N_SAMPLE // OUT_TILE
    return pl.pallas_call(
        _outproj_kernel,
        grid=(N_TOK // OUT_TILE,),
        in_specs=[
            pl.BlockSpec((OUT_TILE, D_MODEL), lambda i: (i, 0)),
            pl.BlockSpec((None, None, N_ADA, D_MODEL), lambda i: (l, _cond_of_tile(i // per), 0, 0)),
            pl.BlockSpec((None, D_MODEL, D_MODEL), lambda i: (l, 0, 0)),
        ] + [pl.BlockSpec((OUT_TILE, GROUP_W), lambda i: (jnp.minimum(i, n_p - 1), 0))] * 4
          + [pl.BlockSpec((OUT_TILE, GROUP_W), lambda i: (jnp.clip(i - n_p, 0, n_s - 1), 0))] * 4,
        out_specs=pl.BlockSpec((OUT_TILE, D_MODEL), lambda i: (i, 0)),
        out_shape=jax.ShapeDtypeStruct((N_TOK, D_MODEL), F32),
        compiler_params=_cparams(("parallel",)),
        name="outproj",
    )(x, ada, w_out_bf, *o_p, *o_s)


def _block_diag_ones():
    idx = np.arange(GROUP_W) // HEAD_W
    return (idx[:, None] == idx[None, :]).astype(np.float32)


def _hgrn_consts():
    C = CHUNK
    i = np.arange(C)[:, None]
    j = np.arange(C)[None, :]
    masks = []
    s = C // 2
    while s >= 1:
        up_i = (i // s) % 2 == 1
        up_j = (j // s) % 2 == 1
        masks.append(up_i & (~up_j) & (i // (2 * s) == j // (2 * s)))
        s //= 2
    masks.append(j <= i)
    fwd_m = np.stack([m.astype(np.float32) for m in masks])
    bwd_m = np.stack([m.astype(np.float32)[::-1, ::-1] for m in masks])
    tril = np.stack([(j <= i), (j >= i)]).astype(np.float32)
    return tril, np.tile(np.stack([fwd_m, bwd_m]), (1, 1, 1, 2))


def _gdn_consts():
    C = CHUNK
    i = np.arange(C)[:, None]
    t = np.arange(C)[None, :]
    tril = np.stack([(t <= i), (t >= i)]).astype(np.float32)
    masks = np.stack([np.stack([(t <= i), (t < i)]), np.stack([(t >= i), (t > i)])]).astype(np.float32)
    expand = np.zeros((2, LANE, 2 * GROUP_W), np.float32)
    for d in range(2):
        for h in range(N_HEADS):
            expand[d, d * N_HEADS + h, h * HEAD_W:(h + 1) * HEAD_W] = 1.0
            expand[d, 8 + d * N_HEADS + h, GROUP_W + h * HEAD_W:GROUP_W + (h + 1) * HEAD_W] = 1.0
    return tril, masks, expand


def _dft_consts(T):
    n2 = 4 * T
    k = np.arange(T, dtype=np.int64)[:, None]
    s = np.arange(T, dtype=np.int64)[None, :]
    ang = np.pi * (((2 * k + 1) * s) % n2).astype(np.float64) / (2 * T)
    fwd = np.concatenate([np.cos(ang), -np.sin(ang)], axis=0)
    inv = fwd.T / T
    return fwd.astype(np.float32), inv.astype(np.float32)


def _np_split2(x):
    hi = jnp.asarray(x, F32).astype(BF16)
    lo = (jnp.asarray(x, F32) - hi.astype(F32)).astype(BF16)
    return hi, lo


def _hyena_pos_consts(T):
    pos = np.arange(T, dtype=np.float32)
    t = pos / np.float32(T - 1)
    bands = np.linspace(1e-4, (HY_EMB - 1) // 2 - 1, (HY_EMB - 1) // 2, dtype=np.float32)
    ang = (np.float32(2.0 * math.pi / T) * pos[:, None]) * bands[None, :]
    z = np.concatenate([t[:, None], np.cos(ang), -np.sin(ang)], axis=-1).astype(np.float32)
    zp = np.zeros((T, LANE), np.float32)
    zp[:, :HY_EMB] = z
    max_decay = math.log(HY_TARGET) / HY_FAST
    min_decay = math.log(HY_TARGET) / HY_SLOW
    deltas = np.linspace(min_decay, max_decay, GROUP_W, dtype=np.float32)
    window = np.exp(-t[:, None] * np.abs(deltas)[None, :]).astype(np.float32)
    return zp, window


def _rope_consts(T):
    rows = T // GRID_W
    row = np.repeat(np.arange(rows, dtype=np.float32), GRID_W)
    col = (np.arange(T) % GRID_W).astype(np.float32)
    pairs = MLA_ROPE // 4
    inv = (np.float32(ROPE_BASE) ** (-np.arange(pairs, dtype=np.float32) / np.float32(pairs))).astype(np.float32)
    ang = np.concatenate([row[:, None] * inv, col[:, None] * inv], axis=-1).astype(np.float32)
    cos, sin = np.cos(ang), np.sin(ang)
    cosf = np.ones((T, LANE), np.float32)
    sinf = np.zeros((T, LANE), np.float32)
    half = MLA_ROPE // 2
    cosf[:, MLA_NOPE:MLA_NOPE + half] = cos
    cosf[:, MLA_NOPE + half:MLA_QK] = cos
    sinf[:, MLA_NOPE:MLA_NOPE + half] = -sin
    sinf[:, MLA_NOPE + half:MLA_QK] = sin
    return cosf, sinf


def _head_norm_gate(tot, bd, gn, gate):
    ms = _sel_dot_right(tot * tot, bd) * (1.0 / HEAD_W)
    return tot * lax.rsqrt(ms + RMS_EPS) * gn * _silu(gate)


def _sel_dot_right(x, c):
    h, l = _split2(x)
    return jnp.dot(h, c, preferred_element_type=F32) + jnp.dot(l, c, preferred_element_type=F32)


def _block_ref(b, two_s, r):
    C, W = b.shape
    if two_s % 8 == 0:
        b3 = b.reshape(C // two_s, two_s, W)
        return jnp.broadcast_to(b3[:, r:r + 1, :], b3.shape).reshape(C, W)
    pos = lax.broadcasted_iota(jnp.int32, b.shape, 0) % two_s
    out = b
    for p in range(two_s):
        if p != r:
            out = jnp.where(pos == p, pltpu.roll(b, (p - r) % C, 0), out)
    return out


N_PAIRS = N_HEADS // 2
HG_GROUP = 2
HG_DIRECT_MAX = 80.0


def _pair_blockdiag(x):
    lane = lax.broadcasted_iota(jnp.int32, x.shape, 1)
    zero = jnp.zeros_like(x)
    return jnp.concatenate([jnp.where(lane < HEAD_W, x, zero), jnp.where(lane >= HEAD_W, x, zero)], axis=0)


def _hgrn_kernel(*refs, T, has_s0):
    if has_s0:
        (u_ref, lb_ref, gn_ref, tril_ref, lmask_ref, bd_ref, s0_ref,
         o_ref, sfin_ref, oi_s, qin_s, up_s, dc_s, st_s) = refs
    else:
        (u_ref, lb_ref, gn_ref, tril_ref, lmask_ref, bd_ref,
         o_ref, sfin_ref, oi_s, qin_s, up_s, dc_s, st_s) = refs
    n = T // CHUNK
    C = CHUNK
    bd = bd_ref[...]
    n_lv = int(math.log2(C))

    log_lb = [jnp.log(lb_ref[d]) for d in range(2)]
    log_1mlb = [jnp.log(1.0 - lb_ref[d]) for d in range(2)]

    def gates(rows, d):
        z = u_ref[rows, (3 + d) * GROUP_W:(4 + d) * GROUP_W]
        t = jnp.exp(-jnp.abs(z))
        log_sig = jnp.minimum(z, 0.0) - jnp.log(1.0 + t)
        c = log_1mlb[d] + log_sig
        m = jnp.maximum(log_lb[d], c)
        lf = m + jnp.log(1.0 + jnp.exp(jnp.minimum(log_lb[d], c) - m))
        sig_neg = jnp.where(z > 0.0, t, 1.0) / (1.0 + t)
        return lf, (1.0 - lb_ref[d]) * sig_neg

    for d in range(2):
        if has_s0:
            st_s[d] = jnp.concatenate([s0_ref[d, h].T for h in range(N_HEADS)], axis=-1)
        else:
            st_s[d] = jnp.zeros((HEAD_W, GROUP_W), F32)

    def prepare(it, carry):
        units = [(c, d) for c in range(HG_GROUP) for d in range(2)]
        rows = [pl.ds(pl.multiple_of((it * HG_GROUP + c) * C, C), C) for c in range(HG_GROUP)]
        arow = [pl.ds(pl.multiple_of((it * HG_GROUP + c) * 8, 8), 8) for c in range(HG_GROUP)]
        q = [u_ref[rows[c], 0:GROUP_W] * (HEAD_W ** -0.5) for c in range(HG_GROUP)]
        v = [u_ref[rows[c], GROUP_W:2 * GROUP_W] for c in range(HG_GROUP)]
        vt = [[jnp.concatenate([v[c][:, h * HEAD_W:(h + 1) * HEAD_W].T for h in (2 * p, 2 * p + 1)],
                               axis=-1).astype(BF16) for p in range(N_PAIRS)]
              for c in range(HG_GROUP)]
        v_bd = [[_pair_blockdiag(v[c][:, p * LANE:(p + 1) * LANE].astype(BF16)) for p in range(N_PAIRS)]
                for c in range(HG_GROUP)]
        lf, ks = zip(*[gates(rows[c], d) for c, d in units])
        bs = []
        for i, (c, d) in enumerate(units):
            hi, lo = _split2(lf[i])
            tril = tril_ref[d]
            bs.append(jnp.dot(tril, hi, preferred_element_type=F32) + jnp.dot(tril, lo, preferred_element_type=F32))
        tot = [jnp.sum(x, axis=0, keepdims=True) for x in lf]
        mid_row = [C // 2 - 1 if d == 0 else C // 2 for c, d in units]
        spread = [jnp.maximum(jnp.abs(bs[i][0:1] - bs[i][r:r + 1]), jnp.abs(bs[i][C - 1:C] - bs[i][r:r + 1]))
                  for i, r in enumerate(mid_row)]
        widest = functools.reduce(jnp.maximum, spread)
        ko = [(ks[i] * jnp.exp(tot[i] - bs[i])).astype(BF16) for i in range(len(units))]
        up = [[jnp.dot(vt[c][p], _pair_blockdiag(ko[i][:, p * LANE:(p + 1) * LANE]), preferred_element_type=F32)
               for p in range(N_PAIRS)] for i, (c, d) in enumerate(units)]
        for i, (c, d) in enumerate(units):
            qin_s[d, rows[c], :] = q[c] * jnp.exp(bs[i])
            up_s[d, rows[c], :] = jnp.concatenate(up[i], axis=-1)
            dc_s[d, arow[c], :] = jnp.broadcast_to(jnp.exp(tot[i]), (8, GROUP_W))

        def masked_scores(qe, ke, lv):
            out = []
            for i, (c, d) in enumerate(units):
                per_pair = []
                for p in range(N_PAIRS):
                    sl = slice(p * LANE, (p + 1) * LANE)
                    prod = lax.dot_general(qe[i][:, sl], _pair_blockdiag(ke[i][:, sl]), (((1,), (1,)), ((), ())),
                                           preferred_element_type=F32)
                    per_pair.append(jnp.where(lmask_ref[d, lv] > 0.5, prod, 0.0))
                out.append(per_pair)
            return out

        def finish(sc):
            return [jnp.concatenate([jnp.dot(sc[i][p].astype(BF16), v_bd[c][p], preferred_element_type=F32)
                                     for p in range(N_PAIRS)], axis=-1) for i, (c, d) in enumerate(units)]

        def intra_direct():
            mid = [_block_ref(bs[i], C, C // 2 - 1 if d == 0 else C // 2) for i, (c, d) in enumerate(units)]
            qe = [(q[c] * jnp.exp(bs[i] - mid[i])).astype(BF16) for i, (c, d) in enumerate(units)]
            ke = [(ks[i] * jnp.exp(mid[i] - bs[i])).astype(BF16) for i in range(len(units))]
            return tuple(finish(masked_scores(qe, ke, n_lv)))

        def intra_split():
            sc = [[jnp.zeros((C, LANE), F32) for _ in range(N_PAIRS)] for _ in units]
            s = C // 2
            lv = 0
            while s >= 1:
                e = [jnp.exp(-jnp.abs(bs[i] - _block_ref(bs[i], 2 * s, s - 1 if d == 0 else s)))
                     for i, (c, d) in enumerate(units)]
                part = masked_scores([(q[c] * e[i]).astype(BF16) for i, (c, d) in enumerate(units)],
                                     [(ks[i] * e[i]).astype(BF16) for i in range(len(units))], lv)
                sc = [[sc[i][p] + part[i][p] for p in range(N_PAIRS)] for i in range(len(units))]
                s //= 2
                lv += 1
            fin = finish(sc)
            return tuple(fin[i] + _bdot(q[c] * ks[i], bd) * v[c] for i, (c, d) in enumerate(units))

        oi = lax.cond(jnp.max(widest) < HG_DIRECT_MAX, intra_direct, intra_split)
        for i, (c, d) in enumerate(units):
            oi_s[d, rows[c], :] = oi[i]
        return carry

    lax.fori_loop(0, n // HG_GROUP, prepare, 0)

    def chunk(ci, carry):
        rows = [pl.ds(pl.multiple_of(cidx * C, C), C) for cidx in (ci, n - 1 - ci)]
        decay = [dc_s[d, pl.ds(pl.multiple_of(cidx * 8, 8), 1), :] for d, cidx in ((0, ci), (1, n - 1 - ci))]
        st = [st_s[d] for d in range(2)]
        o_inter = [[lax.dot_general(qin_s[d, rows[d], p * LANE:(p + 1) * LANE].astype(BF16),
                                    _pair_blockdiag(st[d][:, p * LANE:(p + 1) * LANE].astype(BF16)),
                                    (((1,), (1,)), ((), ())), preferred_element_type=F32)
                    for p in range(N_PAIRS)] for d in range(2)]
        for d in range(2):
            st_s[d] = st[d] * decay[d] + up_s[d, rows[d], :]
            oi_s[d, rows[d], :] = oi_s[d, rows[d], :] + jnp.concatenate(o_inter[d], axis=-1)
        return carry

    lax.fori_loop(0, n, chunk, 0, unroll=2)
    o_ref[...] = _head_norm_gate(oi_s[0] + oi_s[1], bd, gn_ref[...], u_ref[:, 2 * GROUP_W:3 * GROUP_W])
    for d in range(2):
        for h in range(N_HEADS):
            sfin_ref[d, h] = st_s[d][:, h * HEAD_W:(h + 1) * HEAD_W].T


def _hgrn_call(u_hg, lb_l, gn, consts, s0, l, T, nb, row0):
    tril, lmask, bd = consts
    tb = row0 // T
    has_s0 = s0 is not None
    in_specs = [
        pl.BlockSpec((T, HG_COLS), lambda b: (tb + b, 0)),
        pl.BlockSpec((None, 2, 1, GROUP_W), lambda b: (l, 0, 0, 0)),
        pl.BlockSpec((None, 1, GROUP_W), lambda b: (l, 0, 0)),
        pl.BlockSpec((2, CHUNK, CHUNK), lambda b: (0, 0, 0)),
        pl.BlockSpec((2, 7, CHUNK, LANE), lambda b: (0, 0, 0, 0)),
        pl.BlockSpec((GROUP_W, GROUP_W), lambda b: (0, 0)),
    ]
    args = [u_hg, lb_l.reshape(DEPTH, 2, 1, GROUP_W), gn.reshape(DEPTH, 1, GROUP_W), tril, lmask, bd]
    if has_s0:
        in_specs.append(pl.BlockSpec((None, None, 2, N_HEADS, HEAD_W, HEAD_W), lambda b: (b, l, 0, 0, 0, 0)))
        args.append(s0)
    seq = pltpu.VMEM((2, T, GROUP_W), F32)
    return pl.pallas_call(
        functools.partial(_hgrn_kernel, T=T, has_s0=has_s0),
        grid=(nb,),
        in_specs=in_specs,
        out_specs=[
            pl.BlockSpec((T, GROUP_W), lambda b: (b, 0)),
            pl.BlockSpec((None, 2, N_HEADS, HEAD_W, HEAD_W), lambda b: (b, 0, 0, 0, 0)),
        ],
        out_shape=[jax.ShapeDtypeStruct((nb * T, GROUP_W), F32),
                   jax.ShapeDtypeStruct((nb, 2, N_HEADS, HEAD_W, HEAD_W), F32)],
        scratch_shapes=[seq, seq, seq,
                        pltpu.VMEM((2, T // CHUNK * 8, GROUP_W), F32), pltpu.VMEM((2, HEAD_W, GROUP_W), F32)],
        compiler_params=_cparams(("parallel",)),
        name="hgrn",
    )(*args)


def _shift_rows(x, T):
    row = lax.broadcasted_iota(jnp.int32, x.shape, 0)
    prev = jnp.where(row == 0, 0.0, pltpu.roll(x, 1, 0))
    nxt = jnp.where(row == T - 1, 0.0, pltpu.roll(x, T - 1, 0))
    return prev, nxt


def _conv3(x, w_ref, T):
    prev, nxt = _shift_rows(x, T)
    return prev * w_ref[0:1, :] + x * w_ref[1:2, :] + nxt * w_ref[2:3, :]


GDN_UNROLL = 2


def _solve_unit_lower(systems):
    c2 = 2 * CHUNK
    slabs = [jnp.concatenate([nmat, nmat, rhs], axis=-1) for rhs, nmat in systems]
    steps = int(math.log2(CHUNK))
    for step in range(steps):
        last = step == steps - 1
        nxt = []
        for slab in slabs:
            hi = slab.astype(BF16)
            lo = (slab - hi.astype(F32)).astype(BF16)
            lhs = jnp.concatenate([hi[:, :c2], lo[:, :CHUNK]], axis=-1)
            first = c2 if last else 0
            rhs3 = jnp.concatenate([hi[:, first:], lo[:, first:], hi[:, first:]], axis=0)
            prod = jnp.dot(lhs, rhs3, preferred_element_type=F32)
            if last:
                nxt.append(slab[:, c2:] + prod)
            else:
                nxt.append(jnp.concatenate([prod[:, :c2], slab[:, c2:] + prod[:, c2:]], axis=-1))
        slabs = nxt
    return slabs


def _gdn_kernel(*refs, T, has_s0):
    if has_s0:
        (u_ref, cw_ref, alog_ref, dtb_ref, exp_ref, tril_ref, mask_ref, bd_ref, gn_ref, s0_ref,
         o_ref, sfin_ref, q_s, k_s, v_s, la_s, be_s, uw_s, ww_s, at_s, qin_s, kt_s, al_s, of_s, st_s) = refs
    else:
        (u_ref, cw_ref, alog_ref, dtb_ref, exp_ref, tril_ref, mask_ref, bd_ref, gn_ref,
         o_ref, sfin_ref, q_s, k_s, v_s, la_s, be_s, uw_s, ww_s, at_s, qin_s, kt_s, al_s, of_s, st_s) = refs
    n = T // CHUNK
    C = CHUNK
    bd = bd_ref[...]

    qkv = _silu(_conv3(u_ref[:, 0:3 * GROUP_W], cw_ref, T))
    q = qkv[:, 0:GROUP_W]
    k = qkv[:, GROUP_W:2 * GROUP_W]
    q_s[...] = q * lax.rsqrt(_sel_dot_right(q * q, bd) + 1e-6) * (HEAD_W ** -0.5)
    k_s[...] = k * lax.rsqrt(_sel_dot_right(k * k, bd) + 1e-6)
    v_s[...] = qkv[:, 2 * GROUP_W:3 * GROUP_W]

    ab = u_ref[:, 4 * GROUP_W:4 * GROUP_W + LANE]
    xa = ab + dtb_ref[...]
    softplus = jnp.maximum(xa, 0.0) + jnp.log(1.0 + jnp.exp(-jnp.abs(xa)))
    log_a = -jnp.exp(alog_ref[...]) * softplus
    lane = lax.broadcasted_iota(jnp.int32, ab.shape, 1)
    narrow = jnp.where(lane < 8, log_a, _sigmoid(ab))
    for d in range(2):
        wide = _dot_sel(narrow, exp_ref[d])
        la_s[d] = wide[:, 0:GROUP_W]
        be_s[d] = wide[:, GROUP_W:2 * GROUP_W]
        if has_s0:
            st_s[d] = jnp.concatenate([s0_ref[d, h] for h in range(N_HEADS)], axis=-1)
        else:
            st_s[d] = jnp.zeros((HEAD_W, GROUP_W), F32)

    def prepare(cidx):
        r0 = pl.multiple_of(cidx * C, C)
        rows = pl.ds(r0, C)
        arow = pl.ds(pl.multiple_of(cidx * 8, 8), 8)
        q = q_s[rows, :]
        k = k_s[rows, :]
        v = v_s[rows, :]
        systems = []
        attns = []
        kts = []
        for d in range(2):
            incl = mask_ref[d, 0] > 0.5
            strict = mask_ref[d, 1]
            la = la_s[d, rows, :]
            be = be_s[d, rows, :]
            gx = _sel_dot(tril_ref[d], la)
            gtot = jnp.sum(la, axis=0, keepdims=True)
            eg = jnp.exp(gx)
            kout = k * jnp.exp(gtot - gx)
            qin_s[d, rows, :] = q * eg
            al_s[d, arow, :] = jnp.broadcast_to(jnp.exp(gtot), (8, GROUP_W))
            kb = k * be
            vb = v * be
            kbg = kb * eg
            for h in range(N_HEADS):
                sl = slice(h * HEAD_W, (h + 1) * HEAD_W)
                gh = gx[:, sl]
                dmat = gh - gh.T
                dec = jnp.where(incl, jnp.exp(jnp.where(incl, dmat, 0.0)), 0.0)
                qk = _bdot_nt(jnp.concatenate([kb[:, sl], q[:, sl]], axis=0), k[:, sl])
                nmat = -(qk[:C] * dec * strict)
                systems.append((jnp.concatenate([vb[:, sl], kbg[:, sl]], axis=-1), nmat))
                attns.append(qk[C:] * dec)
                kts.append(kout[:, sl].T)
        sols = _solve_unit_lower(systems)
        for d in range(2):
            mine = sols[d * N_HEADS:(d + 1) * N_HEADS]
            uw_s[d, rows, :] = jnp.concatenate([x[:, :HEAD_W] for x in mine], axis=-1)
            ww_s[d, rows, :] = jnp.concatenate([x[:, HEAD_W:] for x in mine], axis=-1)
            at_s[d, rows, :] = jnp.concatenate(attns[d * N_HEADS:(d + 1) * N_HEADS], axis=-1)
            kt_s[d, rows, :] = jnp.concatenate(kts[d * N_HEADS:(d + 1) * N_HEADS], axis=-1)

    def prep_body(i, carry):
        for j in range(GDN_UNROLL):
            prepare(i * GDN_UNROLL + j)
        return carry

    lax.fori_loop(0, n // GDN_UNROLL, prep_body, 0)

    def chunk(ci, carry):
        units = [(d, p) for d in range(2) for p in range(N_PAIRS)]
        rows = []
        alast = []
        for d, cidx in ((0, ci), (1, n - 1 - ci)):
            rows.append(pl.ds(pl.multiple_of(cidx * C, C), C))
            alast.append(al_s[d, pl.ds(pl.multiple_of(cidx * 8, 8), 1), :])
        st = [st_s[d] for d in range(2)]
        lanes = [slice(p * LANE, (p + 1) * LANE) for p in range(N_PAIRS)]
        both = [jnp.dot(jnp.concatenate([ww_s[d, rows[d], lanes[p]], qin_s[d, rows[d], lanes[p]]], axis=0).astype(BF16),
                        _pair_blockdiag(st[d][:, lanes[p]].astype(BF16)), preferred_element_type=F32)
                for d, p in units]
        vnew = [uw_s[d, rows[d], lanes[p]] - both[i][:C] for i, (d, p) in enumerate(units)]
        upd = [jnp.dot(jnp.concatenate([at_s[d, rows[d], lanes[p]], kt_s[d, rows[d], lanes[p]]], axis=0).astype(BF16),
                       _pair_blockdiag(vnew[i].astype(BF16)), preferred_element_type=F32)
               for i, (d, p) in enumerate(units)]
        for d in range(2):
            idx = range(d * N_PAIRS, (d + 1) * N_PAIRS)
            of_s[d, rows[d], :] = jnp.concatenate([both[i][C:] + upd[i][:C] for i in idx], axis=-1)
            st_s[d] = st[d] * alast[d] + jnp.concatenate([upd[i][C:] for i in idx], axis=-1)
        return carry

    lax.fori_loop(0, n, chunk, 0)
    o_ref[...] = _head_norm_gate(of_s[0] + of_s[1], bd, gn_ref[...], u_ref[:, 3 * GROUP_W:4 * GROUP_W])
    for d in range(2):
        for h in range(N_HEADS):
            sfin_ref[d, h] = st_s[d][:, h * HEAD_W:(h + 1) * HEAD_W]


def _gdn_call(u_gd, cw, alog, dtb, gn, consts, s0, l, T, nb, row0):
    tril, masks, expand, bd = consts
    tb = row0 // T
    has_s0 = s0 is not None
    in_specs = [
        pl.BlockSpec((T, GD_PAD), lambda b: (tb + b, 0)),
        pl.BlockSpec((None, 3, 3 * GROUP_W), lambda b: (l, 0, 0)),
        pl.BlockSpec((None, 1, LANE), lambda b: (l, 0, 0)),
        pl.BlockSpec((None, 1, LANE), lambda b: (l, 0, 0)),
        pl.BlockSpec((2, LANE, 2 * GROUP_W), lambda b: (0, 0, 0)),
        pl.BlockSpec((2, CHUNK, CHUNK), lambda b: (0, 0, 0)),
        pl.BlockSpec((2, 2, CHUNK, CHUNK), lambda b: (0, 0, 0, 0)),
        pl.BlockSpec((GROUP_W, GROUP_W), lambda b: (0, 0)),
        pl.BlockSpec((None, 1, GROUP_W), lambda b: (l, 0, 0)),
    ]
    args = [u_gd, cw, alog, dtb, expand, tril, masks, bd, gn]
    if has_s0:
        in_specs.append(pl.BlockSpec((None, None, 2, N_HEADS, HEAD_W, HEAD_W), lambda b: (b, l, 0, 0, 0, 0)))
        args.append(s0)
    seq = pltpu.VMEM((2, T, GROUP_W), F32)
    return pl.pallas_call(
        functools.partial(_gdn_kernel, T=T, has_s0=has_s0),
        grid=(nb,),
        in_specs=in_specs,
        out_specs=[
            pl.BlockSpec((T, GROUP_W), lambda b: (b, 0)),
            pl.BlockSpec((None, 2, N_HEADS, HEAD_W, HEAD_W), lambda b: (b, 0, 0, 0, 0)),
        ],
        out_shape=[jax.ShapeDtypeStruct((nb * T, GROUP_W), F32),
                   jax.ShapeDtypeStruct((nb, 2, N_HEADS, HEAD_W, HEAD_W), F32)],
        scratch_shapes=[pltpu.VMEM((T, GROUP_W), F32)] * 3 + [seq, seq, seq, seq, seq, seq, seq,
            pltpu.VMEM((2, T // CHUNK * 8, GROUP_W), F32), seq, pltpu.VMEM((2, HEAD_W, GROUP_W), F32)],
        compiler_params=_cparams(("parallel",)),
        name="gdn",
    )(*args)


def _hyfilt_kernel(z_ref, win_ref, fh_ref, fl_ref, w1_ref, b1_ref, fr_ref, w2_ref, b2_ref, w3_ref, o_ref, *, T):
    fr = fr_ref[...]
    h = jnp.sin(fr * (_dot3(z_ref[...], w1_ref[...]) + b1_ref[...]))
    h = jnp.sin(fr * (_dot3(h, w2_ref[...]) + b2_ref[...]))
    h = _dot3(h, w3_ref[...])
    win = win_ref[...]
    hf = h[:, 0:GROUP_W] * win
    hb = h[:, GROUP_W:2 * GROUP_W] * win
    row = lax.broadcasted_iota(jnp.int32, hb.shape, 0)
    hb = jnp.where(row == 0, 0.0, hb)
    taps = jnp.concatenate([hf + hb, hf - hb], axis=-1).astype(BF16)
    spec = (jnp.dot(fh_ref[...], taps, preferred_element_type=F32)
            + jnp.dot(fl_ref[...], taps, preferred_element_type=F32))
    o_ref[0:T, :] = spec[0:T, 0:GROUP_W]
    o_ref[T:2 * T, :] = spec[T:2 * T, GROUP_W:2 * GROUP_W]


def _hyfilt_call(T, zp, win, fh, fl, w1p, b1, freq, w2, b2, w3):
    c2 = lambda l: (0, 0)
    return pl.pallas_call(
        functools.partial(_hyfilt_kernel, T=T),
        grid=(DEPTH,),
        in_specs=[
            pl.BlockSpec((T, LANE), c2),
            pl.BlockSpec((T, GROUP_W), c2),
            pl.BlockSpec((2 * T, T), c2),
            pl.BlockSpec((2 * T, T), c2),
            pl.BlockSpec((None, LANE, HY_FH), lambda l: (l, 0, 0)),
            pl.BlockSpec((None, 1, HY_FH), lambda l: (l, 0, 0)),
            pl.BlockSpec((None, 1, HY_FH), lambda l: (l, 0, 0)),
            pl.BlockSpec((None, HY_FH, HY_FH), lambda l: (l, 0, 0)),
            pl.BlockSpec((None, 1, HY_FH), lambda l: (l, 0, 0)),
            pl.BlockSpec((None, HY_FH, 2 * GROUP_W), lambda l: (l, 0, 0)),
        ],
        out_specs=pl.BlockSpec((None, 2 * T, GROUP_W), lambda l: (l, 0, 0)),
        out_shape=jax.ShapeDtypeStruct((DEPTH, 2 * T, GROUP_W), F32),
        compiler_params=_cparams(("parallel",)),
        name="hyfilt",
    )(zp, win, fh, fl, w1p, b1, freq, w2, b2, w3)


def _hyena_kernel(u_ref, cw_ref, cb_ref, spec_ref, skip_ref, fh_ref, fl_ref, ih_ref, il_ref, o_ref, *, T):
    uc = _conv3(u_ref[...], cw_ref, T) + cb_ref[...]
    x0 = uc[:, 0:GROUP_W]
    z = uc[:, GROUP_W:2 * GROUP_W] * uc[:, 2 * GROUP_W:3 * GROUP_W]
    zb = z.astype(BF16)
    zs = (jnp.dot(fh_ref[...], zb, preferred_element_type=F32)
          + jnp.dot(fl_ref[...], zb, preferred_element_type=F32))
    ar, ai = zs[0:T], zs[T:2 * T]
    br, bi = spec_ref[0:T, :], spec_ref[T:2 * T, :]
    pb = jnp.concatenate([ar * br - ai * bi, ar * bi + ai * br], axis=0).astype(BF16)
    y = (jnp.dot(ih_ref[...], pb, preferred_element_type=F32)
         + jnp.dot(il_ref[...], pb, preferred_element_type=F32))
    o_ref[...] = x0 * (y + z * skip_ref[...])


def _hyena_call(u_hy, cw, cb, spec, skip, dft, l, T, nb, row0):
    fh, fl, ih, il = dft
    tb = row0 // T
    c2 = lambda b: (0, 0)
    return pl.pallas_call(
        functools.partial(_hyena_kernel, T=T),
        grid=(nb,),
        in_specs=[
            pl.BlockSpec((T, HY_COLS), lambda b: (tb + b, 0)),
            pl.BlockSpec((None, 3, HY_COLS), lambda b: (l, 0, 0)),
            pl.BlockSpec((None, 1, HY_COLS), lambda b: (l, 0, 0)),
            pl.BlockSpec((None, 2 * T, GROUP_W), lambda b: (l, 0, 0)),
            pl.BlockSpec((None, 1, GROUP_W), lambda b: (l, 0, 0)),
            pl.BlockSpec((2 * T, T), c2),
            pl.BlockSpec((2 * T, T), c2),
            pl.BlockSpec((T, 2 * T), c2),
            pl.BlockSpec((T, 2 * T), c2),
        ],
        out_specs=pl.BlockSpec((T, GROUP_W), lambda b: (b, 0)),
        out_shape=jax.ShapeDtypeStruct((nb * T, GROUP_W), F32),
        compiler_params=_cparams(("parallel",)),
        name="hyena",
    )(u_hy, cw, cb, spec, skip, fh, fl, ih, il)


def _rope(x, cosf, sinf):
    lane = lax.broadcasted_iota(jnp.int32, x.shape, 1)
    half = MLA_ROPE // 2
    partner = jnp.where(lane < MLA_NOPE + half, pltpu.roll(x, LANE - half, 1), pltpu.roll(x, half, 1))
    return x * cosf + partner * sinf


def _qk_norm(x, g):
    ms = jnp.sum(x * x, axis=-1, keepdims=True) * (1.0 / MLA_QK)
    return x * lax.rsqrt(ms + RMS_EPS) * g


def _mla_kernel(*refs, T, ctx):
    if ctx:
        (u_ref, qn_ref, wq_ref, kvn_ref, wkv_ref, qkn_ref, cos_ref, sin_ref, cckv_ref, ckr_ref, o_ref) = refs
    else:
        (u_ref, qn_ref, wq_ref, kvn_ref, wkv_ref, qkn_ref, o_ref, ckv_ref, kr_ref) = refs
    u = u_ref[...]
    cq = _rms(u[:, 0:MLA_Q_LORA], qn_ref[...])
    ckv = _rms(u[:, MLA_Q_LORA:MLA_Q_LORA + MLA_KV_LORA], kvn_ref[...])
    kr = u[:, MLA_Q_LORA + MLA_KV_LORA:MLA_Q_LORA + MLA_KV_LORA + MLA_ROPE]
    if not ctx:
        ckv_ref[...] = ckv
        kr_ref[...] = kr
    q_all = _bdot(cq, wq_ref[...])
    kv = _bdot(ckv, wkv_ref[...])
    gq = qkn_ref[0:1, :]
    gk = qkn_ref[1:2, :]
    if ctx:
        kvc = _bdot(cckv_ref[...], wkv_ref[...])
        krc = ckr_ref[...]
        cosf, sinf = cos_ref[...], sin_ref[...]
    scale = MLA_QK ** -0.5
    outs = []
    for h in range(N_HEADS):
        qh = _qk_norm(q_all[:, h * LANE:(h + 1) * LANE], gq)
        zpad = jnp.zeros((T, LANE - MLA_QK), F32)
        kh = _qk_norm(jnp.concatenate([kv[:, h * HEAD_W:(h + 1) * HEAD_W], kr, zpad], axis=-1), gk)
        vh = kv[:, GROUP_W + h * HEAD_W:GROUP_W + (h + 1) * HEAD_W]
        if ctx:
            qh = _rope(qh, cosf, sinf)
            kh = _rope(kh, cosf, sinf)
            s_len = krc.shape[0]
            zc = jnp.zeros((s_len, LANE - MLA_QK), F32)
            kc = _qk_norm(jnp.concatenate([kvc[:, h * HEAD_W:(h + 1) * HEAD_W], krc, zc], axis=-1), gk)
            kh = jnp.concatenate([kh, kc], axis=0)
            vh = jnp.concatenate([vh, kvc[:, GROUP_W + h * HEAD_W:GROUP_W + (h + 1) * HEAD_W]], axis=0)
        khb = kh.astype(BF16)
        vhb = vh.astype(BF16)
        blocks = []
        for qb in range(T // ATT_QBLOCK):
            s = _bdot_nt(qh[qb * ATT_QBLOCK:(qb + 1) * ATT_QBLOCK], khb) * scale
            e = jnp.exp(s - jnp.max(s, axis=-1, keepdims=True))
            blocks.append(_bdot(e, vhb) / jnp.sum(e, axis=-1, keepdims=True))
        outs.append(blocks[0] if len(blocks) == 1 else jnp.concatenate(blocks, axis=0))
    o_ref[...] = jnp.concatenate(outs, axis=-1)


def _mla_call(u_mla, qn, wq, kvn, wkv, qkn, rope, cache, l, T, nb, row0):
    tb = row0 // T
    ctx = cache is not None
    c2 = lambda b: (0, 0)
    in_specs = [
        pl.BlockSpec((T, MLA_PAD), lambda b: (tb + b, 0)),
        pl.BlockSpec((None, 1, MLA_Q_LORA), lambda b: (l, 0, 0)),
        pl.BlockSpec((None, MLA_Q_LORA, N_HEADS * LANE), lambda b: (l, 0, 0)),
        pl.BlockSpec((None, 1, MLA_KV_LORA), lambda b: (l, 0, 0)),
        pl.BlockSpec((None, MLA_KV_LORA, 2 * GROUP_W), lambda b: (l, 0, 0)),
        pl.BlockSpec((None, 2, LANE), lambda b: (l, 0, 0)),
    ]
    args = [u_mla, qn, wq, kvn, wkv, qkn]
    out_specs = [pl.BlockSpec((T, GROUP_W), lambda b: (b, 0))]
    out_shape = [jax.ShapeDtypeStruct((nb * T, GROUP_W), F32)]
    if ctx:
        in_specs += [
            pl.BlockSpec((T, LANE), c2),
            pl.BlockSpec((T, LANE), c2),
            pl.BlockSpec((None, None, PAST_LEN, MLA_KV_LORA), lambda b: (b, l, 0, 0)),
            pl.BlockSpec((None, None, PAST_LEN, MLA_ROPE), lambda b: (b, l, 0, 0)),
        ]
        args += [rope[0], rope[1], cache[0], cache[1]]
    else:
        out_specs += [pl.BlockSpec((T, MLA_KV_LORA), lambda b: (b, 0)),
                      pl.BlockSpec((T, MLA_ROPE), lambda b: (b, 0))]
        out_shape += [jax.ShapeDtypeStruct((nb * T, MLA_KV_LORA), F32),
                      jax.ShapeDtypeStruct((nb * T, MLA_ROPE), F32)]
    return pl.pallas_call(
        functools.partial(_mla_kernel, T=T, ctx=ctx),
        grid=(nb,),
        in_specs=in_specs,
        out_specs=out_specs,
        out_shape=out_shape,
        compiler_params=_cparams(("parallel",)),
        name="mla",
    )(*args)


def _pad_cols(w, width):
    return jnp.pad(w, [(0, 0)] * (w.ndim - 1) + [(0, width - w.shape[-1])])


W_IN_PREP_ROWS = 256


def _prep_w_in_kernel(w_ref, o_hg, o_hy, o_mla, o_gd):
    w = w_ref[...]
    o1 = HG_COLS
    o2 = o1 + HY_COLS
    o3 = o2 + MLA_COLS
    rows = w.shape[0]
    o_hg[...] = w[:, :o1].astype(BF16)
    o_hy[...] = w[:, o1:o2].astype(BF16)
    o_mla[...] = jnp.concatenate([w[:, o2:o3], jnp.zeros((rows, MLA_PAD - MLA_COLS), F32)], axis=-1).astype(BF16)
    o_gd[...] = jnp.concatenate([w[:, o3:], jnp.zeros((rows, GD_PAD - GD_COLS), F32)], axis=-1).astype(BF16)


def _prep_w_in(w_in):
    widths = (HG_COLS, HY_COLS, MLA_PAD, GD_PAD)
    n_cols = w_in.shape[-1]
    return pl.pallas_call(
        _prep_w_in_kernel,
        grid=(DEPTH, D_MODEL // W_IN_PREP_ROWS),
        in_specs=[pl.BlockSpec((None, W_IN_PREP_ROWS, n_cols), lambda l, r: (l, r, 0))],
        out_specs=[pl.BlockSpec((None, W_IN_PREP_ROWS, w), lambda l, r: (l, r, 0)) for w in widths],
        out_shape=[jax.ShapeDtypeStruct((DEPTH, D_MODEL, w), BF16) for w in widths],
        compiler_params=_cparams(("parallel", "parallel")),
        name="w_in_prep",
    )(w_in)


def _prep_wq(w_q_up):
    w = w_q_up.reshape(DEPTH, MLA_Q_LORA, N_HEADS, MLA_QK)
    return _pad_cols(w, LANE).reshape(DEPTH, MLA_Q_LORA, N_HEADS * LANE).astype(BF16)


def _prep_wkv(w_kv_up):
    w = w_kv_up.reshape(DEPTH, MLA_KV_LORA, N_HEADS, 2, HEAD_W)
    return w.transpose(0, 1, 3, 2, 4).reshape(DEPTH, MLA_KV_LORA, 2 * GROUP_W).astype(BF16)


def _lower_bounds(hgrn_lb):
    lb = jnp.cumsum(jax.nn.softmax(hgrn_lb.astype(F32), axis=0), axis=0)
    return lb - lb[0]


def kernel(x_prompt, x_sample, cache_mla_ckv, cache_mla_krope, state_hgrn, state_gdn, c, c_ctx, w_ada, b_ada, norm_ffn, w_ffn_gu, w_ffn_down, norm_mix, w_in, w_out, hgrn_lb, hgrn_norm, hy_conv_w, hy_conv_b, hy_w1, hy_b1, hy_freq, hy_w2, hy_b2, hy_w3, hy_skip, mla_q_norm_a, mla_w_q_up, mla_kv_norm_a, mla_w_kv_up, mla_qk_norm, gdn_conv_w, gdn_a_log, gdn_dt_bias, gdn_norm):
    x = (x_prompt.reshape(N_PROMPT, D_MODEL), x_sample.reshape(N_SAMPLE, D_MODEL))

    cond8 = jnp.zeros((8, D_MODEL), F32).at[0].set(c_ctx).at[1:1 + DEC_BATCH].set(c)
    ada = _ada_call(cond8, w_ada, b_ada)

    w_in_parts = _prep_w_in(w_in)
    w_out_bf = w_out.astype(BF16)
    wq = _prep_wq(mla_w_q_up)
    wkv = _prep_wkv(mla_w_kv_up)
    qkn = _pad_cols(mla_qk_norm, LANE)
    lb_all = _lower_bounds(hgrn_lb)
    alog = _pad_cols(gdn_a_log.reshape(DEPTH, 1, 8), LANE)
    dtb = _pad_cols(gdn_dt_bias.reshape(DEPTH, 1, 8), LANE)
    gdn_gn = jnp.tile(gdn_norm, (1, N_HEADS)).reshape(DEPTH, 1, GROUP_W)
    w1p = jnp.pad(hy_w1, ((0, 0), (0, LANE - HY_EMB), (0, 0)))

    bd = jnp.asarray(_block_diag_ones(), BF16)
    hg_tril, hg_m = _hgrn_consts()
    hg_consts = (jnp.asarray(hg_tril, BF16), jnp.asarray(hg_m, F32), bd)
    gd_tril, gd_masks, gd_expand = _gdn_consts()
    gd_consts = (jnp.asarray(gd_tril, BF16), jnp.asarray(gd_masks, F32), jnp.asarray(gd_expand, BF16), bd)
    rope = tuple(jnp.asarray(a) for a in _rope_consts(DEC_SEQ))
    groups = ((SEQ, BATCH, 0), (DEC_SEQ, DEC_BATCH, N_PROMPT))
    dft = {}
    spec = {}
    for T, _, _ in groups:
        fwd, inv = _dft_consts(T)
        fh, fl = _np_split2(fwd)
        ih, il = _np_split2(inv)
        dft[T] = (fh, fl, ih, il)
        zp, win = _hyena_pos_consts(T)
        spec[T] = _hyfilt_call(T, jnp.asarray(zp), jnp.asarray(win), fh, fl, w1p,
                               hy_b1.reshape(DEPTH, 1, HY_FH), hy_freq.reshape(DEPTH, 1, HY_FH), hy_w2,
                               hy_b2.reshape(DEPTH, 1, HY_FH), hy_w3)

    new_ckv, new_kr, new_hg, new_gd = [], [], [], []
    for l in range(DEPTH):
        x = _ffn_call(x, ada, norm_ffn, w_ffn_gu, w_ffn_down, l, 0)
        u_hg, u_hy, u_mla, u_gd = _inproj_call(x, ada, norm_mix, w_in_parts, l)
        outs = []
        for gi, (T, nb, row0) in enumerate(groups):
            latent = gi == 1
            o_hg, s_hg = _hgrn_call(u_hg, lb_all, hgrn_norm, hg_consts,
                                    state_hgrn if latent else None, l, T, nb, row0)
            o_hy = _hyena_call(u_hy, hy_conv_w, hy_conv_b.reshape(DEPTH, 1, HY_COLS), spec[T],
                               hy_skip.reshape(DEPTH, 1, GROUP_W), dft[T], l, T, nb, row0)
            mla = _mla_call(u_mla, mla_q_norm_a.reshape(DEPTH, 1, MLA_Q_LORA), wq,
                            mla_kv_norm_a.reshape(DEPTH, 1, MLA_KV_LORA), wkv, qkn,
                            rope if latent else None,
                            (cache_mla_ckv, cache_mla_krope) if latent else None, l, T, nb, row0)
            o_gd, s_gd = _gdn_call(u_gd, gdn_conv_w, alog, dtb, gdn_gn, gd_consts,
                                   state_gdn if latent else None, l, T, nb, row0)
            outs.append((o_hg, o_hy, mla[0], o_gd))
            if not latent:
                new_ckv.append(mla[1].reshape(BATCH, SEQ, MLA_KV_LORA))
                new_kr.append(mla[2].reshape(BATCH, SEQ, MLA_ROPE))
                new_hg.append(s_hg)
                new_gd.append(s_gd)
        x = _outproj_call(x, ada, w_out_bf, outs[0], outs[1], l)
        x = _ffn_call(x, ada, norm_ffn, w_ffn_gu, w_ffn_down, l, 1)

    y_prompt = x[:N_PROMPT].reshape(BATCH, SEQ, D_MODEL)
    y_sample = x[N_PROMPT:].reshape(DEC_BATCH, DEC_SEQ, D_MODEL)
    return (y_prompt, y_sample, jnp.stack(new_ckv, axis=1), jnp.stack(new_kr, axis=1),
            jnp.stack(new_hg, axis=1), jnp.stack(new_gd, axis=1))
```

```python
import functools
import math

import numpy as np
import jax
import jax.numpy as jnp
from jax import lax
from jax.experimental import pallas as pl
from jax.experimental.pallas import tpu as pltpu

F32 = jnp.float32
BF16 = jnp.bfloat16

D_MODEL = 1024
BATCH = 16
SEQ = 256
DEPTH = 4
DEC_BATCH = 2
DEC_SEQ = 1024
PAST_LEN = 256
GRID_W = 64
N_ADA = 9
D_FF = 2816
GROUP_W = 256
CHUNK = 64
RMS_EPS = 1e-6
N_HEADS = 4
HEAD_W = 64
HY_EMB = 33
HY_FH = 64
HY_TARGET = 1e-2
HY_FAST = 0.3
HY_SLOW = 1.5
MLA_NOPE = 64
MLA_ROPE = 32
MLA_QK = MLA_NOPE + MLA_ROPE
MLA_Q_LORA = 256
MLA_KV_LORA = 128
ROPE_BASE = 10000.0

HG_COLS = 5 * GROUP_W
HY_COLS = 3 * GROUP_W
MLA_COLS = MLA_Q_LORA + MLA_KV_LORA + MLA_ROPE
GD_COLS = 4 * GROUP_W + 16
MLA_PAD = 512
GD_PAD = 1152

N_PROMPT = BATCH * SEQ
N_SAMPLE = DEC_BATCH * DEC_SEQ
N_TOK = N_PROMPT + N_SAMPLE
LANE = 128
VMEM_LIMIT = 56 * 1024 * 1024
ROW_TILE = 1024
FF_TILE = 256
ADA_TILE = 1536
ATT_QBLOCK = 256


def _bdot(a, b):
    return jnp.dot(a.astype(BF16), b.astype(BF16), preferred_element_type=F32)


def _bdot_nt(a, b):
    return lax.dot_general(a.astype(BF16), b.astype(BF16), (((1,), (1,)), ((), ())),
                           preferred_element_type=F32)


def _bdot_tn(a, b):
    return lax.dot_general(a.astype(BF16), b.astype(BF16), (((0,), (0,)), ((), ())),
                           preferred_element_type=F32)


def _split2(x):
    hi = x.astype(BF16)
    lo = (x - hi.astype(F32)).astype(BF16)
    return hi, lo


def _split3(x):
    hi = x.astype(BF16)
    r = x - hi.astype(F32)
    mid = r.astype(BF16)
    lo = (r - mid.astype(F32)).astype(BF16)
    return hi, mid, lo


def _dot3(a, b):
    ah, al = _split2(a)
    bh, bl = _split2(b)
    return (jnp.dot(ah, bh, preferred_element_type=F32) + jnp.dot(ah, bl, preferred_element_type=F32)
            + jnp.dot(al, bh, preferred_element_type=F32))


def _sel_dot(c, x):
    h, m, l = _split3(x)
    return (jnp.dot(c, h, preferred_element_type=F32) + jnp.dot(c, m, preferred_element_type=F32)
            + jnp.dot(c, l, preferred_element_type=F32))


def _dot_sel(x, c):
    h, m, l = _split3(x)
    return (jnp.dot(h, c, preferred_element_type=F32) + jnp.dot(m, c, preferred_element_type=F32)
            + jnp.dot(l, c, preferred_element_type=F32))


def _sigmoid(x):
    return 1.0 / (1.0 + jnp.exp(-x))


def _silu(x):
    return x * _sigmoid(x)


def _rms(x, g):
    return x * lax.rsqrt(jnp.mean(x * x, axis=-1, keepdims=True) + RMS_EPS) * g


def _cparams(sem):
    return pltpu.CompilerParams(dimension_semantics=sem, vmem_limit_bytes=VMEM_LIMIT)


def _cond_of_tile(i):
    return jnp.maximum(i - (N_PROMPT // ROW_TILE - 1), 0)


def _ada_kernel(c_ref, w_ref, b_ref, o_ref):
    o_ref[...] = _dot3(_silu(c_ref[...]), w_ref[...]) + b_ref[...]


def _ada_call(cond8, w_ada, b_ada):
    n = N_ADA * D_MODEL
    out = pl.pallas_call(
        _ada_kernel,
        grid=(DEPTH, n // ADA_TILE),
        in_specs=[
            pl.BlockSpec((8, D_MODEL), lambda l, j: (0, 0)),
            pl.BlockSpec((None, D_MODEL, ADA_TILE), lambda l, j: (l, 0, j)),
            pl.BlockSpec((None, 1, ADA_TILE), lambda l, j: (l, 0, j)),
        ],
        out_specs=pl.BlockSpec((None, 8, ADA_TILE), lambda l, j: (l, 0, j)),
        out_shape=jax.ShapeDtypeStruct((DEPTH, 8, n), F32),
        compiler_params=_cparams(("parallel", "parallel")),
        name="ada",
    )(cond8, w_ada, b_ada.reshape(DEPTH, 1, n))
    return out.reshape(DEPTH, 8, N_ADA, D_MODEL)


FFN_SUBTILES = 2


def _ffn_kernel(*refs, sub, split):
    nx = 1 if split is None else 2
    x_refs = refs[:nx]
    ada_refs = refs[nx:nx + FFN_SUBTILES]
    g_ref, wg_ref, wu_ref, wd_ref, o_ref, h_scr = refs[nx + FFN_SUBTILES:]
    i = pl.program_id(0)
    f = pl.program_id(1)

    def prologue(x_ref):
        for r, ada_ref in enumerate(ada_refs):
            rows = slice(r * ROW_TILE, (r + 1) * ROW_TILE)
            y = _rms(x_ref[rows, :], g_ref[...])
            h = y * (1.0 + ada_ref[3 * sub + 1:3 * sub + 2, :]) + ada_ref[3 * sub:3 * sub + 1, :]
            h_scr[rows, :] = h.astype(BF16)
        o_ref[...] = jnp.zeros_like(o_ref)

    def epilogue(x_ref):
        for r, ada_ref in enumerate(ada_refs):
            rows = slice(r * ROW_TILE, (r + 1) * ROW_TILE)
            o_ref[rows, :] = x_ref[rows, :] + 0.5 * ada_ref[3 * sub + 2:3 * sub + 3, :] * o_ref[rows, :]

    def on(step, fn):
        if split is None:
            pl.when(f == step)(functools.partial(fn, x_refs[0]))
        else:
            pl.when((f == step) & (i < split))(functools.partial(fn, x_refs[0]))
            pl.when((f == step) & (i >= split))(functools.partial(fn, x_refs[1]))

    on(0, prologue)
    wg = wg_ref[...].astype(BF16)
    wu = wu_ref[...].astype(BF16)
    wd = wd_ref[...].astype(BF16)
    for r in range(FFN_SUBTILES):
        rows = slice(r * ROW_TILE, (r + 1) * ROW_TILE)
        h = h_scr[rows, :]
        gate = jnp.dot(h, wg, preferred_element_type=F32)
        up = jnp.dot(h, wu, preferred_element_type=F32)
        a = (_silu(gate) * up).astype(BF16)
        o_ref[rows, :] += jnp.dot(a, wd, preferred_element_type=F32)
    on(pl.num_programs(1) - 1, epilogue)


def _ffn_call(xs, ada, norm_ffn, w_gu, w_down, l, j):
    sub = 2 * j
    nf = D_FF // FF_TILE
    rows = FFN_SUBTILES * ROW_TILE

    def ada_spec(r):
        return pl.BlockSpec((None, None, N_ADA, D_MODEL),
                            lambda i, f: (l, _cond_of_tile(i * FFN_SUBTILES + r), 0, 0))

    if isinstance(xs, tuple):
        split = xs[0].shape[0] // rows
        n_tail = xs[1].shape[0] // rows
        x_specs = [pl.BlockSpec((rows, D_MODEL), lambda i, f: (jnp.minimum(i, split - 1), 0),
                                pipeline_mode=pl.Buffered(1)),
                   pl.BlockSpec((rows, D_MODEL), lambda i, f: (jnp.clip(i - split, 0, n_tail - 1), 0),
                                pipeline_mode=pl.Buffered(1))]
    else:
        split = None
        xs = (xs,)
        x_specs = [pl.BlockSpec((rows, D_MODEL), lambda i, f: (i, 0), pipeline_mode=pl.Buffered(1))]

    return pl.pallas_call(
        functools.partial(_ffn_kernel, sub=sub, split=split),
        grid=(N_TOK // rows, nf),
        in_specs=x_specs
        + [ada_spec(r) for r in range(FFN_SUBTILES)] + [
            pl.BlockSpec((None, None, 1, D_MODEL), lambda i, f: (l, j, 0, 0)),
            pl.BlockSpec((None, None, D_MODEL, FF_TILE), lambda i, f: (l, j, 0, f)),
            pl.BlockSpec((None, None, D_MODEL, FF_TILE), lambda i, f: (l, j, 0, nf + f)),
            pl.BlockSpec((None, None, FF_TILE, D_MODEL), lambda i, f: (l, j, f, 0)),
        ],
        out_specs=pl.BlockSpec((rows, D_MODEL), lambda i, f: (i, 0)),
        out_shape=jax.ShapeDtypeStruct((N_TOK, D_MODEL), F32),
        scratch_shapes=[pltpu.VMEM((rows, D_MODEL), BF16)],
        compiler_params=_cparams(("parallel", "arbitrary")),
        name="ffn",
    )(*xs, *([ada] * FFN_SUBTILES), norm_ffn.reshape(DEPTH, 2, 1, D_MODEL), w_gu, w_gu, w_down)


IN_TILE = 512


def _inproj_kernel(x_ref, ada_ref, g_ref, w1, w2, w3, w4, o1, o2, o3, o4):
    y = _rms(x_ref[...], g_ref[...])
    h = (y * (1.0 + ada_ref[4:5, :]) + ada_ref[3:4, :]).astype(BF16)
    for w, o in ((w1, o1), (w2, o2), (w3, o3), (w4, o4)):
        o[...] = jnp.dot(h, w[...], preferred_element_type=F32)


def _inproj_call(x, ada, norm_mix, ws, l):
    widths = (HG_COLS, HY_COLS, MLA_PAD, GD_PAD)
    per = ROW_TILE // IN_TILE
    return pl.pallas_call(
        _inproj_kernel,
        grid=(N_TOK // IN_TILE,),
        in_specs=[
            pl.BlockSpec((IN_TILE, D_MODEL), lambda i: (i, 0)),
            pl.BlockSpec((None, None, N_ADA, D_MODEL), lambda i: (l, _cond_of_tile(i // per), 0, 0)),
            pl.BlockSpec((None, 1, D_MODEL), lambda i: (l, 0, 0)),
        ] + [pl.BlockSpec((None, D_MODEL, w), lambda i: (l, 0, 0)) for w in widths],
        out_specs=[pl.BlockSpec((IN_TILE, w), lambda i: (i, 0)) for w in widths],
        out_shape=[jax.ShapeDtypeStruct((N_TOK, w), F32) for w in widths],
        compiler_params=_cparams(("parallel",)),
        name="inproj",
    )(x, ada, norm_mix.reshape(DEPTH, 1, D_MODEL), *ws)


OUT_TILE = 512


def _outproj_kernel(x_ref, ada_ref, w_ref, *refs):
    o_ref = refs[-1]
    i = pl.program_id(0)
    n_p = N_PROMPT // OUT_TILE

    def run(srcs):
        acc = jnp.zeros((OUT_TILE, D_MODEL), F32)
        for g, s in enumerate(srcs):
            acc += jnp.dot(s[...].astype(BF16), w_ref[g * GROUP_W:(g + 1) * GROUP_W, :],
                           preferred_element_type=F32)
        o_ref[...] = x_ref[...] + ada_ref[5:6, :] * acc

    @pl.when(i < n_p)
    def _():
        run(refs[0:4])

    @pl.when(i >= n_p)
    def _():
        run(refs[4:8])


def _outproj_call(x, ada, w_out_bf, o_p, o_s, l):
    per = ROW_TILE // OUT_TILE
    n_p = N_PROMPT // OUT_TILE
    n_s = N_SAMPLE // OUT_TILE
    return pl.pallas_call(
        _outproj_kernel,
        grid=(N_TOK // OUT_TILE,),
        in_specs=[
            pl.BlockSpec((OUT_TILE, D_MODEL), lambda i: (i, 0)),
            pl.BlockSpec((None, None, N_ADA, D_MODEL), lambda i: (l, _cond_of_tile(i // per), 0, 0)),
            pl.BlockSpec((None, D_MODEL, D_MODEL), lambda i: (l, 0, 0)),
        ] + [pl.BlockSpec((OUT_TILE, GROUP_W), lambda i: (jnp.minimum(i, n_p - 1), 0))] * 4
          + [pl.BlockSpec((OUT_TILE, GROUP_W), lambda i: (jnp.clip(i - n_p, 0, n_s - 1), 0))] * 4,
        out_specs=pl.BlockSpec((OUT_TILE, D_MODEL), lambda i: (i, 0)),
        out_shape=jax.ShapeDtypeStruct((N_TOK, D_MODEL), F32),
        compiler_params=_cparams(("parallel",)),
        name="outproj",
    )(x, ada, w_out_bf, *o_p, *o_s)


def _block_diag_ones():
    idx = np.arange(GROUP_W) // HEAD_W
    return (idx[:, None] == idx[None, :]).astype(np.float32)


def _hgrn_consts():
    C = CHUNK
    i = np.arange(C)[:, None]
    j = np.arange(C)[None, :]
    masks = []
    s = C // 2
    while s >= 1:
        up_i = (i // s) % 2 == 1
        up_j = (j // s) % 2 == 1
        masks.append(up_i & (~up_j) & (i // (2 * s) == j // (2 * s)))
        s //= 2
    masks.append(j <= i)
    fwd_m = np.stack([m.astype(np.float32) for m in masks])
    bwd_m = np.stack([m.astype(np.float32)[::-1, ::-1] for m in masks])
    tril = np.stack([(j <= i), (j >= i)]).astype(np.float32)
    return tril, np.tile(np.stack([fwd_m, bwd_m]), (1, 1, 1, 2))


def _gdn_consts():
    C = CHUNK
    i = np.arange(C)[:, None]
    t = np.arange(C)[None, :]
    tril = np.stack([(t <= i), (t >= i)]).astype(np.float32)
    masks = np.stack([np.stack([(t <= i), (t < i)]), np.stack([(t >= i), (t > i)])]).astype(np.float32)
    expand = np.zeros((2, LANE, 2 * GROUP_W), np.float32)
    for d in range(2):
        for h in range(N_HEADS):
            expand[d, d * N_HEADS + h, h * HEAD_W:(h + 1) * HEAD_W] = 1.0
            expand[d, 8 + d * N_HEADS + h, GROUP_W + h * HEAD_W:GROUP_W + (h + 1) * HEAD_W] = 1.0
    return tril, masks, expand


def _dft_consts(T):
    n2 = 4 * T
    k = np.arange(T, dtype=np.int64)[:, None]
    s = np.arange(T, dtype=np.int64)[None, :]
    ang = np.pi * (((2 * k + 1) * s) % n2).astype(np.float64) / (2 * T)
    fwd = np.concatenate([np.cos(ang), -np.sin(ang)], axis=0)
    inv = fwd.T / T
    return fwd.astype(np.float32), inv.astype(np.float32)


def _np_split2(x):
    hi = jnp.asarray(x, F32).astype(BF16)
    lo = (jnp.asarray(x, F32) - hi.astype(F32)).astype(BF16)
    return hi, lo


def _hyena_pos_consts(T):
    pos = np.arange(T, dtype=np.float32)
    t = pos / np.float32(T - 1)
    bands = np.linspace(1e-4, (HY_EMB - 1) // 2 - 1, (HY_EMB - 1) // 2, dtype=np.float32)
    ang = (np.float32(2.0 * math.pi / T) * pos[:, None]) * bands[None, :]
    z = np.concatenate([t[:, None], np.cos(ang), -np.sin(ang)], axis=-1).astype(np.float32)
    zp = np.zeros((T, LANE), np.float32)
    zp[:, :HY_EMB] = z
    max_decay = math.log(HY_TARGET) / HY_FAST
    min_decay = math.log(HY_TARGET) / HY_SLOW
    deltas = np.linspace(min_decay, max_decay, GROUP_W, dtype=np.float32)
    window = np.exp(-t[:, None] * np.abs(deltas)[None, :]).astype(np.float32)
    return zp, window


def _rope_consts(T):
    rows = T // GRID_W
    row = np.repeat(np.arange(rows, dtype=np.float32), GRID_W)
    col = (np.arange(T) % GRID_W).astype(np.float32)
    pairs = MLA_ROPE // 4
    inv = (np.float32(ROPE_BASE) ** (-np.arange(pairs, dtype=np.float32) / np.float32(pairs))).astype(np.float32)
    ang = np.concatenate([row[:, None] * inv, col[:, None] * inv], axis=-1).astype(np.float32)
    cos, sin = np.cos(ang), np.sin(ang)
    cosf = np.ones((T, LANE), np.float32)
    sinf = np.zeros((T, LANE), np.float32)
    half = MLA_ROPE // 2
    cosf[:, MLA_NOPE:MLA_NOPE + half] = cos
    cosf[:, MLA_NOPE + half:MLA_QK] = cos
    sinf[:, MLA_NOPE:MLA_NOPE + half] = -sin
    sinf[:, MLA_NOPE + half:MLA_QK] = sin
    return cosf, sinf


def _head_norm_gate(tot, bd, gn, gate):
    ms = _sel_dot_right(tot * tot, bd) * (1.0 / HEAD_W)
    return tot * lax.rsqrt(ms + RMS_EPS) * gn * _silu(gate)


def _sel_dot_right(x, c):
    h, l = _split2(x)
    return jnp.dot(h, c, preferred_element_type=F32) + jnp.dot(l, c, preferred_element_type=F32)


def _block_ref(b, two_s, r):
    C, W = b.shape
    if two_s % 8 == 0:
        b3 = b.reshape(C // two_s, two_s, W)
        return jnp.broadcast_to(b3[:, r:r + 1, :], b3.shape).reshape(C, W)
    pos = lax.broadcasted_iota(jnp.int32, b.shape, 0) % two_s
    out = b
    for p in range(two_s):
        if p != r:
            out = jnp.where(pos == p, pltpu.roll(b, (p - r) % C, 0), out)
    return out


N_PAIRS = N_HEADS // 2
HG_GROUP = 2
HG_DIRECT_MAX = 80.0


def _pair_blockdiag(x):
    lane = lax.broadcasted_iota(jnp.int32, x.shape, 1)
    zero = jnp.zeros_like(x)
    return jnp.concatenate([jnp.where(lane < HEAD_W, x, zero), jnp.where(lane >= HEAD_W, x, zero)], axis=0)


def _hgrn_kernel(*refs, T, has_s0):
    if has_s0:
        (u_ref, lb_ref, gn_ref, tril_ref, lmask_ref, bd_ref, s0_ref,
         o_ref, sfin_ref, oi_s, qin_s, up_s, dc_s, st_s) = refs
    else:
        (u_ref, lb_ref, gn_ref, tril_ref, lmask_ref, bd_ref, _,
         o_ref, sfin_ref, oi_s, qin_s, up_s, dc_s, st_s) = refs
    n = T // CHUNK
    C = CHUNK
    bd = bd_ref[...]
    n_lv = int(math.log2(C))

    log_lb = [jnp.log(lb_ref[d]) for d in range(2)]
    log_1mlb = [jnp.log(1.0 - lb_ref[d]) for d in range(2)]

    def gates(rows, d):
        z = u_ref[rows, (3 + d) * GROUP_W:(4 + d) * GROUP_W]
        t = jnp.exp(-jnp.abs(z))
        log_sig = jnp.minimum(z, 0.0) - jnp.log(1.0 + t)
        c = log_1mlb[d] + log_sig
        m = jnp.maximum(log_lb[d], c)
        lf = m + jnp.log(1.0 + jnp.exp(jnp.minimum(log_lb[d], c) - m))
        sig_neg = jnp.where(z > 0.0, t, 1.0) / (1.0 + t)
        return lf, (1.0 - lb_ref[d]) * sig_neg

    for d in range(2):
        if has_s0:
            st_s[d] = jnp.concatenate([s0_ref[d, h].T for h in range(N_HEADS)], axis=-1)
        else:
            st_s[d] = jnp.zeros((HEAD_W, GROUP_W), F32)

    def prepare(it, carry):
        units = [(c, d) for c in range(HG_GROUP) for d in range(2)]
        rows = [pl.ds(pl.multiple_of((it * HG_GROUP + c) * C, C), C) for c in range(HG_GROUP)]
        arow = [pl.ds(pl.multiple_of((it * HG_GROUP + c) * 8, 8), 8) for c in range(HG_GROUP)]
        q = [u_ref[rows[c], 0:GROUP_W] * (HEAD_W ** -0.5) for c in range(HG_GROUP)]
        v = [u_ref[rows[c], GROUP_W:2 * GROUP_W] for c in range(HG_GROUP)]
        vt = [[jnp.concatenate([v[c][:, h * HEAD_W:(h + 1) * HEAD_W].T for h in (2 * p, 2 * p + 1)],
                               axis=-1).astype(BF16) for p in range(N_PAIRS)]
              for c in range(HG_GROUP)]
        v_bd = [[_pair_blockdiag(v[c][:, p * LANE:(p + 1) * LANE].astype(BF16)) for p in range(N_PAIRS)]
                for c in range(HG_GROUP)]
        lf, ks = zip(*[gates(rows[c], d) for c, d in units])
        bs = []
        for i, (c, d) in enumerate(units):
            hi, lo = _split2(lf[i])
            tril = tril_ref[d]
            bs.append(jnp.dot(tril, hi, preferred_element_type=F32) + jnp.dot(tril, lo, preferred_element_type=F32))
        tot = [jnp.sum(x, axis=0, keepdims=True) for x in lf]
        mid_row = [C // 2 - 1 if d == 0 else C // 2 for c, d in units]
        spread = [jnp.maximum(jnp.abs(bs[i][0:1] - bs[i][r:r + 1]), jnp.abs(bs[i][C - 1:C] - bs[i][r:r + 1]))
                  for i, r in enumerate(mid_row)]
        widest = functools.reduce(jnp.maximum, spread)
        ko = [(ks[i] * jnp.exp(tot[i] - bs[i])).astype(BF16) for i in range(len(units))]
        up = [[jnp.dot(vt[c][p], _pair_blockdiag(ko[i][:, p * LANE:(p + 1) * LANE]), preferred_element_type=F32)
               for p in range(N_PAIRS)] for i, (c, d) in enumerate(units)]
        for i, (c, d) in enumerate(units):
            qin_s[d, rows[c], :] = q[c] * jnp.exp(bs[i])
            up_s[d, rows[c], :] = jnp.concatenate(up[i], axis=-1)
            dc_s[d, arow[c], :] = jnp.broadcast_to(jnp.exp(tot[i]), (8, GROUP_W))

        def masked_scores(qe, ke, lv):
            out = []
            for i, (c, d) in enumerate(units):
                per_pair = []
                for p in range(N_PAIRS):
                    sl = slice(p * LANE, (p + 1) * LANE)
                    prod = lax.dot_general(qe[i][:, sl], _pair_blockdiag(ke[i][:, sl]), (((1,), (1,)), ((), ())),
                                           preferred_element_type=F32)
                    per_pair.append(jnp.where(lmask_ref[d, lv] > 0.5, prod, 0.0))
                out.append(per_pair)
            return out

        def finish(sc):
            return [jnp.concatenate([jnp.dot(sc[i][p].astype(BF16), v_bd[c][p], preferred_element_type=F32)
                                     for p in range(N_PAIRS)], axis=-1) for i, (c, d) in enumerate(units)]

        def intra_direct():
            mid = [_block_ref(bs[i], C, C // 2 - 1 if d == 0 else C // 2) for i, (c, d) in enumerate(units)]
            qe = [(q[c] * jnp.exp(bs[i] - mid[i])).astype(BF16) for i, (c, d) in enumerate(units)]
            ke = [(ks[i] * jnp.exp(mid[i] - bs[i])).astype(BF16) for i in range(len(units))]
            return tuple(finish(masked_scores(qe, ke, n_lv)))

        def intra_split():
            sc = [[jnp.zeros((C, LANE), F32) for _ in range(N_PAIRS)] for _ in units]
            s = C // 2
            lv = 0
            while s >= 1:
                e = [jnp.exp(-jnp.abs(bs[i] - _block_ref(bs[i], 2 * s, s - 1 if d == 0 else s)))
                     for i, (c, d) in enumerate(units)]
                part = masked_scores([(q[c] * e[i]).astype(BF16) for i, (c, d) in enumerate(units)],
                                     [(ks[i] * e[i]).astype(BF16) for i in range(len(units))], lv)
                sc = [[sc[i][p] + part[i][p] for p in range(N_PAIRS)] for i in range(len(units))]
                s //= 2
                lv += 1
            fin = finish(sc)
            return tuple(fin[i] + _bdot(q[c] * ks[i], bd) * v[c] for i, (c, d) in enumerate(units))

        oi = lax.cond(jnp.max(widest) < HG_DIRECT_MAX, intra_direct, intra_split)
        for i, (c, d) in enumerate(units):
            oi_s[d, rows[c], :] = oi[i]
        return carry

    lax.fori_loop(0, n // HG_GROUP, prepare, 0)

    def chunk(ci, carry):
        rows = [pl.ds(pl.multiple_of(cidx * C, C), C) for cidx in (ci, n - 1 - ci)]
        decay = [dc_s[d, pl.ds(pl.multiple_of(cidx * 8, 8), 1), :] for d, cidx in ((0, ci), (1, n - 1 - ci))]
        st = [st_s[d] for d in range(2)]
        o_inter = [[lax.dot_general(qin_s[d, rows[d], p * LANE:(p + 1) * LANE].astype(BF16),
                                    _pair_blockdiag(st[d][:, p * LANE:(p + 1) * LANE].astype(BF16)),
                                    (((1,), (1,)), ((), ())), preferred_element_type=F32)
                    for p in range(N_PAIRS)] for d in range(2)]
        for d in range(2):
            st_s[d] = st[d] * decay[d] + up_s[d, rows[d], :]
            oi_s[d, rows[d], :] = oi_s[d, rows[d], :] + jnp.concatenate(o_inter[d], axis=-1)
        return carry

    lax.fori_loop(0, n, chunk, 0, unroll=2)
    o_ref[...] = _head_norm_gate(oi_s[0] + oi_s[1], bd, gn_ref[...], u_ref[:, 2 * GROUP_W:3 * GROUP_W])
    for d in range(2):
        for h in range(N_HEADS):
            sfin_ref[d, h] = st_s[d][:, h * HEAD_W:(h + 1) * HEAD_W].T


def _state_io(in_specs, args, s0, collect, l, nb):
    state_block = (None, None, 2, N_HEADS, HEAD_W, HEAD_W)
    if s0 is not None:
        in_specs.append(pl.BlockSpec(state_block, lambda b: (b, l, 0, 0, 0, 0)))
        args.append(s0)
        return (pl.BlockSpec(state_block[1:], lambda b: (b, 0, 0, 0, 0)),
                jax.ShapeDtypeStruct((nb, 2, N_HEADS, HEAD_W, HEAD_W), F32), {})
    in_specs.append(pl.BlockSpec(memory_space=pl.ANY))
    args.append(collect)
    return (pl.BlockSpec(state_block, lambda b: (b, l, 0, 0, 0, 0)),
            jax.ShapeDtypeStruct(collect.shape, F32), {len(args) - 1: 1})


def _hgrn_call(u_hg, lb_l, gn, consts, s0, collect, l, T, nb, row0):
    tril, lmask, bd = consts
    tb = row0 // T
    has_s0 = s0 is not None
    in_specs = [
        pl.BlockSpec((T, HG_COLS), lambda b: (tb + b, 0)),
        pl.BlockSpec((None, 2, 1, GROUP_W), lambda b: (l, 0, 0, 0)),
        pl.BlockSpec((None, 1, GROUP_W), lambda b: (l, 0, 0)),
        pl.BlockSpec((2, CHUNK, CHUNK), lambda b: (0, 0, 0)),
        pl.BlockSpec((2, 7, CHUNK, LANE), lambda b: (0, 0, 0, 0)),
        pl.BlockSpec((GROUP_W, GROUP_W), lambda b: (0, 0)),
    ]
    args = [u_hg, lb_l.reshape(DEPTH, 2, 1, GROUP_W), gn.reshape(DEPTH, 1, GROUP_W), tril, lmask, bd]
    state_spec, state_shape, aliases = _state_io(in_specs, args, s0, collect, l, nb)
    seq = pltpu.VMEM((2, T, GROUP_W), F32)
    return pl.pallas_call(
        functools.partial(_hgrn_kernel, T=T, has_s0=has_s0),
        grid=(nb,),
        in_specs=in_specs,
        out_specs=[pl.BlockSpec((T, GROUP_W), lambda b: (b, 0)), state_spec],
        out_shape=[jax.ShapeDtypeStruct((nb * T, GROUP_W), F32), state_shape],
        input_output_aliases=aliases,
        scratch_shapes=[seq, seq, seq,
                        pltpu.VMEM((2, T // CHUNK * 8, GROUP_W), F32), pltpu.VMEM((2, HEAD_W, GROUP_W), F32)],
        compiler_params=_cparams(("parallel",)),
        name="hgrn",
    )(*args)


def _shift_rows(x, T):
    row = lax.broadcasted_iota(jnp.int32, x.shape, 0)
    prev = jnp.where(row == 0, 0.0, pltpu.roll(x, 1, 0))
    nxt = jnp.where(row == T - 1, 0.0, pltpu.roll(x, T - 1, 0))
    return prev, nxt


def _conv3(x, w_ref, T):
    prev, nxt = _shift_rows(x, T)
    return prev * w_ref[0:1, :] + x * w_ref[1:2, :] + nxt * w_ref[2:3, :]


GDN_UNROLL = 2


def _solve_unit_lower(systems):
    c2 = 2 * CHUNK
    slabs = [jnp.concatenate([nmat, nmat, rhs], axis=-1) for rhs, nmat in systems]
    steps = int(math.log2(CHUNK))
    for step in range(steps):
        last = step == steps - 1
        nxt = []
        for slab in slabs:
            hi = slab.astype(BF16)
            lo = (slab - hi.astype(F32)).astype(BF16)
            lhs = jnp.concatenate([hi[:, :c2], lo[:, :CHUNK]], axis=-1)
            first = c2 if last else 0
            rhs3 = jnp.concatenate([hi[:, first:], lo[:, first:], hi[:, first:]], axis=0)
            prod = jnp.dot(lhs, rhs3, preferred_element_type=F32)
            if last:
                nxt.append(slab[:, c2:] + prod)
            else:
                nxt.append(jnp.concatenate([prod[:, :c2], slab[:, c2:] + prod[:, c2:]], axis=-1))
        slabs = nxt
    return slabs


def _gdn_kernel(*refs, T, has_s0):
    if has_s0:
        (u_ref, cw_ref, alog_ref, dtb_ref, exp_ref, tril_ref, mask_ref, bd_ref, gn_ref, s0_ref,
         o_ref, sfin_ref, q_s, k_s, v_s, la_s, be_s, uw_s, ww_s, at_s, qin_s, kt_s, al_s, of_s, st_s) = refs
    else:
        (u_ref, cw_ref, alog_ref, dtb_ref, exp_ref, tril_ref, mask_ref, bd_ref, gn_ref, _,
         o_ref, sfin_ref, q_s, k_s, v_s, la_s, be_s, uw_s, ww_s, at_s, qin_s, kt_s, al_s, of_s, st_s) = refs
    n = T // CHUNK
    C = CHUNK
    bd = bd_ref[...]

    qkv = _silu(_conv3(u_ref[:, 0:3 * GROUP_W], cw_ref, T))
    q = qkv[:, 0:GROUP_W]
    k = qkv[:, GROUP_W:2 * GROUP_W]
    q_s[...] = q * lax.rsqrt(_sel_dot_right(q * q, bd) + 1e-6) * (HEAD_W ** -0.5)
    k_s[...] = k * lax.rsqrt(_sel_dot_right(k * k, bd) + 1e-6)
    v_s[...] = qkv[:, 2 * GROUP_W:3 * GROUP_W]

    ab = u_ref[:, 4 * GROUP_W:4 * GROUP_W + LANE]
    xa = ab + dtb_ref[...]
    softplus = jnp.maximum(xa, 0.0) + jnp.log(1.0 + jnp.exp(-jnp.abs(xa)))
    log_a = -jnp.exp(alog_ref[...]) * softplus
    lane = lax.broadcasted_iota(jnp.int32, ab.shape, 1)
    narrow = jnp.where(lane < 8, log_a, _sigmoid(ab))
    for d in range(2):
        wide = _dot_sel(narrow, exp_ref[d])
        la_s[d] = wide[:, 0:GROUP_W]
        be_s[d] = wide[:, GROUP_W:2 * GROUP_W]
        if has_s0:
            st_s[d] = jnp.concatenate([s0_ref[d, h] for h in range(N_HEADS)], axis=-1)
        else:
            st_s[d] = jnp.zeros((HEAD_W, GROUP_W), F32)

    def prepare(cidx):
        r0 = pl.multiple_of(cidx * C, C)
        rows = pl.ds(r0, C)
        arow = pl.ds(pl.multiple_of(cidx * 8, 8), 8)
        q = q_s[rows, :]
        k = k_s[rows, :]
        v = v_s[rows, :]
        systems = []
        attns = []
        kts = []
        for d in range(2):
            incl = mask_ref[d, 0] > 0.5
            strict = mask_ref[d, 1]
            la = la_s[d, rows, :]
            be = be_s[d, rows, :]
            gx = _sel_dot(tril_ref[d], la)
            gtot = jnp.sum(la, axis=0, keepdims=True)
            eg = jnp.exp(gx)
            kout = k * jnp.exp(gtot - gx)
            qin_s[d, rows, :] = q * eg
            al_s[d, arow, :] = jnp.broadcast_to(jnp.exp(gtot), (8, GROUP_W))
            kb = k * be
            vb = v * be
            kbg = kb * eg
            for h in range(N_HEADS):
                sl = slice(h * HEAD_W, (h + 1) * HEAD_W)
                gh = gx[:, sl]
                dmat = gh - gh.T
                dec = jnp.where(incl, jnp.exp(jnp.where(incl, dmat, 0.0)), 0.0)
                qk = _bdot_nt(jnp.concatenate([kb[:, sl], q[:, sl]], axis=0), k[:, sl])
                nmat = -(qk[:C] * dec * strict)
                systems.append((jnp.concatenate([vb[:, sl], kbg[:, sl]], axis=-1), nmat))
                attns.append(qk[C:] * dec)
                kts.append(kout[:, sl].T)
        sols = _solve_unit_lower(systems)
        for d in range(2):
            mine = sols[d * N_HEADS:(d + 1) * N_HEADS]
            uw_s[d, rows, :] = jnp.concatenate([x[:, :HEAD_W] for x in mine], axis=-1)
            ww_s[d, rows, :] = jnp.concatenate([x[:, HEAD_W:] for x in mine], axis=-1)
            at_s[d, rows, :] = jnp.concatenate(attns[d * N_HEADS:(d + 1) * N_HEADS], axis=-1)
            kt_s[d, rows, :] = jnp.concatenate(kts[d * N_HEADS:(d + 1) * N_HEADS], axis=-1)

    def prep_body(i, carry):
        for j in range(GDN_UNROLL):
            prepare(i * GDN_UNROLL + j)
        return carry

    lax.fori_loop(0, n // GDN_UNROLL, prep_body, 0)

    def chunk(ci, carry):
        units = [(d, p) for d in range(2) for p in range(N_PAIRS)]
        rows = []
        alast = []
        for d, cidx in ((0, ci), (1, n - 1 - ci)):
            rows.append(pl.ds(pl.multiple_of(cidx * C, C), C))
            alast.append(al_s[d, pl.ds(pl.multiple_of(cidx * 8, 8), 1), :])
        st = [st_s[d] for d in range(2)]
        lanes = [slice(p * LANE, (p + 1) * LANE) for p in range(N_PAIRS)]
        both = [jnp.dot(jnp.concatenate([ww_s[d, rows[d], lanes[p]], qin_s[d, rows[d], lanes[p]]], axis=0).astype(BF16),
                        _pair_blockdiag(st[d][:, lanes[p]].astype(BF16)), preferred_element_type=F32)
                for d, p in units]
        vnew = [uw_s[d, rows[d], lanes[p]] - both[i][:C] for i, (d, p) in enumerate(units)]
        upd = [jnp.dot(jnp.concatenate([at_s[d, rows[d], lanes[p]], kt_s[d, rows[d], lanes[p]]], axis=0).astype(BF16),
                       _pair_blockdiag(vnew[i].astype(BF16)), preferred_element_type=F32)
               for i, (d, p) in enumerate(units)]
        for d in range(2):
            idx = range(d * N_PAIRS, (d + 1) * N_PAIRS)
            of_s[d, rows[d], :] = jnp.concatenate([both[i][C:] + upd[i][:C] for i in idx], axis=-1)
            st_s[d] = st[d] * alast[d] + jnp.concatenate([upd[i][C:] for i in idx], axis=-1)
        return carry

    lax.fori_loop(0, n, chunk, 0)
    o_ref[...] = _head_norm_gate(of_s[0] + of_s[1], bd, gn_ref[...], u_ref[:, 3 * GROUP_W:4 * GROUP_W])
    for d in range(2):
        for h in range(N_HEADS):
            sfin_ref[d, h] = st_s[d][:, h * HEAD_W:(h + 1) * HEAD_W]


def _gdn_call(u_gd, cw, alog, dtb, gn, consts, s0, collect, l, T, nb, row0):
    tril, masks, expand, bd = consts
    tb = row0 // T
    has_s0 = s0 is not None
    in_specs = [
        pl.BlockSpec((T, GD_PAD), lambda b: (tb + b, 0)),
        pl.BlockSpec((None, 3, 3 * GROUP_W), lambda b: (l, 0, 0)),
        pl.BlockSpec((None, 1, LANE), lambda b: (l, 0, 0)),
        pl.BlockSpec((None, 1, LANE), lambda b: (l, 0, 0)),
        pl.BlockSpec((2, LANE, 2 * GROUP_W), lambda b: (0, 0, 0)),
        pl.BlockSpec((2, CHUNK, CHUNK), lambda b: (0, 0, 0)),
        pl.BlockSpec((2, 2, CHUNK, CHUNK), lambda b: (0, 0, 0, 0)),
        pl.BlockSpec((GROUP_W, GROUP_W), lambda b: (0, 0)),
        pl.BlockSpec((None, 1, GROUP_W), lambda b: (l, 0, 0)),
    ]
    args = [u_gd, cw, alog, dtb, expand, tril, masks, bd, gn]
    state_spec, state_shape, aliases = _state_io(in_specs, args, s0, collect, l, nb)
    seq = pltpu.VMEM((2, T, GROUP_W), F32)
    return pl.pallas_call(
        functools.partial(_gdn_kernel, T=T, has_s0=has_s0),
        grid=(nb,),
        in_specs=in_specs,
        out_specs=[pl.BlockSpec((T, GROUP_W), lambda b: (b, 0)), state_spec],
        out_shape=[jax.ShapeDtypeStruct((nb * T, GROUP_W), F32), state_shape],
        input_output_aliases=aliases,
        scratch_shapes=[pltpu.VMEM((T, GROUP_W), F32)] * 3 + [seq, seq, seq, seq, seq, seq, seq,
            pltpu.VMEM((2, T // CHUNK * 8, GROUP_W), F32), seq, pltpu.VMEM((2, HEAD_W, GROUP_W), F32)],
        compiler_params=_cparams(("parallel",)),
        name="gdn",
    )(*args)


def _hyfilt_kernel(z_ref, win_ref, fh_ref, fl_ref, w1_ref, b1_ref, fr_ref, w2_ref, b2_ref, w3_ref, o_ref, *, T):
    fr = fr_ref[...]
    h = jnp.sin(fr * (_dot3(z_ref[...], w1_ref[...]) + b1_ref[...]))
    h = jnp.sin(fr * (_dot3(h, w2_ref[...]) + b2_ref[...]))
    h = _dot3(h, w3_ref[...])
    win = win_ref[...]
    hf = h[:, 0:GROUP_W] * win
    hb = h[:, GROUP_W:2 * GROUP_W] * win
    row = lax.broadcasted_iota(jnp.int32, hb.shape, 0)
    hb = jnp.where(row == 0, 0.0, hb)
    taps = jnp.concatenate([hf + hb, hf - hb], axis=-1).astype(BF16)
    spec = (jnp.dot(fh_ref[...], taps, preferred_element_type=F32)
            + jnp.dot(fl_ref[...], taps, preferred_element_type=F32))
    o_ref[0:T, :] = spec[0:T, 0:GROUP_W]
    o_ref[T:2 * T, :] = spec[T:2 * T, GROUP_W:2 * GROUP_W]


def _hyfilt_call(T, zp, win, fh, fl, w1p, b1, freq, w2, b2, w3):
    c2 = lambda l: (0, 0)
    return pl.pallas_call(
        functools.partial(_hyfilt_kernel, T=T),
        grid=(DEPTH,),
        in_specs=[
            pl.BlockSpec((T, LANE), c2),
            pl.BlockSpec((T, GROUP_W), c2),
            pl.BlockSpec((2 * T, T), c2),
            pl.BlockSpec((2 * T, T), c2),
            pl.BlockSpec((None, LANE, HY_FH), lambda l: (l, 0, 0)),
            pl.BlockSpec((None, 1, HY_FH), lambda l: (l, 0, 0)),
            pl.BlockSpec((None, 1, HY_FH), lambda l: (l, 0, 0)),
            pl.BlockSpec((None, HY_FH, HY_FH), lambda l: (l, 0, 0)),
            pl.BlockSpec((None, 1, HY_FH), lambda l: (l, 0, 0)),
            pl.BlockSpec((None, HY_FH, 2 * GROUP_W), lambda l: (l, 0, 0)),
        ],
        out_specs=pl.BlockSpec((None, 2 * T, GROUP_W), lambda l: (l, 0, 0)),
        out_shape=jax.ShapeDtypeStruct((DEPTH, 2 * T, GROUP_W), F32),
        compiler_params=_cparams(("parallel",)),
        name="hyfilt",
    )(zp, win, fh, fl, w1p, b1, freq, w2, b2, w3)


def _hyena_kernel(u_ref, cw_ref, cb_ref, spec_ref, skip_ref, fh_ref, fl_ref, ih_ref, il_ref, o_ref, *, T):
    uc = _conv3(u_ref[...], cw_ref, T) + cb_ref[...]
    x0 = uc[:, 0:GROUP_W]
    z = uc[:, GROUP_W:2 * GROUP_W] * uc[:, 2 * GROUP_W:3 * GROUP_W]
    zb = z.astype(BF16)
    zs = (jnp.dot(fh_ref[...], zb, preferred_element_type=F32)
          + jnp.dot(fl_ref[...], zb, preferred_element_type=F32))
    ar, ai = zs[0:T], zs[T:2 * T]
    br, bi = spec_ref[0:T, :], spec_ref[T:2 * T, :]
    pb = jnp.concatenate([ar * br - ai * bi, ar * bi + ai * br], axis=0).astype(BF16)
    y = (jnp.dot(ih_ref[...], pb, preferred_element_type=F32)
         + jnp.dot(il_ref[...], pb, preferred_element_type=F32))
    o_ref[...] = x0 * (y + z * skip_ref[...])


def _hyena_call(u_hy, cw, cb, spec, skip, dft, l, T, nb, row0):
    fh, fl, ih, il = dft
    tb = row0 // T
    c2 = lambda b: (0, 0)
    return pl.pallas_call(
        functools.partial(_hyena_kernel, T=T),
        grid=(nb,),
        in_specs=[
            pl.BlockSpec((T, HY_COLS), lambda b: (tb + b, 0)),
            pl.BlockSpec((None, 3, HY_COLS), lambda b: (l, 0, 0)),
            pl.BlockSpec((None, 1, HY_COLS), lambda b: (l, 0, 0)),
            pl.BlockSpec((None, 2 * T, GROUP_W), lambda b: (l, 0, 0)),
            pl.BlockSpec((None, 1, GROUP_W), lambda b: (l, 0, 0)),
            pl.BlockSpec((2 * T, T), c2),
            pl.BlockSpec((2 * T, T), c2),
            pl.BlockSpec((T, 2 * T), c2),
            pl.BlockSpec((T, 2 * T), c2),
        ],
        out_specs=pl.BlockSpec((T, GROUP_W), lambda b: (b, 0)),
        out_shape=jax.ShapeDtypeStruct((nb * T, GROUP_W), F32),
        compiler_params=_cparams(("parallel",)),
        name="hyena",
    )(u_hy, cw, cb, spec, skip, fh, fl, ih, il)


def _rope(x, cosf, sinf):
    lane = lax.broadcasted_iota(jnp.int32, x.shape, 1)
    half = MLA_ROPE // 2
    partner = jnp.where(lane < MLA_NOPE + half, pltpu.roll(x, LANE - half, 1), pltpu.roll(x, half, 1))
    return x * cosf + partner * sinf


def _qk_norm(x, g):
    ms = jnp.sum(x * x, axis=-1, keepdims=True) * (1.0 / MLA_QK)
    return x * lax.rsqrt(ms + RMS_EPS) * g


def _mla_kernel(*refs, T, ctx):
    if ctx:
        (u_ref, qn_ref, wq_ref, kvn_ref, wkv_ref, qkn_ref, cos_ref, sin_ref, cckv_ref, ckr_ref, o_ref) = refs
    else:
        (u_ref, qn_ref, wq_ref, kvn_ref, wkv_ref, qkn_ref, _, _, o_ref, ckv_ref, kr_ref) = refs
    u = u_ref[...]
    cq = _rms(u[:, 0:MLA_Q_LORA], qn_ref[...])
    ckv = _rms(u[:, MLA_Q_LORA:MLA_Q_LORA + MLA_KV_LORA], kvn_ref[...])
    kr = u[:, MLA_Q_LORA + MLA_KV_LORA:MLA_Q_LORA + MLA_KV_LORA + MLA_ROPE]
    if not ctx:
        ckv_ref[...] = ckv
        kr_ref[...] = kr
    q_all = _bdot(cq, wq_ref[...])
    kv = _bdot(ckv, wkv_ref[...])
    gq = qkn_ref[0:1, :]
    gk = qkn_ref[1:2, :]
    if ctx:
        kvc = _bdot(cckv_ref[...], wkv_ref[...])
        krc = ckr_ref[...]
        cosf, sinf = cos_ref[...], sin_ref[...]
    scale = MLA_QK ** -0.5
    outs = []
    for h in range(N_HEADS):
        qh = _qk_norm(q_all[:, h * LANE:(h + 1) * LANE], gq)
        zpad = jnp.zeros((T, LANE - MLA_QK), F32)
        kh = _qk_norm(jnp.concatenate([kv[:, h * HEAD_W:(h + 1) * HEAD_W], kr, zpad], axis=-1), gk)
        vh = kv[:, GROUP_W + h * HEAD_W:GROUP_W + (h + 1) * HEAD_W]
        if ctx:
            qh = _rope(qh, cosf, sinf)
            kh = _rope(kh, cosf, sinf)
            s_len = krc.shape[0]
            zc = jnp.zeros((s_len, LANE - MLA_QK), F32)
            kc = _qk_norm(jnp.concatenate([kvc[:, h * HEAD_W:(h + 1) * HEAD_W], krc, zc], axis=-1), gk)
            kh = jnp.concatenate([kh, kc], axis=0)
            vh = jnp.concatenate([vh, kvc[:, GROUP_W + h * HEAD_W:GROUP_W + (h + 1) * HEAD_W]], axis=0)
        khb = kh.astype(BF16)
        vhb = vh.astype(BF16)
        blocks = []
        for qb in range(T // ATT_QBLOCK):
            s = _bdot_nt(qh[qb * ATT_QBLOCK:(qb + 1) * ATT_QBLOCK], khb) * scale
            e = jnp.exp(s - jnp.max(s, axis=-1, keepdims=True))
            blocks.append(_bdot(e, vhb) / jnp.sum(e, axis=-1, keepdims=True))
        outs.append(blocks[0] if len(blocks) == 1 else jnp.concatenate(blocks, axis=0))
    o_ref[...] = jnp.concatenate(outs, axis=-1)


def _mla_call(u_mla, qn, wq, kvn, wkv, qkn, rope, cache, collect, l, T, nb, row0):
    tb = row0 // T
    ctx = cache is not None
    c2 = lambda b: (0, 0)
    in_specs = [
        pl.BlockSpec((T, MLA_PAD), lambda b: (tb + b, 0)),
        pl.BlockSpec((None, 1, MLA_Q_LORA), lambda b: (l, 0, 0)),
        pl.BlockSpec((None, MLA_Q_LORA, N_HEADS * LANE), lambda b: (l, 0, 0)),
        pl.BlockSpec((None, 1, MLA_KV_LORA), lambda b: (l, 0, 0)),
        pl.BlockSpec((None, MLA_KV_LORA, 2 * GROUP_W), lambda b: (l, 0, 0)),
        pl.BlockSpec((None, 2, LANE), lambda b: (l, 0, 0)),
    ]
    args = [u_mla, qn, wq, kvn, wkv, qkn]
    out_specs = [pl.BlockSpec((T, GROUP_W), lambda b: (b, 0))]
    out_shape = [jax.ShapeDtypeStruct((nb * T, GROUP_W), F32)]
    if ctx:
        in_specs += [
            pl.BlockSpec((T, LANE), c2),
            pl.BlockSpec((T, LANE), c2),
            pl.BlockSpec((None, None, PAST_LEN, MLA_KV_LORA), lambda b: (b, l, 0, 0)),
            pl.BlockSpec((None, None, PAST_LEN, MLA_ROPE), lambda b: (b, l, 0, 0)),
        ]
        args += [rope[0], rope[1], cache[0], cache[1]]
        aliases = {}
    else:
        in_specs += [pl.BlockSpec(memory_space=pl.ANY)] * 2
        aliases = {len(args): 1, len(args) + 1: 2}
        args += list(collect)
        out_specs += [pl.BlockSpec((None, None, T, MLA_KV_LORA), lambda b: (b, l, 0, 0)),
                      pl.BlockSpec((None, None, T, MLA_ROPE), lambda b: (b, l, 0, 0))]
        out_shape += [jax.ShapeDtypeStruct(a.shape, F32) for a in collect]
    return pl.pallas_call(
        functools.partial(_mla_kernel, T=T, ctx=ctx),
        grid=(nb,),
        in_specs=in_specs,
        out_specs=out_specs,
        out_shape=out_shape,
        input_output_aliases=aliases,
        compiler_params=_cparams(("parallel",)),
        name="mla",
    )(*args)


def _pad_cols(w, width):
    return jnp.pad(w, [(0, 0)] * (w.ndim - 1) + [(0, width - w.shape[-1])])


W_IN_PREP_COLS = 256


def _prep_w_in_kernel(wt_ref, o_hg, o_hy, o_mla, o_gd):
    start = 0
    for o_ref, cols in ((o_hg, HG_COLS), (o_hy, HY_COLS), (o_mla, MLA_COLS), (o_gd, GD_COLS)):
        width = o_ref.shape[-1]
        for c0 in range(0, width, W_IN_PREP_COLS):
            n_out = min(W_IN_PREP_COLS, width - c0)
            n_real = max(0, min(n_out, cols - c0))
            piece = wt_ref[start + c0:start + c0 + n_real, :]
            if n_real < n_out:
                piece = jnp.concatenate([piece, jnp.zeros((n_out - n_real, D_MODEL), F32)], axis=0)
            o_ref[:, c0:c0 + n_out] = piece.T.astype(BF16)
        start += cols


def _prep_w_in(w_in):
    widths = (HG_COLS, HY_COLS, MLA_PAD, GD_PAD)
    n_cols = w_in.shape[-1]
    return pl.pallas_call(
        _prep_w_in_kernel,
        grid=(DEPTH,),
        in_specs=[pl.BlockSpec((None, n_cols, D_MODEL), lambda l: (l, 0, 0))],
        out_specs=[pl.BlockSpec((None, D_MODEL, w), lambda l: (l, 0, 0)) for w in widths],
        out_shape=[jax.ShapeDtypeStruct((DEPTH, D_MODEL, w), BF16) for w in widths],
        compiler_params=_cparams(("parallel",)),
        name="w_in_prep",
    )(jnp.swapaxes(w_in, 1, 2))


def _prep_wq(w_q_up):
    w = w_q_up.reshape(DEPTH, MLA_Q_LORA, N_HEADS, MLA_QK)
    return _pad_cols(w, LANE).reshape(DEPTH, MLA_Q_LORA, N_HEADS * LANE).astype(BF16)


def _prep_wkv(w_kv_up):
    w = w_kv_up.reshape(DEPTH, MLA_KV_LORA, N_HEADS, 2, HEAD_W)
    return w.transpose(0, 1, 3, 2, 4).reshape(DEPTH, MLA_KV_LORA, 2 * GROUP_W).astype(BF16)


def _lower_bounds(hgrn_lb):
    lb = jnp.cumsum(jax.nn.softmax(hgrn_lb.astype(F32), axis=0), axis=0)
    return lb - lb[0]


def kernel(x_prompt, x_sample, cache_mla_ckv, cache_mla_krope, state_hgrn, state_gdn, c, c_ctx, w_ada, b_ada, norm_ffn, w_ffn_gu, w_ffn_down, norm_mix, w_in, w_out, hgrn_lb, hgrn_norm, hy_conv_w, hy_conv_b, hy_w1, hy_b1, hy_freq, hy_w2, hy_b2, hy_w3, hy_skip, mla_q_norm_a, mla_w_q_up, mla_kv_norm_a, mla_w_kv_up, mla_qk_norm, gdn_conv_w, gdn_a_log, gdn_dt_bias, gdn_norm):
    x = (x_prompt.reshape(N_PROMPT, D_MODEL), x_sample.reshape(N_SAMPLE, D_MODEL))

    cond8 = jnp.zeros((8, D_MODEL), F32).at[0].set(c_ctx).at[1:1 + DEC_BATCH].set(c)
    ada = _ada_call(cond8, w_ada, b_ada)

    w_in_parts = _prep_w_in(w_in)
    w_out_bf = w_out.astype(BF16)
    wq = _prep_wq(mla_w_q_up)
    wkv = _prep_wkv(mla_w_kv_up)
    qkn = _pad_cols(mla_qk_norm, LANE)
    lb_all = _lower_bounds(hgrn_lb)
    alog = _pad_cols(gdn_a_log.reshape(DEPTH, 1, 8), LANE)
    dtb = _pad_cols(gdn_dt_bias.reshape(DEPTH, 1, 8), LANE)
    gdn_gn = jnp.tile(gdn_norm, (1, N_HEADS)).reshape(DEPTH, 1, GROUP_W)
    w1p = jnp.pad(hy_w1, ((0, 0), (0, LANE - HY_EMB), (0, 0)))

    bd = jnp.asarray(_block_diag_ones(), BF16)
    hg_tril, hg_m = _hgrn_consts()
    hg_consts = (jnp.asarray(hg_tril, BF16), jnp.asarray(hg_m, F32), bd)
    gd_tril, gd_masks, gd_expand = _gdn_consts()
    gd_consts = (jnp.asarray(gd_tril, BF16), jnp.asarray(gd_masks, F32), jnp.asarray(gd_expand, BF16), bd)
    rope = tuple(jnp.asarray(a) for a in _rope_consts(DEC_SEQ))
    groups = ((SEQ, BATCH, 0), (DEC_SEQ, DEC_BATCH, N_PROMPT))
    dft = {}
    spec = {}
    for T, _, _ in groups:
        fwd, inv = _dft_consts(T)
        fh, fl = _np_split2(fwd)
        ih, il = _np_split2(inv)
        dft[T] = (fh, fl, ih, il)
        zp, win = _hyena_pos_consts(T)
        spec[T] = _hyfilt_call(T, jnp.asarray(zp), jnp.asarray(win), fh, fl, w1p,
                               hy_b1.reshape(DEPTH, 1, HY_FH), hy_freq.reshape(DEPTH, 1, HY_FH), hy_w2,
                               hy_b2.reshape(DEPTH, 1, HY_FH), hy_w3)

    new_ckv = jnp.zeros((BATCH, DEPTH, SEQ, MLA_KV_LORA), F32)
    new_kr = jnp.zeros((BATCH, DEPTH, SEQ, MLA_ROPE), F32)
    new_hg = jnp.zeros((BATCH, DEPTH, 2, N_HEADS, HEAD_W, HEAD_W), F32)
    new_gd = jnp.zeros((BATCH, DEPTH, 2, N_HEADS, HEAD_W, HEAD_W), F32)
    for l in range(DEPTH):
        x = _ffn_call(x, ada, norm_ffn, w_ffn_gu, w_ffn_down, l, 0)
        u_hg, u_hy, u_mla, u_gd = _inproj_call(x, ada, norm_mix, w_in_parts, l)
        outs = []
        for gi, (T, nb, row0) in enumerate(groups):
            latent = gi == 1
            o_hg, s_hg = _hgrn_call(u_hg, lb_all, hgrn_norm, hg_consts, state_hgrn if latent else None,
                                    None if latent else new_hg, l, T, nb, row0)
            o_hy = _hyena_call(u_hy, hy_conv_w, hy_conv_b.reshape(DEPTH, 1, HY_COLS), spec[T],
                               hy_skip.reshape(DEPTH, 1, GROUP_W), dft[T], l, T, nb, row0)
            mla = _mla_call(u_mla, mla_q_norm_a.reshape(DEPTH, 1, MLA_Q_LORA), wq,
                            mla_kv_norm_a.reshape(DEPTH, 1, MLA_KV_LORA), wkv, qkn,
                            rope if latent else None,
                            (cache_mla_ckv, cache_mla_krope) if latent else None,
                            None if latent else (new_ckv, new_kr), l, T, nb, row0)
            o_gd, s_gd = _gdn_call(u_gd, gdn_conv_w, alog, dtb, gdn_gn, gd_consts, state_gdn if latent else None,
                                   None if latent else new_gd, l, T, nb, row0)
            outs.append((o_hg, o_hy, mla[0], o_gd))
            if not latent:
                new_ckv, new_kr, new_hg, new_gd = mla[1], mla[2], s_hg, s_gd
        x = _outproj_call(x, ada, w_out_bf, outs[0], outs[1], l)
        x = _ffn_call(x, ada, norm_ffn, w_ffn_gu, w_ffn_down, l, 1)

    y_prompt = x[:N_PROMPT].reshape(BATCH, SEQ, D_MODEL)
    y_sample = x[N_PROMPT:].reshape(DEC_BATCH, DEC_SEQ, D_MODEL)
    return (y_prompt, y_sample, new_ckv, new_kr, new_hg, new_gd)
```

```python
import functools
import math

import numpy as np
import jax
import jax.numpy as jnp
from jax import lax
from jax.experimental import pallas as pl
from jax.experimental.pallas import tpu as pltpu

F32 = jnp.float32
BF16 = jnp.bfloat16

D_MODEL = 1024
BATCH = 16
SEQ = 256
DEPTH = 4
DEC_BATCH = 2
DEC_SEQ = 1024
PAST_LEN = 256
GRID_W = 64
N_ADA = 9
D_FF = 2816
GROUP_W = 256
CHUNK = 64
RMS_EPS = 1e-6
N_HEADS = 4
HEAD_W = 64
HY_EMB = 33
HY_FH = 64
HY_TARGET = 1e-2
HY_FAST = 0.3
HY_SLOW = 1.5
MLA_NOPE = 64
MLA_ROPE = 32
MLA_QK = MLA_NOPE + MLA_ROPE
MLA_Q_LORA = 256
MLA_KV_LORA = 128
ROPE_BASE = 10000.0

HG_COLS = 5 * GROUP_W
HY_COLS = 3 * GROUP_W
MLA_COLS = MLA_Q_LORA + MLA_KV_LORA + MLA_ROPE
GD_COLS = 4 * GROUP_W + 16
MLA_PAD = 512
GD_PAD = 1152

N_PROMPT = BATCH * SEQ
N_SAMPLE = DEC_BATCH * DEC_SEQ
N_TOK = N_PROMPT + N_SAMPLE
LANE = 128
VMEM_LIMIT = 56 * 1024 * 1024
ROW_TILE = 1024
FF_TILE = 256
ADA_TILE = 1536
ATT_QBLOCK = 256


def _bdot(a, b):
    return jnp.dot(a.astype(BF16), b.astype(BF16), preferred_element_type=F32)


def _bdot_nt(a, b):
    return lax.dot_general(a.astype(BF16), b.astype(BF16), (((1,), (1,)), ((), ())),
                           preferred_element_type=F32)


def _bdot_tn(a, b):
    return lax.dot_general(a.astype(BF16), b.astype(BF16), (((0,), (0,)), ((), ())),
                           preferred_element_type=F32)


def _split2(x):
    hi = x.astype(BF16)
    lo = (x - hi.astype(F32)).astype(BF16)
    return hi, lo


def _split3(x):
    hi = x.astype(BF16)
    r = x - hi.astype(F32)
    mid = r.astype(BF16)
    lo = (r - mid.astype(F32)).astype(BF16)
    return hi, mid, lo


def _dot3(a, b):
    ah, al = _split2(a)
    bh, bl = _split2(b)
    return (jnp.dot(ah, bh, preferred_element_type=F32) + jnp.dot(ah, bl, preferred_element_type=F32)
            + jnp.dot(al, bh, preferred_element_type=F32))


def _sel_dot(c, x):
    h, m, l = _split3(x)
    return (jnp.dot(c, h, preferred_element_type=F32) + jnp.dot(c, m, preferred_element_type=F32)
            + jnp.dot(c, l, preferred_element_type=F32))


def _dot_sel(x, c):
    h, m, l = _split3(x)
    return (jnp.dot(h, c, preferred_element_type=F32) + jnp.dot(m, c, preferred_element_type=F32)
            + jnp.dot(l, c, preferred_element_type=F32))


def _sigmoid(x):
    return 1.0 / (1.0 + jnp.exp(-x))


def _silu(x):
    return x * _sigmoid(x)


def _rms(x, g):
    return x * lax.rsqrt(jnp.mean(x * x, axis=-1, keepdims=True) + RMS_EPS) * g


def _cparams(sem):
    return pltpu.CompilerParams(dimension_semantics=sem, vmem_limit_bytes=VMEM_LIMIT)


def _cond_of_tile(i):
    return jnp.maximum(i - (N_PROMPT // ROW_TILE - 1), 0)


def _ada_kernel(c_ref, w_ref, b_ref, o_ref):
    o_ref[...] = _dot3(_silu(c_ref[...]), w_ref[...]) + b_ref[...]


def _ada_call(cond8, w_ada, b_ada):
    n = N_ADA * D_MODEL
    out = pl.pallas_call(
        _ada_kernel,
        grid=(DEPTH, n // ADA_TILE),
        in_specs=[
            pl.BlockSpec((8, D_MODEL), lambda l, j: (0, 0)),
            pl.BlockSpec((None, D_MODEL, ADA_TILE), lambda l, j: (l, 0, j)),
            pl.BlockSpec((None, 1, ADA_TILE), lambda l, j: (l, 0, j)),
        ],
        out_specs=pl.BlockSpec((None, 8, ADA_TILE), lambda l, j: (l, 0, j)),
        out_shape=jax.ShapeDtypeStruct((DEPTH, 8, n), F32),
        compiler_params=_cparams(("parallel", "parallel")),
        name="ada",
    )(cond8, w_ada, b_ada.reshape(DEPTH, 1, n))
    return out.reshape(DEPTH, 8, N_ADA, D_MODEL)


FFN_SUBTILES = 2


def _ffn_kernel(*refs, sub, split):
    nx = 1 if split is None else 2
    x_refs = refs[:nx]
    ada_refs = refs[nx:nx + FFN_SUBTILES]
    g_ref, wg_ref, wu_ref, wd_ref, o_ref, h_scr = refs[nx + FFN_SUBTILES:]
    i = pl.program_id(0)
    f = pl.program_id(1)

    def prologue(x_ref):
        for r, ada_ref in enumerate(ada_refs):
            rows = slice(r * ROW_TILE, (r + 1) * ROW_TILE)
            y = _rms(x_ref[rows, :], g_ref[...])
            h = y * (1.0 + ada_ref[3 * sub + 1:3 * sub + 2, :]) + ada_ref[3 * sub:3 * sub + 1, :]
            h_scr[rows, :] = h.astype(BF16)
        o_ref[...] = jnp.zeros_like(o_ref)

    def epilogue(x_ref):
        for r, ada_ref in enumerate(ada_refs):
            rows = slice(r * ROW_TILE, (r + 1) * ROW_TILE)
            o_ref[rows, :] = x_ref[rows, :] + 0.5 * ada_ref[3 * sub + 2:3 * sub + 3, :] * o_ref[rows, :]

    def on(step, fn):
        if split is None:
            pl.when(f == step)(functools.partial(fn, x_refs[0]))
        else:
            pl.when((f == step) & (i < split))(functools.partial(fn, x_refs[0]))
            pl.when((f == step) & (i >= split))(functools.partial(fn, x_refs[1]))

    on(0, prologue)
    wg = wg_ref[...].astype(BF16)
    wu = wu_ref[...].astype(BF16)
    wd = wd_ref[...].astype(BF16)
    for r in range(FFN_SUBTILES):
        rows = slice(r * ROW_TILE, (r + 1) * ROW_TILE)
        h = h_scr[rows, :]
        gate = jnp.dot(h, wg, preferred_element_type=F32)
        up = jnp.dot(h, wu, preferred_element_type=F32)
        a = (_silu(gate) * up).astype(BF16)
        o_ref[rows, :] += jnp.dot(a, wd, preferred_element_type=F32)
    on(pl.num_programs(1) - 1, epilogue)


def _ffn_call(xs, ada, norm_ffn, w_gu, w_down, l, j):
    sub = 2 * j
    nf = D_FF // FF_TILE
    rows = FFN_SUBTILES * ROW_TILE

    def ada_spec(r):
        return pl.BlockSpec((None, None, N_ADA, D_MODEL),
                            lambda i, f: (l, _cond_of_tile(i * FFN_SUBTILES + r), 0, 0))

    if isinstance(xs, tuple):
        split = xs[0].shape[0] // rows
        n_tail = xs[1].shape[0] // rows
        x_specs = [pl.BlockSpec((rows, D_MODEL), lambda i, f: (jnp.minimum(i, split - 1), 0),
                                pipeline_mode=pl.Buffered(1)),
                   pl.BlockSpec((rows, D_MODEL), lambda i, f: (jnp.clip(i - split, 0, n_tail - 1), 0),
                                pipeline_mode=pl.Buffered(1))]
    else:
        split = None
        xs = (xs,)
        x_specs = [pl.BlockSpec((rows, D_MODEL), lambda i, f: (i, 0), pipeline_mode=pl.Buffered(1))]

    return pl.pallas_call(
        functools.partial(_ffn_kernel, sub=sub, split=split),
        grid=(N_TOK // rows, nf),
        in_specs=x_specs
        + [ada_spec(r) for r in range(FFN_SUBTILES)] + [
            pl.BlockSpec((None, None, 1, D_MODEL), lambda i, f: (l, j, 0, 0)),
            pl.BlockSpec((None, None, D_MODEL, FF_TILE), lambda i, f: (l, j, 0, f)),
            pl.BlockSpec((None, None, D_MODEL, FF_TILE), lambda i, f: (l, j, 0, nf + f)),
            pl.BlockSpec((None, None, FF_TILE, D_MODEL), lambda i, f: (l, j, f, 0)),
        ],
        out_specs=pl.BlockSpec((rows, D_MODEL), lambda i, f: (i, 0)),
        out_shape=jax.ShapeDtypeStruct((N_TOK, D_MODEL), F32),
        scratch_shapes=[pltpu.VMEM((rows, D_MODEL), BF16)],
        compiler_params=_cparams(("parallel", "arbitrary")),
        name="ffn",
    )(*xs, *([ada] * FFN_SUBTILES), norm_ffn.reshape(DEPTH, 2, 1, D_MODEL), w_gu, w_gu, w_down)


IN_TILE = 512


def _inproj_kernel(x_ref, ada_ref, g_ref, w1, w2, w3, w4, o1, o2, o3, o4):
    y = _rms(x_ref[...], g_ref[...])
    h = (y * (1.0 + ada_ref[4:5, :]) + ada_ref[3:4, :]).astype(BF16)
    for w, o in ((w1, o1), (w2, o2), (w3, o3), (w4, o4)):
        o[...] = jnp.dot(h, w[...], preferred_element_type=F32)


def _inproj_call(x, ada, norm_mix, ws, l):
    widths = (HG_COLS, HY_COLS, MLA_PAD, GD_PAD)
    per = ROW_TILE // IN_TILE
    return pl.pallas_call(
        _inproj_kernel,
        grid=(N_TOK // IN_TILE,),
        in_specs=[
            pl.BlockSpec((IN_TILE, D_MODEL), lambda i: (i, 0)),
            pl.BlockSpec((None, None, N_ADA, D_MODEL), lambda i: (l, _cond_of_tile(i // per), 0, 0)),
            pl.BlockSpec((None, 1, D_MODEL), lambda i: (l, 0, 0)),
        ] + [pl.BlockSpec((None, D_MODEL, w), lambda i: (l, 0, 0)) for w in widths],
        out_specs=[pl.BlockSpec((IN_TILE, w), lambda i: (i, 0)) for w in widths],
        out_shape=[jax.ShapeDtypeStruct((N_TOK, w), F32) for w in widths],
        compiler_params=_cparams(("parallel",)),
        name="inproj",
    )(x, ada, norm_mix.reshape(DEPTH, 1, D_MODEL), *ws)


OUT_TILE = 512


def _outproj_kernel(x_ref, ada_ref, w_ref, *refs):
    o_ref = refs[-1]
    i = pl.program_id(0)
    n_p = N_PROMPT // OUT_TILE

    def run(srcs):
        acc = jnp.zeros((OUT_TILE, D_MODEL), F32)
        for g, s in enumerate(srcs):
            acc += jnp.dot(s[...].astype(BF16), w_ref[g * GROUP_W:(g + 1) * GROUP_W, :],
                           preferred_element_type=F32)
        o_ref[...] = x_ref[...] + ada_ref[5:6, :] * acc

    @pl.when(i < n_p)
    def _():
        run(refs[0:4])

    @pl.when(i >= n_p)
    def _():
        run(refs[4:8])


def _outproj_call(x, ada, w_out_bf, o_p, o_s, l):
    per = ROW_TILE // OUT_TILE
    n_p = N_PROMPT // OUT_TILE
    n_s = N_SAMPLE // OUT_TILE
    return pl.pallas_call(
        _outproj_kernel,
        grid=(N_TOK // OUT_TILE,),
        in_specs=[
            pl.BlockSpec((OUT_TILE, D_MODEL), lambda i: (i, 0)),
            pl.BlockSpec((None, None, N_ADA, D_MODEL), lambda i: (l, _cond_of_tile(i // per), 0, 0)),
            pl.BlockSpec((None, D_MODEL, D_MODEL), lambda i: (l, 0, 0)),
        ] + [pl.BlockSpec((OUT_TILE, GROUP_W), lambda i: (jnp.minimum(i, n_p - 1), 0))] * 4
          + [pl.BlockSpec((OUT_TILE, GROUP_W), lambda i: (jnp.clip(i - n_p, 0, n_s - 1), 0))] * 4,
        out_specs=pl.BlockSpec((OUT_TILE, D_MODEL), lambda i: (i, 0)),
        out_shape=jax.ShapeDtypeStruct((N_TOK, D_MODEL), F32),
        compiler_params=_cparams(("parallel",)),
        name="outproj",
    )(x, ada, w_out_bf, *o_p, *o_s)


def _block_diag_ones():
    idx = np.arange(GROUP_W) // HEAD_W
    return (idx[:, None] == idx[None, :]).astype(np.float32)


def _hgrn_consts():
    C = CHUNK
    i = np.arange(C)[:, None]
    j = np.arange(C)[None, :]
    masks = []
    s = C // 2
    while s >= 1:
        up_i = (i // s) % 2 == 1
        up_j = (j // s) % 2 == 1
        masks.append(up_i & (~up_j) & (i // (2 * s) == j // (2 * s)))
        s //= 2
    masks.append(j <= i)
    fwd_m = np.stack([m.astype(np.float32) for m in masks])
    bwd_m = np.stack([m.astype(np.float32)[::-1, ::-1] for m in masks])
    tril = np.stack([(j <= i), (j >= i)]).astype(np.float32)
    return tril, np.tile(np.stack([fwd_m, bwd_m]), (1, 1, 1, 2))


def _gdn_consts():
    C = CHUNK
    i = np.arange(C)[:, None]
    t = np.arange(C)[None, :]
    tril = np.stack([(t <= i), (t >= i)]).astype(np.float32)
    masks = np.stack([np.stack([(t <= i), (t < i)]), np.stack([(t >= i), (t > i)])]).astype(np.float32)
    expand = np.zeros((2, LANE, 2 * GROUP_W), np.float32)
    for d in range(2):
        for h in range(N_HEADS):
            expand[d, d * N_HEADS + h, h * HEAD_W:(h + 1) * HEAD_W] = 1.0
            expand[d, 8 + d * N_HEADS + h, GROUP_W + h * HEAD_W:GROUP_W + (h + 1) * HEAD_W] = 1.0
    return tril, masks, expand


def _dft_consts(T):
    n2 = 4 * T
    k = np.arange(T, dtype=np.int64)[:, None]
    s = np.arange(T, dtype=np.int64)[None, :]
    ang = np.pi * (((2 * k + 1) * s) % n2).astype(np.float64) / (2 * T)
    fwd = np.concatenate([np.cos(ang), -np.sin(ang)], axis=0)
    inv = fwd.T / T
    return fwd.astype(np.float32), inv.astype(np.float32)


def _np_split2(x):
    hi = jnp.asarray(x, F32).astype(BF16)
    lo = (jnp.asarray(x, F32) - hi.astype(F32)).astype(BF16)
    return hi, lo


def _hyena_pos_consts(T):
    pos = np.arange(T, dtype=np.float32)
    t = pos / np.float32(T - 1)
    bands = np.linspace(1e-4, (HY_EMB - 1) // 2 - 1, (HY_EMB - 1) // 2, dtype=np.float32)
    ang = (np.float32(2.0 * math.pi / T) * pos[:, None]) * bands[None, :]
    z = np.concatenate([t[:, None], np.cos(ang), -np.sin(ang)], axis=-1).astype(np.float32)
    zp = np.zeros((T, LANE), np.float32)
    zp[:, :HY_EMB] = z
    max_decay = math.log(HY_TARGET) / HY_FAST
    min_decay = math.log(HY_TARGET) / HY_SLOW
    deltas = np.linspace(min_decay, max_decay, GROUP_W, dtype=np.float32)
    window = np.exp(-t[:, None] * np.abs(deltas)[None, :]).astype(np.float32)
    return zp, window


def _rope_consts(T):
    rows = T // GRID_W
    row = np.repeat(np.arange(rows, dtype=np.float32), GRID_W)
    col = (np.arange(T) % GRID_W).astype(np.float32)
    pairs = MLA_ROPE // 4
    inv = (np.float32(ROPE_BASE) ** (-np.arange(pairs, dtype=np.float32) / np.float32(pairs))).astype(np.float32)
    ang = np.concatenate([row[:, None] * inv, col[:, None] * inv], axis=-1).astype(np.float32)
    cos, sin = np.cos(ang), np.sin(ang)
    cosf = np.ones((T, LANE), np.float32)
    sinf = np.zeros((T, LANE), np.float32)
    half = MLA_ROPE // 2
    cosf[:, MLA_NOPE:MLA_NOPE + half] = cos
    cosf[:, MLA_NOPE + half:MLA_QK] = cos
    sinf[:, MLA_NOPE:MLA_NOPE + half] = -sin
    sinf[:, MLA_NOPE + half:MLA_QK] = sin
    return cosf, sinf


def _head_norm_gate(tot, bd, gn, gate):
    ms = _sel_dot_right(tot * tot, bd) * (1.0 / HEAD_W)
    return tot * lax.rsqrt(ms + RMS_EPS) * gn * _silu(gate)


def _sel_dot_right(x, c):
    h, l = _split2(x)
    return jnp.dot(h, c, preferred_element_type=F32) + jnp.dot(l, c, preferred_element_type=F32)


def _block_ref(b, two_s, r):
    C, W = b.shape
    if two_s % 8 == 0:
        b3 = b.reshape(C // two_s, two_s, W)
        return jnp.broadcast_to(b3[:, r:r + 1, :], b3.shape).reshape(C, W)
    pos = lax.broadcasted_iota(jnp.int32, b.shape, 0) % two_s
    out = b
    for p in range(two_s):
        if p != r:
            out = jnp.where(pos == p, pltpu.roll(b, (p - r) % C, 0), out)
    return out


N_PAIRS = N_HEADS // 2
HG_GROUP = 2
HG_DIRECT_MAX = 80.0


def _pair_blockdiag(x):
    lane = lax.broadcasted_iota(jnp.int32, x.shape, 1)
    zero = jnp.zeros_like(x)
    return jnp.concatenate([jnp.where(lane < HEAD_W, x, zero), jnp.where(lane >= HEAD_W, x, zero)], axis=0)


def _hgrn_kernel(*refs, T, has_s0):
    if has_s0:
        (u_ref, lb_ref, gn_ref, tril_ref, lmask_ref, bd_ref, s0_ref,
         o_ref, sfin_ref, oi_s, qin_s, up_s, dc_s, st_s) = refs
    else:
        (u_ref, lb_ref, gn_ref, tril_ref, lmask_ref, bd_ref, _,
         o_ref, sfin_ref, oi_s, qin_s, up_s, dc_s, st_s) = refs
    n = T // CHUNK
    C = CHUNK
    bd = bd_ref[...]
    n_lv = int(math.log2(C))

    log_lb = [jnp.log(lb_ref[d]) for d in range(2)]
    log_1mlb = [jnp.log(1.0 - lb_ref[d]) for d in range(2)]

    def gates(rows, d):
        z = u_ref[rows, (3 + d) * GROUP_W:(4 + d) * GROUP_W]
        t = jnp.exp(-jnp.abs(z))
        log_sig = jnp.minimum(z, 0.0) - jnp.log(1.0 + t)
        c = log_1mlb[d] + log_sig
        m = jnp.maximum(log_lb[d], c)
        lf = m + jnp.log(1.0 + jnp.exp(jnp.minimum(log_lb[d], c) - m))
        sig_neg = jnp.where(z > 0.0, t, 1.0) / (1.0 + t)
        return lf, (1.0 - lb_ref[d]) * sig_neg

    for d in range(2):
        if has_s0:
            st_s[d] = jnp.concatenate([s0_ref[d, h].T for h in range(N_HEADS)], axis=-1)
        else:
            st_s[d] = jnp.zeros((HEAD_W, GROUP_W), F32)

    def prepare(it, carry):
        units = [(c, d) for c in range(HG_GROUP) for d in range(2)]
        rows = [pl.ds(pl.multiple_of((it * HG_GROUP + c) * C, C), C) for c in range(HG_GROUP)]
        arow = [pl.ds(pl.multiple_of((it * HG_GROUP + c) * 8, 8), 8) for c in range(HG_GROUP)]
        q = [u_ref[rows[c], 0:GROUP_W] * (HEAD_W ** -0.5) for c in range(HG_GROUP)]
        v = [u_ref[rows[c], GROUP_W:2 * GROUP_W] for c in range(HG_GROUP)]
        vt = [[jnp.concatenate([v[c][:, h * HEAD_W:(h + 1) * HEAD_W].T for h in (2 * p, 2 * p + 1)],
                               axis=-1).astype(BF16) for p in range(N_PAIRS)]
              for c in range(HG_GROUP)]
        v_bd = [[_pair_blockdiag(v[c][:, p * LANE:(p + 1) * LANE].astype(BF16)) for p in range(N_PAIRS)]
                for c in range(HG_GROUP)]
        lf, ks = zip(*[gates(rows[c], d) for c, d in units])
        bs = []
        for i, (c, d) in enumerate(units):
            hi, lo = _split2(lf[i])
            tril = tril_ref[d]
            bs.append(jnp.dot(tril, hi, preferred_element_type=F32) + jnp.dot(tril, lo, preferred_element_type=F32))
        tot = [jnp.sum(x, axis=0, keepdims=True) for x in lf]
        mid_row = [C // 2 - 1 if d == 0 else C // 2 for c, d in units]
        spread = [jnp.maximum(jnp.abs(bs[i][0:1] - bs[i][r:r + 1]), jnp.abs(bs[i][C - 1:C] - bs[i][r:r + 1]))
                  for i, r in enumerate(mid_row)]
        widest = functools.reduce(jnp.maximum, spread)
        ko = [(ks[i] * jnp.exp(tot[i] - bs[i])).astype(BF16) for i in range(len(units))]
        up = [[jnp.dot(vt[c][p], _pair_blockdiag(ko[i][:, p * LANE:(p + 1) * LANE]), preferred_element_type=F32)
               for p in range(N_PAIRS)] for i, (c, d) in enumerate(units)]
        for i, (c, d) in enumerate(units):
            qin_s[d, rows[c], :] = q[c] * jnp.exp(bs[i])
            up_s[d, rows[c], :] = jnp.concatenate(up[i], axis=-1)
            dc_s[d, arow[c], :] = jnp.broadcast_to(jnp.exp(tot[i]), (8, GROUP_W))

        def masked_scores(qe, ke, lv):
            out = []
            for i, (c, d) in enumerate(units):
                per_pair = []
                for p in range(N_PAIRS):
                    sl = slice(p * LANE, (p + 1) * LANE)
                    prod = lax.dot_general(qe[i][:, sl], _pair_blockdiag(ke[i][:, sl]), (((1,), (1,)), ((), ())),
                                           preferred_element_type=F32)
                    per_pair.append(jnp.where(lmask_ref[d, lv] > 0.5, prod, 0.0))
                out.append(per_pair)
            return out

        def finish(sc):
            return [jnp.concatenate([jnp.dot(sc[i][p].astype(BF16), v_bd[c][p], preferred_element_type=F32)
                                     for p in range(N_PAIRS)], axis=-1) for i, (c, d) in enumerate(units)]

        def intra_direct():
            mid = [_block_ref(bs[i], C, C // 2 - 1 if d == 0 else C // 2) for i, (c, d) in enumerate(units)]
            qe = [(q[c] * jnp.exp(bs[i] - mid[i])).astype(BF16) for i, (c, d) in enumerate(units)]
            ke = [(ks[i] * jnp.exp(mid[i] - bs[i])).astype(BF16) for i in range(len(units))]
            return tuple(finish(masked_scores(qe, ke, n_lv)))

        def intra_split():
            sc = [[jnp.zeros((C, LANE), F32) for _ in range(N_PAIRS)] for _ in units]
            s = C // 2
            lv = 0
            while s >= 1:
                e = [jnp.exp(-jnp.abs(bs[i] - _block_ref(bs[i], 2 * s, s - 1 if d == 0 else s)))
                     for i, (c, d) in enumerate(units)]
                part = masked_scores([(q[c] * e[i]).astype(BF16) for i, (c, d) in enumerate(units)],
                                     [(ks[i] * e[i]).astype(BF16) for i in range(len(units))], lv)
                sc = [[sc[i][p] + part[i][p] for p in range(N_PAIRS)] for i in range(len(units))]
                s //= 2
                lv += 1
            fin = finish(sc)
            return tuple(fin[i] + _bdot(q[c] * ks[i], bd) * v[c] for i, (c, d) in enumerate(units))

        oi = lax.cond(jnp.max(widest) < HG_DIRECT_MAX, intra_direct, intra_split)
        for i, (c, d) in enumerate(units):
            oi_s[d, rows[c], :] = oi[i]
        return carry

    lax.fori_loop(0, n // HG_GROUP, prepare, 0)

    def chunk(ci, carry):
        rows = [pl.ds(pl.multiple_of(cidx * C, C), C) for cidx in (ci, n - 1 - ci)]
        decay = [dc_s[d, pl.ds(pl.multiple_of(cidx * 8, 8), 1), :] for d, cidx in ((0, ci), (1, n - 1 - ci))]
        st = [st_s[d] for d in range(2)]
        o_inter = [[lax.dot_general(qin_s[d, rows[d], p * LANE:(p + 1) * LANE].astype(BF16),
                                    _pair_blockdiag(st[d][:, p * LANE:(p + 1) * LANE].astype(BF16)),
                                    (((1,), (1,)), ((), ())), preferred_element_type=F32)
                    for p in range(N_PAIRS)] for d in range(2)]
        for d in range(2):
            st_s[d] = st[d] * decay[d] + up_s[d, rows[d], :]
            oi_s[d, rows[d], :] = oi_s[d, rows[d], :] + jnp.concatenate(o_inter[d], axis=-1)
        return carry

    lax.fori_loop(0, n, chunk, 0, unroll=2)
    o_ref[...] = _head_norm_gate(oi_s[0] + oi_s[1], bd, gn_ref[...], u_ref[:, 2 * GROUP_W:3 * GROUP_W])
    for d in range(2):
        for h in range(N_HEADS):
            sfin_ref[d, h] = st_s[d][:, h * HEAD_W:(h + 1) * HEAD_W].T


def _state_io(in_specs, args, s0, collect, l, nb):
    state_block = (None, None, 2, N_HEADS, HEAD_W, HEAD_W)
    if s0 is not None:
        in_specs.append(pl.BlockSpec(state_block, lambda b: (b, l, 0, 0, 0, 0)))
        args.append(s0)
        return (pl.BlockSpec(state_block[1:], lambda b: (b, 0, 0, 0, 0)),
                jax.ShapeDtypeStruct((nb, 2, N_HEADS, HEAD_W, HEAD_W), F32), {})
    in_specs.append(pl.BlockSpec(memory_space=pl.ANY))
    args.append(collect)
    return (pl.BlockSpec(state_block, lambda b: (b, l, 0, 0, 0, 0)),
            jax.ShapeDtypeStruct(collect.shape, F32), {len(args) - 1: 1})


def _hgrn_call(u_hg, lb_l, gn, consts, s0, collect, l, T, nb, row0):
    tril, lmask, bd = consts
    tb = row0 // T
    has_s0 = s0 is not None
    in_specs = [
        pl.BlockSpec((T, HG_COLS), lambda b: (tb + b, 0)),
        pl.BlockSpec((None, 2, 1, GROUP_W), lambda b: (l, 0, 0, 0)),
        pl.BlockSpec((None, 1, GROUP_W), lambda b: (l, 0, 0)),
        pl.BlockSpec((2, CHUNK, CHUNK), lambda b: (0, 0, 0)),
        pl.BlockSpec((2, 7, CHUNK, LANE), lambda b: (0, 0, 0, 0)),
        pl.BlockSpec((GROUP_W, GROUP_W), lambda b: (0, 0)),
    ]
    args = [u_hg, lb_l.reshape(DEPTH, 2, 1, GROUP_W), gn.reshape(DEPTH, 1, GROUP_W), tril, lmask, bd]
    state_spec, state_shape, aliases = _state_io(in_specs, args, s0, collect, l, nb)
    seq = pltpu.VMEM((2, T, GROUP_W), F32)
    return pl.pallas_call(
        functools.partial(_hgrn_kernel, T=T, has_s0=has_s0),
        grid=(nb,),
        in_specs=in_specs,
        out_specs=[pl.BlockSpec((T, GROUP_W), lambda b: (b, 0)), state_spec],
        out_shape=[jax.ShapeDtypeStruct((nb * T, GROUP_W), F32), state_shape],
        input_output_aliases=aliases,
        scratch_shapes=[seq, seq, seq,
                        pltpu.VMEM((2, T // CHUNK * 8, GROUP_W), F32), pltpu.VMEM((2, HEAD_W, GROUP_W), F32)],
        compiler_params=_cparams(("parallel",)),
        name="hgrn",
    )(*args)


def _shift_rows(x, T):
    row = lax.broadcasted_iota(jnp.int32, x.shape, 0)
    prev = jnp.where(row == 0, 0.0, pltpu.roll(x, 1, 0))
    nxt = jnp.where(row == T - 1, 0.0, pltpu.roll(x, T - 1, 0))
    return prev, nxt


def _conv3(x, w_ref, T):
    prev, nxt = _shift_rows(x, T)
    return prev * w_ref[0:1, :] + x * w_ref[1:2, :] + nxt * w_ref[2:3, :]


GDN_UNROLL = 2


def _solve_unit_lower(systems):
    c2 = 2 * CHUNK
    slabs = [jnp.concatenate([nmat, nmat, rhs], axis=-1) for rhs, nmat in systems]
    steps = int(math.log2(CHUNK))
    for step in range(steps):
        last = step == steps - 1
        nxt = []
        for slab in slabs:
            hi = slab.astype(BF16)
            lo = (slab - hi.astype(F32)).astype(BF16)
            lhs = jnp.concatenate([hi[:, :c2], lo[:, :CHUNK]], axis=-1)
            first = c2 if last else 0
            rhs3 = jnp.concatenate([hi[:, first:], lo[:, first:], hi[:, first:]], axis=0)
            prod = jnp.dot(lhs, rhs3, preferred_element_type=F32)
            if last:
                nxt.append(slab[:, c2:] + prod)
            else:
                nxt.append(jnp.concatenate([prod[:, :c2], slab[:, c2:] + prod[:, c2:]], axis=-1))
        slabs = nxt
    return slabs


def _gdn_kernel(*refs, T, has_s0):
    if has_s0:
        (u_ref, cw_ref, alog_ref, dtb_ref, exp_ref, tril_ref, mask_ref, bd_ref, gn_ref, s0_ref,
         o_ref, sfin_ref, q_s, k_s, v_s, la_s, be_s, uw_s, ww_s, at_s, qin_s, kt_s, al_s, of_s, st_s) = refs
    else:
        (u_ref, cw_ref, alog_ref, dtb_ref, exp_ref, tril_ref, mask_ref, bd_ref, gn_ref, _,
         o_ref, sfin_ref, q_s, k_s, v_s, la_s, be_s, uw_s, ww_s, at_s, qin_s, kt_s, al_s, of_s, st_s) = refs
    n = T // CHUNK
    C = CHUNK
    bd = bd_ref[...]

    qkv = _silu(_conv3(u_ref[:, 0:3 * GROUP_W], cw_ref, T))
    q = qkv[:, 0:GROUP_W]
    k = qkv[:, GROUP_W:2 * GROUP_W]
    q_s[...] = q * lax.rsqrt(_sel_dot_right(q * q, bd) + 1e-6) * (HEAD_W ** -0.5)
    k_s[...] = k * lax.rsqrt(_sel_dot_right(k * k, bd) + 1e-6)
    v_s[...] = qkv[:, 2 * GROUP_W:3 * GROUP_W]

    ab = u_ref[:, 4 * GROUP_W:4 * GROUP_W + LANE]
    xa = ab + dtb_ref[...]
    softplus = jnp.maximum(xa, 0.0) + jnp.log(1.0 + jnp.exp(-jnp.abs(xa)))
    log_a = -jnp.exp(alog_ref[...]) * softplus
    lane = lax.broadcasted_iota(jnp.int32, ab.shape, 1)
    narrow = jnp.where(lane < 8, log_a, _sigmoid(ab))
    for d in range(2):
        wide = _dot_sel(narrow, exp_ref[d])
        la_s[d] = wide[:, 0:GROUP_W]
        be_s[d] = wide[:, GROUP_W:2 * GROUP_W]
        if has_s0:
            st_s[d] = jnp.concatenate([s0_ref[d, h] for h in range(N_HEADS)], axis=-1)
        else:
            st_s[d] = jnp.zeros((HEAD_W, GROUP_W), F32)

    def prepare(cidx):
        r0 = pl.multiple_of(cidx * C, C)
        rows = pl.ds(r0, C)
        arow = pl.ds(pl.multiple_of(cidx * 8, 8), 8)
        q = q_s[rows, :]
        k = k_s[rows, :]
        v = v_s[rows, :]
        systems = []
        attns = []
        kts = []
        for d in range(2):
            incl = mask_ref[d, 0] > 0.5
            strict = mask_ref[d, 1]
            la = la_s[d, rows, :]
            be = be_s[d, rows, :]
            gx = _sel_dot(tril_ref[d], la)
            gtot = jnp.sum(la, axis=0, keepdims=True)
            eg = jnp.exp(gx)
            kout = k * jnp.exp(gtot - gx)
            qin_s[d, rows, :] = q * eg
            al_s[d, arow, :] = jnp.broadcast_to(jnp.exp(gtot), (8, GROUP_W))
            kb = k * be
            vb = v * be
            kbg = kb * eg
            for h in range(N_HEADS):
                sl = slice(h * HEAD_W, (h + 1) * HEAD_W)
                gh = gx[:, sl]
                dmat = gh - gh.T
                dec = jnp.where(incl, jnp.exp(jnp.where(incl, dmat, 0.0)), 0.0)
                qk = _bdot_nt(jnp.concatenate([kb[:, sl], q[:, sl]], axis=0), k[:, sl])
                nmat = -(qk[:C] * dec * strict)
                systems.append((jnp.concatenate([vb[:, sl], kbg[:, sl]], axis=-1), nmat))
                attns.append(qk[C:] * dec)
                kts.append(kout[:, sl].T)
        sols = _solve_unit_lower(systems)
        for d in range(2):
            mine = sols[d * N_HEADS:(d + 1) * N_HEADS]
            uw_s[d, rows, :] = jnp.concatenate([x[:, :HEAD_W] for x in mine], axis=-1)
            ww_s[d, rows, :] = jnp.concatenate([x[:, HEAD_W:] for x in mine], axis=-1)
            at_s[d, rows, :] = jnp.concatenate(attns[d * N_HEADS:(d + 1) * N_HEADS], axis=-1)
            kt_s[d, rows, :] = jnp.concatenate(kts[d * N_HEADS:(d + 1) * N_HEADS], axis=-1)

    def prep_body(i, carry):
        for j in range(GDN_UNROLL):
            prepare(i * GDN_UNROLL + j)
        return carry

    lax.fori_loop(0, n // GDN_UNROLL, prep_body, 0)

    def chunk(ci, carry):
        units = [(d, p) for d in range(2) for p in range(N_PAIRS)]
        rows = []
        alast = []
        for d, cidx in ((0, ci), (1, n - 1 - ci)):
            rows.append(pl.ds(pl.multiple_of(cidx * C, C), C))
            alast.append(al_s[d, pl.ds(pl.multiple_of(cidx * 8, 8), 1), :])
        st = [st_s[d] for d in range(2)]
        lanes = [slice(p * LANE, (p + 1) * LANE) for p in range(N_PAIRS)]
        both = [jnp.dot(jnp.concatenate([ww_s[d, rows[d], lanes[p]], qin_s[d, rows[d], lanes[p]]], axis=0).astype(BF16),
                        _pair_blockdiag(st[d][:, lanes[p]].astype(BF16)), preferred_element_type=F32)
                for d, p in units]
        vnew = [uw_s[d, rows[d], lanes[p]] - both[i][:C] for i, (d, p) in enumerate(units)]
        upd = [jnp.dot(jnp.concatenate([at_s[d, rows[d], lanes[p]], kt_s[d, rows[d], lanes[p]]], axis=0).astype(BF16),
                       _pair_blockdiag(vnew[i].astype(BF16)), preferred_element_type=F32)
               for i, (d, p) in enumerate(units)]
        for d in range(2):
            idx = range(d * N_PAIRS, (d + 1) * N_PAIRS)
            of_s[d, rows[d], :] = jnp.concatenate([both[i][C:] + upd[i][:C] for i in idx], axis=-1)
            st_s[d] = st[d] * alast[d] + jnp.concatenate([upd[i][C:] for i in idx], axis=-1)
        return carry

    lax.fori_loop(0, n, chunk, 0)
    o_ref[...] = _head_norm_gate(of_s[0] + of_s[1], bd, gn_ref[...], u_ref[:, 3 * GROUP_W:4 * GROUP_W])
    for d in range(2):
        for h in range(N_HEADS):
            sfin_ref[d, h] = st_s[d][:, h * HEAD_W:(h + 1) * HEAD_W]


def _gdn_call(u_gd, cw, alog, dtb, gn, consts, s0, collect, l, T, nb, row0):
    tril, masks, expand, bd = consts
    tb = row0 // T
    has_s0 = s0 is not None
    in_specs = [
        pl.BlockSpec((T, GD_PAD), lambda b: (tb + b, 0)),
        pl.BlockSpec((None, 3, 3 * GROUP_W), lambda b: (l, 0, 0)),
        pl.BlockSpec((None, 1, LANE), lambda b: (l, 0, 0)),
        pl.BlockSpec((None, 1, LANE), lambda b: (l, 0, 0)),
        pl.BlockSpec((2, LANE, 2 * GROUP_W), lambda b: (0, 0, 0)),
        pl.BlockSpec((2, CHUNK, CHUNK), lambda b: (0, 0, 0)),
        pl.BlockSpec((2, 2, CHUNK, CHUNK), lambda b: (0, 0, 0, 0)),
        pl.BlockSpec((GROUP_W, GROUP_W), lambda b: (0, 0)),
        pl.BlockSpec((None, 1, GROUP_W), lambda b: (l, 0, 0)),
    ]
    args = [u_gd, cw, alog, dtb, expand, tril, masks, bd, gn]
    state_spec, state_shape, aliases = _state_io(in_specs, args, s0, collect, l, nb)
    seq = pltpu.VMEM((2, T, GROUP_W), F32)
    return pl.pallas_call(
        functools.partial(_gdn_kernel, T=T, has_s0=has_s0),
        grid=(nb,),
        in_specs=in_specs,
        out_specs=[pl.BlockSpec((T, GROUP_W), lambda b: (b, 0)), state_spec],
        out_shape=[jax.ShapeDtypeStruct((nb * T, GROUP_W), F32), state_shape],
        input_output_aliases=aliases,
        scratch_shapes=[pltpu.VMEM((T, GROUP_W), F32)] * 3 + [seq, seq, seq, seq, seq, seq, seq,
            pltpu.VMEM((2, T // CHUNK * 8, GROUP_W), F32), seq, pltpu.VMEM((2, HEAD_W, GROUP_W), F32)],
        compiler_params=_cparams(("parallel",)),
        name="gdn",
    )(*args)


def _hyfilt_kernel(z_ref, win_ref, fh_ref, fl_ref, w1_ref, b1_ref, fr_ref, w2_ref, b2_ref, w3_ref, o_ref, *, T):
    fr = fr_ref[...]
    h = jnp.sin(fr * (_dot3(z_ref[...], w1_ref[...]) + b1_ref[...]))
    h = jnp.sin(fr * (_dot3(h, w2_ref[...]) + b2_ref[...]))
    h = _dot3(h, w3_ref[...])
    win = win_ref[...]
    hf = h[:, 0:GROUP_W] * win
    hb = h[:, GROUP_W:2 * GROUP_W] * win
    row = lax.broadcasted_iota(jnp.int32, hb.shape, 0)
    hb = jnp.where(row == 0, 0.0, hb)
    taps = jnp.concatenate([hf + hb, hf - hb], axis=-1).astype(BF16)
    spec = (jnp.dot(fh_ref[...], taps, preferred_element_type=F32)
            + jnp.dot(fl_ref[...], taps, preferred_element_type=F32))
    o_ref[0:T, :] = spec[0:T, 0:GROUP_W]
    o_ref[T:2 * T, :] = spec[T:2 * T, GROUP_W:2 * GROUP_W]


def _hyfilt_call(T, zp, win, fh, fl, w1p, b1, freq, w2, b2, w3):
    c2 = lambda l: (0, 0)
    return pl.pallas_call(
        functools.partial(_hyfilt_kernel, T=T),
        grid=(DEPTH,),
        in_specs=[
            pl.BlockSpec((T, LANE), c2),
            pl.BlockSpec((T, GROUP_W), c2),
            pl.BlockSpec((2 * T, T), c2),
            pl.BlockSpec((2 * T, T), c2),
            pl.BlockSpec((None, LANE, HY_FH), lambda l: (l, 0, 0)),
            pl.BlockSpec((None, 1, HY_FH), lambda l: (l, 0, 0)),
            pl.BlockSpec((None, 1, HY_FH), lambda l: (l, 0, 0)),
            pl.BlockSpec((None, HY_FH, HY_FH), lambda l: (l, 0, 0)),
            pl.BlockSpec((None, 1, HY_FH), lambda l: (l, 0, 0)),
            pl.BlockSpec((None, HY_FH, 2 * GROUP_W), lambda l: (l, 0, 0)),
        ],
        out_specs=pl.BlockSpec((None, 2 * T, GROUP_W), lambda l: (l, 0, 0)),
        out_shape=jax.ShapeDtypeStruct((DEPTH, 2 * T, GROUP_W), F32),
        compiler_params=_cparams(("parallel",)),
        name="hyfilt",
    )(zp, win, fh, fl, w1p, b1, freq, w2, b2, w3)


def _hyena_kernel(u_ref, cw_ref, cb_ref, spec_ref, skip_ref, fh_ref, fl_ref, ih_ref, il_ref, o_ref, *, T):
    uc = _conv3(u_ref[...], cw_ref, T) + cb_ref[...]
    x0 = uc[:, 0:GROUP_W]
    z = uc[:, GROUP_W:2 * GROUP_W] * uc[:, 2 * GROUP_W:3 * GROUP_W]
    zb = z.astype(BF16)
    zs = (jnp.dot(fh_ref[...], zb, preferred_element_type=F32)
          + jnp.dot(fl_ref[...], zb, preferred_element_type=F32))
    ar, ai = zs[0:T], zs[T:2 * T]
    br, bi = spec_ref[0:T, :], spec_ref[T:2 * T, :]
    pb = jnp.concatenate([ar * br - ai * bi, ar * bi + ai * br], axis=0).astype(BF16)
    y = (jnp.dot(ih_ref[...], pb, preferred_element_type=F32)
         + jnp.dot(il_ref[...], pb, preferred_element_type=F32))
    o_ref[...] = x0 * (y + z * skip_ref[...])


def _hyena_call(u_hy, cw, cb, spec, skip, dft, l, T, nb, row0):
    fh, fl, ih, il = dft
    tb = row0 // T
    c2 = lambda b: (0, 0)
    return pl.pallas_call(
        functools.partial(_hyena_kernel, T=T),
        grid=(nb,),
        in_specs=[
            pl.BlockSpec((T, HY_COLS), lambda b: (tb + b, 0)),
            pl.BlockSpec((None, 3, HY_COLS), lambda b: (l, 0, 0)),
            pl.BlockSpec((None, 1, HY_COLS), lambda b: (l, 0, 0)),
            pl.BlockSpec((None, 2 * T, GROUP_W), lambda b: (l, 0, 0)),
            pl.BlockSpec((None, 1, GROUP_W), lambda b: (l, 0, 0)),
            pl.BlockSpec((2 * T, T), c2),
            pl.BlockSpec((2 * T, T), c2),
            pl.BlockSpec((T, 2 * T), c2),
            pl.BlockSpec((T, 2 * T), c2),
        ],
        out_specs=pl.BlockSpec((T, GROUP_W), lambda b: (b, 0)),
        out_shape=jax.ShapeDtypeStruct((nb * T, GROUP_W), F32),
        compiler_params=_cparams(("parallel",)),
        name="hyena",
    )(u_hy, cw, cb, spec, skip, fh, fl, ih, il)


def _rope(x, cosf, sinf):
    lane = lax.broadcasted_iota(jnp.int32, x.shape, 1)
    half = MLA_ROPE // 2
    partner = jnp.where(lane < MLA_NOPE + half, pltpu.roll(x, LANE - half, 1), pltpu.roll(x, half, 1))
    return x * cosf + partner * sinf


def _qk_norm(x, g):
    ms = jnp.sum(x * x, axis=-1, keepdims=True) * (1.0 / MLA_QK)
    return x * lax.rsqrt(ms + RMS_EPS) * g


def _mla_kernel(*refs, T, ctx):
    if ctx:
        (u_ref, qn_ref, wq_ref, kvn_ref, wkv_ref, qkn_ref, cos_ref, sin_ref, cckv_ref, ckr_ref, o_ref,
         q_s, k_s, v_s) = refs
    else:
        (u_ref, qn_ref, wq_ref, kvn_ref, wkv_ref, qkn_ref, _, _, o_ref, ckv_ref, kr_ref, q_s, k_s, v_s) = refs
    n_keys = k_s.shape[1]
    u = u_ref[...]
    cq = _rms(u[:, 0:MLA_Q_LORA], qn_ref[...])
    ckv = _rms(u[:, MLA_Q_LORA:MLA_Q_LORA + MLA_KV_LORA], kvn_ref[...])
    kr = u[:, MLA_Q_LORA + MLA_KV_LORA:MLA_Q_LORA + MLA_KV_LORA + MLA_ROPE]
    if not ctx:
        ckv_ref[...] = ckv
        kr_ref[...] = kr
    q_all = _bdot(cq, wq_ref[...])
    kv = _bdot(ckv, wkv_ref[...])
    gq = qkn_ref[0:1, :]
    gk = qkn_ref[1:2, :]
    if ctx:
        kvc = _bdot(cckv_ref[...], wkv_ref[...])
        krc = ckr_ref[...]
        cosf, sinf = cos_ref[...], sin_ref[...]
    q_scale = MLA_QK ** -0.5 * math.log2(math.e)
    kr_tile = jnp.concatenate([jnp.zeros((T, MLA_NOPE), F32), kr, jnp.zeros((T, LANE - MLA_QK), F32)], axis=-1)
    kr_rot = kr_tile * gk
    if ctx:
        kr_rot = _rope(kr_rot, cosf, sinf)
    nope_lane = lax.broadcasted_iota(jnp.int32, (T, LANE), 1) < MLA_NOPE
    for h in range(N_HEADS):
        qh = _qk_norm(q_all[:, h * LANE:(h + 1) * LANE], gq)
        k_nope = jnp.concatenate([kv[:, h * HEAD_W:(h + 1) * HEAD_W], jnp.zeros((T, LANE - MLA_NOPE), F32)], axis=-1)
        ms = jnp.sum(k_nope * k_nope + kr_tile * kr_tile, axis=-1, keepdims=True) * (1.0 / MLA_QK)
        kh = jnp.where(nope_lane, k_nope * gk, kr_rot) * lax.rsqrt(ms + RMS_EPS)
        if ctx:
            qh = _rope(qh, cosf, sinf)
            zc = jnp.zeros((n_keys - T, LANE - MLA_QK), F32)
            kc = _qk_norm(jnp.concatenate([kvc[:, h * HEAD_W:(h + 1) * HEAD_W], krc, zc], axis=-1), gk)
            k_s[h, T:n_keys, :] = kc.astype(BF16)
        q_s[:, h * LANE:(h + 1) * LANE] = (qh * q_scale).astype(BF16)
        k_s[h, 0:T, :] = kh.astype(BF16)
    for p in range(N_PAIRS):
        lanes = slice(GROUP_W + p * LANE, GROUP_W + (p + 1) * LANE)
        vp = kv[:, lanes]
        if ctx:
            vp = jnp.concatenate([vp, kvc[:, lanes]], axis=0)
        v_s[p] = _pair_blockdiag(vp.astype(BF16))

    def q_block(qb, carry):
        rows = pl.ds(pl.multiple_of(qb * ATT_QBLOCK, ATT_QBLOCK), ATT_QBLOCK)
        lane = lax.broadcasted_iota(jnp.int32, (ATT_QBLOCK, LANE), 1)
        for p in range(N_PAIRS):
            es, sums = [], []
            for h in (2 * p, 2 * p + 1):
                s = lax.dot_general(q_s[rows, h * LANE:(h + 1) * LANE], k_s[h], (((1,), (1,)), ((), ())),
                                    preferred_element_type=F32)
                e = jnp.exp2(s - jnp.max(s, axis=-1, keepdims=True))
                sums.append(jnp.sum(e, axis=-1, keepdims=True))
                es.append(e.astype(BF16))
            o = jnp.dot(jnp.concatenate(es, axis=-1), v_s[p], preferred_element_type=F32)
            o_ref[rows, p * LANE:(p + 1) * LANE] = o / jnp.where(lane < HEAD_W, sums[0], sums[1])
        return carry

    lax.fori_loop(0, T // ATT_QBLOCK, q_block, 0)


def _mla_call(u_mla, qn, wq, kvn, wkv, qkn, rope, cache, collect, l, T, nb, row0):
    tb = row0 // T
    ctx = cache is not None
    n_keys = T + (PAST_LEN if ctx else 0)
    c2 = lambda b: (0, 0)
    in_specs = [
        pl.BlockSpec((T, MLA_PAD), lambda b: (tb + b, 0)),
        pl.BlockSpec((None, 1, MLA_Q_LORA), lambda b: (l, 0, 0)),
        pl.BlockSpec((None, MLA_Q_LORA, N_HEADS * LANE), lambda b: (l, 0, 0)),
        pl.BlockSpec((None, 1, MLA_KV_LORA), lambda b: (l, 0, 0)),
        pl.BlockSpec((None, MLA_KV_LORA, 2 * GROUP_W), lambda b: (l, 0, 0)),
        pl.BlockSpec((None, 2, LANE), lambda b: (l, 0, 0)),
    ]
    args = [u_mla, qn, wq, kvn, wkv, qkn]
    out_specs = [pl.BlockSpec((T, GROUP_W), lambda b: (b, 0))]
    out_shape = [jax.ShapeDtypeStruct((nb * T, GROUP_W), F32)]
    if ctx:
        in_specs += [
            pl.BlockSpec((T, LANE), c2),
            pl.BlockSpec((T, LANE), c2),
            pl.BlockSpec((None, None, PAST_LEN, MLA_KV_LORA), lambda b: (b, l, 0, 0)),
            pl.BlockSpec((None, None, PAST_LEN, MLA_ROPE), lambda b: (b, l, 0, 0)),
        ]
        args += [rope[0], rope[1], cache[0], cache[1]]
        aliases = {}
    else:
        in_specs += [pl.BlockSpec(memory_space=pl.ANY)] * 2
        aliases = {len(args): 1, len(args) + 1: 2}
        args += list(collect)
        out_specs += [pl.BlockSpec((None, None, T, MLA_KV_LORA), lambda b: (b, l, 0, 0)),
                      pl.BlockSpec((None, None, T, MLA_ROPE), lambda b: (b, l, 0, 0))]
        out_shape += [jax.ShapeDtypeStruct(a.shape, F32) for a in collect]
    return pl.pallas_call(
        functools.partial(_mla_kernel, T=T, ctx=ctx),
        grid=(nb,),
        in_specs=in_specs,
        out_specs=out_specs,
        out_shape=out_shape,
        input_output_aliases=aliases,
        scratch_shapes=[pltpu.VMEM((T, N_HEADS * LANE), BF16), pltpu.VMEM((N_HEADS, n_keys, LANE), BF16),
                        pltpu.VMEM((N_PAIRS, 2 * n_keys, LANE), BF16)],
        compiler_params=_cparams(("parallel",)),
        name="mla",
    )(*args)


def _pad_cols(w, width):
    return jnp.pad(w, [(0, 0)] * (w.ndim - 1) + [(0, width - w.shape[-1])])


W_IN_PREP_COLS = 256


def _prep_w_in_kernel(wt_ref, o_hg, o_hy, o_mla, o_gd):
    start = 0
    for o_ref, cols in ((o_hg, HG_COLS), (o_hy, HY_COLS), (o_mla, MLA_COLS), (o_gd, GD_COLS)):
        width = o_ref.shape[-1]
        for c0 in range(0, width, W_IN_PREP_COLS):
            n_out = min(W_IN_PREP_COLS, width - c0)
            n_real = max(0, min(n_out, cols - c0))
            piece = wt_ref[start + c0:start + c0 + n_real, :]
            if n_real < n_out:
                piece = jnp.concatenate([piece, jnp.zeros((n_out - n_real, D_MODEL), F32)], axis=0)
            o_ref[:, c0:c0 + n_out] = piece.T.astype(BF16)
        start += cols


def _prep_w_in(w_in):
    widths = (HG_COLS, HY_COLS, MLA_PAD, GD_PAD)
    n_cols = w_in.shape[-1]
    return pl.pallas_call(
        _prep_w_in_kernel,
        grid=(DEPTH,),
        in_specs=[pl.BlockSpec((None, n_cols, D_MODEL), lambda l: (l, 0, 0))],
        out_specs=[pl.BlockSpec((None, D_MODEL, w), lambda l: (l, 0, 0)) for w in widths],
        out_shape=[jax.ShapeDtypeStruct((DEPTH, D_MODEL, w), BF16) for w in widths],
        compiler_params=_cparams(("parallel",)),
        name="w_in_prep",
    )(jnp.swapaxes(w_in, 1, 2))


def _prep_wq(w_q_up):
    w = w_q_up.reshape(DEPTH, MLA_Q_LORA, N_HEADS, MLA_QK)
    return _pad_cols(w, LANE).reshape(DEPTH, MLA_Q_LORA, N_HEADS * LANE).astype(BF16)


def _prep_wkv(w_kv_up):
    w = w_kv_up.reshape(DEPTH, MLA_KV_LORA, N_HEADS, 2, HEAD_W)
    return w.transpose(0, 1, 3, 2, 4).reshape(DEPTH, MLA_KV_LORA, 2 * GROUP_W).astype(BF16)


def _lower_bounds(hgrn_lb):
    lb = jnp.cumsum(jax.nn.softmax(hgrn_lb.astype(F32), axis=0), axis=0)
    return lb - lb[0]


def kernel(x_prompt, x_sample, cache_mla_ckv, cache_mla_krope, state_hgrn, state_gdn, c, c_ctx, w_ada, b_ada, norm_ffn, w_ffn_gu, w_ffn_down, norm_mix, w_in, w_out, hgrn_lb, hgrn_norm, hy_conv_w, hy_conv_b, hy_w1, hy_b1, hy_freq, hy_w2, hy_b2, hy_w3, hy_skip, mla_q_norm_a, mla_w_q_up, mla_kv_norm_a, mla_w_kv_up, mla_qk_norm, gdn_conv_w, gdn_a_log, gdn_dt_bias, gdn_norm):
    x = (x_prompt.reshape(N_PROMPT, D_MODEL), x_sample.reshape(N_SAMPLE, D_MODEL))

    cond8 = jnp.zeros((8, D_MODEL), F32).at[0].set(c_ctx).at[1:1 + DEC_BATCH].set(c)
    ada = _ada_call(cond8, w_ada, b_ada)

    w_in_parts = _prep_w_in(w_in)
    w_out_bf = w_out.astype(BF16)
    wq = _prep_wq(mla_w_q_up)
    wkv = _prep_wkv(mla_w_kv_up)
    qkn = _pad_cols(mla_qk_norm, LANE)
    lb_all = _lower_bounds(hgrn_lb)
    alog = _pad_cols(gdn_a_log.reshape(DEPTH, 1, 8), LANE)
    dtb = _pad_cols(gdn_dt_bias.reshape(DEPTH, 1, 8), LANE)
    gdn_gn = jnp.tile(gdn_norm, (1, N_HEADS)).reshape(DEPTH, 1, GROUP_W)
    w1p = jnp.pad(hy_w1, ((0, 0), (0, LANE - HY_EMB), (0, 0)))

    bd = jnp.asarray(_block_diag_ones(), BF16)
    hg_tril, hg_m = _hgrn_consts()
    hg_consts = (jnp.asarray(hg_tril, BF16), jnp.asarray(hg_m, F32), bd)
    gd_tril, gd_masks, gd_expand = _gdn_consts()
    gd_consts = (jnp.asarray(gd_tril, BF16), jnp.asarray(gd_masks, F32), jnp.asarray(gd_expand, BF16), bd)
    rope = tuple(jnp.asarray(a) for a in _rope_consts(DEC_SEQ))
    groups = ((SEQ, BATCH, 0), (DEC_SEQ, DEC_BATCH, N_PROMPT))
    dft = {}
    spec = {}
    for T, _, _ in groups:
        fwd, inv = _dft_consts(T)
        fh, fl = _np_split2(fwd)
        ih, il = _np_split2(inv)
        dft[T] = (fh, fl, ih, il)
        zp, win = _hyena_pos_consts(T)
        spec[T] = _hyfilt_call(T, jnp.asarray(zp), jnp.asarray(win), fh, fl, w1p,
                               hy_b1.reshape(DEPTH, 1, HY_FH), hy_freq.reshape(DEPTH, 1, HY_FH), hy_w2,
                               hy_b2.reshape(DEPTH, 1, HY_FH), hy_w3)

    new_ckv = jnp.zeros((BATCH, DEPTH, SEQ, MLA_KV_LORA), F32)
    new_kr = jnp.zeros((BATCH, DEPTH, SEQ, MLA_ROPE), F32)
    new_hg = jnp.zeros((BATCH, DEPTH, 2, N_HEADS, HEAD_W, HEAD_W), F32)
    new_gd = jnp.zeros((BATCH, DEPTH, 2, N_HEADS, HEAD_W, HEAD_W), F32)
    for l in range(DEPTH):
        x = _ffn_call(x, ada, norm_ffn, w_ffn_gu, w_ffn_down, l, 0)
        u_hg, u_hy, u_mla, u_gd = _inproj_call(x, ada, norm_mix, w_in_parts, l)
        outs = []
        for gi, (T, nb, row0) in enumerate(groups):
            latent = gi == 1
            o_hg, s_hg = _hgrn_call(u_hg, lb_all, hgrn_norm, hg_consts, state_hgrn if latent else None,
                                    None if latent else new_hg, l, T, nb, row0)
            o_hy = _hyena_call(u_hy, hy_conv_w, hy_conv_b.reshape(DEPTH, 1, HY_COLS), spec[T],
                               hy_skip.reshape(DEPTH, 1, GROUP_W), dft[T], l, T, nb, row0)
            mla = _mla_call(u_mla, mla_q_norm_a.reshape(DEPTH, 1, MLA_Q_LORA), wq,
                            mla_kv_norm_a.reshape(DEPTH, 1, MLA_KV_LORA), wkv, qkn,
                            rope if latent else None,
                            (cache_mla_ckv, cache_mla_krope) if latent else None,
                            None if latent else (new_ckv, new_kr), l, T, nb, row0)
            o_gd, s_gd = _gdn_call(u_gd, gdn_conv_w, alog, dtb, gdn_gn, gd_consts, state_gdn if latent else None,
                                   None if latent else new_gd, l, T, nb, row0)
            outs.append((o_hg, o_hy, mla[0], o_gd))
            if not latent:
                new_ckv, new_kr, new_hg, new_gd = mla[1], mla[2], s_hg, s_gd
        x = _outproj_call(x, ada, w_out_bf, outs[0], outs[1], l)
        x = _ffn_call(x, ada, norm_ffn, w_ffn_gu, w_ffn_down, l, 1)

    y_prompt = x[:N_PROMPT].reshape(BATCH, SEQ, D_MODEL)
    y_sample = x[N_PROMPT:].reshape(DEC_BATCH, DEC_SEQ, D_MODEL)
    return (y_prompt, y_sample, new_ckv, new_kr, new_hg, new_gd)
```

```python
import functools
import math

import numpy as np
import jax
import jax.numpy as jnp
from jax import lax
from jax.experimental import pallas as pl
from jax.experimental.pallas import tpu as pltpu

F32 = jnp.float32
BF16 = jnp.bfloat16

D_MODEL = 1024
BATCH = 16
SEQ = 256
DEPTH = 4
DEC_BATCH = 2
DEC_SEQ = 1024
PAST_LEN = 256
GRID_W = 64
N_ADA = 9
D_FF = 2816
GROUP_W = 256
MIX_W = 4 * GROUP_W
CHUNK = 64
RMS_EPS = 1e-6
N_HEADS = 4
HEAD_W = 64
HY_EMB = 33
HY_FH = 64
HY_TARGET = 1e-2
HY_FAST = 0.3
HY_SLOW = 1.5
MLA_NOPE = 64
MLA_ROPE = 32
MLA_QK = MLA_NOPE + MLA_ROPE
MLA_Q_LORA = 256
MLA_KV_LORA = 128
ROPE_BASE = 10000.0

HG_COLS = 5 * GROUP_W
HY_COLS = 3 * GROUP_W
MLA_COLS = MLA_Q_LORA + MLA_KV_LORA + MLA_ROPE
GD_COLS = 4 * GROUP_W + 16
MLA_PAD = 512
GD_PAD = 1152

N_PROMPT = BATCH * SEQ
N_SAMPLE = DEC_BATCH * DEC_SEQ
N_TOK = N_PROMPT + N_SAMPLE
LANE = 128
VMEM_LIMIT = 56 * 1024 * 1024
ROW_TILE = 1024
FF_TILE = 256
ADA_TILE = 1536
ATT_QBLOCK = 256


def _bdot(a, b):
    return jnp.dot(a.astype(BF16), b.astype(BF16), preferred_element_type=F32)


def _bdot_nt(a, b):
    return lax.dot_general(a.astype(BF16), b.astype(BF16), (((1,), (1,)), ((), ())),
                           preferred_element_type=F32)


def _bdot_tn(a, b):
    return lax.dot_general(a.astype(BF16), b.astype(BF16), (((0,), (0,)), ((), ())),
                           preferred_element_type=F32)


def _split2(x):
    hi = x.astype(BF16)
    lo = (x - hi.astype(F32)).astype(BF16)
    return hi, lo


def _split3(x):
    hi = x.astype(BF16)
    r = x - hi.astype(F32)
    mid = r.astype(BF16)
    lo = (r - mid.astype(F32)).astype(BF16)
    return hi, mid, lo


def _dot3(a, b):
    ah, al = _split2(a)
    bh, bl = _split2(b)
    return (jnp.dot(ah, bh, preferred_element_type=F32) + jnp.dot(ah, bl, preferred_element_type=F32)
            + jnp.dot(al, bh, preferred_element_type=F32))


def _sel_dot(c, x):
    h, m, l = _split3(x)
    return (jnp.dot(c, h, preferred_element_type=F32) + jnp.dot(c, m, preferred_element_type=F32)
            + jnp.dot(c, l, preferred_element_type=F32))


def _dot_sel(x, c):
    h, m, l = _split3(x)
    return (jnp.dot(h, c, preferred_element_type=F32) + jnp.dot(m, c, preferred_element_type=F32)
            + jnp.dot(l, c, preferred_element_type=F32))


def _sigmoid(x):
    return 1.0 / (1.0 + jnp.exp(-x))


def _silu(x):
    return x * _sigmoid(x)


def _rms(x, g):
    return x * lax.rsqrt(jnp.mean(x * x, axis=-1, keepdims=True) + RMS_EPS) * g


def _cparams(sem):
    return pltpu.CompilerParams(dimension_semantics=sem, vmem_limit_bytes=VMEM_LIMIT)


def _cond_of_tile(i):
    return jnp.maximum(i - (N_PROMPT // ROW_TILE - 1), 0)


def _ada_kernel(c_ref, w_ref, b_ref, o_ref):
    o_ref[...] = _dot3(_silu(c_ref[...]), w_ref[...]) + b_ref[...]


def _ada_call(cond8, w_ada, b_ada):
    n = N_ADA * D_MODEL
    out = pl.pallas_call(
        _ada_kernel,
        grid=(DEPTH, n // ADA_TILE),
        in_specs=[
            pl.BlockSpec((8, D_MODEL), lambda l, j: (0, 0)),
            pl.BlockSpec((None, D_MODEL, ADA_TILE), lambda l, j: (l, 0, j)),
            pl.BlockSpec((None, 1, ADA_TILE), lambda l, j: (l, 0, j)),
        ],
        out_specs=pl.BlockSpec((None, 8, ADA_TILE), lambda l, j: (l, 0, j)),
        out_shape=jax.ShapeDtypeStruct((DEPTH, 8, n), F32),
        compiler_params=_cparams(("parallel", "parallel")),
        name="ada",
    )(cond8, w_ada, b_ada.reshape(DEPTH, 1, n))
    return out.reshape(DEPTH, 8, N_ADA, D_MODEL)


FFN_SUBTILES = 2


def _ffn_kernel(*refs, sub, split):
    nx = 1 if split is None else 2
    x_refs = refs[:nx]
    ada_refs = refs[nx:nx + FFN_SUBTILES]
    g_ref, wg_ref, wu_ref, wd_ref, o_ref, h_scr = refs[nx + FFN_SUBTILES:]
    i = pl.program_id(0)
    f = pl.program_id(1)

    def prologue(x_ref):
        for r, ada_ref in enumerate(ada_refs):
            rows = slice(r * ROW_TILE, (r + 1) * ROW_TILE)
            y = _rms(x_ref[rows, :], g_ref[...])
            h = y * (1.0 + ada_ref[3 * sub + 1:3 * sub + 2, :]) + ada_ref[3 * sub:3 * sub + 1, :]
            h_scr[rows, :] = h.astype(BF16)
        o_ref[...] = jnp.zeros_like(o_ref)

    def epilogue(x_ref):
        for r, ada_ref in enumerate(ada_refs):
            rows = slice(r * ROW_TILE, (r + 1) * ROW_TILE)
            o_ref[rows, :] = x_ref[rows, :] + 0.5 * ada_ref[3 * sub + 2:3 * sub + 3, :] * o_ref[rows, :]

    def on(step, fn):
        if split is None:
            pl.when(f == step)(functools.partial(fn, x_refs[0]))
        else:
            pl.when((f == step) & (i < split))(functools.partial(fn, x_refs[0]))
            pl.when((f == step) & (i >= split))(functools.partial(fn, x_refs[1]))

    on(0, prologue)
    wg = wg_ref[...].astype(BF16)
    wu = wu_ref[...].astype(BF16)
    wd = wd_ref[...].astype(BF16)
    for r in range(FFN_SUBTILES):
        rows = slice(r * ROW_TILE, (r + 1) * ROW_TILE)
        h = h_scr[rows, :]
        gate = jnp.dot(h, wg, preferred_element_type=F32)
        up = jnp.dot(h, wu, preferred_element_type=F32)
        a = (_silu(gate) * up).astype(BF16)
        o_ref[rows, :] += jnp.dot(a, wd, preferred_element_type=F32)
    on(pl.num_programs(1) - 1, epilogue)


def _ffn_call(xs, ada, norm_ffn, w_gu, w_down, l, j):
    sub = 2 * j
    nf = D_FF // FF_TILE
    rows = FFN_SUBTILES * ROW_TILE

    def ada_spec(r):
        return pl.BlockSpec((None, None, N_ADA, D_MODEL),
                            lambda i, f: (l, _cond_of_tile(i * FFN_SUBTILES + r), 0, 0))

    if isinstance(xs, tuple):
        split = xs[0].shape[0] // rows
        n_tail = xs[1].shape[0] // rows
        x_specs = [pl.BlockSpec((rows, D_MODEL), lambda i, f: (jnp.minimum(i, split - 1), 0),
                                pipeline_mode=pl.Buffered(1)),
                   pl.BlockSpec((rows, D_MODEL), lambda i, f: (jnp.clip(i - split, 0, n_tail - 1), 0),
                                pipeline_mode=pl.Buffered(1))]
    else:
        split = None
        xs = (xs,)
        x_specs = [pl.BlockSpec((rows, D_MODEL), lambda i, f: (i, 0), pipeline_mode=pl.Buffered(1))]

    return pl.pallas_call(
        functools.partial(_ffn_kernel, sub=sub, split=split),
        grid=(N_TOK // rows, nf),
        in_specs=x_specs
        + [ada_spec(r) for r in range(FFN_SUBTILES)] + [
            pl.BlockSpec((None, None, 1, D_MODEL), lambda i, f: (l, j, 0, 0)),
            pl.BlockSpec((None, None, D_MODEL, FF_TILE), lambda i, f: (l, j, 0, f)),
            pl.BlockSpec((None, None, D_MODEL, FF_TILE), lambda i, f: (l, j, 0, nf + f)),
            pl.BlockSpec((None, None, FF_TILE, D_MODEL), lambda i, f: (l, j, f, 0)),
        ],
        out_specs=pl.BlockSpec((rows, D_MODEL), lambda i, f: (i, 0)),
        out_shape=jax.ShapeDtypeStruct((N_TOK, D_MODEL), F32),
        scratch_shapes=[pltpu.VMEM((rows, D_MODEL), BF16)],
        compiler_params=_cparams(("parallel", "arbitrary")),
        name="ffn",
    )(*xs, *([ada] * FFN_SUBTILES), norm_ffn.reshape(DEPTH, 2, 1, D_MODEL), w_gu, w_gu, w_down)


def _mix_ffn_kernel(x_ref, *refs, n_ctx):
    n_mix = MIX_W // GROUP_W
    o_ctx, o_lat = refs[0:n_mix], refs[n_mix:2 * n_mix]
    w_out_ref, ada_ref, g_ref, wg_ref, wu_ref, wd_ref, out_ref, h_scr, x1_scr = refs[2 * n_mix:]
    i = pl.program_id(0)
    f = pl.program_id(1)

    def prologue(srcs):
        acc = jnp.zeros((ROW_TILE, D_MODEL), F32)
        for g, src in enumerate(srcs):
            acc += jnp.dot(src[...].astype(BF16), w_out_ref[g * GROUP_W:(g + 1) * GROUP_W, :].astype(BF16),
                           preferred_element_type=F32)
        x1 = x_ref[...] + ada_ref[5:6, :] * acc
        x1_scr[...] = x1
        h = _rms(x1, g_ref[...]) * (1.0 + ada_ref[7:8, :]) + ada_ref[6:7, :]
        h_scr[...] = h.astype(BF16)
        out_ref[...] = jnp.zeros_like(out_ref)

    pl.when((f == 0) & (i < n_ctx))(functools.partial(prologue, o_ctx))
    pl.when((f == 0) & (i >= n_ctx))(functools.partial(prologue, o_lat))

    h = h_scr[...]
    gate = jnp.dot(h, wg_ref[...].astype(BF16), preferred_element_type=F32)
    up = jnp.dot(h, wu_ref[...].astype(BF16), preferred_element_type=F32)
    a = (_silu(gate) * up).astype(BF16)
    out_ref[...] += jnp.dot(a, wd_ref[...].astype(BF16), preferred_element_type=F32)

    @pl.when(f == pl.num_programs(1) - 1)
    def _():
        out_ref[...] = x1_scr[...] + 0.5 * ada_ref[8:9, :] * out_ref[...]


def _mix_ffn_call(x, ada, norm_ffn, w_gu, w_down, w_out, o_ctx, o_lat, l):
    nf = D_FF // FF_TILE
    n_ctx = N_PROMPT // ROW_TILE
    n_lat = N_SAMPLE // ROW_TILE
    once = dict(pipeline_mode=pl.Buffered(1))
    mix_specs = ([pl.BlockSpec((ROW_TILE, GROUP_W), lambda i, f: (jnp.minimum(i, n_ctx - 1), 0), **once)] * len(o_ctx)
                 + [pl.BlockSpec((ROW_TILE, GROUP_W), lambda i, f: (jnp.clip(i - n_ctx, 0, n_lat - 1), 0), **once)]
                 * len(o_lat))
    return pl.pallas_call(
        functools.partial(_mix_ffn_kernel, n_ctx=n_ctx),
        grid=(N_TOK // ROW_TILE, nf),
        in_specs=[pl.BlockSpec((ROW_TILE, D_MODEL), lambda i, f: (i, 0), **once)] + mix_specs + [
            pl.BlockSpec((None, MIX_W, D_MODEL), lambda i, f: (l, 0, 0), **once),
            pl.BlockSpec((None, None, N_ADA, D_MODEL), lambda i, f: (l, _cond_of_tile(i), 0, 0)),
            pl.BlockSpec((None, None, 1, D_MODEL), lambda i, f: (l, 1, 0, 0)),
            pl.BlockSpec((None, None, D_MODEL, FF_TILE), lambda i, f: (l, 1, 0, f)),
            pl.BlockSpec((None, None, D_MODEL, FF_TILE), lambda i, f: (l, 1, 0, nf + f)),
            pl.BlockSpec((None, None, FF_TILE, D_MODEL), lambda i, f: (l, 1, f, 0)),
        ],
        out_specs=pl.BlockSpec((ROW_TILE, D_MODEL), lambda i, f: (i, 0)),
        out_shape=jax.ShapeDtypeStruct((N_TOK, D_MODEL), F32),
        scratch_shapes=[pltpu.VMEM((ROW_TILE, D_MODEL), BF16), pltpu.VMEM((ROW_TILE, D_MODEL), F32)],
        compiler_params=_cparams(("parallel", "arbitrary")),
        name="mix_ffn",
    )(x, *o_ctx, *o_lat, w_out, ada, norm_ffn.reshape(DEPTH, 2, 1, D_MODEL), w_gu, w_gu, w_down)


IN_TILE = 512


def _inproj_kernel(x_ref, ada_ref, g_ref, w1, w2, w3, w4, o1, o2, o3, o4):
    y = _rms(x_ref[...], g_ref[...])
    h = (y * (1.0 + ada_ref[4:5, :]) + ada_ref[3:4, :]).astype(BF16)
    for w, o in ((w1, o1), (w2, o2), (w3, o3), (w4, o4)):
        o[...] = jnp.dot(h, w[...], preferred_element_type=F32)


def _inproj_call(x, ada, norm_mix, ws, l):
    widths = (HG_COLS, HY_COLS, MLA_PAD, GD_PAD)
    per = ROW_TILE // IN_TILE
    return pl.pallas_call(
        _inproj_kernel,
        grid=(N_TOK // IN_TILE,),
        in_specs=[
            pl.BlockSpec((IN_TILE, D_MODEL), lambda i: (i, 0)),
            pl.BlockSpec((None, None, N_ADA, D_MODEL), lambda i: (l, _cond_of_tile(i // per), 0, 0)),
            pl.BlockSpec((None, 1, D_MODEL), lambda i: (l, 0, 0)),
        ] + [pl.BlockSpec((None, D_MODEL, w), lambda i: (l, 0, 0)) for w in widths],
        out_specs=[pl.BlockSpec((IN_TILE, w), lambda i: (i, 0)) for w in widths],
        out_shape=[jax.ShapeDtypeStruct((N_TOK, w), F32) for w in widths],
        compiler_params=_cparams(("parallel",)),
        name="inproj",
    )(x, ada, norm_mix.reshape(DEPTH, 1, D_MODEL), *ws)


def _block_diag_ones():
    idx = np.arange(GROUP_W) // HEAD_W
    return (idx[:, None] == idx[None, :]).astype(np.float32)


def _hgrn_consts():
    C = CHUNK
    i = np.arange(C)[:, None]
    j = np.arange(C)[None, :]
    masks = []
    s = C // 2
    while s >= 1:
        up_i = (i // s) % 2 == 1
        up_j = (j // s) % 2 == 1
        masks.append(up_i & (~up_j) & (i // (2 * s) == j // (2 * s)))
        s //= 2
    masks.append(j <= i)
    fwd_m = np.stack([m.astype(np.float32) for m in masks])
    bwd_m = np.stack([m.astype(np.float32)[::-1, ::-1] for m in masks])
    tril = np.stack([(j <= i), (j >= i)]).astype(np.float32)
    return tril, np.tile(np.stack([fwd_m, bwd_m]), (1, 1, 1, 2))


def _gdn_consts():
    C = CHUNK
    i = np.arange(C)[:, None]
    t = np.arange(C)[None, :]
    tril = np.stack([(t <= i), (t >= i)]).astype(np.float32)
    masks = np.stack([np.stack([(t <= i), (t < i)]), np.stack([(t >= i), (t > i)])]).astype(np.float32)
    expand = np.zeros((2, LANE, 2 * GROUP_W), np.float32)
    for d in range(2):
        for h in range(N_HEADS):
            expand[d, d * N_HEADS + h, h * HEAD_W:(h + 1) * HEAD_W] = 1.0
            expand[d, 8 + d * N_HEADS + h, GROUP_W + h * HEAD_W:GROUP_W + (h + 1) * HEAD_W] = 1.0
    return tril, masks, expand


def _dft_consts(T):
    n2 = 4 * T
    k = np.arange(T, dtype=np.int64)[:, None]
    s = np.arange(T, dtype=np.int64)[None, :]
    ang = np.pi * (((2 * k + 1) * s) % n2).astype(np.float64) / (2 * T)
    fwd = np.concatenate([np.cos(ang), -np.sin(ang)], axis=0)
    inv = fwd.T / T
    return fwd.astype(np.float32), inv.astype(np.float32)


def _np_split2(x):
    hi = jnp.asarray(x, F32).astype(BF16)
    lo = (jnp.asarray(x, F32) - hi.astype(F32)).astype(BF16)
    return hi, lo


def _hyena_pos_consts(T):
    pos = np.arange(T, dtype=np.float32)
    t = pos / np.float32(T - 1)
    bands = np.linspace(1e-4, (HY_EMB - 1) // 2 - 1, (HY_EMB - 1) // 2, dtype=np.float32)
    ang = (np.float32(2.0 * math.pi / T) * pos[:, None]) * bands[None, :]
    z = np.concatenate([t[:, None], np.cos(ang), -np.sin(ang)], axis=-1).astype(np.float32)
    zp = np.zeros((T, LANE), np.float32)
    zp[:, :HY_EMB] = z
    max_decay = math.log(HY_TARGET) / HY_FAST
    min_decay = math.log(HY_TARGET) / HY_SLOW
    deltas = np.linspace(min_decay, max_decay, GROUP_W, dtype=np.float32)
    window = np.exp(-t[:, None] * np.abs(deltas)[None, :]).astype(np.float32)
    return zp, window


def _rope_consts(T):
    rows = T // GRID_W
    row = np.repeat(np.arange(rows, dtype=np.float32), GRID_W)
    col = (np.arange(T) % GRID_W).astype(np.float32)
    pairs = MLA_ROPE // 4
    inv = (np.float32(ROPE_BASE) ** (-np.arange(pairs, dtype=np.float32) / np.float32(pairs))).astype(np.float32)
    ang = np.concatenate([row[:, None] * inv, col[:, None] * inv], axis=-1).astype(np.float32)
    cos, sin = np.cos(ang), np.sin(ang)
    cosf = np.ones((T, LANE), np.float32)
    sinf = np.zeros((T, LANE), np.float32)
    half = MLA_ROPE // 2
    cosf[:, MLA_NOPE:MLA_NOPE + half] = cos
    cosf[:, MLA_NOPE + half:MLA_QK] = cos
    sinf[:, MLA_NOPE:MLA_NOPE + half] = -sin
    sinf[:, MLA_NOPE + half:MLA_QK] = sin
    return cosf, sinf


def _head_norm_gate(tot, bd, gn, gate):
    ms = _sel_dot_right(tot * tot, bd) * (1.0 / HEAD_W)
    return tot * lax.rsqrt(ms + RMS_EPS) * gn * _silu(gate)


def _sel_dot_right(x, c):
    h, l = _split2(x)
    return jnp.dot(h, c, preferred_element_type=F32) + jnp.dot(l, c, preferred_element_type=F32)


def _block_ref(b, two_s, r):
    C, W = b.shape
    if two_s % 8 == 0:
        b3 = b.reshape(C // two_s, two_s, W)
        return jnp.broadcast_to(b3[:, r:r + 1, :], b3.shape).reshape(C, W)
    pos = lax.broadcasted_iota(jnp.int32, b.shape, 0) % two_s
    out = b
    for p in range(two_s):
        if p != r:
            out = jnp.where(pos == p, pltpu.roll(b, (p - r) % C, 0), out)
    return out


N_PAIRS = N_HEADS // 2
HG_GROUP = 2
HG_DIRECT_MAX = 80.0


def _pair_blockdiag(x):
    lane = lax.broadcasted_iota(jnp.int32, x.shape, 1)
    zero = jnp.zeros_like(x)
    return jnp.concatenate([jnp.where(lane < HEAD_W, x, zero), jnp.where(lane >= HEAD_W, x, zero)], axis=0)


def _hgrn_kernel(*refs, T, has_s0):
    if has_s0:
        (u_ref, lb_ref, gn_ref, tril_ref, lmask_ref, bd_ref, s0_ref,
         o_ref, sfin_ref, oi_s, qin_s, up_s, dc_s, st_s) = refs
    else:
        (u_ref, lb_ref, gn_ref, tril_ref, lmask_ref, bd_ref, _,
         o_ref, sfin_ref, oi_s, qin_s, up_s, dc_s, st_s) = refs
    n = T // CHUNK
    C = CHUNK
    bd = bd_ref[...]
    n_lv = int(math.log2(C))

    log_lb = [jnp.log(lb_ref[d]) for d in range(2)]
    log_1mlb = [jnp.log(1.0 - lb_ref[d]) for d in range(2)]

    def gates(rows, d):
        z = u_ref[rows, (3 + d) * GROUP_W:(4 + d) * GROUP_W]
        t = jnp.exp(-jnp.abs(z))
        log_sig = jnp.minimum(z, 0.0) - jnp.log(1.0 + t)
        c = log_1mlb[d] + log_sig
        m = jnp.maximum(log_lb[d], c)
        lf = m + jnp.log(1.0 + jnp.exp(jnp.minimum(log_lb[d], c) - m))
        sig_neg = jnp.where(z > 0.0, t, 1.0) / (1.0 + t)
        return lf, (1.0 - lb_ref[d]) * sig_neg

    for d in range(2):
        if has_s0:
            st_s[d] = jnp.concatenate([s0_ref[d, h].T for h in range(N_HEADS)], axis=-1)
        else:
            st_s[d] = jnp.zeros((HEAD_W, GROUP_W), F32)

    def prepare(it, carry):
        units = [(c, d) for c in range(HG_GROUP) for d in range(2)]
        rows = [pl.ds(pl.multiple_of((it * HG_GROUP + c) * C, C), C) for c in range(HG_GROUP)]
        arow = [pl.ds(pl.multiple_of((it * HG_GROUP + c) * 8, 8), 8) for c in range(HG_GROUP)]
        q = [u_ref[rows[c], 0:GROUP_W] * (HEAD_W ** -0.5) for c in range(HG_GROUP)]
        v = [u_ref[rows[c], GROUP_W:2 * GROUP_W] for c in range(HG_GROUP)]
        vt = [[jnp.concatenate([v[c][:, h * HEAD_W:(h + 1) * HEAD_W].T for h in (2 * p, 2 * p + 1)],
                               axis=-1).astype(BF16) for p in range(N_PAIRS)]
              for c in range(HG_GROUP)]
        v_bd = [[_pair_blockdiag(v[c][:, p * LANE:(p + 1) * LANE].astype(BF16)) for p in range(N_PAIRS)]
                for c in range(HG_GROUP)]
        lf, ks = zip(*[gates(rows[c], d) for c, d in units])
        bs = []
        for i, (c, d) in enumerate(units):
            hi, lo = _split2(lf[i])
            tril = tril_ref[d]
            bs.append(jnp.dot(tril, hi, preferred_element_type=F32) + jnp.dot(tril, lo, preferred_element_type=F32))
        tot = [jnp.sum(x, axis=0, keepdims=True) for x in lf]
        mid_row = [C // 2 - 1 if d == 0 else C // 2 for c, d in units]
        spread = [jnp.maximum(jnp.abs(bs[i][0:1] - bs[i][r:r + 1]), jnp.abs(bs[i][C - 1:C] - bs[i][r:r + 1]))
                  for i, r in enumerate(mid_row)]
        widest = functools.reduce(jnp.maximum, spread)
        ko = [(ks[i] * jnp.exp(tot[i] - bs[i])).astype(BF16) for i in range(len(units))]
        up = [[jnp.dot(vt[c][p], _pair_blockdiag(ko[i][:, p * LANE:(p + 1) * LANE]), preferred_element_type=F32)
               for p in range(N_PAIRS)] for i, (c, d) in enumerate(units)]
        for i, (c, d) in enumerate(units):
            qin_s[d, rows[c], :] = q[c] * jnp.exp(bs[i])
            up_s[d, rows[c], :] = jnp.concatenate(up[i], axis=-1)
            dc_s[d, arow[c], :] = jnp.broadcast_to(jnp.exp(tot[i]), (8, GROUP_W))

        def masked_scores(qe, ke, lv):
            out = []
            for i, (c, d) in enumerate(units):
                per_pair = []
                for p in range(N_PAIRS):
                    sl = slice(p * LANE, (p + 1) * LANE)
                    prod = lax.dot_general(qe[i][:, sl], _pair_blockdiag(ke[i][:, sl]), (((1,), (1,)), ((), ())),
                                           preferred_element_type=F32)
                    per_pair.append(jnp.where(lmask_ref[d, lv] > 0.5, prod, 0.0))
                out.append(per_pair)
            return out

        def finish(sc):
            return [jnp.concatenate([jnp.dot(sc[i][p].astype(BF16), v_bd[c][p], preferred_element_type=F32)
                                     for p in range(N_PAIRS)], axis=-1) for i, (c, d) in enumerate(units)]

        def intra_direct():
            mid = [_block_ref(bs[i], C, C // 2 - 1 if d == 0 else C // 2) for i, (c, d) in enumerate(units)]
            qe = [(q[c] * jnp.exp(bs[i] - mid[i])).astype(BF16) for i, (c, d) in enumerate(units)]
            ke = [(ks[i] * jnp.exp(mid[i] - bs[i])).astype(BF16) for i in range(len(units))]
            return tuple(finish(masked_scores(qe, ke, n_lv)))

        def intra_split():
            sc = [[jnp.zeros((C, LANE), F32) for _ in range(N_PAIRS)] for _ in units]
            s = C // 2
            lv = 0
            while s >= 1:
                e = [jnp.exp(-jnp.abs(bs[i] - _block_ref(bs[i], 2 * s, s - 1 if d == 0 else s)))
                     for i, (c, d) in enumerate(units)]
                part = masked_scores([(q[c] * e[i]).astype(BF16) for i, (c, d) in enumerate(units)],
                                     [(ks[i] * e[i]).astype(BF16) for i in range(len(units))], lv)
                sc = [[sc[i][p] + part[i][p] for p in range(N_PAIRS)] for i in range(len(units))]
                s //= 2
                lv += 1
            fin = finish(sc)
            return tuple(fin[i] + _bdot(q[c] * ks[i], bd) * v[c] for i, (c, d) in enumerate(units))

        oi = lax.cond(jnp.max(widest) < HG_DIRECT_MAX, intra_direct, intra_split)
        for i, (c, d) in enumerate(units):
            oi_s[d, rows[c], :] = oi[i]
        return carry

    lax.fori_loop(0, n // HG_GROUP, prepare, 0)

    def chunk(ci, carry):
        rows = [pl.ds(pl.multiple_of(cidx * C, C), C) for cidx in (ci, n - 1 - ci)]
        decay = [dc_s[d, pl.ds(pl.multiple_of(cidx * 8, 8), 1), :] for d, cidx in ((0, ci), (1, n - 1 - ci))]
        st = [st_s[d] for d in range(2)]
        o_inter = [[lax.dot_general(qin_s[d, rows[d], p * LANE:(p + 1) * LANE].astype(BF16),
                                    _pair_blockdiag(st[d][:, p * LANE:(p + 1) * LANE].astype(BF16)),
                                    (((1,), (1,)), ((), ())), preferred_element_type=F32)
                    for p in range(N_PAIRS)] for d in range(2)]
        for d in range(2):
            st_s[d] = st[d] * decay[d] + up_s[d, rows[d], :]
            oi_s[d, rows[d], :] = oi_s[d, rows[d], :] + jnp.concatenate(o_inter[d], axis=-1)
        return carry

    lax.fori_loop(0, n, chunk, 0, unroll=2)
    o_ref[...] = _head_norm_gate(oi_s[0] + oi_s[1], bd, gn_ref[...], u_ref[:, 2 * GROUP_W:3 * GROUP_W])
    for d in range(2):
        for h in range(N_HEADS):
            sfin_ref[d, h] = st_s[d][:, h * HEAD_W:(h + 1) * HEAD_W].T


def _state_io(in_specs, args, s0, collect, l, nb):
    state_block = (None, None, 2, N_HEADS, HEAD_W, HEAD_W)
    if s0 is not None:
        in_specs.append(pl.BlockSpec(state_block, lambda b: (b, l, 0, 0, 0, 0)))
        args.append(s0)
        return (pl.BlockSpec(state_block[1:], lambda b: (b, 0, 0, 0, 0)),
                jax.ShapeDtypeStruct((nb, 2, N_HEADS, HEAD_W, HEAD_W), F32), {})
    in_specs.append(pl.BlockSpec(memory_space=pl.ANY))
    args.append(collect)
    return (pl.BlockSpec(state_block, lambda b: (b, l, 0, 0, 0, 0)),
            jax.ShapeDtypeStruct(collect.shape, F32), {len(args) - 1: 1})


def _hgrn_call(u_hg, lb_l, gn, consts, s0, collect, l, T, nb, row0):
    tril, lmask, bd = consts
    tb = row0 // T
    has_s0 = s0 is not None
    in_specs = [
        pl.BlockSpec((T, HG_COLS), lambda b: (tb + b, 0)),
        pl.BlockSpec((None, 2, 1, GROUP_W), lambda b: (l, 0, 0, 0)),
        pl.BlockSpec((None, 1, GROUP_W), lambda b: (l, 0, 0)),
        pl.BlockSpec((2, CHUNK, CHUNK), lambda b: (0, 0, 0)),
        pl.BlockSpec((2, 7, CHUNK, LANE), lambda b: (0, 0, 0, 0)),
        pl.BlockSpec((GROUP_W, GROUP_W), lambda b: (0, 0)),
    ]
    args = [u_hg, lb_l.reshape(DEPTH, 2, 1, GROUP_W), gn.reshape(DEPTH, 1, GROUP_W), tril, lmask, bd]
    state_spec, state_shape, aliases = _state_io(in_specs, args, s0, collect, l, nb)
    seq = pltpu.VMEM((2, T, GROUP_W), F32)
    return pl.pallas_call(
        functools.partial(_hgrn_kernel, T=T, has_s0=has_s0),
        grid=(nb,),
        in_specs=in_specs,
        out_specs=[pl.BlockSpec((T, GROUP_W), lambda b: (b, 0)), state_spec],
        out_shape=[jax.ShapeDtypeStruct((nb * T, GROUP_W), F32), state_shape],
        input_output_aliases=aliases,
        scratch_shapes=[seq, seq, seq,
                        pltpu.VMEM((2, T // CHUNK * 8, GROUP_W), F32), pltpu.VMEM((2, HEAD_W, GROUP_W), F32)],
        compiler_params=_cparams(("parallel",)),
        name="hgrn",
    )(*args)


def _shift_rows(x, T):
    row = lax.broadcasted_iota(jnp.int32, x.shape, 0)
    prev = jnp.where(row == 0, 0.0, pltpu.roll(x, 1, 0))
    nxt = jnp.where(row == T - 1, 0.0, pltpu.roll(x, T - 1, 0))
    return prev, nxt


def _conv3(x, w_ref, T):
    prev, nxt = _shift_rows(x, T)
    return prev * w_ref[0:1, :] + x * w_ref[1:2, :] + nxt * w_ref[2:3, :]


GDN_UNROLL = 2


def _solve_unit_lower(systems):
    c2 = 2 * CHUNK
    slabs = [jnp.concatenate([nmat, nmat, rhs], axis=-1) for rhs, nmat in systems]
    steps = int(math.log2(CHUNK))
    for step in range(steps):
        last = step == steps - 1
        nxt = []
        for slab in slabs:
            hi = slab.astype(BF16)
            lo = (slab - hi.astype(F32)).astype(BF16)
            lhs = jnp.concatenate([hi[:, :c2], lo[:, :CHUNK]], axis=-1)
            first = c2 if last else 0
            rhs3 = jnp.concatenate([hi[:, first:], lo[:, first:], hi[:, first:]], axis=0)
            prod = jnp.dot(lhs, rhs3, preferred_element_type=F32)
            if last:
                nxt.append(slab[:, c2:] + prod)
            else:
                nxt.append(jnp.concatenate([prod[:, :c2], slab[:, c2:] + prod[:, c2:]], axis=-1))
        slabs = nxt
    return slabs


def _gdn_kernel(*refs, T, has_s0):
    if has_s0:
        (u_ref, cw_ref, alog_ref, dtb_ref, exp_ref, tril_ref, mask_ref, bd_ref, gn_ref, s0_ref,
         o_ref, sfin_ref, q_s, k_s, v_s, la_s, be_s, uw_s, ww_s, at_s, qin_s, kt_s, al_s, of_s, st_s) = refs
    else:
        (u_ref, cw_ref, alog_ref, dtb_ref, exp_ref, tril_ref, mask_ref, bd_ref, gn_ref, _,
         o_ref, sfin_ref, q_s, k_s, v_s, la_s, be_s, uw_s, ww_s, at_s, qin_s, kt_s, al_s, of_s, st_s) = refs
    n = T // CHUNK
    C = CHUNK
    bd = bd_ref[...]

    qkv = _silu(_conv3(u_ref[:, 0:3 * GROUP_W], cw_ref, T))
    q = qkv[:, 0:GROUP_W]
    k = qkv[:, GROUP_W:2 * GROUP_W]
    q_s[...] = q * lax.rsqrt(_sel_dot_right(q * q, bd) + 1e-6) * (HEAD_W ** -0.5)
    k_s[...] = k * lax.rsqrt(_sel_dot_right(k * k, bd) + 1e-6)
    v_s[...] = qkv[:, 2 * GROUP_W:3 * GROUP_W]

    ab = u_ref[:, 4 * GROUP_W:4 * GROUP_W + LANE]
    xa = ab + dtb_ref[...]
    softplus = jnp.maximum(xa, 0.0) + jnp.log(1.0 + jnp.exp(-jnp.abs(xa)))
    log_a = -jnp.exp(alog_ref[...]) * softplus
    lane = lax.broadcasted_iota(jnp.int32, ab.shape, 1)
    narrow = jnp.where(lane < 8, log_a, _sigmoid(ab))
    for d in range(2):
        wide = _dot_sel(narrow, exp_ref[d])
        la_s[d] = wide[:, 0:GROUP_W]
        be_s[d] = wide[:, GROUP_W:2 * GROUP_W]
        if has_s0:
            st_s[d] = jnp.concatenate([s0_ref[d, h] for h in range(N_HEADS)], axis=-1)
        else:
            st_s[d] = jnp.zeros((HEAD_W, GROUP_W), F32)

    def prepare(cidx):
        r0 = pl.multiple_of(cidx * C, C)
        rows = pl.ds(r0, C)
        arow = pl.ds(pl.multiple_of(cidx * 8, 8), 8)
        q = q_s[rows, :]
        k = k_s[rows, :]
        v = v_s[rows, :]
        systems = []
        attns = []
        kts = []
        for d in range(2):
            incl = mask_ref[d, 0] > 0.5
            strict = mask_ref[d, 1]
            la = la_s[d, rows, :]
            be = be_s[d, rows, :]
            gx = _sel_dot(tril_ref[d], la)
            gtot = jnp.sum(la, axis=0, keepdims=True)
            eg = jnp.exp(gx)
            kout = k * jnp.exp(gtot - gx)
            qin_s[d, rows, :] = q * eg
            al_s[d, arow, :] = jnp.broadcast_to(jnp.exp(gtot), (8, GROUP_W))
            kb = k * be
            vb = v * be
            kbg = kb * eg
            for h in range(N_HEADS):
                sl = slice(h * HEAD_W, (h + 1) * HEAD_W)
                gh = gx[:, sl]
                dmat = gh - gh.T
                dec = jnp.where(incl, jnp.exp(jnp.where(incl, dmat, 0.0)), 0.0)
                qk = _bdot_nt(jnp.concatenate([kb[:, sl], q[:, sl]], axis=0), k[:, sl])
                nmat = -(qk[:C] * dec * strict)
                systems.append((jnp.concatenate([vb[:, sl], kbg[:, sl]], axis=-1), nmat))
                attns.append(qk[C:] * dec)
                kts.append(kout[:, sl].T)
        sols = _solve_unit_lower(systems)
        for d in range(2):
            mine = sols[d * N_HEADS:(d + 1) * N_HEADS]
            uw_s[d, rows, :] = jnp.concatenate([x[:, :HEAD_W] for x in mine], axis=-1)
            ww_s[d, rows, :] = jnp.concatenate([x[:, HEAD_W:] for x in mine], axis=-1)
            at_s[d, rows, :] = jnp.concatenate(attns[d * N_HEADS:(d + 1) * N_HEADS], axis=-1)
            kt_s[d, rows, :] = jnp.concatenate(kts[d * N_HEADS:(d + 1) * N_HEADS], axis=-1)

    def prep_body(i, carry):
        for j in range(GDN_UNROLL):
            prepare(i * GDN_UNROLL + j)
        return carry

    lax.fori_loop(0, n // GDN_UNROLL, prep_body, 0)

    def chunk(ci, carry):
        units = [(d, p) for d in range(2) for p in range(N_PAIRS)]
        rows = []
        alast = []
        for d, cidx in ((0, ci), (1, n - 1 - ci)):
            rows.append(pl.ds(pl.multiple_of(cidx * C, C), C))
            alast.append(al_s[d, pl.ds(pl.multiple_of(cidx * 8, 8), 1), :])
        st = [st_s[d] for d in range(2)]
        lanes = [slice(p * LANE, (p + 1) * LANE) for p in range(N_PAIRS)]
        both = [jnp.dot(jnp.concatenate([ww_s[d, rows[d], lanes[p]], qin_s[d, rows[d], lanes[p]]], axis=0).astype(BF16),
                        _pair_blockdiag(st[d][:, lanes[p]].astype(BF16)), preferred_element_type=F32)
                for d, p in units]
        vnew = [uw_s[d, rows[d], lanes[p]] - both[i][:C] for i, (d, p) in enumerate(units)]
        upd = [jnp.dot(jnp.concatenate([at_s[d, rows[d], lanes[p]], kt_s[d, rows[d], lanes[p]]], axis=0).astype(BF16),
                       _pair_blockdiag(vnew[i].astype(BF16)), preferred_element_type=F32)
               for i, (d, p) in enumerate(units)]
        for d in range(2):
            idx = range(d * N_PAIRS, (d + 1) * N_PAIRS)
            of_s[d, rows[d], :] = jnp.concatenate([both[i][C:] + upd[i][:C] for i in idx], axis=-1)
            st_s[d] = st[d] * alast[d] + jnp.concatenate([upd[i][C:] for i in idx], axis=-1)
        return carry

    lax.fori_loop(0, n, chunk, 0)
    o_ref[...] = _head_norm_gate(of_s[0] + of_s[1], bd, gn_ref[...], u_ref[:, 3 * GROUP_W:4 * GROUP_W])
    for d in range(2):
        for h in range(N_HEADS):
            sfin_ref[d, h] = st_s[d][:, h * HEAD_W:(h + 1) * HEAD_W]


def _gdn_call(u_gd, cw, alog, dtb, gn, consts, s0, collect, l, T, nb, row0):
    tril, masks, expand, bd = consts
    tb = row0 // T
    has_s0 = s0 is not None
    in_specs = [
        pl.BlockSpec((T, GD_PAD), lambda b: (tb + b, 0)),
        pl.BlockSpec((None, 3, 3 * GROUP_W), lambda b: (l, 0, 0)),
        pl.BlockSpec((None, 1, LANE), lambda b: (l, 0, 0)),
        pl.BlockSpec((None, 1, LANE), lambda b: (l, 0, 0)),
        pl.BlockSpec((2, LANE, 2 * GROUP_W), lambda b: (0, 0, 0)),
        pl.BlockSpec((2, CHUNK, CHUNK), lambda b: (0, 0, 0)),
        pl.BlockSpec((2, 2, CHUNK, CHUNK), lambda b: (0, 0, 0, 0)),
        pl.BlockSpec((GROUP_W, GROUP_W), lambda b: (0, 0)),
        pl.BlockSpec((None, 1, GROUP_W), lambda b: (l, 0, 0)),
    ]
    args = [u_gd, cw, alog, dtb, expand, tril, masks, bd, gn]
    state_spec, state_shape, aliases = _state_io(in_specs, args, s0, collect, l, nb)
    seq = pltpu.VMEM((2, T, GROUP_W), F32)
    return pl.pallas_call(
        functools.partial(_gdn_kernel, T=T, has_s0=has_s0),
        grid=(nb,),
        in_specs=in_specs,
        out_specs=[pl.BlockSpec((T, GROUP_W), lambda b: (b, 0)), state_spec],
        out_shape=[jax.ShapeDtypeStruct((nb * T, GROUP_W), F32), state_shape],
        input_output_aliases=aliases,
        scratch_shapes=[pltpu.VMEM((T, GROUP_W), F32)] * 3 + [seq, seq, seq, seq, seq, seq, seq,
            pltpu.VMEM((2, T // CHUNK * 8, GROUP_W), F32), seq, pltpu.VMEM((2, HEAD_W, GROUP_W), F32)],
        compiler_params=_cparams(("parallel",)),
        name="gdn",
    )(*args)


def _hyfilt_kernel(z_ref, win_ref, fh_ref, fl_ref, w1_ref, b1_ref, fr_ref, w2_ref, b2_ref, w3_ref, o_ref, *, T):
    fr = fr_ref[...]
    h = jnp.sin(fr * (_dot3(z_ref[...], w1_ref[...]) + b1_ref[...]))
    h = jnp.sin(fr * (_dot3(h, w2_ref[...]) + b2_ref[...]))
    h = _dot3(h, w3_ref[...])
    win = win_ref[...]
    hf = h[:, 0:GROUP_W] * win
    hb = h[:, GROUP_W:2 * GROUP_W] * win
    row = lax.broadcasted_iota(jnp.int32, hb.shape, 0)
    hb = jnp.where(row == 0, 0.0, hb)
    taps = jnp.concatenate([hf + hb, hf - hb], axis=-1).astype(BF16)
    spec = (jnp.dot(fh_ref[...], taps, preferred_element_type=F32)
            + jnp.dot(fl_ref[...], taps, preferred_element_type=F32))
    o_ref[0:T, :] = spec[0:T, 0:GROUP_W]
    o_ref[T:2 * T, :] = spec[T:2 * T, GROUP_W:2 * GROUP_W]


def _hyfilt_call(T, zp, win, fh, fl, w1p, b1, freq, w2, b2, w3):
    c2 = lambda l: (0, 0)
    return pl.pallas_call(
        functools.partial(_hyfilt_kernel, T=T),
        grid=(DEPTH,),
        in_specs=[
            pl.BlockSpec((T, LANE), c2),
            pl.BlockSpec((T, GROUP_W), c2),
            pl.BlockSpec((2 * T, T), c2),
            pl.BlockSpec((2 * T, T), c2),
            pl.BlockSpec((None, LANE, HY_FH), lambda l: (l, 0, 0)),
            pl.BlockSpec((None, 1, HY_FH), lambda l: (l, 0, 0)),
            pl.BlockSpec((None, 1, HY_FH), lambda l: (l, 0, 0)),
            pl.BlockSpec((None, HY_FH, HY_FH), lambda l: (l, 0, 0)),
            pl.BlockSpec((None, 1, HY_FH), lambda l: (l, 0, 0)),
            pl.BlockSpec((None, HY_FH, 2 * GROUP_W), lambda l: (l, 0, 0)),
        ],
        out_specs=pl.BlockSpec((None, 2 * T, GROUP_W), lambda l: (l, 0, 0)),
        out_shape=jax.ShapeDtypeStruct((DEPTH, 2 * T, GROUP_W), F32),
        compiler_params=_cparams(("parallel",)),
        name="hyfilt",
    )(zp, win, fh, fl, w1p, b1, freq, w2, b2, w3)


def _hyena_kernel(u_ref, cw_ref, cb_ref, spec_ref, skip_ref, fh_ref, fl_ref, ih_ref, il_ref, o_ref, *, T):
    uc = _conv3(u_ref[...], cw_ref, T) + cb_ref[...]
    x0 = uc[:, 0:GROUP_W]
    z = uc[:, GROUP_W:2 * GROUP_W] * uc[:, 2 * GROUP_W:3 * GROUP_W]
    zb = z.astype(BF16)
    zs = (jnp.dot(fh_ref[...], zb, preferred_element_type=F32)
          + jnp.dot(fl_ref[...], zb, preferred_element_type=F32))
    ar, ai = zs[0:T], zs[T:2 * T]
    br, bi = spec_ref[0:T, :], spec_ref[T:2 * T, :]
    pb = jnp.concatenate([ar * br - ai * bi, ar * bi + ai * br], axis=0).astype(BF16)
    y = (jnp.dot(ih_ref[...], pb, preferred_element_type=F32)
         + jnp.dot(il_ref[...], pb, preferred_element_type=F32))
    o_ref[...] = x0 * (y + z * skip_ref[...])


def _hyena_call(u_hy, cw, cb, spec, skip, dft, l, T, nb, row0):
    fh, fl, ih, il = dft
    tb = row0 // T
    c2 = lambda b: (0, 0)
    return pl.pallas_call(
        functools.partial(_hyena_kernel, T=T),
        grid=(nb,),
        in_specs=[
            pl.BlockSpec((T, HY_COLS), lambda b: (tb + b, 0)),
            pl.BlockSpec((None, 3, HY_COLS), lambda b: (l, 0, 0)),
            pl.BlockSpec((None, 1, HY_COLS), lambda b: (l, 0, 0)),
            pl.BlockSpec((None, 2 * T, GROUP_W), lambda b: (l, 0, 0)),
            pl.BlockSpec((None, 1, GROUP_W), lambda b: (l, 0, 0)),
            pl.BlockSpec((2 * T, T), c2),
            pl.BlockSpec((2 * T, T), c2),
            pl.BlockSpec((T, 2 * T), c2),
            pl.BlockSpec((T, 2 * T), c2),
        ],
        out_specs=pl.BlockSpec((T, GROUP_W), lambda b: (b, 0)),
        out_shape=jax.ShapeDtypeStruct((nb * T, GROUP_W), F32),
        compiler_params=_cparams(("parallel",)),
        name="hyena",
    )(u_hy, cw, cb, spec, skip, fh, fl, ih, il)


def _rope(x, cosf, sinf):
    lane = lax.broadcasted_iota(jnp.int32, x.shape, 1)
    half = MLA_ROPE // 2
    partner = jnp.where(lane < MLA_NOPE + half, pltpu.roll(x, LANE - half, 1), pltpu.roll(x, half, 1))
    return x * cosf + partner * sinf


def _qk_norm(x, g):
    ms = jnp.sum(x * x, axis=-1, keepdims=True) * (1.0 / MLA_QK)
    return x * lax.rsqrt(ms + RMS_EPS) * g


def _mla_kernel(*refs, T, ctx):
    if ctx:
        (u_ref, qn_ref, wq_ref, kvn_ref, wkv_ref, qkn_ref, cos_ref, sin_ref, cckv_ref, ckr_ref, o_ref,
         q_s, k_s, v_s) = refs
    else:
        (u_ref, qn_ref, wq_ref, kvn_ref, wkv_ref, qkn_ref, _, _, o_ref, ckv_ref, kr_ref, q_s, k_s, v_s) = refs
    n_keys = k_s.shape[1]
    u = u_ref[...]
    cq = _rms(u[:, 0:MLA_Q_LORA], qn_ref[...])
    ckv = _rms(u[:, MLA_Q_LORA:MLA_Q_LORA + MLA_KV_LORA], kvn_ref[...])
    kr = u[:, MLA_Q_LORA + MLA_KV_LORA:MLA_Q_LORA + MLA_KV_LORA + MLA_ROPE]
    if not ctx:
        ckv_ref[...] = ckv
        kr_ref[...] = kr
    q_all = _bdot(cq, wq_ref[...])
    kv = _bdot(ckv, wkv_ref[...])
    gq = qkn_ref[0:1, :]
    gk = qkn_ref[1:2, :]
    if ctx:
        kvc = _bdot(cckv_ref[...], wkv_ref[...])
        krc = ckr_ref[...]
        cosf, sinf = cos_ref[...], sin_ref[...]
    q_scale = MLA_QK ** -0.5 * math.log2(math.e)
    kr_tile = jnp.concatenate([jnp.zeros((T, MLA_NOPE), F32), kr, jnp.zeros((T, LANE - MLA_QK), F32)], axis=-1)
    kr_rot = kr_tile * gk
    if ctx:
        kr_rot = _rope(kr_rot, cosf, sinf)
    nope_lane = lax.broadcasted_iota(jnp.int32, (T, LANE), 1) < MLA_NOPE
    for h in range(N_HEADS):
        qh = _qk_norm(q_all[:, h * LANE:(h + 1) * LANE], gq)
        k_nope = jnp.concatenate([kv[:, h * HEAD_W:(h + 1) * HEAD_W], jnp.zeros((T, LANE - MLA_NOPE), F32)], axis=-1)
        ms = jnp.sum(k_nope * k_nope + kr_tile * kr_tile, axis=-1, keepdims=True) * (1.0 / MLA_QK)
        kh = jnp.where(nope_lane, k_nope * gk, kr_rot) * lax.rsqrt(ms + RMS_EPS)
        if ctx:
            qh = _rope(qh, cosf, sinf)
            zc = jnp.zeros((n_keys - T, LANE - MLA_QK), F32)
            kc = _qk_norm(jnp.concatenate([kvc[:, h * HEAD_W:(h + 1) * HEAD_W], krc, zc], axis=-1), gk)
            k_s[h, T:n_keys, :] = kc.astype(BF16)
        q_s[:, h * LANE:(h + 1) * LANE] = (qh * q_scale).astype(BF16)
        k_s[h, 0:T, :] = kh.astype(BF16)
    for p in range(N_PAIRS):
        lanes = slice(GROUP_W + p * LANE, GROUP_W + (p + 1) * LANE)
        vp = kv[:, lanes]
        if ctx:
            vp = jnp.concatenate([vp, kvc[:, lanes]], axis=0)
        v_s[p] = _pair_blockdiag(vp.astype(BF16))

    def q_block(qb, carry):
        rows = pl.ds(pl.multiple_of(qb * ATT_QBLOCK, ATT_QBLOCK), ATT_QBLOCK)
        lane = lax.broadcasted_iota(jnp.int32, (ATT_QBLOCK, LANE), 1)
        for p in range(N_PAIRS):
            es, sums = [], []
            for h in (2 * p, 2 * p + 1):
                s = lax.dot_general(q_s[rows, h * LANE:(h + 1) * LANE], k_s[h], (((1,), (1,)), ((), ())),
                                    preferred_element_type=F32)
                e = jnp.exp2(s - jnp.max(s, axis=-1, keepdims=True))
                sums.append(jnp.sum(e, axis=-1, keepdims=True))
                es.append(e.astype(BF16))
            o = jnp.dot(jnp.concatenate(es, axis=-1), v_s[p], preferred_element_type=F32)
            o_ref[rows, p * LANE:(p + 1) * LANE] = o / jnp.where(lane < HEAD_W, sums[0], sums[1])
        return carry

    lax.fori_loop(0, T // ATT_QBLOCK, q_block, 0)


def _mla_call(u_mla, qn, wq, kvn, wkv, qkn, rope, cache, collect, l, T, nb, row0):
    tb = row0 // T
    ctx = cache is not None
    n_keys = T + (PAST_LEN if ctx else 0)
    c2 = lambda b: (0, 0)
    in_specs = [
        pl.BlockSpec((T, MLA_PAD), lambda b: (tb + b, 0)),
        pl.BlockSpec((None, 1, MLA_Q_LORA), lambda b: (l, 0, 0)),
        pl.BlockSpec((None, MLA_Q_LORA, N_HEADS * LANE), lambda b: (l, 0, 0)),
        pl.BlockSpec((None, 1, MLA_KV_LORA), lambda b: (l, 0, 0)),
        pl.BlockSpec((None, MLA_KV_LORA, 2 * GROUP_W), lambda b: (l, 0, 0)),
        pl.BlockSpec((None, 2, LANE), lambda b: (l, 0, 0)),
    ]
    args = [u_mla, qn, wq, kvn, wkv, qkn]
    out_specs = [pl.BlockSpec((T, GROUP_W), lambda b: (b, 0))]
    out_shape = [jax.ShapeDtypeStruct((nb * T, GROUP_W), F32)]
    if ctx:
        in_specs += [
            pl.BlockSpec((T, LANE), c2),
            pl.BlockSpec((T, LANE), c2),
            pl.BlockSpec((None, None, PAST_LEN, MLA_KV_LORA), lambda b: (b, l, 0, 0)),
            pl.BlockSpec((None, None, PAST_LEN, MLA_ROPE), lambda b: (b, l, 0, 0)),
        ]
        args += [rope[0], rope[1], cache[0], cache[1]]
        aliases = {}
    else:
        in_specs += [pl.BlockSpec(memory_space=pl.ANY)] * 2
        aliases = {len(args): 1, len(args) + 1: 2}
        args += list(collect)
        out_specs += [pl.BlockSpec((None, None, T, MLA_KV_LORA), lambda b: (b, l, 0, 0)),
                      pl.BlockSpec((None, None, T, MLA_ROPE), lambda b: (b, l, 0, 0))]
        out_shape += [jax.ShapeDtypeStruct(a.shape, F32) for a in collect]
    return pl.pallas_call(
        functools.partial(_mla_kernel, T=T, ctx=ctx),
        grid=(nb,),
        in_specs=in_specs,
        out_specs=out_specs,
        out_shape=out_shape,
        input_output_aliases=aliases,
        scratch_shapes=[pltpu.VMEM((T, N_HEADS * LANE), BF16), pltpu.VMEM((N_HEADS, n_keys, LANE), BF16),
                        pltpu.VMEM((N_PAIRS, 2 * n_keys, LANE), BF16)],
        compiler_params=_cparams(("parallel",)),
        name="mla",
    )(*args)


def _pad_cols(w, width):
    return jnp.pad(w, [(0, 0)] * (w.ndim - 1) + [(0, width - w.shape[-1])])


W_IN_PREP_COLS = 256


def _prep_w_in_kernel(wt_ref, o_hg, o_hy, o_mla, o_gd):
    start = 0
    for o_ref, cols in ((o_hg, HG_COLS), (o_hy, HY_COLS), (o_mla, MLA_COLS), (o_gd, GD_COLS)):
        width = o_ref.shape[-1]
        for c0 in range(0, width, W_IN_PREP_COLS):
            n_out = min(W_IN_PREP_COLS, width - c0)
            n_real = max(0, min(n_out, cols - c0))
            piece = wt_ref[start + c0:start + c0 + n_real, :]
            if n_real < n_out:
                piece = jnp.concatenate([piece, jnp.zeros((n_out - n_real, D_MODEL), F32)], axis=0)
            o_ref[:, c0:c0 + n_out] = piece.T.astype(BF16)
        start += cols


def _prep_w_in(w_in):
    widths = (HG_COLS, HY_COLS, MLA_PAD, GD_PAD)
    n_cols = w_in.shape[-1]
    return pl.pallas_call(
        _prep_w_in_kernel,
        grid=(DEPTH,),
        in_specs=[pl.BlockSpec((None, n_cols, D_MODEL), lambda l: (l, 0, 0))],
        out_specs=[pl.BlockSpec((None, D_MODEL, w), lambda l: (l, 0, 0)) for w in widths],
        out_shape=[jax.ShapeDtypeStruct((DEPTH, D_MODEL, w), BF16) for w in widths],
        compiler_params=_cparams(("parallel",)),
        name="w_in_prep",
    )(jnp.swapaxes(w_in, 1, 2))


def _prep_wq(w_q_up):
    w = w_q_up.reshape(DEPTH, MLA_Q_LORA, N_HEADS, MLA_QK)
    return _pad_cols(w, LANE).reshape(DEPTH, MLA_Q_LORA, N_HEADS * LANE).astype(BF16)


def _prep_wkv(w_kv_up):
    w = w_kv_up.reshape(DEPTH, MLA_KV_LORA, N_HEADS, 2, HEAD_W)
    return w.transpose(0, 1, 3, 2, 4).reshape(DEPTH, MLA_KV_LORA, 2 * GROUP_W).astype(BF16)


def _lower_bounds(hgrn_lb):
    lb = jnp.cumsum(jax.nn.softmax(hgrn_lb.astype(F32), axis=0), axis=0)
    return lb - lb[0]


def kernel(x_prompt, x_sample, cache_mla_ckv, cache_mla_krope, state_hgrn, state_gdn, c, c_ctx, w_ada, b_ada, norm_ffn, w_ffn_gu, w_ffn_down, norm_mix, w_in, w_out, hgrn_lb, hgrn_norm, hy_conv_w, hy_conv_b, hy_w1, hy_b1, hy_freq, hy_w2, hy_b2, hy_w3, hy_skip, mla_q_norm_a, mla_w_q_up, mla_kv_norm_a, mla_w_kv_up, mla_qk_norm, gdn_conv_w, gdn_a_log, gdn_dt_bias, gdn_norm):
    x = (x_prompt.reshape(N_PROMPT, D_MODEL), x_sample.reshape(N_SAMPLE, D_MODEL))

    cond8 = jnp.zeros((8, D_MODEL), F32).at[0].set(c_ctx).at[1:1 + DEC_BATCH].set(c)
    ada = _ada_call(cond8, w_ada, b_ada)

    w_in_parts = _prep_w_in(w_in)
    wq = _prep_wq(mla_w_q_up)
    wkv = _prep_wkv(mla_w_kv_up)
    qkn = _pad_cols(mla_qk_norm, LANE)
    lb_all = _lower_bounds(hgrn_lb)
    alog = _pad_cols(gdn_a_log.reshape(DEPTH, 1, 8), LANE)
    dtb = _pad_cols(gdn_dt_bias.reshape(DEPTH, 1, 8), LANE)
    gdn_gn = jnp.tile(gdn_norm, (1, N_HEADS)).reshape(DEPTH, 1, GROUP_W)
    w1p = jnp.pad(hy_w1, ((0, 0), (0, LANE - HY_EMB), (0, 0)))

    bd = jnp.asarray(_block_diag_ones(), BF16)
    hg_tril, hg_m = _hgrn_consts()
    hg_consts = (jnp.asarray(hg_tril, BF16), jnp.asarray(hg_m, F32), bd)
    gd_tril, gd_masks, gd_expand = _gdn_consts()
    gd_consts = (jnp.asarray(gd_tril, BF16), jnp.asarray(gd_masks, F32), jnp.asarray(gd_expand, BF16), bd)
    rope = tuple(jnp.asarray(a) for a in _rope_consts(DEC_SEQ))
    groups = ((SEQ, BATCH, 0), (DEC_SEQ, DEC_BATCH, N_PROMPT))
    dft = {}
    spec = {}
    for T, _, _ in groups:
        fwd, inv = _dft_consts(T)
        fh, fl = _np_split2(fwd)
        ih, il = _np_split2(inv)
        dft[T] = (fh, fl, ih, il)
        zp, win = _hyena_pos_consts(T)
        spec[T] = _hyfilt_call(T, jnp.asarray(zp), jnp.asarray(win), fh, fl, w1p,
                               hy_b1.reshape(DEPTH, 1, HY_FH), hy_freq.reshape(DEPTH, 1, HY_FH), hy_w2,
                               hy_b2.reshape(DEPTH, 1, HY_FH), hy_w3)

    new_ckv = jnp.zeros((BATCH, DEPTH, SEQ, MLA_KV_LORA), F32)
    new_kr = jnp.zeros((BATCH, DEPTH, SEQ, MLA_ROPE), F32)
    new_hg = jnp.zeros((BATCH, DEPTH, 2, N_HEADS, HEAD_W, HEAD_W), F32)
    new_gd = jnp.zeros((BATCH, DEPTH, 2, N_HEADS, HEAD_W, HEAD_W), F32)
    for l in range(DEPTH):
        x = _ffn_call(x, ada, norm_ffn, w_ffn_gu, w_ffn_down, l, 0)
        u_hg, u_hy, u_mla, u_gd = _inproj_call(x, ada, norm_mix, w_in_parts, l)
        outs = []
        for gi, (T, nb, row0) in enumerate(groups):
            latent = gi == 1
            o_hg, s_hg = _hgrn_call(u_hg, lb_all, hgrn_norm, hg_consts, state_hgrn if latent else None,
                                    None if latent else new_hg, l, T, nb, row0)
            o_hy = _hyena_call(u_hy, hy_conv_w, hy_conv_b.reshape(DEPTH, 1, HY_COLS), spec[T],
                               hy_skip.reshape(DEPTH, 1, GROUP_W), dft[T], l, T, nb, row0)
            mla = _mla_call(u_mla, mla_q_norm_a.reshape(DEPTH, 1, MLA_Q_LORA), wq,
                            mla_kv_norm_a.reshape(DEPTH, 1, MLA_KV_LORA), wkv, qkn,
                            rope if latent else None,
                            (cache_mla_ckv, cache_mla_krope) if latent else None,
                            None if latent else (new_ckv, new_kr), l, T, nb, row0)
            o_gd, s_gd = _gdn_call(u_gd, gdn_conv_w, alog, dtb, gdn_gn, gd_consts, state_gdn if latent else None,
                                   None if latent else new_gd, l, T, nb, row0)
            outs.append((o_hg, o_hy, mla[0], o_gd))
            if not latent:
                new_ckv, new_kr, new_hg, new_gd = mla[1], mla[2], s_hg, s_gd
        x = _mix_ffn_call(x, ada, norm_ffn, w_ffn_gu, w_ffn_down, w_out, outs[0], outs[1], l)

    y_prompt = x[:N_PROMPT].reshape(BATCH, SEQ, D_MODEL)
    y_sample = x[N_PROMPT:].reshape(DEC_BATCH, DEC_SEQ, D_MODEL)
    return (y_prompt, y_sample, new_ckv, new_kr, new_hg, new_gd)
```

```python
import functools
import math

import numpy as np
import jax
import jax.numpy as jnp
from jax import lax
from jax.experimental import pallas as pl
from jax.experimental.pallas import tpu as pltpu

F32 = jnp.float32
BF16 = jnp.bfloat16

D_MODEL = 1024
BATCH = 16
SEQ = 256
DEPTH = 4
DEC_BATCH = 2
DEC_SEQ = 1024
PAST_LEN = 256
GRID_W = 64
N_ADA = 9
D_FF = 2816
GROUP_W = 256
CHUNK = 64
RMS_EPS = 1e-6
N_HEADS = 4
HEAD_W = 64
HY_EMB = 33
HY_FH = 64
HY_TARGET = 1e-2
HY_FAST = 0.3
HY_SLOW = 1.5
MLA_NOPE = 64
MLA_ROPE = 32
MLA_QK = MLA_NOPE + MLA_ROPE
MLA_Q_LORA = 256
MLA_KV_LORA = 128
ROPE_BASE = 10000.0

HG_COLS = 5 * GROUP_W
HY_COLS = 3 * GROUP_W
MLA_COLS = MLA_Q_LORA + MLA_KV_LORA + MLA_ROPE
GD_COLS = 4 * GROUP_W + 16
MLA_PAD = 512
GD_PAD = 1152

N_PROMPT = BATCH * SEQ
N_SAMPLE = DEC_BATCH * DEC_SEQ
N_TOK = N_PROMPT + N_SAMPLE
LANE = 128
VMEM_LIMIT = 56 * 1024 * 1024
ROW_TILE = 1024
FF_TILE = 256
ADA_TILE = 1536
ATT_QBLOCK = 256


def _bdot(a, b):
    return jnp.dot(a.astype(BF16), b.astype(BF16), preferred_element_type=F32)


def _bdot_nt(a, b):
    return lax.dot_general(a.astype(BF16), b.astype(BF16), (((1,), (1,)), ((), ())),
                           preferred_element_type=F32)


def _bdot_tn(a, b):
    return lax.dot_general(a.astype(BF16), b.astype(BF16), (((0,), (0,)), ((), ())),
                           preferred_element_type=F32)


def _split2(x):
    hi = x.astype(BF16)
    lo = (x - hi.astype(F32)).astype(BF16)
    return hi, lo


def _split3(x):
    hi = x.astype(BF16)
    r = x - hi.astype(F32)
    mid = r.astype(BF16)
    lo = (r - mid.astype(F32)).astype(BF16)
    return hi, mid, lo


def _dot3(a, b):
    ah, al = _split2(a)
    bh, bl = _split2(b)
    return (jnp.dot(ah, bh, preferred_element_type=F32) + jnp.dot(ah, bl, preferred_element_type=F32)
            + jnp.dot(al, bh, preferred_element_type=F32))


def _sel_dot(c, x):
    h, m, l = _split3(x)
    return (jnp.dot(c, h, preferred_element_type=F32) + jnp.dot(c, m, preferred_element_type=F32)
            + jnp.dot(c, l, preferred_element_type=F32))


def _dot_sel(x, c):
    h, m, l = _split3(x)
    return (jnp.dot(h, c, preferred_element_type=F32) + jnp.dot(m, c, preferred_element_type=F32)
            + jnp.dot(l, c, preferred_element_type=F32))


def _sigmoid(x):
    return 1.0 / (1.0 + jnp.exp(-x))


def _silu(x):
    return x * _sigmoid(x)


def _rms(x, g):
    return x * lax.rsqrt(jnp.mean(x * x, axis=-1, keepdims=True) + RMS_EPS) * g


def _cparams(sem):
    return pltpu.CompilerParams(dimension_semantics=sem, vmem_limit_bytes=VMEM_LIMIT)


def _cond_of_tile(i):
    return jnp.maximum(i - (N_PROMPT // ROW_TILE - 1), 0)


def _ada_kernel(c_ref, w_ref, b_ref, o_ref):
    ch, cl = _split2(_silu(c_ref[...]))
    w = w_ref[...].astype(BF16)
    o_ref[...] = (jnp.dot(ch, w, preferred_element_type=F32) + jnp.dot(cl, w, preferred_element_type=F32)
                  + b_ref[...])


def _ada_call(cond8, w_ada, b_ada):
    n = N_ADA * D_MODEL
    out = pl.pallas_call(
        _ada_kernel,
        grid=(DEPTH, n // ADA_TILE),
        in_specs=[
            pl.BlockSpec((8, D_MODEL), lambda l, j: (0, 0)),
            pl.BlockSpec((None, D_MODEL, ADA_TILE), lambda l, j: (l, 0, j)),
            pl.BlockSpec((None, 1, ADA_TILE), lambda l, j: (l, 0, j)),
        ],
        out_specs=pl.BlockSpec((None, 8, ADA_TILE), lambda l, j: (l, 0, j)),
        out_shape=jax.ShapeDtypeStruct((DEPTH, 8, n), F32),
        compiler_params=_cparams(("parallel", "parallel")),
        name="ada",
    )(cond8, w_ada, b_ada.reshape(DEPTH, 1, n))
    return out.reshape(DEPTH, 8, N_ADA, D_MODEL)


FFN_SUBTILES = 2


def _ffn_kernel(*refs, sub, split):
    nx = 1 if split is None else 2
    x_refs = refs[:nx]
    ada_refs = refs[nx:nx + FFN_SUBTILES]
    g_ref, wg_ref, wu_ref, wd_ref, o_ref, h_scr = refs[nx + FFN_SUBTILES:]
    i = pl.program_id(0)
    f = pl.program_id(1)

    def prologue(x_ref):
        for r, ada_ref in enumerate(ada_refs):
            rows = slice(r * ROW_TILE, (r + 1) * ROW_TILE)
            y = _rms(x_ref[rows, :], g_ref[...])
            h = y * (1.0 + ada_ref[3 * sub + 1:3 * sub + 2, :]) + ada_ref[3 * sub:3 * sub + 1, :]
            h_scr[rows, :] = h.astype(BF16)
        o_ref[...] = jnp.zeros_like(o_ref)

    def epilogue(x_ref):
        for r, ada_ref in enumerate(ada_refs):
            rows = slice(r * ROW_TILE, (r + 1) * ROW_TILE)
            o_ref[rows, :] = x_ref[rows, :] + 0.5 * ada_ref[3 * sub + 2:3 * sub + 3, :] * o_ref[rows, :]

    def on(step, fn):
        if split is None:
            pl.when(f == step)(functools.partial(fn, x_refs[0]))
        else:
            pl.when((f == step) & (i < split))(functools.partial(fn, x_refs[0]))
            pl.when((f == step) & (i >= split))(functools.partial(fn, x_refs[1]))

    on(0, prologue)
    wg = wg_ref[...].astype(BF16)
    wu = wu_ref[...].astype(BF16)
    wd = wd_ref[...].astype(BF16)
    for r in range(FFN_SUBTILES):
        rows = slice(r * ROW_TILE, (r + 1) * ROW_TILE)
        h = h_scr[rows, :]
        gate = jnp.dot(h, wg, preferred_element_type=F32)
        up = jnp.dot(h, wu, preferred_element_type=F32)
        a = (_silu(gate) * up).astype(BF16)
        o_ref[rows, :] += jnp.dot(a, wd, preferred_element_type=F32)
    on(pl.num_programs(1) - 1, epilogue)


def _ffn_call(xs, ada, norm_ffn, w_gu, w_down, l, j):
    sub = 2 * j
    nf = D_FF // FF_TILE
    rows = FFN_SUBTILES * ROW_TILE

    def ada_spec(r):
        return pl.BlockSpec((None, None, N_ADA, D_MODEL),
                            lambda i, f: (l, _cond_of_tile(i * FFN_SUBTILES + r), 0, 0))

    if isinstance(xs, tuple):
        split = xs[0].shape[0] // rows
        n_tail = xs[1].shape[0] // rows
        x_specs = [pl.BlockSpec((rows, D_MODEL), lambda i, f: (jnp.minimum(i, split - 1), 0),
                                pipeline_mode=pl.Buffered(1)),
                   pl.BlockSpec((rows, D_MODEL), lambda i, f: (jnp.clip(i - split, 0, n_tail - 1), 0),
                                pipeline_mode=pl.Buffered(1))]
    else:
        split = None
        xs = (xs,)
        x_specs = [pl.BlockSpec((rows, D_MODEL), lambda i, f: (i, 0), pipeline_mode=pl.Buffered(1))]

    return pl.pallas_call(
        functools.partial(_ffn_kernel, sub=sub, split=split),
        grid=(N_TOK // rows, nf),
        in_specs=x_specs
        + [ada_spec(r) for r in range(FFN_SUBTILES)] + [
            pl.BlockSpec((None, None, 1, D_MODEL), lambda i, f: (l, j, 0, 0)),
            pl.BlockSpec((None, None, D_MODEL, FF_TILE), lambda i, f: (l, j, 0, f)),
            pl.BlockSpec((None, None, D_MODEL, FF_TILE), lambda i, f: (l, j, 0, nf + f)),
            pl.BlockSpec((None, None, FF_TILE, D_MODEL), lambda i, f: (l, j, f, 0)),
        ],
        out_specs=pl.BlockSpec((rows, D_MODEL), lambda i, f: (i, 0)),
        out_shape=jax.ShapeDtypeStruct((N_TOK, D_MODEL), F32),
        scratch_shapes=[pltpu.VMEM((rows, D_MODEL), BF16)],
        compiler_params=_cparams(("parallel", "arbitrary")),
        name="ffn",
    )(*xs, *([ada] * FFN_SUBTILES), norm_ffn.reshape(DEPTH, 2, 1, D_MODEL), w_gu, w_gu, w_down)


IN_TILE = 512


def _inproj_kernel(x_ref, ada_ref, g_ref, w1, w2, w3, w4, o1, o2, o3, o4):
    y = _rms(x_ref[...], g_ref[...])
    h = (y * (1.0 + ada_ref[4:5, :]) + ada_ref[3:4, :]).astype(BF16)
    for w, o in ((w1, o1), (w2, o2), (w3, o3), (w4, o4)):
        o[...] = jnp.dot(h, w[...], preferred_element_type=F32)


def _inproj_call(x, ada, norm_mix, ws, l):
    widths = (HG_COLS, HY_COLS, MLA_PAD, GD_PAD)
    per = ROW_TILE // IN_TILE
    return pl.pallas_call(
        _inproj_kernel,
        grid=(N_TOK // IN_TILE,),
        in_specs=[
            pl.BlockSpec((IN_TILE, D_MODEL), lambda i: (i, 0)),
            pl.BlockSpec((None, None, N_ADA, D_MODEL), lambda i: (l, _cond_of_tile(i // per), 0, 0)),
            pl.BlockSpec((None, 1, D_MODEL), lambda i: (l, 0, 0)),
        ] + [pl.BlockSpec((None, D_MODEL, w), lambda i: (l, 0, 0)) for w in widths],
        out_specs=[pl.BlockSpec((IN_TILE, w), lambda i: (i, 0)) for w in widths],
        out_shape=[jax.ShapeDtypeStruct((N_TOK, w), F32) for w in widths],
        compiler_params=_cparams(("parallel",)),
        name="inproj",
    )(x, ada, norm_mix.reshape(DEPTH, 1, D_MODEL), *ws)


OUT_TILE = 512


def _outproj_kernel(x_ref, ada_ref, w_ref, *refs):
    o_ref = refs[-1]
    i = pl.program_id(0)
    n_p = N_PROMPT // OUT_TILE

    def run(srcs):
        acc = jnp.zeros((OUT_TILE, D_MODEL), F32)
        for g, s in enumerate(srcs):
            acc += jnp.dot(s[...].astype(BF16), w_ref[g * GROUP_W:(g + 1) * GROUP_W, :],
                           preferred_element_type=F32)
        o_ref[...] = x_ref[...] + ada_ref[5:6, :] * acc

    @pl.when(i < n_p)
    def _():
        run(refs[0:4])

    @pl.when(i >= n_p)
    def _():
        run(refs[4:8])


def _outproj_call(x, ada, w_out_bf, o_p, o_s, l):
    per = ROW_TILE // OUT_TILE
    n_p = N_PROMPT // OUT_TILE
    n_s = N_SAMPLE // OUT_TILE
    return pl.pallas_call(
        _outproj_kernel,
        grid=(N_TOK // OUT_TILE,),
        in_specs=[
            pl.BlockSpec((OUT_TILE, D_MODEL), lambda i: (i, 0)),
            pl.BlockSpec((None, None, N_ADA, D_MODEL), lambda i: (l, _cond_of_tile(i // per), 0, 0)),
            pl.BlockSpec((None, D_MODEL, D_MODEL), lambda i: (l, 0, 0)),
        ] + [pl.BlockSpec((OUT_TILE, GROUP_W), lambda i: (jnp.minimum(i, n_p - 1), 0))] * 4
          + [pl.BlockSpec((OUT_TILE, GROUP_W), lambda i: (jnp.clip(i - n_p, 0, n_s - 1), 0))] * 4,
        out_specs=pl.BlockSpec((OUT_TILE, D_MODEL), lambda i: (i, 0)),
        out_shape=jax.ShapeDtypeStruct((N_TOK, D_MODEL), F32),
        compiler_params=_cparams(("parallel",)),
        name="outproj",
    )(x, ada, w_out_bf, *o_p, *o_s)


def _block_diag_ones():
    idx = np.arange(GROUP_W) // HEAD_W
    return (idx[:, None] == idx[None, :]).astype(np.float32)


def _hgrn_consts():
    C = CHUNK
    i = np.arange(C)[:, None]
    j = np.arange(C)[None, :]
    masks = []
    s = C // 2
    while s >= 1:
        up_i = (i // s) % 2 == 1
        up_j = (j // s) % 2 == 1
        masks.append(up_i & (~up_j) & (i // (2 * s) == j // (2 * s)))
        s //= 2
    masks.append(j <= i)
    fwd_m = np.stack([m.astype(np.float32) for m in masks])
    bwd_m = np.stack([m.astype(np.float32)[::-1, ::-1] for m in masks])
    tril = np.stack([(j <= i), (j >= i)]).astype(np.float32)
    return tril, np.tile(np.stack([fwd_m, bwd_m]), (1, 1, 1, 2))


def _gdn_consts():
    C = CHUNK
    i = np.arange(C)[:, None]
    t = np.arange(C)[None, :]
    tril = np.stack([(t <= i), (t >= i)]).astype(np.float32)
    masks = np.stack([np.stack([(t <= i), (t < i)]), np.stack([(t >= i), (t > i)])]).astype(np.float32)
    expand = np.zeros((2, LANE, 2 * GROUP_W), np.float32)
    for d in range(2):
        for h in range(N_HEADS):
            expand[d, d * N_HEADS + h, h * HEAD_W:(h + 1) * HEAD_W] = 1.0
            expand[d, 8 + d * N_HEADS + h, GROUP_W + h * HEAD_W:GROUP_W + (h + 1) * HEAD_W] = 1.0
    return tril, masks, expand


def _dft_consts(T):
    n2 = 4 * T
    k = np.arange(T, dtype=np.int64)[:, None]
    s = np.arange(T, dtype=np.int64)[None, :]
    ang = np.pi * (((2 * k + 1) * s) % n2).astype(np.float64) / (2 * T)
    fwd = np.concatenate([np.cos(ang), -np.sin(ang)], axis=0)
    inv = fwd.T / T
    return fwd.astype(np.float32), inv.astype(np.float32)


def _np_split2(x):
    hi = jnp.asarray(x, F32).astype(BF16)
    lo = (jnp.asarray(x, F32) - hi.astype(F32)).astype(BF16)
    return hi, lo


def _hyena_pos_consts(T):
    pos = np.arange(T, dtype=np.float32)
    t = pos / np.float32(T - 1)
    bands = np.linspace(1e-4, (HY_EMB - 1) // 2 - 1, (HY_EMB - 1) // 2, dtype=np.float32)
    ang = (np.float32(2.0 * math.pi / T) * pos[:, None]) * bands[None, :]
    z = np.concatenate([t[:, None], np.cos(ang), -np.sin(ang)], axis=-1).astype(np.float32)
    zp = np.zeros((T, LANE), np.float32)
    zp[:, :HY_EMB] = z
    max_decay = math.log(HY_TARGET) / HY_FAST
    min_decay = math.log(HY_TARGET) / HY_SLOW
    deltas = np.linspace(min_decay, max_decay, GROUP_W, dtype=np.float32)
    window = np.exp(-t[:, None] * np.abs(deltas)[None, :]).astype(np.float32)
    return zp, window


def _rope_consts(T):
    rows = T // GRID_W
    row = np.repeat(np.arange(rows, dtype=np.float32), GRID_W)
    col = (np.arange(T) % GRID_W).astype(np.float32)
    pairs = MLA_ROPE // 4
    inv = (np.float32(ROPE_BASE) ** (-np.arange(pairs, dtype=np.float32) / np.float32(pairs))).astype(np.float32)
    ang = np.concatenate([row[:, None] * inv, col[:, None] * inv], axis=-1).astype(np.float32)
    cos, sin = np.cos(ang), np.sin(ang)
    cosf = np.ones((T, LANE), np.float32)
    sinf = np.zeros((T, LANE), np.float32)
    half = MLA_ROPE // 2
    cosf[:, MLA_NOPE:MLA_NOPE + half] = cos
    cosf[:, MLA_NOPE + half:MLA_QK] = cos
    sinf[:, MLA_NOPE:MLA_NOPE + half] = -sin
    sinf[:, MLA_NOPE + half:MLA_QK] = sin
    return cosf, sinf


def _head_norm_gate(tot, bd, gn, gate):
    ms = _sel_dot_right(tot * tot, bd) * (1.0 / HEAD_W)
    return tot * lax.rsqrt(ms + RMS_EPS) * gn * _silu(gate)


def _sel_dot_right(x, c):
    h, l = _split2(x)
    return jnp.dot(h, c, preferred_element_type=F32) + jnp.dot(l, c, preferred_element_type=F32)


def _block_ref(b, two_s, r):
    C, W = b.shape
    if two_s % 8 == 0:
        b3 = b.reshape(C // two_s, two_s, W)
        return jnp.broadcast_to(b3[:, r:r + 1, :], b3.shape).reshape(C, W)
    pos = lax.broadcasted_iota(jnp.int32, b.shape, 0) % two_s
    out = b
    for p in range(two_s):
        if p != r:
            out = jnp.where(pos == p, pltpu.roll(b, (p - r) % C, 0), out)
    return out


N_PAIRS = N_HEADS // 2
HG_GROUP = 2
HG_DIRECT_MAX = 80.0


def _pair_blockdiag(x):
    lane = lax.broadcasted_iota(jnp.int32, x.shape, 1)
    zero = jnp.zeros_like(x)
    return jnp.concatenate([jnp.where(lane < HEAD_W, x, zero), jnp.where(lane >= HEAD_W, x, zero)], axis=0)


def _hgrn_kernel(*refs, T, has_s0):
    if has_s0:
        (u_ref, lb_ref, gn_ref, tril_ref, lmask_ref, bd_ref, s0_ref,
         o_ref, sfin_ref, oi_s, qin_s, up_s, dc_s, st_s) = refs
    else:
        (u_ref, lb_ref, gn_ref, tril_ref, lmask_ref, bd_ref, _,
         o_ref, sfin_ref, oi_s, qin_s, up_s, dc_s, st_s) = refs
    n = T // CHUNK
    C = CHUNK
    bd = bd_ref[...]
    n_lv = int(math.log2(C))

    log_lb = [jnp.log(lb_ref[d]) for d in range(2)]
    log_1mlb = [jnp.log(1.0 - lb_ref[d]) for d in range(2)]

    def gates(rows, d):
        z = u_ref[rows, (3 + d) * GROUP_W:(4 + d) * GROUP_W]
        t = jnp.exp(-jnp.abs(z))
        log_sig = jnp.minimum(z, 0.0) - jnp.log(1.0 + t)
        c = log_1mlb[d] + log_sig
        m = jnp.maximum(log_lb[d], c)
        lf = m + jnp.log(1.0 + jnp.exp(jnp.minimum(log_lb[d], c) - m))
        sig_neg = jnp.where(z > 0.0, t, 1.0) / (1.0 + t)
        return lf, (1.0 - lb_ref[d]) * sig_neg

    for d in range(2):
        if has_s0:
            st_s[d] = jnp.concatenate([s0_ref[d, h].T for h in range(N_HEADS)], axis=-1)
        else:
            st_s[d] = jnp.zeros((HEAD_W, GROUP_W), F32)

    def prepare(it, carry):
        units = [(c, d) for c in range(HG_GROUP) for d in range(2)]
        rows = [pl.ds(pl.multiple_of((it * HG_GROUP + c) * C, C), C) for c in range(HG_GROUP)]
        arow = [pl.ds(pl.multiple_of((it * HG_GROUP + c) * 8, 8), 8) for c in range(HG_GROUP)]
        q = [u_ref[rows[c], 0:GROUP_W] * (HEAD_W ** -0.5) for c in range(HG_GROUP)]
        v = [u_ref[rows[c], GROUP_W:2 * GROUP_W] for c in range(HG_GROUP)]
        vt = [[jnp.concatenate([v[c][:, h * HEAD_W:(h + 1) * HEAD_W].T for h in (2 * p, 2 * p + 1)],
                               axis=-1).astype(BF16) for p in range(N_PAIRS)]
              for c in range(HG_GROUP)]
        v_bd = [[_pair_blockdiag(v[c][:, p * LANE:(p + 1) * LANE].astype(BF16)) for p in range(N_PAIRS)]
                for c in range(HG_GROUP)]
        lf, ks = zip(*[gates(rows[c], d) for c, d in units])
        bs = []
        for i, (c, d) in enumerate(units):
            hi, lo = _split2(lf[i])
            tril = tril_ref[d]
            bs.append(jnp.dot(tril, hi, preferred_element_type=F32) + jnp.dot(tril, lo, preferred_element_type=F32))
        tot = [jnp.sum(x, axis=0, keepdims=True) for x in lf]
        mid_row = [C // 2 - 1 if d == 0 else C // 2 for c, d in units]
        spread = [jnp.maximum(jnp.abs(bs[i][0:1] - bs[i][r:r + 1]), jnp.abs(bs[i][C - 1:C] - bs[i][r:r + 1]))
                  for i, r in enumerate(mid_row)]
        widest = functools.reduce(jnp.maximum, spread)
        ko = [(ks[i] * jnp.exp(tot[i] - bs[i])).astype(BF16) for i in range(len(units))]
        up = [[jnp.dot(vt[c][p], _pair_blockdiag(ko[i][:, p * LANE:(p + 1) * LANE]), preferred_element_type=F32)
               for p in range(N_PAIRS)] for i, (c, d) in enumerate(units)]
        for i, (c, d) in enumerate(units):
            qin_s[d, rows[c], :] = q[c] * jnp.exp(bs[i])
            up_s[d, rows[c], :] = jnp.concatenate(up[i], axis=-1)
            dc_s[d, arow[c], :] = jnp.broadcast_to(jnp.exp(tot[i]), (8, GROUP_W))

        def masked_scores(qe, ke, lv):
            out = []
            for i, (c, d) in enumerate(units):
                per_pair = []
                for p in range(N_PAIRS):
                    sl = slice(p * LANE, (p + 1) * LANE)
                    prod = lax.dot_general(qe[i][:, sl], _pair_blockdiag(ke[i][:, sl]), (((1,), (1,)), ((), ())),
                                           preferred_element_type=F32)
                    per_pair.append(jnp.where(lmask_ref[d, lv] > 0.5, prod, 0.0))
                out.append(per_pair)
            return out

        def finish(sc):
            return [jnp.concatenate([jnp.dot(sc[i][p].astype(BF16), v_bd[c][p], preferred_element_type=F32)
                                     for p in range(N_PAIRS)], axis=-1) for i, (c, d) in enumerate(units)]

        def intra_direct():
            mid = [_block_ref(bs[i], C, C // 2 - 1 if d == 0 else C // 2) for i, (c, d) in enumerate(units)]
            qe = [(q[c] * jnp.exp(bs[i] - mid[i])).astype(BF16) for i, (c, d) in enumerate(units)]
            ke = [(ks[i] * jnp.exp(mid[i] - bs[i])).astype(BF16) for i in range(len(units))]
            return tuple(finish(masked_scores(qe, ke, n_lv)))

        def intra_split():
            sc = [[jnp.zeros((C, LANE), F32) for _ in range(N_PAIRS)] for _ in units]
            s = C // 2
            lv = 0
            while s >= 1:
                e = [jnp.exp(-jnp.abs(bs[i] - _block_ref(bs[i], 2 * s, s - 1 if d == 0 else s)))
                     for i, (c, d) in enumerate(units)]
                part = masked_scores([(q[c] * e[i]).astype(BF16) for i, (c, d) in enumerate(units)],
                                     [(ks[i] * e[i]).astype(BF16) for i in range(len(units))], lv)
                sc = [[sc[i][p] + part[i][p] for p in range(N_PAIRS)] for i in range(len(units))]
                s //= 2
                lv += 1
            fin = finish(sc)
            return tuple(fin[i] + _bdot(q[c] * ks[i], bd) * v[c] for i, (c, d) in enumerate(units))

        oi = lax.cond(jnp.max(widest) < HG_DIRECT_MAX, intra_direct, intra_split)
        for i, (c, d) in enumerate(units):
            oi_s[d, rows[c], :] = oi[i]
        return carry

    lax.fori_loop(0, n // HG_GROUP, prepare, 0)

    def chunk(ci, carry):
        rows = [pl.ds(pl.multiple_of(cidx * C, C), C) for cidx in (ci, n - 1 - ci)]
        decay = [dc_s[d, pl.ds(pl.multiple_of(cidx * 8, 8), 1), :] for d, cidx in ((0, ci), (1, n - 1 - ci))]
        st = [st_s[d] for d in range(2)]
        o_inter = [[lax.dot_general(qin_s[d, rows[d], p * LANE:(p + 1) * LANE].astype(BF16),
                                    _pair_blockdiag(st[d][:, p * LANE:(p + 1) * LANE].astype(BF16)),
                                    (((1,), (1,)), ((), ())), preferred_element_type=F32)
                    for p in range(N_PAIRS)] for d in range(2)]
        for d in range(2):
            st_s[d] = st[d] * decay[d] + up_s[d, rows[d], :]
            oi_s[d, rows[d], :] = oi_s[d, rows[d], :] + jnp.concatenate(o_inter[d], axis=-1)
        return carry

    lax.fori_loop(0, n, chunk, 0, unroll=2)
    o_ref[...] = _head_norm_gate(oi_s[0] + oi_s[1], bd, gn_ref[...], u_ref[:, 2 * GROUP_W:3 * GROUP_W])
    for d in range(2):
        for h in range(N_HEADS):
            sfin_ref[d, h] = st_s[d][:, h * HEAD_W:(h + 1) * HEAD_W].T


def _state_io(in_specs, args, s0, collect, l, nb):
    state_block = (None, None, 2, N_HEADS, HEAD_W, HEAD_W)
    if s0 is not None:
        in_specs.append(pl.BlockSpec(state_block, lambda b: (b, l, 0, 0, 0, 0)))
        args.append(s0)
        return (pl.BlockSpec(state_block[1:], lambda b: (b, 0, 0, 0, 0)),
                jax.ShapeDtypeStruct((nb, 2, N_HEADS, HEAD_W, HEAD_W), F32), {})
    in_specs.append(pl.BlockSpec(memory_space=pl.ANY))
    args.append(collect)
    return (pl.BlockSpec(state_block, lambda b: (b, l, 0, 0, 0, 0)),
            jax.ShapeDtypeStruct(collect.shape, F32), {len(args) - 1: 1})


def _hgrn_call(u_hg, lb_l, gn, consts, s0, collect, l, T, nb, row0):
    tril, lmask, bd = consts
    tb = row0 // T
    has_s0 = s0 is not None
    in_specs = [
        pl.BlockSpec((T, HG_COLS), lambda b: (tb + b, 0)),
        pl.BlockSpec((None, 2, 1, GROUP_W), lambda b: (l, 0, 0, 0)),
        pl.BlockSpec((None, 1, GROUP_W), lambda b: (l, 0, 0)),
        pl.BlockSpec((2, CHUNK, CHUNK), lambda b: (0, 0, 0)),
        pl.BlockSpec((2, 7, CHUNK, LANE), lambda b: (0, 0, 0, 0)),
        pl.BlockSpec((GROUP_W, GROUP_W), lambda b: (0, 0)),
    ]
    args = [u_hg, lb_l.reshape(DEPTH, 2, 1, GROUP_W), gn.reshape(DEPTH, 1, GROUP_W), tril, lmask, bd]
    state_spec, state_shape, aliases = _state_io(in_specs, args, s0, collect, l, nb)
    seq = pltpu.VMEM((2, T, GROUP_W), F32)
    return pl.pallas_call(
        functools.partial(_hgrn_kernel, T=T, has_s0=has_s0),
        grid=(nb,),
        in_specs=in_specs,
        out_specs=[pl.BlockSpec((T, GROUP_W), lambda b: (b, 0)), state_spec],
        out_shape=[jax.ShapeDtypeStruct((nb * T, GROUP_W), F32), state_shape],
        input_output_aliases=aliases,
        scratch_shapes=[seq, seq, seq,
                        pltpu.VMEM((2, T // CHUNK * 8, GROUP_W), F32), pltpu.VMEM((2, HEAD_W, GROUP_W), F32)],
        compiler_params=_cparams(("parallel",)),
        name="hgrn",
    )(*args)


def _shift_rows(x, T):
    row = lax.broadcasted_iota(jnp.int32, x.shape, 0)
    prev = jnp.where(row == 0, 0.0, pltpu.roll(x, 1, 0))
    nxt = jnp.where(row == T - 1, 0.0, pltpu.roll(x, T - 1, 0))
    return prev, nxt


def _conv3(x, w_ref, T):
    prev, nxt = _shift_rows(x, T)
    return prev * w_ref[0:1, :] + x * w_ref[1:2, :] + nxt * w_ref[2:3, :]


GDN_UNROLL = 2


def _solve_unit_lower(systems):
    c2 = 2 * CHUNK
    slabs = [jnp.concatenate([nmat, nmat, rhs], axis=-1) for rhs, nmat in systems]
    steps = int(math.log2(CHUNK))
    for step in range(steps):
        last = step == steps - 1
        nxt = []
        for slab in slabs:
            hi = slab.astype(BF16)
            lo = (slab - hi.astype(F32)).astype(BF16)
            lhs = jnp.concatenate([hi[:, :c2], lo[:, :CHUNK]], axis=-1)
            first = c2 if last else 0
            rhs3 = jnp.concatenate([hi[:, first:], lo[:, first:], hi[:, first:]], axis=0)
            prod = jnp.dot(lhs, rhs3, preferred_element_type=F32)
            if last:
                nxt.append(slab[:, c2:] + prod)
            else:
                nxt.append(jnp.concatenate([prod[:, :c2], slab[:, c2:] + prod[:, c2:]], axis=-1))
        slabs = nxt
    return slabs


def _gdn_kernel(*refs, T, has_s0):
    if has_s0:
        (u_ref, cw_ref, alog_ref, dtb_ref, exp_ref, tril_ref, mask_ref, bd_ref, gn_ref, s0_ref,
         o_ref, sfin_ref, q_s, k_s, v_s, la_s, be_s, uw_s, ww_s, at_s, qin_s, kt_s, al_s, of_s, st_s) = refs
    else:
        (u_ref, cw_ref, alog_ref, dtb_ref, exp_ref, tril_ref, mask_ref, bd_ref, gn_ref, _,
         o_ref, sfin_ref, q_s, k_s, v_s, la_s, be_s, uw_s, ww_s, at_s, qin_s, kt_s, al_s, of_s, st_s) = refs
    n = T // CHUNK
    C = CHUNK
    bd = bd_ref[...]

    qkv = _silu(_conv3(u_ref[:, 0:3 * GROUP_W], cw_ref, T))
    q = qkv[:, 0:GROUP_W]
    k = qkv[:, GROUP_W:2 * GROUP_W]
    q_s[...] = q * lax.rsqrt(_sel_dot_right(q * q, bd) + 1e-6) * (HEAD_W ** -0.5)
    k_s[...] = k * lax.rsqrt(_sel_dot_right(k * k, bd) + 1e-6)
    v_s[...] = qkv[:, 2 * GROUP_W:3 * GROUP_W]

    ab = u_ref[:, 4 * GROUP_W:4 * GROUP_W + LANE]
    xa = ab + dtb_ref[...]
    softplus = jnp.maximum(xa, 0.0) + jnp.log(1.0 + jnp.exp(-jnp.abs(xa)))
    log_a = -jnp.exp(alog_ref[...]) * softplus
    lane = lax.broadcasted_iota(jnp.int32, ab.shape, 1)
    narrow = jnp.where(lane < 8, log_a, _sigmoid(ab))
    for d in range(2):
        wide = _dot_sel(narrow, exp_ref[d])
        la_s[d] = wide[:, 0:GROUP_W]
        be_s[d] = wide[:, GROUP_W:2 * GROUP_W]
        if has_s0:
            st_s[d] = jnp.concatenate([s0_ref[d, h] for h in range(N_HEADS)], axis=-1)
        else:
            st_s[d] = jnp.zeros((HEAD_W, GROUP_W), F32)

    def prepare(cidx):
        r0 = pl.multiple_of(cidx * C, C)
        rows = pl.ds(r0, C)
        arow = pl.ds(pl.multiple_of(cidx * 8, 8), 8)
        q = q_s[rows, :]
        k = k_s[rows, :]
        v = v_s[rows, :]
        systems = []
        attns = []
        kts = []
        for d in range(2):
            incl = mask_ref[d, 0] > 0.5
            strict = mask_ref[d, 1]
            la = la_s[d, rows, :]
            be = be_s[d, rows, :]
            gx = _sel_dot(tril_ref[d], la)
            gtot = jnp.sum(la, axis=0, keepdims=True)
            eg = jnp.exp(gx)
            kout = k * jnp.exp(gtot - gx)
            qin_s[d, rows, :] = q * eg
            al_s[d, arow, :] = jnp.broadcast_to(jnp.exp(gtot), (8, GROUP_W))
            kb = k * be
            vb = v * be
            kbg = kb * eg
            for h in range(N_HEADS):
                sl = slice(h * HEAD_W, (h + 1) * HEAD_W)
                gh = gx[:, sl]
                dmat = gh - gh.T
                dec = jnp.where(incl, jnp.exp(jnp.where(incl, dmat, 0.0)), 0.0)
                qk = _bdot_nt(jnp.concatenate([kb[:, sl], q[:, sl]], axis=0), k[:, sl])
                nmat = -(qk[:C] * dec * strict)
                systems.append((jnp.concatenate([vb[:, sl], kbg[:, sl]], axis=-1), nmat))
                attns.append(qk[C:] * dec)
                kts.append(kout[:, sl].T)
        sols = _solve_unit_lower(systems)
        for d in range(2):
            mine = sols[d * N_HEADS:(d + 1) * N_HEADS]
            uw_s[d, rows, :] = jnp.concatenate([x[:, :HEAD_W] for x in mine], axis=-1)
            ww_s[d, rows, :] = jnp.concatenate([x[:, HEAD_W:] for x in mine], axis=-1)
            at_s[d, rows, :] = jnp.concatenate(attns[d * N_HEADS:(d + 1) * N_HEADS], axis=-1)
            kt_s[d, rows, :] = jnp.concatenate(kts[d * N_HEADS:(d + 1) * N_HEADS], axis=-1)

    def prep_body(i, carry):
        for j in range(GDN_UNROLL):
            prepare(i * GDN_UNROLL + j)
        return carry

    lax.fori_loop(0, n // GDN_UNROLL, prep_body, 0)

    def chunk(ci, carry):
        units = [(d, p) for d in range(2) for p in range(N_PAIRS)]
        rows = []
        alast = []
        for d, cidx in ((0, ci), (1, n - 1 - ci)):
            rows.append(pl.ds(pl.multiple_of(cidx * C, C), C))
            alast.append(al_s[d, pl.ds(pl.multiple_of(cidx * 8, 8), 1), :])
        st = [st_s[d] for d in range(2)]
        lanes = [slice(p * LANE, (p + 1) * LANE) for p in range(N_PAIRS)]
        both = [jnp.dot(jnp.concatenate([ww_s[d, rows[d], lanes[p]], qin_s[d, rows[d], lanes[p]]], axis=0).astype(BF16),
                        _pair_blockdiag(st[d][:, lanes[p]].astype(BF16)), preferred_element_type=F32)
                for d, p in units]
        vnew = [uw_s[d, rows[d], lanes[p]] - both[i][:C] for i, (d, p) in enumerate(units)]
        upd = [jnp.dot(jnp.concatenate([at_s[d, rows[d], lanes[p]], kt_s[d, rows[d], lanes[p]]], axis=0).astype(BF16),
                       _pair_blockdiag(vnew[i].astype(BF16)), preferred_element_type=F32)
               for i, (d, p) in enumerate(units)]
        for d in range(2):
            idx = range(d * N_PAIRS, (d + 1) * N_PAIRS)
            of_s[d, rows[d], :] = jnp.concatenate([both[i][C:] + upd[i][:C] for i in idx], axis=-1)
            st_s[d] = st[d] * alast[d] + jnp.concatenate([upd[i][C:] for i in idx], axis=-1)
        return carry

    lax.fori_loop(0, n, chunk, 0)
    o_ref[...] = _head_norm_gate(of_s[0] + of_s[1], bd, gn_ref[...], u_ref[:, 3 * GROUP_W:4 * GROUP_W])
    for d in range(2):
        for h in range(N_HEADS):
            sfin_ref[d, h] = st_s[d][:, h * HEAD_W:(h + 1) * HEAD_W]


def _gdn_call(u_gd, cw, alog, dtb, gn, consts, s0, collect, l, T, nb, row0):
    tril, masks, expand, bd = consts
    tb = row0 // T
    has_s0 = s0 is not None
    in_specs = [
        pl.BlockSpec((T, GD_PAD), lambda b: (tb + b, 0)),
        pl.BlockSpec((None, 3, 3 * GROUP_W), lambda b: (l, 0, 0)),
        pl.BlockSpec((None, 1, LANE), lambda b: (l, 0, 0)),
        pl.BlockSpec((None, 1, LANE), lambda b: (l, 0, 0)),
        pl.BlockSpec((2, LANE, 2 * GROUP_W), lambda b: (0, 0, 0)),
        pl.BlockSpec((2, CHUNK, CHUNK), lambda b: (0, 0, 0)),
        pl.BlockSpec((2, 2, CHUNK, CHUNK), lambda b: (0, 0, 0, 0)),
        pl.BlockSpec((GROUP_W, GROUP_W), lambda b: (0, 0)),
        pl.BlockSpec((None, 1, GROUP_W), lambda b: (l, 0, 0)),
    ]
    args = [u_gd, cw, alog, dtb, expand, tril, masks, bd, gn]
    state_spec, state_shape, aliases = _state_io(in_specs, args, s0, collect, l, nb)
    seq = pltpu.VMEM((2, T, GROUP_W), F32)
    return pl.pallas_call(
        functools.partial(_gdn_kernel, T=T, has_s0=has_s0),
        grid=(nb,),
        in_specs=in_specs,
        out_specs=[pl.BlockSpec((T, GROUP_W), lambda b: (b, 0)), state_spec],
        out_shape=[jax.ShapeDtypeStruct((nb * T, GROUP_W), F32), state_shape],
        input_output_aliases=aliases,
        scratch_shapes=[pltpu.VMEM((T, GROUP_W), F32)] * 3 + [seq, seq, seq, seq, seq, seq, seq,
            pltpu.VMEM((2, T // CHUNK * 8, GROUP_W), F32), seq, pltpu.VMEM((2, HEAD_W, GROUP_W), F32)],
        compiler_params=_cparams(("parallel",)),
        name="gdn",
    )(*args)


def _hyfilt_kernel(z_ref, win_ref, fh_ref, fl_ref, w1_ref, b1_ref, fr_ref, w2_ref, b2_ref, w3_ref, o_ref, *, T):
    fr = fr_ref[...]
    h = jnp.sin(fr * (_dot3(z_ref[...], w1_ref[...]) + b1_ref[...]))
    h = jnp.sin(fr * (_dot3(h, w2_ref[...]) + b2_ref[...]))
    h = _dot3(h, w3_ref[...])
    win = win_ref[...]
    hf = h[:, 0:GROUP_W] * win
    hb = h[:, GROUP_W:2 * GROUP_W] * win
    row = lax.broadcasted_iota(jnp.int32, hb.shape, 0)
    hb = jnp.where(row == 0, 0.0, hb)
    for rows, taps in ((slice(0, T), (hf + hb).astype(BF16)), (slice(T, 2 * T), (hf - hb).astype(BF16))):
        o_ref[rows, :] = (jnp.dot(fh_ref[rows, :], taps, preferred_element_type=F32)
                          + jnp.dot(fl_ref[rows, :], taps, preferred_element_type=F32))


def _hyfilt_call(T, zp, win, fh, fl, w1p, b1, freq, w2, b2, w3):
    c2 = lambda l: (0, 0)
    return pl.pallas_call(
        functools.partial(_hyfilt_kernel, T=T),
        grid=(DEPTH,),
        in_specs=[
            pl.BlockSpec((T, LANE), c2),
            pl.BlockSpec((T, GROUP_W), c2),
            pl.BlockSpec((2 * T, T), c2),
            pl.BlockSpec((2 * T, T), c2),
            pl.BlockSpec((None, LANE, HY_FH), lambda l: (l, 0, 0)),
            pl.BlockSpec((None, 1, HY_FH), lambda l: (l, 0, 0)),
            pl.BlockSpec((None, 1, HY_FH), lambda l: (l, 0, 0)),
            pl.BlockSpec((None, HY_FH, HY_FH), lambda l: (l, 0, 0)),
            pl.BlockSpec((None, 1, HY_FH), lambda l: (l, 0, 0)),
            pl.BlockSpec((None, HY_FH, 2 * GROUP_W), lambda l: (l, 0, 0)),
        ],
        out_specs=pl.BlockSpec((None, 2 * T, GROUP_W), lambda l: (l, 0, 0)),
        out_shape=jax.ShapeDtypeStruct((DEPTH, 2 * T, GROUP_W), F32),
        compiler_params=_cparams(("parallel",)),
        name="hyfilt",
    )(zp, win, fh, fl, w1p, b1, freq, w2, b2, w3)


def _hyena_kernel(u_ref, cw_ref, cb_ref, spec_ref, skip_ref, fh_ref, fl_ref, ih_ref, il_ref, o_ref, *, T):
    uc = _conv3(u_ref[...], cw_ref, T) + cb_ref[...]
    x0 = uc[:, 0:GROUP_W]
    z = uc[:, GROUP_W:2 * GROUP_W] * uc[:, 2 * GROUP_W:3 * GROUP_W]
    zb = z.astype(BF16)
    zs = (jnp.dot(fh_ref[...], zb, preferred_element_type=F32)
          + jnp.dot(fl_ref[...], zb, preferred_element_type=F32))
    ar, ai = zs[0:T], zs[T:2 * T]
    br, bi = spec_ref[0:T, :], spec_ref[T:2 * T, :]
    pb = jnp.concatenate([ar * br - ai * bi, ar * bi + ai * br], axis=0).astype(BF16)
    y = (jnp.dot(ih_ref[...], pb, preferred_element_type=F32)
         + jnp.dot(il_ref[...], pb, preferred_element_type=F32))
    o_ref[...] = x0 * (y + z * skip_ref[...])


def _hyena_call(u_hy, cw, cb, spec, skip, dft, l, T, nb, row0):
    fh, fl, ih, il = dft
    tb = row0 // T
    c2 = lambda b: (0, 0)
    return pl.pallas_call(
        functools.partial(_hyena_kernel, T=T),
        grid=(nb,),
        in_specs=[
            pl.BlockSpec((T, HY_COLS), lambda b: (tb + b, 0)),
            pl.BlockSpec((None, 3, HY_COLS), lambda b: (l, 0, 0)),
            pl.BlockSpec((None, 1, HY_COLS), lambda b: (l, 0, 0)),
            pl.BlockSpec((None, 2 * T, GROUP_W), lambda b: (l, 0, 0)),
            pl.BlockSpec((None, 1, GROUP_W), lambda b: (l, 0, 0)),
            pl.BlockSpec((2 * T, T), c2),
            pl.BlockSpec((2 * T, T), c2),
            pl.BlockSpec((T, 2 * T), c2),
            pl.BlockSpec((T, 2 * T), c2),
        ],
        out_specs=pl.BlockSpec((T, GROUP_W), lambda b: (b, 0)),
        out_shape=jax.ShapeDtypeStruct((nb * T, GROUP_W), F32),
        compiler_params=_cparams(("parallel",)),
        name="hyena",
    )(u_hy, cw, cb, spec, skip, fh, fl, ih, il)


def _rope(x, cosf, sinf):
    lane = lax.broadcasted_iota(jnp.int32, x.shape, 1)
    half = MLA_ROPE // 2
    partner = jnp.where(lane < MLA_NOPE + half, pltpu.roll(x, LANE - half, 1), pltpu.roll(x, half, 1))
    return x * cosf + partner * sinf


def _qk_norm(x, g):
    ms = jnp.sum(x * x, axis=-1, keepdims=True) * (1.0 / MLA_QK)
    return x * lax.rsqrt(ms + RMS_EPS) * g


def _mla_kernel(*refs, T, ctx):
    if ctx:
        (u_ref, qn_ref, wq_ref, kvn_ref, wkv_ref, qkn_ref, cos_ref, sin_ref, cckv_ref, ckr_ref, o_ref,
         q_s, k_s, v_s) = refs
    else:
        (u_ref, qn_ref, wq_ref, kvn_ref, wkv_ref, qkn_ref, _, _, o_ref, ckv_ref, kr_ref, q_s, k_s, v_s) = refs
    n_keys = k_s.shape[1]
    u = u_ref[...]
    cq = _rms(u[:, 0:MLA_Q_LORA], qn_ref[...])
    ckv = _rms(u[:, MLA_Q_LORA:MLA_Q_LORA + MLA_KV_LORA], kvn_ref[...])
    kr = u[:, MLA_Q_LORA + MLA_KV_LORA:MLA_Q_LORA + MLA_KV_LORA + MLA_ROPE]
    if not ctx:
        ckv_ref[...] = ckv
        kr_ref[...] = kr
    q_all = _bdot(cq, wq_ref[...])
    kv = _bdot(ckv, wkv_ref[...])
    gq = qkn_ref[0:1, :]
    gk = qkn_ref[1:2, :]
    if ctx:
        kvc = _bdot(cckv_ref[...], wkv_ref[...])
        krc = ckr_ref[...]
        cosf, sinf = cos_ref[...], sin_ref[...]
    q_scale = MLA_QK ** -0.5 * math.log2(math.e)
    kr_tile = jnp.concatenate([jnp.zeros((T, MLA_NOPE), F32), kr, jnp.zeros((T, LANE - MLA_QK), F32)], axis=-1)
    kr_rot = kr_tile * gk
    if ctx:
        kr_rot = _rope(kr_rot, cosf, sinf)
    nope_lane = lax.broadcasted_iota(jnp.int32, (T, LANE), 1) < MLA_NOPE
    for h in range(N_HEADS):
        qh = _qk_norm(q_all[:, h * LANE:(h + 1) * LANE], gq)
        k_nope = jnp.concatenate([kv[:, h * HEAD_W:(h + 1) * HEAD_W], jnp.zeros((T, LANE - MLA_NOPE), F32)], axis=-1)
        ms = jnp.sum(k_nope * k_nope + kr_tile * kr_tile, axis=-1, keepdims=True) * (1.0 / MLA_QK)
        kh = jnp.where(nope_lane, k_nope * gk, kr_rot) * lax.rsqrt(ms + RMS_EPS)
        if ctx:
            qh = _rope(qh, cosf, sinf)
            zc = jnp.zeros((n_keys - T, LANE - MLA_QK), F32)
            kc = _qk_norm(jnp.concatenate([kvc[:, h * HEAD_W:(h + 1) * HEAD_W], krc, zc], axis=-1), gk)
            k_s[h, T:n_keys, :] = kc.astype(BF16)
        q_s[:, h * LANE:(h + 1) * LANE] = (qh * q_scale).astype(BF16)
        k_s[h, 0:T, :] = kh.astype(BF16)
    for p in range(N_PAIRS):
        lanes = slice(GROUP_W + p * LANE, GROUP_W + (p + 1) * LANE)
        vp = kv[:, lanes]
        if ctx:
            vp = jnp.concatenate([vp, kvc[:, lanes]], axis=0)
        v_s[p] = _pair_blockdiag(vp.astype(BF16))

    def q_block(qb, carry):
        rows = pl.ds(pl.multiple_of(qb * ATT_QBLOCK, ATT_QBLOCK), ATT_QBLOCK)
        lane = lax.broadcasted_iota(jnp.int32, (ATT_QBLOCK, LANE), 1)
        for p in range(N_PAIRS):
            es, sums = [], []
            for h in (2 * p, 2 * p + 1):
                s = lax.dot_general(q_s[rows, h * LANE:(h + 1) * LANE], k_s[h], (((1,), (1,)), ((), ())),
                                    preferred_element_type=F32)
                e = jnp.exp2(s - jnp.max(s, axis=-1, keepdims=True))
                sums.append(jnp.sum(e, axis=-1, keepdims=True))
                es.append(e.astype(BF16))
            o = jnp.dot(jnp.concatenate(es, axis=-1), v_s[p], preferred_element_type=F32)
            o_ref[rows, p * LANE:(p + 1) * LANE] = o / jnp.where(lane < HEAD_W, sums[0], sums[1])
        return carry

    lax.fori_loop(0, T // ATT_QBLOCK, q_block, 0)


def _mla_call(u_mla, qn, wq, kvn, wkv, qkn, rope, cache, collect, l, T, nb, row0):
    tb = row0 // T
    ctx = cache is not None
    n_keys = T + (PAST_LEN if ctx else 0)
    c2 = lambda b: (0, 0)
    in_specs = [
        pl.BlockSpec((T, MLA_PAD), lambda b: (tb + b, 0)),
        pl.BlockSpec((None, 1, MLA_Q_LORA), lambda b: (l, 0, 0)),
        pl.BlockSpec((None, MLA_Q_LORA, N_HEADS * LANE), lambda b: (l, 0, 0)),
        pl.BlockSpec((None, 1, MLA_KV_LORA), lambda b: (l, 0, 0)),
        pl.BlockSpec((None, MLA_KV_LORA, 2 * GROUP_W), lambda b: (l, 0, 0)),
        pl.BlockSpec((None, 2, LANE), lambda b: (l, 0, 0)),
    ]
    args = [u_mla, qn, wq, kvn, wkv, qkn]
    out_specs = [pl.BlockSpec((T, GROUP_W), lambda b: (b, 0))]
    out_shape = [jax.ShapeDtypeStruct((nb * T, GROUP_W), F32)]
    if ctx:
        in_specs += [
            pl.BlockSpec((T, LANE), c2),
            pl.BlockSpec((T, LANE), c2),
            pl.BlockSpec((None, None, PAST_LEN, MLA_KV_LORA), lambda b: (b, l, 0, 0)),
            pl.BlockSpec((None, None, PAST_LEN, MLA_ROPE), lambda b: (b, l, 0, 0)),
        ]
        args += [rope[0], rope[1], cache[0], cache[1]]
        aliases = {}
    else:
        in_specs += [pl.BlockSpec(memory_space=pl.ANY)] * 2
        aliases = {len(args): 1, len(args) + 1: 2}
        args += list(collect)
        out_specs += [pl.BlockSpec((None, None, T, MLA_KV_LORA), lambda b: (b, l, 0, 0)),
                      pl.BlockSpec((None, None, T, MLA_ROPE), lambda b: (b, l, 0, 0))]
        out_shape += [jax.ShapeDtypeStruct(a.shape, F32) for a in collect]
    return pl.pallas_call(
        functools.partial(_mla_kernel, T=T, ctx=ctx),
        grid=(nb,),
        in_specs=in_specs,
        out_specs=out_specs,
        out_shape=out_shape,
        input_output_aliases=aliases,
        scratch_shapes=[pltpu.VMEM((T, N_HEADS * LANE), BF16), pltpu.VMEM((N_HEADS, n_keys, LANE), BF16),
                        pltpu.VMEM((N_PAIRS, 2 * n_keys, LANE), BF16)],
        compiler_params=_cparams(("parallel",)),
        name="mla",
    )(*args)


def _pad_cols(w, width):
    return jnp.pad(w, [(0, 0)] * (w.ndim - 1) + [(0, width - w.shape[-1])])


W_IN_PREP_COLS = 256


def _prep_w_in_kernel(wt_ref, o_hg, o_hy, o_mla, o_gd):
    start = 0
    for o_ref, cols in ((o_hg, HG_COLS), (o_hy, HY_COLS), (o_mla, MLA_COLS), (o_gd, GD_COLS)):
        width = o_ref.shape[-1]
        for c0 in range(0, width, W_IN_PREP_COLS):
            n_out = min(W_IN_PREP_COLS, width - c0)
            n_real = max(0, min(n_out, cols - c0))
            piece = wt_ref[start + c0:start + c0 + n_real, :]
            if n_real < n_out:
                piece = jnp.concatenate([piece, jnp.zeros((n_out - n_real, D_MODEL), F32)], axis=0)
            o_ref[:, c0:c0 + n_out] = piece.T.astype(BF16)
        start += cols


def _prep_w_in(w_in):
    widths = (HG_COLS, HY_COLS, MLA_PAD, GD_PAD)
    n_cols = w_in.shape[-1]
    return pl.pallas_call(
        _prep_w_in_kernel,
        grid=(DEPTH,),
        in_specs=[pl.BlockSpec((None, n_cols, D_MODEL), lambda l: (l, 0, 0))],
        out_specs=[pl.BlockSpec((None, D_MODEL, w), lambda l: (l, 0, 0)) for w in widths],
        out_shape=[jax.ShapeDtypeStruct((DEPTH, D_MODEL, w), BF16) for w in widths],
        compiler_params=_cparams(("parallel",)),
        name="w_in_prep",
    )(jnp.swapaxes(w_in, 1, 2))


def _prep_wq(w_q_up):
    w = w_q_up.reshape(DEPTH, MLA_Q_LORA, N_HEADS, MLA_QK)
    return _pad_cols(w, LANE).reshape(DEPTH, MLA_Q_LORA, N_HEADS * LANE).astype(BF16)


def _prep_wkv(w_kv_up):
    w = w_kv_up.reshape(DEPTH, MLA_KV_LORA, N_HEADS, 2, HEAD_W)
    return w.transpose(0, 1, 3, 2, 4).reshape(DEPTH, MLA_KV_LORA, 2 * GROUP_W).astype(BF16)


def _lower_bounds(hgrn_lb):
    lb = jnp.cumsum(jax.nn.softmax(hgrn_lb.astype(F32), axis=0), axis=0)
    return lb - lb[0]


def kernel(x_prompt, x_sample, cache_mla_ckv, cache_mla_krope, state_hgrn, state_gdn, c, c_ctx, w_ada, b_ada, norm_ffn, w_ffn_gu, w_ffn_down, norm_mix, w_in, w_out, hgrn_lb, hgrn_norm, hy_conv_w, hy_conv_b, hy_w1, hy_b1, hy_freq, hy_w2, hy_b2, hy_w3, hy_skip, mla_q_norm_a, mla_w_q_up, mla_kv_norm_a, mla_w_kv_up, mla_qk_norm, gdn_conv_w, gdn_a_log, gdn_dt_bias, gdn_norm):
    x = (x_prompt.reshape(N_PROMPT, D_MODEL), x_sample.reshape(N_SAMPLE, D_MODEL))

    cond8 = jnp.zeros((8, D_MODEL), F32).at[0].set(c_ctx).at[1:1 + DEC_BATCH].set(c)
    ada = _ada_call(cond8, w_ada, b_ada)

    w_in_parts = _prep_w_in(w_in)
    w_out_bf = w_out.astype(BF16)
    wq = _prep_wq(mla_w_q_up)
    wkv = _prep_wkv(mla_w_kv_up)
    qkn = _pad_cols(mla_qk_norm, LANE)
    lb_all = _lower_bounds(hgrn_lb)
    alog = _pad_cols(gdn_a_log.reshape(DEPTH, 1, 8), LANE)
    dtb = _pad_cols(gdn_dt_bias.reshape(DEPTH, 1, 8), LANE)
    gdn_gn = jnp.tile(gdn_norm, (1, N_HEADS)).reshape(DEPTH, 1, GROUP_W)
    w1p = jnp.pad(hy_w1, ((0, 0), (0, LANE - HY_EMB), (0, 0)))

    bd = jnp.asarray(_block_diag_ones(), BF16)
    hg_tril, hg_m = _hgrn_consts()
    hg_consts = (jnp.asarray(hg_tril, BF16), jnp.asarray(hg_m, F32), bd)
    gd_tril, gd_masks, gd_expand = _gdn_consts()
    gd_consts = (jnp.asarray(gd_tril, BF16), jnp.asarray(gd_masks, F32), jnp.asarray(gd_expand, BF16), bd)
    rope = tuple(jnp.asarray(a) for a in _rope_consts(DEC_SEQ))
    groups = ((SEQ, BATCH, 0), (DEC_SEQ, DEC_BATCH, N_PROMPT))
    dft = {}
    spec = {}
    for T, _, _ in groups:
        fwd, inv = _dft_consts(T)
        fh, fl = _np_split2(fwd)
        ih, il = _np_split2(inv)
        dft[T] = (fh, fl, ih, il)
        zp, win = _hyena_pos_consts(T)
        spec[T] = _hyfilt_call(T, jnp.asarray(zp), jnp.asarray(win), fh, fl, w1p,
                               hy_b1.reshape(DEPTH, 1, HY_FH), hy_freq.reshape(DEPTH, 1, HY_FH), hy_w2,
                               hy_b2.reshape(DEPTH, 1, HY_FH), hy_w3)

    new_ckv = jnp.zeros((BATCH, DEPTH, SEQ, MLA_KV_LORA), F32)
    new_kr = jnp.zeros((BATCH, DEPTH, SEQ, MLA_ROPE), F32)
    new_hg = jnp.zeros((BATCH, DEPTH, 2, N_HEADS, HEAD_W, HEAD_W), F32)
    new_gd = jnp.zeros((BATCH, DEPTH, 2, N_HEADS, HEAD_W, HEAD_W), F32)
    for l in range(DEPTH):
        x = _ffn_call(x, ada, norm_ffn, w_ffn_gu, w_ffn_down, l, 0)
        u_hg, u_hy, u_mla, u_gd = _inproj_call(x, ada, norm_mix, w_in_parts, l)
        outs = []
        for gi, (T, nb, row0) in enumerate(groups):
            latent = gi == 1
            o_hg, s_hg = _hgrn_call(u_hg, lb_all, hgrn_norm, hg_consts, state_hgrn if latent else None,
                                    None if latent else new_hg, l, T, nb, row0)
            o_hy = _hyena_call(u_hy, hy_conv_w, hy_conv_b.reshape(DEPTH, 1, HY_COLS), spec[T],
                               hy_skip.reshape(DEPTH, 1, GROUP_W), dft[T], l, T, nb, row0)
            mla = _mla_call(u_mla, mla_q_norm_a.reshape(DEPTH, 1, MLA_Q_LORA), wq,
                            mla_kv_norm_a.reshape(DEPTH, 1, MLA_KV_LORA), wkv, qkn,
                            rope if latent else None,
                            (cache_mla_ckv, cache_mla_krope) if latent else None,
                            None if latent else (new_ckv, new_kr), l, T, nb, row0)
            o_gd, s_gd = _gdn_call(u_gd, gdn_conv_w, alog, dtb, gdn_gn, gd_consts, state_gdn if latent else None,
                                   None if latent else new_gd, l, T, nb, row0)
            outs.append((o_hg, o_hy, mla[0], o_gd))
            if not latent:
                new_ckv, new_kr, new_hg, new_gd = mla[1], mla[2], s_hg, s_gd
        x = _outproj_call(x, ada, w_out_bf, outs[0], outs[1], l)
        x = _ffn_call(x, ada, norm_ffn, w_ffn_gu, w_ffn_down, l, 1)

    y_prompt = x[:N_PROMPT].reshape(BATCH, SEQ, D_MODEL)
    y_sample = x[N_PROMPT:].reshape(DEC_BATCH, DEC_SEQ, D_MODEL)
    return (y_prompt, y_sample, new_ckv, new_kr, new_hg, new_gd)
```

```python
import functools
import math

import numpy as np
import jax
import jax.numpy as jnp
from jax import lax
from jax.experimental import pallas as pl
from jax.experimental.pallas import tpu as pltpu

F32 = jnp.float32
BF16 = jnp.bfloat16

D_MODEL = 1024
BATCH = 16
SEQ = 256
DEPTH = 4
DEC_BATCH = 2
DEC_SEQ = 1024
PAST_LEN = 256
GRID_W = 64
N_ADA = 9
D_FF = 2816
GROUP_W = 256
CHUNK = 64
RMS_EPS = 1e-6
N_HEADS = 4
HEAD_W = 64
HY_EMB = 33
HY_FH = 64
HY_TARGET = 1e-2
HY_FAST = 0.3
HY_SLOW = 1.5
MLA_NOPE = 64
MLA_ROPE = 32
MLA_QK = MLA_NOPE + MLA_ROPE
MLA_Q_LORA = 256
MLA_KV_LORA = 128
ROPE_BASE = 10000.0

HG_COLS = 5 * GROUP_W
HY_COLS = 3 * GROUP_W
MLA_COLS = MLA_Q_LORA + MLA_KV_LORA + MLA_ROPE
GD_COLS = 4 * GROUP_W + 16
MLA_PAD = 512
GD_PAD = 1152

N_PROMPT = BATCH * SEQ
N_SAMPLE = DEC_BATCH * DEC_SEQ
N_TOK = N_PROMPT + N_SAMPLE
LANE = 128
VMEM_LIMIT = 56 * 1024 * 1024
ROW_TILE = 1024
FF_TILE = 256
ADA_TILE = 1536
ATT_QBLOCK = 256


def _bdot(a, b):
    return jnp.dot(a.astype(BF16), b.astype(BF16), preferred_element_type=F32)


def _bdot_nt(a, b):
    return lax.dot_general(a.astype(BF16), b.astype(BF16), (((1,), (1,)), ((), ())),
                           preferred_element_type=F32)


def _split2(x):
    hi = x.astype(BF16)
    lo = (x - hi.astype(F32)).astype(BF16)
    return hi, lo


def _split3(x):
    hi = x.astype(BF16)
    r = x - hi.astype(F32)
    mid = r.astype(BF16)
    lo = (r - mid.astype(F32)).astype(BF16)
    return hi, mid, lo


def _dot3(a, b):
    ah, al = _split2(a)
    bh, bl = _split2(b)
    return (jnp.dot(ah, bh, preferred_element_type=F32) + jnp.dot(ah, bl, preferred_element_type=F32)
            + jnp.dot(al, bh, preferred_element_type=F32))


def _sel_dot(c, x):
    h, m, l = _split3(x)
    return (jnp.dot(c, h, preferred_element_type=F32) + jnp.dot(c, m, preferred_element_type=F32)
            + jnp.dot(c, l, preferred_element_type=F32))


def _dot_sel(x, c):
    h, m, l = _split3(x)
    return (jnp.dot(h, c, preferred_element_type=F32) + jnp.dot(m, c, preferred_element_type=F32)
            + jnp.dot(l, c, preferred_element_type=F32))


def _sigmoid(x):
    return 1.0 / (1.0 + jnp.exp(-x))


def _silu(x):
    return x * _sigmoid(x)


def _rms(x, g):
    return x * lax.rsqrt(jnp.mean(x * x, axis=-1, keepdims=True) + RMS_EPS) * g


def _cparams(sem):
    return pltpu.CompilerParams(dimension_semantics=sem, vmem_limit_bytes=VMEM_LIMIT)


def _cond_of_tile(i):
    return jnp.maximum(i - (N_PROMPT // ROW_TILE - 1), 0)


def _ada_kernel(c_ref, w_ref, b_ref, o_ref):
    ch, cl = _split2(_silu(c_ref[...]))
    w = w_ref[...].astype(BF16)
    o_ref[...] = (jnp.dot(ch, w, preferred_element_type=F32) + jnp.dot(cl, w, preferred_element_type=F32)
                  + b_ref[...])


def _ada_call(cond8, w_ada, b_ada):
    n = N_ADA * D_MODEL
    out = pl.pallas_call(
        _ada_kernel,
        grid=(DEPTH, n // ADA_TILE),
        in_specs=[
            pl.BlockSpec((8, D_MODEL), lambda l, j: (0, 0)),
            pl.BlockSpec((None, D_MODEL, ADA_TILE), lambda l, j: (l, 0, j)),
            pl.BlockSpec((None, 1, ADA_TILE), lambda l, j: (l, 0, j)),
        ],
        out_specs=pl.BlockSpec((None, 8, ADA_TILE), lambda l, j: (l, 0, j)),
        out_shape=jax.ShapeDtypeStruct((DEPTH, 8, n), F32),
        compiler_params=_cparams(("parallel", "parallel")),
        name="ada",
    )(cond8, w_ada, b_ada.reshape(DEPTH, 1, n))
    return out.reshape(DEPTH, 8, N_ADA, D_MODEL)


FFN_SUBTILES = 2


def _ffn_kernel(*refs, sub, split):
    nx = 1 if split is None else 2
    x_refs = refs[:nx]
    ada_refs = refs[nx:nx + FFN_SUBTILES]
    g_ref, wg_ref, wu_ref, wd_ref, o_ref, h_scr = refs[nx + FFN_SUBTILES:]
    i = pl.program_id(0)
    f = pl.program_id(1)

    def prologue(x_ref):
        for r, ada_ref in enumerate(ada_refs):
            rows = slice(r * ROW_TILE, (r + 1) * ROW_TILE)
            y = _rms(x_ref[rows, :], g_ref[...])
            h = y * (1.0 + ada_ref[3 * sub + 1:3 * sub + 2, :]) + ada_ref[3 * sub:3 * sub + 1, :]
            h_scr[rows, :] = h.astype(BF16)
        o_ref[...] = jnp.zeros_like(o_ref)

    def epilogue(x_ref):
        for r, ada_ref in enumerate(ada_refs):
            rows = slice(r * ROW_TILE, (r + 1) * ROW_TILE)
            o_ref[rows, :] = x_ref[rows, :] + 0.5 * ada_ref[3 * sub + 2:3 * sub + 3, :] * o_ref[rows, :]

    def on(step, fn):
        if split is None:
            pl.when(f == step)(functools.partial(fn, x_refs[0]))
        else:
            pl.when((f == step) & (i < split))(functools.partial(fn, x_refs[0]))
            pl.when((f == step) & (i >= split))(functools.partial(fn, x_refs[1]))

    on(0, prologue)
    wg = wg_ref[...].astype(BF16)
    wu = wu_ref[...].astype(BF16)
    wd = wd_ref[...].astype(BF16)
    for r in range(FFN_SUBTILES):
        rows = slice(r * ROW_TILE, (r + 1) * ROW_TILE)
        h = h_scr[rows, :]
        gate = jnp.dot(h, wg, preferred_element_type=F32)
        up = jnp.dot(h, wu, preferred_element_type=F32)
        a = (_silu(gate) * up).astype(BF16)
        o_ref[rows, :] += jnp.dot(a, wd, preferred_element_type=F32)
    on(pl.num_programs(1) - 1, epilogue)


def _ffn_call(xs, ada, norm_ffn, w_gu, w_down, l, j):
    sub = 2 * j
    nf = D_FF // FF_TILE
    rows = FFN_SUBTILES * ROW_TILE

    def ada_spec(r):
        return pl.BlockSpec((None, None, N_ADA, D_MODEL),
                            lambda i, f: (l, _cond_of_tile(i * FFN_SUBTILES + r), 0, 0))

    if isinstance(xs, tuple):
        split = xs[0].shape[0] // rows
        n_tail = xs[1].shape[0] // rows
        x_specs = [pl.BlockSpec((rows, D_MODEL), lambda i, f: (jnp.minimum(i, split - 1), 0),
                                pipeline_mode=pl.Buffered(1)),
                   pl.BlockSpec((rows, D_MODEL), lambda i, f: (jnp.clip(i - split, 0, n_tail - 1), 0),
                                pipeline_mode=pl.Buffered(1))]
    else:
        split = None
        xs = (xs,)
        x_specs = [pl.BlockSpec((rows, D_MODEL), lambda i, f: (i, 0), pipeline_mode=pl.Buffered(1))]

    return pl.pallas_call(
        functools.partial(_ffn_kernel, sub=sub, split=split),
        grid=(N_TOK // rows, nf),
        in_specs=x_specs
        + [ada_spec(r) for r in range(FFN_SUBTILES)] + [
            pl.BlockSpec((None, None, 1, D_MODEL), lambda i, f: (l, j, 0, 0)),
            pl.BlockSpec((None, None, D_MODEL, FF_TILE), lambda i, f: (l, j, 0, f)),
            pl.BlockSpec((None, None, D_MODEL, FF_TILE), lambda i, f: (l, j, 0, nf + f)),
            pl.BlockSpec((None, None, FF_TILE, D_MODEL), lambda i, f: (l, j, f, 0)),
        ],
        out_specs=pl.BlockSpec((rows, D_MODEL), lambda i, f: (i, 0)),
        out_shape=jax.ShapeDtypeStruct((N_TOK, D_MODEL), F32),
        scratch_shapes=[pltpu.VMEM((rows, D_MODEL), BF16)],
        compiler_params=_cparams(("parallel", "arbitrary")),
        name="ffn",
    )(*xs, *([ada] * FFN_SUBTILES), norm_ffn.reshape(DEPTH, 2, 1, D_MODEL), w_gu, w_gu, w_down)


IN_TILE = 512


def _inproj_kernel(x_ref, ada_ref, g_ref, w1, w2, w3, w4, o1, o2, o3, o4):
    y = _rms(x_ref[...], g_ref[...])
    h = (y * (1.0 + ada_ref[4:5, :]) + ada_ref[3:4, :]).astype(BF16)
    for w, o in ((w1, o1), (w2, o2), (w3, o3), (w4, o4)):
        o[...] = jnp.dot(h, w[...], preferred_element_type=F32)


def _inproj_call(x, ada, norm_mix, ws, l):
    widths = (HG_COLS, HY_COLS, MLA_PAD, GD_PAD)
    per = ROW_TILE // IN_TILE
    return pl.pallas_call(
        _inproj_kernel,
        grid=(N_TOK // IN_TILE,),
        in_specs=[
            pl.BlockSpec((IN_TILE, D_MODEL), lambda i: (i, 0)),
            pl.BlockSpec((None, None, N_ADA, D_MODEL), lambda i: (l, _cond_of_tile(i // per), 0, 0)),
            pl.BlockSpec((None, 1, D_MODEL), lambda i: (l, 0, 0)),
        ] + [pl.BlockSpec((None, D_MODEL, w), lambda i: (l, 0, 0)) for w in widths],
        out_specs=[pl.BlockSpec((IN_TILE, w), lambda i: (i, 0)) for w in widths],
        out_shape=[jax.ShapeDtypeStruct((N_TOK, w), F32) for w in widths],
        compiler_params=_cparams(("parallel",)),
        name="inproj",
    )(x, ada, norm_mix.reshape(DEPTH, 1, D_MODEL), *ws)


OUT_TILE = 512


def _outproj_kernel(x_ref, ada_ref, w_ref, *refs):
    o_ref = refs[-1]
    i = pl.program_id(0)
    n_p = N_PROMPT // OUT_TILE

    def run(srcs):
        acc = jnp.zeros((OUT_TILE, D_MODEL), F32)
        for g, s in enumerate(srcs):
            acc += jnp.dot(s[...].astype(BF16), w_ref[g * GROUP_W:(g + 1) * GROUP_W, :],
                           preferred_element_type=F32)
        o_ref[...] = x_ref[...] + ada_ref[5:6, :] * acc

    @pl.when(i < n_p)
    def _():
        run(refs[0:4])

    @pl.when(i >= n_p)
    def _():
        run(refs[4:8])


def _outproj_call(x, ada, w_out_bf, o_p, o_s, l):
    per = ROW_TILE // OUT_TILE
    n_p = N_PROMPT // OUT_TILE
    n_s = N_SAMPLE // OUT_TILE
    return pl.pallas_call(
        _outproj_kernel,
        grid=(N_TOK // OUT_TILE,),
        in_specs=[
            pl.BlockSpec((OUT_TILE, D_MODEL), lambda i: (i, 0)),
            pl.BlockSpec((None, None, N_ADA, D_MODEL), lambda i: (l, _cond_of_tile(i // per), 0, 0)),
            pl.BlockSpec((None, D_MODEL, D_MODEL), lambda i: (l, 0, 0)),
        ] + [pl.BlockSpec((OUT_TILE, GROUP_W), lambda i: (jnp.minimum(i, n_p - 1), 0))] * 4
          + [pl.BlockSpec((OUT_TILE, GROUP_W), lambda i: (jnp.clip(i - n_p, 0, n_s - 1), 0))] * 4,
        out_specs=pl.BlockSpec((OUT_TILE, D_MODEL), lambda i: (i, 0)),
        out_shape=jax.ShapeDtypeStruct((N_TOK, D_MODEL), F32),
        compiler_params=_cparams(("parallel",)),
        name="outproj",
    )(x, ada, w_out_bf, *o_p, *o_s)


def _block_diag_ones():
    idx = np.arange(GROUP_W) // HEAD_W
    return (idx[:, None] == idx[None, :]).astype(np.float32)


def _hgrn_consts():
    C = CHUNK
    i = np.arange(C)[:, None]
    j = np.arange(C)[None, :]
    masks = []
    s = C // 2
    while s >= 1:
        up_i = (i // s) % 2 == 1
        up_j = (j // s) % 2 == 1
        masks.append(up_i & (~up_j) & (i // (2 * s) == j // (2 * s)))
        s //= 2
    masks.append(j <= i)
    fwd_m = np.stack([m.astype(np.float32) for m in masks])
    bwd_m = np.stack([m.astype(np.float32)[::-1, ::-1] for m in masks])
    tril = np.stack([(j <= i), (j >= i)]).astype(np.float32)
    return tril, np.tile(np.stack([fwd_m, bwd_m]), (1, 1, 1, 2))


def _gdn_consts():
    C = CHUNK
    i = np.arange(C)[:, None]
    t = np.arange(C)[None, :]
    tril = np.stack([(t <= i), (t >= i)]).astype(np.float32)
    masks = np.stack([np.stack([(t <= i), (t < i)]), np.stack([(t >= i), (t > i)])]).astype(np.float32)
    expand = np.zeros((2, LANE, 2 * GROUP_W), np.float32)
    for d in range(2):
        for h in range(N_HEADS):
            expand[d, d * N_HEADS + h, h * HEAD_W:(h + 1) * HEAD_W] = 1.0
            expand[d, 8 + d * N_HEADS + h, GROUP_W + h * HEAD_W:GROUP_W + (h + 1) * HEAD_W] = 1.0
    return tril, masks, expand


def _dft_consts(T):
    n2 = 4 * T
    k = np.arange(T, dtype=np.int64)[:, None]
    s = np.arange(T, dtype=np.int64)[None, :]
    ang = np.pi * (((2 * k + 1) * s) % n2).astype(np.float64) / (2 * T)
    fwd = np.concatenate([np.cos(ang), -np.sin(ang)], axis=0)
    inv = fwd.T / T
    return fwd.astype(np.float32), inv.astype(np.float32)


def _np_split2(x):
    hi = jnp.asarray(x, F32).astype(BF16)
    lo = (jnp.asarray(x, F32) - hi.astype(F32)).astype(BF16)
    return hi, lo


def _hyena_pos_consts(T):
    pos = np.arange(T, dtype=np.float32)
    t = pos / np.float32(T - 1)
    bands = np.linspace(1e-4, (HY_EMB - 1) // 2 - 1, (HY_EMB - 1) // 2, dtype=np.float32)
    ang = (np.float32(2.0 * math.pi / T) * pos[:, None]) * bands[None, :]
    z = np.concatenate([t[:, None], np.cos(ang), -np.sin(ang)], axis=-1).astype(np.float32)
    zp = np.zeros((T, LANE), np.float32)
    zp[:, :HY_EMB] = z
    max_decay = math.log(HY_TARGET) / HY_FAST
    min_decay = math.log(HY_TARGET) / HY_SLOW
    deltas = np.linspace(min_decay, max_decay, GROUP_W, dtype=np.float32)
    window = np.exp(-t[:, None] * np.abs(deltas)[None, :]).astype(np.float32)
    return zp, window


def _rope_consts(T):
    rows = T // GRID_W
    row = np.repeat(np.arange(rows, dtype=np.float32), GRID_W)
    col = (np.arange(T) % GRID_W).astype(np.float32)
    pairs = MLA_ROPE // 4
    inv = (np.float32(ROPE_BASE) ** (-np.arange(pairs, dtype=np.float32) / np.float32(pairs))).astype(np.float32)
    ang = np.concatenate([row[:, None] * inv, col[:, None] * inv], axis=-1).astype(np.float32)
    cos, sin = np.cos(ang), np.sin(ang)
    cosf = np.ones((T, LANE), np.float32)
    sinf = np.zeros((T, LANE), np.float32)
    half = MLA_ROPE // 2
    cosf[:, MLA_NOPE:MLA_NOPE + half] = cos
    cosf[:, MLA_NOPE + half:MLA_QK] = cos
    sinf[:, MLA_NOPE:MLA_NOPE + half] = -sin
    sinf[:, MLA_NOPE + half:MLA_QK] = sin
    return cosf, sinf


def _head_norm_gate(tot, bd, gn, gate):
    ms = _sel_dot_right(tot * tot, bd) * (1.0 / HEAD_W)
    return tot * lax.rsqrt(ms + RMS_EPS) * gn * _silu(gate)


def _sel_dot_right(x, c):
    h, l = _split2(x)
    return jnp.dot(h, c, preferred_element_type=F32) + jnp.dot(l, c, preferred_element_type=F32)


def _block_ref(b, two_s, r):
    C, W = b.shape
    if two_s % 8 == 0:
        b3 = b.reshape(C // two_s, two_s, W)
        return jnp.broadcast_to(b3[:, r:r + 1, :], b3.shape).reshape(C, W)
    pos = lax.broadcasted_iota(jnp.int32, b.shape, 0) % two_s
    out = b
    for p in range(two_s):
        if p != r:
            out = jnp.where(pos == p, pltpu.roll(b, (p - r) % C, 0), out)
    return out


N_PAIRS = N_HEADS // 2
HG_GROUP = 4
HG_DIRECT_MAX = 80.0


def _pair_blockdiag(x):
    lane = lax.broadcasted_iota(jnp.int32, x.shape, 1)
    zero = jnp.zeros_like(x)
    return jnp.concatenate([jnp.where(lane < HEAD_W, x, zero), jnp.where(lane >= HEAD_W, x, zero)], axis=0)


def _hgrn_kernel(*refs, T, has_s0):
    if has_s0:
        (u_ref, lb_ref, gn_ref, tril_ref, lmask_ref, bd_ref, s0_ref,
         o_ref, sfin_ref, oi_s, qin_s, up_s, dc_s, st_s) = refs
    else:
        (u_ref, lb_ref, gn_ref, tril_ref, lmask_ref, bd_ref, _,
         o_ref, sfin_ref, oi_s, qin_s, up_s, dc_s, st_s) = refs
    n = T // CHUNK
    C = CHUNK
    bd = bd_ref[...]
    n_lv = int(math.log2(C))

    log_lb = [jnp.log(lb_ref[d]) for d in range(2)]
    log_1mlb = [jnp.log(1.0 - lb_ref[d]) for d in range(2)]

    def gates(rows, d):
        z = u_ref[rows, (3 + d) * GROUP_W:(4 + d) * GROUP_W]
        t = jnp.exp(-jnp.abs(z))
        log_sig = jnp.minimum(z, 0.0) - jnp.log(1.0 + t)
        c = log_1mlb[d] + log_sig
        m = jnp.maximum(log_lb[d], c)
        lf = m + jnp.log(1.0 + jnp.exp(jnp.minimum(log_lb[d], c) - m))
        sig_neg = jnp.where(z > 0.0, t, 1.0) / (1.0 + t)
        return lf, (1.0 - lb_ref[d]) * sig_neg

    for d in range(2):
        if has_s0:
            st_s[d] = jnp.concatenate([s0_ref[d, h].T for h in range(N_HEADS)], axis=-1)
        else:
            st_s[d] = jnp.zeros((HEAD_W, GROUP_W), F32)

    def prepare(it, carry):
        units = [(c, d) for c in range(HG_GROUP) for d in range(2)]
        rows = [pl.ds(pl.multiple_of((it * HG_GROUP + c) * C, C), C) for c in range(HG_GROUP)]
        arow = [pl.ds(pl.multiple_of((it * HG_GROUP + c) * 8, 8), 8) for c in range(HG_GROUP)]
        q = [u_ref[rows[c], 0:GROUP_W] * (HEAD_W ** -0.5) for c in range(HG_GROUP)]
        v = [u_ref[rows[c], GROUP_W:2 * GROUP_W] for c in range(HG_GROUP)]
        vt = [[jnp.concatenate([v[c][:, h * HEAD_W:(h + 1) * HEAD_W].T for h in (2 * p, 2 * p + 1)],
                               axis=-1).astype(BF16) for p in range(N_PAIRS)]
              for c in range(HG_GROUP)]
        v_bd = [[_pair_blockdiag(v[c][:, p * LANE:(p + 1) * LANE].astype(BF16)) for p in range(N_PAIRS)]
                for c in range(HG_GROUP)]
        lf, ks = zip(*[gates(rows[c], d) for c, d in units])
        bs = []
        for i, (c, d) in enumerate(units):
            hi, lo = _split2(lf[i])
            tril = tril_ref[d]
            bs.append(jnp.dot(tril, hi, preferred_element_type=F32) + jnp.dot(tril, lo, preferred_element_type=F32))
        tot = [jnp.sum(x, axis=0, keepdims=True) for x in lf]
        mid_row = [C // 2 - 1 if d == 0 else C // 2 for c, d in units]
        spread = [jnp.maximum(jnp.abs(bs[i][0:1] - bs[i][r:r + 1]), jnp.abs(bs[i][C - 1:C] - bs[i][r:r + 1]))
                  for i, r in enumerate(mid_row)]
        widest = functools.reduce(jnp.maximum, spread)
        ko = [(ks[i] * jnp.exp(tot[i] - bs[i])).astype(BF16) for i in range(len(units))]
        up = [[jnp.dot(vt[c][p], _pair_blockdiag(ko[i][:, p * LANE:(p + 1) * LANE]), preferred_element_type=F32)
               for p in range(N_PAIRS)] for i, (c, d) in enumerate(units)]
        for i, (c, d) in enumerate(units):
            qin_s[d, rows[c], :] = q[c] * jnp.exp(bs[i])
            up_s[d, rows[c], :] = jnp.concatenate(up[i], axis=-1)
            dc_s[d, arow[c], :] = jnp.broadcast_to(jnp.exp(tot[i]), (8, GROUP_W))

        def masked_scores(qe, ke, lv):
            out = []
            for i, (c, d) in enumerate(units):
                per_pair = []
                for p in range(N_PAIRS):
                    sl = slice(p * LANE, (p + 1) * LANE)
                    prod = lax.dot_general(qe[i][:, sl], _pair_blockdiag(ke[i][:, sl]), (((1,), (1,)), ((), ())),
                                           preferred_element_type=F32)
                    per_pair.append(jnp.where(lmask_ref[d, lv] > 0.5, prod, 0.0))
                out.append(per_pair)
            return out

        def finish(sc):
            return [jnp.concatenate([jnp.dot(sc[i][p].astype(BF16), v_bd[c][p], preferred_element_type=F32)
                                     for p in range(N_PAIRS)], axis=-1) for i, (c, d) in enumerate(units)]

        def intra_direct():
            mid = [_block_ref(bs[i], C, C // 2 - 1 if d == 0 else C // 2) for i, (c, d) in enumerate(units)]
            qe = [(q[c] * jnp.exp(bs[i] - mid[i])).astype(BF16) for i, (c, d) in enumerate(units)]
            ke = [(ks[i] * jnp.exp(mid[i] - bs[i])).astype(BF16) for i in range(len(units))]
            return tuple(finish(masked_scores(qe, ke, n_lv)))

        def intra_split():
            sc = [[jnp.zeros((C, LANE), F32) for _ in range(N_PAIRS)] for _ in units]
            s = C // 2
            lv = 0
            while s >= 1:
                e = [jnp.exp(-jnp.abs(bs[i] - _block_ref(bs[i], 2 * s, s - 1 if d == 0 else s)))
                     for i, (c, d) in enumerate(units)]
                part = masked_scores([(q[c] * e[i]).astype(BF16) for i, (c, d) in enumerate(units)],
                                     [(ks[i] * e[i]).astype(BF16) for i in range(len(units))], lv)
                sc = [[sc[i][p] + part[i][p] for p in range(N_PAIRS)] for i in range(len(units))]
                s //= 2
                lv += 1
            fin = finish(sc)
            return tuple(fin[i] + _bdot(q[c] * ks[i], bd) * v[c] for i, (c, d) in enumerate(units))

        oi = lax.cond(jnp.max(widest) < HG_DIRECT_MAX, intra_direct, intra_split)
        for i, (c, d) in enumerate(units):
            oi_s[d, rows[c], :] = oi[i]
        return carry

    lax.fori_loop(0, n // HG_GROUP, prepare, 0)

    def chunk(ci, carry):
        rows = [pl.ds(pl.multiple_of(cidx * C, C), C) for cidx in (ci, n - 1 - ci)]
        decay = [dc_s[d, pl.ds(pl.multiple_of(cidx * 8, 8), 1), :] for d, cidx in ((0, ci), (1, n - 1 - ci))]
        st = [st_s[d] for d in range(2)]
        o_inter = [[lax.dot_general(qin_s[d, rows[d], p * LANE:(p + 1) * LANE].astype(BF16),
                                    _pair_blockdiag(st[d][:, p * LANE:(p + 1) * LANE].astype(BF16)),
                                    (((1,), (1,)), ((), ())), preferred_element_type=F32)
                    for p in range(N_PAIRS)] for d in range(2)]
        for d in range(2):
            st_s[d] = st[d] * decay[d] + up_s[d, rows[d], :]
            oi_s[d, rows[d], :] = oi_s[d, rows[d], :] + jnp.concatenate(o_inter[d], axis=-1)
        return carry

    lax.fori_loop(0, n, chunk, 0, unroll=2)
    o_ref[...] = _head_norm_gate(oi_s[0] + oi_s[1], bd, gn_ref[...], u_ref[:, 2 * GROUP_W:3 * GROUP_W])
    for d in range(2):
        for h in range(N_HEADS):
            sfin_ref[d, h] = st_s[d][:, h * HEAD_W:(h + 1) * HEAD_W].T


def _state_io(in_specs, args, s0, collect, l, nb):
    state_block = (None, None, 2, N_HEADS, HEAD_W, HEAD_W)
    if s0 is not None:
        in_specs.append(pl.BlockSpec(state_block, lambda b: (b, l, 0, 0, 0, 0)))
        args.append(s0)
        return (pl.BlockSpec(state_block[1:], lambda b: (b, 0, 0, 0, 0)),
                jax.ShapeDtypeStruct((nb, 2, N_HEADS, HEAD_W, HEAD_W), F32), {})
    in_specs.append(pl.BlockSpec(memory_space=pl.ANY))
    args.append(collect)
    return (pl.BlockSpec(state_block, lambda b: (b, l, 0, 0, 0, 0)),
            jax.ShapeDtypeStruct(collect.shape, F32), {len(args) - 1: 1})


def _hgrn_call(u_hg, lb_l, gn, consts, s0, collect, l, T, nb, row0):
    tril, lmask, bd = consts
    tb = row0 // T
    has_s0 = s0 is not None
    in_specs = [
        pl.BlockSpec((T, HG_COLS), lambda b: (tb + b, 0)),
        pl.BlockSpec((None, 2, 1, GROUP_W), lambda b: (l, 0, 0, 0)),
        pl.BlockSpec((None, 1, GROUP_W), lambda b: (l, 0, 0)),
        pl.BlockSpec((2, CHUNK, CHUNK), lambda b: (0, 0, 0)),
        pl.BlockSpec((2, 7, CHUNK, LANE), lambda b: (0, 0, 0, 0)),
        pl.BlockSpec((GROUP_W, GROUP_W), lambda b: (0, 0)),
    ]
    args = [u_hg, lb_l.reshape(DEPTH, 2, 1, GROUP_W), gn.reshape(DEPTH, 1, GROUP_W), tril, lmask, bd]
    state_spec, state_shape, aliases = _state_io(in_specs, args, s0, collect, l, nb)
    seq = pltpu.VMEM((2, T, GROUP_W), F32)
    return pl.pallas_call(
        functools.partial(_hgrn_kernel, T=T, has_s0=has_s0),
        grid=(nb,),
        in_specs=in_specs,
        out_specs=[pl.BlockSpec((T, GROUP_W), lambda b: (b, 0)), state_spec],
        out_shape=[jax.ShapeDtypeStruct((nb * T, GROUP_W), F32), state_shape],
        input_output_aliases=aliases,
        scratch_shapes=[seq, seq, seq,
                        pltpu.VMEM((2, T // CHUNK * 8, GROUP_W), F32), pltpu.VMEM((2, HEAD_W, GROUP_W), F32)],
        compiler_params=_cparams(("parallel",)),
        name="hgrn",
    )(*args)


def _shift_rows(x, T):
    row = lax.broadcasted_iota(jnp.int32, x.shape, 0)
    prev = jnp.where(row == 0, 0.0, pltpu.roll(x, 1, 0))
    nxt = jnp.where(row == T - 1, 0.0, pltpu.roll(x, T - 1, 0))
    return prev, nxt


def _conv3(x, w_ref, T):
    prev, nxt = _shift_rows(x, T)
    return prev * w_ref[0:1, :] + x * w_ref[1:2, :] + nxt * w_ref[2:3, :]


GDN_UNROLL = 2


def _solve_unit_lower(systems):
    c2 = 2 * CHUNK
    slabs = [jnp.concatenate([nmat, nmat, rhs], axis=-1) for rhs, nmat in systems]
    steps = int(math.log2(CHUNK))
    for step in range(steps):
        last = step == steps - 1
        nxt = []
        for slab in slabs:
            hi = slab.astype(BF16)
            lo = (slab - hi.astype(F32)).astype(BF16)
            lhs = jnp.concatenate([hi[:, :c2], lo[:, :CHUNK]], axis=-1)
            first = c2 if last else 0
            rhs3 = jnp.concatenate([hi[:, first:], lo[:, first:], hi[:, first:]], axis=0)
            prod = jnp.dot(lhs, rhs3, preferred_element_type=F32)
            if last:
                nxt.append(slab[:, c2:] + prod)
            else:
                nxt.append(jnp.concatenate([prod[:, :c2], slab[:, c2:] + prod[:, c2:]], axis=-1))
        slabs = nxt
    return slabs


def _gdn_kernel(*refs, T, has_s0):
    if has_s0:
        (u_ref, cw_ref, alog_ref, dtb_ref, exp_ref, tril_ref, mask_ref, bd_ref, gn_ref, s0_ref,
         o_ref, sfin_ref, q_s, k_s, v_s, la_s, be_s, uw_s, ww_s, at_s, qin_s, kt_s, al_s, of_s, st_s) = refs
    else:
        (u_ref, cw_ref, alog_ref, dtb_ref, exp_ref, tril_ref, mask_ref, bd_ref, gn_ref, _,
         o_ref, sfin_ref, q_s, k_s, v_s, la_s, be_s, uw_s, ww_s, at_s, qin_s, kt_s, al_s, of_s, st_s) = refs
    n = T // CHUNK
    C = CHUNK
    bd = bd_ref[...]

    qkv = _silu(_conv3(u_ref[:, 0:3 * GROUP_W], cw_ref, T))
    q = qkv[:, 0:GROUP_W]
    k = qkv[:, GROUP_W:2 * GROUP_W]
    q_s[...] = q * lax.rsqrt(_sel_dot_right(q * q, bd) + 1e-6) * (HEAD_W ** -0.5)
    k_s[...] = k * lax.rsqrt(_sel_dot_right(k * k, bd) + 1e-6)
    v_s[...] = qkv[:, 2 * GROUP_W:3 * GROUP_W]

    ab = u_ref[:, 4 * GROUP_W:4 * GROUP_W + LANE]
    xa = ab + dtb_ref[...]
    softplus = jnp.maximum(xa, 0.0) + jnp.log(1.0 + jnp.exp(-jnp.abs(xa)))
    log_a = -jnp.exp(alog_ref[...]) * softplus
    lane = lax.broadcasted_iota(jnp.int32, ab.shape, 1)
    narrow = jnp.where(lane < 8, log_a, _sigmoid(ab))
    for d in range(2):
        wide = _dot_sel(narrow, exp_ref[d])
        la_s[d] = wide[:, 0:GROUP_W]
        be_s[d] = wide[:, GROUP_W:2 * GROUP_W]
        if has_s0:
            st_s[d] = jnp.concatenate([s0_ref[d, h] for h in range(N_HEADS)], axis=-1)
        else:
            st_s[d] = jnp.zeros((HEAD_W, GROUP_W), F32)

    def prepare(cidx):
        r0 = pl.multiple_of(cidx * C, C)
        rows = pl.ds(r0, C)
        arow = pl.ds(pl.multiple_of(cidx * 8, 8), 8)
        q = q_s[rows, :]
        k = k_s[rows, :]
        v = v_s[rows, :]
        systems = []
        attns = []
        kts = []
        for d in range(2):
            incl = mask_ref[d, 0] > 0.5
            strict = mask_ref[d, 1]
            la = la_s[d, rows, :]
            be = be_s[d, rows, :]
            gx = _sel_dot(tril_ref[d], la)
            gtot = jnp.sum(la, axis=0, keepdims=True)
            eg = jnp.exp(gx)
            kout = k * jnp.exp(gtot - gx)
            qin_s[d, rows, :] = q * eg
            al_s[d, arow, :] = jnp.broadcast_to(jnp.exp(gtot), (8, GROUP_W))
            kb = k * be
            vb = v * be
            kbg = kb * eg
            for h in range(N_HEADS):
                sl = slice(h * HEAD_W, (h + 1) * HEAD_W)
                gh = gx[:, sl]
                dmat = gh - gh.T
                dec = jnp.where(incl, jnp.exp(jnp.where(incl, dmat, 0.0)), 0.0)
                qk = _bdot_nt(jnp.concatenate([kb[:, sl], q[:, sl]], axis=0), k[:, sl])
                nmat = -(qk[:C] * dec * strict)
                systems.append((jnp.concatenate([vb[:, sl], kbg[:, sl]], axis=-1), nmat))
                attns.append(qk[C:] * dec)
                kts.append(kout[:, sl].T)
        sols = _solve_unit_lower(systems)
        for d in range(2):
            mine = sols[d * N_HEADS:(d + 1) * N_HEADS]
            uw_s[d, rows, :] = jnp.concatenate([x[:, :HEAD_W] for x in mine], axis=-1)
            ww_s[d, rows, :] = jnp.concatenate([x[:, HEAD_W:] for x in mine], axis=-1)
            at_s[d, rows, :] = jnp.concatenate(attns[d * N_HEADS:(d + 1) * N_HEADS], axis=-1)
            kt_s[d, rows, :] = jnp.concatenate(kts[d * N_HEADS:(d + 1) * N_HEADS], axis=-1)

    def prep_body(i, carry):
        for j in range(GDN_UNROLL):
            prepare(i * GDN_UNROLL + j)
        return carry

    lax.fori_loop(0, n // GDN_UNROLL, prep_body, 0)

    def chunk(ci, carry):
        units = [(d, p) for d in range(2) for p in range(N_PAIRS)]
        rows = []
        alast = []
        for d, cidx in ((0, ci), (1, n - 1 - ci)):
            rows.append(pl.ds(pl.multiple_of(cidx * C, C), C))
            alast.append(al_s[d, pl.ds(pl.multiple_of(cidx * 8, 8), 1), :])
        st = [st_s[d] for d in range(2)]
        lanes = [slice(p * LANE, (p + 1) * LANE) for p in range(N_PAIRS)]
        both = [jnp.dot(jnp.concatenate([ww_s[d, rows[d], lanes[p]], qin_s[d, rows[d], lanes[p]]], axis=0).astype(BF16),
                        _pair_blockdiag(st[d][:, lanes[p]].astype(BF16)), preferred_element_type=F32)
                for d, p in units]
        vnew = [uw_s[d, rows[d], lanes[p]] - both[i][:C] for i, (d, p) in enumerate(units)]
        upd = [jnp.dot(jnp.concatenate([at_s[d, rows[d], lanes[p]], kt_s[d, rows[d], lanes[p]]], axis=0).astype(BF16),
                       _pair_blockdiag(vnew[i].astype(BF16)), preferred_element_type=F32)
               for i, (d, p) in enumerate(units)]
        for d in range(2):
            idx = range(d * N_PAIRS, (d + 1) * N_PAIRS)
            of_s[d, rows[d], :] = jnp.concatenate([both[i][C:] + upd[i][:C] for i in idx], axis=-1)
            st_s[d] = st[d] * alast[d] + jnp.concatenate([upd[i][C:] for i in idx], axis=-1)
        return carry

    lax.fori_loop(0, n, chunk, 0)
    o_ref[...] = _head_norm_gate(of_s[0] + of_s[1], bd, gn_ref[...], u_ref[:, 3 * GROUP_W:4 * GROUP_W])
    for d in range(2):
        for h in range(N_HEADS):
            sfin_ref[d, h] = st_s[d][:, h * HEAD_W:(h + 1) * HEAD_W]


def _gdn_call(u_gd, cw, alog, dtb, gn, consts, s0, collect, l, T, nb, row0):
    tril, masks, expand, bd = consts
    tb = row0 // T
    has_s0 = s0 is not None
    in_specs = [
        pl.BlockSpec((T, GD_PAD), lambda b: (tb + b, 0)),
        pl.BlockSpec((None, 3, 3 * GROUP_W), lambda b: (l, 0, 0)),
        pl.BlockSpec((None, 1, LANE), lambda b: (l, 0, 0)),
        pl.BlockSpec((None, 1, LANE), lambda b: (l, 0, 0)),
        pl.BlockSpec((2, LANE, 2 * GROUP_W), lambda b: (0, 0, 0)),
        pl.BlockSpec((2, CHUNK, CHUNK), lambda b: (0, 0, 0)),
        pl.BlockSpec((2, 2, CHUNK, CHUNK), lambda b: (0, 0, 0, 0)),
        pl.BlockSpec((GROUP_W, GROUP_W), lambda b: (0, 0)),
        pl.BlockSpec((None, 1, GROUP_W), lambda b: (l, 0, 0)),
    ]
    args = [u_gd, cw, alog, dtb, expand, tril, masks, bd, gn]
    state_spec, state_shape, aliases = _state_io(in_specs, args, s0, collect, l, nb)
    seq = pltpu.VMEM((2, T, GROUP_W), F32)
    return pl.pallas_call(
        functools.partial(_gdn_kernel, T=T, has_s0=has_s0),
        grid=(nb,),
        in_specs=in_specs,
        out_specs=[pl.BlockSpec((T, GROUP_W), lambda b: (b, 0)), state_spec],
        out_shape=[jax.ShapeDtypeStruct((nb * T, GROUP_W), F32), state_shape],
        input_output_aliases=aliases,
        scratch_shapes=[pltpu.VMEM((T, GROUP_W), F32)] * 3 + [seq, seq, seq, seq, seq, seq, seq,
            pltpu.VMEM((2, T // CHUNK * 8, GROUP_W), F32), seq, pltpu.VMEM((2, HEAD_W, GROUP_W), F32)],
        compiler_params=_cparams(("parallel",)),
        name="gdn",
    )(*args)


def _hyfilt_kernel(z_ref, win_ref, fh_ref, fl_ref, w1_ref, b1_ref, fr_ref, w2_ref, b2_ref, w3_ref, o_ref, *, T):
    fr = fr_ref[...]
    h = jnp.sin(fr * (_dot3(z_ref[...], w1_ref[...]) + b1_ref[...]))
    h = jnp.sin(fr * (_dot3(h, w2_ref[...]) + b2_ref[...]))
    h = _dot3(h, w3_ref[...])
    win = win_ref[...]
    hf = h[:, 0:GROUP_W] * win
    hb = h[:, GROUP_W:2 * GROUP_W] * win
    row = lax.broadcasted_iota(jnp.int32, hb.shape, 0)
    hb = jnp.where(row == 0, 0.0, hb)
    for rows, taps in ((slice(0, T), (hf + hb).astype(BF16)), (slice(T, 2 * T), (hf - hb).astype(BF16))):
        o_ref[rows, :] = (jnp.dot(fh_ref[rows, :], taps, preferred_element_type=F32)
                          + jnp.dot(fl_ref[rows, :], taps, preferred_element_type=F32))


def _hyfilt_call(T, zp, win, fh, fl, w1p, b1, freq, w2, b2, w3):
    c2 = lambda l: (0, 0)
    return pl.pallas_call(
        functools.partial(_hyfilt_kernel, T=T),
        grid=(DEPTH,),
        in_specs=[
            pl.BlockSpec((T, LANE), c2),
            pl.BlockSpec((T, GROUP_W), c2),
            pl.BlockSpec((2 * T, T), c2),
            pl.BlockSpec((2 * T, T), c2),
            pl.BlockSpec((None, LANE, HY_FH), lambda l: (l, 0, 0)),
            pl.BlockSpec((None, 1, HY_FH), lambda l: (l, 0, 0)),
            pl.BlockSpec((None, 1, HY_FH), lambda l: (l, 0, 0)),
            pl.BlockSpec((None, HY_FH, HY_FH), lambda l: (l, 0, 0)),
            pl.BlockSpec((None, 1, HY_FH), lambda l: (l, 0, 0)),
            pl.BlockSpec((None, HY_FH, 2 * GROUP_W), lambda l: (l, 0, 0)),
        ],
        out_specs=pl.BlockSpec((None, 2 * T, GROUP_W), lambda l: (l, 0, 0)),
        out_shape=jax.ShapeDtypeStruct((DEPTH, 2 * T, GROUP_W), F32),
        compiler_params=_cparams(("parallel",)),
        name="hyfilt",
    )(zp, win, fh, fl, w1p, b1, freq, w2, b2, w3)


def _hyena_kernel(u_ref, cw_ref, cb_ref, spec_ref, skip_ref, fh_ref, fl_ref, ih_ref, il_ref, o_ref, *, T):
    uc = _conv3(u_ref[...], cw_ref, T) + cb_ref[...]
    x0 = uc[:, 0:GROUP_W]
    z = uc[:, GROUP_W:2 * GROUP_W] * uc[:, 2 * GROUP_W:3 * GROUP_W]
    zb = z.astype(BF16)
    zs = (jnp.dot(fh_ref[...], zb, preferred_element_type=F32)
          + jnp.dot(fl_ref[...], zb, preferred_element_type=F32))
    ar, ai = zs[0:T], zs[T:2 * T]
    br, bi = spec_ref[0:T, :], spec_ref[T:2 * T, :]
    pb = jnp.concatenate([ar * br - ai * bi, ar * bi + ai * br], axis=0).astype(BF16)
    y = (jnp.dot(ih_ref[...], pb, preferred_element_type=F32)
         + jnp.dot(il_ref[...], pb, preferred_element_type=F32))
    o_ref[...] = x0 * (y + z * skip_ref[...])


def _hyena_call(u_hy, cw, cb, spec, skip, dft, l, T, nb, row0):
    fh, fl, ih, il = dft
    tb = row0 // T
    c2 = lambda b: (0, 0)
    return pl.pallas_call(
        functools.partial(_hyena_kernel, T=T),
        grid=(nb,),
        in_specs=[
            pl.BlockSpec((T, HY_COLS), lambda b: (tb + b, 0)),
            pl.BlockSpec((None, 3, HY_COLS), lambda b: (l, 0, 0)),
            pl.BlockSpec((None, 1, HY_COLS), lambda b: (l, 0, 0)),
            pl.BlockSpec((None, 2 * T, GROUP_W), lambda b: (l, 0, 0)),
            pl.BlockSpec((None, 1, GROUP_W), lambda b: (l, 0, 0)),
            pl.BlockSpec((2 * T, T), c2),
            pl.BlockSpec((2 * T, T), c2),
            pl.BlockSpec((T, 2 * T), c2),
            pl.BlockSpec((T, 2 * T), c2),
        ],
        out_specs=pl.BlockSpec((T, GROUP_W), lambda b: (b, 0)),
        out_shape=jax.ShapeDtypeStruct((nb * T, GROUP_W), F32),
        compiler_params=_cparams(("parallel",)),
        name="hyena",
    )(u_hy, cw, cb, spec, skip, fh, fl, ih, il)


def _rope(x, cosf, sinf):
    lane = lax.broadcasted_iota(jnp.int32, x.shape, 1)
    half = MLA_ROPE // 2
    partner = jnp.where(lane < MLA_NOPE + half, pltpu.roll(x, LANE - half, 1), pltpu.roll(x, half, 1))
    return x * cosf + partner * sinf


def _qk_norm(x, g):
    ms = jnp.sum(x * x, axis=-1, keepdims=True) * (1.0 / MLA_QK)
    return x * lax.rsqrt(ms + RMS_EPS) * g


def _mla_kernel(*refs, T, ctx):
    if ctx:
        (u_ref, qn_ref, wq_ref, kvn_ref, wkv_ref, qkn_ref, cos_ref, sin_ref, cckv_ref, ckr_ref, o_ref,
         q_s, k_s, v_s) = refs
    else:
        (u_ref, qn_ref, wq_ref, kvn_ref, wkv_ref, qkn_ref, _, _, o_ref, ckv_ref, kr_ref, q_s, k_s, v_s) = refs
    n_keys = k_s.shape[1]
    u = u_ref[...]
    cq = _rms(u[:, 0:MLA_Q_LORA], qn_ref[...])
    ckv = _rms(u[:, MLA_Q_LORA:MLA_Q_LORA + MLA_KV_LORA], kvn_ref[...])
    kr = u[:, MLA_Q_LORA + MLA_KV_LORA:MLA_Q_LORA + MLA_KV_LORA + MLA_ROPE]
    if not ctx:
        ckv_ref[...] = ckv
        kr_ref[...] = kr
    q_all = _bdot(cq, wq_ref[...])
    kv = _bdot(ckv, wkv_ref[...])
    gq = qkn_ref[0:1, :]
    gk = qkn_ref[1:2, :]
    if ctx:
        kvc = _bdot(cckv_ref[...], wkv_ref[...])
        krc = ckr_ref[...]
        cosf, sinf = cos_ref[...], sin_ref[...]
    q_scale = MLA_QK ** -0.5 * math.log2(math.e)
    kr_tile = jnp.concatenate([jnp.zeros((T, MLA_NOPE), F32), kr, jnp.zeros((T, LANE - MLA_QK), F32)], axis=-1)
    kr_rot = kr_tile * gk
    if ctx:
        kr_rot = _rope(kr_rot, cosf, sinf)
    nope_lane = lax.broadcasted_iota(jnp.int32, (T, LANE), 1) < MLA_NOPE
    for h in range(N_HEADS):
        qh = _qk_norm(q_all[:, h * LANE:(h + 1) * LANE], gq)
        k_nope = jnp.concatenate([kv[:, h * HEAD_W:(h + 1) * HEAD_W], jnp.zeros((T, LANE - MLA_NOPE), F32)], axis=-1)
        ms = jnp.sum(k_nope * k_nope + kr_tile * kr_tile, axis=-1, keepdims=True) * (1.0 / MLA_QK)
        kh = jnp.where(nope_lane, k_nope * gk, kr_rot) * lax.rsqrt(ms + RMS_EPS)
        if ctx:
            qh = _rope(qh, cosf, sinf)
            zc = jnp.zeros((n_keys - T, LANE - MLA_QK), F32)
            kc = _qk_norm(jnp.concatenate([kvc[:, h * HEAD_W:(h + 1) * HEAD_W], krc, zc], axis=-1), gk)
            k_s[h, T:n_keys, :] = kc.astype(BF16)
        q_s[:, h * LANE:(h + 1) * LANE] = (qh * q_scale).astype(BF16)
        k_s[h, 0:T, :] = kh.astype(BF16)
    for p in range(N_PAIRS):
        lanes = slice(GROUP_W + p * LANE, GROUP_W + (p + 1) * LANE)
        vp = kv[:, lanes]
        if ctx:
            vp = jnp.concatenate([vp, kvc[:, lanes]], axis=0)
        v_s[p] = _pair_blockdiag(vp.astype(BF16))

    def q_block(qb, carry):
        rows = pl.ds(pl.multiple_of(qb * ATT_QBLOCK, ATT_QBLOCK), ATT_QBLOCK)
        lane = lax.broadcasted_iota(jnp.int32, (ATT_QBLOCK, LANE), 1)
        for p in range(N_PAIRS):
            es, sums = [], []
            for h in (2 * p, 2 * p + 1):
                s = lax.dot_general(q_s[rows, h * LANE:(h + 1) * LANE], k_s[h], (((1,), (1,)), ((), ())),
                                    preferred_element_type=F32)
                e = jnp.exp2(s - jnp.max(s, axis=-1, keepdims=True))
                sums.append(jnp.sum(e, axis=-1, keepdims=True))
                es.append(e.astype(BF16))
            o = jnp.dot(jnp.concatenate(es, axis=-1), v_s[p], preferred_element_type=F32)
            o_ref[rows, p * LANE:(p + 1) * LANE] = o / jnp.where(lane < HEAD_W, sums[0], sums[1])
        return carry

    lax.fori_loop(0, T // ATT_QBLOCK, q_block, 0)


def _mla_call(u_mla, qn, wq, kvn, wkv, qkn, rope, cache, collect, l, T, nb, row0):
    tb = row0 // T
    ctx = cache is not None
    n_keys = T + (PAST_LEN if ctx else 0)
    c2 = lambda b: (0, 0)
    in_specs = [
        pl.BlockSpec((T, MLA_PAD), lambda b: (tb + b, 0)),
        pl.BlockSpec((None, 1, MLA_Q_LORA), lambda b: (l, 0, 0)),
        pl.BlockSpec((None, MLA_Q_LORA, N_HEADS * LANE), lambda b: (l, 0, 0)),
        pl.BlockSpec((None, 1, MLA_KV_LORA), lambda b: (l, 0, 0)),
        pl.BlockSpec((None, MLA_KV_LORA, 2 * GROUP_W), lambda b: (l, 0, 0)),
        pl.BlockSpec((None, 2, LANE), lambda b: (l, 0, 0)),
    ]
    args = [u_mla, qn, wq, kvn, wkv, qkn]
    out_specs = [pl.BlockSpec((T, GROUP_W), lambda b: (b, 0))]
    out_shape = [jax.ShapeDtypeStruct((nb * T, GROUP_W), F32)]
    if ctx:
        in_specs += [
            pl.BlockSpec((T, LANE), c2),
            pl.BlockSpec((T, LANE), c2),
            pl.BlockSpec((None, None, PAST_LEN, MLA_KV_LORA), lambda b: (b, l, 0, 0)),
            pl.BlockSpec((None, None, PAST_LEN, MLA_ROPE), lambda b: (b, l, 0, 0)),
        ]
        args += [rope[0], rope[1], cache[0], cache[1]]
        aliases = {}
    else:
        in_specs += [pl.BlockSpec(memory_space=pl.ANY)] * 2
        aliases = {len(args): 1, len(args) + 1: 2}
        args += list(collect)
        out_specs += [pl.BlockSpec((None, None, T, MLA_KV_LORA), lambda b: (b, l, 0, 0)),
                      pl.BlockSpec((None, None, T, MLA_ROPE), lambda b: (b, l, 0, 0))]
        out_shape += [jax.ShapeDtypeStruct(a.shape, F32) for a in collect]
    return pl.pallas_call(
        functools.partial(_mla_kernel, T=T, ctx=ctx),
        grid=(nb,),
        in_specs=in_specs,
        out_specs=out_specs,
        out_shape=out_shape,
        input_output_aliases=aliases,
        scratch_shapes=[pltpu.VMEM((T, N_HEADS * LANE), BF16), pltpu.VMEM((N_HEADS, n_keys, LANE), BF16),
                        pltpu.VMEM((N_PAIRS, 2 * n_keys, LANE), BF16)],
        compiler_params=_cparams(("parallel",)),
        name="mla",
    )(*args)


def _pad_cols(w, width):
    return jnp.pad(w, [(0, 0)] * (w.ndim - 1) + [(0, width - w.shape[-1])])


W_IN_PREP_COLS = 256


def _prep_w_in_kernel(wt_ref, o_hg, o_hy, o_mla, o_gd):
    start = 0
    for o_ref, cols in ((o_hg, HG_COLS), (o_hy, HY_COLS), (o_mla, MLA_COLS), (o_gd, GD_COLS)):
        width = o_ref.shape[-1]
        for c0 in range(0, width, W_IN_PREP_COLS):
            n_out = min(W_IN_PREP_COLS, width - c0)
            n_real = max(0, min(n_out, cols - c0))
            piece = wt_ref[start + c0:start + c0 + n_real, :]
            if n_real < n_out:
                piece = jnp.concatenate([piece, jnp.zeros((n_out - n_real, D_MODEL), F32)], axis=0)
            o_ref[:, c0:c0 + n_out] = piece.T.astype(BF16)
        start += cols


def _prep_w_in(w_in):
    widths = (HG_COLS, HY_COLS, MLA_PAD, GD_PAD)
    n_cols = w_in.shape[-1]
    return pl.pallas_call(
        _prep_w_in_kernel,
        grid=(DEPTH,),
        in_specs=[pl.BlockSpec((None, n_cols, D_MODEL), lambda l: (l, 0, 0))],
        out_specs=[pl.BlockSpec((None, D_MODEL, w), lambda l: (l, 0, 0)) for w in widths],
        out_shape=[jax.ShapeDtypeStruct((DEPTH, D_MODEL, w), BF16) for w in widths],
        compiler_params=_cparams(("parallel",)),
        name="w_in_prep",
    )(jnp.swapaxes(w_in, 1, 2))


def _prep_wq(w_q_up):
    w = w_q_up.reshape(DEPTH, MLA_Q_LORA, N_HEADS, MLA_QK)
    return _pad_cols(w, LANE).reshape(DEPTH, MLA_Q_LORA, N_HEADS * LANE).astype(BF16)


def _prep_wkv(w_kv_up):
    w = w_kv_up.reshape(DEPTH, MLA_KV_LORA, N_HEADS, 2, HEAD_W)
    return w.transpose(0, 1, 3, 2, 4).reshape(DEPTH, MLA_KV_LORA, 2 * GROUP_W).astype(BF16)


def _lower_bounds(hgrn_lb):
    lb = jnp.cumsum(jax.nn.softmax(hgrn_lb.astype(F32), axis=0), axis=0)
    return lb - lb[0]


def kernel(x_prompt, x_sample, cache_mla_ckv, cache_mla_krope, state_hgrn, state_gdn, c, c_ctx, w_ada, b_ada, norm_ffn, w_ffn_gu, w_ffn_down, norm_mix, w_in, w_out, hgrn_lb, hgrn_norm, hy_conv_w, hy_conv_b, hy_w1, hy_b1, hy_freq, hy_w2, hy_b2, hy_w3, hy_skip, mla_q_norm_a, mla_w_q_up, mla_kv_norm_a, mla_w_kv_up, mla_qk_norm, gdn_conv_w, gdn_a_log, gdn_dt_bias, gdn_norm):
    x = (x_prompt.reshape(N_PROMPT, D_MODEL), x_sample.reshape(N_SAMPLE, D_MODEL))

    cond8 = jnp.zeros((8, D_MODEL), F32).at[0].set(c_ctx).at[1:1 + DEC_BATCH].set(c)
    ada = _ada_call(cond8, w_ada, b_ada)

    w_in_parts = _prep_w_in(w_in)
    w_out_bf = w_out.astype(BF16)
    wq = _prep_wq(mla_w_q_up)
    wkv = _prep_wkv(mla_w_kv_up)
    qkn = _pad_cols(mla_qk_norm, LANE)
    lb_all = _lower_bounds(hgrn_lb)
    alog = _pad_cols(gdn_a_log.reshape(DEPTH, 1, 8), LANE)
    dtb = _pad_cols(gdn_dt_bias.reshape(DEPTH, 1, 8), LANE)
    gdn_gn = jnp.tile(gdn_norm, (1, N_HEADS)).reshape(DEPTH, 1, GROUP_W)
    w1p = jnp.pad(hy_w1, ((0, 0), (0, LANE - HY_EMB), (0, 0)))

    bd = jnp.asarray(_block_diag_ones(), BF16)
    hg_tril, hg_m = _hgrn_consts()
    hg_consts = (jnp.asarray(hg_tril, BF16), jnp.asarray(hg_m, F32), bd)
    gd_tril, gd_masks, gd_expand = _gdn_consts()
    gd_consts = (jnp.asarray(gd_tril, BF16), jnp.asarray(gd_masks, F32), jnp.asarray(gd_expand, BF16), bd)
    rope = tuple(jnp.asarray(a) for a in _rope_consts(DEC_SEQ))
    groups = ((SEQ, BATCH, 0), (DEC_SEQ, DEC_BATCH, N_PROMPT))
    dft = {}
    spec = {}
    for T, _, _ in groups:
        fwd, inv = _dft_consts(T)
        fh, fl = _np_split2(fwd)
        ih, il = _np_split2(inv)
        dft[T] = (fh, fl, ih, il)
        zp, win = _hyena_pos_consts(T)
        spec[T] = _hyfilt_call(T, jnp.asarray(zp), jnp.asarray(win), fh, fl, w1p,
                               hy_b1.reshape(DEPTH, 1, HY_FH), hy_freq.reshape(DEPTH, 1, HY_FH), hy_w2,
                               hy_b2.reshape(DEPTH, 1, HY_FH), hy_w3)

    new_ckv = jnp.zeros((BATCH, DEPTH, SEQ, MLA_KV_LORA), F32)
    new_kr = jnp.zeros((BATCH, DEPTH, SEQ, MLA_ROPE), F32)
    new_hg = jnp.zeros((BATCH, DEPTH, 2, N_HEADS, HEAD_W, HEAD_W), F32)
    new_gd = jnp.zeros((BATCH, DEPTH, 2, N_HEADS, HEAD_W, HEAD_W), F32)
    for l in range(DEPTH):
        x = _ffn_call(x, ada, norm_ffn, w_ffn_gu, w_ffn_down, l, 0)
        u_hg, u_hy, u_mla, u_gd = _inproj_call(x, ada, norm_mix, w_in_parts, l)
        outs = []
        for gi, (T, nb, row0) in enumerate(groups):
            latent = gi == 1
            o_hg, s_hg = _hgrn_call(u_hg, lb_all, hgrn_norm, hg_consts, state_hgrn if latent else None,
                                    None if latent else new_hg, l, T, nb, row0)
            o_hy = _hyena_call(u_hy, hy_conv_w, hy_conv_b.reshape(DEPTH, 1, HY_COLS), spec[T],
                               hy_skip.reshape(DEPTH, 1, GROUP_W), dft[T], l, T, nb, row0)
            mla = _mla_call(u_mla, mla_q_norm_a.reshape(DEPTH, 1, MLA_Q_LORA), wq,
                            mla_kv_norm_a.reshape(DEPTH, 1, MLA_KV_LORA), wkv, qkn,
                            rope if latent else None,
                            (cache_mla_ckv, cache_mla_krope) if latent else None,
                            None if latent else (new_ckv, new_kr), l, T, nb, row0)
            o_gd, s_gd = _gdn_call(u_gd, gdn_conv_w, alog, dtb, gdn_gn, gd_consts, state_gdn if latent else None,
                                   None if latent else new_gd, l, T, nb, row0)
            outs.append((o_hg, o_hy, mla[0], o_gd))
            if not latent:
                new_ckv, new_kr, new_hg, new_gd = mla[1], mla[2], s_hg, s_gd
        x = _outproj_call(x, ada, w_out_bf, outs[0], outs[1], l)
        x = _ffn_call(x, ada, norm_ffn, w_ffn_gu, w_ffn_down, l, 1)

    y_prompt = x[:N_PROMPT].reshape(BATCH, SEQ, D_MODEL)
    y_sample = x[N_PROMPT:].reshape(DEC_BATCH, DEC_SEQ, D_MODEL)
    return (y_prompt, y_sample, new_ckv, new_kr, new_hg, new_gd)
```

```python
import functools
import math

import numpy as np
import jax
import jax.numpy as jnp
from jax import lax
from jax.experimental import pallas as pl
from jax.experimental.pallas import tpu as pltpu

F32 = jnp.float32
BF16 = jnp.bfloat16

D_MODEL = 1024
BATCH = 16
SEQ = 256
DEPTH = 4
DEC_BATCH = 2
DEC_SEQ = 1024
PAST_LEN = 256
GRID_W = 64
N_ADA = 9
D_FF = 2816
GROUP_W = 256
CHUNK = 64
RMS_EPS = 1e-6
N_HEADS = 4
HEAD_W = 64
HY_EMB = 33
HY_FH = 64
HY_TARGET = 1e-2
HY_FAST = 0.3
HY_SLOW = 1.5
MLA_NOPE = 64
MLA_ROPE = 32
MLA_QK = MLA_NOPE + MLA_ROPE
MLA_Q_LORA = 256
MLA_KV_LORA = 128
ROPE_BASE = 10000.0

HG_COLS = 5 * GROUP_W
HY_COLS = 3 * GROUP_W
MLA_COLS = MLA_Q_LORA + MLA_KV_LORA + MLA_ROPE
GD_COLS = 4 * GROUP_W + 16
MLA_PAD = 512
GD_PAD = 1152

N_PROMPT = BATCH * SEQ
N_SAMPLE = DEC_BATCH * DEC_SEQ
N_TOK = N_PROMPT + N_SAMPLE
LANE = 128
VMEM_LIMIT = 56 * 1024 * 1024
ROW_TILE = 1024
FF_TILE = 256
ADA_TILE = 1536
ATT_QBLOCK = 256


def _bdot(a, b):
    return jnp.dot(a.astype(BF16), b.astype(BF16), preferred_element_type=F32)


def _bdot_nt(a, b):
    return lax.dot_general(a.astype(BF16), b.astype(BF16), (((1,), (1,)), ((), ())),
                           preferred_element_type=F32)


def _split2(x):
    hi = x.astype(BF16)
    lo = (x - hi.astype(F32)).astype(BF16)
    return hi, lo


def _split3(x):
    hi = x.astype(BF16)
    r = x - hi.astype(F32)
    mid = r.astype(BF16)
    lo = (r - mid.astype(F32)).astype(BF16)
    return hi, mid, lo


def _dot3(a, b):
    ah, al = _split2(a)
    bh, bl = _split2(b)
    return (jnp.dot(ah, bh, preferred_element_type=F32) + jnp.dot(ah, bl, preferred_element_type=F32)
            + jnp.dot(al, bh, preferred_element_type=F32))


def _sel_dot(c, x):
    h, m, l = _split3(x)
    return (jnp.dot(c, h, preferred_element_type=F32) + jnp.dot(c, m, preferred_element_type=F32)
            + jnp.dot(c, l, preferred_element_type=F32))


def _dot_sel(x, c):
    h, m, l = _split3(x)
    return (jnp.dot(h, c, preferred_element_type=F32) + jnp.dot(m, c, preferred_element_type=F32)
            + jnp.dot(l, c, preferred_element_type=F32))


def _sigmoid(x):
    return 1.0 / (1.0 + jnp.exp(-x))


def _silu(x):
    return x * _sigmoid(x)


def _rms(x, g):
    return x * lax.rsqrt(jnp.mean(x * x, axis=-1, keepdims=True) + RMS_EPS) * g


def _cparams(sem):
    return pltpu.CompilerParams(dimension_semantics=sem, vmem_limit_bytes=VMEM_LIMIT)


def _cond_of_tile(i):
    return jnp.maximum(i - (N_PROMPT // ROW_TILE - 1), 0)


def _ada_kernel(c_ref, w_ref, b_ref, o_ref):
    ch, cl = _split2(_silu(c_ref[...]))
    w = w_ref[...].astype(BF16)
    o_ref[...] = (jnp.dot(ch, w, preferred_element_type=F32) + jnp.dot(cl, w, preferred_element_type=F32)
                  + b_ref[...])


def _ada_call(cond8, w_ada, b_ada):
    n = N_ADA * D_MODEL
    out = pl.pallas_call(
        _ada_kernel,
        grid=(DEPTH, n // ADA_TILE),
        in_specs=[
            pl.BlockSpec((8, D_MODEL), lambda l, j: (0, 0)),
            pl.BlockSpec((None, D_MODEL, ADA_TILE), lambda l, j: (l, 0, j)),
            pl.BlockSpec((None, 1, ADA_TILE), lambda l, j: (l, 0, j)),
        ],
        out_specs=pl.BlockSpec((None, 8, ADA_TILE), lambda l, j: (l, 0, j)),
        out_shape=jax.ShapeDtypeStruct((DEPTH, 8, n), F32),
        compiler_params=_cparams(("parallel", "parallel")),
        name="ada",
    )(cond8, w_ada, b_ada.reshape(DEPTH, 1, n))
    return out.reshape(DEPTH, 8, N_ADA, D_MODEL)


FFN_SUBTILES = 2


def _ffn_kernel(*refs, sub, split):
    nx = 1 if split is None else 2
    x_refs = refs[:nx]
    ada_refs = refs[nx:nx + FFN_SUBTILES]
    g_ref, wg_ref, wu_ref, wd_ref, o_ref, h_scr = refs[nx + FFN_SUBTILES:]
    i = pl.program_id(0)
    f = pl.program_id(1)

    def prologue(x_ref):
        for r, ada_ref in enumerate(ada_refs):
            rows = slice(r * ROW_TILE, (r + 1) * ROW_TILE)
            y = _rms(x_ref[rows, :], g_ref[...])
            h = y * (1.0 + ada_ref[3 * sub + 1:3 * sub + 2, :]) + ada_ref[3 * sub:3 * sub + 1, :]
            h_scr[rows, :] = h.astype(BF16)
        o_ref[...] = jnp.zeros_like(o_ref)

    def epilogue(x_ref):
        for r, ada_ref in enumerate(ada_refs):
            rows = slice(r * ROW_TILE, (r + 1) * ROW_TILE)
            o_ref[rows, :] = x_ref[rows, :] + 0.5 * ada_ref[3 * sub + 2:3 * sub + 3, :] * o_ref[rows, :]

    def on(step, fn):
        if split is None:
            pl.when(f == step)(functools.partial(fn, x_refs[0]))
        else:
            pl.when((f == step) & (i < split))(functools.partial(fn, x_refs[0]))
            pl.when((f == step) & (i >= split))(functools.partial(fn, x_refs[1]))

    on(0, prologue)
    wg = wg_ref[...].astype(BF16)
    wu = wu_ref[...].astype(BF16)
    wd = wd_ref[...].astype(BF16)
    for r in range(FFN_SUBTILES):
        rows = slice(r * ROW_TILE, (r + 1) * ROW_TILE)
        h = h_scr[rows, :]
        gate = jnp.dot(h, wg, preferred_element_type=F32)
        up = jnp.dot(h, wu, preferred_element_type=F32)
        a = (_silu(gate) * up).astype(BF16)
        o_ref[rows, :] += jnp.dot(a, wd, preferred_element_type=F32)
    on(pl.num_programs(1) - 1, epilogue)


def _ffn_call(xs, ada, norm_ffn, w_gu, w_down, l, j):
    sub = 2 * j
    nf = D_FF // FF_TILE
    rows = FFN_SUBTILES * ROW_TILE

    def ada_spec(r):
        return pl.BlockSpec((None, None, N_ADA, D_MODEL),
                            lambda i, f: (l, _cond_of_tile(i * FFN_SUBTILES + r), 0, 0))

    if isinstance(xs, tuple):
        split = xs[0].shape[0] // rows
        n_tail = xs[1].shape[0] // rows
        x_specs = [pl.BlockSpec((rows, D_MODEL), lambda i, f: (jnp.minimum(i, split - 1), 0),
                                pipeline_mode=pl.Buffered(1)),
                   pl.BlockSpec((rows, D_MODEL), lambda i, f: (jnp.clip(i - split, 0, n_tail - 1), 0),
                                pipeline_mode=pl.Buffered(1))]
    else:
        split = None
        xs = (xs,)
        x_specs = [pl.BlockSpec((rows, D_MODEL), lambda i, f: (i, 0), pipeline_mode=pl.Buffered(1))]

    return pl.pallas_call(
        functools.partial(_ffn_kernel, sub=sub, split=split),
        grid=(N_TOK // rows, nf),
        in_specs=x_specs
        + [ada_spec(r) for r in range(FFN_SUBTILES)] + [
            pl.BlockSpec((None, None, 1, D_MODEL), lambda i, f: (l, j, 0, 0)),
            pl.BlockSpec((None, None, D_MODEL, FF_TILE), lambda i, f: (l, j, 0, f)),
            pl.BlockSpec((None, None, D_MODEL, FF_TILE), lambda i, f: (l, j, 0, nf + f)),
            pl.BlockSpec((None, None, FF_TILE, D_MODEL), lambda i, f: (l, j, f, 0)),
        ],
        out_specs=pl.BlockSpec((rows, D_MODEL), lambda i, f: (i, 0)),
        out_shape=jax.ShapeDtypeStruct((N_TOK, D_MODEL), F32),
        scratch_shapes=[pltpu.VMEM((rows, D_MODEL), BF16)],
        compiler_params=_cparams(("parallel", "arbitrary")),
        name="ffn",
    )(*xs, *([ada] * FFN_SUBTILES), norm_ffn.reshape(DEPTH, 2, 1, D_MODEL), w_gu, w_gu, w_down)


IN_TILE = 512


def _inproj_kernel(x_ref, ada_ref, g_ref, w1, w2, w3, w4, o1, o2, o3, o4):
    y = _rms(x_ref[...], g_ref[...])
    h = (y * (1.0 + ada_ref[4:5, :]) + ada_ref[3:4, :]).astype(BF16)
    for w, o in ((w1, o1), (w2, o2), (w3, o3), (w4, o4)):
        o[...] = jnp.dot(h, w[...], preferred_element_type=F32)


def _inproj_call(x, ada, norm_mix, ws, l):
    widths = (HG_COLS, HY_COLS, MLA_PAD, GD_PAD)
    per = ROW_TILE // IN_TILE
    return pl.pallas_call(
        _inproj_kernel,
        grid=(N_TOK // IN_TILE,),
        in_specs=[
            pl.BlockSpec((IN_TILE, D_MODEL), lambda i: (i, 0)),
            pl.BlockSpec((None, None, N_ADA, D_MODEL), lambda i: (l, _cond_of_tile(i // per), 0, 0)),
            pl.BlockSpec((None, 1, D_MODEL), lambda i: (l, 0, 0)),
        ] + [pl.BlockSpec((None, D_MODEL, w), lambda i: (l, 0, 0)) for w in widths],
        out_specs=[pl.BlockSpec((IN_TILE, w), lambda i: (i, 0)) for w in widths],
        out_shape=[jax.ShapeDtypeStruct((N_TOK, w), F32) for w in widths],
        compiler_params=_cparams(("parallel",)),
        name="inproj",
    )(x, ada, norm_mix.reshape(DEPTH, 1, D_MODEL), *ws)


OUT_TILE = 512


def _outproj_kernel(x_ref, ada_ref, w_ref, *refs):
    o_ref = refs[-1]
    i = pl.program_id(0)
    n_p = N_PROMPT // OUT_TILE

    def run(srcs):
        acc = jnp.zeros((OUT_TILE, D_MODEL), F32)
        for g, s in enumerate(srcs):
            acc += jnp.dot(s[...].astype(BF16), w_ref[g * GROUP_W:(g + 1) * GROUP_W, :],
                           preferred_element_type=F32)
        o_ref[...] = x_ref[...] + ada_ref[5:6, :] * acc

    @pl.when(i < n_p)
    def _():
        run(refs[0:4])

    @pl.when(i >= n_p)
    def _():
        run(refs[4:8])


def _outproj_call(x, ada, w_out_bf, o_p, o_s, l):
    per = ROW_TILE // OUT_TILE
    n_p = N_PROMPT // OUT_TILE
    n_s = N_SAMPLE // OUT_TILE
    return pl.pallas_call(
        _outproj_kernel,
        grid=(N_TOK // OUT_TILE,),
        in_specs=[
            pl.BlockSpec((OUT_TILE, D_MODEL), lambda i: (i, 0)),
            pl.BlockSpec((None, None, N_ADA, D_MODEL), lambda i: (l, _cond_of_tile(i // per), 0, 0)),
            pl.BlockSpec((None, D_MODEL, D_MODEL), lambda i: (l, 0, 0)),
        ] + [pl.BlockSpec((OUT_TILE, GROUP_W), lambda i: (jnp.minimum(i, n_p - 1), 0))] * 4
          + [pl.BlockSpec((OUT_TILE, GROUP_W), lambda i: (jnp.clip(i - n_p, 0, n_s - 1), 0))] * 4,
        out_specs=pl.BlockSpec((OUT_TILE, D_MODEL), lambda i: (i, 0)),
        out_shape=jax.ShapeDtypeStruct((N_TOK, D_MODEL), F32),
        compiler_params=_cparams(("parallel",)),
        name="outproj",
    )(x, ada, w_out_bf, *o_p, *o_s)


def _block_diag_ones():
    idx = np.arange(GROUP_W) // HEAD_W
    return (idx[:, None] == idx[None, :]).astype(np.float32)


def _hgrn_consts():
    C = CHUNK
    i = np.arange(C)[:, None]
    j = np.arange(C)[None, :]
    masks = []
    s = C // 2
    while s >= 1:
        up_i = (i // s) % 2 == 1
        up_j = (j // s) % 2 == 1
        masks.append(up_i & (~up_j) & (i // (2 * s) == j // (2 * s)))
        s //= 2
    masks.append(j <= i)
    fwd_m = np.stack([m.astype(np.float32) for m in masks])
    bwd_m = np.stack([m.astype(np.float32)[::-1, ::-1] for m in masks])
    tril = np.stack([(j <= i), (j >= i)]).astype(np.float32)
    return tril, np.tile(np.stack([fwd_m, bwd_m]), (1, 1, 1, 2))


def _gdn_consts():
    C = CHUNK
    i = np.arange(C)[:, None]
    t = np.arange(C)[None, :]
    tril = np.stack([(t <= i), (t >= i)]).astype(np.float32)
    masks = np.stack([np.stack([(t <= i), (t < i)]), np.stack([(t >= i), (t > i)])]).astype(np.float32)
    expand = np.zeros((2, LANE, 2 * GROUP_W), np.float32)
    for d in range(2):
        for h in range(N_HEADS):
            expand[d, d * N_HEADS + h, h * HEAD_W:(h + 1) * HEAD_W] = 1.0
            expand[d, 8 + d * N_HEADS + h, GROUP_W + h * HEAD_W:GROUP_W + (h + 1) * HEAD_W] = 1.0
    return tril, masks, expand


def _dft_consts(T):
    n2 = 4 * T
    k = np.arange(T, dtype=np.int64)[:, None]
    s = np.arange(T, dtype=np.int64)[None, :]
    ang = np.pi * (((2 * k + 1) * s) % n2).astype(np.float64) / (2 * T)
    fwd = np.concatenate([np.cos(ang), -np.sin(ang)], axis=0)
    inv = fwd.T / T
    return fwd.astype(np.float32), inv.astype(np.float32)


def _np_split2(x):
    hi = jnp.asarray(x, F32).astype(BF16)
    lo = (jnp.asarray(x, F32) - hi.astype(F32)).astype(BF16)
    return hi, lo


def _hyena_pos_consts(T):
    pos = np.arange(T, dtype=np.float32)
    t = pos / np.float32(T - 1)
    bands = np.linspace(1e-4, (HY_EMB - 1) // 2 - 1, (HY_EMB - 1) // 2, dtype=np.float32)
    ang = (np.float32(2.0 * math.pi / T) * pos[:, None]) * bands[None, :]
    z = np.concatenate([t[:, None], np.cos(ang), -np.sin(ang)], axis=-1).astype(np.float32)
    zp = np.zeros((T, LANE), np.float32)
    zp[:, :HY_EMB] = z
    max_decay = math.log(HY_TARGET) / HY_FAST
    min_decay = math.log(HY_TARGET) / HY_SLOW
    deltas = np.linspace(min_decay, max_decay, GROUP_W, dtype=np.float32)
    window = np.exp(-t[:, None] * np.abs(deltas)[None, :]).astype(np.float32)
    return zp, window


def _rope_consts(T):
    rows = T // GRID_W
    row = np.repeat(np.arange(rows, dtype=np.float32), GRID_W)
    col = (np.arange(T) % GRID_W).astype(np.float32)
    pairs = MLA_ROPE // 4
    inv = (np.float32(ROPE_BASE) ** (-np.arange(pairs, dtype=np.float32) / np.float32(pairs))).astype(np.float32)
    ang = np.concatenate([row[:, None] * inv, col[:, None] * inv], axis=-1).astype(np.float32)
    cos, sin = np.cos(ang), np.sin(ang)
    cosf = np.ones((T, LANE), np.float32)
    sinf = np.zeros((T, LANE), np.float32)
    half = MLA_ROPE // 2
    cosf[:, MLA_NOPE:MLA_NOPE + half] = cos
    cosf[:, MLA_NOPE + half:MLA_QK] = cos
    sinf[:, MLA_NOPE:MLA_NOPE + half] = -sin
    sinf[:, MLA_NOPE + half:MLA_QK] = sin
    return cosf, sinf


def _head_norm_gate(tot, bd, gn, gate):
    ms = _sel_dot_right(tot * tot, bd) * (1.0 / HEAD_W)
    return tot * lax.rsqrt(ms + RMS_EPS) * gn * _silu(gate)


def _sel_dot_right(x, c):
    h, l = _split2(x)
    return jnp.dot(h, c, preferred_element_type=F32) + jnp.dot(l, c, preferred_element_type=F32)


def _block_ref(b, two_s, r):
    C, W = b.shape
    if two_s % 8 == 0:
        b3 = b.reshape(C // two_s, two_s, W)
        return jnp.broadcast_to(b3[:, r:r + 1, :], b3.shape).reshape(C, W)
    pos = lax.broadcasted_iota(jnp.int32, b.shape, 0) % two_s
    out = b
    for p in range(two_s):
        if p != r:
            out = jnp.where(pos == p, pltpu.roll(b, (p - r) % C, 0), out)
    return out


N_PAIRS = N_HEADS // 2
HG_GROUP = 4
HG_DIRECT_MAX = 80.0


def _pair_blockdiag(x):
    lane = lax.broadcasted_iota(jnp.int32, x.shape, 1)
    zero = jnp.zeros_like(x)
    return jnp.concatenate([jnp.where(lane < HEAD_W, x, zero), jnp.where(lane >= HEAD_W, x, zero)], axis=0)


def _hgrn_kernel(*refs, T, has_s0):
    if has_s0:
        (u_ref, lb_ref, gn_ref, tril_ref, lmask_ref, bd_ref, s0_ref,
         o_ref, sfin_ref, oi_s, qin_s, up_s, dc_s, st_s) = refs
    else:
        (u_ref, lb_ref, gn_ref, tril_ref, lmask_ref, bd_ref, _,
         o_ref, sfin_ref, oi_s, qin_s, up_s, dc_s, st_s) = refs
    n = T // CHUNK
    C = CHUNK
    bd = bd_ref[...]
    n_lv = int(math.log2(C))

    log_lb = [jnp.log(lb_ref[d]) for d in range(2)]
    log_1mlb = [jnp.log(1.0 - lb_ref[d]) for d in range(2)]

    def gates(rows, d):
        z = u_ref[rows, (3 + d) * GROUP_W:(4 + d) * GROUP_W]
        t = jnp.exp(-jnp.abs(z))
        log_sig = jnp.minimum(z, 0.0) - jnp.log(1.0 + t)
        c = log_1mlb[d] + log_sig
        m = jnp.maximum(log_lb[d], c)
        lf = m + jnp.log(1.0 + jnp.exp(jnp.minimum(log_lb[d], c) - m))
        sig_neg = jnp.where(z > 0.0, t, 1.0) / (1.0 + t)
        return lf, (1.0 - lb_ref[d]) * sig_neg

    for d in range(2):
        if has_s0:
            st_s[d] = jnp.concatenate([s0_ref[d, h].T for h in range(N_HEADS)], axis=-1)
        else:
            st_s[d] = jnp.zeros((HEAD_W, GROUP_W), F32)

    def prepare(it, carry):
        units = [(c, d) for c in range(HG_GROUP) for d in range(2)]
        rows = [pl.ds(pl.multiple_of((it * HG_GROUP + c) * C, C), C) for c in range(HG_GROUP)]
        arow = [pl.ds(pl.multiple_of((it * HG_GROUP + c) * 8, 8), 8) for c in range(HG_GROUP)]
        q = [u_ref[rows[c], 0:GROUP_W] * (HEAD_W ** -0.5) for c in range(HG_GROUP)]
        v = [u_ref[rows[c], GROUP_W:2 * GROUP_W] for c in range(HG_GROUP)]
        vt = [[jnp.concatenate([v[c][:, h * HEAD_W:(h + 1) * HEAD_W].T for h in (2 * p, 2 * p + 1)],
                               axis=-1).astype(BF16) for p in range(N_PAIRS)]
              for c in range(HG_GROUP)]
        v_bd = [[_pair_blockdiag(v[c][:, p * LANE:(p + 1) * LANE].astype(BF16)) for p in range(N_PAIRS)]
                for c in range(HG_GROUP)]
        lf, ks = zip(*[gates(rows[c], d) for c, d in units])
        bs = []
        for i, (c, d) in enumerate(units):
            hi, lo = _split2(lf[i])
            tril = tril_ref[d]
            bs.append(jnp.dot(tril, hi, preferred_element_type=F32) + jnp.dot(tril, lo, preferred_element_type=F32))
        tot = [jnp.sum(x, axis=0, keepdims=True) for x in lf]
        mid_row = [C // 2 - 1 if d == 0 else C // 2 for c, d in units]
        spread = [jnp.maximum(jnp.abs(bs[i][0:1] - bs[i][r:r + 1]), jnp.abs(bs[i][C - 1:C] - bs[i][r:r + 1]))
                  for i, r in enumerate(mid_row)]
        widest = functools.reduce(jnp.maximum, spread)
        ko = [(ks[i] * jnp.exp(tot[i] - bs[i])).astype(BF16) for i in range(len(units))]
        up = [[jnp.dot(vt[c][p], _pair_blockdiag(ko[i][:, p * LANE:(p + 1) * LANE]), preferred_element_type=F32)
               for p in range(N_PAIRS)] for i, (c, d) in enumerate(units)]
        for i, (c, d) in enumerate(units):
            qin_s[d, rows[c], :] = q[c] * jnp.exp(bs[i])
            up_s[d, rows[c], :] = jnp.concatenate(up[i], axis=-1)
            dc_s[d, arow[c], :] = jnp.broadcast_to(jnp.exp(tot[i]), (8, GROUP_W))

        def masked_scores(qe, ke, lv):
            out = []
            for i, (c, d) in enumerate(units):
                per_pair = []
                for p in range(N_PAIRS):
                    sl = slice(p * LANE, (p + 1) * LANE)
                    prod = lax.dot_general(qe[i][:, sl], _pair_blockdiag(ke[i][:, sl]), (((1,), (1,)), ((), ())),
                                           preferred_element_type=F32)
                    per_pair.append(jnp.where(lmask_ref[d, lv] > 0.5, prod, 0.0))
                out.append(per_pair)
            return out

        def finish(sc):
            return [jnp.concatenate([jnp.dot(sc[i][p].astype(BF16), v_bd[c][p], preferred_element_type=F32)
                                     for p in range(N_PAIRS)], axis=-1) for i, (c, d) in enumerate(units)]

        def intra_direct():
            mid = [_block_ref(bs[i], C, C // 2 - 1 if d == 0 else C // 2) for i, (c, d) in enumerate(units)]
            qe = [(q[c] * jnp.exp(bs[i] - mid[i])).astype(BF16) for i, (c, d) in enumerate(units)]
            ke = [(ks[i] * jnp.exp(mid[i] - bs[i])).astype(BF16) for i in range(len(units))]
            return tuple(finish(masked_scores(qe, ke, n_lv)))

        def intra_split():
            sc = [[jnp.zeros((C, LANE), F32) for _ in range(N_PAIRS)] for _ in units]
            s = C // 2
            lv = 0
            while s >= 1:
                e = [jnp.exp(-jnp.abs(bs[i] - _block_ref(bs[i], 2 * s, s - 1 if d == 0 else s)))
                     for i, (c, d) in enumerate(units)]
                part = masked_scores([(q[c] * e[i]).astype(BF16) for i, (c, d) in enumerate(units)],
                                     [(ks[i] * e[i]).astype(BF16) for i in range(len(units))], lv)
                sc = [[sc[i][p] + part[i][p] for p in range(N_PAIRS)] for i in range(len(units))]
                s //= 2
                lv += 1
            fin = finish(sc)
            return tuple(fin[i] + _bdot(q[c] * ks[i], bd) * v[c] for i, (c, d) in enumerate(units))

        oi = lax.cond(jnp.max(widest) < HG_DIRECT_MAX, intra_direct, intra_split)
        for i, (c, d) in enumerate(units):
            oi_s[d, rows[c], :] = oi[i]
        return carry

    lax.fori_loop(0, n // HG_GROUP, prepare, 0)

    def chunk(ci, carry):
        rows = [pl.ds(pl.multiple_of(cidx * C, C), C) for cidx in (ci, n - 1 - ci)]
        decay = [dc_s[d, pl.ds(pl.multiple_of(cidx * 8, 8), 1), :] for d, cidx in ((0, ci), (1, n - 1 - ci))]
        st = [st_s[d] for d in range(2)]
        o_inter = [[lax.dot_general(qin_s[d, rows[d], p * LANE:(p + 1) * LANE].astype(BF16),
                                    _pair_blockdiag(st[d][:, p * LANE:(p + 1) * LANE].astype(BF16)),
                                    (((1,), (1,)), ((), ())), preferred_element_type=F32)
                    for p in range(N_PAIRS)] for d in range(2)]
        for d in range(2):
            st_s[d] = st[d] * decay[d] + up_s[d, rows[d], :]
            oi_s[d, rows[d], :] = oi_s[d, rows[d], :] + jnp.concatenate(o_inter[d], axis=-1)
        return carry

    lax.fori_loop(0, n, chunk, 0, unroll=2)
    o_ref[...] = _head_norm_gate(oi_s[0] + oi_s[1], bd, gn_ref[...], u_ref[:, 2 * GROUP_W:3 * GROUP_W])
    for d in range(2):
        for h in range(N_HEADS):
            sfin_ref[d, h] = st_s[d][:, h * HEAD_W:(h + 1) * HEAD_W].T


def _state_io(in_specs, args, s0, collect, l, nb, per_step=None):
    state_block = (per_step, None, 2, N_HEADS, HEAD_W, HEAD_W)
    if s0 is not None:
        in_specs.append(pl.BlockSpec(state_block, lambda b: (b, l, 0, 0, 0, 0)))
        args.append(s0)
        return (pl.BlockSpec(state_block[:1] + state_block[2:], lambda b: (b, 0, 0, 0, 0)),
                jax.ShapeDtypeStruct((nb, 2, N_HEADS, HEAD_W, HEAD_W), F32), {})
    in_specs.append(pl.BlockSpec(memory_space=pl.ANY))
    args.append(collect)
    return (pl.BlockSpec(state_block, lambda b: (b, l, 0, 0, 0, 0)),
            jax.ShapeDtypeStruct(collect.shape, F32), {len(args) - 1: 1})


def _hgrn_call(u_hg, lb_l, gn, consts, s0, collect, l, T, nb, row0):
    tril, lmask, bd = consts
    tb = row0 // T
    has_s0 = s0 is not None
    in_specs = [
        pl.BlockSpec((T, HG_COLS), lambda b: (tb + b, 0)),
        pl.BlockSpec((None, 2, 1, GROUP_W), lambda b: (l, 0, 0, 0)),
        pl.BlockSpec((None, 1, GROUP_W), lambda b: (l, 0, 0)),
        pl.BlockSpec((2, CHUNK, CHUNK), lambda b: (0, 0, 0)),
        pl.BlockSpec((2, 7, CHUNK, LANE), lambda b: (0, 0, 0, 0)),
        pl.BlockSpec((GROUP_W, GROUP_W), lambda b: (0, 0)),
    ]
    args = [u_hg, lb_l.reshape(DEPTH, 2, 1, GROUP_W), gn.reshape(DEPTH, 1, GROUP_W), tril, lmask, bd]
    state_spec, state_shape, aliases = _state_io(in_specs, args, s0, collect, l, nb)
    seq = pltpu.VMEM((2, T, GROUP_W), F32)
    return pl.pallas_call(
        functools.partial(_hgrn_kernel, T=T, has_s0=has_s0),
        grid=(nb,),
        in_specs=in_specs,
        out_specs=[pl.BlockSpec((T, GROUP_W), lambda b: (b, 0)), state_spec],
        out_shape=[jax.ShapeDtypeStruct((nb * T, GROUP_W), F32), state_shape],
        input_output_aliases=aliases,
        scratch_shapes=[seq, seq, seq,
                        pltpu.VMEM((2, T // CHUNK * 8, GROUP_W), F32), pltpu.VMEM((2, HEAD_W, GROUP_W), F32)],
        compiler_params=_cparams(("parallel",)),
        name="hgrn",
    )(*args)


def _shift_rows(x, T):
    pos = lax.broadcasted_iota(jnp.int32, x.shape, 0) % T
    prev = jnp.where(pos == 0, 0.0, pltpu.roll(x, 1, 0))
    nxt = jnp.where(pos == T - 1, 0.0, pltpu.roll(x, x.shape[0] - 1, 0))
    return prev, nxt


def _conv3(x, w_ref, T):
    prev, nxt = _shift_rows(x, T)
    return prev * w_ref[0:1, :] + x * w_ref[1:2, :] + nxt * w_ref[2:3, :]


GDN_UNROLL = 2


def _solve_unit_lower(systems):
    c2 = 2 * CHUNK
    slabs = [jnp.concatenate([nmat, nmat, rhs], axis=-1) for rhs, nmat in systems]
    steps = int(math.log2(CHUNK))
    for step in range(steps):
        last = step == steps - 1
        nxt = []
        for slab in slabs:
            hi = slab.astype(BF16)
            lo = (slab - hi.astype(F32)).astype(BF16)
            lhs = jnp.concatenate([hi[:, :c2], lo[:, :CHUNK]], axis=-1)
            first = c2 if last else 0
            rhs3 = jnp.concatenate([hi[:, first:], lo[:, first:], hi[:, first:]], axis=0)
            prod = jnp.dot(lhs, rhs3, preferred_element_type=F32)
            if last:
                nxt.append(slab[:, c2:] + prod)
            else:
                nxt.append(jnp.concatenate([prod[:, :c2], slab[:, c2:] + prod[:, c2:]], axis=-1))
        slabs = nxt
    return slabs


def _gdn_kernel(*refs, T, sps, has_s0):
    if has_s0:
        (u_ref, cw_ref, alog_ref, dtb_ref, exp_ref, tril_ref, mask_ref, bd_ref, gn_ref, s0_ref,
         o_ref, sfin_ref, q_s, k_s, v_s, la_s, be_s, uw_s, ww_s, at_s, qin_s, kt_s, al_s, of_s, st_s) = refs
    else:
        (u_ref, cw_ref, alog_ref, dtb_ref, exp_ref, tril_ref, mask_ref, bd_ref, gn_ref, _,
         o_ref, sfin_ref, q_s, k_s, v_s, la_s, be_s, uw_s, ww_s, at_s, qin_s, kt_s, al_s, of_s, st_s) = refs
    n = T // CHUNK
    C = CHUNK
    bd = bd_ref[...]

    qkv = _silu(_conv3(u_ref[:, 0:3 * GROUP_W], cw_ref, T))
    q = qkv[:, 0:GROUP_W]
    k = qkv[:, GROUP_W:2 * GROUP_W]
    q_s[...] = q * lax.rsqrt(_sel_dot_right(q * q, bd) + 1e-6) * (HEAD_W ** -0.5)
    k_s[...] = k * lax.rsqrt(_sel_dot_right(k * k, bd) + 1e-6)
    v_s[...] = qkv[:, 2 * GROUP_W:3 * GROUP_W]

    ab = u_ref[:, 4 * GROUP_W:4 * GROUP_W + LANE]
    xa = ab + dtb_ref[...]
    softplus = jnp.maximum(xa, 0.0) + jnp.log(1.0 + jnp.exp(-jnp.abs(xa)))
    log_a = -jnp.exp(alog_ref[...]) * softplus
    lane = lax.broadcasted_iota(jnp.int32, ab.shape, 1)
    narrow = jnp.where(lane < 8, log_a, _sigmoid(ab))
    for d in range(2):
        wide = _dot_sel(narrow, exp_ref[d])
        la_s[d] = wide[:, 0:GROUP_W]
        be_s[d] = wide[:, GROUP_W:2 * GROUP_W]
        for s in range(sps):
            if has_s0:
                st_s[s, d] = jnp.concatenate([s0_ref[s, d, h] for h in range(N_HEADS)], axis=-1)
            else:
                st_s[s, d] = jnp.zeros((HEAD_W, GROUP_W), F32)

    def prepare(cidx):
        r0 = pl.multiple_of(cidx * C, C)
        rows = pl.ds(r0, C)
        arow = pl.ds(pl.multiple_of(cidx * 8, 8), 8)
        q = q_s[rows, :]
        k = k_s[rows, :]
        v = v_s[rows, :]
        systems = []
        attns = []
        kts = []
        for d in range(2):
            incl = mask_ref[d, 0] > 0.5
            strict = mask_ref[d, 1]
            la = la_s[d, rows, :]
            be = be_s[d, rows, :]
            gx = _sel_dot(tril_ref[d], la)
            gtot = jnp.sum(la, axis=0, keepdims=True)
            eg = jnp.exp(gx)
            kout = k * jnp.exp(gtot - gx)
            qin_s[d, rows, :] = q * eg
            al_s[d, arow, :] = jnp.broadcast_to(jnp.exp(gtot), (8, GROUP_W))
            kb = k * be
            vb = v * be
            kbg = kb * eg
            for h in range(N_HEADS):
                sl = slice(h * HEAD_W, (h + 1) * HEAD_W)
                gh = gx[:, sl]
                dmat = gh - gh.T
                dec = jnp.where(incl, jnp.exp(jnp.where(incl, dmat, 0.0)), 0.0)
                qk = _bdot_nt(jnp.concatenate([kb[:, sl], q[:, sl]], axis=0), k[:, sl])
                nmat = -(qk[:C] * dec * strict)
                systems.append((jnp.concatenate([vb[:, sl], kbg[:, sl]], axis=-1), nmat))
                attns.append(qk[C:] * dec)
                kts.append(kout[:, sl].T)
        sols = _solve_unit_lower(systems)
        for d in range(2):
            mine = sols[d * N_HEADS:(d + 1) * N_HEADS]
            uw_s[d, rows, :] = jnp.concatenate([x[:, :HEAD_W] for x in mine], axis=-1)
            ww_s[d, rows, :] = jnp.concatenate([x[:, HEAD_W:] for x in mine], axis=-1)
            at_s[d, rows, :] = jnp.concatenate(attns[d * N_HEADS:(d + 1) * N_HEADS], axis=-1)
            kt_s[d, rows, :] = jnp.concatenate(kts[d * N_HEADS:(d + 1) * N_HEADS], axis=-1)

    def prep_body(i, carry):
        for j in range(GDN_UNROLL):
            prepare(i * GDN_UNROLL + j)
        return carry

    lax.fori_loop(0, sps * n // GDN_UNROLL, prep_body, 0)

    def chunk(ci, carry):
        scans = [(s, d) for s in range(sps) for d in range(2)]
        units = [(j, p) for j in range(len(scans)) for p in range(N_PAIRS)]
        rows, alast = [], []
        for s, d in scans:
            cidx = s * n + (ci if d == 0 else n - 1 - ci)
            rows.append(pl.ds(pl.multiple_of(cidx * C, C), C))
            alast.append(al_s[d, pl.ds(pl.multiple_of(cidx * 8, 8), 1), :])
        st = [st_s[s, d] for s, d in scans]
        dirs = [d for s, d in scans]
        lanes = [slice(p * LANE, (p + 1) * LANE) for p in range(N_PAIRS)]
        both = [jnp.dot(jnp.concatenate([ww_s[dirs[j], rows[j], lanes[p]], qin_s[dirs[j], rows[j], lanes[p]]],
                                        axis=0).astype(BF16),
                        _pair_blockdiag(st[j][:, lanes[p]].astype(BF16)), preferred_element_type=F32)
                for j, p in units]
        vnew = [uw_s[dirs[j], rows[j], lanes[p]] - both[i][:C] for i, (j, p) in enumerate(units)]
        upd = [jnp.dot(jnp.concatenate([at_s[dirs[j], rows[j], lanes[p]], kt_s[dirs[j], rows[j], lanes[p]]],
                                       axis=0).astype(BF16),
                       _pair_blockdiag(vnew[i].astype(BF16)), preferred_element_type=F32)
               for i, (j, p) in enumerate(units)]
        for j, (s, d) in enumerate(scans):
            idx = range(j * N_PAIRS, (j + 1) * N_PAIRS)
            of_s[d, rows[j], :] = jnp.concatenate([both[i][C:] + upd[i][:C] for i in idx], axis=-1)
            st_s[s, d] = st[j] * alast[j] + jnp.concatenate([upd[i][C:] for i in idx], axis=-1)
        return carry

    lax.fori_loop(0, n, chunk, 0)
    o_ref[...] = _head_norm_gate(of_s[0] + of_s[1], bd, gn_ref[...], u_ref[:, 3 * GROUP_W:4 * GROUP_W])
    for s in range(sps):
        for d in range(2):
            for h in range(N_HEADS):
                sfin_ref[s, d, h] = st_s[s, d][:, h * HEAD_W:(h + 1) * HEAD_W]


GDN_STEP_ROWS = 1024


def _gdn_call(u_gd, cw, alog, dtb, gn, consts, s0, collect, l, T, nb, row0):
    tril, masks, expand, bd = consts
    sps = max(1, GDN_STEP_ROWS // T)
    rows = sps * T
    tb = row0 // rows
    has_s0 = s0 is not None
    in_specs = [
        pl.BlockSpec((rows, GD_PAD), lambda b: (tb + b, 0)),
        pl.BlockSpec((None, 3, 3 * GROUP_W), lambda b: (l, 0, 0)),
        pl.BlockSpec((None, 1, LANE), lambda b: (l, 0, 0)),
        pl.BlockSpec((None, 1, LANE), lambda b: (l, 0, 0)),
        pl.BlockSpec((2, LANE, 2 * GROUP_W), lambda b: (0, 0, 0)),
        pl.BlockSpec((2, CHUNK, CHUNK), lambda b: (0, 0, 0)),
        pl.BlockSpec((2, 2, CHUNK, CHUNK), lambda b: (0, 0, 0, 0)),
        pl.BlockSpec((GROUP_W, GROUP_W), lambda b: (0, 0)),
        pl.BlockSpec((None, 1, GROUP_W), lambda b: (l, 0, 0)),
    ]
    args = [u_gd, cw, alog, dtb, expand, tril, masks, bd, gn]
    state_spec, state_shape, aliases = _state_io(in_specs, args, s0, collect, l, nb, per_step=sps)
    seq = pltpu.VMEM((2, rows, GROUP_W), F32)
    return pl.pallas_call(
        functools.partial(_gdn_kernel, T=T, sps=sps, has_s0=has_s0),
        grid=(nb // sps,),
        in_specs=in_specs,
        out_specs=[pl.BlockSpec((rows, GROUP_W), lambda b: (b, 0)), state_spec],
        out_shape=[jax.ShapeDtypeStruct((nb * T, GROUP_W), F32), state_shape],
        input_output_aliases=aliases,
        scratch_shapes=[pltpu.VMEM((rows, GROUP_W), F32)] * 3 + [seq, seq, seq, seq, seq, seq, seq,
            pltpu.VMEM((2, rows // CHUNK * 8, GROUP_W), F32), seq, pltpu.VMEM((sps, 2, HEAD_W, GROUP_W), F32)],
        compiler_params=_cparams(("parallel",)),
        name="gdn",
    )(*args)


def _hyfilt_kernel(z_ref, win_ref, fh_ref, fl_ref, w1_ref, b1_ref, fr_ref, w2_ref, b2_ref, w3_ref, o_ref, *, T):
    fr = fr_ref[...]
    h = jnp.sin(fr * (_dot3(z_ref[...], w1_ref[...]) + b1_ref[...]))
    h = jnp.sin(fr * (_dot3(h, w2_ref[...]) + b2_ref[...]))
    h = _dot3(h, w3_ref[...])
    win = win_ref[...]
    hf = h[:, 0:GROUP_W] * win
    hb = h[:, GROUP_W:2 * GROUP_W] * win
    row = lax.broadcasted_iota(jnp.int32, hb.shape, 0)
    hb = jnp.where(row == 0, 0.0, hb)
    for rows, taps in ((slice(0, T), (hf + hb).astype(BF16)), (slice(T, 2 * T), (hf - hb).astype(BF16))):
        o_ref[rows, :] = (jnp.dot(fh_ref[rows, :], taps, preferred_element_type=F32)
                          + jnp.dot(fl_ref[rows, :], taps, preferred_element_type=F32))


def _hyfilt_call(T, zp, win, fh, fl, w1p, b1, freq, w2, b2, w3):
    c2 = lambda l: (0, 0)
    return pl.pallas_call(
        functools.partial(_hyfilt_kernel, T=T),
        grid=(DEPTH,),
        in_specs=[
            pl.BlockSpec((T, LANE), c2),
            pl.BlockSpec((T, GROUP_W), c2),
            pl.BlockSpec((2 * T, T), c2),
            pl.BlockSpec((2 * T, T), c2),
            pl.BlockSpec((None, LANE, HY_FH), lambda l: (l, 0, 0)),
            pl.BlockSpec((None, 1, HY_FH), lambda l: (l, 0, 0)),
            pl.BlockSpec((None, 1, HY_FH), lambda l: (l, 0, 0)),
            pl.BlockSpec((None, HY_FH, HY_FH), lambda l: (l, 0, 0)),
            pl.BlockSpec((None, 1, HY_FH), lambda l: (l, 0, 0)),
            pl.BlockSpec((None, HY_FH, 2 * GROUP_W), lambda l: (l, 0, 0)),
        ],
        out_specs=pl.BlockSpec((None, 2 * T, GROUP_W), lambda l: (l, 0, 0)),
        out_shape=jax.ShapeDtypeStruct((DEPTH, 2 * T, GROUP_W), F32),
        compiler_params=_cparams(("parallel",)),
        name="hyfilt",
    )(zp, win, fh, fl, w1p, b1, freq, w2, b2, w3)


def _hyena_kernel(u_ref, cw_ref, cb_ref, spec_ref, skip_ref, fh_ref, fl_ref, ih_ref, il_ref, o_ref, *, T):
    uc = _conv3(u_ref[...], cw_ref, T) + cb_ref[...]
    x0 = uc[:, 0:GROUP_W]
    z = uc[:, GROUP_W:2 * GROUP_W] * uc[:, 2 * GROUP_W:3 * GROUP_W]
    zb = z.astype(BF16)
    zs = (jnp.dot(fh_ref[...], zb, preferred_element_type=F32)
          + jnp.dot(fl_ref[...], zb, preferred_element_type=F32))
    ar, ai = zs[0:T], zs[T:2 * T]
    br, bi = spec_ref[0:T, :], spec_ref[T:2 * T, :]
    pb = jnp.concatenate([ar * br - ai * bi, ar * bi + ai * br], axis=0).astype(BF16)
    y = (jnp.dot(ih_ref[...], pb, preferred_element_type=F32)
         + jnp.dot(il_ref[...], pb, preferred_element_type=F32))
    o_ref[...] = x0 * (y + z * skip_ref[...])


def _hyena_call(u_hy, cw, cb, spec, skip, dft, l, T, nb, row0):
    fh, fl, ih, il = dft
    tb = row0 // T
    c2 = lambda b: (0, 0)
    return pl.pallas_call(
        functools.partial(_hyena_kernel, T=T),
        grid=(nb,),
        in_specs=[
            pl.BlockSpec((T, HY_COLS), lambda b: (tb + b, 0)),
            pl.BlockSpec((None, 3, HY_COLS), lambda b: (l, 0, 0)),
            pl.BlockSpec((None, 1, HY_COLS), lambda b: (l, 0, 0)),
            pl.BlockSpec((None, 2 * T, GROUP_W), lambda b: (l, 0, 0)),
            pl.BlockSpec((None, 1, GROUP_W), lambda b: (l, 0, 0)),
            pl.BlockSpec((2 * T, T), c2),
            pl.BlockSpec((2 * T, T), c2),
            pl.BlockSpec((T, 2 * T), c2),
            pl.BlockSpec((T, 2 * T), c2),
        ],
        out_specs=pl.BlockSpec((T, GROUP_W), lambda b: (b, 0)),
        out_shape=jax.ShapeDtypeStruct((nb * T, GROUP_W), F32),
        compiler_params=_cparams(("parallel",)),
        name="hyena",
    )(u_hy, cw, cb, spec, skip, fh, fl, ih, il)


def _rope(x, cosf, sinf):
    lane = lax.broadcasted_iota(jnp.int32, x.shape, 1)
    half = MLA_ROPE // 2
    partner = jnp.where(lane < MLA_NOPE + half, pltpu.roll(x, LANE - half, 1), pltpu.roll(x, half, 1))
    return x * cosf + partner * sinf


def _qk_norm(x, g):
    ms = jnp.sum(x * x, axis=-1, keepdims=True) * (1.0 / MLA_QK)
    return x * lax.rsqrt(ms + RMS_EPS) * g


def _mla_kernel(*refs, T, ctx):
    if ctx:
        (u_ref, qn_ref, wq_ref, kvn_ref, wkv_ref, qkn_ref, cos_ref, sin_ref, cckv_ref, ckr_ref, o_ref,
         q_s, k_s, v_s) = refs
    else:
        (u_ref, qn_ref, wq_ref, kvn_ref, wkv_ref, qkn_ref, _, _, o_ref, ckv_ref, kr_ref, q_s, k_s, v_s) = refs
    n_keys = k_s.shape[1]
    u = u_ref[...]
    cq = _rms(u[:, 0:MLA_Q_LORA], qn_ref[...])
    ckv = _rms(u[:, MLA_Q_LORA:MLA_Q_LORA + MLA_KV_LORA], kvn_ref[...])
    kr = u[:, MLA_Q_LORA + MLA_KV_LORA:MLA_Q_LORA + MLA_KV_LORA + MLA_ROPE]
    if not ctx:
        ckv_ref[...] = ckv
        kr_ref[...] = kr
    q_all = _bdot(cq, wq_ref[...])
    kv = _bdot(ckv, wkv_ref[...])
    gq = qkn_ref[0:1, :]
    gk = qkn_ref[1:2, :]
    if ctx:
        kvc = _bdot(cckv_ref[...], wkv_ref[...])
        krc = ckr_ref[...]
        cosf, sinf = cos_ref[...], sin_ref[...]
    q_scale = MLA_QK ** -0.5 * math.log2(math.e)
    kr_tile = jnp.concatenate([jnp.zeros((T, MLA_NOPE), F32), kr, jnp.zeros((T, LANE - MLA_QK), F32)], axis=-1)
    kr_rot = kr_tile * gk
    if ctx:
        kr_rot = _rope(kr_rot, cosf, sinf)
    nope_lane = lax.broadcasted_iota(jnp.int32, (T, LANE), 1) < MLA_NOPE
    for h in range(N_HEADS):
        qh = _qk_norm(q_all[:, h * LANE:(h + 1) * LANE], gq)
        k_nope = jnp.concatenate([kv[:, h * HEAD_W:(h + 1) * HEAD_W], jnp.zeros((T, LANE - MLA_NOPE), F32)], axis=-1)
        ms = jnp.sum(k_nope * k_nope + kr_tile * kr_tile, axis=-1, keepdims=True) * (1.0 / MLA_QK)
        kh = jnp.where(nope_lane, k_nope * gk, kr_rot) * lax.rsqrt(ms + RMS_EPS)
        if ctx:
            qh = _rope(qh, cosf, sinf)
            zc = jnp.zeros((n_keys - T, LANE - MLA_QK), F32)
            kc = _qk_norm(jnp.concatenate([kvc[:, h * HEAD_W:(h + 1) * HEAD_W], krc, zc], axis=-1), gk)
            k_s[h, T:n_keys, :] = kc.astype(BF16)
        q_s[:, h * LANE:(h + 1) * LANE] = (qh * q_scale).astype(BF16)
        k_s[h, 0:T, :] = kh.astype(BF16)
    for p in range(N_PAIRS):
        lanes = slice(GROUP_W + p * LANE, GROUP_W + (p + 1) * LANE)
        vp = kv[:, lanes]
        if ctx:
            vp = jnp.concatenate([vp, kvc[:, lanes]], axis=0)
        v_s[p] = _pair_blockdiag(vp.astype(BF16))

    def q_block(qb, carry):
        rows = pl.ds(pl.multiple_of(qb * ATT_QBLOCK, ATT_QBLOCK), ATT_QBLOCK)
        lane = lax.broadcasted_iota(jnp.int32, (ATT_QBLOCK, LANE), 1)
        for p in range(N_PAIRS):
            es, sums = [], []
            for h in (2 * p, 2 * p + 1):
                s = lax.dot_general(q_s[rows, h * LANE:(h + 1) * LANE], k_s[h], (((1,), (1,)), ((), ())),
                                    preferred_element_type=F32)
                e = jnp.exp2(s - jnp.max(s, axis=-1, keepdims=True))
                sums.append(jnp.sum(e, axis=-1, keepdims=True))
                es.append(e.astype(BF16))
            o = jnp.dot(jnp.concatenate(es, axis=-1), v_s[p], preferred_element_type=F32)
            o_ref[rows, p * LANE:(p + 1) * LANE] = o / jnp.where(lane < HEAD_W, sums[0], sums[1])
        return carry

    lax.fori_loop(0, T // ATT_QBLOCK, q_block, 0)


def _mla_call(u_mla, qn, wq, kvn, wkv, qkn, rope, cache, collect, l, T, nb, row0):
    tb = row0 // T
    ctx = cache is not None
    n_keys = T + (PAST_LEN if ctx else 0)
    c2 = lambda b: (0, 0)
    in_specs = [
        pl.BlockSpec((T, MLA_PAD), lambda b: (tb + b, 0)),
        pl.BlockSpec((None, 1, MLA_Q_LORA), lambda b: (l, 0, 0)),
        pl.BlockSpec((None, MLA_Q_LORA, N_HEADS * LANE), lambda b: (l, 0, 0)),
        pl.BlockSpec((None, 1, MLA_KV_LORA), lambda b: (l, 0, 0)),
        pl.BlockSpec((None, MLA_KV_LORA, 2 * GROUP_W), lambda b: (l, 0, 0)),
        pl.BlockSpec((None, 2, LANE), lambda b: (l, 0, 0)),
    ]
    args = [u_mla, qn, wq, kvn, wkv, qkn]
    out_specs = [pl.BlockSpec((T, GROUP_W), lambda b: (b, 0))]
    out_shape = [jax.ShapeDtypeStruct((nb * T, GROUP_W), F32)]
    if ctx:
        in_specs += [
            pl.BlockSpec((T, LANE), c2),
            pl.BlockSpec((T, LANE), c2),
            pl.BlockSpec((None, None, PAST_LEN, MLA_KV_LORA), lambda b: (b, l, 0, 0)),
            pl.BlockSpec((None, None, PAST_LEN, MLA_ROPE), lambda b: (b, l, 0, 0)),
        ]
        args += [rope[0], rope[1], cache[0], cache[1]]
        aliases = {}
    else:
        in_specs += [pl.BlockSpec(memory_space=pl.ANY)] * 2
        aliases = {len(args): 1, len(args) + 1: 2}
        args += list(collect)
        out_specs += [pl.BlockSpec((None, None, T, MLA_KV_LORA), lambda b: (b, l, 0, 0)),
                      pl.BlockSpec((None, None, T, MLA_ROPE), lambda b: (b, l, 0, 0))]
        out_shape += [jax.ShapeDtypeStruct(a.shape, F32) for a in collect]
    return pl.pallas_call(
        functools.partial(_mla_kernel, T=T, ctx=ctx),
        grid=(nb,),
        in_specs=in_specs,
        out_specs=out_specs,
        out_shape=out_shape,
        input_output_aliases=aliases,
        scratch_shapes=[pltpu.VMEM((T, N_HEADS * LANE), BF16), pltpu.VMEM((N_HEADS, n_keys, LANE), BF16),
                        pltpu.VMEM((N_PAIRS, 2 * n_keys, LANE), BF16)],
        compiler_params=_cparams(("parallel",)),
        name="mla",
    )(*args)


def _pad_cols(w, width):
    return jnp.pad(w, [(0, 0)] * (w.ndim - 1) + [(0, width - w.shape[-1])])


W_IN_PREP_COLS = 256


def _prep_w_in_kernel(wt_ref, o_hg, o_hy, o_mla, o_gd):
    start = 0
    for o_ref, cols in ((o_hg, HG_COLS), (o_hy, HY_COLS), (o_mla, MLA_COLS), (o_gd, GD_COLS)):
        width = o_ref.shape[-1]
        for c0 in range(0, width, W_IN_PREP_COLS):
            n_out = min(W_IN_PREP_COLS, width - c0)
            n_real = max(0, min(n_out, cols - c0))
            piece = wt_ref[start + c0:start + c0 + n_real, :]
            if n_real < n_out:
                piece = jnp.concatenate([piece, jnp.zeros((n_out - n_real, D_MODEL), F32)], axis=0)
            o_ref[:, c0:c0 + n_out] = piece.T.astype(BF16)
        start += cols


def _prep_w_in(w_in):
    widths = (HG_COLS, HY_COLS, MLA_PAD, GD_PAD)
    n_cols = w_in.shape[-1]
    return pl.pallas_call(
        _prep_w_in_kernel,
        grid=(DEPTH,),
        in_specs=[pl.BlockSpec((None, n_cols, D_MODEL), lambda l: (l, 0, 0))],
        out_specs=[pl.BlockSpec((None, D_MODEL, w), lambda l: (l, 0, 0)) for w in widths],
        out_shape=[jax.ShapeDtypeStruct((DEPTH, D_MODEL, w), BF16) for w in widths],
        compiler_params=_cparams(("parallel",)),
        name="w_in_prep",
    )(jnp.swapaxes(w_in, 1, 2))


def _prep_wq(w_q_up):
    w = w_q_up.reshape(DEPTH, MLA_Q_LORA, N_HEADS, MLA_QK)
    return _pad_cols(w, LANE).reshape(DEPTH, MLA_Q_LORA, N_HEADS * LANE).astype(BF16)


def _prep_wkv(w_kv_up):
    w = w_kv_up.reshape(DEPTH, MLA_KV_LORA, N_HEADS, 2, HEAD_W)
    return w.transpose(0, 1, 3, 2, 4).reshape(DEPTH, MLA_KV_LORA, 2 * GROUP_W).astype(BF16)


def _lower_bounds(hgrn_lb):
    lb = jnp.cumsum(jax.nn.softmax(hgrn_lb.astype(F32), axis=0), axis=0)
    return lb - lb[0]


def kernel(x_prompt, x_sample, cache_mla_ckv, cache_mla_krope, state_hgrn, state_gdn, c, c_ctx, w_ada, b_ada, norm_ffn, w_ffn_gu, w_ffn_down, norm_mix, w_in, w_out, hgrn_lb, hgrn_norm, hy_conv_w, hy_conv_b, hy_w1, hy_b1, hy_freq, hy_w2, hy_b2, hy_w3, hy_skip, mla_q_norm_a, mla_w_q_up, mla_kv_norm_a, mla_w_kv_up, mla_qk_norm, gdn_conv_w, gdn_a_log, gdn_dt_bias, gdn_norm):
    x = (x_prompt.reshape(N_PROMPT, D_MODEL), x_sample.reshape(N_SAMPLE, D_MODEL))

    cond8 = jnp.zeros((8, D_MODEL), F32).at[0].set(c_ctx).at[1:1 + DEC_BATCH].set(c)
    ada = _ada_call(cond8, w_ada, b_ada)

    w_in_parts = _prep_w_in(w_in)
    w_out_bf = w_out.astype(BF16)
    wq = _prep_wq(mla_w_q_up)
    wkv = _prep_wkv(mla_w_kv_up)
    qkn = _pad_cols(mla_qk_norm, LANE)
    lb_all = _lower_bounds(hgrn_lb)
    alog = _pad_cols(gdn_a_log.reshape(DEPTH, 1, 8), LANE)
    dtb = _pad_cols(gdn_dt_bias.reshape(DEPTH, 1, 8), LANE)
    gdn_gn = jnp.tile(gdn_norm, (1, N_HEADS)).reshape(DEPTH, 1, GROUP_W)
    w1p = jnp.pad(hy_w1, ((0, 0), (0, LANE - HY_EMB), (0, 0)))

    bd = jnp.asarray(_block_diag_ones(), BF16)
    hg_tril, hg_m = _hgrn_consts()
    hg_consts = (jnp.asarray(hg_tril, BF16), jnp.asarray(hg_m, F32), bd)
    gd_tril, gd_masks, gd_expand = _gdn_consts()
    gd_consts = (jnp.asarray(gd_tril, BF16), jnp.asarray(gd_masks, F32), jnp.asarray(gd_expand, BF16), bd)
    rope = tuple(jnp.asarray(a) for a in _rope_consts(DEC_SEQ))
    groups = ((SEQ, BATCH, 0), (DEC_SEQ, DEC_BATCH, N_PROMPT))
    dft = {}
    spec = {}
    for T, _, _ in groups:
        fwd, inv = _dft_consts(T)
        fh, fl = _np_split2(fwd)
        ih, il = _np_split2(inv)
        dft[T] = (fh, fl, ih, il)
        zp, win = _hyena_pos_consts(T)
        spec[T] = _hyfilt_call(T, jnp.asarray(zp), jnp.asarray(win), fh, fl, w1p,
                               hy_b1.reshape(DEPTH, 1, HY_FH), hy_freq.reshape(DEPTH, 1, HY_FH), hy_w2,
                               hy_b2.reshape(DEPTH, 1, HY_FH), hy_w3)

    new_ckv = jnp.zeros((BATCH, DEPTH, SEQ, MLA_KV_LORA), F32)
    new_kr = jnp.zeros((BATCH, DEPTH, SEQ, MLA_ROPE), F32)
    new_hg = jnp.zeros((BATCH, DEPTH, 2, N_HEADS, HEAD_W, HEAD_W), F32)
    new_gd = jnp.zeros((BATCH, DEPTH, 2, N_HEADS, HEAD_W, HEAD_W), F32)
    for l in range(DEPTH):
        x = _ffn_call(x, ada, norm_ffn, w_ffn_gu, w_ffn_down, l, 0)
        u_hg, u_hy, u_mla, u_gd = _inproj_call(x, ada, norm_mix, w_in_parts, l)
        outs = []
        for gi, (T, nb, row0) in enumerate(groups):
            latent = gi == 1
            o_hg, s_hg = _hgrn_call(u_hg, lb_all, hgrn_norm, hg_consts, state_hgrn if latent else None,
                                    None if latent else new_hg, l, T, nb, row0)
            o_hy = _hyena_call(u_hy, hy_conv_w, hy_conv_b.reshape(DEPTH, 1, HY_COLS), spec[T],
                               hy_skip.reshape(DEPTH, 1, GROUP_W), dft[T], l, T, nb, row0)
            mla = _mla_call(u_mla, mla_q_norm_a.reshape(DEPTH, 1, MLA_Q_LORA), wq,
                            mla_kv_norm_a.reshape(DEPTH, 1, MLA_KV_LORA), wkv, qkn,
                            rope if latent else None,
                            (cache_mla_ckv, cache_mla_krope) if latent else None,
                            None if latent else (new_ckv, new_kr), l, T, nb, row0)
            o_gd, s_gd = _gdn_call(u_gd, gdn_conv_w, alog, dtb, gdn_gn, gd_consts, state_gdn if latent else None,
                                   None if latent else new_gd, l, T, nb, row0)
            outs.append((o_hg, o_hy, mla[0], o_gd))
            if not latent:
                new_ckv, new_kr, new_hg, new_gd = mla[1], mla[2], s_hg, s_gd
        x = _outproj_call(x, ada, w_out_bf, outs[0], outs[1], l)
        x = _ffn_call(x, ada, norm_ffn, w_ffn_gu, w_ffn_down, l, 1)

    y_prompt = x[:N_PROMPT].reshape(BATCH, SEQ, D_MODEL)
    y_sample = x[N_PROMPT:].reshape(DEC_BATCH, DEC_SEQ, D_MODEL)
    return (y_prompt, y_sample, new_ckv, new_kr, new_hg, new_gd)
```

```python
import functools
import math

import numpy as np
import jax
import jax.numpy as jnp
from jax import lax
from jax.experimental import pallas as pl
from jax.experimental.pallas import tpu as pltpu

F32 = jnp.float32
BF16 = jnp.bfloat16

D_MODEL = 1024
BATCH = 16
SEQ = 256
DEPTH = 4
DEC_BATCH = 2
DEC_SEQ = 1024
PAST_LEN = 256
GRID_W = 64
N_ADA = 9
D_FF = 2816
GROUP_W = 256
CHUNK = 64
RMS_EPS = 1e-6
N_HEADS = 4
HEAD_W = 64
HY_EMB = 33
HY_FH = 64
HY_TARGET = 1e-2
HY_FAST = 0.3
HY_SLOW = 1.5
MLA_NOPE = 64
MLA_ROPE = 32
MLA_QK = MLA_NOPE + MLA_ROPE
MLA_Q_LORA = 256
MLA_KV_LORA = 128
ROPE_BASE = 10000.0

HG_COLS = 5 * GROUP_W
HY_COLS = 3 * GROUP_W
MLA_COLS = MLA_Q_LORA + MLA_KV_LORA + MLA_ROPE
GD_COLS = 4 * GROUP_W + 16
MLA_PAD = 512
GD_PAD = 1152

N_PROMPT = BATCH * SEQ
N_SAMPLE = DEC_BATCH * DEC_SEQ
N_TOK = N_PROMPT + N_SAMPLE
LANE = 128
VMEM_LIMIT = 56 * 1024 * 1024
ROW_TILE = 1024
FF_TILE = 256
ADA_TILE = 1536
ATT_QBLOCK = 256
SCAN_STEP_ROWS = 1024


def _bdot(a, b):
    return jnp.dot(a.astype(BF16), b.astype(BF16), preferred_element_type=F32)


def _bdot_nt(a, b):
    return lax.dot_general(a.astype(BF16), b.astype(BF16), (((1,), (1,)), ((), ())),
                           preferred_element_type=F32)


def _split2(x):
    hi = x.astype(BF16)
    lo = (x - hi.astype(F32)).astype(BF16)
    return hi, lo


def _split3(x):
    hi = x.astype(BF16)
    r = x - hi.astype(F32)
    mid = r.astype(BF16)
    lo = (r - mid.astype(F32)).astype(BF16)
    return hi, mid, lo


def _dot3(a, b):
    ah, al = _split2(a)
    bh, bl = _split2(b)
    return (jnp.dot(ah, bh, preferred_element_type=F32) + jnp.dot(ah, bl, preferred_element_type=F32)
            + jnp.dot(al, bh, preferred_element_type=F32))


def _sel_dot(c, x):
    h, m, l = _split3(x)
    return (jnp.dot(c, h, preferred_element_type=F32) + jnp.dot(c, m, preferred_element_type=F32)
            + jnp.dot(c, l, preferred_element_type=F32))


def _dot_sel(x, c):
    h, m, l = _split3(x)
    return (jnp.dot(h, c, preferred_element_type=F32) + jnp.dot(m, c, preferred_element_type=F32)
            + jnp.dot(l, c, preferred_element_type=F32))


def _sigmoid(x):
    return 1.0 / (1.0 + jnp.exp(-x))


def _silu(x):
    return x * _sigmoid(x)


def _rms(x, g):
    return x * lax.rsqrt(jnp.mean(x * x, axis=-1, keepdims=True) + RMS_EPS) * g


def _cparams(sem):
    return pltpu.CompilerParams(dimension_semantics=sem, vmem_limit_bytes=VMEM_LIMIT)


def _cond_of_tile(i):
    return jnp.maximum(i - (N_PROMPT // ROW_TILE - 1), 0)


def _ada_kernel(c_ref, w_ref, b_ref, o_ref):
    ch, cl = _split2(_silu(c_ref[...]))
    w = w_ref[...].astype(BF16)
    o_ref[...] = (jnp.dot(ch, w, preferred_element_type=F32) + jnp.dot(cl, w, preferred_element_type=F32)
                  + b_ref[...])


def _ada_call(cond8, w_ada, b_ada):
    n = N_ADA * D_MODEL
    out = pl.pallas_call(
        _ada_kernel,
        grid=(DEPTH, n // ADA_TILE),
        in_specs=[
            pl.BlockSpec((8, D_MODEL), lambda l, j: (0, 0)),
            pl.BlockSpec((None, D_MODEL, ADA_TILE), lambda l, j: (l, 0, j)),
            pl.BlockSpec((None, 1, ADA_TILE), lambda l, j: (l, 0, j)),
        ],
        out_specs=pl.BlockSpec((None, 8, ADA_TILE), lambda l, j: (l, 0, j)),
        out_shape=jax.ShapeDtypeStruct((DEPTH, 8, n), F32),
        compiler_params=_cparams(("parallel", "parallel")),
        name="ada",
    )(cond8, w_ada, b_ada.reshape(DEPTH, 1, n))
    return out.reshape(DEPTH, 8, N_ADA, D_MODEL)


FFN_SUBTILES = 2


def _ffn_kernel(*refs, sub, split):
    nx = 1 if split is None else 2
    x_refs = refs[:nx]
    ada_refs = refs[nx:nx + FFN_SUBTILES]
    g_ref, wg_ref, wu_ref, wd_ref, o_ref, h_scr = refs[nx + FFN_SUBTILES:]
    i = pl.program_id(0)
    f = pl.program_id(1)

    def prologue(x_ref):
        for r, ada_ref in enumerate(ada_refs):
            rows = slice(r * ROW_TILE, (r + 1) * ROW_TILE)
            y = _rms(x_ref[rows, :], g_ref[...])
            h = y * (1.0 + ada_ref[3 * sub + 1:3 * sub + 2, :]) + ada_ref[3 * sub:3 * sub + 1, :]
            h_scr[rows, :] = h.astype(BF16)
        o_ref[...] = jnp.zeros_like(o_ref)

    def epilogue(x_ref):
        for r, ada_ref in enumerate(ada_refs):
            rows = slice(r * ROW_TILE, (r + 1) * ROW_TILE)
            o_ref[rows, :] = x_ref[rows, :] + 0.5 * ada_ref[3 * sub + 2:3 * sub + 3, :] * o_ref[rows, :]

    def on(step, fn):
        if split is None:
            pl.when(f == step)(functools.partial(fn, x_refs[0]))
        else:
            pl.when((f == step) & (i < split))(functools.partial(fn, x_refs[0]))
            pl.when((f == step) & (i >= split))(functools.partial(fn, x_refs[1]))

    on(0, prologue)
    wg = wg_ref[...].astype(BF16)
    wu = wu_ref[...].astype(BF16)
    wd = wd_ref[...].astype(BF16)
    for r in range(FFN_SUBTILES):
        rows = slice(r * ROW_TILE, (r + 1) * ROW_TILE)
        h = h_scr[rows, :]
        gate = jnp.dot(h, wg, preferred_element_type=F32)
        up = jnp.dot(h, wu, preferred_element_type=F32)
        a = (_silu(gate) * up).astype(BF16)
        o_ref[rows, :] += jnp.dot(a, wd, preferred_element_type=F32)
    on(pl.num_programs(1) - 1, epilogue)


def _ffn_call(xs, ada, norm_ffn, w_gu, w_down, l, j):
    sub = 2 * j
    nf = D_FF // FF_TILE
    rows = FFN_SUBTILES * ROW_TILE

    def ada_spec(r):
        return pl.BlockSpec((None, None, N_ADA, D_MODEL),
                            lambda i, f: (l, _cond_of_tile(i * FFN_SUBTILES + r), 0, 0))

    if isinstance(xs, tuple):
        split = xs[0].shape[0] // rows
        n_tail = xs[1].shape[0] // rows
        x_specs = [pl.BlockSpec((rows, D_MODEL), lambda i, f: (jnp.minimum(i, split - 1), 0),
                                pipeline_mode=pl.Buffered(1)),
                   pl.BlockSpec((rows, D_MODEL), lambda i, f: (jnp.clip(i - split, 0, n_tail - 1), 0),
                                pipeline_mode=pl.Buffered(1))]
    else:
        split = None
        xs = (xs,)
        x_specs = [pl.BlockSpec((rows, D_MODEL), lambda i, f: (i, 0), pipeline_mode=pl.Buffered(1))]

    return pl.pallas_call(
        functools.partial(_ffn_kernel, sub=sub, split=split),
        grid=(N_TOK // rows, nf),
        in_specs=x_specs
        + [ada_spec(r) for r in range(FFN_SUBTILES)] + [
            pl.BlockSpec((None, None, 1, D_MODEL), lambda i, f: (l, j, 0, 0)),
            pl.BlockSpec((None, None, D_MODEL, FF_TILE), lambda i, f: (l, j, 0, f)),
            pl.BlockSpec((None, None, D_MODEL, FF_TILE), lambda i, f: (l, j, 0, nf + f)),
            pl.BlockSpec((None, None, FF_TILE, D_MODEL), lambda i, f: (l, j, f, 0)),
        ],
        out_specs=pl.BlockSpec((rows, D_MODEL), lambda i, f: (i, 0)),
        out_shape=jax.ShapeDtypeStruct((N_TOK, D_MODEL), F32),
        scratch_shapes=[pltpu.VMEM((rows, D_MODEL), BF16)],
        compiler_params=_cparams(("parallel", "arbitrary")),
        name="ffn",
    )(*xs, *([ada] * FFN_SUBTILES), norm_ffn.reshape(DEPTH, 2, 1, D_MODEL), w_gu, w_gu, w_down)


IN_TILE = 512


def _inproj_kernel(x_ref, ada_ref, g_ref, w1, w2, w3, w4, o1, o2, o3, o4):
    y = _rms(x_ref[...], g_ref[...])
    h = (y * (1.0 + ada_ref[4:5, :]) + ada_ref[3:4, :]).astype(BF16)
    for w, o in ((w1, o1), (w2, o2), (w3, o3), (w4, o4)):
        o[...] = jnp.dot(h, w[...], preferred_element_type=F32)


def _inproj_call(x, ada, norm_mix, ws, l):
    widths = (HG_COLS, HY_COLS, MLA_PAD, GD_PAD)
    per = ROW_TILE // IN_TILE
    return pl.pallas_call(
        _inproj_kernel,
        grid=(N_TOK // IN_TILE,),
        in_specs=[
            pl.BlockSpec((IN_TILE, D_MODEL), lambda i: (i, 0)),
            pl.BlockSpec((None, None, N_ADA, D_MODEL), lambda i: (l, _cond_of_tile(i // per), 0, 0)),
            pl.BlockSpec((None, 1, D_MODEL), lambda i: (l, 0, 0)),
        ] + [pl.BlockSpec((None, D_MODEL, w), lambda i: (l, 0, 0)) for w in widths],
        out_specs=[pl.BlockSpec((IN_TILE, w), lambda i: (i, 0)) for w in widths],
        out_shape=[jax.ShapeDtypeStruct((N_TOK, w), F32) for w in widths],
        compiler_params=_cparams(("parallel",)),
        name="inproj",
    )(x, ada, norm_mix.reshape(DEPTH, 1, D_MODEL), *ws)


OUT_TILE = 512


def _outproj_kernel(x_ref, ada_ref, w_ref, *refs):
    o_ref = refs[-1]
    i = pl.program_id(0)
    n_p = N_PROMPT // OUT_TILE

    def run(srcs):
        acc = jnp.zeros((OUT_TILE, D_MODEL), F32)
        for g, s in enumerate(srcs):
            acc += jnp.dot(s[...].astype(BF16), w_ref[g * GROUP_W:(g + 1) * GROUP_W, :],
                           preferred_element_type=F32)
        o_ref[...] = x_ref[...] + ada_ref[5:6, :] * acc

    @pl.when(i < n_p)
    def _():
        run(refs[0:4])

    @pl.when(i >= n_p)
    def _():
        run(refs[4:8])


def _outproj_call(x, ada, w_out_bf, o_p, o_s, l):
    per = ROW_TILE // OUT_TILE
    n_p = N_PROMPT // OUT_TILE
    n_s = N_SAMPLE // OUT_TILE
    return pl.pallas_call(
        _outproj_kernel,
        grid=(N_TOK // OUT_TILE,),
        in_specs=[
            pl.BlockSpec((OUT_TILE, D_MODEL), lambda i: (i, 0)),
            pl.BlockSpec((None, None, N_ADA, D_MODEL), lambda i: (l, _cond_of_tile(i // per), 0, 0)),
            pl.BlockSpec((None, D_MODEL, D_MODEL), lambda i: (l, 0, 0)),
        ] + [pl.BlockSpec((OUT_TILE, GROUP_W), lambda i: (jnp.minimum(i, n_p - 1), 0))] * 4
          + [pl.BlockSpec((OUT_TILE, GROUP_W), lambda i: (jnp.clip(i - n_p, 0, n_s - 1), 0))] * 4,
        out_specs=pl.BlockSpec((OUT_TILE, D_MODEL), lambda i: (i, 0)),
        out_shape=jax.ShapeDtypeStruct((N_TOK, D_MODEL), F32),
        compiler_params=_cparams(("parallel",)),
        name="outproj",
    )(x, ada, w_out_bf, *o_p, *o_s)


def _block_diag_ones():
    idx = np.arange(GROUP_W) // HEAD_W
    return (idx[:, None] == idx[None, :]).astype(np.float32)


def _hgrn_consts():
    C = CHUNK
    i = np.arange(C)[:, None]
    j = np.arange(C)[None, :]
    masks = []
    s = C // 2
    while s >= 1:
        up_i = (i // s) % 2 == 1
        up_j = (j // s) % 2 == 1
        masks.append(up_i & (~up_j) & (i // (2 * s) == j // (2 * s)))
        s //= 2
    masks.append(j <= i)
    fwd_m = np.stack([m.astype(np.float32) for m in masks])
    bwd_m = np.stack([m.astype(np.float32)[::-1, ::-1] for m in masks])
    tril = np.stack([(j <= i), (j >= i)]).astype(np.float32)
    return tril, np.tile(np.stack([fwd_m, bwd_m]), (1, 1, 1, 2))


def _gdn_consts():
    C = CHUNK
    i = np.arange(C)[:, None]
    t = np.arange(C)[None, :]
    tril = np.stack([(t <= i), (t >= i)]).astype(np.float32)
    masks = np.stack([np.stack([(t <= i), (t < i)]), np.stack([(t >= i), (t > i)])]).astype(np.float32)
    expand = np.zeros((2, LANE, 2 * GROUP_W), np.float32)
    for d in range(2):
        for h in range(N_HEADS):
            expand[d, d * N_HEADS + h, h * HEAD_W:(h + 1) * HEAD_W] = 1.0
            expand[d, 8 + d * N_HEADS + h, GROUP_W + h * HEAD_W:GROUP_W + (h + 1) * HEAD_W] = 1.0
    return tril, masks, expand


def _dft_consts(T):
    n2 = 4 * T
    k = np.arange(T, dtype=np.int64)[:, None]
    s = np.arange(T, dtype=np.int64)[None, :]
    ang = np.pi * (((2 * k + 1) * s) % n2).astype(np.float64) / (2 * T)
    fwd = np.concatenate([np.cos(ang), -np.sin(ang)], axis=0)
    inv = fwd.T / T
    return fwd.astype(np.float32), inv.astype(np.float32)


def _np_split2(x):
    hi = jnp.asarray(x, F32).astype(BF16)
    lo = (jnp.asarray(x, F32) - hi.astype(F32)).astype(BF16)
    return hi, lo


def _hyena_pos_consts(T):
    pos = np.arange(T, dtype=np.float32)
    t = pos / np.float32(T - 1)
    bands = np.linspace(1e-4, (HY_EMB - 1) // 2 - 1, (HY_EMB - 1) // 2, dtype=np.float32)
    ang = (np.float32(2.0 * math.pi / T) * pos[:, None]) * bands[None, :]
    z = np.concatenate([t[:, None], np.cos(ang), -np.sin(ang)], axis=-1).astype(np.float32)
    zp = np.zeros((T, LANE), np.float32)
    zp[:, :HY_EMB] = z
    max_decay = math.log(HY_TARGET) / HY_FAST
    min_decay = math.log(HY_TARGET) / HY_SLOW
    deltas = np.linspace(min_decay, max_decay, GROUP_W, dtype=np.float32)
    window = np.exp(-t[:, None] * np.abs(deltas)[None, :]).astype(np.float32)
    return zp, window


def _rope_consts(T):
    rows = T // GRID_W
    row = np.repeat(np.arange(rows, dtype=np.float32), GRID_W)
    col = (np.arange(T) % GRID_W).astype(np.float32)
    pairs = MLA_ROPE // 4
    inv = (np.float32(ROPE_BASE) ** (-np.arange(pairs, dtype=np.float32) / np.float32(pairs))).astype(np.float32)
    ang = np.concatenate([row[:, None] * inv, col[:, None] * inv], axis=-1).astype(np.float32)
    cos, sin = np.cos(ang), np.sin(ang)
    cosf = np.ones((T, LANE), np.float32)
    sinf = np.zeros((T, LANE), np.float32)
    half = MLA_ROPE // 2
    cosf[:, MLA_NOPE:MLA_NOPE + half] = cos
    cosf[:, MLA_NOPE + half:MLA_QK] = cos
    sinf[:, MLA_NOPE:MLA_NOPE + half] = -sin
    sinf[:, MLA_NOPE + half:MLA_QK] = sin
    return cosf, sinf


def _head_norm_gate(tot, bd, gn, gate):
    ms = _sel_dot_right(tot * tot, bd) * (1.0 / HEAD_W)
    return tot * lax.rsqrt(ms + RMS_EPS) * gn * _silu(gate)


def _sel_dot_right(x, c):
    h, l = _split2(x)
    return jnp.dot(h, c, preferred_element_type=F32) + jnp.dot(l, c, preferred_element_type=F32)


def _block_ref(b, two_s, r):
    C, W = b.shape
    if two_s % 8 == 0:
        b3 = b.reshape(C // two_s, two_s, W)
        return jnp.broadcast_to(b3[:, r:r + 1, :], b3.shape).reshape(C, W)
    pos = lax.broadcasted_iota(jnp.int32, b.shape, 0) % two_s
    out = b
    for p in range(two_s):
        if p != r:
            out = jnp.where(pos == p, pltpu.roll(b, (p - r) % C, 0), out)
    return out


N_PAIRS = N_HEADS // 2
HG_GROUP = 4
HG_DIRECT_MAX = 80.0


def _pair_blockdiag(x):
    lane = lax.broadcasted_iota(jnp.int32, x.shape, 1)
    zero = jnp.zeros_like(x)
    return jnp.concatenate([jnp.where(lane < HEAD_W, x, zero), jnp.where(lane >= HEAD_W, x, zero)], axis=0)


def _hgrn_kernel(*refs, T, sps, has_s0):
    if has_s0:
        (u_ref, lb_ref, gn_ref, tril_ref, lmask_ref, bd_ref, s0_ref,
         o_ref, sfin_ref, oi_s, qin_s, up_s, dc_s, st_s) = refs
    else:
        (u_ref, lb_ref, gn_ref, tril_ref, lmask_ref, bd_ref, _,
         o_ref, sfin_ref, oi_s, qin_s, up_s, dc_s, st_s) = refs
    n = T // CHUNK
    C = CHUNK
    bd = bd_ref[...]
    n_lv = int(math.log2(C))

    log_lb = [jnp.log(lb_ref[d]) for d in range(2)]
    log_1mlb = [jnp.log(1.0 - lb_ref[d]) for d in range(2)]

    def gates(rows, d):
        z = u_ref[rows, (3 + d) * GROUP_W:(4 + d) * GROUP_W]
        t = jnp.exp(-jnp.abs(z))
        log_sig = jnp.minimum(z, 0.0) - jnp.log(1.0 + t)
        c = log_1mlb[d] + log_sig
        m = jnp.maximum(log_lb[d], c)
        lf = m + jnp.log(1.0 + jnp.exp(jnp.minimum(log_lb[d], c) - m))
        sig_neg = jnp.where(z > 0.0, t, 1.0) / (1.0 + t)
        return lf, (1.0 - lb_ref[d]) * sig_neg

    for s in range(sps):
        for d in range(2):
            if has_s0:
                st_s[s, d] = jnp.concatenate([s0_ref[s, d, h].T for h in range(N_HEADS)], axis=-1)
            else:
                st_s[s, d] = jnp.zeros((HEAD_W, GROUP_W), F32)

    def prepare(it, carry):
        units = [(c, d) for c in range(HG_GROUP) for d in range(2)]
        rows = [pl.ds(pl.multiple_of((it * HG_GROUP + c) * C, C), C) for c in range(HG_GROUP)]
        arow = [pl.ds(pl.multiple_of((it * HG_GROUP + c) * 8, 8), 8) for c in range(HG_GROUP)]
        q = [u_ref[rows[c], 0:GROUP_W] * (HEAD_W ** -0.5) for c in range(HG_GROUP)]
        v = [u_ref[rows[c], GROUP_W:2 * GROUP_W] for c in range(HG_GROUP)]
        vt = [[jnp.concatenate([v[c][:, h * HEAD_W:(h + 1) * HEAD_W].T for h in (2 * p, 2 * p + 1)],
                               axis=-1).astype(BF16) for p in range(N_PAIRS)]
              for c in range(HG_GROUP)]
        v_bd = [[_pair_blockdiag(v[c][:, p * LANE:(p + 1) * LANE].astype(BF16)) for p in range(N_PAIRS)]
                for c in range(HG_GROUP)]
        lf, ks = zip(*[gates(rows[c], d) for c, d in units])
        bs = []
        for i, (c, d) in enumerate(units):
            hi, lo = _split2(lf[i])
            tril = tril_ref[d]
            bs.append(jnp.dot(tril, hi, preferred_element_type=F32) + jnp.dot(tril, lo, preferred_element_type=F32))
        tot = [jnp.sum(x, axis=0, keepdims=True) for x in lf]
        mid_row = [C // 2 - 1 if d == 0 else C // 2 for c, d in units]
        spread = [jnp.maximum(jnp.abs(bs[i][0:1] - bs[i][r:r + 1]), jnp.abs(bs[i][C - 1:C] - bs[i][r:r + 1]))
                  for i, r in enumerate(mid_row)]
        widest = functools.reduce(jnp.maximum, spread)
        ko = [(ks[i] * jnp.exp(tot[i] - bs[i])).astype(BF16) for i in range(len(units))]
        up = [[jnp.dot(vt[c][p], _pair_blockdiag(ko[i][:, p * LANE:(p + 1) * LANE]), preferred_element_type=F32)
               for p in range(N_PAIRS)] for i, (c, d) in enumerate(units)]
        for i, (c, d) in enumerate(units):
            qin_s[d, rows[c], :] = q[c] * jnp.exp(bs[i])
            up_s[d, rows[c], :] = jnp.concatenate(up[i], axis=-1)
            dc_s[d, arow[c], :] = jnp.broadcast_to(jnp.exp(tot[i]), (8, GROUP_W))

        def masked_scores(qe, ke, lv):
            out = []
            for i, (c, d) in enumerate(units):
                per_pair = []
                for p in range(N_PAIRS):
                    sl = slice(p * LANE, (p + 1) * LANE)
                    prod = lax.dot_general(qe[i][:, sl], _pair_blockdiag(ke[i][:, sl]), (((1,), (1,)), ((), ())),
                                           preferred_element_type=F32)
                    per_pair.append(jnp.where(lmask_ref[d, lv] > 0.5, prod, 0.0))
                out.append(per_pair)
            return out

        def finish(sc):
            return [jnp.concatenate([jnp.dot(sc[i][p].astype(BF16), v_bd[c][p], preferred_element_type=F32)
                                     for p in range(N_PAIRS)], axis=-1) for i, (c, d) in enumerate(units)]

        def intra_direct():
            mid = [_block_ref(bs[i], C, C // 2 - 1 if d == 0 else C // 2) for i, (c, d) in enumerate(units)]
            qe = [(q[c] * jnp.exp(bs[i] - mid[i])).astype(BF16) for i, (c, d) in enumerate(units)]
            ke = [(ks[i] * jnp.exp(mid[i] - bs[i])).astype(BF16) for i in range(len(units))]
            return tuple(finish(masked_scores(qe, ke, n_lv)))

        def intra_split():
            sc = [[jnp.zeros((C, LANE), F32) for _ in range(N_PAIRS)] for _ in units]
            s = C // 2
            lv = 0
            while s >= 1:
                e = [jnp.exp(-jnp.abs(bs[i] - _block_ref(bs[i], 2 * s, s - 1 if d == 0 else s)))
                     for i, (c, d) in enumerate(units)]
                part = masked_scores([(q[c] * e[i]).astype(BF16) for i, (c, d) in enumerate(units)],
                                     [(ks[i] * e[i]).astype(BF16) for i in range(len(units))], lv)
                sc = [[sc[i][p] + part[i][p] for p in range(N_PAIRS)] for i in range(len(units))]
                s //= 2
                lv += 1
            fin = finish(sc)
            return tuple(fin[i] + _bdot(q[c] * ks[i], bd) * v[c] for i, (c, d) in enumerate(units))

        oi = lax.cond(jnp.max(widest) < HG_DIRECT_MAX, intra_direct, intra_split)
        for i, (c, d) in enumerate(units):
            oi_s[d, rows[c], :] = oi[i]
        return carry

    lax.fori_loop(0, sps * n // HG_GROUP, prepare, 0)

    def chunk(ci, carry):
        scans = [(s, d) for s in range(sps) for d in range(2)]
        cidx = [s * n + (ci if d == 0 else n - 1 - ci) for s, d in scans]
        rows = [pl.ds(pl.multiple_of(c * C, C), C) for c in cidx]
        decay = [dc_s[d, pl.ds(pl.multiple_of(c * 8, 8), 1), :] for (s, d), c in zip(scans, cidx)]
        st = [st_s[s, d] for s, d in scans]
        o_inter = [[lax.dot_general(qin_s[d, rows[j], p * LANE:(p + 1) * LANE].astype(BF16),
                                    _pair_blockdiag(st[j][:, p * LANE:(p + 1) * LANE].astype(BF16)),
                                    (((1,), (1,)), ((), ())), preferred_element_type=F32)
                    for p in range(N_PAIRS)] for j, (s, d) in enumerate(scans)]
        for j, (s, d) in enumerate(scans):
            st_s[s, d] = st[j] * decay[j] + up_s[d, rows[j], :]
            oi_s[d, rows[j], :] = oi_s[d, rows[j], :] + jnp.concatenate(o_inter[j], axis=-1)
        return carry

    lax.fori_loop(0, n, chunk, 0, unroll=2)
    o_ref[...] = _head_norm_gate(oi_s[0] + oi_s[1], bd, gn_ref[...], u_ref[:, 2 * GROUP_W:3 * GROUP_W])
    for s in range(sps):
        for d in range(2):
            for h in range(N_HEADS):
                sfin_ref[s, d, h] = st_s[s, d][:, h * HEAD_W:(h + 1) * HEAD_W].T


def _state_io(in_specs, args, s0, collect, l, nb, per_step=None):
    state_block = (per_step, None, 2, N_HEADS, HEAD_W, HEAD_W)
    if s0 is not None:
        in_specs.append(pl.BlockSpec(state_block, lambda b: (b, l, 0, 0, 0, 0)))
        args.append(s0)
        return (pl.BlockSpec(state_block[:1] + state_block[2:], lambda b: (b, 0, 0, 0, 0)),
                jax.ShapeDtypeStruct((nb, 2, N_HEADS, HEAD_W, HEAD_W), F32), {})
    in_specs.append(pl.BlockSpec(memory_space=pl.ANY))
    args.append(collect)
    return (pl.BlockSpec(state_block, lambda b: (b, l, 0, 0, 0, 0)),
            jax.ShapeDtypeStruct(collect.shape, F32), {len(args) - 1: 1})


def _hgrn_call(u_hg, lb_l, gn, consts, s0, collect, l, T, nb, row0):
    tril, lmask, bd = consts
    sps = max(1, SCAN_STEP_ROWS // T)
    rows = sps * T
    tb = row0 // rows
    has_s0 = s0 is not None
    in_specs = [
        pl.BlockSpec((rows, HG_COLS), lambda b: (tb + b, 0)),
        pl.BlockSpec((None, 2, 1, GROUP_W), lambda b: (l, 0, 0, 0)),
        pl.BlockSpec((None, 1, GROUP_W), lambda b: (l, 0, 0)),
        pl.BlockSpec((2, CHUNK, CHUNK), lambda b: (0, 0, 0)),
        pl.BlockSpec((2, 7, CHUNK, LANE), lambda b: (0, 0, 0, 0)),
        pl.BlockSpec((GROUP_W, GROUP_W), lambda b: (0, 0)),
    ]
    args = [u_hg, lb_l.reshape(DEPTH, 2, 1, GROUP_W), gn.reshape(DEPTH, 1, GROUP_W), tril, lmask, bd]
    state_spec, state_shape, aliases = _state_io(in_specs, args, s0, collect, l, nb, sps)
    seq = pltpu.VMEM((2, rows, GROUP_W), F32)
    return pl.pallas_call(
        functools.partial(_hgrn_kernel, T=T, sps=sps, has_s0=has_s0),
        grid=(nb // sps,),
        in_specs=in_specs,
        out_specs=[pl.BlockSpec((rows, GROUP_W), lambda b: (b, 0)), state_spec],
        out_shape=[jax.ShapeDtypeStruct((nb * T, GROUP_W), F32), state_shape],
        input_output_aliases=aliases,
        scratch_shapes=[seq, seq, seq, pltpu.VMEM((2, rows // CHUNK * 8, GROUP_W), F32),
                        pltpu.VMEM((sps, 2, HEAD_W, GROUP_W), F32)],
        compiler_params=_cparams(("parallel",)),
        name="hgrn",
    )(*args)


def _shift_rows(x, T):
    pos = lax.broadcasted_iota(jnp.int32, x.shape, 0) % T
    prev = jnp.where(pos == 0, 0.0, pltpu.roll(x, 1, 0))
    nxt = jnp.where(pos == T - 1, 0.0, pltpu.roll(x, x.shape[0] - 1, 0))
    return prev, nxt


def _conv3(x, w_ref, T):
    prev, nxt = _shift_rows(x, T)
    return prev * w_ref[0:1, :] + x * w_ref[1:2, :] + nxt * w_ref[2:3, :]


GDN_UNROLL = 2


def _solve_unit_lower(systems):
    c2 = 2 * CHUNK
    slabs = [jnp.concatenate([nmat, nmat, rhs], axis=-1) for rhs, nmat in systems]
    steps = int(math.log2(CHUNK))
    for step in range(steps):
        last = step == steps - 1
        nxt = []
        for slab in slabs:
            hi = slab.astype(BF16)
            lo = (slab - hi.astype(F32)).astype(BF16)
            lhs = jnp.concatenate([hi[:, :c2], lo[:, :CHUNK]], axis=-1)
            first = c2 if last else 0
            rhs3 = jnp.concatenate([hi[:, first:], lo[:, first:], hi[:, first:]], axis=0)
            prod = jnp.dot(lhs, rhs3, preferred_element_type=F32)
            if last:
                nxt.append(slab[:, c2:] + prod)
            else:
                nxt.append(jnp.concatenate([prod[:, :c2], slab[:, c2:] + prod[:, c2:]], axis=-1))
        slabs = nxt
    return slabs


def _gdn_kernel(*refs, T, sps, has_s0):
    if has_s0:
        (u_ref, cw_ref, alog_ref, dtb_ref, exp_ref, tril_ref, mask_ref, bd_ref, gn_ref, s0_ref,
         o_ref, sfin_ref, q_s, k_s, v_s, la_s, be_s, uw_s, ww_s, at_s, qin_s, kt_s, al_s, of_s, st_s) = refs
    else:
        (u_ref, cw_ref, alog_ref, dtb_ref, exp_ref, tril_ref, mask_ref, bd_ref, gn_ref, _,
         o_ref, sfin_ref, q_s, k_s, v_s, la_s, be_s, uw_s, ww_s, at_s, qin_s, kt_s, al_s, of_s, st_s) = refs
    n = T // CHUNK
    C = CHUNK
    bd = bd_ref[...]

    qkv = _silu(_conv3(u_ref[:, 0:3 * GROUP_W], cw_ref, T))
    q = qkv[:, 0:GROUP_W]
    k = qkv[:, GROUP_W:2 * GROUP_W]
    q_s[...] = q * lax.rsqrt(_sel_dot_right(q * q, bd) + 1e-6) * (HEAD_W ** -0.5)
    k_s[...] = k * lax.rsqrt(_sel_dot_right(k * k, bd) + 1e-6)
    v_s[...] = qkv[:, 2 * GROUP_W:3 * GROUP_W]

    ab = u_ref[:, 4 * GROUP_W:4 * GROUP_W + LANE]
    xa = ab + dtb_ref[...]
    softplus = jnp.maximum(xa, 0.0) + jnp.log(1.0 + jnp.exp(-jnp.abs(xa)))
    log_a = -jnp.exp(alog_ref[...]) * softplus
    lane = lax.broadcasted_iota(jnp.int32, ab.shape, 1)
    narrow = jnp.where(lane < 8, log_a, _sigmoid(ab))
    for d in range(2):
        wide = _dot_sel(narrow, exp_ref[d])
        la_s[d] = wide[:, 0:GROUP_W]
        be_s[d] = wide[:, GROUP_W:2 * GROUP_W]
        for s in range(sps):
            if has_s0:
                st_s[s, d] = jnp.concatenate([s0_ref[s, d, h] for h in range(N_HEADS)], axis=-1)
            else:
                st_s[s, d] = jnp.zeros((HEAD_W, GROUP_W), F32)

    def prepare(cidx):
        r0 = pl.multiple_of(cidx * C, C)
        rows = pl.ds(r0, C)
        arow = pl.ds(pl.multiple_of(cidx * 8, 8), 8)
        q = q_s[rows, :]
        k = k_s[rows, :]
        v = v_s[rows, :]
        systems = []
        attns = []
        kts = []
        for d in range(2):
            incl = mask_ref[d, 0] > 0.5
            strict = mask_ref[d, 1]
            la = la_s[d, rows, :]
            be = be_s[d, rows, :]
            gx = _sel_dot(tril_ref[d], la)
            gtot = jnp.sum(la, axis=0, keepdims=True)
            eg = jnp.exp(gx)
            kout = k * jnp.exp(gtot - gx)
            qin_s[d, rows, :] = q * eg
            al_s[d, arow, :] = jnp.broadcast_to(jnp.exp(gtot), (8, GROUP_W))
            kb = k * be
            vb = v * be
            kbg = kb * eg
            for h in range(N_HEADS):
                sl = slice(h * HEAD_W, (h + 1) * HEAD_W)
                gh = gx[:, sl]
                dmat = gh - gh.T
                dec = jnp.where(incl, jnp.exp(jnp.where(incl, dmat, 0.0)), 0.0)
                qk = _bdot_nt(jnp.concatenate([kb[:, sl], q[:, sl]], axis=0), k[:, sl])
                nmat = -(qk[:C] * dec * strict)
                systems.append((jnp.concatenate([vb[:, sl], kbg[:, sl]], axis=-1), nmat))
                attns.append(qk[C:] * dec)
                kts.append(kout[:, sl].T)
        sols = _solve_unit_lower(systems)
        for d in range(2):
            mine = sols[d * N_HEADS:(d + 1) * N_HEADS]
            uw_s[d, rows, :] = jnp.concatenate([x[:, :HEAD_W] for x in mine], axis=-1)
            ww_s[d, rows, :] = jnp.concatenate([x[:, HEAD_W:] for x in mine], axis=-1)
            at_s[d, rows, :] = jnp.concatenate(attns[d * N_HEADS:(d + 1) * N_HEADS], axis=-1)
            kt_s[d, rows, :] = jnp.concatenate(kts[d * N_HEADS:(d + 1) * N_HEADS], axis=-1)

    def prep_body(i, carry):
        for j in range(GDN_UNROLL):
            prepare(i * GDN_UNROLL + j)
        return carry

    lax.fori_loop(0, sps * n // GDN_UNROLL, prep_body, 0)

    def chunk(ci, carry):
        scans = [(s, d) for s in range(sps) for d in range(2)]
        units = [(j, p) for j in range(len(scans)) for p in range(N_PAIRS)]
        rows, alast = [], []
        for s, d in scans:
            cidx = s * n + (ci if d == 0 else n - 1 - ci)
            rows.append(pl.ds(pl.multiple_of(cidx * C, C), C))
            alast.append(al_s[d, pl.ds(pl.multiple_of(cidx * 8, 8), 1), :])
        st = [st_s[s, d] for s, d in scans]
        dirs = [d for s, d in scans]
        lanes = [slice(p * LANE, (p + 1) * LANE) for p in range(N_PAIRS)]
        both = [jnp.dot(jnp.concatenate([ww_s[dirs[j], rows[j], lanes[p]], qin_s[dirs[j], rows[j], lanes[p]]],
                                        axis=0).astype(BF16),
                        _pair_blockdiag(st[j][:, lanes[p]].astype(BF16)), preferred_element_type=F32)
                for j, p in units]
        vnew = [uw_s[dirs[j], rows[j], lanes[p]] - both[i][:C] for i, (j, p) in enumerate(units)]
        upd = [jnp.dot(jnp.concatenate([at_s[dirs[j], rows[j], lanes[p]], kt_s[dirs[j], rows[j], lanes[p]]],
                                       axis=0).astype(BF16),
                       _pair_blockdiag(vnew[i].astype(BF16)), preferred_element_type=F32)
               for i, (j, p) in enumerate(units)]
        for j, (s, d) in enumerate(scans):
            idx = range(j * N_PAIRS, (j + 1) * N_PAIRS)
            of_s[d, rows[j], :] = jnp.concatenate([both[i][C:] + upd[i][:C] for i in idx], axis=-1)
            st_s[s, d] = st[j] * alast[j] + jnp.concatenate([upd[i][C:] for i in idx], axis=-1)
        return carry

    lax.fori_loop(0, n, chunk, 0)
    o_ref[...] = _head_norm_gate(of_s[0] + of_s[1], bd, gn_ref[...], u_ref[:, 3 * GROUP_W:4 * GROUP_W])
    for s in range(sps):
        for d in range(2):
            for h in range(N_HEADS):
                sfin_ref[s, d, h] = st_s[s, d][:, h * HEAD_W:(h + 1) * HEAD_W]


def _gdn_call(u_gd, cw, alog, dtb, gn, consts, s0, collect, l, T, nb, row0):
    tril, masks, expand, bd = consts
    sps = max(1, SCAN_STEP_ROWS // T)
    rows = sps * T
    tb = row0 // rows
    has_s0 = s0 is not None
    in_specs = [
        pl.BlockSpec((rows, GD_PAD), lambda b: (tb + b, 0)),
        pl.BlockSpec((None, 3, 3 * GROUP_W), lambda b: (l, 0, 0)),
        pl.BlockSpec((None, 1, LANE), lambda b: (l, 0, 0)),
        pl.BlockSpec((None, 1, LANE), lambda b: (l, 0, 0)),
        pl.BlockSpec((2, LANE, 2 * GROUP_W), lambda b: (0, 0, 0)),
        pl.BlockSpec((2, CHUNK, CHUNK), lambda b: (0, 0, 0)),
        pl.BlockSpec((2, 2, CHUNK, CHUNK), lambda b: (0, 0, 0, 0)),
        pl.BlockSpec((GROUP_W, GROUP_W), lambda b: (0, 0)),
        pl.BlockSpec((None, 1, GROUP_W), lambda b: (l, 0, 0)),
    ]
    args = [u_gd, cw, alog, dtb, expand, tril, masks, bd, gn]
    state_spec, state_shape, aliases = _state_io(in_specs, args, s0, collect, l, nb, per_step=sps)
    seq = pltpu.VMEM((2, rows, GROUP_W), F32)
    return pl.pallas_call(
        functools.partial(_gdn_kernel, T=T, sps=sps, has_s0=has_s0),
        grid=(nb // sps,),
        in_specs=in_specs,
        out_specs=[pl.BlockSpec((rows, GROUP_W), lambda b: (b, 0)), state_spec],
        out_shape=[jax.ShapeDtypeStruct((nb * T, GROUP_W), F32), state_shape],
        input_output_aliases=aliases,
        scratch_shapes=[pltpu.VMEM((rows, GROUP_W), F32)] * 3 + [seq, seq, seq, seq, seq, seq, seq,
            pltpu.VMEM((2, rows // CHUNK * 8, GROUP_W), F32), seq, pltpu.VMEM((sps, 2, HEAD_W, GROUP_W), F32)],
        compiler_params=_cparams(("parallel",)),
        name="gdn",
    )(*args)


def _hyfilt_kernel(z_ref, win_ref, fh_ref, fl_ref, w1_ref, b1_ref, fr_ref, w2_ref, b2_ref, w3_ref, o_ref, *, T):
    fr = fr_ref[...]
    h = jnp.sin(fr * (_dot3(z_ref[...], w1_ref[...]) + b1_ref[...]))
    h = jnp.sin(fr * (_dot3(h, w2_ref[...]) + b2_ref[...]))
    h = _dot3(h, w3_ref[...])
    win = win_ref[...]
    hf = h[:, 0:GROUP_W] * win
    hb = h[:, GROUP_W:2 * GROUP_W] * win
    row = lax.broadcasted_iota(jnp.int32, hb.shape, 0)
    hb = jnp.where(row == 0, 0.0, hb)
    for rows, taps in ((slice(0, T), (hf + hb).astype(BF16)), (slice(T, 2 * T), (hf - hb).astype(BF16))):
        o_ref[rows, :] = (jnp.dot(fh_ref[rows, :], taps, preferred_element_type=F32)
                          + jnp.dot(fl_ref[rows, :], taps, preferred_element_type=F32))


def _hyfilt_call(T, zp, win, fh, fl, w1p, b1, freq, w2, b2, w3):
    c2 = lambda l: (0, 0)
    return pl.pallas_call(
        functools.partial(_hyfilt_kernel, T=T),
        grid=(DEPTH,),
        in_specs=[
            pl.BlockSpec((T, LANE), c2),
            pl.BlockSpec((T, GROUP_W), c2),
            pl.BlockSpec((2 * T, T), c2),
            pl.BlockSpec((2 * T, T), c2),
            pl.BlockSpec((None, LANE, HY_FH), lambda l: (l, 0, 0)),
            pl.BlockSpec((None, 1, HY_FH), lambda l: (l, 0, 0)),
            pl.BlockSpec((None, 1, HY_FH), lambda l: (l, 0, 0)),
            pl.BlockSpec((None, HY_FH, HY_FH), lambda l: (l, 0, 0)),
            pl.BlockSpec((None, 1, HY_FH), lambda l: (l, 0, 0)),
            pl.BlockSpec((None, HY_FH, 2 * GROUP_W), lambda l: (l, 0, 0)),
        ],
        out_specs=pl.BlockSpec((None, 2 * T, GROUP_W), lambda l: (l, 0, 0)),
        out_shape=jax.ShapeDtypeStruct((DEPTH, 2 * T, GROUP_W), F32),
        compiler_params=_cparams(("parallel",)),
        name="hyfilt",
    )(zp, win, fh, fl, w1p, b1, freq, w2, b2, w3)


def _hyena_kernel(u_ref, cw_ref, cb_ref, spec_ref, skip_ref, fh_ref, fl_ref, ih_ref, il_ref, o_ref, *, T):
    uc = _conv3(u_ref[...], cw_ref, T) + cb_ref[...]
    x0 = uc[:, 0:GROUP_W]
    z = uc[:, GROUP_W:2 * GROUP_W] * uc[:, 2 * GROUP_W:3 * GROUP_W]
    zb = z.astype(BF16)
    zs = (jnp.dot(fh_ref[...], zb, preferred_element_type=F32)
          + jnp.dot(fl_ref[...], zb, preferred_element_type=F32))
    ar, ai = zs[0:T], zs[T:2 * T]
    br, bi = spec_ref[0:T, :], spec_ref[T:2 * T, :]
    pb = jnp.concatenate([ar * br - ai * bi, ar * bi + ai * br], axis=0).astype(BF16)
    y = (jnp.dot(ih_ref[...], pb, preferred_element_type=F32)
         + jnp.dot(il_ref[...], pb, preferred_element_type=F32))
    o_ref[...] = x0 * (y + z * skip_ref[...])


def _hyena_call(u_hy, cw, cb, spec, skip, dft, l, T, nb, row0):
    fh, fl, ih, il = dft
    tb = row0 // T
    c2 = lambda b: (0, 0)
    return pl.pallas_call(
        functools.partial(_hyena_kernel, T=T),
        grid=(nb,),
        in_specs=[
            pl.BlockSpec((T, HY_COLS), lambda b: (tb + b, 0)),
            pl.BlockSpec((None, 3, HY_COLS), lambda b: (l, 0, 0)),
            pl.BlockSpec((None, 1, HY_COLS), lambda b: (l, 0, 0)),
            pl.BlockSpec((None, 2 * T, GROUP_W), lambda b: (l, 0, 0)),
            pl.BlockSpec((None, 1, GROUP_W), lambda b: (l, 0, 0)),
            pl.BlockSpec((2 * T, T), c2),
            pl.BlockSpec((2 * T, T), c2),
            pl.BlockSpec((T, 2 * T), c2),
            pl.BlockSpec((T, 2 * T), c2),
        ],
        out_specs=pl.BlockSpec((T, GROUP_W), lambda b: (b, 0)),
        out_shape=jax.ShapeDtypeStruct((nb * T, GROUP_W), F32),
        compiler_params=_cparams(("parallel",)),
        name="hyena",
    )(u_hy, cw, cb, spec, skip, fh, fl, ih, il)


def _rope(x, cosf, sinf):
    lane = lax.broadcasted_iota(jnp.int32, x.shape, 1)
    half = MLA_ROPE // 2
    partner = jnp.where(lane < MLA_NOPE + half, pltpu.roll(x, LANE - half, 1), pltpu.roll(x, half, 1))
    return x * cosf + partner * sinf


def _qk_norm(x, g):
    ms = jnp.sum(x * x, axis=-1, keepdims=True) * (1.0 / MLA_QK)
    return x * lax.rsqrt(ms + RMS_EPS) * g


def _mla_kernel(*refs, T, ctx):
    if ctx:
        (u_ref, qn_ref, wq_ref, kvn_ref, wkv_ref, qkn_ref, cos_ref, sin_ref, cckv_ref, ckr_ref, o_ref,
         q_s, k_s, v_s) = refs
    else:
        (u_ref, qn_ref, wq_ref, kvn_ref, wkv_ref, qkn_ref, _, _, o_ref, ckv_ref, kr_ref, q_s, k_s, v_s) = refs
    n_keys = k_s.shape[1]
    u = u_ref[...]
    cq = _rms(u[:, 0:MLA_Q_LORA], qn_ref[...])
    ckv = _rms(u[:, MLA_Q_LORA:MLA_Q_LORA + MLA_KV_LORA], kvn_ref[...])
    kr = u[:, MLA_Q_LORA + MLA_KV_LORA:MLA_Q_LORA + MLA_KV_LORA + MLA_ROPE]
    if not ctx:
        ckv_ref[...] = ckv
        kr_ref[...] = kr
    q_all = _bdot(cq, wq_ref[...])
    kv = _bdot(ckv, wkv_ref[...])
    gq = qkn_ref[0:1, :]
    gk = qkn_ref[1:2, :]
    if ctx:
        kvc = _bdot(cckv_ref[...], wkv_ref[...])
        krc = ckr_ref[...]
        cosf, sinf = cos_ref[...], sin_ref[...]
    q_scale = MLA_QK ** -0.5 * math.log2(math.e)
    kr_tile = jnp.concatenate([jnp.zeros((T, MLA_NOPE), F32), kr, jnp.zeros((T, LANE - MLA_QK), F32)], axis=-1)
    kr_rot = kr_tile * gk
    if ctx:
        kr_rot = _rope(kr_rot, cosf, sinf)
    nope_lane = lax.broadcasted_iota(jnp.int32, (T, LANE), 1) < MLA_NOPE
    for h in range(N_HEADS):
        qh = _qk_norm(q_all[:, h * LANE:(h + 1) * LANE], gq)
        k_nope = jnp.concatenate([kv[:, h * HEAD_W:(h + 1) * HEAD_W], jnp.zeros((T, LANE - MLA_NOPE), F32)], axis=-1)
        ms = jnp.sum(k_nope * k_nope + kr_tile * kr_tile, axis=-1, keepdims=True) * (1.0 / MLA_QK)
        kh = jnp.where(nope_lane, k_nope * gk, kr_rot) * lax.rsqrt(ms + RMS_EPS)
        if ctx:
            qh = _rope(qh, cosf, sinf)
            zc = jnp.zeros((n_keys - T, LANE - MLA_QK), F32)
            kc = _qk_norm(jnp.concatenate([kvc[:, h * HEAD_W:(h + 1) * HEAD_W], krc, zc], axis=-1), gk)
            k_s[h, T:n_keys, :] = kc.astype(BF16)
        q_s[:, h * LANE:(h + 1) * LANE] = (qh * q_scale).astype(BF16)
        k_s[h, 0:T, :] = kh.astype(BF16)
    for p in range(N_PAIRS):
        lanes = slice(GROUP_W + p * LANE, GROUP_W + (p + 1) * LANE)
        vp = kv[:, lanes]
        if ctx:
            vp = jnp.concatenate([vp, kvc[:, lanes]], axis=0)
        v_s[p] = _pair_blockdiag(vp.astype(BF16))

    def q_block(qb, carry):
        rows = pl.ds(pl.multiple_of(qb * ATT_QBLOCK, ATT_QBLOCK), ATT_QBLOCK)
        lane = lax.broadcasted_iota(jnp.int32, (ATT_QBLOCK, LANE), 1)
        for p in range(N_PAIRS):
            es, sums = [], []
            for h in (2 * p, 2 * p + 1):
                s = lax.dot_general(q_s[rows, h * LANE:(h + 1) * LANE], k_s[h], (((1,), (1,)), ((), ())),
                                    preferred_element_type=F32)
                e = jnp.exp2(s - jnp.max(s, axis=-1, keepdims=True))
                sums.append(jnp.sum(e, axis=-1, keepdims=True))
                es.append(e.astype(BF16))
            o = jnp.dot(jnp.concatenate(es, axis=-1), v_s[p], preferred_element_type=F32)
            o_ref[rows, p * LANE:(p + 1) * LANE] = o / jnp.where(lane < HEAD_W, sums[0], sums[1])
        return carry

    lax.fori_loop(0, T // ATT_QBLOCK, q_block, 0)


def _mla_call(u_mla, qn, wq, kvn, wkv, qkn, rope, cache, collect, l, T, nb, row0):
    tb = row0 // T
    ctx = cache is not None
    n_keys = T + (PAST_LEN if ctx else 0)
    c2 = lambda b: (0, 0)
    in_specs = [
        pl.BlockSpec((T, MLA_PAD), lambda b: (tb + b, 0)),
        pl.BlockSpec((None, 1, MLA_Q_LORA), lambda b: (l, 0, 0)),
        pl.BlockSpec((None, MLA_Q_LORA, N_HEADS * LANE), lambda b: (l, 0, 0)),
        pl.BlockSpec((None, 1, MLA_KV_LORA), lambda b: (l, 0, 0)),
        pl.BlockSpec((None, MLA_KV_LORA, 2 * GROUP_W), lambda b: (l, 0, 0)),
        pl.BlockSpec((None, 2, LANE), lambda b: (l, 0, 0)),
    ]
    args = [u_mla, qn, wq, kvn, wkv, qkn]
    out_specs = [pl.BlockSpec((T, GROUP_W), lambda b: (b, 0))]
    out_shape = [jax.ShapeDtypeStruct((nb * T, GROUP_W), F32)]
    if ctx:
        in_specs += [
            pl.BlockSpec((T, LANE), c2),
            pl.BlockSpec((T, LANE), c2),
            pl.BlockSpec((None, None, PAST_LEN, MLA_KV_LORA), lambda b: (b, l, 0, 0)),
            pl.BlockSpec((None, None, PAST_LEN, MLA_ROPE), lambda b: (b, l, 0, 0)),
        ]
        args += [rope[0], rope[1], cache[0], cache[1]]
        aliases = {}
    else:
        in_specs += [pl.BlockSpec(memory_space=pl.ANY)] * 2
        aliases = {len(args): 1, len(args) + 1: 2}
        args += list(collect)
        out_specs += [pl.BlockSpec((None, None, T, MLA_KV_LORA), lambda b: (b, l, 0, 0)),
                      pl.BlockSpec((None, None, T, MLA_ROPE), lambda b: (b, l, 0, 0))]
        out_shape += [jax.ShapeDtypeStruct(a.shape, F32) for a in collect]
    return pl.pallas_call(
        functools.partial(_mla_kernel, T=T, ctx=ctx),
        grid=(nb,),
        in_specs=in_specs,
        out_specs=out_specs,
        out_shape=out_shape,
        input_output_aliases=aliases,
        scratch_shapes=[pltpu.VMEM((T, N_HEADS * LANE), BF16), pltpu.VMEM((N_HEADS, n_keys, LANE), BF16),
                        pltpu.VMEM((N_PAIRS, 2 * n_keys, LANE), BF16)],
        compiler_params=_cparams(("parallel",)),
        name="mla",
    )(*args)


def _pad_cols(w, width):
    return jnp.pad(w, [(0, 0)] * (w.ndim - 1) + [(0, width - w.shape[-1])])


W_IN_PREP_COLS = 256


def _prep_w_in_kernel(wt_ref, o_hg, o_hy, o_mla, o_gd):
    start = 0
    for o_ref, cols in ((o_hg, HG_COLS), (o_hy, HY_COLS), (o_mla, MLA_COLS), (o_gd, GD_COLS)):
        width = o_ref.shape[-1]
        for c0 in range(0, width, W_IN_PREP_COLS):
            n_out = min(W_IN_PREP_COLS, width - c0)
            n_real = max(0, min(n_out, cols - c0))
            piece = wt_ref[start + c0:start + c0 + n_real, :]
            if n_real < n_out:
                piece = jnp.concatenate([piece, jnp.zeros((n_out - n_real, D_MODEL), F32)], axis=0)
            o_ref[:, c0:c0 + n_out] = piece.T.astype(BF16)
        start += cols


def _prep_w_in(w_in):
    widths = (HG_COLS, HY_COLS, MLA_PAD, GD_PAD)
    n_cols = w_in.shape[-1]
    return pl.pallas_call(
        _prep_w_in_kernel,
        grid=(DEPTH,),
        in_specs=[pl.BlockSpec((None, n_cols, D_MODEL), lambda l: (l, 0, 0))],
        out_specs=[pl.BlockSpec((None, D_MODEL, w), lambda l: (l, 0, 0)) for w in widths],
        out_shape=[jax.ShapeDtypeStruct((DEPTH, D_MODEL, w), BF16) for w in widths],
        compiler_params=_cparams(("parallel",)),
        name="w_in_prep",
    )(jnp.swapaxes(w_in, 1, 2))


def _prep_wq(w_q_up):
    w = w_q_up.reshape(DEPTH, MLA_Q_LORA, N_HEADS, MLA_QK)
    return _pad_cols(w, LANE).reshape(DEPTH, MLA_Q_LORA, N_HEADS * LANE).astype(BF16)


def _prep_wkv(w_kv_up):
    w = w_kv_up.reshape(DEPTH, MLA_KV_LORA, N_HEADS, 2, HEAD_W)
    return w.transpose(0, 1, 3, 2, 4).reshape(DEPTH, MLA_KV_LORA, 2 * GROUP_W).astype(BF16)


def _lower_bounds(hgrn_lb):
    lb = jnp.cumsum(jax.nn.softmax(hgrn_lb.astype(F32), axis=0), axis=0)
    return lb - lb[0]


def kernel(x_prompt, x_sample, cache_mla_ckv, cache_mla_krope, state_hgrn, state_gdn, c, c_ctx, w_ada, b_ada, norm_ffn, w_ffn_gu, w_ffn_down, norm_mix, w_in, w_out, hgrn_lb, hgrn_norm, hy_conv_w, hy_conv_b, hy_w1, hy_b1, hy_freq, hy_w2, hy_b2, hy_w3, hy_skip, mla_q_norm_a, mla_w_q_up, mla_kv_norm_a, mla_w_kv_up, mla_qk_norm, gdn_conv_w, gdn_a_log, gdn_dt_bias, gdn_norm):
    x = (x_prompt.reshape(N_PROMPT, D_MODEL), x_sample.reshape(N_SAMPLE, D_MODEL))

    cond8 = jnp.zeros((8, D_MODEL), F32).at[0].set(c_ctx).at[1:1 + DEC_BATCH].set(c)
    ada = _ada_call(cond8, w_ada, b_ada)

    w_in_parts = _prep_w_in(w_in)
    w_out_bf = w_out.astype(BF16)
    wq = _prep_wq(mla_w_q_up)
    wkv = _prep_wkv(mla_w_kv_up)
    qkn = _pad_cols(mla_qk_norm, LANE)
    lb_all = _lower_bounds(hgrn_lb)
    alog = _pad_cols(gdn_a_log.reshape(DEPTH, 1, 8), LANE)
    dtb = _pad_cols(gdn_dt_bias.reshape(DEPTH, 1, 8), LANE)
    gdn_gn = jnp.tile(gdn_norm, (1, N_HEADS)).reshape(DEPTH, 1, GROUP_W)
    w1p = jnp.pad(hy_w1, ((0, 0), (0, LANE - HY_EMB), (0, 0)))

    bd = jnp.asarray(_block_diag_ones(), BF16)
    hg_tril, hg_m = _hgrn_consts()
    hg_consts = (jnp.asarray(hg_tril, BF16), jnp.asarray(hg_m, F32), bd)
    gd_tril, gd_masks, gd_expand = _gdn_consts()
    gd_consts = (jnp.asarray(gd_tril, BF16), jnp.asarray(gd_masks, F32), jnp.asarray(gd_expand, BF16), bd)
    rope = tuple(jnp.asarray(a) for a in _rope_consts(DEC_SEQ))
    groups = ((SEQ, BATCH, 0), (DEC_SEQ, DEC_BATCH, N_PROMPT))
    dft = {}
    spec = {}
    for T, _, _ in groups:
        fwd, inv = _dft_consts(T)
        fh, fl = _np_split2(fwd)
        ih, il = _np_split2(inv)
        dft[T] = (fh, fl, ih, il)
        zp, win = _hyena_pos_consts(T)
        spec[T] = _hyfilt_call(T, jnp.asarray(zp), jnp.asarray(win), fh, fl, w1p,
                               hy_b1.reshape(DEPTH, 1, HY_FH), hy_freq.reshape(DEPTH, 1, HY_FH), hy_w2,
                               hy_b2.reshape(DEPTH, 1, HY_FH), hy_w3)

    new_ckv = jnp.zeros((BATCH, DEPTH, SEQ, MLA_KV_LORA), F32)
    new_kr = jnp.zeros((BATCH, DEPTH, SEQ, MLA_ROPE), F32)
    new_hg = jnp.zeros((BATCH, DEPTH, 2, N_HEADS, HEAD_W, HEAD_W), F32)
    new_gd = jnp.zeros((BATCH, DEPTH, 2, N_HEADS, HEAD_W, HEAD_W), F32)
    for l in range(DEPTH):
        x = _ffn_call(x, ada, norm_ffn, w_ffn_gu, w_ffn_down, l, 0)
        u_hg, u_hy, u_mla, u_gd = _inproj_call(x, ada, norm_mix, w_in_parts, l)
        outs = []
        for gi, (T, nb, row0) in enumerate(groups):
            latent = gi == 1
            o_hg, s_hg = _hgrn_call(u_hg, lb_all, hgrn_norm, hg_consts, state_hgrn if latent else None,
                                    None if latent else new_hg, l, T, nb, row0)
            o_hy = _hyena_call(u_hy, hy_conv_w, hy_conv_b.reshape(DEPTH, 1, HY_COLS), spec[T],
                               hy_skip.reshape(DEPTH, 1, GROUP_W), dft[T], l, T, nb, row0)
            mla = _mla_call(u_mla, mla_q_norm_a.reshape(DEPTH, 1, MLA_Q_LORA), wq,
                            mla_kv_norm_a.reshape(DEPTH, 1, MLA_KV_LORA), wkv, qkn,
                            rope if latent else None,
                            (cache_mla_ckv, cache_mla_krope) if latent else None,
                            None if latent else (new_ckv, new_kr), l, T, nb, row0)
            o_gd, s_gd = _gdn_call(u_gd, gdn_conv_w, alog, dtb, gdn_gn, gd_consts, state_gdn if latent else None,
                                   None if latent else new_gd, l, T, nb, row0)
            outs.append((o_hg, o_hy, mla[0], o_gd))
            if not latent:
                new_ckv, new_kr, new_hg, new_gd = mla[1], mla[2], s_hg, s_gd
        x = _outproj_call(x, ada, w_out_bf, outs[0], outs[1], l)
        x = _ffn_call(x, ada, norm_ffn, w_ffn_gu, w_ffn_down, l, 1)

    y_prompt = x[:N_PROMPT].reshape(BATCH, SEQ, D_MODEL)
    y_sample = x[N_PROMPT:].reshape(DEC_BATCH, DEC_SEQ, D_MODEL)
    return (y_prompt, y_sample, new_ckv, new_kr, new_hg, new_gd)
```

```python
import functools
import math

import numpy as np
import jax
import jax.numpy as jnp
from jax import lax
from jax.experimental import pallas as pl
from jax.experimental.pallas import tpu as pltpu

F32 = jnp.float32
BF16 = jnp.bfloat16

D_MODEL = 1024
BATCH = 16
SEQ = 256
DEPTH = 4
DEC_BATCH = 2
DEC_SEQ = 1024
PAST_LEN = 256
GRID_W = 64
N_ADA = 9
D_FF = 2816
GROUP_W = 256
CHUNK = 64
RMS_EPS = 1e-6
N_HEADS = 4
HEAD_W = 64
HY_EMB = 33
HY_FH = 64
HY_TARGET = 1e-2
HY_FAST = 0.3
HY_SLOW = 1.5
MLA_NOPE = 64
MLA_ROPE = 32
MLA_QK = MLA_NOPE + MLA_ROPE
MLA_Q_LORA = 256
MLA_KV_LORA = 128
ROPE_BASE = 10000.0

HG_COLS = 5 * GROUP_W
HY_COLS = 3 * GROUP_W
MLA_COLS = MLA_Q_LORA + MLA_KV_LORA + MLA_ROPE
GD_COLS = 4 * GROUP_W + 16
MLA_PAD = 512
GD_PAD = 1152

N_PROMPT = BATCH * SEQ
N_SAMPLE = DEC_BATCH * DEC_SEQ
N_TOK = N_PROMPT + N_SAMPLE
LANE = 128
VMEM_LIMIT = 56 * 1024 * 1024
ROW_TILE = 1024
FF_TILE = 256
ADA_TILE = 3072
ATT_QBLOCK = 256
SCAN_STEP_ROWS = 1024


def _bdot(a, b):
    return jnp.dot(a.astype(BF16), b.astype(BF16), preferred_element_type=F32)


def _bdot_nt(a, b):
    return lax.dot_general(a.astype(BF16), b.astype(BF16), (((1,), (1,)), ((), ())),
                           preferred_element_type=F32)


def _split2(x):
    hi = x.astype(BF16)
    lo = (x - hi.astype(F32)).astype(BF16)
    return hi, lo


def _split3(x):
    hi = x.astype(BF16)
    r = x - hi.astype(F32)
    mid = r.astype(BF16)
    lo = (r - mid.astype(F32)).astype(BF16)
    return hi, mid, lo


def _dot3(a, b):
    ah, al = _split2(a)
    bh, bl = _split2(b)
    return (jnp.dot(ah, bh, preferred_element_type=F32) + jnp.dot(ah, bl, preferred_element_type=F32)
            + jnp.dot(al, bh, preferred_element_type=F32))


def _sel_dot(c, x):
    h, m, l = _split3(x)
    return (jnp.dot(c, h, preferred_element_type=F32) + jnp.dot(c, m, preferred_element_type=F32)
            + jnp.dot(c, l, preferred_element_type=F32))


def _dot_sel(x, c):
    h, m, l = _split3(x)
    return (jnp.dot(h, c, preferred_element_type=F32) + jnp.dot(m, c, preferred_element_type=F32)
            + jnp.dot(l, c, preferred_element_type=F32))


def _sigmoid(x):
    return 1.0 / (1.0 + jnp.exp(-x))


def _silu(x):
    return x * _sigmoid(x)


def _rms(x, g):
    return x * lax.rsqrt(jnp.mean(x * x, axis=-1, keepdims=True) + RMS_EPS) * g


def _cparams(sem):
    return pltpu.CompilerParams(dimension_semantics=sem, vmem_limit_bytes=VMEM_LIMIT)


def _cond_of_tile(i):
    return jnp.maximum(i - (N_PROMPT // ROW_TILE - 1), 0)


def _ada_kernel(c_ref, w_ref, b_ref, o_ref):
    ch, cl = _split2(_silu(c_ref[...]))
    w = w_ref[...].astype(BF16)
    o_ref[...] = (jnp.dot(ch, w, preferred_element_type=F32) + jnp.dot(cl, w, preferred_element_type=F32)
                  + b_ref[...])


def _ada_call(cond8, w_ada, b_ada):
    n = N_ADA * D_MODEL
    out = pl.pallas_call(
        _ada_kernel,
        grid=(DEPTH, n // ADA_TILE),
        in_specs=[
            pl.BlockSpec((8, D_MODEL), lambda l, j: (0, 0)),
            pl.BlockSpec((None, D_MODEL, ADA_TILE), lambda l, j: (l, 0, j)),
            pl.BlockSpec((None, 1, ADA_TILE), lambda l, j: (l, 0, j)),
        ],
        out_specs=pl.BlockSpec((None, 8, ADA_TILE), lambda l, j: (l, 0, j)),
        out_shape=jax.ShapeDtypeStruct((DEPTH, 8, n), F32),
        compiler_params=_cparams(("parallel", "parallel")),
        name="ada",
    )(cond8, w_ada, b_ada.reshape(DEPTH, 1, n))
    return out.reshape(DEPTH, 8, N_ADA, D_MODEL)


FFN_SUBTILES = 2


def _ffn_kernel(*refs, sub, split):
    nx = 1 if split is None else 2
    x_refs = refs[:nx]
    ada_refs = refs[nx:nx + FFN_SUBTILES]
    g_ref, wg_ref, wu_ref, wd_ref, o_ref, h_scr = refs[nx + FFN_SUBTILES:]
    i = pl.program_id(0)
    f = pl.program_id(1)

    def prologue(x_ref):
        for r, ada_ref in enumerate(ada_refs):
            rows = slice(r * ROW_TILE, (r + 1) * ROW_TILE)
            y = _rms(x_ref[rows, :], g_ref[...])
            h = y * (1.0 + ada_ref[3 * sub + 1:3 * sub + 2, :]) + ada_ref[3 * sub:3 * sub + 1, :]
            h_scr[rows, :] = h.astype(BF16)
        o_ref[...] = jnp.zeros_like(o_ref)

    def epilogue(x_ref):
        for r, ada_ref in enumerate(ada_refs):
            rows = slice(r * ROW_TILE, (r + 1) * ROW_TILE)
            o_ref[rows, :] = x_ref[rows, :] + 0.5 * ada_ref[3 * sub + 2:3 * sub + 3, :] * o_ref[rows, :]

    def on(step, fn):
        if split is None:
            pl.when(f == step)(functools.partial(fn, x_refs[0]))
        else:
            pl.when((f == step) & (i < split))(functools.partial(fn, x_refs[0]))
            pl.when((f == step) & (i >= split))(functools.partial(fn, x_refs[1]))

    on(0, prologue)
    wg = wg_ref[...].astype(BF16)
    wu = wu_ref[...].astype(BF16)
    wd = wd_ref[...].astype(BF16)
    for r in range(FFN_SUBTILES):
        rows = slice(r * ROW_TILE, (r + 1) * ROW_TILE)
        h = h_scr[rows, :]
        gate = jnp.dot(h, wg, preferred_element_type=F32)
        up = jnp.dot(h, wu, preferred_element_type=F32)
        a = (_silu(gate) * up).astype(BF16)
        o_ref[rows, :] += jnp.dot(a, wd, preferred_element_type=F32)
    on(pl.num_programs(1) - 1, epilogue)


def _ffn_call(xs, ada, norm_ffn, w_gu, w_down, l, j):
    sub = 2 * j
    nf = D_FF // FF_TILE
    rows = FFN_SUBTILES * ROW_TILE

    def ada_spec(r):
        return pl.BlockSpec((None, None, N_ADA, D_MODEL),
                            lambda i, f: (l, _cond_of_tile(i * FFN_SUBTILES + r), 0, 0))

    if isinstance(xs, tuple):
        split = xs[0].shape[0] // rows
        n_tail = xs[1].shape[0] // rows
        x_specs = [pl.BlockSpec((rows, D_MODEL), lambda i, f: (jnp.minimum(i, split - 1), 0),
                                pipeline_mode=pl.Buffered(1)),
                   pl.BlockSpec((rows, D_MODEL), lambda i, f: (jnp.clip(i - split, 0, n_tail - 1), 0),
                                pipeline_mode=pl.Buffered(1))]
    else:
        split = None
        xs = (xs,)
        x_specs = [pl.BlockSpec((rows, D_MODEL), lambda i, f: (i, 0), pipeline_mode=pl.Buffered(1))]

    return pl.pallas_call(
        functools.partial(_ffn_kernel, sub=sub, split=split),
        grid=(N_TOK // rows, nf),
        in_specs=x_specs
        + [ada_spec(r) for r in range(FFN_SUBTILES)] + [
            pl.BlockSpec((None, None, 1, D_MODEL), lambda i, f: (l, j, 0, 0)),
            pl.BlockSpec((None, None, D_MODEL, FF_TILE), lambda i, f: (l, j, 0, f)),
            pl.BlockSpec((None, None, D_MODEL, FF_TILE), lambda i, f: (l, j, 0, nf + f)),
            pl.BlockSpec((None, None, FF_TILE, D_MODEL), lambda i, f: (l, j, f, 0)),
        ],
        out_specs=pl.BlockSpec((rows, D_MODEL), lambda i, f: (i, 0)),
        out_shape=jax.ShapeDtypeStruct((N_TOK, D_MODEL), F32),
        scratch_shapes=[pltpu.VMEM((rows, D_MODEL), BF16)],
        compiler_params=_cparams(("parallel", "arbitrary")),
        name="ffn",
    )(*xs, *([ada] * FFN_SUBTILES), norm_ffn.reshape(DEPTH, 2, 1, D_MODEL), w_gu, w_gu, w_down)


IN_TILE = 512


def _inproj_kernel(x_ref, ada_ref, g_ref, w1, w2, w3, w4, o1, o2, o3, o4):
    y = _rms(x_ref[...], g_ref[...])
    h = (y * (1.0 + ada_ref[4:5, :]) + ada_ref[3:4, :]).astype(BF16)
    for w, o in ((w1, o1), (w2, o2), (w3, o3), (w4, o4)):
        o[...] = jnp.dot(h, w[...], preferred_element_type=F32)


def _inproj_call(x, ada, norm_mix, ws, l):
    widths = (HG_COLS, HY_COLS, MLA_PAD, GD_PAD)
    per = ROW_TILE // IN_TILE
    return pl.pallas_call(
        _inproj_kernel,
        grid=(N_TOK // IN_TILE,),
        in_specs=[
            pl.BlockSpec((IN_TILE, D_MODEL), lambda i: (i, 0)),
            pl.BlockSpec((None, None, N_ADA, D_MODEL), lambda i: (l, _cond_of_tile(i // per), 0, 0)),
            pl.BlockSpec((None, 1, D_MODEL), lambda i: (l, 0, 0)),
        ] + [pl.BlockSpec((None, D_MODEL, w), lambda i: (l, 0, 0)) for w in widths],
        out_specs=[pl.BlockSpec((IN_TILE, w), lambda i: (i, 0)) for w in widths],
        out_shape=[jax.ShapeDtypeStruct((N_TOK, w), F32) for w in widths],
        compiler_params=_cparams(("parallel",)),
        name="inproj",
    )(x, ada, norm_mix.reshape(DEPTH, 1, D_MODEL), *ws)


OUT_TILE = 1024


def _outproj_kernel(x_ref, ada_ref, w_ref, *refs):
    o_ref = refs[-1]
    i = pl.program_id(0)
    n_p = N_PROMPT // OUT_TILE

    def run(srcs):
        acc = jnp.zeros((OUT_TILE, D_MODEL), F32)
        for g, s in enumerate(srcs):
            acc += jnp.dot(s[...].astype(BF16), w_ref[g * GROUP_W:(g + 1) * GROUP_W, :],
                           preferred_element_type=F32)
        o_ref[...] = x_ref[...] + ada_ref[5:6, :] * acc

    @pl.when(i < n_p)
    def _():
        run(refs[0:4])

    @pl.when(i >= n_p)
    def _():
        run(refs[4:8])


def _outproj_call(x, ada, w_out_bf, o_p, o_s, l):
    per = ROW_TILE // OUT_TILE
    n_p = N_PROMPT // OUT_TILE
    n_s = N_SAMPLE // OUT_TILE
    return pl.pallas_call(
        _outproj_kernel,
        grid=(N_TOK // OUT_TILE,),
        in_specs=[
            pl.BlockSpec((OUT_TILE, D_MODEL), lambda i: (i, 0)),
            pl.BlockSpec((None, None, N_ADA, D_MODEL), lambda i: (l, _cond_of_tile(i // per), 0, 0)),
            pl.BlockSpec((None, D_MODEL, D_MODEL), lambda i: (l, 0, 0)),
        ] + [pl.BlockSpec((OUT_TILE, GROUP_W), lambda i: (jnp.minimum(i, n_p - 1), 0))] * 4
          + [pl.BlockSpec((OUT_TILE, GROUP_W), lambda i: (jnp.clip(i - n_p, 0, n_s - 1), 0))] * 4,
        out_specs=pl.BlockSpec((OUT_TILE, D_MODEL), lambda i: (i, 0)),
        out_shape=jax.ShapeDtypeStruct((N_TOK, D_MODEL), F32),
        compiler_params=_cparams(("parallel",)),
        name="outproj",
    )(x, ada, w_out_bf, *o_p, *o_s)


def _block_diag_ones():
    idx = np.arange(GROUP_W) // HEAD_W
    return (idx[:, None] == idx[None, :]).astype(np.float32)


def _hgrn_consts():
    C = CHUNK
    i = np.arange(C)[:, None]
    j = np.arange(C)[None, :]
    masks = []
    s = C // 2
    while s >= 1:
        up_i = (i // s) % 2 == 1
        up_j = (j // s) % 2 == 1
        masks.append(up_i & (~up_j) & (i // (2 * s) == j // (2 * s)))
        s //= 2
    masks.append(j <= i)
    fwd_m = np.stack([m.astype(np.float32) for m in masks])
    bwd_m = np.stack([m.astype(np.float32)[::-1, ::-1] for m in masks])
    tril = np.stack([(j <= i), (j >= i)]).astype(np.float32)
    return tril, np.tile(np.stack([fwd_m, bwd_m]), (1, 1, 1, 2))


def _gdn_consts():
    C = CHUNK
    i = np.arange(C)[:, None]
    t = np.arange(C)[None, :]
    tril = np.stack([(t <= i), (t >= i)]).astype(np.float32)
    masks = np.stack([np.stack([(t <= i), (t < i)]), np.stack([(t >= i), (t > i)])]).astype(np.float32)
    expand = np.zeros((2, LANE, 2 * GROUP_W), np.float32)
    for d in range(2):
        for h in range(N_HEADS):
            expand[d, d * N_HEADS + h, h * HEAD_W:(h + 1) * HEAD_W] = 1.0
            expand[d, 8 + d * N_HEADS + h, GROUP_W + h * HEAD_W:GROUP_W + (h + 1) * HEAD_W] = 1.0
    return tril, masks, expand


def _dft_consts(T):
    n2 = 4 * T
    k = np.arange(T, dtype=np.int64)[:, None]
    s = np.arange(T, dtype=np.int64)[None, :]
    ang = np.pi * (((2 * k + 1) * s) % n2).astype(np.float64) / (2 * T)
    fwd = np.concatenate([np.cos(ang), -np.sin(ang)], axis=0)
    inv = fwd.T / T
    return fwd.astype(np.float32), inv.astype(np.float32)


def _np_split2(x):
    hi = jnp.asarray(x, F32).astype(BF16)
    lo = (jnp.asarray(x, F32) - hi.astype(F32)).astype(BF16)
    return hi, lo


def _hyena_pos_consts(T):
    pos = np.arange(T, dtype=np.float32)
    t = pos / np.float32(T - 1)
    bands = np.linspace(1e-4, (HY_EMB - 1) // 2 - 1, (HY_EMB - 1) // 2, dtype=np.float32)
    ang = (np.float32(2.0 * math.pi / T) * pos[:, None]) * bands[None, :]
    z = np.concatenate([t[:, None], np.cos(ang), -np.sin(ang)], axis=-1).astype(np.float32)
    zp = np.zeros((T, LANE), np.float32)
    zp[:, :HY_EMB] = z
    max_decay = math.log(HY_TARGET) / HY_FAST
    min_decay = math.log(HY_TARGET) / HY_SLOW
    deltas = np.linspace(min_decay, max_decay, GROUP_W, dtype=np.float32)
    window = np.exp(-t[:, None] * np.abs(deltas)[None, :]).astype(np.float32)
    return zp, window


def _rope_consts(T):
    rows = T // GRID_W
    row = np.repeat(np.arange(rows, dtype=np.float32), GRID_W)
    col = (np.arange(T) % GRID_W).astype(np.float32)
    pairs = MLA_ROPE // 4
    inv = (np.float32(ROPE_BASE) ** (-np.arange(pairs, dtype=np.float32) / np.float32(pairs))).astype(np.float32)
    ang = np.concatenate([row[:, None] * inv, col[:, None] * inv], axis=-1).astype(np.float32)
    cos, sin = np.cos(ang), np.sin(ang)
    cosf = np.ones((T, LANE), np.float32)
    sinf = np.zeros((T, LANE), np.float32)
    half = MLA_ROPE // 2
    cosf[:, MLA_NOPE:MLA_NOPE + half] = cos
    cosf[:, MLA_NOPE + half:MLA_QK] = cos
    sinf[:, MLA_NOPE:MLA_NOPE + half] = -sin
    sinf[:, MLA_NOPE + half:MLA_QK] = sin
    return cosf, sinf


def _head_norm_gate(tot, bd, gn, gate):
    ms = _sel_dot_right(tot * tot, bd) * (1.0 / HEAD_W)
    return tot * lax.rsqrt(ms + RMS_EPS) * gn * _silu(gate)


def _sel_dot_right(x, c):
    h, l = _split2(x)
    return jnp.dot(h, c, preferred_element_type=F32) + jnp.dot(l, c, preferred_element_type=F32)


def _block_ref(b, two_s, r):
    C, W = b.shape
    if two_s % 8 == 0:
        b3 = b.reshape(C // two_s, two_s, W)
        return jnp.broadcast_to(b3[:, r:r + 1, :], b3.shape).reshape(C, W)
    pos = lax.broadcasted_iota(jnp.int32, b.shape, 0) % two_s
    out = b
    for p in range(two_s):
        if p != r:
            out = jnp.where(pos == p, pltpu.roll(b, (p - r) % C, 0), out)
    return out


N_PAIRS = N_HEADS // 2
HG_GROUP = 4
HG_DIRECT_MAX = 80.0


def _pair_blockdiag(x):
    lane = lax.broadcasted_iota(jnp.int32, x.shape, 1)
    zero = jnp.zeros_like(x)
    return jnp.concatenate([jnp.where(lane < HEAD_W, x, zero), jnp.where(lane >= HEAD_W, x, zero)], axis=0)


def _hgrn_kernel(*refs, T, sps, has_s0):
    if has_s0:
        (u_ref, lb_ref, gn_ref, tril_ref, lmask_ref, bd_ref, s0_ref,
         o_ref, sfin_ref, oi_s, qin_s, up_s, dc_s, st_s) = refs
    else:
        (u_ref, lb_ref, gn_ref, tril_ref, lmask_ref, bd_ref, _,
         o_ref, sfin_ref, oi_s, qin_s, up_s, dc_s, st_s) = refs
    n = T // CHUNK
    C = CHUNK
    bd = bd_ref[...]
    n_lv = int(math.log2(C))

    log_lb = [jnp.log(lb_ref[d]) for d in range(2)]
    log_1mlb = [jnp.log(1.0 - lb_ref[d]) for d in range(2)]

    def gates(rows, d):
        z = u_ref[rows, (3 + d) * GROUP_W:(4 + d) * GROUP_W]
        t = jnp.exp(-jnp.abs(z))
        log_sig = jnp.minimum(z, 0.0) - jnp.log(1.0 + t)
        c = log_1mlb[d] + log_sig
        m = jnp.maximum(log_lb[d], c)
        lf = m + jnp.log(1.0 + jnp.exp(jnp.minimum(log_lb[d], c) - m))
        sig_neg = jnp.where(z > 0.0, t, 1.0) / (1.0 + t)
        return lf, (1.0 - lb_ref[d]) * sig_neg

    for s in range(sps):
        for d in range(2):
            if has_s0:
                st_s[s, d] = jnp.concatenate([s0_ref[s, d, h].T for h in range(N_HEADS)], axis=-1)
            else:
                st_s[s, d] = jnp.zeros((HEAD_W, GROUP_W), F32)

    def prepare(it, carry):
        units = [(c, d) for c in range(HG_GROUP) for d in range(2)]
        rows = [pl.ds(pl.multiple_of((it * HG_GROUP + c) * C, C), C) for c in range(HG_GROUP)]
        arow = [pl.ds(pl.multiple_of((it * HG_GROUP + c) * 8, 8), 8) for c in range(HG_GROUP)]
        q = [u_ref[rows[c], 0:GROUP_W] * (HEAD_W ** -0.5) for c in range(HG_GROUP)]
        v = [u_ref[rows[c], GROUP_W:2 * GROUP_W] for c in range(HG_GROUP)]
        vt = [[jnp.concatenate([v[c][:, h * HEAD_W:(h + 1) * HEAD_W].T for h in (2 * p, 2 * p + 1)],
                               axis=-1).astype(BF16) for p in range(N_PAIRS)]
              for c in range(HG_GROUP)]
        v_bd = [[_pair_blockdiag(v[c][:, p * LANE:(p + 1) * LANE].astype(BF16)) for p in range(N_PAIRS)]
                for c in range(HG_GROUP)]
        lf, ks = zip(*[gates(rows[c], d) for c, d in units])
        bs = []
        for i, (c, d) in enumerate(units):
            hi, lo = _split2(lf[i])
            tril = tril_ref[d]
            bs.append(jnp.dot(tril, hi, preferred_element_type=F32) + jnp.dot(tril, lo, preferred_element_type=F32))
        tot = [jnp.sum(x, axis=0, keepdims=True) for x in lf]
        mid_row = [C // 2 - 1 if d == 0 else C // 2 for c, d in units]
        spread = [jnp.maximum(jnp.abs(bs[i][0:1] - bs[i][r:r + 1]), jnp.abs(bs[i][C - 1:C] - bs[i][r:r + 1]))
                  for i, r in enumerate(mid_row)]
        widest = functools.reduce(jnp.maximum, spread)
        ko = [(ks[i] * jnp.exp(tot[i] - bs[i])).astype(BF16) for i in range(len(units))]
        up = [[jnp.dot(vt[c][p], _pair_blockdiag(ko[i][:, p * LANE:(p + 1) * LANE]), preferred_element_type=F32)
               for p in range(N_PAIRS)] for i, (c, d) in enumerate(units)]
        for i, (c, d) in enumerate(units):
            qin_s[d, rows[c], :] = q[c] * jnp.exp(bs[i])
            up_s[d, rows[c], :] = jnp.concatenate(up[i], axis=-1)
            dc_s[d, arow[c], :] = jnp.broadcast_to(jnp.exp(tot[i]), (8, GROUP_W))

        def masked_scores(qe, ke, lv):
            out = []
            for i, (c, d) in enumerate(units):
                per_pair = []
                for p in range(N_PAIRS):
                    sl = slice(p * LANE, (p + 1) * LANE)
                    prod = lax.dot_general(qe[i][:, sl], _pair_blockdiag(ke[i][:, sl]), (((1,), (1,)), ((), ())),
                                           preferred_element_type=F32)
                    per_pair.append(jnp.where(lmask_ref[d, lv] > 0.5, prod, 0.0))
                out.append(per_pair)
            return out

        def finish(sc):
            return [jnp.concatenate([jnp.dot(sc[i][p].astype(BF16), v_bd[c][p], preferred_element_type=F32)
                                     for p in range(N_PAIRS)], axis=-1) for i, (c, d) in enumerate(units)]

        def intra_direct():
            mid = [_block_ref(bs[i], C, C // 2 - 1 if d == 0 else C // 2) for i, (c, d) in enumerate(units)]
            qe = [(q[c] * jnp.exp(bs[i] - mid[i])).astype(BF16) for i, (c, d) in enumerate(units)]
            ke = [(ks[i] * jnp.exp(mid[i] - bs[i])).astype(BF16) for i in range(len(units))]
            return tuple(finish(masked_scores(qe, ke, n_lv)))

        def intra_split():
            sc = [[jnp.zeros((C, LANE), F32) for _ in range(N_PAIRS)] for _ in units]
            s = C // 2
            lv = 0
            while s >= 1:
                e = [jnp.exp(-jnp.abs(bs[i] - _block_ref(bs[i], 2 * s, s - 1 if d == 0 else s)))
                     for i, (c, d) in enumerate(units)]
                part = masked_scores([(q[c] * e[i]).astype(BF16) for i, (c, d) in enumerate(units)],
                                     [(ks[i] * e[i]).astype(BF16) for i in range(len(units))], lv)
                sc = [[sc[i][p] + part[i][p] for p in range(N_PAIRS)] for i in range(len(units))]
                s //= 2
                lv += 1
            fin = finish(sc)
            return tuple(fin[i] + _bdot(q[c] * ks[i], bd) * v[c] for i, (c, d) in enumerate(units))

        oi = lax.cond(jnp.max(widest) < HG_DIRECT_MAX, intra_direct, intra_split)
        for i, (c, d) in enumerate(units):
            oi_s[d, rows[c], :] = oi[i]
        return carry

    lax.fori_loop(0, sps * n // HG_GROUP, prepare, 0)

    def chunk(ci, carry):
        scans = [(s, d) for s in range(sps) for d in range(2)]
        cidx = [s * n + (ci if d == 0 else n - 1 - ci) for s, d in scans]
        rows = [pl.ds(pl.multiple_of(c * C, C), C) for c in cidx]
        decay = [dc_s[d, pl.ds(pl.multiple_of(c * 8, 8), 1), :] for (s, d), c in zip(scans, cidx)]
        st = [st_s[s, d] for s, d in scans]
        o_inter = [[lax.dot_general(qin_s[d, rows[j], p * LANE:(p + 1) * LANE].astype(BF16),
                                    _pair_blockdiag(st[j][:, p * LANE:(p + 1) * LANE].astype(BF16)),
                                    (((1,), (1,)), ((), ())), preferred_element_type=F32)
                    for p in range(N_PAIRS)] for j, (s, d) in enumerate(scans)]
        for j, (s, d) in enumerate(scans):
            st_s[s, d] = st[j] * decay[j] + up_s[d, rows[j], :]
            oi_s[d, rows[j], :] = oi_s[d, rows[j], :] + jnp.concatenate(o_inter[j], axis=-1)
        return carry

    lax.fori_loop(0, n, chunk, 0, unroll=2)
    o_ref[...] = _head_norm_gate(oi_s[0] + oi_s[1], bd, gn_ref[...], u_ref[:, 2 * GROUP_W:3 * GROUP_W])
    for s in range(sps):
        for d in range(2):
            for h in range(N_HEADS):
                sfin_ref[s, d, h] = st_s[s, d][:, h * HEAD_W:(h + 1) * HEAD_W].T


def _state_io(in_specs, args, s0, collect, l, nb, per_step=None):
    state_block = (per_step, None, 2, N_HEADS, HEAD_W, HEAD_W)
    if s0 is not None:
        in_specs.append(pl.BlockSpec(state_block, lambda b: (b, l, 0, 0, 0, 0)))
        args.append(s0)
        return (pl.BlockSpec(state_block[:1] + state_block[2:], lambda b: (b, 0, 0, 0, 0)),
                jax.ShapeDtypeStruct((nb, 2, N_HEADS, HEAD_W, HEAD_W), F32), {})
    in_specs.append(pl.BlockSpec(memory_space=pl.ANY))
    args.append(collect)
    return (pl.BlockSpec(state_block, lambda b: (b, l, 0, 0, 0, 0)),
            jax.ShapeDtypeStruct(collect.shape, F32), {len(args) - 1: 1})


def _hgrn_call(u_hg, lb_l, gn, consts, s0, collect, l, T, nb, row0):
    tril, lmask, bd = consts
    sps = max(1, SCAN_STEP_ROWS // T)
    rows = sps * T
    tb = row0 // rows
    has_s0 = s0 is not None
    in_specs = [
        pl.BlockSpec((rows, HG_COLS), lambda b: (tb + b, 0)),
        pl.BlockSpec((None, 2, 1, GROUP_W), lambda b: (l, 0, 0, 0)),
        pl.BlockSpec((None, 1, GROUP_W), lambda b: (l, 0, 0)),
        pl.BlockSpec((2, CHUNK, CHUNK), lambda b: (0, 0, 0)),
        pl.BlockSpec((2, 7, CHUNK, LANE), lambda b: (0, 0, 0, 0)),
        pl.BlockSpec((GROUP_W, GROUP_W), lambda b: (0, 0)),
    ]
    args = [u_hg, lb_l.reshape(DEPTH, 2, 1, GROUP_W), gn.reshape(DEPTH, 1, GROUP_W), tril, lmask, bd]
    state_spec, state_shape, aliases = _state_io(in_specs, args, s0, collect, l, nb, sps)
    seq = pltpu.VMEM((2, rows, GROUP_W), F32)
    return pl.pallas_call(
        functools.partial(_hgrn_kernel, T=T, sps=sps, has_s0=has_s0),
        grid=(nb // sps,),
        in_specs=in_specs,
        out_specs=[pl.BlockSpec((rows, GROUP_W), lambda b: (b, 0)), state_spec],
        out_shape=[jax.ShapeDtypeStruct((nb * T, GROUP_W), F32), state_shape],
        input_output_aliases=aliases,
        scratch_shapes=[seq, seq, seq, pltpu.VMEM((2, rows // CHUNK * 8, GROUP_W), F32),
                        pltpu.VMEM((sps, 2, HEAD_W, GROUP_W), F32)],
        compiler_params=_cparams(("parallel",)),
        name="hgrn",
    )(*args)


def _shift_rows(x, T):
    pos = lax.broadcasted_iota(jnp.int32, x.shape, 0) % T
    prev = jnp.where(pos == 0, 0.0, pltpu.roll(x, 1, 0))
    nxt = jnp.where(pos == T - 1, 0.0, pltpu.roll(x, x.shape[0] - 1, 0))
    return prev, nxt


def _conv3(x, w_ref, T):
    prev, nxt = _shift_rows(x, T)
    return prev * w_ref[0:1, :] + x * w_ref[1:2, :] + nxt * w_ref[2:3, :]


GDN_UNROLL = 2


def _solve_unit_lower(systems):
    c2 = 2 * CHUNK
    slabs = [jnp.concatenate([nmat, nmat, rhs], axis=-1) for rhs, nmat in systems]
    steps = int(math.log2(CHUNK))
    for step in range(steps):
        last = step == steps - 1
        nxt = []
        for slab in slabs:
            hi = slab.astype(BF16)
            lo = (slab - hi.astype(F32)).astype(BF16)
            lhs = jnp.concatenate([hi[:, :c2], lo[:, :CHUNK]], axis=-1)
            first = c2 if last else 0
            rhs3 = jnp.concatenate([hi[:, first:], lo[:, first:], hi[:, first:]], axis=0)
            prod = jnp.dot(lhs, rhs3, preferred_element_type=F32)
            if last:
                nxt.append(slab[:, c2:] + prod)
            else:
                nxt.append(jnp.concatenate([prod[:, :c2], slab[:, c2:] + prod[:, c2:]], axis=-1))
        slabs = nxt
    return slabs


def _gdn_kernel(*refs, T, sps, has_s0):
    if has_s0:
        (u_ref, cw_ref, alog_ref, dtb_ref, exp_ref, tril_ref, mask_ref, bd_ref, gn_ref, s0_ref,
         o_ref, sfin_ref, q_s, k_s, v_s, la_s, be_s, uw_s, ww_s, at_s, qin_s, kt_s, al_s, of_s, st_s) = refs
    else:
        (u_ref, cw_ref, alog_ref, dtb_ref, exp_ref, tril_ref, mask_ref, bd_ref, gn_ref, _,
         o_ref, sfin_ref, q_s, k_s, v_s, la_s, be_s, uw_s, ww_s, at_s, qin_s, kt_s, al_s, of_s, st_s) = refs
    n = T // CHUNK
    C = CHUNK
    bd = bd_ref[...]

    qkv = _silu(_conv3(u_ref[:, 0:3 * GROUP_W], cw_ref, T))
    q = qkv[:, 0:GROUP_W]
    k = qkv[:, GROUP_W:2 * GROUP_W]
    q_s[...] = q * lax.rsqrt(_sel_dot_right(q * q, bd) + 1e-6) * (HEAD_W ** -0.5)
    k_s[...] = k * lax.rsqrt(_sel_dot_right(k * k, bd) + 1e-6)
    v_s[...] = qkv[:, 2 * GROUP_W:3 * GROUP_W]

    ab = u_ref[:, 4 * GROUP_W:4 * GROUP_W + LANE]
    xa = ab + dtb_ref[...]
    softplus = jnp.maximum(xa, 0.0) + jnp.log(1.0 + jnp.exp(-jnp.abs(xa)))
    log_a = -jnp.exp(alog_ref[...]) * softplus
    lane = lax.broadcasted_iota(jnp.int32, ab.shape, 1)
    narrow = jnp.where(lane < 8, log_a, _sigmoid(ab))
    for d in range(2):
        wide = _dot_sel(narrow, exp_ref[d])
        la_s[d] = wide[:, 0:GROUP_W]
        be_s[d] = wide[:, GROUP_W:2 * GROUP_W]
        for s in range(sps):
            if has_s0:
                st_s[s, d] = jnp.concatenate([s0_ref[s, d, h] for h in range(N_HEADS)], axis=-1)
            else:
                st_s[s, d] = jnp.zeros((HEAD_W, GROUP_W), F32)

    def prepare(cidx):
        r0 = pl.multiple_of(cidx * C, C)
        rows = pl.ds(r0, C)
        arow = pl.ds(pl.multiple_of(cidx * 8, 8), 8)
        q = q_s[rows, :]
        k = k_s[rows, :]
        v = v_s[rows, :]
        systems = []
        attns = []
        kts = []
        for d in range(2):
            incl = mask_ref[d, 0] > 0.5
            strict = mask_ref[d, 1]
            la = la_s[d, rows, :]
            be = be_s[d, rows, :]
            gx = _sel_dot(tril_ref[d], la)
            gtot = jnp.sum(la, axis=0, keepdims=True)
            eg = jnp.exp(gx)
            kout = k * jnp.exp(gtot - gx)
            qin_s[d, rows, :] = q * eg
            al_s[d, arow, :] = jnp.broadcast_to(jnp.exp(gtot), (8, GROUP_W))
            kb = k * be
            vb = v * be
            kbg = kb * eg
            for h in range(N_HEADS):
                sl = slice(h * HEAD_W, (h + 1) * HEAD_W)
                gh = gx[:, sl]
                dmat = gh - gh.T
                dec = jnp.where(incl, jnp.exp(jnp.where(incl, dmat, 0.0)), 0.0)
                qk = _bdot_nt(jnp.concatenate([kb[:, sl], q[:, sl]], axis=0), k[:, sl])
                nmat = -(qk[:C] * dec * strict)
                systems.append((jnp.concatenate([vb[:, sl], kbg[:, sl]], axis=-1), nmat))
                attns.append(qk[C:] * dec)
                kts.append(kout[:, sl].T)
        sols = _solve_unit_lower(systems)
        for d in range(2):
            mine = sols[d * N_HEADS:(d + 1) * N_HEADS]
            uw_s[d, rows, :] = jnp.concatenate([x[:, :HEAD_W] for x in mine], axis=-1)
            ww_s[d, rows, :] = jnp.concatenate([x[:, HEAD_W:] for x in mine], axis=-1)
            at_s[d, rows, :] = jnp.concatenate(attns[d * N_HEADS:(d + 1) * N_HEADS], axis=-1)
            kt_s[d, rows, :] = jnp.concatenate(kts[d * N_HEADS:(d + 1) * N_HEADS], axis=-1)

    def prep_body(i, carry):
        for j in range(GDN_UNROLL):
            prepare(i * GDN_UNROLL + j)
        return carry

    lax.fori_loop(0, sps * n // GDN_UNROLL, prep_body, 0)

    def chunk(ci, carry):
        scans = [(s, d) for s in range(sps) for d in range(2)]
        units = [(j, p) for j in range(len(scans)) for p in range(N_PAIRS)]
        rows, alast = [], []
        for s, d in scans:
            cidx = s * n + (ci if d == 0 else n - 1 - ci)
            rows.append(pl.ds(pl.multiple_of(cidx * C, C), C))
            alast.append(al_s[d, pl.ds(pl.multiple_of(cidx * 8, 8), 1), :])
        st = [st_s[s, d] for s, d in scans]
        dirs = [d for s, d in scans]
        lanes = [slice(p * LANE, (p + 1) * LANE) for p in range(N_PAIRS)]
        both = [jnp.dot(jnp.concatenate([ww_s[dirs[j], rows[j], lanes[p]], qin_s[dirs[j], rows[j], lanes[p]]],
                                        axis=0).astype(BF16),
                        _pair_blockdiag(st[j][:, lanes[p]].astype(BF16)), preferred_element_type=F32)
                for j, p in units]
        vnew = [uw_s[dirs[j], rows[j], lanes[p]] - both[i][:C] for i, (j, p) in enumerate(units)]
        upd = [jnp.dot(jnp.concatenate([at_s[dirs[j], rows[j], lanes[p]], kt_s[dirs[j], rows[j], lanes[p]]],
                                       axis=0).astype(BF16),
                       _pair_blockdiag(vnew[i].astype(BF16)), preferred_element_type=F32)
               for i, (j, p) in enumerate(units)]
        for j, (s, d) in enumerate(scans):
            idx = range(j * N_PAIRS, (j + 1) * N_PAIRS)
            of_s[d, rows[j], :] = jnp.concatenate([both[i][C:] + upd[i][:C] for i in idx], axis=-1)
            st_s[s, d] = st[j] * alast[j] + jnp.concatenate([upd[i][C:] for i in idx], axis=-1)
        return carry

    lax.fori_loop(0, n, chunk, 0)
    o_ref[...] = _head_norm_gate(of_s[0] + of_s[1], bd, gn_ref[...], u_ref[:, 3 * GROUP_W:4 * GROUP_W])
    for s in range(sps):
        for d in range(2):
            for h in range(N_HEADS):
                sfin_ref[s, d, h] = st_s[s, d][:, h * HEAD_W:(h + 1) * HEAD_W]


def _gdn_call(u_gd, cw, alog, dtb, gn, consts, s0, collect, l, T, nb, row0):
    tril, masks, expand, bd = consts
    sps = max(1, SCAN_STEP_ROWS // T)
    rows = sps * T
    tb = row0 // rows
    has_s0 = s0 is not None
    in_specs = [
        pl.BlockSpec((rows, GD_PAD), lambda b: (tb + b, 0)),
        pl.BlockSpec((None, 3, 3 * GROUP_W), lambda b: (l, 0, 0)),
        pl.BlockSpec((None, 1, LANE), lambda b: (l, 0, 0)),
        pl.BlockSpec((None, 1, LANE), lambda b: (l, 0, 0)),
        pl.BlockSpec((2, LANE, 2 * GROUP_W), lambda b: (0, 0, 0)),
        pl.BlockSpec((2, CHUNK, CHUNK), lambda b: (0, 0, 0)),
        pl.BlockSpec((2, 2, CHUNK, CHUNK), lambda b: (0, 0, 0, 0)),
        pl.BlockSpec((GROUP_W, GROUP_W), lambda b: (0, 0)),
        pl.BlockSpec((None, 1, GROUP_W), lambda b: (l, 0, 0)),
    ]
    args = [u_gd, cw, alog, dtb, expand, tril, masks, bd, gn]
    state_spec, state_shape, aliases = _state_io(in_specs, args, s0, collect, l, nb, per_step=sps)
    seq = pltpu.VMEM((2, rows, GROUP_W), F32)
    return pl.pallas_call(
        functools.partial(_gdn_kernel, T=T, sps=sps, has_s0=has_s0),
        grid=(nb // sps,),
        in_specs=in_specs,
        out_specs=[pl.BlockSpec((rows, GROUP_W), lambda b: (b, 0)), state_spec],
        out_shape=[jax.ShapeDtypeStruct((nb * T, GROUP_W), F32), state_shape],
        input_output_aliases=aliases,
        scratch_shapes=[pltpu.VMEM((rows, GROUP_W), F32)] * 3 + [seq, seq, seq, seq, seq, seq, seq,
            pltpu.VMEM((2, rows // CHUNK * 8, GROUP_W), F32), seq, pltpu.VMEM((sps, 2, HEAD_W, GROUP_W), F32)],
        compiler_params=_cparams(("parallel",)),
        name="gdn",
    )(*args)


def _hyfilt_kernel(z_ref, win_ref, fh_ref, fl_ref, w1_ref, b1_ref, fr_ref, w2_ref, b2_ref, w3_ref, o_ref, *, T):
    fr = fr_ref[...]
    h = jnp.sin(fr * (_dot3(z_ref[...], w1_ref[...]) + b1_ref[...]))
    h = jnp.sin(fr * (_dot3(h, w2_ref[...]) + b2_ref[...]))
    h = _dot3(h, w3_ref[...])
    win = win_ref[...]
    hf = h[:, 0:GROUP_W] * win
    hb = h[:, GROUP_W:2 * GROUP_W] * win
    row = lax.broadcasted_iota(jnp.int32, hb.shape, 0)
    hb = jnp.where(row == 0, 0.0, hb)
    for rows, taps in ((slice(0, T), (hf + hb).astype(BF16)), (slice(T, 2 * T), (hf - hb).astype(BF16))):
        o_ref[rows, :] = (jnp.dot(fh_ref[rows, :], taps, preferred_element_type=F32)
                          + jnp.dot(fl_ref[rows, :], taps, preferred_element_type=F32))


def _hyfilt_call(T, zp, win, fh, fl, w1p, b1, freq, w2, b2, w3):
    c2 = lambda l: (0, 0)
    return pl.pallas_call(
        functools.partial(_hyfilt_kernel, T=T),
        grid=(DEPTH,),
        in_specs=[
            pl.BlockSpec((T, LANE), c2),
            pl.BlockSpec((T, GROUP_W), c2),
            pl.BlockSpec((2 * T, T), c2),
            pl.BlockSpec((2 * T, T), c2),
            pl.BlockSpec((None, LANE, HY_FH), lambda l: (l, 0, 0)),
            pl.BlockSpec((None, 1, HY_FH), lambda l: (l, 0, 0)),
            pl.BlockSpec((None, 1, HY_FH), lambda l: (l, 0, 0)),
            pl.BlockSpec((None, HY_FH, HY_FH), lambda l: (l, 0, 0)),
            pl.BlockSpec((None, 1, HY_FH), lambda l: (l, 0, 0)),
            pl.BlockSpec((None, HY_FH, 2 * GROUP_W), lambda l: (l, 0, 0)),
        ],
        out_specs=pl.BlockSpec((None, 2 * T, GROUP_W), lambda l: (l, 0, 0)),
        out_shape=jax.ShapeDtypeStruct((DEPTH, 2 * T, GROUP_W), F32),
        compiler_params=_cparams(("parallel",)),
        name="hyfilt",
    )(zp, win, fh, fl, w1p, b1, freq, w2, b2, w3)


def _hyena_kernel(u_ref, cw_ref, cb_ref, spec_ref, skip_ref, fh_ref, fl_ref, ih_ref, il_ref, o_ref, *, T):
    uc = _conv3(u_ref[...], cw_ref, T) + cb_ref[...]
    x0 = uc[:, 0:GROUP_W]
    z = uc[:, GROUP_W:2 * GROUP_W] * uc[:, 2 * GROUP_W:3 * GROUP_W]
    zb = z.astype(BF16)
    zs = (jnp.dot(fh_ref[...], zb, preferred_element_type=F32)
          + jnp.dot(fl_ref[...], zb, preferred_element_type=F32))
    ar, ai = zs[0:T], zs[T:2 * T]
    br, bi = spec_ref[0:T, :], spec_ref[T:2 * T, :]
    pb = jnp.concatenate([ar * br - ai * bi, ar * bi + ai * br], axis=0).astype(BF16)
    y = (jnp.dot(ih_ref[...], pb, preferred_element_type=F32)
         + jnp.dot(il_ref[...], pb, preferred_element_type=F32))
    o_ref[...] = x0 * (y + z * skip_ref[...])


def _hyena_call(u_hy, cw, cb, spec, skip, dft, l, T, nb, row0):
    fh, fl, ih, il = dft
    tb = row0 // T
    c2 = lambda b: (0, 0)
    return pl.pallas_call(
        functools.partial(_hyena_kernel, T=T),
        grid=(nb,),
        in_specs=[
            pl.BlockSpec((T, HY_COLS), lambda b: (tb + b, 0)),
            pl.BlockSpec((None, 3, HY_COLS), lambda b: (l, 0, 0)),
            pl.BlockSpec((None, 1, HY_COLS), lambda b: (l, 0, 0)),
            pl.BlockSpec((None, 2 * T, GROUP_W), lambda b: (l, 0, 0)),
            pl.BlockSpec((None, 1, GROUP_W), lambda b: (l, 0, 0)),
            pl.BlockSpec((2 * T, T), c2),
            pl.BlockSpec((2 * T, T), c2),
            pl.BlockSpec((T, 2 * T), c2),
            pl.BlockSpec((T, 2 * T), c2),
        ],
        out_specs=pl.BlockSpec((T, GROUP_W), lambda b: (b, 0)),
        out_shape=jax.ShapeDtypeStruct((nb * T, GROUP_W), F32),
        compiler_params=_cparams(("parallel",)),
        name="hyena",
    )(u_hy, cw, cb, spec, skip, fh, fl, ih, il)


def _rope(x, cosf, sinf):
    lane = lax.broadcasted_iota(jnp.int32, x.shape, 1)
    half = MLA_ROPE // 2
    partner = jnp.where(lane < MLA_NOPE + half, pltpu.roll(x, LANE - half, 1), pltpu.roll(x, half, 1))
    return x * cosf + partner * sinf


def _qk_norm(x, g):
    ms = jnp.sum(x * x, axis=-1, keepdims=True) * (1.0 / MLA_QK)
    return x * lax.rsqrt(ms + RMS_EPS) * g


def _mla_kernel(*refs, T, ctx):
    if ctx:
        (u_ref, qn_ref, wq_ref, kvn_ref, wkv_ref, qkn_ref, cos_ref, sin_ref, cckv_ref, ckr_ref, o_ref,
         q_s, k_s, v_s) = refs
    else:
        (u_ref, qn_ref, wq_ref, kvn_ref, wkv_ref, qkn_ref, _, _, o_ref, ckv_ref, kr_ref, q_s, k_s, v_s) = refs
    n_keys = k_s.shape[1]
    u = u_ref[...]
    cq = _rms(u[:, 0:MLA_Q_LORA], qn_ref[...])
    ckv = _rms(u[:, MLA_Q_LORA:MLA_Q_LORA + MLA_KV_LORA], kvn_ref[...])
    kr = u[:, MLA_Q_LORA + MLA_KV_LORA:MLA_Q_LORA + MLA_KV_LORA + MLA_ROPE]
    if not ctx:
        ckv_ref[...] = ckv
        kr_ref[...] = kr
    q_all = _bdot(cq, wq_ref[...])
    kv = _bdot(ckv, wkv_ref[...])
    gq = qkn_ref[0:1, :]
    gk = qkn_ref[1:2, :]
    if ctx:
        kvc = _bdot(cckv_ref[...], wkv_ref[...])
        krc = ckr_ref[...]
        cosf, sinf = cos_ref[...], sin_ref[...]
    q_scale = MLA_QK ** -0.5 * math.log2(math.e)
    kr_tile = jnp.concatenate([jnp.zeros((T, MLA_NOPE), F32), kr, jnp.zeros((T, LANE - MLA_QK), F32)], axis=-1)
    kr_rot = kr_tile * gk
    if ctx:
        kr_rot = _rope(kr_rot, cosf, sinf)
    nope_lane = lax.broadcasted_iota(jnp.int32, (T, LANE), 1) < MLA_NOPE
    for h in range(N_HEADS):
        qh = _qk_norm(q_all[:, h * LANE:(h + 1) * LANE], gq)
        k_nope = jnp.concatenate([kv[:, h * HEAD_W:(h + 1) * HEAD_W], jnp.zeros((T, LANE - MLA_NOPE), F32)], axis=-1)
        ms = jnp.sum(k_nope * k_nope + kr_tile * kr_tile, axis=-1, keepdims=True) * (1.0 / MLA_QK)
        kh = jnp.where(nope_lane, k_nope * gk, kr_rot) * lax.rsqrt(ms + RMS_EPS)
        if ctx:
            qh = _rope(qh, cosf, sinf)
            zc = jnp.zeros((n_keys - T, LANE - MLA_QK), F32)
            kc = _qk_norm(jnp.concatenate([kvc[:, h * HEAD_W:(h + 1) * HEAD_W], krc, zc], axis=-1), gk)
            k_s[h, T:n_keys, :] = kc.astype(BF16)
        q_s[:, h * LANE:(h + 1) * LANE] = (qh * q_scale).astype(BF16)
        k_s[h, 0:T, :] = kh.astype(BF16)
    for p in range(N_PAIRS):
        lanes = slice(GROUP_W + p * LANE, GROUP_W + (p + 1) * LANE)
        vp = kv[:, lanes]
        if ctx:
            vp = jnp.concatenate([vp, kvc[:, lanes]], axis=0)
        v_s[p] = _pair_blockdiag(vp.astype(BF16))

    def q_block(qb, carry):
        rows = pl.ds(pl.multiple_of(qb * ATT_QBLOCK, ATT_QBLOCK), ATT_QBLOCK)
        lane = lax.broadcasted_iota(jnp.int32, (ATT_QBLOCK, LANE), 1)
        for p in range(N_PAIRS):
            es, sums = [], []
            for h in (2 * p, 2 * p + 1):
                s = lax.dot_general(q_s[rows, h * LANE:(h + 1) * LANE], k_s[h], (((1,), (1,)), ((), ())),
                                    preferred_element_type=F32)
                e = jnp.exp2(s - jnp.max(s, axis=-1, keepdims=True))
                sums.append(jnp.sum(e, axis=-1, keepdims=True))
                es.append(e.astype(BF16))
            o = jnp.dot(jnp.concatenate(es, axis=-1), v_s[p], preferred_element_type=F32)
            o_ref[rows, p * LANE:(p + 1) * LANE] = o / jnp.where(lane < HEAD_W, sums[0], sums[1])
        return carry

    lax.fori_loop(0, T // ATT_QBLOCK, q_block, 0)


def _mla_call(u_mla, qn, wq, kvn, wkv, qkn, rope, cache, collect, l, T, nb, row0):
    tb = row0 // T
    ctx = cache is not None
    n_keys = T + (PAST_LEN if ctx else 0)
    c2 = lambda b: (0, 0)
    in_specs = [
        pl.BlockSpec((T, MLA_PAD), lambda b: (tb + b, 0)),
        pl.BlockSpec((None, 1, MLA_Q_LORA), lambda b: (l, 0, 0)),
        pl.BlockSpec((None, MLA_Q_LORA, N_HEADS * LANE), lambda b: (l, 0, 0)),
        pl.BlockSpec((None, 1, MLA_KV_LORA), lambda b: (l, 0, 0)),
        pl.BlockSpec((None, MLA_KV_LORA, 2 * GROUP_W), lambda b: (l, 0, 0)),
        pl.BlockSpec((None, 2, LANE), lambda b: (l, 0, 0)),
    ]
    args = [u_mla, qn, wq, kvn, wkv, qkn]
    out_specs = [pl.BlockSpec((T, GROUP_W), lambda b: (b, 0))]
    out_shape = [jax.ShapeDtypeStruct((nb * T, GROUP_W), F32)]
    if ctx:
        in_specs += [
            pl.BlockSpec((T, LANE), c2),
            pl.BlockSpec((T, LANE), c2),
            pl.BlockSpec((None, None, PAST_LEN, MLA_KV_LORA), lambda b: (b, l, 0, 0)),
            pl.BlockSpec((None, None, PAST_LEN, MLA_ROPE), lambda b: (b, l, 0, 0)),
        ]
        args += [rope[0], rope[1], cache[0], cache[1]]
        aliases = {}
    else:
        in_specs += [pl.BlockSpec(memory_space=pl.ANY)] * 2
        aliases = {len(args): 1, len(args) + 1: 2}
        args += list(collect)
        out_specs += [pl.BlockSpec((None, None, T, MLA_KV_LORA), lambda b: (b, l, 0, 0)),
                      pl.BlockSpec((None, None, T, MLA_ROPE), lambda b: (b, l, 0, 0))]
        out_shape += [jax.ShapeDtypeStruct(a.shape, F32) for a in collect]
    return pl.pallas_call(
        functools.partial(_mla_kernel, T=T, ctx=ctx),
        grid=(nb,),
        in_specs=in_specs,
        out_specs=out_specs,
        out_shape=out_shape,
        input_output_aliases=aliases,
        scratch_shapes=[pltpu.VMEM((T, N_HEADS * LANE), BF16), pltpu.VMEM((N_HEADS, n_keys, LANE), BF16),
                        pltpu.VMEM((N_PAIRS, 2 * n_keys, LANE), BF16)],
        compiler_params=_cparams(("parallel",)),
        name="mla",
    )(*args)


def _pad_cols(w, width):
    return jnp.pad(w, [(0, 0)] * (w.ndim - 1) + [(0, width - w.shape[-1])])


W_IN_PREP_COLS = 256


def _prep_w_in_kernel(wt_ref, o_hg, o_hy, o_mla, o_gd):
    start = 0
    for o_ref, cols in ((o_hg, HG_COLS), (o_hy, HY_COLS), (o_mla, MLA_COLS), (o_gd, GD_COLS)):
        width = o_ref.shape[-1]
        for c0 in range(0, width, W_IN_PREP_COLS):
            n_out = min(W_IN_PREP_COLS, width - c0)
            n_real = max(0, min(n_out, cols - c0))
            piece = wt_ref[start + c0:start + c0 + n_real, :]
            if n_real < n_out:
                piece = jnp.concatenate([piece, jnp.zeros((n_out - n_real, D_MODEL), F32)], axis=0)
            o_ref[:, c0:c0 + n_out] = piece.T.astype(BF16)
        start += cols


def _prep_w_in(w_in):
    widths = (HG_COLS, HY_COLS, MLA_PAD, GD_PAD)
    n_cols = w_in.shape[-1]
    return pl.pallas_call(
        _prep_w_in_kernel,
        grid=(DEPTH,),
        in_specs=[pl.BlockSpec((None, n_cols, D_MODEL), lambda l: (l, 0, 0))],
        out_specs=[pl.BlockSpec((None, D_MODEL, w), lambda l: (l, 0, 0)) for w in widths],
        out_shape=[jax.ShapeDtypeStruct((DEPTH, D_MODEL, w), BF16) for w in widths],
        compiler_params=_cparams(("parallel",)),
        name="w_in_prep",
    )(jnp.swapaxes(w_in, 1, 2))


def _prep_wq(w_q_up):
    w = w_q_up.reshape(DEPTH, MLA_Q_LORA, N_HEADS, MLA_QK)
    return _pad_cols(w, LANE).reshape(DEPTH, MLA_Q_LORA, N_HEADS * LANE).astype(BF16)


def _prep_wkv(w_kv_up):
    w = w_kv_up.reshape(DEPTH, MLA_KV_LORA, N_HEADS, 2, HEAD_W)
    return w.transpose(0, 1, 3, 2, 4).reshape(DEPTH, MLA_KV_LORA, 2 * GROUP_W).astype(BF16)


def _lower_bounds(hgrn_lb):
    lb = jnp.cumsum(jax.nn.softmax(hgrn_lb.astype(F32), axis=0), axis=0)
    return lb - lb[0]


def kernel(x_prompt, x_sample, cache_mla_ckv, cache_mla_krope, state_hgrn, state_gdn, c, c_ctx, w_ada, b_ada, norm_ffn, w_ffn_gu, w_ffn_down, norm_mix, w_in, w_out, hgrn_lb, hgrn_norm, hy_conv_w, hy_conv_b, hy_w1, hy_b1, hy_freq, hy_w2, hy_b2, hy_w3, hy_skip, mla_q_norm_a, mla_w_q_up, mla_kv_norm_a, mla_w_kv_up, mla_qk_norm, gdn_conv_w, gdn_a_log, gdn_dt_bias, gdn_norm):
    x = (x_prompt.reshape(N_PROMPT, D_MODEL), x_sample.reshape(N_SAMPLE, D_MODEL))

    cond8 = jnp.zeros((8, D_MODEL), F32).at[0].set(c_ctx).at[1:1 + DEC_BATCH].set(c)
    ada = _ada_call(cond8, w_ada, b_ada)

    w_in_parts = _prep_w_in(w_in)
    w_out_bf = w_out.astype(BF16)
    wq = _prep_wq(mla_w_q_up)
    wkv = _prep_wkv(mla_w_kv_up)
    qkn = _pad_cols(mla_qk_norm, LANE)
    lb_all = _lower_bounds(hgrn_lb)
    alog = _pad_cols(gdn_a_log.reshape(DEPTH, 1, 8), LANE)
    dtb = _pad_cols(gdn_dt_bias.reshape(DEPTH, 1, 8), LANE)
    gdn_gn = jnp.tile(gdn_norm, (1, N_HEADS)).reshape(DEPTH, 1, GROUP_W)
    w1p = jnp.pad(hy_w1, ((0, 0), (0, LANE - HY_EMB), (0, 0)))

    bd = jnp.asarray(_block_diag_ones(), BF16)
    hg_tril, hg_m = _hgrn_consts()
    hg_consts = (jnp.asarray(hg_tril, BF16), jnp.asarray(hg_m, F32), bd)
    gd_tril, gd_masks, gd_expand = _gdn_consts()
    gd_consts = (jnp.asarray(gd_tril, BF16), jnp.asarray(gd_masks, F32), jnp.asarray(gd_expand, BF16), bd)
    rope = tuple(jnp.asarray(a) for a in _rope_consts(DEC_SEQ))
    groups = ((SEQ, BATCH, 0), (DEC_SEQ, DEC_BATCH, N_PROMPT))
    dft = {}
    spec = {}
    for T, _, _ in groups:
        fwd, inv = _dft_consts(T)
        fh, fl = _np_split2(fwd)
        ih, il = _np_split2(inv)
        dft[T] = (fh, fl, ih, il)
        zp, win = _hyena_pos_consts(T)
        spec[T] = _hyfilt_call(T, jnp.asarray(zp), jnp.asarray(win), fh, fl, w1p,
                               hy_b1.reshape(DEPTH, 1, HY_FH), hy_freq.reshape(DEPTH, 1, HY_FH), hy_w2,
                               hy_b2.reshape(DEPTH, 1, HY_FH), hy_w3)

    new_ckv = jnp.zeros((BATCH, DEPTH, SEQ, MLA_KV_LORA), F32)
    new_kr = jnp.zeros((BATCH, DEPTH, SEQ, MLA_ROPE), F32)
    new_hg = jnp.zeros((BATCH, DEPTH, 2, N_HEADS, HEAD_W, HEAD_W), F32)
    new_gd = jnp.zeros((BATCH, DEPTH, 2, N_HEADS, HEAD_W, HEAD_W), F32)
    for l in range(DEPTH):
        x = _ffn_call(x, ada, norm_ffn, w_ffn_gu, w_ffn_down, l, 0)
        u_hg, u_hy, u_mla, u_gd = _inproj_call(x, ada, norm_mix, w_in_parts, l)
        outs = []
        for gi, (T, nb, row0) in enumerate(groups):
            latent = gi == 1
            o_hg, s_hg = _hgrn_call(u_hg, lb_all, hgrn_norm, hg_consts, state_hgrn if latent else None,
                                    None if latent else new_hg, l, T, nb, row0)
            o_hy = _hyena_call(u_hy, hy_conv_w, hy_conv_b.reshape(DEPTH, 1, HY_COLS), spec[T],
                               hy_skip.reshape(DEPTH, 1, GROUP_W), dft[T], l, T, nb, row0)
            mla = _mla_call(u_mla, mla_q_norm_a.reshape(DEPTH, 1, MLA_Q_LORA), wq,
                            mla_kv_norm_a.reshape(DEPTH, 1, MLA_KV_LORA), wkv, qkn,
                            rope if latent else None,
                            (cache_mla_ckv, cache_mla_krope) if latent else None,
                            None if latent else (new_ckv, new_kr), l, T, nb, row0)
            o_gd, s_gd = _gdn_call(u_gd, gdn_conv_w, alog, dtb, gdn_gn, gd_consts, state_gdn if latent else None,
                                   None if latent else new_gd, l, T, nb, row0)
            outs.append((o_hg, o_hy, mla[0], o_gd))
            if not latent:
                new_ckv, new_kr, new_hg, new_gd = mla[1], mla[2], s_hg, s_gd
        x = _outproj_call(x, ada, w_out_bf, outs[0], outs[1], l)
        x = _ffn_call(x, ada, norm_ffn, w_ffn_gu, w_ffn_down, l, 1)

    y_prompt = x[:N_PROMPT].reshape(BATCH, SEQ, D_MODEL)
    y_sample = x[N_PROMPT:].reshape(DEC_BATCH, DEC_SEQ, D_MODEL)
    return (y_prompt, y_sample, new_ckv, new_kr, new_hg, new_gd)
```

```python
import functools
import math

import numpy as np
import jax
import jax.numpy as jnp
from jax import lax
from jax.experimental import pallas as pl
from jax.experimental.pallas import tpu as pltpu

F32 = jnp.float32
BF16 = jnp.bfloat16

D_MODEL = 1024
BATCH = 16
SEQ = 256
DEPTH = 4
DEC_BATCH = 2
DEC_SEQ = 1024
PAST_LEN = 256
GRID_W = 64
N_ADA = 9
D_FF = 2816
GROUP_W = 256
CHUNK = 64
RMS_EPS = 1e-6
N_HEADS = 4
HEAD_W = 64
HY_EMB = 33
HY_FH = 64
HY_TARGET = 1e-2
HY_FAST = 0.3
HY_SLOW = 1.5
MLA_NOPE = 64
MLA_ROPE = 32
MLA_QK = MLA_NOPE + MLA_ROPE
MLA_Q_LORA = 256
MLA_KV_LORA = 128
ROPE_BASE = 10000.0

HG_COLS = 5 * GROUP_W
HY_COLS = 3 * GROUP_W
MLA_COLS = MLA_Q_LORA + MLA_KV_LORA + MLA_ROPE
GD_COLS = 4 * GROUP_W + 16
MLA_PAD = 512
GD_PAD = 1152

N_PROMPT = BATCH * SEQ
N_SAMPLE = DEC_BATCH * DEC_SEQ
N_TOK = N_PROMPT + N_SAMPLE
LANE = 128
VMEM_LIMIT = 56 * 1024 * 1024
ROW_TILE = 1024
FF_TILE = 256
ADA_TILE = 1536
ATT_QBLOCK = 256
SCAN_STEP_ROWS = 1024


def _bdot(a, b):
    return jnp.dot(a.astype(BF16), b.astype(BF16), preferred_element_type=F32)


def _bdot_nt(a, b):
    return lax.dot_general(a.astype(BF16), b.astype(BF16), (((1,), (1,)), ((), ())),
                           preferred_element_type=F32)


def _split2(x):
    hi = x.astype(BF16)
    lo = (x - hi.astype(F32)).astype(BF16)
    return hi, lo


def _split3(x):
    hi = x.astype(BF16)
    r = x - hi.astype(F32)
    mid = r.astype(BF16)
    lo = (r - mid.astype(F32)).astype(BF16)
    return hi, mid, lo


def _dot3(a, b):
    ah, al = _split2(a)
    bh, bl = _split2(b)
    return (jnp.dot(ah, bh, preferred_element_type=F32) + jnp.dot(ah, bl, preferred_element_type=F32)
            + jnp.dot(al, bh, preferred_element_type=F32))


def _sel_dot(c, x):
    h, m, l = _split3(x)
    return (jnp.dot(c, h, preferred_element_type=F32) + jnp.dot(c, m, preferred_element_type=F32)
            + jnp.dot(c, l, preferred_element_type=F32))


def _dot_sel(x, c):
    h, m, l = _split3(x)
    return (jnp.dot(h, c, preferred_element_type=F32) + jnp.dot(m, c, preferred_element_type=F32)
            + jnp.dot(l, c, preferred_element_type=F32))


def _sigmoid(x):
    return 1.0 / (1.0 + jnp.exp(-x))


def _silu(x):
    return x * _sigmoid(x)


def _rms(x, g):
    return x * lax.rsqrt(jnp.mean(x * x, axis=-1, keepdims=True) + RMS_EPS) * g


def _cparams(sem):
    return pltpu.CompilerParams(dimension_semantics=sem, vmem_limit_bytes=VMEM_LIMIT)


def _cond_of_tile(i):
    return jnp.maximum(i - (N_PROMPT // ROW_TILE - 1), 0)


def _ada_kernel(c_ref, w_ref, b_ref, o_ref):
    ch, cl = _split2(_silu(c_ref[...]))
    w = w_ref[...].astype(BF16)
    o_ref[...] = (jnp.dot(ch, w, preferred_element_type=F32) + jnp.dot(cl, w, preferred_element_type=F32)
                  + b_ref[...])


def _ada_call(cond8, w_ada, b_ada):
    n = N_ADA * D_MODEL
    out = pl.pallas_call(
        _ada_kernel,
        grid=(DEPTH, n // ADA_TILE),
        in_specs=[
            pl.BlockSpec((8, D_MODEL), lambda l, j: (0, 0)),
            pl.BlockSpec((None, D_MODEL, ADA_TILE), lambda l, j: (l, 0, j)),
            pl.BlockSpec((None, 1, ADA_TILE), lambda l, j: (l, 0, j)),
        ],
        out_specs=pl.BlockSpec((None, 8, ADA_TILE), lambda l, j: (l, 0, j)),
        out_shape=jax.ShapeDtypeStruct((DEPTH, 8, n), F32),
        compiler_params=_cparams(("parallel", "parallel")),
        name="ada",
    )(cond8, w_ada, b_ada.reshape(DEPTH, 1, n))
    return out.reshape(DEPTH, 8, N_ADA, D_MODEL)


FFN_SUBTILES = 2


def _ffn_kernel(*refs, sub, split):
    nx = 1 if split is None else 2
    x_refs = refs[:nx]
    ada_refs = refs[nx:nx + FFN_SUBTILES]
    g_ref, wg_ref, wu_ref, wd_ref, o_ref, h_scr = refs[nx + FFN_SUBTILES:]
    i = pl.program_id(0)
    f = pl.program_id(1)

    def prologue(x_ref):
        for r, ada_ref in enumerate(ada_refs):
            rows = slice(r * ROW_TILE, (r + 1) * ROW_TILE)
            y = _rms(x_ref[rows, :], g_ref[...])
            h = y * (1.0 + ada_ref[3 * sub + 1:3 * sub + 2, :]) + ada_ref[3 * sub:3 * sub + 1, :]
            h_scr[rows, :] = h.astype(BF16)
        o_ref[...] = jnp.zeros_like(o_ref)

    def epilogue(x_ref):
        for r, ada_ref in enumerate(ada_refs):
            rows = slice(r * ROW_TILE, (r + 1) * ROW_TILE)
            o_ref[rows, :] = x_ref[rows, :] + 0.5 * ada_ref[3 * sub + 2:3 * sub + 3, :] * o_ref[rows, :]

    def on(step, fn):
        if split is None:
            pl.when(f == step)(functools.partial(fn, x_refs[0]))
        else:
            pl.when((f == step) & (i < split))(functools.partial(fn, x_refs[0]))
            pl.when((f == step) & (i >= split))(functools.partial(fn, x_refs[1]))

    on(0, prologue)
    wg = wg_ref[...].astype(BF16)
    wu = wu_ref[...].astype(BF16)
    wd = wd_ref[...].astype(BF16)
    for r in range(FFN_SUBTILES):
        rows = slice(r * ROW_TILE, (r + 1) * ROW_TILE)
        h = h_scr[rows, :]
        gate = jnp.dot(h, wg, preferred_element_type=F32)
        up = jnp.dot(h, wu, preferred_element_type=F32)
        a = (_silu(gate) * up).astype(BF16)
        o_ref[rows, :] += jnp.dot(a, wd, preferred_element_type=F32)
    on(pl.num_programs(1) - 1, epilogue)


def _ffn_call(xs, ada, norm_ffn, w_gu, w_down, l, j):
    sub = 2 * j
    nf = D_FF // FF_TILE
    rows = FFN_SUBTILES * ROW_TILE

    def ada_spec(r):
        return pl.BlockSpec((None, None, N_ADA, D_MODEL),
                            lambda i, f: (l, _cond_of_tile(i * FFN_SUBTILES + r), 0, 0))

    if isinstance(xs, tuple):
        split = xs[0].shape[0] // rows
        n_tail = xs[1].shape[0] // rows
        x_specs = [pl.BlockSpec((rows, D_MODEL), lambda i, f: (jnp.minimum(i, split - 1), 0),
                                pipeline_mode=pl.Buffered(1)),
                   pl.BlockSpec((rows, D_MODEL), lambda i, f: (jnp.clip(i - split, 0, n_tail - 1), 0),
                                pipeline_mode=pl.Buffered(1))]
    else:
        split = None
        xs = (xs,)
        x_specs = [pl.BlockSpec((rows, D_MODEL), lambda i, f: (i, 0), pipeline_mode=pl.Buffered(1))]

    return pl.pallas_call(
        functools.partial(_ffn_kernel, sub=sub, split=split),
        grid=(N_TOK // rows, nf),
        in_specs=x_specs
        + [ada_spec(r) for r in range(FFN_SUBTILES)] + [
            pl.BlockSpec((None, None, 1, D_MODEL), lambda i, f: (l, j, 0, 0)),
            pl.BlockSpec((None, None, D_MODEL, FF_TILE), lambda i, f: (l, j, 0, f)),
            pl.BlockSpec((None, None, D_MODEL, FF_TILE), lambda i, f: (l, j, 0, nf + f)),
            pl.BlockSpec((None, None, FF_TILE, D_MODEL), lambda i, f: (l, j, f, 0)),
        ],
        out_specs=pl.BlockSpec((rows, D_MODEL), lambda i, f: (i, 0)),
        out_shape=jax.ShapeDtypeStruct((N_TOK, D_MODEL), F32),
        scratch_shapes=[pltpu.VMEM((rows, D_MODEL), BF16)],
        compiler_params=_cparams(("parallel", "arbitrary")),
        name="ffn",
    )(*xs, *([ada] * FFN_SUBTILES), norm_ffn.reshape(DEPTH, 2, 1, D_MODEL), w_gu, w_gu, w_down)


IN_TILE = 512


def _inproj_kernel(x_ref, ada_ref, g_ref, w1, w2, w3, w4, o1, o2, o3, o4):
    y = _rms(x_ref[...], g_ref[...])
    h = (y * (1.0 + ada_ref[4:5, :]) + ada_ref[3:4, :]).astype(BF16)
    for w, o in ((w1, o1), (w2, o2), (w3, o3), (w4, o4)):
        o[...] = jnp.dot(h, w[...], preferred_element_type=F32)


def _inproj_call(x, ada, norm_mix, ws, l):
    widths = (HG_COLS, HY_COLS, MLA_PAD, GD_PAD)
    per = ROW_TILE // IN_TILE
    return pl.pallas_call(
        _inproj_kernel,
        grid=(N_TOK // IN_TILE,),
        in_specs=[
            pl.BlockSpec((IN_TILE, D_MODEL), lambda i: (i, 0)),
            pl.BlockSpec((None, None, N_ADA, D_MODEL), lambda i: (l, _cond_of_tile(i // per), 0, 0)),
            pl.BlockSpec((None, 1, D_MODEL), lambda i: (l, 0, 0)),
        ] + [pl.BlockSpec((None, D_MODEL, w), lambda i: (l, 0, 0)) for w in widths],
        out_specs=[pl.BlockSpec((IN_TILE, w), lambda i: (i, 0)) for w in widths],
        out_shape=[jax.ShapeDtypeStruct((N_TOK, w), F32) for w in widths],
        compiler_params=_cparams(("parallel",)),
        name="inproj",
    )(x, ada, norm_mix.reshape(DEPTH, 1, D_MODEL), *ws)


OUT_TILE = 1024


def _outproj_kernel(x_ref, ada_ref, w_ref, *refs):
    o_ref = refs[-1]
    i = pl.program_id(0)
    n_p = N_PROMPT // OUT_TILE

    def run(srcs):
        acc = jnp.zeros((OUT_TILE, D_MODEL), F32)
        for g, s in enumerate(srcs):
            acc += jnp.dot(s[...].astype(BF16), w_ref[g * GROUP_W:(g + 1) * GROUP_W, :],
                           preferred_element_type=F32)
        o_ref[...] = x_ref[...] + ada_ref[5:6, :] * acc

    @pl.when(i < n_p)
    def _():
        run(refs[0:4])

    @pl.when(i >= n_p)
    def _():
        run(refs[4:8])


def _outproj_call(x, ada, w_out_bf, o_p, o_s, l):
    per = ROW_TILE // OUT_TILE
    n_p = N_PROMPT // OUT_TILE
    n_s = N_SAMPLE // OUT_TILE
    return pl.pallas_call(
        _outproj_kernel,
        grid=(N_TOK // OUT_TILE,),
        in_specs=[
            pl.BlockSpec((OUT_TILE, D_MODEL), lambda i: (i, 0)),
            pl.BlockSpec((None, None, N_ADA, D_MODEL), lambda i: (l, _cond_of_tile(i // per), 0, 0)),
            pl.BlockSpec((None, D_MODEL, D_MODEL), lambda i: (l, 0, 0)),
        ] + [pl.BlockSpec((OUT_TILE, GROUP_W), lambda i: (jnp.minimum(i, n_p - 1), 0))] * 4
          + [pl.BlockSpec((OUT_TILE, GROUP_W), lambda i: (jnp.clip(i - n_p, 0, n_s - 1), 0))] * 4,
        out_specs=pl.BlockSpec((OUT_TILE, D_MODEL), lambda i: (i, 0)),
        out_shape=jax.ShapeDtypeStruct((N_TOK, D_MODEL), F32),
        compiler_params=_cparams(("parallel",)),
        name="outproj",
    )(x, ada, w_out_bf, *o_p, *o_s)


def _block_diag_ones():
    idx = np.arange(GROUP_W) // HEAD_W
    return (idx[:, None] == idx[None, :]).astype(np.float32)


def _hgrn_consts():
    C = CHUNK
    i = np.arange(C)[:, None]
    j = np.arange(C)[None, :]
    masks = []
    s = C // 2
    while s >= 1:
        up_i = (i // s) % 2 == 1
        up_j = (j // s) % 2 == 1
        masks.append(up_i & (~up_j) & (i // (2 * s) == j // (2 * s)))
        s //= 2
    masks.append(j <= i)
    fwd_m = np.stack([m.astype(np.float32) for m in masks])
    bwd_m = np.stack([m.astype(np.float32)[::-1, ::-1] for m in masks])
    tril = np.stack([(j <= i), (j >= i)]).astype(np.float32)
    return tril, np.tile(np.stack([fwd_m, bwd_m]), (1, 1, 1, 2))


def _gdn_consts():
    C = CHUNK
    i = np.arange(C)[:, None]
    t = np.arange(C)[None, :]
    tril = np.stack([(t <= i), (t >= i)]).astype(np.float32)
    masks = np.stack([np.stack([(t <= i), (t < i)]), np.stack([(t >= i), (t > i)])]).astype(np.float32)
    expand = np.zeros((2, LANE, 2 * GROUP_W), np.float32)
    for d in range(2):
        for h in range(N_HEADS):
            expand[d, d * N_HEADS + h, h * HEAD_W:(h + 1) * HEAD_W] = 1.0
            expand[d, 8 + d * N_HEADS + h, GROUP_W + h * HEAD_W:GROUP_W + (h + 1) * HEAD_W] = 1.0
    return tril, masks, expand


def _dft_consts(T):
    n2 = 4 * T
    k = np.arange(T, dtype=np.int64)[:, None]
    s = np.arange(T, dtype=np.int64)[None, :]
    ang = np.pi * (((2 * k + 1) * s) % n2).astype(np.float64) / (2 * T)
    fwd = np.concatenate([np.cos(ang), -np.sin(ang)], axis=0)
    inv = fwd.T / T
    return fwd.astype(np.float32), inv.astype(np.float32)


def _np_split2(x):
    hi = jnp.asarray(x, F32).astype(BF16)
    lo = (jnp.asarray(x, F32) - hi.astype(F32)).astype(BF16)
    return hi, lo


def _hyena_pos_consts(T):
    pos = np.arange(T, dtype=np.float32)
    t = pos / np.float32(T - 1)
    bands = np.linspace(1e-4, (HY_EMB - 1) // 2 - 1, (HY_EMB - 1) // 2, dtype=np.float32)
    ang = (np.float32(2.0 * math.pi / T) * pos[:, None]) * bands[None, :]
    z = np.concatenate([t[:, None], np.cos(ang), -np.sin(ang)], axis=-1).astype(np.float32)
    zp = np.zeros((T, LANE), np.float32)
    zp[:, :HY_EMB] = z
    max_decay = math.log(HY_TARGET) / HY_FAST
    min_decay = math.log(HY_TARGET) / HY_SLOW
    deltas = np.linspace(min_decay, max_decay, GROUP_W, dtype=np.float32)
    window = np.exp(-t[:, None] * np.abs(deltas)[None, :]).astype(np.float32)
    return zp, window


def _rope_consts(T):
    rows = T // GRID_W
    row = np.repeat(np.arange(rows, dtype=np.float32), GRID_W)
    col = (np.arange(T) % GRID_W).astype(np.float32)
    pairs = MLA_ROPE // 4
    inv = (np.float32(ROPE_BASE) ** (-np.arange(pairs, dtype=np.float32) / np.float32(pairs))).astype(np.float32)
    ang = np.concatenate([row[:, None] * inv, col[:, None] * inv], axis=-1).astype(np.float32)
    cos, sin = np.cos(ang), np.sin(ang)
    cosf = np.ones((T, LANE), np.float32)
    sinf = np.zeros((T, LANE), np.float32)
    half = MLA_ROPE // 2
    cosf[:, MLA_NOPE:MLA_NOPE + half] = cos
    cosf[:, MLA_NOPE + half:MLA_QK] = cos
    sinf[:, MLA_NOPE:MLA_NOPE + half] = -sin
    sinf[:, MLA_NOPE + half:MLA_QK] = sin
    return cosf, sinf


def _head_norm_gate(tot, bd, gn, gate):
    ms = _sel_dot_right(tot * tot, bd) * (1.0 / HEAD_W)
    return tot * lax.rsqrt(ms + RMS_EPS) * gn * _silu(gate)


def _sel_dot_right(x, c):
    h, l = _split2(x)
    return jnp.dot(h, c, preferred_element_type=F32) + jnp.dot(l, c, preferred_element_type=F32)


def _block_ref(b, two_s, r):
    C, W = b.shape
    if two_s % 8 == 0:
        b3 = b.reshape(C // two_s, two_s, W)
        return jnp.broadcast_to(b3[:, r:r + 1, :], b3.shape).reshape(C, W)
    pos = lax.broadcasted_iota(jnp.int32, b.shape, 0) % two_s
    out = b
    for p in range(two_s):
        if p != r:
            out = jnp.where(pos == p, pltpu.roll(b, (p - r) % C, 0), out)
    return out


N_PAIRS = N_HEADS // 2
HG_GROUP = 4
HG_DIRECT_MAX = 80.0


def _pair_blockdiag(x):
    lane = lax.broadcasted_iota(jnp.int32, x.shape, 1)
    zero = jnp.zeros_like(x)
    return jnp.concatenate([jnp.where(lane < HEAD_W, x, zero), jnp.where(lane >= HEAD_W, x, zero)], axis=0)


def _hgrn_kernel(*refs, T, sps, has_s0):
    if has_s0:
        (u_ref, lb_ref, gn_ref, tril_ref, lmask_ref, bd_ref, s0_ref,
         o_ref, sfin_ref, oi_s, qin_s, up_s, dc_s, st_s) = refs
    else:
        (u_ref, lb_ref, gn_ref, tril_ref, lmask_ref, bd_ref, _,
         o_ref, sfin_ref, oi_s, qin_s, up_s, dc_s, st_s) = refs
    n = T // CHUNK
    C = CHUNK
    bd = bd_ref[...]
    n_lv = int(math.log2(C))

    log_lb = [jnp.log(lb_ref[d]) for d in range(2)]
    log_1mlb = [jnp.log(1.0 - lb_ref[d]) for d in range(2)]

    def gates(rows, d):
        z = u_ref[rows, (3 + d) * GROUP_W:(4 + d) * GROUP_W]
        t = jnp.exp(-jnp.abs(z))
        log_sig = jnp.minimum(z, 0.0) - jnp.log(1.0 + t)
        c = log_1mlb[d] + log_sig
        m = jnp.maximum(log_lb[d], c)
        lf = m + jnp.log(1.0 + jnp.exp(jnp.minimum(log_lb[d], c) - m))
        sig_neg = jnp.where(z > 0.0, t, 1.0) / (1.0 + t)
        return lf, (1.0 - lb_ref[d]) * sig_neg

    for s in range(sps):
        for d in range(2):
            if has_s0:
                st_s[s, d] = jnp.concatenate([s0_ref[s, d, h].T for h in range(N_HEADS)], axis=-1)
            else:
                st_s[s, d] = jnp.zeros((HEAD_W, GROUP_W), F32)

    def prepare(it, carry):
        units = [(c, d) for c in range(HG_GROUP) for d in range(2)]
        rows = [pl.ds(pl.multiple_of((it * HG_GROUP + c) * C, C), C) for c in range(HG_GROUP)]
        arow = [pl.ds(pl.multiple_of((it * HG_GROUP + c) * 8, 8), 8) for c in range(HG_GROUP)]
        q = [u_ref[rows[c], 0:GROUP_W] * (HEAD_W ** -0.5) for c in range(HG_GROUP)]
        v = [u_ref[rows[c], GROUP_W:2 * GROUP_W] for c in range(HG_GROUP)]
        vt = [[jnp.concatenate([v[c][:, h * HEAD_W:(h + 1) * HEAD_W].T for h in (2 * p, 2 * p + 1)],
                               axis=-1).astype(BF16) for p in range(N_PAIRS)]
              for c in range(HG_GROUP)]
        v_bd = [[_pair_blockdiag(v[c][:, p * LANE:(p + 1) * LANE].astype(BF16)) for p in range(N_PAIRS)]
                for c in range(HG_GROUP)]
        lf, ks = zip(*[gates(rows[c], d) for c, d in units])
        bs = []
        for i, (c, d) in enumerate(units):
            hi, lo = _split2(lf[i])
            tril = tril_ref[d]
            bs.append(jnp.dot(tril, hi, preferred_element_type=F32) + jnp.dot(tril, lo, preferred_element_type=F32))
        tot = [jnp.sum(x, axis=0, keepdims=True) for x in lf]
        mid_row = [C // 2 - 1 if d == 0 else C // 2 for c, d in units]
        spread = [jnp.maximum(jnp.abs(bs[i][0:1] - bs[i][r:r + 1]), jnp.abs(bs[i][C - 1:C] - bs[i][r:r + 1]))
                  for i, r in enumerate(mid_row)]
        widest = functools.reduce(jnp.maximum, spread)
        ko = [(ks[i] * jnp.exp(tot[i] - bs[i])).astype(BF16) for i in range(len(units))]
        up = [[jnp.dot(vt[c][p], _pair_blockdiag(ko[i][:, p * LANE:(p + 1) * LANE]), preferred_element_type=F32)
               for p in range(N_PAIRS)] for i, (c, d) in enumerate(units)]
        for i, (c, d) in enumerate(units):
            qin_s[d, rows[c], :] = q[c] * jnp.exp(bs[i])
            up_s[d, rows[c], :] = jnp.concatenate(up[i], axis=-1)
            dc_s[d, arow[c], :] = jnp.broadcast_to(jnp.exp(tot[i]), (8, GROUP_W))

        def masked_scores(qe, ke, lv):
            out = []
            for i, (c, d) in enumerate(units):
                per_pair = []
                for p in range(N_PAIRS):
                    sl = slice(p * LANE, (p + 1) * LANE)
                    prod = lax.dot_general(qe[i][:, sl], _pair_blockdiag(ke[i][:, sl]), (((1,), (1,)), ((), ())),
                                           preferred_element_type=F32)
                    per_pair.append(jnp.where(lmask_ref[d, lv] > 0.5, prod, 0.0))
                out.append(per_pair)
            return out

        def finish(sc):
            return [jnp.concatenate([jnp.dot(sc[i][p].astype(BF16), v_bd[c][p], preferred_element_type=F32)
                                     for p in range(N_PAIRS)], axis=-1) for i, (c, d) in enumerate(units)]

        def intra_direct():
            mid = [_block_ref(bs[i], C, C // 2 - 1 if d == 0 else C // 2) for i, (c, d) in enumerate(units)]
            qe = [(q[c] * jnp.exp(bs[i] - mid[i])).astype(BF16) for i, (c, d) in enumerate(units)]
            ke = [(ks[i] * jnp.exp(mid[i] - bs[i])).astype(BF16) for i in range(len(units))]
            return tuple(finish(masked_scores(qe, ke, n_lv)))

        def intra_split():
            sc = [[jnp.zeros((C, LANE), F32) for _ in range(N_PAIRS)] for _ in units]
            s = C // 2
            lv = 0
            while s >= 1:
                e = [jnp.exp(-jnp.abs(bs[i] - _block_ref(bs[i], 2 * s, s - 1 if d == 0 else s)))
                     for i, (c, d) in enumerate(units)]
                part = masked_scores([(q[c] * e[i]).astype(BF16) for i, (c, d) in enumerate(units)],
                                     [(ks[i] * e[i]).astype(BF16) for i in range(len(units))], lv)
                sc = [[sc[i][p] + part[i][p] for p in range(N_PAIRS)] for i in range(len(units))]
                s //= 2
                lv += 1
            fin = finish(sc)
            return tuple(fin[i] + _bdot(q[c] * ks[i], bd) * v[c] for i, (c, d) in enumerate(units))

        oi = lax.cond(jnp.max(widest) < HG_DIRECT_MAX, intra_direct, intra_split)
        for i, (c, d) in enumerate(units):
            oi_s[d, rows[c], :] = oi[i]
        return carry

    lax.fori_loop(0, sps * n // HG_GROUP, prepare, 0)

    def chunk(ci, carry):
        scans = [(s, d) for s in range(sps) for d in range(2)]
        cidx = [s * n + (ci if d == 0 else n - 1 - ci) for s, d in scans]
        rows = [pl.ds(pl.multiple_of(c * C, C), C) for c in cidx]
        decay = [dc_s[d, pl.ds(pl.multiple_of(c * 8, 8), 1), :] for (s, d), c in zip(scans, cidx)]
        st = [st_s[s, d] for s, d in scans]
        o_inter = [[lax.dot_general(qin_s[d, rows[j], p * LANE:(p + 1) * LANE].astype(BF16),
                                    _pair_blockdiag(st[j][:, p * LANE:(p + 1) * LANE].astype(BF16)),
                                    (((1,), (1,)), ((), ())), preferred_element_type=F32)
                    for p in range(N_PAIRS)] for j, (s, d) in enumerate(scans)]
        for j, (s, d) in enumerate(scans):
            st_s[s, d] = st[j] * decay[j] + up_s[d, rows[j], :]
            oi_s[d, rows[j], :] = oi_s[d, rows[j], :] + jnp.concatenate(o_inter[j], axis=-1)
        return carry

    lax.fori_loop(0, n, chunk, 0, unroll=2)
    o_ref[...] = _head_norm_gate(oi_s[0] + oi_s[1], bd, gn_ref[...], u_ref[:, 2 * GROUP_W:3 * GROUP_W])
    for s in range(sps):
        for d in range(2):
            for h in range(N_HEADS):
                sfin_ref[s, d, h] = st_s[s, d][:, h * HEAD_W:(h + 1) * HEAD_W].T


def _state_io(in_specs, args, s0, collect, l, nb, per_step=None):
    state_block = (per_step, None, 2, N_HEADS, HEAD_W, HEAD_W)
    if s0 is not None:
        in_specs.append(pl.BlockSpec(state_block, lambda b: (b, l, 0, 0, 0, 0)))
        args.append(s0)
        return (pl.BlockSpec(state_block[:1] + state_block[2:], lambda b: (b, 0, 0, 0, 0)),
                jax.ShapeDtypeStruct((nb, 2, N_HEADS, HEAD_W, HEAD_W), F32), {})
    in_specs.append(pl.BlockSpec(memory_space=pl.ANY))
    args.append(collect)
    return (pl.BlockSpec(state_block, lambda b: (b, l, 0, 0, 0, 0)),
            jax.ShapeDtypeStruct(collect.shape, F32), {len(args) - 1: 1})


def _hgrn_call(u_hg, lb_l, gn, consts, s0, collect, l, T, nb, row0):
    tril, lmask, bd = consts
    sps = max(1, SCAN_STEP_ROWS // T)
    rows = sps * T
    tb = row0 // rows
    has_s0 = s0 is not None
    in_specs = [
        pl.BlockSpec((rows, HG_COLS), lambda b: (tb + b, 0)),
        pl.BlockSpec((None, 2, 1, GROUP_W), lambda b: (l, 0, 0, 0)),
        pl.BlockSpec((None, 1, GROUP_W), lambda b: (l, 0, 0)),
        pl.BlockSpec((2, CHUNK, CHUNK), lambda b: (0, 0, 0)),
        pl.BlockSpec((2, 7, CHUNK, LANE), lambda b: (0, 0, 0, 0)),
        pl.BlockSpec((GROUP_W, GROUP_W), lambda b: (0, 0)),
    ]
    args = [u_hg, lb_l.reshape(DEPTH, 2, 1, GROUP_W), gn.reshape(DEPTH, 1, GROUP_W), tril, lmask, bd]
    state_spec, state_shape, aliases = _state_io(in_specs, args, s0, collect, l, nb, sps)
    seq = pltpu.VMEM((2, rows, GROUP_W), F32)
    return pl.pallas_call(
        functools.partial(_hgrn_kernel, T=T, sps=sps, has_s0=has_s0),
        grid=(nb // sps,),
        in_specs=in_specs,
        out_specs=[pl.BlockSpec((rows, GROUP_W), lambda b: (b, 0)), state_spec],
        out_shape=[jax.ShapeDtypeStruct((nb * T, GROUP_W), F32), state_shape],
        input_output_aliases=aliases,
        scratch_shapes=[seq, seq, seq, pltpu.VMEM((2, rows // CHUNK * 8, GROUP_W), F32),
                        pltpu.VMEM((sps, 2, HEAD_W, GROUP_W), F32)],
        compiler_params=_cparams(("parallel",)),
        name="hgrn",
    )(*args)


def _shift_rows(x, T):
    pos = lax.broadcasted_iota(jnp.int32, x.shape, 0) % T
    prev = jnp.where(pos == 0, 0.0, pltpu.roll(x, 1, 0))
    nxt = jnp.where(pos == T - 1, 0.0, pltpu.roll(x, x.shape[0] - 1, 0))
    return prev, nxt


def _conv3(x, w_ref, T):
    prev, nxt = _shift_rows(x, T)
    return prev * w_ref[0:1, :] + x * w_ref[1:2, :] + nxt * w_ref[2:3, :]


GDN_UNROLL = 2


def _solve_unit_lower(systems):
    c2 = 2 * CHUNK
    slabs = [jnp.concatenate([nmat, nmat, rhs], axis=-1) for rhs, nmat in systems]
    steps = int(math.log2(CHUNK))
    for step in range(steps):
        last = step == steps - 1
        nxt = []
        for slab in slabs:
            hi = slab.astype(BF16)
            lo = (slab - hi.astype(F32)).astype(BF16)
            lhs = jnp.concatenate([hi[:, :c2], lo[:, :CHUNK]], axis=-1)
            first = c2 if last else 0
            rhs3 = jnp.concatenate([hi[:, first:], lo[:, first:], hi[:, first:]], axis=0)
            prod = jnp.dot(lhs, rhs3, preferred_element_type=F32)
            if last:
                nxt.append(slab[:, c2:] + prod)
            else:
                nxt.append(jnp.concatenate([prod[:, :c2], slab[:, c2:] + prod[:, c2:]], axis=-1))
        slabs = nxt
    return slabs


def _gdn_kernel(*refs, T, sps, has_s0):
    if has_s0:
        (u_ref, cw_ref, alog_ref, dtb_ref, exp_ref, tril_ref, mask_ref, bd_ref, gn_ref, s0_ref,
         o_ref, sfin_ref, q_s, k_s, v_s, la_s, be_s, uw_s, ww_s, at_s, qin_s, kt_s, al_s, of_s, st_s) = refs
    else:
        (u_ref, cw_ref, alog_ref, dtb_ref, exp_ref, tril_ref, mask_ref, bd_ref, gn_ref, _,
         o_ref, sfin_ref, q_s, k_s, v_s, la_s, be_s, uw_s, ww_s, at_s, qin_s, kt_s, al_s, of_s, st_s) = refs
    n = T // CHUNK
    C = CHUNK
    bd = bd_ref[...]

    qkv = _silu(_conv3(u_ref[:, 0:3 * GROUP_W], cw_ref, T))
    q = qkv[:, 0:GROUP_W]
    k = qkv[:, GROUP_W:2 * GROUP_W]
    q_s[...] = q * lax.rsqrt(_sel_dot_right(q * q, bd) + 1e-6) * (HEAD_W ** -0.5)
    k_s[...] = k * lax.rsqrt(_sel_dot_right(k * k, bd) + 1e-6)
    v_s[...] = qkv[:, 2 * GROUP_W:3 * GROUP_W]

    ab = u_ref[:, 4 * GROUP_W:4 * GROUP_W + LANE]
    xa = ab + dtb_ref[...]
    softplus = jnp.maximum(xa, 0.0) + jnp.log(1.0 + jnp.exp(-jnp.abs(xa)))
    log_a = -jnp.exp(alog_ref[...]) * softplus
    lane = lax.broadcasted_iota(jnp.int32, ab.shape, 1)
    narrow = jnp.where(lane < 8, log_a, _sigmoid(ab))
    for d in range(2):
        wide = _dot_sel(narrow, exp_ref[d])
        la_s[d] = wide[:, 0:GROUP_W]
        be_s[d] = wide[:, GROUP_W:2 * GROUP_W]
        for s in range(sps):
            if has_s0:
                st_s[s, d] = jnp.concatenate([s0_ref[s, d, h] for h in range(N_HEADS)], axis=-1)
            else:
                st_s[s, d] = jnp.zeros((HEAD_W, GROUP_W), F32)

    def prepare(cidx):
        r0 = pl.multiple_of(cidx * C, C)
        rows = pl.ds(r0, C)
        arow = pl.ds(pl.multiple_of(cidx * 8, 8), 8)
        q = q_s[rows, :]
        k = k_s[rows, :]
        v = v_s[rows, :]
        systems = []
        attns = []
        kts = []
        for d in range(2):
            incl = mask_ref[d, 0] > 0.5
            strict = mask_ref[d, 1]
            la = la_s[d, rows, :]
            be = be_s[d, rows, :]
            gx = _sel_dot(tril_ref[d], la)
            gtot = jnp.sum(la, axis=0, keepdims=True)
            eg = jnp.exp(gx)
            kout = k * jnp.exp(gtot - gx)
            qin_s[d, rows, :] = q * eg
            al_s[d, arow, :] = jnp.broadcast_to(jnp.exp(gtot), (8, GROUP_W))
            kb = k * be
            vb = v * be
            kbg = kb * eg
            for h in range(N_HEADS):
                sl = slice(h * HEAD_W, (h + 1) * HEAD_W)
                gh = gx[:, sl]
                dmat = gh - gh.T
                dec = jnp.where(incl, jnp.exp(jnp.where(incl, dmat, 0.0)), 0.0)
                qk = _bdot_nt(jnp.concatenate([kb[:, sl], q[:, sl]], axis=0), k[:, sl])
                nmat = -(qk[:C] * dec * strict)
                systems.append((jnp.concatenate([vb[:, sl], kbg[:, sl]], axis=-1), nmat))
                attns.append(qk[C:] * dec)
                kts.append(kout[:, sl].T)
        sols = _solve_unit_lower(systems)
        for d in range(2):
            mine = sols[d * N_HEADS:(d + 1) * N_HEADS]
            uw_s[d, rows, :] = jnp.concatenate([x[:, :HEAD_W] for x in mine], axis=-1)
            ww_s[d, rows, :] = jnp.concatenate([x[:, HEAD_W:] for x in mine], axis=-1)
            at_s[d, rows, :] = jnp.concatenate(attns[d * N_HEADS:(d + 1) * N_HEADS], axis=-1)
            kt_s[d, rows, :] = jnp.concatenate(kts[d * N_HEADS:(d + 1) * N_HEADS], axis=-1)

    def prep_body(i, carry):
        for j in range(GDN_UNROLL):
            prepare(i * GDN_UNROLL + j)
        return carry

    lax.fori_loop(0, sps * n // GDN_UNROLL, prep_body, 0)

    def chunk(ci, carry):
        scans = [(s, d) for s in range(sps) for d in range(2)]
        units = [(j, p) for j in range(len(scans)) for p in range(N_PAIRS)]
        rows, alast = [], []
        for s, d in scans:
            cidx = s * n + (ci if d == 0 else n - 1 - ci)
            rows.append(pl.ds(pl.multiple_of(cidx * C, C), C))
            alast.append(al_s[d, pl.ds(pl.multiple_of(cidx * 8, 8), 1), :])
        st = [st_s[s, d] for s, d in scans]
        dirs = [d for s, d in scans]
        lanes = [slice(p * LANE, (p + 1) * LANE) for p in range(N_PAIRS)]
        both = [jnp.dot(jnp.concatenate([ww_s[dirs[j], rows[j], lanes[p]], qin_s[dirs[j], rows[j], lanes[p]]],
                                        axis=0).astype(BF16),
                        _pair_blockdiag(st[j][:, lanes[p]].astype(BF16)), preferred_element_type=F32)
                for j, p in units]
        vnew = [uw_s[dirs[j], rows[j], lanes[p]] - both[i][:C] for i, (j, p) in enumerate(units)]
        upd = [jnp.dot(jnp.concatenate([at_s[dirs[j], rows[j], lanes[p]], kt_s[dirs[j], rows[j], lanes[p]]],
                                       axis=0).astype(BF16),
                       _pair_blockdiag(vnew[i].astype(BF16)), preferred_element_type=F32)
               for i, (j, p) in enumerate(units)]
        for j, (s, d) in enumerate(scans):
            idx = range(j * N_PAIRS, (j + 1) * N_PAIRS)
            of_s[d, rows[j], :] = jnp.concatenate([both[i][C:] + upd[i][:C] for i in idx], axis=-1)
            st_s[s, d] = st[j] * alast[j] + jnp.concatenate([upd[i][C:] for i in idx], axis=-1)
        return carry

    lax.fori_loop(0, n, chunk, 0)
    o_ref[...] = _head_norm_gate(of_s[0] + of_s[1], bd, gn_ref[...], u_ref[:, 3 * GROUP_W:4 * GROUP_W])
    for s in range(sps):
        for d in range(2):
            for h in range(N_HEADS):
                sfin_ref[s, d, h] = st_s[s, d][:, h * HEAD_W:(h + 1) * HEAD_W]


def _gdn_call(u_gd, cw, alog, dtb, gn, consts, s0, collect, l, T, nb, row0):
    tril, masks, expand, bd = consts
    sps = max(1, SCAN_STEP_ROWS // T)
    rows = sps * T
    tb = row0 // rows
    has_s0 = s0 is not None
    in_specs = [
        pl.BlockSpec((rows, GD_PAD), lambda b: (tb + b, 0)),
        pl.BlockSpec((None, 3, 3 * GROUP_W), lambda b: (l, 0, 0)),
        pl.BlockSpec((None, 1, LANE), lambda b: (l, 0, 0)),
        pl.BlockSpec((None, 1, LANE), lambda b: (l, 0, 0)),
        pl.BlockSpec((2, LANE, 2 * GROUP_W), lambda b: (0, 0, 0)),
        pl.BlockSpec((2, CHUNK, CHUNK), lambda b: (0, 0, 0)),
        pl.BlockSpec((2, 2, CHUNK, CHUNK), lambda b: (0, 0, 0, 0)),
        pl.BlockSpec((GROUP_W, GROUP_W), lambda b: (0, 0)),
        pl.BlockSpec((None, 1, GROUP_W), lambda b: (l, 0, 0)),
    ]
    args = [u_gd, cw, alog, dtb, expand, tril, masks, bd, gn]
    state_spec, state_shape, aliases = _state_io(in_specs, args, s0, collect, l, nb, per_step=sps)
    seq = pltpu.VMEM((2, rows, GROUP_W), F32)
    return pl.pallas_call(
        functools.partial(_gdn_kernel, T=T, sps=sps, has_s0=has_s0),
        grid=(nb // sps,),
        in_specs=in_specs,
        out_specs=[pl.BlockSpec((rows, GROUP_W), lambda b: (b, 0)), state_spec],
        out_shape=[jax.ShapeDtypeStruct((nb * T, GROUP_W), F32), state_shape],
        input_output_aliases=aliases,
        scratch_shapes=[pltpu.VMEM((rows, GROUP_W), F32)] * 3 + [seq, seq, seq, seq, seq, seq, seq,
            pltpu.VMEM((2, rows // CHUNK * 8, GROUP_W), F32), seq, pltpu.VMEM((sps, 2, HEAD_W, GROUP_W), F32)],
        compiler_params=_cparams(("parallel",)),
        name="gdn",
    )(*args)


def _hyfilt_kernel(z_ref, win_ref, fh_ref, fl_ref, w1_ref, b1_ref, fr_ref, w2_ref, b2_ref, w3_ref, o_ref, *, T):
    fr = fr_ref[...]
    h = jnp.sin(fr * (_dot3(z_ref[...], w1_ref[...]) + b1_ref[...]))
    h = jnp.sin(fr * (_dot3(h, w2_ref[...]) + b2_ref[...]))
    h = _dot3(h, w3_ref[...])
    win = win_ref[...]
    hf = h[:, 0:GROUP_W] * win
    hb = h[:, GROUP_W:2 * GROUP_W] * win
    row = lax.broadcasted_iota(jnp.int32, hb.shape, 0)
    hb = jnp.where(row == 0, 0.0, hb)
    for rows, taps in ((slice(0, T), (hf + hb).astype(BF16)), (slice(T, 2 * T), (hf - hb).astype(BF16))):
        o_ref[rows, :] = (jnp.dot(fh_ref[rows, :], taps, preferred_element_type=F32)
                          + jnp.dot(fl_ref[rows, :], taps, preferred_element_type=F32))


def _hyfilt_call(T, zp, win, fh, fl, w1p, b1, freq, w2, b2, w3):
    c2 = lambda l: (0, 0)
    return pl.pallas_call(
        functools.partial(_hyfilt_kernel, T=T),
        grid=(DEPTH,),
        in_specs=[
            pl.BlockSpec((T, LANE), c2),
            pl.BlockSpec((T, GROUP_W), c2),
            pl.BlockSpec((2 * T, T), c2),
            pl.BlockSpec((2 * T, T), c2),
            pl.BlockSpec((None, LANE, HY_FH), lambda l: (l, 0, 0)),
            pl.BlockSpec((None, 1, HY_FH), lambda l: (l, 0, 0)),
            pl.BlockSpec((None, 1, HY_FH), lambda l: (l, 0, 0)),
            pl.BlockSpec((None, HY_FH, HY_FH), lambda l: (l, 0, 0)),
            pl.BlockSpec((None, 1, HY_FH), lambda l: (l, 0, 0)),
            pl.BlockSpec((None, HY_FH, 2 * GROUP_W), lambda l: (l, 0, 0)),
        ],
        out_specs=pl.BlockSpec((None, 2 * T, GROUP_W), lambda l: (l, 0, 0)),
        out_shape=jax.ShapeDtypeStruct((DEPTH, 2 * T, GROUP_W), F32),
        compiler_params=_cparams(("parallel",)),
        name="hyfilt",
    )(zp, win, fh, fl, w1p, b1, freq, w2, b2, w3)


def _hyena_kernel(u_ref, cw_ref, cb_ref, spec_ref, skip_ref, fh_ref, fl_ref, ih_ref, il_ref, o_ref, *, T):
    uc = _conv3(u_ref[...], cw_ref, T) + cb_ref[...]
    x0 = uc[:, 0:GROUP_W]
    z = uc[:, GROUP_W:2 * GROUP_W] * uc[:, 2 * GROUP_W:3 * GROUP_W]
    zb = z.astype(BF16)
    zs = (jnp.dot(fh_ref[...], zb, preferred_element_type=F32)
          + jnp.dot(fl_ref[...], zb, preferred_element_type=F32))
    ar, ai = zs[0:T], zs[T:2 * T]
    br, bi = spec_ref[0:T, :], spec_ref[T:2 * T, :]
    pb = jnp.concatenate([ar * br - ai * bi, ar * bi + ai * br], axis=0).astype(BF16)
    y = (jnp.dot(ih_ref[...], pb, preferred_element_type=F32)
         + jnp.dot(il_ref[...], pb, preferred_element_type=F32))
    o_ref[...] = x0 * (y + z * skip_ref[...])


def _hyena_call(u_hy, cw, cb, spec, skip, dft, l, T, nb, row0):
    fh, fl, ih, il = dft
    tb = row0 // T
    c2 = lambda b: (0, 0)
    return pl.pallas_call(
        functools.partial(_hyena_kernel, T=T),
        grid=(nb,),
        in_specs=[
            pl.BlockSpec((T, HY_COLS), lambda b: (tb + b, 0)),
            pl.BlockSpec((None, 3, HY_COLS), lambda b: (l, 0, 0)),
            pl.BlockSpec((None, 1, HY_COLS), lambda b: (l, 0, 0)),
            pl.BlockSpec((None, 2 * T, GROUP_W), lambda b: (l, 0, 0)),
            pl.BlockSpec((None, 1, GROUP_W), lambda b: (l, 0, 0)),
            pl.BlockSpec((2 * T, T), c2),
            pl.BlockSpec((2 * T, T), c2),
            pl.BlockSpec((T, 2 * T), c2),
            pl.BlockSpec((T, 2 * T), c2),
        ],
        out_specs=pl.BlockSpec((T, GROUP_W), lambda b: (b, 0)),
        out_shape=jax.ShapeDtypeStruct((nb * T, GROUP_W), F32),
        compiler_params=_cparams(("parallel",)),
        name="hyena",
    )(u_hy, cw, cb, spec, skip, fh, fl, ih, il)


def _rope(x, cosf, sinf):
    lane = lax.broadcasted_iota(jnp.int32, x.shape, 1)
    half = MLA_ROPE // 2
    partner = jnp.where(lane < MLA_NOPE + half, pltpu.roll(x, LANE - half, 1), pltpu.roll(x, half, 1))
    return x * cosf + partner * sinf


def _qk_norm(x, g):
    ms = jnp.sum(x * x, axis=-1, keepdims=True) * (1.0 / MLA_QK)
    return x * lax.rsqrt(ms + RMS_EPS) * g


def _mla_kernel(*refs, T, ctx):
    if ctx:
        (u_ref, qn_ref, wq_ref, kvn_ref, wkv_ref, qkn_ref, cos_ref, sin_ref, cckv_ref, ckr_ref, o_ref,
         q_s, k_s, v_s) = refs
    else:
        (u_ref, qn_ref, wq_ref, kvn_ref, wkv_ref, qkn_ref, _, _, o_ref, ckv_ref, kr_ref, q_s, k_s, v_s) = refs
    n_keys = k_s.shape[1]
    u = u_ref[...]
    cq = _rms(u[:, 0:MLA_Q_LORA], qn_ref[...])
    ckv = _rms(u[:, MLA_Q_LORA:MLA_Q_LORA + MLA_KV_LORA], kvn_ref[...])
    kr = u[:, MLA_Q_LORA + MLA_KV_LORA:MLA_Q_LORA + MLA_KV_LORA + MLA_ROPE]
    if not ctx:
        ckv_ref[...] = ckv
        kr_ref[...] = kr
    q_all = _bdot(cq, wq_ref[...])
    kv = _bdot(ckv, wkv_ref[...])
    gq = qkn_ref[0:1, :]
    gk = qkn_ref[1:2, :]
    if ctx:
        kvc = _bdot(cckv_ref[...], wkv_ref[...])
        krc = ckr_ref[...]
        cosf, sinf = cos_ref[...], sin_ref[...]
    q_scale = MLA_QK ** -0.5 * math.log2(math.e)
    kr_tile = jnp.concatenate([jnp.zeros((T, MLA_NOPE), F32), kr, jnp.zeros((T, LANE - MLA_QK), F32)], axis=-1)
    kr_rot = kr_tile * gk
    if ctx:
        kr_rot = _rope(kr_rot, cosf, sinf)
    nope_lane = lax.broadcasted_iota(jnp.int32, (T, LANE), 1) < MLA_NOPE
    for h in range(N_HEADS):
        qh = _qk_norm(q_all[:, h * LANE:(h + 1) * LANE], gq)
        k_nope = jnp.concatenate([kv[:, h * HEAD_W:(h + 1) * HEAD_W], jnp.zeros((T, LANE - MLA_NOPE), F32)], axis=-1)
        ms = jnp.sum(k_nope * k_nope + kr_tile * kr_tile, axis=-1, keepdims=True) * (1.0 / MLA_QK)
        kh = jnp.where(nope_lane, k_nope * gk, kr_rot) * lax.rsqrt(ms + RMS_EPS)
        if ctx:
            qh = _rope(qh, cosf, sinf)
            zc = jnp.zeros((n_keys - T, LANE - MLA_QK), F32)
            kc = _qk_norm(jnp.concatenate([kvc[:, h * HEAD_W:(h + 1) * HEAD_W], krc, zc], axis=-1), gk)
            k_s[h, T:n_keys, :] = kc.astype(BF16)
        q_s[:, h * LANE:(h + 1) * LANE] = (qh * q_scale).astype(BF16)
        k_s[h, 0:T, :] = kh.astype(BF16)
    for p in range(N_PAIRS):
        lanes = slice(GROUP_W + p * LANE, GROUP_W + (p + 1) * LANE)
        vp = kv[:, lanes]
        if ctx:
            vp = jnp.concatenate([vp, kvc[:, lanes]], axis=0)
        v_s[p] = _pair_blockdiag(vp.astype(BF16))

    def q_block(qb, carry):
        rows = pl.ds(pl.multiple_of(qb * ATT_QBLOCK, ATT_QBLOCK), ATT_QBLOCK)
        lane = lax.broadcasted_iota(jnp.int32, (ATT_QBLOCK, LANE), 1)
        for p in range(N_PAIRS):
            es, sums = [], []
            for h in (2 * p, 2 * p + 1):
                s = lax.dot_general(q_s[rows, h * LANE:(h + 1) * LANE], k_s[h], (((1,), (1,)), ((), ())),
                                    preferred_element_type=F32)
                e = jnp.exp2(s - jnp.max(s, axis=-1, keepdims=True))
                sums.append(jnp.sum(e, axis=-1, keepdims=True))
                es.append(e.astype(BF16))
            o = jnp.dot(jnp.concatenate(es, axis=-1), v_s[p], preferred_element_type=F32)
            o_ref[rows, p * LANE:(p + 1) * LANE] = o / jnp.where(lane < HEAD_W, sums[0], sums[1])
        return carry

    lax.fori_loop(0, T // ATT_QBLOCK, q_block, 0)


def _mla_call(u_mla, qn, wq, kvn, wkv, qkn, rope, cache, collect, l, T, nb, row0):
    tb = row0 // T
    ctx = cache is not None
    n_keys = T + (PAST_LEN if ctx else 0)
    c2 = lambda b: (0, 0)
    in_specs = [
        pl.BlockSpec((T, MLA_PAD), lambda b: (tb + b, 0)),
        pl.BlockSpec((None, 1, MLA_Q_LORA), lambda b: (l, 0, 0)),
        pl.BlockSpec((None, MLA_Q_LORA, N_HEADS * LANE), lambda b: (l, 0, 0)),
        pl.BlockSpec((None, 1, MLA_KV_LORA), lambda b: (l, 0, 0)),
        pl.BlockSpec((None, MLA_KV_LORA, 2 * GROUP_W), lambda b: (l, 0, 0)),
        pl.BlockSpec((None, 2, LANE), lambda b: (l, 0, 0)),
    ]
    args = [u_mla, qn, wq, kvn, wkv, qkn]
    out_specs = [pl.BlockSpec((T, GROUP_W), lambda b: (b, 0))]
    out_shape = [jax.ShapeDtypeStruct((nb * T, GROUP_W), F32)]
    if ctx:
        in_specs += [
            pl.BlockSpec((T, LANE), c2),
            pl.BlockSpec((T, LANE), c2),
            pl.BlockSpec((None, None, PAST_LEN, MLA_KV_LORA), lambda b: (b, l, 0, 0)),
            pl.BlockSpec((None, None, PAST_LEN, MLA_ROPE), lambda b: (b, l, 0, 0)),
        ]
        args += [rope[0], rope[1], cache[0], cache[1]]
        aliases = {}
    else:
        in_specs += [pl.BlockSpec(memory_space=pl.ANY)] * 2
        aliases = {len(args): 1, len(args) + 1: 2}
        args += list(collect)
        out_specs += [pl.BlockSpec((None, None, T, MLA_KV_LORA), lambda b: (b, l, 0, 0)),
                      pl.BlockSpec((None, None, T, MLA_ROPE), lambda b: (b, l, 0, 0))]
        out_shape += [jax.ShapeDtypeStruct(a.shape, F32) for a in collect]
    return pl.pallas_call(
        functools.partial(_mla_kernel, T=T, ctx=ctx),
        grid=(nb,),
        in_specs=in_specs,
        out_specs=out_specs,
        out_shape=out_shape,
        input_output_aliases=aliases,
        scratch_shapes=[pltpu.VMEM((T, N_HEADS * LANE), BF16), pltpu.VMEM((N_HEADS, n_keys, LANE), BF16),
                        pltpu.VMEM((N_PAIRS, 2 * n_keys, LANE), BF16)],
        compiler_params=_cparams(("parallel",)),
        name="mla",
    )(*args)


def _pad_cols(w, width):
    return jnp.pad(w, [(0, 0)] * (w.ndim - 1) + [(0, width - w.shape[-1])])


W_IN_PREP_COLS = 256


def _prep_w_in_kernel(wt_ref, o_hg, o_hy, o_mla, o_gd):
    start = 0
    for o_ref, cols in ((o_hg, HG_COLS), (o_hy, HY_COLS), (o_mla, MLA_COLS), (o_gd, GD_COLS)):
        width = o_ref.shape[-1]
        for c0 in range(0, width, W_IN_PREP_COLS):
            n_out = min(W_IN_PREP_COLS, width - c0)
            n_real = max(0, min(n_out, cols - c0))
            piece = wt_ref[start + c0:start + c0 + n_real, :]
            if n_real < n_out:
                piece = jnp.concatenate([piece, jnp.zeros((n_out - n_real, D_MODEL), F32)], axis=0)
            o_ref[:, c0:c0 + n_out] = piece.T.astype(BF16)
        start += cols


def _prep_w_in(w_in):
    widths = (HG_COLS, HY_COLS, MLA_PAD, GD_PAD)
    n_cols = w_in.shape[-1]
    return pl.pallas_call(
        _prep_w_in_kernel,
        grid=(DEPTH,),
        in_specs=[pl.BlockSpec((None, n_cols, D_MODEL), lambda l: (l, 0, 0))],
        out_specs=[pl.BlockSpec((None, D_MODEL, w), lambda l: (l, 0, 0)) for w in widths],
        out_shape=[jax.ShapeDtypeStruct((DEPTH, D_MODEL, w), BF16) for w in widths],
        compiler_params=_cparams(("parallel",)),
        name="w_in_prep",
    )(jnp.swapaxes(w_in, 1, 2))


def _prep_wq(w_q_up):
    w = w_q_up.reshape(DEPTH, MLA_Q_LORA, N_HEADS, MLA_QK)
    return _pad_cols(w, LANE).reshape(DEPTH, MLA_Q_LORA, N_HEADS * LANE).astype(BF16)


def _prep_wkv(w_kv_up):
    w = w_kv_up.reshape(DEPTH, MLA_KV_LORA, N_HEADS, 2, HEAD_W)
    return w.transpose(0, 1, 3, 2, 4).reshape(DEPTH, MLA_KV_LORA, 2 * GROUP_W).astype(BF16)


def _lower_bounds(hgrn_lb):
    lb = jnp.cumsum(jax.nn.softmax(hgrn_lb.astype(F32), axis=0), axis=0)
    return lb - lb[0]


def kernel(x_prompt, x_sample, cache_mla_ckv, cache_mla_krope, state_hgrn, state_gdn, c, c_ctx, w_ada, b_ada, norm_ffn, w_ffn_gu, w_ffn_down, norm_mix, w_in, w_out, hgrn_lb, hgrn_norm, hy_conv_w, hy_conv_b, hy_w1, hy_b1, hy_freq, hy_w2, hy_b2, hy_w3, hy_skip, mla_q_norm_a, mla_w_q_up, mla_kv_norm_a, mla_w_kv_up, mla_qk_norm, gdn_conv_w, gdn_a_log, gdn_dt_bias, gdn_norm):
    x = (x_prompt.reshape(N_PROMPT, D_MODEL), x_sample.reshape(N_SAMPLE, D_MODEL))

    cond8 = jnp.zeros((8, D_MODEL), F32).at[0].set(c_ctx).at[1:1 + DEC_BATCH].set(c)
    ada = _ada_call(cond8, w_ada, b_ada)

    w_in_parts = _prep_w_in(w_in)
    w_out_bf = w_out.astype(BF16)
    wq = _prep_wq(mla_w_q_up)
    wkv = _prep_wkv(mla_w_kv_up)
    qkn = _pad_cols(mla_qk_norm, LANE)
    lb_all = _lower_bounds(hgrn_lb)
    alog = _pad_cols(gdn_a_log.reshape(DEPTH, 1, 8), LANE)
    dtb = _pad_cols(gdn_dt_bias.reshape(DEPTH, 1, 8), LANE)
    gdn_gn = jnp.tile(gdn_norm, (1, N_HEADS)).reshape(DEPTH, 1, GROUP_W)
    w1p = jnp.pad(hy_w1, ((0, 0), (0, LANE - HY_EMB), (0, 0)))

    bd = jnp.asarray(_block_diag_ones(), BF16)
    hg_tril, hg_m = _hgrn_consts()
    hg_consts = (jnp.asarray(hg_tril, BF16), jnp.asarray(hg_m, F32), bd)
    gd_tril, gd_masks, gd_expand = _gdn_consts()
    gd_consts = (jnp.asarray(gd_tril, BF16), jnp.asarray(gd_masks, F32), jnp.asarray(gd_expand, BF16), bd)
    rope = tuple(jnp.asarray(a) for a in _rope_consts(DEC_SEQ))
    groups = ((SEQ, BATCH, 0), (DEC_SEQ, DEC_BATCH, N_PROMPT))
    dft = {}
    spec = {}
    for T, _, _ in groups:
        fwd, inv = _dft_consts(T)
        fh, fl = _np_split2(fwd)
        ih, il = _np_split2(inv)
        dft[T] = (fh, fl, ih, il)
        zp, win = _hyena_pos_consts(T)
        spec[T] = _hyfilt_call(T, jnp.asarray(zp), jnp.asarray(win), fh, fl, w1p,
                               hy_b1.reshape(DEPTH, 1, HY_FH), hy_freq.reshape(DEPTH, 1, HY_FH), hy_w2,
                               hy_b2.reshape(DEPTH, 1, HY_FH), hy_w3)

    new_ckv = jnp.zeros((BATCH, DEPTH, SEQ, MLA_KV_LORA), F32)
    new_kr = jnp.zeros((BATCH, DEPTH, SEQ, MLA_ROPE), F32)
    new_hg = jnp.zeros((BATCH, DEPTH, 2, N_HEADS, HEAD_W, HEAD_W), F32)
    new_gd = jnp.zeros((BATCH, DEPTH, 2, N_HEADS, HEAD_W, HEAD_W), F32)
    for l in range(DEPTH):
        x = _ffn_call(x, ada, norm_ffn, w_ffn_gu, w_ffn_down, l, 0)
        u_hg, u_hy, u_mla, u_gd = _inproj_call(x, ada, norm_mix, w_in_parts, l)
        outs = []
        for gi, (T, nb, row0) in enumerate(groups):
            latent = gi == 1
            o_hg, s_hg = _hgrn_call(u_hg, lb_all, hgrn_norm, hg_consts, state_hgrn if latent else None,
                                    None if latent else new_hg, l, T, nb, row0)
            o_hy = _hyena_call(u_hy, hy_conv_w, hy_conv_b.reshape(DEPTH, 1, HY_COLS), spec[T],
                               hy_skip.reshape(DEPTH, 1, GROUP_W), dft[T], l, T, nb, row0)
            mla = _mla_call(u_mla, mla_q_norm_a.reshape(DEPTH, 1, MLA_Q_LORA), wq,
                            mla_kv_norm_a.reshape(DEPTH, 1, MLA_KV_LORA), wkv, qkn,
                            rope if latent else None,
                            (cache_mla_ckv, cache_mla_krope) if latent else None,
                            None if latent else (new_ckv, new_kr), l, T, nb, row0)
            o_gd, s_gd = _gdn_call(u_gd, gdn_conv_w, alog, dtb, gdn_gn, gd_consts, state_gdn if latent else None,
                                   None if latent else new_gd, l, T, nb, row0)
            outs.append((o_hg, o_hy, mla[0], o_gd))
            if not latent:
                new_ckv, new_kr, new_hg, new_gd = mla[1], mla[2], s_hg, s_gd
        x = _outproj_call(x, ada, w_out_bf, outs[0], outs[1], l)
        x = _ffn_call(x, ada, norm_ffn, w_ffn_gu, w_ffn_down, l, 1)

    y_prompt = x[:N_PROMPT].reshape(BATCH, SEQ, D_MODEL)
    y_sample = x[N_PROMPT:].reshape(DEC_BATCH, DEC_SEQ, D_MODEL)
    return (y_prompt, y_sample, new_ckv, new_kr, new_hg, new_gd)
```
